```python
import math
import jax
import jax.numpy as jnp
from jax import lax
import numpy as np

D_MODEL = 1024
BATCH = 8
SEQ = 2048
DEPTH = 2

CHUNK = 64
N_BRANCH = 4
BRANCH_DIM = D_MODEL // 2
NORM_EPS = 1e-6

RWKV_HEAD_DIM = 64
RWKV_HEADS = BRANCH_DIM // RWKV_HEAD_DIM
RWKV_DIM = RWKV_HEADS * RWKV_HEAD_DIM
RWKV_DECAY_RANK = 64
RWKV_A_RANK = 64
RWKV_V_RANK = 32
RWKV_GATE_RANK = 160
RWKV_LN_EPS = 64e-5

SSM_HEAD_DIM = 64
SSM_HEADS = BRANCH_DIM // SSM_HEAD_DIM
SSM_DIM = SSM_HEADS * SSM_HEAD_DIM
SSM_GROUPS = 2
SSM_STATE = 128
SSM_CONV = 4
SSM_XBC = SSM_DIM + 2 * SSM_GROUPS * SSM_STATE

ATT_HEAD_DIM = 64
ATT_HEADS = BRANCH_DIM // ATT_HEAD_DIM
ATT_DIM = ATT_HEADS * ATT_HEAD_DIM
ATT_LEFT_CHUNKS = 8
ATT_BAND = (ATT_LEFT_CHUNKS + 1) * CHUNK
REL_CLIP = 2 * CHUNK

GLA_HEADS = 4
GLA_KEY_DIM = BRANCH_DIM // 2
GLA_VAL_DIM = BRANCH_DIM
GLA_GATE_RANK = 16
GLA_GATE_NORM = 16.0

RWKV_COLS = 3 * RWKV_DIM + RWKV_DECAY_RANK + RWKV_A_RANK + RWKV_GATE_RANK
SSM_COLS = SSM_DIM + SSM_XBC + SSM_HEADS
ATT_COLS = 3 * ATT_DIM
GLA_COLS = 2 * GLA_KEY_DIM + 2 * GLA_VAL_DIM + GLA_GATE_RANK
GATE_COLS = N_BRANCH * D_MODEL
IN_COLS = RWKV_COLS + SSM_COLS + ATT_COLS + GLA_COLS + GATE_COLS

FFN_DIM = 2816
N_EXPERTS = 8
TOP_K = 2
EXPERT_DIM = 3584
MOE_ROWS = 256
N_DENSE = (DEPTH + 1) // 2
N_MOE = DEPTH // 2
N_VRES = DEPTH - 1

kernel_name = 'hybrid_streaming_block'


def _split(z, sizes):
    return jnp.split(z, np.cumsum(sizes)[:-1].tolist(), axis=-1)


def _rms(x, g, eps=NORM_EPS):
    xf = x.astype(jnp.float32)
    y = xf * lax.rsqrt(jnp.mean(xf * xf, axis=-1, keepdims=True) + eps)
    return (y * g.astype(jnp.float32)).astype(x.dtype)


def _shift(z):
    return jnp.pad(z, ((0, 0), (1, 0), (0, 0)))[:, :-1]


def _chunks(t, *tail):
    return t.reshape(t.shape[0], t.shape[1] // CHUNK, CHUNK, *tail)


def _chunk_scan(contrib, decay):
    def step(state, inp):
        inc, dec = inp
        return state * dec + inc, state
    init = jnp.zeros_like(contrib[:, 0])
    _, entering = lax.scan(step, init, (jnp.moveaxis(contrib, 1, 0), jnp.moveaxis(decay, 1, 0)))
    return jnp.moveaxis(entering, 0, 1)


def _causal_dwconv(x, w, b):
    width = w.shape[0]
    y = lax.conv_general_dilated(x, w[:, None, :].astype(x.dtype), window_strides=(1,),
                                 padding=[(width - 1, 0)], dimension_numbers=('NWC', 'WIO', 'NWC'),
                                 feature_group_count=x.shape[-1])
    return y + b


def _rwkv7_scan(r, w, k, v, a, b):
    bsz, _, heads, n = r.shape
    seq_in = tuple(jnp.moveaxis(t.astype(jnp.float32), 1, 0) for t in (r, w, k, v, a, b))

    def step(state, inp):
        rt, wt, kt, vt, at, bt = inp
        sa = jnp.einsum('bhij,bhj->bhi', state, at)
        state = (state * wt[:, :, None, :] + vt[..., None] * kt[:, :, None, :]
                 + sa[..., None] * bt[:, :, None, :])
        return state, jnp.einsum('bhij,bhj->bhi', state, rt)

    s0 = jnp.zeros((bsz, heads, n, n), jnp.float32)
    _, y = lax.scan(step, s0, seq_in)
    return jnp.moveaxis(y, 0, 1)


def _rwkv7(za, mu, decay_up, w0, a_up, a0, gate_up, k_k, k_a, r_k, ln_w, ln_b, v_first, v_mix):
    bsz, seq, _ = za.shape
    f32 = jnp.float32
    za = za + (_shift(za) - za) * mu
    r, k, v, xw, xa, xg = _split(za, [RWKV_DIM, RWKV_DIM, RWKV_DIM, RWKV_DECAY_RANK, RWKV_A_RANK, RWKV_GATE_RANK])
    w_log = -jax.nn.softplus(-(w0 + jnp.tanh(xw) @ decay_up).astype(f32)) - 0.5
    decay = jnp.exp(-jnp.exp(w_log))
    a = jax.nn.sigmoid(a0 + xa @ a_up)
    g = jax.nn.sigmoid(xg) @ gate_up
    v_raw = v
    if v_mix is not None:
        v = v + (v_first - v) * v_mix
    kk = k * k_k
    k = k * (1.0 + (a - 1.0) * k_a)

    def heads(t):
        return t.astype(f32).reshape(bsz, seq, RWKV_HEADS, RWKV_HEAD_DIM)

    r, k, v, a, kk, decay = (heads(t) for t in (r, k, v, a, kk, decay))
    kk = kk / jnp.maximum(jnp.sqrt(jnp.sum(kk * kk, axis=-1, keepdims=True)), 1e-12)
    y = _rwkv7_scan(r, decay, k, v, -kk, kk * a)
    mean = jnp.mean(y, axis=-1, keepdims=True)
    var = jnp.mean(jnp.square(y - mean), axis=-1, keepdims=True)
    y = (y - mean) * lax.rsqrt(var + RWKV_LN_EPS)
    y = (y * ln_w.astype(f32).reshape(RWKV_HEADS, RWKV_HEAD_DIM)
         + ln_b.astype(f32).reshape(RWKV_HEADS, RWKV_HEAD_DIM))
    y = y + jnp.sum(r * k * r_k.astype(f32), axis=-1, keepdims=True) * v
    out = y.reshape(bsz, seq, RWKV_DIM) * g.astype(f32)
    return out.astype(za.dtype), v_raw


def _mamba2(zb, conv_w, conv_b, dt_bias, a_log, d_skip, norm_w):
    bsz, seq, _ = zb.shape
    f32 = jnp.float32
    e = SSM_HEADS // SSM_GROUPS
    gate, xbc, dt = _split(zb, [SSM_DIM, SSM_XBC, SSM_HEADS])
    xbc = jax.nn.silu(_causal_dwconv(xbc, conv_w, conv_b))
    xs, bm, cm = _split(xbc, [SSM_DIM, SSM_GROUPS * SSM_STATE, SSM_GROUPS * SSM_STATE])
    xs = _chunks(xs.astype(f32), SSM_GROUPS, e, SSM_HEAD_DIM)
    bm = _chunks(bm.astype(f32), SSM_GROUPS, SSM_STATE)
    cm = _chunks(cm.astype(f32), SSM_GROUPS, SSM_STATE)
    dt = _chunks(jax.nn.softplus(dt.astype(f32) + dt_bias.astype(f32)), SSM_GROUPS, e)
    a = -jnp.exp(a_log.astype(f32)).reshape(SSM_GROUPS, e)
    acs = jnp.cumsum(dt * a, axis=2)
    xdt = xs * dt[..., None]
    causal = jnp.tril(jnp.ones((CHUNK, CHUNK), bool))
    seg = acs[:, :, :, None] - acs[:, :, None, :]
    lmat = jnp.exp(jnp.where(causal[:, :, None, None], seg, -jnp.inf))
    cb = jnp.einsum('bclgn,bcsgn->bclsg', cm, bm)
    y = jnp.einsum('bclsg,bclsge,bcsgep->bclgep', cb, lmat, xdt)
    decay_to_end = jnp.exp(acs[:, :, -1:] - acs)
    states = jnp.einsum('bclgn,bclge,bclgep->bcgepn', bm, decay_to_end, xdt)
    entering = _chunk_scan(states, jnp.exp(acs[:, :, -1])[..., None, None])
    y = y + jnp.einsum('bclgn,bcgepn,bclge->bclgep', cm, entering, jnp.exp(acs))
    y = y + xs * d_skip.astype(f32).reshape(SSM_GROUPS, e, 1)
    y = y.reshape(bsz, seq, SSM_DIM) * jax.nn.silu(gate.astype(f32))
    y = _rms(y.reshape(bsz, seq, SSM_GROUPS, SSM_DIM // SSM_GROUPS), norm_w.reshape(SSM_GROUPS, -1))
    return y.reshape(bsz, seq, SSM_DIM).astype(zb.dtype)


def _band_attention(zc, q_gain, k_gain, rel_bias):
    bsz, seq, _ = zc.shape
    nc = seq // CHUNK
    q, k, v = _split(zc, [ATT_DIM, ATT_DIM, ATT_DIM])
    q = _chunks(_rms(q.reshape(bsz, seq, ATT_HEADS, ATT_HEAD_DIM), q_gain), ATT_HEADS, ATT_HEAD_DIM)
    k = _chunks(_rms(k.reshape(bsz, seq, ATT_HEADS, ATT_HEAD_DIM), k_gain), ATT_HEADS, ATT_HEAD_DIM)
    v = _chunks(v, ATT_HEADS, ATT_HEAD_DIM)
    pad = ((0, 0), (ATT_LEFT_CHUNKS, 0), (0, 0), (0, 0), (0, 0))
    kp = jnp.pad(k, pad)
    vp = jnp.pad(v, pad)
    k_band = jnp.concatenate([kp[:, i:i + nc] for i in range(ATT_LEFT_CHUNKS + 1)], axis=2)
    v_band = jnp.concatenate([vp[:, i:i + nc] for i in range(ATT_LEFT_CHUNKS + 1)], axis=2)
    scores = jnp.einsum('bclhd,bckhd->bhclk', q, k_band,
                        preferred_element_type=jnp.float32) * (ATT_HEAD_DIM ** -0.5)
    q_pos = ATT_LEFT_CHUNKS * CHUNK + jnp.arange(CHUNK)
    k_pos = jnp.arange(ATT_BAND)
    rel = jnp.clip(k_pos[None, :] - q_pos[:, None], -REL_CLIP, REL_CLIP) + REL_CLIP
    bias = rel_bias.astype(jnp.float32)[:, rel]
    key_chunk = jnp.arange(nc)[:, None] - ATT_LEFT_CHUNKS + k_pos[None, :] // CHUNK
    scores = jnp.where((key_chunk >= 0)[None, None, :, None, :], scores + bias[:, None], -jnp.inf)
    probs = jax.nn.softmax(scores, axis=-1)
    o = jnp.einsum('bhclk,bckhd->bclhd', probs.astype(v.dtype), v_band)
    return o.reshape(bsz, seq, ATT_DIM)


def _gla(zd, gate_up, gate_bias, norm_w):
    bsz, seq, _ = zd.shape
    f32 = jnp.float32
    dk = GLA_KEY_DIM // GLA_HEADS
    dv = GLA_VAL_DIM // GLA_HEADS
    q, k, v, xgk, g = _split(zd, [GLA_KEY_DIM, GLA_KEY_DIM, GLA_VAL_DIM, GLA_GATE_RANK, GLA_VAL_DIM])
    log_a = jax.nn.log_sigmoid((xgk @ gate_up + gate_bias).astype(f32)) / GLA_GATE_NORM
    q = _chunks(q.astype(f32) * (dk ** -0.5), GLA_HEADS, dk)
    k = _chunks(k.astype(f32), GLA_HEADS, dk)
    v = _chunks(v.astype(f32), GLA_HEADS, dv)
    bcum = jnp.cumsum(_chunks(log_a, GLA_HEADS, dk), axis=2)
    blast = bcum[:, :, -1:]
    qg = q * jnp.exp(bcum)
    kg = k * jnp.exp(-bcum)
    kd = k * jnp.exp(blast - bcum)
    causal = jnp.tril(jnp.ones((CHUNK, CHUNK), bool))
    att = jnp.where(causal, jnp.einsum('bclhk,bcshk->bchls', qg, kg), 0.0)
    o = jnp.einsum('bchls,bcshv->bclhv', att, v)
    contrib = jnp.einsum('bclhk,bclhv->bchkv', kd, v)
    entering = _chunk_scan(contrib, jnp.exp(blast[:, :, 0])[..., None])
    o = o + jnp.einsum('bclhk,bchkv->bclhv', qg, entering)
    o = _rms(o.reshape(bsz, seq, GLA_HEADS, dv), norm_w)
    o = o.reshape(bsz, seq, GLA_VAL_DIM) * jax.nn.silu(g.astype(f32))
    return o.astype(zd.dtype)


def _swiglu(x, w1, w3, w2):
    return (jax.nn.silu(x @ w1) * (x @ w3)) @ w2


def _moe(x, w_router, w1, w3, w2):
    bsz, seq, d = x.shape
    n_tok = bsz * seq
    n_pair = n_tok * TOP_K
    xf = x.reshape(n_tok, d)
    logits = (xf @ w_router).astype(jnp.float32)
    top_val, top_idx = lax.top_k(logits, TOP_K)
    gate = jax.nn.softmax(top_val, axis=-1)
    e_pair = top_idx.reshape(-1)
    t_pair = jnp.arange(n_pair) // TOP_K
    g_pair = gate.reshape(-1)
    order = jnp.argsort(e_pair)
    e_sorted = e_pair[order]
    counts = jnp.bincount(e_pair, length=N_EXPERTS)
    padded = (counts + MOE_ROWS - 1) // MOE_ROWS * MOE_ROWS
    start_sorted = jnp.cumsum(counts) - counts
    end_padded = jnp.cumsum(padded)
    start_padded = end_padded - padded
    dest = start_padded[e_sorted] + jnp.arange(n_pair) - start_sorted[e_sorted]
    n_groups = (n_pair + MOE_ROWS - 1) // MOE_ROWS + N_EXPERTS
    n_rows = n_groups * MOE_ROWS
    row_token = jnp.full((n_rows,), n_tok, jnp.int32).at[dest].set(t_pair[order].astype(jnp.int32))
    row_gate = jnp.zeros((n_rows,), jnp.float32).at[dest].set(g_pair[order])
    group_expert = jnp.minimum(jnp.searchsorted(end_padded, jnp.arange(n_groups) * MOE_ROWS, side='right'),
                               N_EXPERTS - 1)
    x_pad = jnp.concatenate([xf, jnp.zeros((1, d), xf.dtype)], axis=0)
    xg = x_pad[row_token].reshape(n_groups, MOE_ROWS, d)

    def expert_rows(args):
        rows, e = args
        return _swiglu(rows, w1[e], w3[e], w2[e])

    yg = lax.map(expert_rows, (xg, group_expert)).reshape(n_rows, d)
    out = jnp.zeros((n_tok + 1, d), jnp.float32).at[row_token].add(yg.astype(jnp.float32) * row_gate[:, None])
    return out[:n_tok].reshape(bsz, seq, d).astype(x.dtype)


def setup_inputs(seed: int = 0) -> dict:
    key = jax.random.key(seed)
    ks = iter(jax.random.split(key, 64))
    f32 = jnp.float32

    def nrm(shape, scale):
        return jax.random.normal(next(ks), shape, f32) * scale

    def uni(shape, lo, hi):
        return jax.random.uniform(next(ks), shape, f32, lo, hi)

    L = DEPTH
    dt0 = jnp.exp(uni((L, SSM_HEADS), math.log(1e-3), math.log(1e-1)))
    return {
        'x': nrm((BATCH, SEQ, D_MODEL), 1.0),
        'w_in': nrm((L, D_MODEL, IN_COLS), D_MODEL ** -0.5),
        'norm_mix': 1.0 + nrm((L, D_MODEL), 0.02),
        'rwkv_mu': uni((L, RWKV_COLS), 0.0, 1.0),
        'rwkv_decay_up': nrm((L, RWKV_DECAY_RANK, RWKV_DIM), RWKV_DECAY_RANK ** -0.5),
        'rwkv_w0': uni((L, RWKV_DIM), -6.0, 0.0),
        'rwkv_a_up': nrm((L, RWKV_A_RANK, RWKV_DIM), RWKV_A_RANK ** -0.5),
        'rwkv_a0': nrm((L, RWKV_DIM), 0.1),
        'rwkv_gate_up': nrm((L, RWKV_GATE_RANK, RWKV_DIM), RWKV_GATE_RANK ** -0.5),
        'rwkv_k_k': 0.85 + nrm((L, RWKV_DIM), 0.02),
        'rwkv_k_a': 1.0 + nrm((L, RWKV_DIM), 0.02),
        'rwkv_r_k': nrm((L, RWKV_HEADS, RWKV_HEAD_DIM), 0.1),
        'rwkv_ln_w': 1.0 + nrm((L, RWKV_DIM), 0.02),
        'rwkv_ln_b': nrm((L, RWKV_DIM), 0.02),
        'vres_down': nrm((N_VRES, D_MODEL, RWKV_V_RANK), D_MODEL ** -0.5),
        'vres_up': nrm((N_VRES, RWKV_V_RANK, RWKV_DIM), RWKV_V_RANK ** -0.5),
        'vres_v0': 1.0 + nrm((N_VRES, RWKV_DIM), 0.1),
        'ssm_conv_w': nrm((L, SSM_CONV, SSM_XBC), SSM_CONV ** -0.5),
        'ssm_conv_b': nrm((L, SSM_XBC), 0.02),
        'ssm_dt_bias': dt0 + jnp.log(-jnp.expm1(-dt0)),
        'ssm_a_log': jnp.log(uni((L, SSM_HEADS), 1.0, 16.0)),
        'ssm_d': 1.0 + nrm((L, SSM_HEADS), 0.1),
        'ssm_norm_w': 1.0 + nrm((L, SSM_DIM), 0.02),
        'att_q_gain': 1.0 + nrm((L, ATT_HEAD_DIM), 0.02),
        'att_k_gain': 1.0 + nrm((L, ATT_HEAD_DIM), 0.02),
        'att_rel_bias': nrm((ATT_HEADS, 2 * REL_CLIP + 1), 0.2),
        'gla_gate_up': nrm((L, GLA_GATE_RANK, GLA_KEY_DIM), GLA_GATE_RANK ** -0.5),
        'gla_gate_bias': nrm((L, GLA_KEY_DIM), 0.1),
        'gla_norm_w': 1.0 + nrm((L, GLA_VAL_DIM // GLA_HEADS), 0.02),
        'w_branch': nrm((L, N_BRANCH, BRANCH_DIM, D_MODEL), BRANCH_DIM ** -0.5),
        'w_out': nrm((L, D_MODEL, D_MODEL), D_MODEL ** -0.5),
        'norm_ffn': 1.0 + nrm((L, D_MODEL), 0.02),
        'ffn_w1': nrm((N_DENSE, D_MODEL, FFN_DIM), D_MODEL ** -0.5),
        'ffn_w3': nrm((N_DENSE, D_MODEL, FFN_DIM), D_MODEL ** -0.5),
        'ffn_w2': nrm((N_DENSE, FFN_DIM, D_MODEL), FFN_DIM ** -0.5),
        'moe_router': nrm((N_MOE, D_MODEL, N_EXPERTS), D_MODEL ** -0.5),
        'moe_w1': nrm((N_MOE, N_EXPERTS, D_MODEL, EXPERT_DIM), D_MODEL ** -0.5),
        'moe_w3': nrm((N_MOE, N_EXPERTS, D_MODEL, EXPERT_DIM), D_MODEL ** -0.5),
        'moe_w2': nrm((N_MOE, N_EXPERTS, EXPERT_DIM, D_MODEL), EXPERT_DIM ** -0.5),
    }


def reference(x, w_in, norm_mix, rwkv_mu, rwkv_decay_up, rwkv_w0, rwkv_a_up, rwkv_a0, rwkv_gate_up,
              rwkv_k_k, rwkv_k_a, rwkv_r_k, rwkv_ln_w, rwkv_ln_b, vres_down, vres_up, vres_v0,
              ssm_conv_w, ssm_conv_b, ssm_dt_bias, ssm_a_log, ssm_d, ssm_norm_w,
              att_q_gain, att_k_gain, att_rel_bias, gla_gate_up, gla_gate_bias, gla_norm_w,
              w_branch, w_out, norm_ffn, ffn_w1, ffn_w3, ffn_w2, moe_router, moe_w1, moe_w3, moe_w2):
    bsz, seq, _ = x.shape
    v_first = None
    for l in range(DEPTH):
        h = _rms(x, norm_mix[l])
        z = h @ w_in[l]
        za, zb, zc, zd, zg = _split(z, [RWKV_COLS, SSM_COLS, ATT_COLS, GLA_COLS, GATE_COLS])
        v_mix = None
        if l > 0:
            v_mix = jax.nn.sigmoid(vres_v0[l - 1] + (h @ vres_down[l - 1]) @ vres_up[l - 1])
        o_a, v_raw = _rwkv7(za, rwkv_mu[l], rwkv_decay_up[l], rwkv_w0[l], rwkv_a_up[l], rwkv_a0[l],
                            rwkv_gate_up[l], rwkv_k_k[l], rwkv_k_a[l], rwkv_r_k[l], rwkv_ln_w[l],
                            rwkv_ln_b[l], v_first, v_mix)
        if l == 0:
            v_first = v_raw
        o_b = _mamba2(zb, ssm_conv_w[l], ssm_conv_b[l], ssm_dt_bias[l], ssm_a_log[l], ssm_d[l], ssm_norm_w[l])
        o_c = _band_attention(zc, att_q_gain[l], att_k_gain[l], att_rel_bias)
        o_d = _gla(zd, gla_gate_up[l], gla_gate_bias[l], gla_norm_w[l])
        branches = jnp.stack([o_a, o_b, o_c, o_d], axis=2).astype(x.dtype)
        u = jnp.einsum('bsnc,ncd->bsnd', branches, w_branch[l])
        gates = jax.nn.sigmoid(zg.reshape(bsz, seq, N_BRANCH, D_MODEL))
        x = x + jnp.sum(gates * u, axis=2) @ w_out[l]
        hf = _rms(x, norm_ffn[l])
        if l % 2 == 0:
            x = x + _swiglu(hf, ffn_w1[l // 2], ffn_w3[l // 2], ffn_w2[l // 2])
        else:
            x = x + _moe(hf, moe_router[l // 2], moe_w1[l // 2], moe_w3[l // 2], moe_w2[l // 2])
    return x
```

```python
import functools

import jax
import jax.numpy as jnp
from jax import lax
from jax.experimental import pallas as pl
from jax.experimental.pallas import tpu as pltpu

F32 = jnp.float32
BF16 = jnp.bfloat16
HIGHEST = lax.Precision.HIGHEST

D_MODEL = 1024
CHUNK = 64
BRANCH_DIM = 512
NORM_EPS = 1e-6
LANES = 128
VMEM_LIMIT = 56 * 1024 * 1024

RWKV_LN_EPS = 64e-5
RWKV_COLS = 1824
RWKV_PAD = 1920
SSM_COLS = 1544
SSM_PAD = 1664
SSM_HEADS = 8
SSM_STATE = 128
ATT_COLS = 1536
ATT_HEADS = 8
ATT_LEFT = 8 * CHUNK
REL_CLIP = 2 * CHUNK
GLA_COLS = 1552
GLA_PAD = 1664
GLA_GATE_NORM = 16.0
FFN_DIM = 2816
N_EXPERTS = 8
EXPERT_DIM = 3584
MOE_ROWS = 256


def _dot(a, b):
    return jnp.dot(a.astype(BF16), b.astype(BF16), preferred_element_type=F32)


def _dot_nt(a, b):
    return lax.dot_general(a.astype(BF16), b.astype(BF16), (((1,), (1,)), ((), ())),
                           preferred_element_type=F32)


def _dot_tn(a, b):
    return lax.dot_general(a.astype(BF16), b.astype(BF16), (((0,), (0,)), ((), ())),
                           preferred_element_type=F32)


def _dot_f32(a, b):
    return jnp.dot(a, b, precision=HIGHEST, preferred_element_type=F32)


def _dot_split(a, m):
    m = m.astype(BF16)
    hi = a.astype(BF16)
    r1 = a - hi.astype(F32)
    mid = r1.astype(BF16)
    lo = (r1 - mid.astype(F32)).astype(BF16)
    return (jnp.dot(hi, m, preferred_element_type=F32) + jnp.dot(mid, m, preferred_element_type=F32)
            + jnp.dot(lo, m, preferred_element_type=F32))


def _dot_split_l(m, a):
    m = m.astype(BF16)
    hi = a.astype(BF16)
    r1 = a - hi.astype(F32)
    mid = r1.astype(BF16)
    lo = (r1 - mid.astype(F32)).astype(BF16)
    return (jnp.dot(m, hi, preferred_element_type=F32) + jnp.dot(m, mid, preferred_element_type=F32)
            + jnp.dot(m, lo, preferred_element_type=F32))


def _softplus(x):
    return jnp.maximum(x, 0.0) + jnp.log(1.0 + jnp.exp(-jnp.abs(x)))


def _sigmoid(x):
    return 1.0 / (1.0 + jnp.exp(-x))


def _silu(x):
    return x * _sigmoid(x)


def _iota2(shape, axis):
    return lax.broadcasted_iota(jnp.int32, shape, axis)


def _group_mean_matrix(n, group):
    r = _iota2((n, n), 0) // group
    c = _iota2((n, n), 1) // group
    return jnp.where(r == c, 1.0 / group, 0.0).astype(F32)


def _chunk_tri(n):
    r = _iota2((n, n), 0)
    c = _iota2((n, n), 1)
    return jnp.where((r // CHUNK == c // CHUNK) & (r >= c), 1.0, 0.0).astype(F32)


def _resident(shape):
    nd = len(shape)
    return pl.BlockSpec(shape, lambda *_: (0,) * nd, pipeline_mode=pl.Buffered(1))


def _params(*sem):
    return pltpu.CompilerParams(dimension_semantics=sem, vmem_limit_bytes=VMEM_LIMIT)


def _inproj_kernel(x_ref, g_ref, wa_ref, wb_ref, wc_ref, wd_ref,
                   h_ref, za_ref, zb_ref, zc_ref, zd_ref):
    x = x_ref[...]
    ms = jnp.mean(x * x, axis=-1, keepdims=True)
    h = (x * lax.rsqrt(ms + NORM_EPS) * g_ref[...]).astype(BF16)
    h_ref[...] = h
    za_ref[...] = jnp.dot(h, wa_ref[...], preferred_element_type=F32)
    zb_ref[...] = jnp.dot(h, wb_ref[...], preferred_element_type=F32)
    zc_ref[...] = jnp.dot(h, wc_ref[...], preferred_element_type=F32)
    zd_ref[...] = jnp.dot(h, wd_ref[...], preferred_element_type=F32)


def _inproj(x2, g, wa, wb, wc, wd, tm=256):
    n_tok = x2.shape[0]
    row = lambda n: pl.BlockSpec((tm, n), lambda i: (i, 0))
    outs = (D_MODEL, RWKV_PAD, SSM_PAD, ATT_COLS, GLA_PAD)
    return pl.pallas_call(
        _inproj_kernel,
        grid=(n_tok // tm,),
        in_specs=[row(D_MODEL), _resident((1, D_MODEL)), _resident(wa.shape), _resident(wb.shape),
                  _resident(wc.shape), _resident(wd.shape)],
        out_specs=[row(n) for n in outs],
        out_shape=[jax.ShapeDtypeStruct((n_tok, outs[0]), BF16)]
        + [jax.ShapeDtypeStruct((n_tok, n), F32) for n in outs[1:]],
        compiler_params=_params("parallel"),
        name="inproj",
    )(x2, g, wa, wb, wc, wd)


def _tri_inverse(a_bd):
    n = a_bd.shape[0]
    eye = jnp.where(_iota2((n, n), 0) == _iota2((n, n), 1), 1.0, 0.0).astype(F32)
    t = eye + a_bd
    p = a_bd
    steps = CHUNK.bit_length() - 2
    for _ in range(steps):
        p = _dot(p, p)
        t = t + _dot(t, p)
    return t


def _rwkv_chunk(r, k, v, kk, ka, ld, tri, state_ref):
    cum = _dot_split_l(tri, ld)
    p_inc = jnp.exp(cum)
    p_inv = jnp.exp(-cum)
    p_prev = jnp.exp(cum - ld)
    p_last = jnp.exp(cum[CHUNK - 1:CHUNK, :])
    a_t = -kk * p_prev
    b_t = kk * ka * p_inv
    k_t = k * p_inv
    r_t = r * p_inc
    b_end = b_t * p_last
    k_end = k_t * p_last

    lane_head = _iota2((CHUNK, LANES), 1) // CHUNK
    row = _iota2((CHUNK, LANES), 0)
    col = _iota2((CHUNK, LANES), 1) % CHUNK
    strict = row > col
    incl = row >= col
    bd_mask = (_iota2((LANES, LANES), 0) // CHUNK) == (_iota2((LANES, LANES), 1) // CHUNK)

    def stack(z):
        return jnp.concatenate([jnp.where(lane_head == 0, z, 0.0), jnp.where(lane_head == 1, z, 0.0)], axis=0)

    outs = []
    for p in range(4):
        sl = slice(p * LANES, (p + 1) * LANES)
        ar = jnp.concatenate([a_t[:, sl], r_t[:, sl]], axis=0)
        g_b = _dot_nt(ar, stack(b_t[:, sl]))
        g_k = _dot_nt(ar, stack(k_t[:, sl]))
        a_ab = jnp.where(strict, g_b[:CHUNK], 0.0)
        a_ak = jnp.where(strict, g_k[:CHUNK], 0.0)
        a_rb = jnp.where(incl, g_b[CHUNK:], 0.0)
        a_rk = jnp.where(incl, g_k[CHUNK:], 0.0)
        t_bd = _tri_inverse(stack(a_ab))
        v_p = v[:, sl]
        v_bd = stack(v_p)
        h_t = state_ref[p]
        x_p = _dot_nt(a_t[:, sl], h_t) + _dot(a_ak, v_bd)
        u2 = _dot(t_bd, stack(x_p))
        u_p = u2[:CHUNK] + u2[CHUNK:]
        y_p = _dot_nt(r_t[:, sl], h_t) + _dot(a_rb, stack(u_p)) + _dot(a_rk, v_bd)
        upd = _dot_tn(jnp.concatenate([u_p, v_p], axis=0),
                      jnp.concatenate([b_end[:, sl], k_end[:, sl]], axis=0))
        state_ref[p] = h_t * p_last[:, sl] + jnp.where(bd_mask, upd, 0.0)
        outs.append(y_p)
    return outs


def _rwkv_kernel(has_vres, tb, *refs):
    if has_vres:
        (za_ref, vfirst_ref, mu_ref, dup_ref, w0_ref, aup_ref, a0_ref, gup_ref, vup_ref, v0_ref,
         kk_ref, ka_ref, rk_ref, lnw_ref, lnb_ref, o_ref, prev_ref, state_ref, y_ref) = refs
    else:
        (za_ref, mu_ref, dup_ref, w0_ref, aup_ref, a0_ref, gup_ref,
         kk_ref, ka_ref, rk_ref, lnw_ref, lnb_ref, o_ref, vraw_ref, prev_ref, state_ref, y_ref) = refs

    @pl.when(pl.program_id(1) == 0)
    def _():
        prev_ref[...] = jnp.zeros_like(prev_ref)
        state_ref[...] = jnp.zeros_like(state_ref)

    za = za_ref[...]
    shifted = pltpu.roll(za, 1, axis=0)
    shifted = jnp.where(_iota2(za.shape, 0) == 0, prev_ref[7:8, :], shifted)
    prev_ref[...] = za[tb - 8:, :]
    za = za + (shifted - za) * mu_ref[...]

    r = za[:, 0:512]
    k = za[:, 512:1024]
    v = za[:, 1024:1536]
    xwa = za[:, 1536:1664]
    xg = za[:, 1664:1920]
    w_log = -_softplus(-(w0_ref[...] + _dot_f32(jnp.tanh(xwa), dup_ref[...]))) - 0.5
    ld = -jnp.exp(w_log)
    a = _sigmoid(a0_ref[...] + _dot_f32(xwa, aup_ref[...]))
    g = _dot(_sigmoid(xg), gup_ref[...])
    if has_vres:
        v_mix = _sigmoid(v0_ref[...] + _dot_f32(xg, vup_ref[...]))
        v = v + (vfirst_ref[...] - v) * v_mix
    else:
        vraw_ref[...] = v
    kk = k * kk_ref[...]
    k = k * (1.0 + (a - 1.0) * ka_ref[...])
    head_sum = _group_mean_matrix(512, CHUNK) * float(CHUNK)
    kk = kk / jnp.maximum(jnp.sqrt(_dot_split(kk * kk, head_sum)), 1e-12)

    tri = _chunk_tri(CHUNK)
    for c in range(tb // CHUNK):
        rows = slice(c * CHUNK, (c + 1) * CHUNK)
        ys = _rwkv_chunk(r[rows], k[rows], v[rows], kk[rows], a[rows], ld[rows], tri, state_ref)
        for p in range(4):
            y_ref[rows, p * LANES:(p + 1) * LANES] = ys[p]

    y = y_ref[...]
    head_mean = _group_mean_matrix(512, CHUNK)
    mean = _dot_split(y, head_mean)
    yc = y - mean
    var = _dot_split(yc * yc, head_mean)
    y = yc * lax.rsqrt(var + RWKV_LN_EPS) * lnw_ref[...] + lnb_ref[...]
    y = y + _dot_split(r * k * rk_ref[...], head_sum) * v
    o_ref[...] = (y * g).astype(o_ref.dtype)


def _rwkv(za, v_first, prm, tb=128):
    bsz, seq, _ = za.shape
    has_vres = v_first is not None
    blk = lambda n: pl.BlockSpec((None, tb, n), lambda b, i: (b, i, 0))
    names = (["mu", "dup", "w0", "aup", "a0", "gup"] + (["vup", "v0"] if has_vres else [])
             + ["kk", "ka", "rk", "lnw", "lnb"])
    weights = [prm[n] for n in names]
    ins = [za] + ([v_first] if has_vres else []) + weights
    in_specs = [blk(RWKV_PAD)] + ([blk(512)] if has_vres else []) + [_resident(w.shape) for w in weights]
    out_shape = [jax.ShapeDtypeStruct((bsz, seq, 512), BF16)]
    out_specs = [blk(512)]
    if not has_vres:
        out_shape.append(jax.ShapeDtypeStruct((bsz, seq, 512), F32))
        out_specs.append(blk(512))
    res = pl.pallas_call(
        functools.partial(_rwkv_kernel, has_vres, tb),
        grid=(bsz, seq // tb),
        in_specs=in_specs,
        out_specs=out_specs,
        out_shape=out_shape,
        scratch_shapes=[pltpu.VMEM((8, RWKV_PAD), F32), pltpu.VMEM((4, LANES, LANES), F32),
                        pltpu.VMEM((tb, 512), F32)],
        compiler_params=_params("parallel", "arbitrary"),
        name="rwkv7",
    )(*ins)
    return (res[0], v_first) if has_vres else (res[0], res[1])


def _mamba_kernel(tb, zb_ref, cw_ref, cb_ref, dtb_ref, alog_ref, dexp_ref, nw_ref,
                  o_ref, xbuf_ref, state_ref, y_ref):
    @pl.when(pl.program_id(1) == 0)
    def _():
        xbuf_ref[0:8, :] = jnp.zeros((8, 1024), F32)
        state_ref[...] = jnp.zeros_like(state_ref)

    xbuf_ref[8:8 + tb, :] = zb_ref[:, 512:1536]
    conv = cb_ref[...]
    for i in range(4):
        conv = conv + cw_ref[i:i + 1, :] * xbuf_ref[5 + i:5 + i + tb, :]
    xbuf_ref[0:8, :] = xbuf_ref[tb:tb + 8, :]
    xbc = _silu(conv)
    xs = xbc[:, 0:512]
    bm = xbc[:, 512:768]
    cm = xbc[:, 768:1024]

    lane = _iota2((1, LANES), 1)
    dt = _softplus(zb_ref[:, 1536:1664] + dtb_ref[...])
    a_neg = jnp.where(lane < SSM_HEADS, -jnp.exp(alog_ref[...]), 0.0)
    acs = _dot_split_l(_chunk_tri(tb), dt * a_neg)
    expand = jnp.where(_iota2((LANES, 512), 0) == _iota2((LANES, 512), 1) // CHUNK, 1.0, 0.0)
    dt_e = _dot_split(dt, expand)
    acs_e = _dot_split(acs, expand)
    xdt = xs * dt_e

    causal = _iota2((CHUNK, CHUNK), 0) >= _iota2((CHUNK, CHUNK), 1)
    lane_head = _iota2((CHUNK, 512), 1) // CHUNK
    for c in range(tb // CHUNK):
        rows = slice(c * CHUNK, (c + 1) * CHUNK)
        acs_c = acs[rows]
        acs_t = acs_c.T
        acs_ec = acs_e[rows]
        last_e = acs_ec[CHUNK - 1:CHUNK, :]
        xdt_c = xdt[rows]
        y = jnp.zeros((CHUNK, 512), F32)
        for g in range(2):
            gs = slice(g * SSM_STATE, (g + 1) * SSM_STATE)
            cb = _dot_nt(cm[rows, gs], bm[rows, gs])
            for e in range(4):
                h = g * 4 + e
                seg = acs_c[:, h:h + 1] - acs_t[h:h + 1, :]
                m = cb * jnp.exp(jnp.where(causal, seg, -jnp.inf))
                y = y + _dot(m, jnp.where(lane_head == h, xdt_c, 0.0))
        x_out = xdt_c * jnp.exp(last_e - acs_ec)
        e_in = jnp.exp(acs_ec)
        e_last = jnp.exp(last_e)
        for g in range(2):
            gs = slice(g * SSM_STATE, (g + 1) * SSM_STATE)
            ls = slice(g * 256, (g + 1) * 256)
            s_g = state_ref[g]
            y_ref[rows, ls] = y[:, ls] + _dot(cm[rows, gs], s_g) * e_in[:, ls]
            state_ref[g] = s_g * e_last[:, ls] + _dot_tn(bm[rows, gs], x_out[:, ls])

    y = (y_ref[...] + xs * dexp_ref[...]) * _silu(zb_ref[:, 0:512])
    for g in range(2):
        ls = slice(g * 256, (g + 1) * 256)
        yg = y[:, ls]
        ms = jnp.mean(yg * yg, axis=-1, keepdims=True)
        o_ref[:, ls] = (yg * lax.rsqrt(ms + NORM_EPS) * nw_ref[:, ls]).astype(o_ref.dtype)


def _mamba(zb, prm, tb=128):
    bsz, seq, _ = zb.shape
    blk = lambda n: pl.BlockSpec((None, tb, n), lambda b, i: (b, i, 0))
    weights = [prm[n] for n in ("cw", "cb", "dtb", "alog", "dexp", "nw")]
    return pl.pallas_call(
        functools.partial(_mamba_kernel, tb),
        grid=(bsz, seq // tb),
        in_specs=[blk(SSM_PAD)] + [_resident(w.shape) for w in weights],
        out_specs=blk(512),
        out_shape=jax.ShapeDtypeStruct((bsz, seq, 512), BF16),
        scratch_shapes=[pltpu.VMEM((tb + 8, 1024), F32), pltpu.VMEM((2, SSM_STATE, 256), F32),
                        pltpu.VMEM((tb, 512), F32)],
        compiler_params=_params("parallel", "arbitrary"),
        name="mamba2",
    )(zb, *weights)


def _attn_kernel(tq, seq, q_ref, k_ref, v_ref, qg_ref, kg_ref, bias_ref, o_ref, kn_ref, vb_ref):
    i = pl.program_id(1)
    head_mean = _group_mean_matrix(512, CHUNK)
    win = tq + ATT_LEFT

    @pl.when(i == 0)
    def _():
        kn_ref[0:ATT_LEFT, :] = jnp.zeros((ATT_LEFT, 512), BF16)
        vb_ref[0:ATT_LEFT, :] = jnp.zeros((ATT_LEFT, 512), BF16)
        for j in range(seq // tq):
            kj = k_ref[j * tq:(j + 1) * tq, :]
            ms = _dot_split(kj * kj, head_mean)
            kn_ref[ATT_LEFT + j * tq:ATT_LEFT + (j + 1) * tq, :] = (
                kj * lax.rsqrt(ms + NORM_EPS) * kg_ref[...]).astype(BF16)
            vb_ref[ATT_LEFT + j * tq:ATT_LEFT + (j + 1) * tq, :] = v_ref[j * tq:(j + 1) * tq, :].astype(BF16)

    q = q_ref[...]
    ms = _dot_split(q * q, head_mean)
    qn = q * lax.rsqrt(ms + NORM_EPS) * qg_ref[...] * (CHUNK ** -0.5)
    start = pl.multiple_of(i * tq, tq)
    kwin = kn_ref[pl.ds(start, win), :]
    vwin = vb_ref[pl.ds(start, win), :]
    qc = ATT_LEFT // CHUNK + _iota2((tq, win), 0) // CHUNK
    kc = _iota2((tq, win), 1) // CHUNK
    first_kc = jnp.maximum(qc - ATT_LEFT // CHUNK, ATT_LEFT // CHUNK - i * (tq // CHUNK))
    valid = (kc <= qc) & (kc >= first_kc)
    lane_head = _iota2((tq, LANES), 1) // CHUNK
    for p in range(4):
        ls = slice(p * LANES, (p + 1) * LANES)
        acc = jnp.zeros((tq, LANES), F32)
        for s in range(2):
            qh = jnp.where(lane_head == s, qn[:, ls], 0.0)
            sc = _dot_nt(qh, kwin[:, ls]) + bias_ref[2 * p + s]
            sc = jnp.where(valid, sc, -jnp.inf)
            e = jnp.exp(sc - jnp.max(sc, axis=-1, keepdims=True))
            pr = e / jnp.sum(e, axis=-1, keepdims=True)
            acc = jnp.where(lane_head == s, _dot(pr, vwin[:, ls]), acc)
        o_ref[:, ls] = acc.astype(o_ref.dtype)


def _attention(zc, q_gain, k_gain, rel_bias, tq=256):
    bsz, seq, _ = zc.shape
    win = tq + ATT_LEFT
    rel = jnp.clip(jnp.arange(win)[None, :] - ATT_LEFT - jnp.arange(tq)[:, None], -REL_CLIP, REL_CLIP) + REL_CLIP
    bias = rel_bias.astype(F32)[:, rel]
    qg = jnp.tile(q_gain, ATT_HEADS)[None, :]
    kg = jnp.tile(k_gain, ATT_HEADS)[None, :]
    full = lambda col: pl.BlockSpec((None, seq, 512), lambda b, i: (b, 0, col))
    return pl.pallas_call(
        functools.partial(_attn_kernel, tq, seq),
        grid=(bsz, seq // tq),
        in_specs=[pl.BlockSpec((None, tq, 512), lambda b, i: (b, i, 0)), full(1), full(2),
                  _resident(qg.shape), _resident(kg.shape), _resident(bias.shape)],
        out_specs=pl.BlockSpec((None, tq, 512), lambda b, i: (b, i, 0)),
        out_shape=jax.ShapeDtypeStruct((bsz, seq, 512), BF16),
        scratch_shapes=[pltpu.VMEM((ATT_LEFT + seq, 512), BF16), pltpu.VMEM((ATT_LEFT + seq, 512), BF16)],
        compiler_params=_params("parallel", "arbitrary"),
        name="band_attention",
    )(zc, zc, zc, qg, kg, bias)


def _gla_kernel(tb, zd_ref, gup_ref, gb_ref, nw_ref, o_ref, state_ref, y_ref):
    @pl.when(pl.program_id(1) == 0)
    def _():
        state_ref[...] = jnp.zeros_like(state_ref)

    q = zd_ref[:, 0:256] * (CHUNK ** -0.5)
    k = zd_ref[:, 256:512]
    v = zd_ref[:, 512:1024]
    log_a = -_softplus(-(_dot_f32(zd_ref[:, 1536:1664], gup_ref[...]) + gb_ref[...])) / GLA_GATE_NORM
    bcum = _dot_split_l(_chunk_tri(tb), log_a)

    k_head = _iota2((CHUNK, 256), 1) // CHUNK
    v_head = _iota2((CHUNK, 512), 1) // LANES
    causal = _iota2((CHUNK, 256), 0) >= _iota2((CHUNK, 256), 1) % CHUNK
    bd = (_iota2((512, 256), 0) // LANES) == (_iota2((512, 256), 1) // CHUNK)
    for c in range(tb // CHUNK):
        rows = slice(c * CHUNK, (c + 1) * CHUNK)
        bc = bcum[rows]
        blast = bc[CHUNK - 1:CHUNK, :]
        qg = q[rows] * jnp.exp(bc)
        kg = k[rows] * jnp.exp(-bc)
        kd = k[rows] * jnp.exp(blast - bc)
        v_c = v[rows]
        kg_bd = jnp.concatenate([jnp.where(k_head == h, kg, 0.0) for h in range(4)], axis=0)
        v_bd = jnp.concatenate([jnp.where(v_head == h, v_c, 0.0) for h in range(4)], axis=0)
        att = jnp.where(causal, _dot_nt(qg, kg_bd), 0.0)
        st = state_ref[...]
        y_ref[rows, :] = _dot(att, v_bd) + _dot_nt(qg, st)
        state_ref[...] = st * jnp.exp(blast) + jnp.where(bd, _dot_tn(v_c, kd), 0.0)

    o = y_ref[...]
    for h in range(4):
        ls = slice(h * LANES, (h + 1) * LANES)
        oh = o[:, ls]
        ms = jnp.mean(oh * oh, axis=-1, keepdims=True)
        o_ref[:, ls] = (oh * lax.rsqrt(ms + NORM_EPS) * nw_ref[...]
                        * _silu(zd_ref[:, 1024 + h * LANES:1024 + (h + 1) * LANES])).astype(o_ref.dtype)


def _gla(zd, prm, tb=128):
    bsz, seq, _ = zd.shape
    blk = lambda n: pl.BlockSpec((None, tb, n), lambda b, i: (b, i, 0))
    weights = [prm[n] for n in ("gup", "gb", "nw")]
    return pl.pallas_call(
        functools.partial(_gla_kernel, tb),
        grid=(bsz, seq // tb),
        in_specs=[blk(GLA_PAD)] + [_resident(w.shape) for w in weights],
        out_specs=blk(512),
        out_shape=jax.ShapeDtypeStruct((bsz, seq, 512), BF16),
        scratch_shapes=[pltpu.VMEM((512, 256), F32), pltpu.VMEM((tb, 512), F32)],
        compiler_params=_params("parallel", "arbitrary"),
        name="gla",
    )(zd, *weights)


def _merge_kernel(x_ref, h_ref, oa_ref, ob_ref, oc_ref, od_ref, wg_ref, wb_ref, wo_ref, nf_ref,
                  x1_ref, hf_ref):
    h = h_ref[...]
    acc = jnp.zeros(x_ref.shape, F32)
    for i, o_ref in enumerate((oa_ref, ob_ref, oc_ref, od_ref)):
        gate = _sigmoid(jnp.dot(h, wg_ref[:, i * D_MODEL:(i + 1) * D_MODEL], preferred_element_type=F32))
        acc = acc + gate * jnp.dot(o_ref[...], wb_ref[i], preferred_element_type=F32)
    x1 = x_ref[...] + jnp.dot(acc.astype(BF16), wo_ref[...], preferred_element_type=F32)
    x1_ref[...] = x1
    ms = jnp.mean(x1 * x1, axis=-1, keepdims=True)
    hf_ref[...] = (x1 * lax.rsqrt(ms + NORM_EPS) * nf_ref[...]).astype(BF16)


def _merge(x2, h, outs, wg, wb, wo, nf, tm=256):
    n_tok = x2.shape[0]
    row = lambda n: pl.BlockSpec((tm, n), lambda i: (i, 0))
    return pl.pallas_call(
        _merge_kernel,
        grid=(n_tok // tm,),
        in_specs=[row(D_MODEL), row(D_MODEL)] + [row(BRANCH_DIM)] * 4
        + [_resident(wg.shape), _resident(wb.shape), _resident(wo.shape), _resident(nf.shape)],
        out_specs=[row(D_MODEL), row(D_MODEL)],
        out_shape=[jax.ShapeDtypeStruct((n_tok, D_MODEL), F32), jax.ShapeDtypeStruct((n_tok, D_MODEL), BF16)],
        compiler_params=_params("parallel"),
        name="merge",
    )(x2, h, *outs, wg, wb, wo, nf)


def _ffn_kernel(tf, x1_ref, hf_ref, w1_ref, w3_ref, w2_ref, o_ref):
    hf = hf_ref[...]
    acc = x1_ref[...]
    for lo in range(0, FFN_DIM, tf):
        cols = slice(lo, min(lo + tf, FFN_DIM))
        a = jnp.dot(hf, w1_ref[:, cols], preferred_element_type=F32)
        b = jnp.dot(hf, w3_ref[:, cols], preferred_element_type=F32)
        acc = acc + jnp.dot((_silu(a) * b).astype(BF16), w2_ref[cols, :], preferred_element_type=F32)
    o_ref[...] = acc


def _ffn(x1, hf, w1, w3, w2, tm=512, tf=512):
    n_tok = x1.shape[0]
    row = pl.BlockSpec((tm, D_MODEL), lambda i: (i, 0))
    return pl.pallas_call(
        functools.partial(_ffn_kernel, tf),
        grid=(n_tok // tm,),
        in_specs=[row, row, _resident(w1.shape), _resident(w3.shape), _resident(w2.shape)],
        out_specs=row,
        out_shape=jax.ShapeDtypeStruct((n_tok, D_MODEL), F32),
        compiler_params=_params("parallel"),
        name="ffn",
    )(x1, hf, w1, w3, w2)


def _router_kernel(tm, x1_ref, nf_ref, wr_ref, idx_ref, gate_ref, rank_ref, cnt_ref, carry_ref):
    @pl.when(pl.program_id(0) == 0)
    def _():
        carry_ref[...] = jnp.zeros_like(carry_ref)

    x1 = x1_ref[...]
    ms = jnp.mean(x1 * x1, axis=-1, keepdims=True)
    hf = x1 * lax.rsqrt(ms + NORM_EPS) * nf_ref[...]
    logits = lax.dot_general(wr_ref[...], hf, (((1,), (1,)), ((), ())), precision=HIGHEST,
                             preferred_element_type=F32)
    e_iota = _iota2((N_EXPERTS, tm), 0)
    m1 = jnp.max(logits, axis=0, keepdims=True)
    i1 = jnp.min(jnp.where(logits == m1, e_iota, N_EXPERTS), axis=0, keepdims=True)
    rest = jnp.where(e_iota == i1, -jnp.inf, logits)
    m2 = jnp.max(rest, axis=0, keepdims=True)
    i2 = jnp.min(jnp.where(rest == m2, e_iota, N_EXPERTS), axis=0, keepdims=True)
    e2 = jnp.exp(m2 - m1)
    gate_ref[0:1, :] = 1.0 / (1.0 + e2)
    gate_ref[1:2, :] = e2 / (1.0 + e2)
    idx_ref[0:1, :] = i1
    idx_ref[1:2, :] = i2
    hit1 = jnp.where(e_iota == i1, 1.0, 0.0)
    hit2 = jnp.where(e_iota == i2, 1.0, 0.0)
    before = jnp.where(_iota2((tm, tm), 0) < _iota2((tm, tm), 1), 1.0, 0.0)
    prior = _dot(hit1 + hit2, before) + carry_ref[:, 0:1]
    rank_ref[0:1, :] = jnp.sum(hit1 * prior, axis=0, keepdims=True).astype(jnp.int32)
    rank_ref[1:2, :] = jnp.sum(hit2 * prior, axis=0, keepdims=True).astype(jnp.int32)
    carry_ref[...] = carry_ref[...] + jnp.sum(hit1 + hit2, axis=1, keepdims=True)
    cnt_ref[...] = carry_ref[...]


def _router(x1, nf, wr_t, tm=256):
    n_tok = x1.shape[0]
    col = pl.BlockSpec((2, tm), lambda i: (0, i))
    return pl.pallas_call(
        functools.partial(_router_kernel, tm),
        grid=(n_tok // tm,),
        in_specs=[pl.BlockSpec((tm, D_MODEL), lambda i: (i, 0)), _resident(nf.shape), _resident(wr_t.shape)],
        out_specs=[col, col, col, pl.BlockSpec((N_EXPERTS, LANES), lambda i: (0, 0))],
        out_shape=[jax.ShapeDtypeStruct((2, n_tok), jnp.int32), jax.ShapeDtypeStruct((2, n_tok), F32),
                   jax.ShapeDtypeStruct((2, n_tok), jnp.int32), jax.ShapeDtypeStruct((N_EXPERTS, LANES), F32)],
        scratch_shapes=[pltpu.VMEM((N_EXPERTS, LANES), F32)],
        compiler_params=_params("arbitrary"),
        name="moe_router",
    )(x1, nf, wr_t)


def _expert_up_kernel(ge_ref, nv_ref, x_ref, w1_ref, w3_ref, act_ref):
    g = pl.program_id(1)

    @pl.when(g < nv_ref[0])
    def _():
        x = x_ref[...]
        a = jnp.dot(x, w1_ref[...], preferred_element_type=F32)
        b = jnp.dot(x, w3_ref[...], preferred_element_type=F32)
        act_ref[...] = (_silu(a) * b).astype(act_ref.dtype)

    @pl.when(g >= nv_ref[0])
    def _():
        act_ref[...] = jnp.zeros_like(act_ref)


def _expert_down_kernel(ge_ref, nv_ref, act_ref, w2_ref, y_ref):
    g = pl.program_id(0)

    @pl.when(g < nv_ref[0])
    def _():
        y_ref[...] = jnp.dot(act_ref[...], w2_ref[...], preferred_element_type=F32)

    @pl.when(g >= nv_ref[0])
    def _():
        y_ref[...] = jnp.zeros_like(y_ref)


def _experts(xg, group_expert, n_valid, w1, w3, w2, tf=512):
    n_rows = xg.shape[0]
    n_groups = n_rows // MOE_ROWS
    act = pl.pallas_call(
        _expert_up_kernel,
        grid_spec=pltpu.PrefetchScalarGridSpec(
            num_scalar_prefetch=2,
            grid=(EXPERT_DIM // tf, n_groups),
            in_specs=[pl.BlockSpec((MOE_ROWS, D_MODEL), lambda f, g, ge, nv: (g, 0)),
                      pl.BlockSpec((None, D_MODEL, tf), lambda f, g, ge, nv: (ge[g], 0, f)),
                      pl.BlockSpec((None, D_MODEL, tf), lambda f, g, ge, nv: (ge[g], 0, f))],
            out_specs=pl.BlockSpec((MOE_ROWS, tf), lambda f, g, ge, nv: (g, f)),
        ),
        out_shape=jax.ShapeDtypeStruct((n_rows, EXPERT_DIM), BF16),
        compiler_params=_params("parallel", "arbitrary"),
        name="expert_up",
    )(group_expert, n_valid, xg, w1, w3)
    return pl.pallas_call(
        _expert_down_kernel,
        grid_spec=pltpu.PrefetchScalarGridSpec(
            num_scalar_prefetch=2,
            grid=(n_groups,),
            in_specs=[pl.BlockSpec((MOE_ROWS, EXPERT_DIM), lambda g, ge, nv: (g, 0)),
                      pl.BlockSpec((None, EXPERT_DIM, D_MODEL), lambda g, ge, nv: (ge[g], 0, 0))],
            out_specs=pl.BlockSpec((MOE_ROWS, D_MODEL), lambda g, ge, nv: (g, 0)),
        ),
        out_shape=jax.ShapeDtypeStruct((n_rows, D_MODEL), F32),
        compiler_params=_params("arbitrary"),
        name="expert_down",
    )(group_expert, n_valid, act, w2)


def _moe(x1, hf, nf, w_router, w1, w3, w2):
    n_tok = x1.shape[0]
    idx, gate, rank, cnt = _router(x1, nf, w_router.T)
    counts = cnt[:, 0].astype(jnp.int32)
    padded = (counts + MOE_ROWS - 1) // MOE_ROWS * MOE_ROWS
    end_padded = jnp.cumsum(padded)
    start_padded = end_padded - padded
    dest = start_padded[idx] + rank
    n_groups = (n_tok * 2 + MOE_ROWS - 1) // MOE_ROWS + N_EXPERTS
    n_rows = n_groups * MOE_ROWS
    group_expert = jnp.minimum(
        jnp.searchsorted(end_padded, jnp.arange(n_groups) * MOE_ROWS, side="right"), N_EXPERTS - 1).astype(jnp.int32)
    n_valid = (end_padded[-1:] // MOE_ROWS).astype(jnp.int32)
    tok = jnp.broadcast_to(jnp.arange(n_tok, dtype=jnp.int32)[None, :], (2, n_tok))
    row_token = jnp.full((n_rows,), n_tok, jnp.int32).at[dest.reshape(-1)].set(tok.reshape(-1))
    hf_pad = jnp.concatenate([hf, jnp.zeros((1, D_MODEL), hf.dtype)], axis=0)
    xg = hf_pad[row_token]
    y = _experts(xg, group_expert, n_valid, w1, w3, w2)
    return x1 + gate[0][:, None] * y[dest[0]] + gate[1][:, None] * y[dest[1]]


def _row(v, pad=0):
    v = v.reshape(1, -1).astype(F32)
    return jnp.pad(v, ((0, 0), (0, pad))) if pad else v


def _rows_at(w, start, total):
    return jnp.pad(w.astype(F32), ((start, total - start - w.shape[0]), (0, 0)))


def kernel(x, w_in, norm_mix, rwkv_mu, rwkv_decay_up, rwkv_w0, rwkv_a_up, rwkv_a0, rwkv_gate_up, rwkv_k_k, rwkv_k_a, rwkv_r_k, rwkv_ln_w, rwkv_ln_b, vres_down, vres_up, vres_v0, ssm_conv_w, ssm_conv_b, ssm_dt_bias, ssm_a_log, ssm_d, ssm_norm_w, att_q_gain, att_k_gain, att_rel_bias, gla_gate_up, gla_gate_bias, gla_norm_w, w_branch, w_out, norm_ffn, ffn_w1, ffn_w3, ffn_w2, moe_router, moe_w1, moe_w3, moe_w2):
    bsz, seq, _ = x.shape
    n_tok = bsz * seq
    depth = w_in.shape[0]
    x2 = x.reshape(n_tok, D_MODEL)
    v_first = None
    o_rw, o_ss, o_at = RWKV_COLS, RWKV_COLS + SSM_COLS, RWKV_COLS + SSM_COLS + ATT_COLS
    o_gl = o_at + GLA_COLS
    for l in range(depth):
        w = w_in[l]
        vd = vres_down[l - 1] if l > 0 else jnp.zeros((D_MODEL, 32), F32)
        wa = jnp.concatenate([w[:, :o_rw], vd, jnp.zeros((D_MODEL, RWKV_PAD - RWKV_COLS - 32), F32)], axis=1)
        wb = jnp.pad(w[:, o_rw:o_ss], ((0, 0), (0, SSM_PAD - SSM_COLS)))
        wc = w[:, o_ss:o_at]
        gl = w[:, o_at:o_gl]
        wd = jnp.concatenate([gl[:, 0:1024], gl[:, 1040:1552], gl[:, 1024:1040],
                              jnp.zeros((D_MODEL, GLA_PAD - GLA_COLS), F32)], axis=1)
        wg = w[:, o_gl:]
        h, za, zb, zc, zd = _inproj(x2, _row(norm_mix[l]), wa.astype(BF16), wb.astype(BF16),
                                    wc.astype(BF16), wd.astype(BF16))
        shp = lambda z: z.reshape(bsz, seq, z.shape[-1])

        rw = dict(mu=_row(rwkv_mu[l], RWKV_PAD - RWKV_COLS),
                  dup=_rows_at(rwkv_decay_up[l], 0, LANES), w0=_row(rwkv_w0[l]),
                  aup=_rows_at(rwkv_a_up[l], 64, LANES), a0=_row(rwkv_a0[l]),
                  gup=_rows_at(rwkv_gate_up[l], 0, 256).astype(BF16),
                  kk=_row(rwkv_k_k[l]), ka=_row(rwkv_k_a[l]), rk=_row(rwkv_r_k[l]),
                  lnw=_row(rwkv_ln_w[l]), lnb=_row(rwkv_ln_b[l]))
        if l > 0:
            rw.update(vup=_rows_at(vres_up[l - 1], 160, 256), v0=_row(vres_v0[l - 1]))
        o_a, v_first = _rwkv(shp(za), v_first, rw)
        o_b = _mamba(shp(zb), dict(cw=ssm_conv_w[l], cb=_row(ssm_conv_b[l]),
                                   dtb=_row(ssm_dt_bias[l], LANES - SSM_HEADS),
                                   alog=_row(ssm_a_log[l], LANES - SSM_HEADS),
                                   dexp=_row(jnp.repeat(ssm_d[l], CHUNK)), nw=_row(ssm_norm_w[l])))
        o_c = _attention(shp(zc), att_q_gain[l], att_k_gain[l], att_rel_bias)
        o_d = _gla(shp(zd), dict(gup=_rows_at(gla_gate_up[l], 0, LANES), gb=_row(gla_gate_bias[l]),
                                 nw=_row(gla_norm_w[l])))
        outs = [o.reshape(n_tok, BRANCH_DIM) for o in (o_a, o_b, o_c, o_d)]
        x1, hf = _merge(x2, h, outs, wg.astype(BF16), w_branch[l].astype(BF16), w_out[l].astype(BF16),
                        _row(norm_ffn[l]))
        if l % 2 == 0:
            x2 = _ffn(x1, hf, ffn_w1[l // 2].astype(BF16), ffn_w3[l // 2].astype(BF16),
                      ffn_w2[l // 2].astype(BF16))
        else:
            x2 = _moe(x1, hf, _row(norm_ffn[l]), moe_router[l // 2], moe_w1[l // 2].astype(BF16),
                      moe_w3[l // 2].astype(BF16), moe_w2[l // 2].astype(BF16))
    return x2.reshape(bsz, seq, D_MODEL)
```

```python
import functools

import jax
import jax.numpy as jnp
from jax import lax
from jax.experimental import pallas as pl
from jax.experimental.pallas import tpu as pltpu

F32 = jnp.float32
BF16 = jnp.bfloat16
HIGHEST = lax.Precision.HIGHEST

D_MODEL = 1024
CHUNK = 64
BRANCH_DIM = 512
NORM_EPS = 1e-6
LANES = 128
VMEM_LIMIT = 56 * 1024 * 1024

RWKV_LN_EPS = 64e-5
RWKV_COLS = 1824
RWKV_PAD = 1920
SSM_COLS = 1544
SSM_PAD = 1664
SSM_HEADS = 8
SSM_STATE = 128
ATT_COLS = 1536
ATT_HEADS = 8
ATT_LEFT = 8 * CHUNK
REL_CLIP = 2 * CHUNK
GLA_COLS = 1552
GLA_PAD = 1664
GLA_GATE_NORM = 16.0
FFN_DIM = 2816
N_EXPERTS = 8
EXPERT_DIM = 3584
MOE_ROWS = 256


def _dot(a, b):
    return jnp.dot(a.astype(BF16), b.astype(BF16), preferred_element_type=F32)


def _dot_nt(a, b):
    return lax.dot_general(a.astype(BF16), b.astype(BF16), (((1,), (1,)), ((), ())),
                           preferred_element_type=F32)


def _dot_tn(a, b):
    return lax.dot_general(a.astype(BF16), b.astype(BF16), (((0,), (0,)), ((), ())),
                           preferred_element_type=F32)


def _dot_f32(a, b):
    return jnp.dot(a, b, precision=HIGHEST, preferred_element_type=F32)


def _dot_split(a, m):
    m = m.astype(BF16)
    hi = a.astype(BF16)
    r1 = a - hi.astype(F32)
    mid = r1.astype(BF16)
    lo = (r1 - mid.astype(F32)).astype(BF16)
    return (jnp.dot(hi, m, preferred_element_type=F32) + jnp.dot(mid, m, preferred_element_type=F32)
            + jnp.dot(lo, m, preferred_element_type=F32))


def _dot_split_l(m, a):
    m = m.astype(BF16)
    hi = a.astype(BF16)
    r1 = a - hi.astype(F32)
    mid = r1.astype(BF16)
    lo = (r1 - mid.astype(F32)).astype(BF16)
    return (jnp.dot(m, hi, preferred_element_type=F32) + jnp.dot(m, mid, preferred_element_type=F32)
            + jnp.dot(m, lo, preferred_element_type=F32))


def _softplus(x):
    return jnp.maximum(x, 0.0) + jnp.log(1.0 + jnp.exp(-jnp.abs(x)))


def _sigmoid(x):
    return 1.0 / (1.0 + jnp.exp(-x))


def _silu(x):
    return x * _sigmoid(x)


def _iota2(shape, axis):
    return lax.broadcasted_iota(jnp.int32, shape, axis)


def _group_mean_matrix(n, group):
    r = _iota2((n, n), 0) // group
    c = _iota2((n, n), 1) // group
    return jnp.where(r == c, 1.0 / group, 0.0).astype(F32)


def _chunk_tri(n):
    r = _iota2((n, n), 0)
    c = _iota2((n, n), 1)
    return jnp.where((r // CHUNK == c // CHUNK) & (r >= c), 1.0, 0.0).astype(F32)


def _resident(shape):
    nd = len(shape)
    return pl.BlockSpec(shape, lambda *_: (0,) * nd, pipeline_mode=pl.Buffered(1))


def _params(*sem):
    return pltpu.CompilerParams(dimension_semantics=sem, vmem_limit_bytes=VMEM_LIMIT)


def _inproj_kernel(x_ref, g_ref, wa_ref, wb_ref, wc_ref, wd_ref,
                   h_ref, za_ref, zb_ref, zc_ref, zd_ref):
    x = x_ref[...]
    ms = jnp.mean(x * x, axis=-1, keepdims=True)
    h = (x * lax.rsqrt(ms + NORM_EPS) * g_ref[...]).astype(BF16)
    h_ref[...] = h
    za_ref[...] = jnp.dot(h, wa_ref[...], preferred_element_type=F32)
    zb_ref[...] = jnp.dot(h, wb_ref[...], preferred_element_type=F32)
    zc_ref[...] = jnp.dot(h, wc_ref[...], preferred_element_type=F32)
    zd_ref[...] = jnp.dot(h, wd_ref[...], preferred_element_type=F32)


def _inproj(x2, g, wa, wb, wc, wd, tm=256):
    n_tok = x2.shape[0]
    row = lambda n: pl.BlockSpec((tm, n), lambda i: (i, 0))
    outs = (D_MODEL, RWKV_PAD, SSM_PAD, ATT_COLS, GLA_PAD)
    return pl.pallas_call(
        _inproj_kernel,
        grid=(n_tok // tm,),
        in_specs=[row(D_MODEL), _resident((1, D_MODEL)), _resident(wa.shape), _resident(wb.shape),
                  _resident(wc.shape), _resident(wd.shape)],
        out_specs=[row(n) for n in outs],
        out_shape=[jax.ShapeDtypeStruct((n_tok, outs[0]), BF16)]
        + [jax.ShapeDtypeStruct((n_tok, n), F32) for n in outs[1:]],
        compiler_params=_params("parallel"),
        name="inproj",
    )(x2, g, wa, wb, wc, wd)


def _rwkv_block(r, k, v, kk, ka, ld, tb, state_ref, y_ref):
    n_chunks = tb // CHUNK
    cum = _dot_split_l(_chunk_tri(tb), ld)
    p_inv = jnp.exp(-cum)
    a_t = -kk * jnp.exp(cum - ld)
    b_t = kk * ka * p_inv
    k_t = k * p_inv
    r_t = r * jnp.exp(cum)

    lane_head = _iota2((CHUNK, LANES), 1) // CHUNK
    row = _iota2((CHUNK, LANES), 0)
    col = _iota2((CHUNK, LANES), 1) % CHUNK
    strict = row > col
    incl = row >= col
    bd_mask = (_iota2((LANES, LANES), 0) // CHUNK) == (_iota2((LANES, LANES), 1) // CHUNK)
    eye = jnp.where(_iota2((LANES, LANES), 0) == _iota2((LANES, LANES), 1), 1.0, 0.0).astype(F32)

    def stack(z):
        return jnp.concatenate([jnp.where(lane_head == 0, z, 0.0), jnp.where(lane_head == 1, z, 0.0)], axis=0)

    def sub(z, u):
        c, p = u
        return z[c * CHUNK:(c + 1) * CHUNK, p * LANES:(p + 1) * LANES]

    units = [(c, p) for c in range(n_chunks) for p in range(4)]
    bf = lambda z: z.astype(BF16)
    ar = {u: bf(jnp.concatenate([sub(a_t, u), sub(r_t, u)], axis=0)) for u in units}
    g_b = {u: _dot_nt(ar[u], stack(sub(b_t, u))) for u in units}
    g_k = {u: _dot_nt(ar[u], stack(sub(k_t, u))) for u in units}
    a_rb = {u: bf(jnp.where(incl, g_b[u][CHUNK:], 0.0)) for u in units}
    a_rk = {u: bf(jnp.where(incl, g_k[u][CHUNK:], 0.0)) for u in units}
    a_ak = {u: bf(jnp.where(strict, g_k[u][:CHUNK], 0.0)) for u in units}
    pw = {u: stack(jnp.where(strict, g_b[u][:CHUNK], 0.0)) for u in units}
    t_bd = {u: eye + pw[u] for u in units}
    for _ in range(CHUNK.bit_length() - 2):
        pw = {u: _dot(pw[u], pw[u]) for u in units}
        t_bd = {u: t_bd[u] + _dot(t_bd[u], pw[u]) for u in units}
    t_bd = {u: bf(t_bd[u]) for u in units}
    v_bd = {u: bf(stack(sub(v, u))) for u in units}
    w_bd = {u: bf(_dot(t_bd[u], stack(sub(a_t, u)))) for u in units}
    av = {u: _dot(a_ak[u], v_bd[u]) for u in units}
    uv_bd = {u: _dot(t_bd[u], stack(av[u])) for u in units}
    y_v = {u: _dot(a_rk[u], v_bd[u]) for u in units}

    h = [state_ref[p] for p in range(4)]
    for c in range(n_chunks):
        last = c * CHUNK + CHUNK - 1
        p_last = jnp.exp(cum[last:last + 1, :])
        hb = [bf(h[p]) for p in range(4)]
        u2 = [_dot_nt(w_bd[(c, p)], hb[p]) + uv_bd[(c, p)] for p in range(4)]
        u_p = [u2[p][:CHUNK] + u2[p][CHUNK:] for p in range(4)]
        upd = [_dot_tn(jnp.concatenate([u_p[p], sub(v, (c, p))], axis=0),
                       jnp.concatenate([sub(b_t, (c, p)), sub(k_t, (c, p))], axis=0)
                       * p_last[:, p * LANES:(p + 1) * LANES]) for p in range(4)]
        for p in range(4):
            y_ref[c * CHUNK:(c + 1) * CHUNK, p * LANES:(p + 1) * LANES] = (
                _dot_nt(sub(r_t, (c, p)), hb[p]) + _dot(a_rb[(c, p)], stack(u_p[p])) + y_v[(c, p)])
        h = [h[p] * p_last[:, p * LANES:(p + 1) * LANES] + jnp.where(bd_mask, upd[p], 0.0) for p in range(4)]
    for p in range(4):
        state_ref[p] = h[p]


def _rwkv_kernel(has_vres, tb, *refs):
    if has_vres:
        (za_ref, vfirst_ref, mu_ref, dup_ref, w0_ref, aup_ref, a0_ref, gup_ref, vup_ref, v0_ref,
         kk_ref, ka_ref, rk_ref, lnw_ref, lnb_ref, o_ref, prev_ref, state_ref, y_ref) = refs
    else:
        (za_ref, mu_ref, dup_ref, w0_ref, aup_ref, a0_ref, gup_ref,
         kk_ref, ka_ref, rk_ref, lnw_ref, lnb_ref, o_ref, vraw_ref, prev_ref, state_ref, y_ref) = refs

    @pl.when(pl.program_id(1) == 0)
    def _():
        prev_ref[...] = jnp.zeros_like(prev_ref)
        state_ref[...] = jnp.zeros_like(state_ref)

    za = za_ref[...]
    shifted = pltpu.roll(za, 1, axis=0)
    shifted = jnp.where(_iota2(za.shape, 0) == 0, prev_ref[7:8, :], shifted)
    prev_ref[...] = za[tb - 8:, :]
    za = za + (shifted - za) * mu_ref[...]

    r = za[:, 0:512]
    k = za[:, 512:1024]
    v = za[:, 1024:1536]
    xwa = za[:, 1536:1664]
    xg = za[:, 1664:1920]
    w_log = -_softplus(-(w0_ref[...] + _dot_f32(jnp.tanh(xwa), dup_ref[...]))) - 0.5
    ld = -jnp.exp(w_log)
    a = _sigmoid(a0_ref[...] + _dot_f32(xwa, aup_ref[...]))
    g = _dot(_sigmoid(xg), gup_ref[...])
    if has_vres:
        v_mix = _sigmoid(v0_ref[...] + _dot_f32(xg, vup_ref[...]))
        v = v + (vfirst_ref[...] - v) * v_mix
    else:
        vraw_ref[...] = v
    kk = k * kk_ref[...]
    k = k * (1.0 + (a - 1.0) * ka_ref[...])
    head_sum = _group_mean_matrix(512, CHUNK) * float(CHUNK)
    kk = kk / jnp.maximum(jnp.sqrt(_dot_split(kk * kk, head_sum)), 1e-12)

    _rwkv_block(r, k, v, kk, a, ld, tb, state_ref, y_ref)
    y = y_ref[...]
    head_mean = _group_mean_matrix(512, CHUNK)
    mean = _dot_split(y, head_mean)
    yc = y - mean
    var = _dot_split(yc * yc, head_mean)
    y = yc * lax.rsqrt(var + RWKV_LN_EPS) * lnw_ref[...] + lnb_ref[...]
    y = y + _dot_split(r * k * rk_ref[...], head_sum) * v
    o_ref[...] = (y * g).astype(o_ref.dtype)


def _rwkv(za, v_first, prm, tb=256):
    bsz, seq, _ = za.shape
    has_vres = v_first is not None
    blk = lambda n: pl.BlockSpec((None, tb, n), lambda b, i: (b, i, 0))
    names = (["mu", "dup", "w0", "aup", "a0", "gup"] + (["vup", "v0"] if has_vres else [])
             + ["kk", "ka", "rk", "lnw", "lnb"])
    weights = [prm[n] for n in names]
    ins = [za] + ([v_first] if has_vres else []) + weights
    in_specs = [blk(RWKV_PAD)] + ([blk(512)] if has_vres else []) + [_resident(w.shape) for w in weights]
    out_shape = [jax.ShapeDtypeStruct((bsz, seq, 512), BF16)]
    out_specs = [blk(512)]
    if not has_vres:
        out_shape.append(jax.ShapeDtypeStruct((bsz, seq, 512), F32))
        out_specs.append(blk(512))
    res = pl.pallas_call(
        functools.partial(_rwkv_kernel, has_vres, tb),
        grid=(bsz, seq // tb),
        in_specs=in_specs,
        out_specs=out_specs,
        out_shape=out_shape,
        scratch_shapes=[pltpu.VMEM((8, RWKV_PAD), F32), pltpu.VMEM((4, LANES, LANES), F32),
                        pltpu.VMEM((tb, 512), F32)],
        compiler_params=_params("parallel", "arbitrary"),
        name="rwkv7",
    )(*ins)
    return (res[0], v_first) if has_vres else (res[0], res[1])


def _mamba_kernel(tb, zb_ref, cw_ref, cb_ref, dtb_ref, alog_ref, dexp_ref, nw_ref,
                  o_ref, xbuf_ref, state_ref, y_ref):
    @pl.when(pl.program_id(1) == 0)
    def _():
        xbuf_ref[0:8, :] = jnp.zeros((8, 1024), F32)
        state_ref[...] = jnp.zeros_like(state_ref)

    xbuf_ref[8:8 + tb, :] = zb_ref[:, 512:1536]
    conv = cb_ref[...]
    for i in range(4):
        conv = conv + cw_ref[i:i + 1, :] * xbuf_ref[5 + i:5 + i + tb, :]
    xbuf_ref[0:8, :] = xbuf_ref[tb:tb + 8, :]
    xbc = _silu(conv)
    xs = xbc[:, 0:512]
    bm = xbc[:, 512:768]
    cm = xbc[:, 768:1024]

    lane = _iota2((1, LANES), 1)
    dt = _softplus(zb_ref[:, 1536:1664] + dtb_ref[...])
    a_neg = jnp.where(lane < SSM_HEADS, -jnp.exp(alog_ref[...]), 0.0)
    acs = _dot_split_l(_chunk_tri(tb), dt * a_neg)
    expand = jnp.where(_iota2((LANES, 512), 0) == _iota2((LANES, 512), 1) // CHUNK, 1.0, 0.0)
    dt_e = _dot_split(dt, expand)
    acs_e = _dot_split(acs, expand)
    xdt = xs * dt_e

    causal = _iota2((CHUNK, CHUNK), 0) >= _iota2((CHUNK, CHUNK), 1)
    lane_head = _iota2((CHUNK, 512), 1) // CHUNK
    for c in range(tb // CHUNK):
        rows = slice(c * CHUNK, (c + 1) * CHUNK)
        acs_c = acs[rows]
        acs_t = acs_c.T
        acs_ec = acs_e[rows]
        last_e = acs_ec[CHUNK - 1:CHUNK, :]
        xdt_c = xdt[rows]
        y = jnp.zeros((CHUNK, 512), F32)
        for g in range(2):
            gs = slice(g * SSM_STATE, (g + 1) * SSM_STATE)
            cb = _dot_nt(cm[rows, gs], bm[rows, gs])
            for e in range(4):
                h = g * 4 + e
                seg = acs_c[:, h:h + 1] - acs_t[h:h + 1, :]
                m = cb * jnp.exp(jnp.where(causal, seg, -jnp.inf))
                y = y + _dot(m, jnp.where(lane_head == h, xdt_c, 0.0))
        x_out = xdt_c * jnp.exp(last_e - acs_ec)
        e_in = jnp.exp(acs_ec)
        e_last = jnp.exp(last_e)
        for g in range(2):
            gs = slice(g * SSM_STATE, (g + 1) * SSM_STATE)
            ls = slice(g * 256, (g + 1) * 256)
            s_g = state_ref[g]
            y_ref[rows, ls] = y[:, ls] + _dot(cm[rows, gs], s_g) * e_in[:, ls]
            state_ref[g] = s_g * e_last[:, ls] + _dot_tn(bm[rows, gs], x_out[:, ls])

    y = (y_ref[...] + xs * dexp_ref[...]) * _silu(zb_ref[:, 0:512])
    for g in range(2):
        ls = slice(g * 256, (g + 1) * 256)
        yg = y[:, ls]
        ms = jnp.mean(yg * yg, axis=-1, keepdims=True)
        o_ref[:, ls] = (yg * lax.rsqrt(ms + NORM_EPS) * nw_ref[:, ls]).astype(o_ref.dtype)


def _mamba(zb, prm, tb=128):
    bsz, seq, _ = zb.shape
    blk = lambda n: pl.BlockSpec((None, tb, n), lambda b, i: (b, i, 0))
    weights = [prm[n] for n in ("cw", "cb", "dtb", "alog", "dexp", "nw")]
    return pl.pallas_call(
        functools.partial(_mamba_kernel, tb),
        grid=(bsz, seq // tb),
        in_specs=[blk(SSM_PAD)] + [_resident(w.shape) for w in weights],
        out_specs=blk(512),
        out_shape=jax.ShapeDtypeStruct((bsz, seq, 512), BF16),
        scratch_shapes=[pltpu.VMEM((tb + 8, 1024), F32), pltpu.VMEM((2, SSM_STATE, 256), F32),
                        pltpu.VMEM((tb, 512), F32)],
        compiler_params=_params("parallel", "arbitrary"),
        name="mamba2",
    )(zb, *weights)


def _attn_kernel(tq, seq, q_ref, k_ref, v_ref, qg_ref, kg_ref, bias_ref, o_ref, kn_ref, vb_ref):
    i = pl.program_id(1)
    head_mean = _group_mean_matrix(512, CHUNK)
    win = tq + ATT_LEFT

    @pl.when(i == 0)
    def _():
        kn_ref[0:ATT_LEFT, :] = jnp.zeros((ATT_LEFT, 512), BF16)
        vb_ref[0:ATT_LEFT, :] = jnp.zeros((ATT_LEFT, 512), BF16)
        for j in range(seq // tq):
            kj = k_ref[j * tq:(j + 1) * tq, :]
            ms = _dot_split(kj * kj, head_mean)
            kn_ref[ATT_LEFT + j * tq:ATT_LEFT + (j + 1) * tq, :] = (
                kj * lax.rsqrt(ms + NORM_EPS) * kg_ref[...]).astype(BF16)
            vb_ref[ATT_LEFT + j * tq:ATT_LEFT + (j + 1) * tq, :] = v_ref[j * tq:(j + 1) * tq, :].astype(BF16)

    q = q_ref[...]
    ms = _dot_split(q * q, head_mean)
    qn = q * lax.rsqrt(ms + NORM_EPS) * qg_ref[...] * (CHUNK ** -0.5)
    start = pl.multiple_of(i * tq, tq)
    kwin = kn_ref[pl.ds(start, win), :]
    vwin = vb_ref[pl.ds(start, win), :]
    qc = ATT_LEFT // CHUNK + _iota2((tq, win), 0) // CHUNK
    kc = _iota2((tq, win), 1) // CHUNK
    first_kc = jnp.maximum(qc - ATT_LEFT // CHUNK, ATT_LEFT // CHUNK - i * (tq // CHUNK))
    valid = (kc <= qc) & (kc >= first_kc)
    lane_head = _iota2((tq, LANES), 1) // CHUNK
    for p in range(4):
        ls = slice(p * LANES, (p + 1) * LANES)
        acc = jnp.zeros((tq, LANES), F32)
        for s in range(2):
            qh = jnp.where(lane_head == s, qn[:, ls], 0.0)
            sc = _dot_nt(qh, kwin[:, ls]) + bias_ref[2 * p + s]
            sc = jnp.where(valid, sc, -jnp.inf)
            e = jnp.exp(sc - jnp.max(sc, axis=-1, keepdims=True))
            pr = e / jnp.sum(e, axis=-1, keepdims=True)
            acc = jnp.where(lane_head == s, _dot(pr, vwin[:, ls]), acc)
        o_ref[:, ls] = acc.astype(o_ref.dtype)


def _attention(zc, q_gain, k_gain, rel_bias, tq=256):
    bsz, seq, _ = zc.shape
    win = tq + ATT_LEFT
    rel = jnp.clip(jnp.arange(win)[None, :] - ATT_LEFT - jnp.arange(tq)[:, None], -REL_CLIP, REL_CLIP) + REL_CLIP
    bias = rel_bias.astype(F32)[:, rel]
    qg = jnp.tile(q_gain, ATT_HEADS)[None, :]
    kg = jnp.tile(k_gain, ATT_HEADS)[None, :]
    full = lambda col: pl.BlockSpec((None, seq, 512), lambda b, i: (b, 0, col))
    return pl.pallas_call(
        functools.partial(_attn_kernel, tq, seq),
        grid=(bsz, seq // tq),
        in_specs=[pl.BlockSpec((None, tq, 512), lambda b, i: (b, i, 0)), full(1), full(2),
                  _resident(qg.shape), _resident(kg.shape), _resident(bias.shape)],
        out_specs=pl.BlockSpec((None, tq, 512), lambda b, i: (b, i, 0)),
        out_shape=jax.ShapeDtypeStruct((bsz, seq, 512), BF16),
        scratch_shapes=[pltpu.VMEM((ATT_LEFT + seq, 512), BF16), pltpu.VMEM((ATT_LEFT + seq, 512), BF16)],
        compiler_params=_params("parallel", "arbitrary"),
        name="band_attention",
    )(zc, zc, zc, qg, kg, bias)


def _gla_kernel(tb, zd_ref, gup_ref, gb_ref, nw_ref, o_ref, state_ref, y_ref):
    @pl.when(pl.program_id(1) == 0)
    def _():
        state_ref[...] = jnp.zeros_like(state_ref)

    q = zd_ref[:, 0:256] * (CHUNK ** -0.5)
    k = zd_ref[:, 256:512]
    v = zd_ref[:, 512:1024]
    log_a = -_softplus(-(_dot_f32(zd_ref[:, 1536:1664], gup_ref[...]) + gb_ref[...])) / GLA_GATE_NORM
    bcum = _dot_split_l(_chunk_tri(tb), log_a)

    k_head = _iota2((CHUNK, 256), 1) // CHUNK
    v_head = _iota2((CHUNK, 512), 1) // LANES
    causal = _iota2((CHUNK, 256), 0) >= _iota2((CHUNK, 256), 1) % CHUNK
    bd = (_iota2((512, 256), 0) // LANES) == (_iota2((512, 256), 1) // CHUNK)
    for c in range(tb // CHUNK):
        rows = slice(c * CHUNK, (c + 1) * CHUNK)
        bc = bcum[rows]
        blast = bc[CHUNK - 1:CHUNK, :]
        qg = q[rows] * jnp.exp(bc)
        kg = k[rows] * jnp.exp(-bc)
        kd = k[rows] * jnp.exp(blast - bc)
        v_c = v[rows]
        kg_bd = jnp.concatenate([jnp.where(k_head == h, kg, 0.0) for h in range(4)], axis=0)
        v_bd = jnp.concatenate([jnp.where(v_head == h, v_c, 0.0) for h in range(4)], axis=0)
        att = jnp.where(causal, _dot_nt(qg, kg_bd), 0.0)
        st = state_ref[...]
        y_ref[rows, :] = _dot(att, v_bd) + _dot_nt(qg, st)
        state_ref[...] = st * jnp.exp(blast) + jnp.where(bd, _dot_tn(v_c, kd), 0.0)

    o = y_ref[...]
    for h in range(4):
        ls = slice(h * LANES, (h + 1) * LANES)
        oh = o[:, ls]
        ms = jnp.mean(oh * oh, axis=-1, keepdims=True)
        o_ref[:, ls] = (oh * lax.rsqrt(ms + NORM_EPS) * nw_ref[...]
                        * _silu(zd_ref[:, 1024 + h * LANES:1024 + (h + 1) * LANES])).astype(o_ref.dtype)


def _gla(zd, prm, tb=128):
    bsz, seq, _ = zd.shape
    blk = lambda n: pl.BlockSpec((None, tb, n), lambda b, i: (b, i, 0))
    weights = [prm[n] for n in ("gup", "gb", "nw")]
    return pl.pallas_call(
        functools.partial(_gla_kernel, tb),
        grid=(bsz, seq // tb),
        in_specs=[blk(GLA_PAD)] + [_resident(w.shape) for w in weights],
        out_specs=blk(512),
        out_shape=jax.ShapeDtypeStruct((bsz, seq, 512), BF16),
        scratch_shapes=[pltpu.VMEM((512, 256), F32), pltpu.VMEM((tb, 512), F32)],
        compiler_params=_params("parallel", "arbitrary"),
        name="gla",
    )(zd, *weights)


def _merge_kernel(x_ref, h_ref, oa_ref, ob_ref, oc_ref, od_ref, wg_ref, wb_ref, wo_ref, nf_ref,
                  x1_ref, hf_ref):
    h = h_ref[...]
    acc = jnp.zeros(x_ref.shape, F32)
    for i, o_ref in enumerate((oa_ref, ob_ref, oc_ref, od_ref)):
        gate = _sigmoid(jnp.dot(h, wg_ref[:, i * D_MODEL:(i + 1) * D_MODEL], preferred_element_type=F32))
        acc = acc + gate * jnp.dot(o_ref[...], wb_ref[i], preferred_element_type=F32)
    x1 = x_ref[...] + jnp.dot(acc.astype(BF16), wo_ref[...], preferred_element_type=F32)
    x1_ref[...] = x1
    ms = jnp.mean(x1 * x1, axis=-1, keepdims=True)
    hf_ref[...] = (x1 * lax.rsqrt(ms + NORM_EPS) * nf_ref[...]).astype(BF16)


def _merge(x2, h, outs, wg, wb, wo, nf, tm=256):
    n_tok = x2.shape[0]
    row = lambda n: pl.BlockSpec((tm, n), lambda i: (i, 0))
    return pl.pallas_call(
        _merge_kernel,
        grid=(n_tok // tm,),
        in_specs=[row(D_MODEL), row(D_MODEL)] + [row(BRANCH_DIM)] * 4
        + [_resident(wg.shape), _resident(wb.shape), _resident(wo.shape), _resident(nf.shape)],
        out_specs=[row(D_MODEL), row(D_MODEL)],
        out_shape=[jax.ShapeDtypeStruct((n_tok, D_MODEL), F32), jax.ShapeDtypeStruct((n_tok, D_MODEL), BF16)],
        compiler_params=_params("parallel"),
        name="merge",
    )(x2, h, *outs, wg, wb, wo, nf)


def _ffn_kernel(tf, x1_ref, hf_ref, w1_ref, w3_ref, w2_ref, o_ref):
    hf = hf_ref[...]
    acc = x1_ref[...]
    for lo in range(0, FFN_DIM, tf):
        cols = slice(lo, min(lo + tf, FFN_DIM))
        a = jnp.dot(hf, w1_ref[:, cols], preferred_element_type=F32)
        b = jnp.dot(hf, w3_ref[:, cols], preferred_element_type=F32)
        acc = acc + jnp.dot((_silu(a) * b).astype(BF16), w2_ref[cols, :], preferred_element_type=F32)
    o_ref[...] = acc


def _ffn(x1, hf, w1, w3, w2, tm=512, tf=512):
    n_tok = x1.shape[0]
    row = pl.BlockSpec((tm, D_MODEL), lambda i: (i, 0))
    return pl.pallas_call(
        functools.partial(_ffn_kernel, tf),
        grid=(n_tok // tm,),
        in_specs=[row, row, _resident(w1.shape), _resident(w3.shape), _resident(w2.shape)],
        out_specs=row,
        out_shape=jax.ShapeDtypeStruct((n_tok, D_MODEL), F32),
        compiler_params=_params("parallel"),
        name="ffn",
    )(x1, hf, w1, w3, w2)


def _router_kernel(tm, x1_ref, nf_ref, wr_ref, idx_ref, gate_ref, rank_ref, cnt_ref, carry_ref):
    @pl.when(pl.program_id(0) == 0)
    def _():
        carry_ref[...] = jnp.zeros_like(carry_ref)

    x1 = x1_ref[...]
    ms = jnp.mean(x1 * x1, axis=-1, keepdims=True)
    hf = x1 * lax.rsqrt(ms + NORM_EPS) * nf_ref[...]
    logits = lax.dot_general(wr_ref[...], hf, (((1,), (1,)), ((), ())), precision=HIGHEST,
                             preferred_element_type=F32)
    e_iota = _iota2((N_EXPERTS, tm), 0)
    m1 = jnp.max(logits, axis=0, keepdims=True)
    i1 = jnp.min(jnp.where(logits == m1, e_iota, N_EXPERTS), axis=0, keepdims=True)
    rest = jnp.where(e_iota == i1, -jnp.inf, logits)
    m2 = jnp.max(rest, axis=0, keepdims=True)
    i2 = jnp.min(jnp.where(rest == m2, e_iota, N_EXPERTS), axis=0, keepdims=True)
    e2 = jnp.exp(m2 - m1)
    gate_ref[0:1, :] = 1.0 / (1.0 + e2)
    gate_ref[1:2, :] = e2 / (1.0 + e2)
    idx_ref[0:1, :] = i1
    idx_ref[1:2, :] = i2
    hit1 = jnp.where(e_iota == i1, 1.0, 0.0)
    hit2 = jnp.where(e_iota == i2, 1.0, 0.0)
    before = jnp.where(_iota2((tm, tm), 0) < _iota2((tm, tm), 1), 1.0, 0.0)
    prior = _dot(hit1 + hit2, before) + carry_ref[:, 0:1]
    rank_ref[0:1, :] = jnp.sum(hit1 * prior, axis=0, keepdims=True).astype(jnp.int32)
    rank_ref[1:2, :] = jnp.sum(hit2 * prior, axis=0, keepdims=True).astype(jnp.int32)
    carry_ref[...] = carry_ref[...] + jnp.sum(hit1 + hit2, axis=1, keepdims=True)
    cnt_ref[...] = carry_ref[...]


def _router(x1, nf, wr_t, tm=256):
    n_tok = x1.shape[0]
    col = pl.BlockSpec((2, tm), lambda i: (0, i))
    return pl.pallas_call(
        functools.partial(_router_kernel, tm),
        grid=(n_tok // tm,),
        in_specs=[pl.BlockSpec((tm, D_MODEL), lambda i: (i, 0)), _resident(nf.shape), _resident(wr_t.shape)],
        out_specs=[col, col, col, pl.BlockSpec((N_EXPERTS, LANES), lambda i: (0, 0))],
        out_shape=[jax.ShapeDtypeStruct((2, n_tok), jnp.int32), jax.ShapeDtypeStruct((2, n_tok), F32),
                   jax.ShapeDtypeStruct((2, n_tok), jnp.int32), jax.ShapeDtypeStruct((N_EXPERTS, LANES), F32)],
        scratch_shapes=[pltpu.VMEM((N_EXPERTS, LANES), F32)],
        compiler_params=_params("arbitrary"),
        name="moe_router",
    )(x1, nf, wr_t)


def _expert_up_kernel(ge_ref, nv_ref, x_ref, w1_ref, w3_ref, act_ref):
    g = pl.program_id(1)

    @pl.when(g < nv_ref[0])
    def _():
        x = x_ref[...]
        a = jnp.dot(x, w1_ref[...], preferred_element_type=F32)
        b = jnp.dot(x, w3_ref[...], preferred_element_type=F32)
        act_ref[...] = (_silu(a) * b).astype(act_ref.dtype)

    @pl.when(g >= nv_ref[0])
    def _():
        act_ref[...] = jnp.zeros_like(act_ref)


def _expert_down_kernel(ge_ref, nv_ref, act_ref, w2_ref, y_ref):
    g = pl.program_id(0)

    @pl.when(g < nv_ref[0])
    def _():
        y_ref[...] = jnp.dot(act_ref[...], w2_ref[...], preferred_element_type=F32)

    @pl.when(g >= nv_ref[0])
    def _():
        y_ref[...] = jnp.zeros_like(y_ref)


def _experts(xg, group_expert, n_valid, w1, w3, w2, tf=512):
    n_rows = xg.shape[0]
    n_groups = n_rows // MOE_ROWS
    act = pl.pallas_call(
        _expert_up_kernel,
        grid_spec=pltpu.PrefetchScalarGridSpec(
            num_scalar_prefetch=2,
            grid=(EXPERT_DIM // tf, n_groups),
            in_specs=[pl.BlockSpec((MOE_ROWS, D_MODEL), lambda f, g, ge, nv: (g, 0)),
                      pl.BlockSpec((None, D_MODEL, tf), lambda f, g, ge, nv: (ge[g], 0, f)),
                      pl.BlockSpec((None, D_MODEL, tf), lambda f, g, ge, nv: (ge[g], 0, f))],
            out_specs=pl.BlockSpec((MOE_ROWS, tf), lambda f, g, ge, nv: (g, f)),
        ),
        out_shape=jax.ShapeDtypeStruct((n_rows, EXPERT_DIM), BF16),
        compiler_params=_params("parallel", "arbitrary"),
        name="expert_up",
    )(group_expert, n_valid, xg, w1, w3)
    return pl.pallas_call(
        _expert_down_kernel,
        grid_spec=pltpu.PrefetchScalarGridSpec(
            num_scalar_prefetch=2,
            grid=(n_groups,),
            in_specs=[pl.BlockSpec((MOE_ROWS, EXPERT_DIM), lambda g, ge, nv: (g, 0)),
                      pl.BlockSpec((None, EXPERT_DIM, D_MODEL), lambda g, ge, nv: (ge[g], 0, 0))],
            out_specs=pl.BlockSpec((MOE_ROWS, D_MODEL), lambda g, ge, nv: (g, 0)),
        ),
        out_shape=jax.ShapeDtypeStruct((n_rows, D_MODEL), F32),
        compiler_params=_params("arbitrary"),
        name="expert_down",
    )(group_expert, n_valid, act, w2)


def _moe(x1, hf, nf, w_router, w1, w3, w2):
    n_tok = x1.shape[0]
    idx, gate, rank, cnt = _router(x1, nf, w_router.T)
    counts = cnt[:, 0].astype(jnp.int32)
    padded = (counts + MOE_ROWS - 1) // MOE_ROWS * MOE_ROWS
    end_padded = jnp.cumsum(padded)
    start_padded = end_padded - padded
    dest = start_padded[idx] + rank
    n_groups = (n_tok * 2 + MOE_ROWS - 1) // MOE_ROWS + N_EXPERTS
    n_rows = n_groups * MOE_ROWS
    group_expert = jnp.minimum(
        jnp.searchsorted(end_padded, jnp.arange(n_groups) * MOE_ROWS, side="right"), N_EXPERTS - 1).astype(jnp.int32)
    n_valid = (end_padded[-1:] // MOE_ROWS).astype(jnp.int32)
    tok = jnp.broadcast_to(jnp.arange(n_tok, dtype=jnp.int32)[None, :], (2, n_tok))
    row_token = jnp.full((n_rows,), n_tok, jnp.int32).at[dest.reshape(-1)].set(tok.reshape(-1))
    hf_pad = jnp.concatenate([hf, jnp.zeros((1, D_MODEL), hf.dtype)], axis=0)
    xg = hf_pad[row_token]
    y = _experts(xg, group_expert, n_valid, w1, w3, w2)
    return x1 + gate[0][:, None] * y[dest[0]] + gate[1][:, None] * y[dest[1]]


def _row(v, pad=0):
    v = v.reshape(1, -1).astype(F32)
    return jnp.pad(v, ((0, 0), (0, pad))) if pad else v


def _rows_at(w, start, total):
    return jnp.pad(w.astype(F32), ((start, total - start - w.shape[0]), (0, 0)))


def kernel(x, w_in, norm_mix, rwkv_mu, rwkv_decay_up, rwkv_w0, rwkv_a_up, rwkv_a0, rwkv_gate_up, rwkv_k_k, rwkv_k_a, rwkv_r_k, rwkv_ln_w, rwkv_ln_b, vres_down, vres_up, vres_v0, ssm_conv_w, ssm_conv_b, ssm_dt_bias, ssm_a_log, ssm_d, ssm_norm_w, att_q_gain, att_k_gain, att_rel_bias, gla_gate_up, gla_gate_bias, gla_norm_w, w_branch, w_out, norm_ffn, ffn_w1, ffn_w3, ffn_w2, moe_router, moe_w1, moe_w3, moe_w2):
    bsz, seq, _ = x.shape
    n_tok = bsz * seq
    depth = w_in.shape[0]
    x2 = x.reshape(n_tok, D_MODEL)
    v_first = None
    o_rw, o_ss, o_at = RWKV_COLS, RWKV_COLS + SSM_COLS, RWKV_COLS + SSM_COLS + ATT_COLS
    o_gl = o_at + GLA_COLS
    for l in range(depth):
        w = w_in[l]
        vd = vres_down[l - 1] if l > 0 else jnp.zeros((D_MODEL, 32), F32)
        wa = jnp.concatenate([w[:, :o_rw], vd, jnp.zeros((D_MODEL, RWKV_PAD - RWKV_COLS - 32), F32)], axis=1)
        wb = jnp.pad(w[:, o_rw:o_ss], ((0, 0), (0, SSM_PAD - SSM_COLS)))
        wc = w[:, o_ss:o_at]
        gl = w[:, o_at:o_gl]
        wd = jnp.concatenate([gl[:, 0:1024], gl[:, 1040:1552], gl[:, 1024:1040],
                              jnp.zeros((D_MODEL, GLA_PAD - GLA_COLS), F32)], axis=1)
        wg = w[:, o_gl:]
        h, za, zb, zc, zd = _inproj(x2, _row(norm_mix[l]), wa.astype(BF16), wb.astype(BF16),
                                    wc.astype(BF16), wd.astype(BF16))
        shp = lambda z: z.reshape(bsz, seq, z.shape[-1])

        rw = dict(mu=_row(rwkv_mu[l], RWKV_PAD - RWKV_COLS),
                  dup=_rows_at(rwkv_decay_up[l], 0, LANES), w0=_row(rwkv_w0[l]),
                  aup=_rows_at(rwkv_a_up[l], 64, LANES), a0=_row(rwkv_a0[l]),
                  gup=_rows_at(rwkv_gate_up[l], 0, 256).astype(BF16),
                  kk=_row(rwkv_k_k[l]), ka=_row(rwkv_k_a[l]), rk=_row(rwkv_r_k[l]),
                  lnw=_row(rwkv_ln_w[l]), lnb=_row(rwkv_ln_b[l]))
        if l > 0:
            rw.update(vup=_rows_at(vres_up[l - 1], 160, 256), v0=_row(vres_v0[l - 1]))
        o_a, v_first = _rwkv(shp(za), v_first, rw)
        o_b = _mamba(shp(zb), dict(cw=ssm_conv_w[l], cb=_row(ssm_conv_b[l]),
                                   dtb=_row(ssm_dt_bias[l], LANES - SSM_HEADS),
                                   alog=_row(ssm_a_log[l], LANES - SSM_HEADS),
                                   dexp=_row(jnp.repeat(ssm_d[l], CHUNK)), nw=_row(ssm_norm_w[l])))
        o_c = _attention(shp(zc), att_q_gain[l], att_k_gain[l], att_rel_bias)
        o_d = _gla(shp(zd), dict(gup=_rows_at(gla_gate_up[l], 0, LANES), gb=_row(gla_gate_bias[l]),
                                 nw=_row(gla_norm_w[l])))
        outs = [o.reshape(n_tok, BRANCH_DIM) for o in (o_a, o_b, o_c, o_d)]
        x1, hf = _merge(x2, h, outs, wg.astype(BF16), w_branch[l].astype(BF16), w_out[l].astype(BF16),
                        _row(norm_ffn[l]))
        if l % 2 == 0:
            x2 = _ffn(x1, hf, ffn_w1[l // 2].astype(BF16), ffn_w3[l // 2].astype(BF16),
                      ffn_w2[l // 2].astype(BF16))
        else:
            x2 = _moe(x1, hf, _row(norm_ffn[l]), moe_router[l // 2], moe_w1[l // 2].astype(BF16),
                      moe_w3[l // 2].astype(BF16), moe_w2[l // 2].astype(BF16))
    return x2.reshape(bsz, seq, D_MODEL)
```

```python
import functools

import jax
import jax.numpy as jnp
from jax import lax
from jax.experimental import pallas as pl
from jax.experimental.pallas import tpu as pltpu
from jax.experimental.pallas import tpu_sc as plsc

F32 = jnp.float32
BF16 = jnp.bfloat16
HIGHEST = lax.Precision.HIGHEST

D_MODEL = 1024
CHUNK = 64
BRANCH_DIM = 512
NORM_EPS = 1e-6
LANES = 128
VMEM_LIMIT = 56 * 1024 * 1024
EXPERT_VMEM_LIMIT = 60 * 1024 * 1024

RWKV_LN_EPS = 64e-5
RWKV_COLS = 1824
RWKV_PAD = 1920
SSM_COLS = 1544
SSM_PAD = 1664
SSM_HEADS = 8
SSM_STATE = 128
ATT_COLS = 1536
ATT_HEADS = 8
ATT_LEFT = 8 * CHUNK
REL_CLIP = 2 * CHUNK
GLA_COLS = 1552
GLA_PAD = 1664
GLA_GATE_NORM = 16.0
FFN_DIM = 2816
N_EXPERTS = 8
EXPERT_DIM = 3584
MOE_ROWS = 256


def _dot(a, b):
    return jnp.dot(a.astype(BF16), b.astype(BF16), preferred_element_type=F32)


def _dot_nt(a, b):
    return lax.dot_general(a.astype(BF16), b.astype(BF16), (((1,), (1,)), ((), ())),
                           preferred_element_type=F32)


def _dot_tn(a, b):
    return lax.dot_general(a.astype(BF16), b.astype(BF16), (((0,), (0,)), ((), ())),
                           preferred_element_type=F32)


def _dot_f32(a, b):
    return jnp.dot(a, b, precision=HIGHEST, preferred_element_type=F32)


def _dot_split(a, m):
    m = m.astype(BF16)
    hi = a.astype(BF16)
    r1 = a - hi.astype(F32)
    mid = r1.astype(BF16)
    lo = (r1 - mid.astype(F32)).astype(BF16)
    return (jnp.dot(hi, m, preferred_element_type=F32) + jnp.dot(mid, m, preferred_element_type=F32)
            + jnp.dot(lo, m, preferred_element_type=F32))


def _dot_split_l(m, a):
    m = m.astype(BF16)
    hi = a.astype(BF16)
    r1 = a - hi.astype(F32)
    mid = r1.astype(BF16)
    lo = (r1 - mid.astype(F32)).astype(BF16)
    return (jnp.dot(m, hi, preferred_element_type=F32) + jnp.dot(m, mid, preferred_element_type=F32)
            + jnp.dot(m, lo, preferred_element_type=F32))


def _softplus(x):
    return jnp.maximum(x, 0.0) + jnp.log(1.0 + jnp.exp(-jnp.abs(x)))


def _sigmoid(x):
    return 1.0 / (1.0 + jnp.exp(-x))


def _silu(x):
    return x * _sigmoid(x)


def _iota2(shape, axis):
    return lax.broadcasted_iota(jnp.int32, shape, axis)


def _group_mean_matrix(n, group):
    r = _iota2((n, n), 0) // group
    c = _iota2((n, n), 1) // group
    return jnp.where(r == c, 1.0 / group, 0.0).astype(F32)


def _chunk_tri(n):
    r = _iota2((n, n), 0)
    c = _iota2((n, n), 1)
    return jnp.where((r // CHUNK == c // CHUNK) & (r >= c), 1.0, 0.0).astype(F32)


def _resident(shape):
    nd = len(shape)
    return pl.BlockSpec(shape, lambda *_: (0,) * nd, pipeline_mode=pl.Buffered(1))


def _params(*sem):
    return pltpu.CompilerParams(dimension_semantics=sem, vmem_limit_bytes=VMEM_LIMIT)


def _inproj_kernel(x_ref, g_ref, wa_ref, wb_ref, wc_ref, wd_ref,
                   h_ref, za_ref, zb_ref, zc_ref, zd_ref):
    x = x_ref[...]
    ms = jnp.mean(x * x, axis=-1, keepdims=True)
    h = (x * lax.rsqrt(ms + NORM_EPS) * g_ref[...]).astype(BF16)
    h_ref[...] = h
    za_ref[...] = jnp.dot(h, wa_ref[...], preferred_element_type=F32)
    zb_ref[...] = jnp.dot(h, wb_ref[...], preferred_element_type=F32)
    zc_ref[...] = jnp.dot(h, wc_ref[...], preferred_element_type=F32)
    zd_ref[...] = jnp.dot(h, wd_ref[...], preferred_element_type=F32)


def _inproj(x2, g, wa, wb, wc, wd, tm=256):
    n_tok = x2.shape[0]
    row = lambda n: pl.BlockSpec((tm, n), lambda i: (i, 0))
    outs = (D_MODEL, RWKV_PAD, SSM_PAD, ATT_COLS, GLA_PAD)
    return pl.pallas_call(
        _inproj_kernel,
        grid=(n_tok // tm,),
        in_specs=[row(D_MODEL), _resident((1, D_MODEL)), _resident(wa.shape), _resident(wb.shape),
                  _resident(wc.shape), _resident(wd.shape)],
        out_specs=[row(n) for n in outs],
        out_shape=[jax.ShapeDtypeStruct((n_tok, outs[0]), BF16)]
        + [jax.ShapeDtypeStruct((n_tok, n), F32) for n in outs[1:]],
        compiler_params=_params("parallel"),
        name="inproj",
    )(x2, g, wa, wb, wc, wd)


def _rwkv_block(r, k, v, kk, ka, ld, tb, state_ref, y_ref):
    n_chunks = tb // CHUNK
    cum = _dot_split_l(_chunk_tri(tb), ld)
    p_inv = jnp.exp(-cum)
    a_t = -kk * jnp.exp(cum - ld)
    b_t = kk * ka * p_inv
    k_t = k * p_inv
    r_t = r * jnp.exp(cum)

    lane_head = _iota2((CHUNK, LANES), 1) // CHUNK
    row = _iota2((CHUNK, LANES), 0)
    col = _iota2((CHUNK, LANES), 1) % CHUNK
    strict = row > col
    incl = row >= col
    bd_mask = (_iota2((LANES, LANES), 0) // CHUNK) == (_iota2((LANES, LANES), 1) // CHUNK)
    eye = jnp.where(_iota2((LANES, LANES), 0) == _iota2((LANES, LANES), 1), 1.0, 0.0).astype(F32)

    def stack(z):
        return jnp.concatenate([jnp.where(lane_head == 0, z, 0.0), jnp.where(lane_head == 1, z, 0.0)], axis=0)

    def sub(z, u):
        c, p = u
        return z[c * CHUNK:(c + 1) * CHUNK, p * LANES:(p + 1) * LANES]

    units = [(c, p) for c in range(n_chunks) for p in range(4)]
    bf = lambda z: z.astype(BF16)
    ar = {u: bf(jnp.concatenate([sub(a_t, u), sub(r_t, u)], axis=0)) for u in units}
    g_b = {u: _dot_nt(ar[u], stack(sub(b_t, u))) for u in units}
    g_k = {u: _dot_nt(ar[u], stack(sub(k_t, u))) for u in units}
    a_rb = {u: bf(jnp.where(incl, g_b[u][CHUNK:], 0.0)) for u in units}
    a_rk = {u: bf(jnp.where(incl, g_k[u][CHUNK:], 0.0)) for u in units}
    a_ak = {u: bf(jnp.where(strict, g_k[u][:CHUNK], 0.0)) for u in units}
    pw = {u: stack(jnp.where(strict, g_b[u][:CHUNK], 0.0)) for u in units}
    t_bd = {u: eye + pw[u] for u in units}
    for _ in range(CHUNK.bit_length() - 2):
        pw = {u: _dot(pw[u], pw[u]) for u in units}
        t_bd = {u: t_bd[u] + _dot(t_bd[u], pw[u]) for u in units}
    t_bd = {u: bf(t_bd[u]) for u in units}
    v_bd = {u: bf(stack(sub(v, u))) for u in units}
    w_bd = {u: bf(_dot(t_bd[u], stack(sub(a_t, u)))) for u in units}
    av = {u: _dot(a_ak[u], v_bd[u]) for u in units}
    uv_bd = {u: _dot(t_bd[u], stack(av[u])) for u in units}
    y_v = {u: _dot(a_rk[u], v_bd[u]) for u in units}

    h = [state_ref[p] for p in range(4)]
    for c in range(n_chunks):
        last = c * CHUNK + CHUNK - 1
        p_last = jnp.exp(cum[last:last + 1, :])
        hb = [bf(h[p]) for p in range(4)]
        u2 = [_dot_nt(w_bd[(c, p)], hb[p]) + uv_bd[(c, p)] for p in range(4)]
        u_p = [u2[p][:CHUNK] + u2[p][CHUNK:] for p in range(4)]
        upd = [_dot_tn(jnp.concatenate([u_p[p], sub(v, (c, p))], axis=0),
                       jnp.concatenate([sub(b_t, (c, p)), sub(k_t, (c, p))], axis=0)
                       * p_last[:, p * LANES:(p + 1) * LANES]) for p in range(4)]
        for p in range(4):
            y_ref[c * CHUNK:(c + 1) * CHUNK, p * LANES:(p + 1) * LANES] = (
                _dot_nt(sub(r_t, (c, p)), hb[p]) + _dot(a_rb[(c, p)], stack(u_p[p])) + y_v[(c, p)])
        h = [h[p] * p_last[:, p * LANES:(p + 1) * LANES] + jnp.where(bd_mask, upd[p], 0.0) for p in range(4)]
    for p in range(4):
        state_ref[p] = h[p]


def _rwkv_kernel(has_vres, tb, *refs):
    if has_vres:
        (za_ref, vfirst_ref, mu_ref, dup_ref, w0_ref, aup_ref, a0_ref, gup_ref, vup_ref, v0_ref,
         kk_ref, ka_ref, rk_ref, lnw_ref, lnb_ref, o_ref, prev_ref, state_ref, y_ref) = refs
    else:
        (za_ref, mu_ref, dup_ref, w0_ref, aup_ref, a0_ref, gup_ref,
         kk_ref, ka_ref, rk_ref, lnw_ref, lnb_ref, o_ref, vraw_ref, prev_ref, state_ref, y_ref) = refs

    @pl.when(pl.program_id(1) == 0)
    def _():
        prev_ref[...] = jnp.zeros_like(prev_ref)
        state_ref[...] = jnp.zeros_like(state_ref)

    za = za_ref[...]
    shifted = pltpu.roll(za, 1, axis=0)
    shifted = jnp.where(_iota2(za.shape, 0) == 0, prev_ref[7:8, :], shifted)
    prev_ref[...] = za[tb - 8:, :]
    za = za + (shifted - za) * mu_ref[...]

    r = za[:, 0:512]
    k = za[:, 512:1024]
    v = za[:, 1024:1536]
    xwa = za[:, 1536:1664]
    xg = za[:, 1664:1920]
    w_log = -_softplus(-(w0_ref[...] + _dot_f32(jnp.tanh(xwa), dup_ref[...]))) - 0.5
    ld = -jnp.exp(w_log)
    a = _sigmoid(a0_ref[...] + _dot_f32(xwa, aup_ref[...]))
    g = _dot(_sigmoid(xg), gup_ref[...])
    if has_vres:
        v_mix = _sigmoid(v0_ref[...] + _dot_f32(xg, vup_ref[...]))
        v = v + (vfirst_ref[...] - v) * v_mix
    else:
        vraw_ref[...] = v
    kk = k * kk_ref[...]
    k = k * (1.0 + (a - 1.0) * ka_ref[...])
    head_sum = _group_mean_matrix(512, CHUNK) * float(CHUNK)
    kk = kk / jnp.maximum(jnp.sqrt(_dot_split(kk * kk, head_sum)), 1e-12)

    _rwkv_block(r, k, v, kk, a, ld, tb, state_ref, y_ref)
    y = y_ref[...]
    head_mean = _group_mean_matrix(512, CHUNK)
    mean = _dot_split(y, head_mean)
    yc = y - mean
    var = _dot_split(yc * yc, head_mean)
    y = yc * lax.rsqrt(var + RWKV_LN_EPS) * lnw_ref[...] + lnb_ref[...]
    y = y + _dot_split(r * k * rk_ref[...], head_sum) * v
    o_ref[...] = (y * g).astype(o_ref.dtype)


def _rwkv(za, v_first, prm, tb=256):
    bsz, seq, _ = za.shape
    has_vres = v_first is not None
    blk = lambda n: pl.BlockSpec((None, tb, n), lambda b, i: (b, i, 0))
    names = (["mu", "dup", "w0", "aup", "a0", "gup"] + (["vup", "v0"] if has_vres else [])
             + ["kk", "ka", "rk", "lnw", "lnb"])
    weights = [prm[n] for n in names]
    ins = [za] + ([v_first] if has_vres else []) + weights
    in_specs = [blk(RWKV_PAD)] + ([blk(512)] if has_vres else []) + [_resident(w.shape) for w in weights]
    out_shape = [jax.ShapeDtypeStruct((bsz, seq, 512), BF16)]
    out_specs = [blk(512)]
    if not has_vres:
        out_shape.append(jax.ShapeDtypeStruct((bsz, seq, 512), F32))
        out_specs.append(blk(512))
    res = pl.pallas_call(
        functools.partial(_rwkv_kernel, has_vres, tb),
        grid=(bsz, seq // tb),
        in_specs=in_specs,
        out_specs=out_specs,
        out_shape=out_shape,
        scratch_shapes=[pltpu.VMEM((8, RWKV_PAD), F32), pltpu.VMEM((4, LANES, LANES), F32),
                        pltpu.VMEM((tb, 512), F32)],
        compiler_params=_params("parallel", "arbitrary"),
        name="rwkv7",
    )(*ins)
    return (res[0], v_first) if has_vres else (res[0], res[1])


def _mamba_kernel(tb, zb_ref, cw_ref, cb_ref, dtb_ref, alog_ref, dexp_ref, nw_ref,
                  o_ref, xbuf_ref, state_ref, y_ref):
    @pl.when(pl.program_id(1) == 0)
    def _():
        xbuf_ref[0:8, :] = jnp.zeros((8, 1024), F32)
        state_ref[...] = jnp.zeros_like(state_ref)

    xbuf_ref[8:8 + tb, :] = zb_ref[:, 512:1536]
    conv = cb_ref[...]
    for i in range(4):
        conv = conv + cw_ref[i:i + 1, :] * xbuf_ref[5 + i:5 + i + tb, :]
    xbuf_ref[0:8, :] = xbuf_ref[tb:tb + 8, :]
    xbc = _silu(conv)
    xs = xbc[:, 0:512]
    bm = xbc[:, 512:768]
    cm = xbc[:, 768:1024]

    lane = _iota2((1, LANES), 1)
    dt = _softplus(zb_ref[:, 1536:1664] + dtb_ref[...])
    a_neg = jnp.where(lane < SSM_HEADS, -jnp.exp(alog_ref[...]), 0.0)
    acs = _dot_split_l(_chunk_tri(tb), dt * a_neg)
    expand = jnp.where(_iota2((LANES, 512), 0) == _iota2((LANES, 512), 1) // CHUNK, 1.0, 0.0)
    dt_e = _dot_split(dt, expand)
    acs_e = _dot_split(acs, expand)
    xdt = xs * dt_e

    causal = _iota2((CHUNK, CHUNK), 0) >= _iota2((CHUNK, CHUNK), 1)
    lane_head = _iota2((CHUNK, 512), 1) // CHUNK
    for c in range(tb // CHUNK):
        rows = slice(c * CHUNK, (c + 1) * CHUNK)
        acs_c = acs[rows]
        acs_t = acs_c.T
        acs_ec = acs_e[rows]
        last_e = acs_ec[CHUNK - 1:CHUNK, :]
        xdt_c = xdt[rows]
        y = jnp.zeros((CHUNK, 512), F32)
        for g in range(2):
            gs = slice(g * SSM_STATE, (g + 1) * SSM_STATE)
            cb = _dot_nt(cm[rows, gs], bm[rows, gs])
            for e in range(4):
                h = g * 4 + e
                seg = acs_c[:, h:h + 1] - acs_t[h:h + 1, :]
                m = cb * jnp.exp(jnp.where(causal, seg, -jnp.inf))
                y = y + _dot(m, jnp.where(lane_head == h, xdt_c, 0.0))
        x_out = xdt_c * jnp.exp(last_e - acs_ec)
        e_in = jnp.exp(acs_ec)
        e_last = jnp.exp(last_e)
        for g in range(2):
            gs = slice(g * SSM_STATE, (g + 1) * SSM_STATE)
            ls = slice(g * 256, (g + 1) * 256)
            s_g = state_ref[g]
            y_ref[rows, ls] = y[:, ls] + _dot(cm[rows, gs], s_g) * e_in[:, ls]
            state_ref[g] = s_g * e_last[:, ls] + _dot_tn(bm[rows, gs], x_out[:, ls])

    y = (y_ref[...] + xs * dexp_ref[...]) * _silu(zb_ref[:, 0:512])
    for g in range(2):
        ls = slice(g * 256, (g + 1) * 256)
        yg = y[:, ls]
        ms = jnp.mean(yg * yg, axis=-1, keepdims=True)
        o_ref[:, ls] = (yg * lax.rsqrt(ms + NORM_EPS) * nw_ref[:, ls]).astype(o_ref.dtype)


def _mamba(zb, prm, tb=128):
    bsz, seq, _ = zb.shape
    blk = lambda n: pl.BlockSpec((None, tb, n), lambda b, i: (b, i, 0))
    weights = [prm[n] for n in ("cw", "cb", "dtb", "alog", "dexp", "nw")]
    return pl.pallas_call(
        functools.partial(_mamba_kernel, tb),
        grid=(bsz, seq // tb),
        in_specs=[blk(SSM_PAD)] + [_resident(w.shape) for w in weights],
        out_specs=blk(512),
        out_shape=jax.ShapeDtypeStruct((bsz, seq, 512), BF16),
        scratch_shapes=[pltpu.VMEM((tb + 8, 1024), F32), pltpu.VMEM((2, SSM_STATE, 256), F32),
                        pltpu.VMEM((tb, 512), F32)],
        compiler_params=_params("parallel", "arbitrary"),
        name="mamba2",
    )(zb, *weights)


def _attn_kernel(tq, seq, q_ref, k_ref, v_ref, qg_ref, kg_ref, bias_ref, o_ref, kn_ref, vb_ref):
    i = pl.program_id(1)
    head_mean = _group_mean_matrix(512, CHUNK)
    win = tq + ATT_LEFT

    @pl.when(i == 0)
    def _():
        kn_ref[0:ATT_LEFT, :] = jnp.zeros((ATT_LEFT, 512), BF16)
        vb_ref[0:ATT_LEFT, :] = jnp.zeros((ATT_LEFT, 512), BF16)
        for j in range(seq // tq):
            kj = k_ref[j * tq:(j + 1) * tq, :]
            ms = _dot_split(kj * kj, head_mean)
            kn_ref[ATT_LEFT + j * tq:ATT_LEFT + (j + 1) * tq, :] = (
                kj * lax.rsqrt(ms + NORM_EPS) * kg_ref[...]).astype(BF16)
            vb_ref[ATT_LEFT + j * tq:ATT_LEFT + (j + 1) * tq, :] = v_ref[j * tq:(j + 1) * tq, :].astype(BF16)

    q = q_ref[...]
    ms = _dot_split(q * q, head_mean)
    qn = q * lax.rsqrt(ms + NORM_EPS) * qg_ref[...] * (CHUNK ** -0.5)
    start = pl.multiple_of(i * tq, tq)
    kwin = kn_ref[pl.ds(start, win), :]
    vwin = vb_ref[pl.ds(start, win), :]
    qc = ATT_LEFT // CHUNK + _iota2((tq, win), 0) // CHUNK
    kc = _iota2((tq, win), 1) // CHUNK
    first_kc = jnp.maximum(qc - ATT_LEFT // CHUNK, ATT_LEFT // CHUNK - i * (tq // CHUNK))
    valid = (kc <= qc) & (kc >= first_kc)
    lane_head = _iota2((tq, LANES), 1) // CHUNK
    for p in range(4):
        ls = slice(p * LANES, (p + 1) * LANES)
        acc = jnp.zeros((tq, LANES), F32)
        for s in range(2):
            qh = jnp.where(lane_head == s, qn[:, ls], 0.0)
            sc = _dot_nt(qh, kwin[:, ls]) + bias_ref[2 * p + s]
            sc = jnp.where(valid, sc, -jnp.inf)
            e = jnp.exp(sc - jnp.max(sc, axis=-1, keepdims=True))
            pr = e / jnp.sum(e, axis=-1, keepdims=True)
            acc = jnp.where(lane_head == s, _dot(pr, vwin[:, ls]), acc)
        o_ref[:, ls] = acc.astype(o_ref.dtype)


def _attention(zc, q_gain, k_gain, rel_bias, tq=256):
    bsz, seq, _ = zc.shape
    win = tq + ATT_LEFT
    rel = jnp.clip(jnp.arange(win)[None, :] - ATT_LEFT - jnp.arange(tq)[:, None], -REL_CLIP, REL_CLIP) + REL_CLIP
    bias = rel_bias.astype(F32)[:, rel]
    qg = jnp.tile(q_gain, ATT_HEADS)[None, :]
    kg = jnp.tile(k_gain, ATT_HEADS)[None, :]
    full = lambda col: pl.BlockSpec((None, seq, 512), lambda b, i: (b, 0, col))
    return pl.pallas_call(
        functools.partial(_attn_kernel, tq, seq),
        grid=(bsz, seq // tq),
        in_specs=[pl.BlockSpec((None, tq, 512), lambda b, i: (b, i, 0)), full(1), full(2),
                  _resident(qg.shape), _resident(kg.shape), _resident(bias.shape)],
        out_specs=pl.BlockSpec((None, tq, 512), lambda b, i: (b, i, 0)),
        out_shape=jax.ShapeDtypeStruct((bsz, seq, 512), BF16),
        scratch_shapes=[pltpu.VMEM((ATT_LEFT + seq, 512), BF16), pltpu.VMEM((ATT_LEFT + seq, 512), BF16)],
        compiler_params=_params("parallel", "arbitrary"),
        name="band_attention",
    )(zc, zc, zc, qg, kg, bias)


def _gla_kernel(tb, zd_ref, gup_ref, gb_ref, nw_ref, o_ref, state_ref, y_ref):
    @pl.when(pl.program_id(1) == 0)
    def _():
        state_ref[...] = jnp.zeros_like(state_ref)

    q = zd_ref[:, 0:256] * (CHUNK ** -0.5)
    k = zd_ref[:, 256:512]
    v = zd_ref[:, 512:1024]
    log_a = -_softplus(-(_dot_f32(zd_ref[:, 1536:1664], gup_ref[...]) + gb_ref[...])) / GLA_GATE_NORM
    bcum = _dot_split_l(_chunk_tri(tb), log_a)

    k_head = _iota2((CHUNK, 256), 1) // CHUNK
    v_head = _iota2((CHUNK, 512), 1) // LANES
    causal = _iota2((CHUNK, 256), 0) >= _iota2((CHUNK, 256), 1) % CHUNK
    bd = (_iota2((512, 256), 0) // LANES) == (_iota2((512, 256), 1) // CHUNK)
    for c in range(tb // CHUNK):
        rows = slice(c * CHUNK, (c + 1) * CHUNK)
        bc = bcum[rows]
        blast = bc[CHUNK - 1:CHUNK, :]
        qg = q[rows] * jnp.exp(bc)
        kg = k[rows] * jnp.exp(-bc)
        kd = k[rows] * jnp.exp(blast - bc)
        v_c = v[rows]
        kg_bd = jnp.concatenate([jnp.where(k_head == h, kg, 0.0) for h in range(4)], axis=0)
        v_bd = jnp.concatenate([jnp.where(v_head == h, v_c, 0.0) for h in range(4)], axis=0)
        att = jnp.where(causal, _dot_nt(qg, kg_bd), 0.0)
        st = state_ref[...]
        y_ref[rows, :] = _dot(att, v_bd) + _dot_nt(qg, st)
        state_ref[...] = st * jnp.exp(blast) + jnp.where(bd, _dot_tn(v_c, kd), 0.0)

    o = y_ref[...]
    for h in range(4):
        ls = slice(h * LANES, (h + 1) * LANES)
        oh = o[:, ls]
        ms = jnp.mean(oh * oh, axis=-1, keepdims=True)
        o_ref[:, ls] = (oh * lax.rsqrt(ms + NORM_EPS) * nw_ref[...]
                        * _silu(zd_ref[:, 1024 + h * LANES:1024 + (h + 1) * LANES])).astype(o_ref.dtype)


def _gla(zd, prm, tb=128):
    bsz, seq, _ = zd.shape
    blk = lambda n: pl.BlockSpec((None, tb, n), lambda b, i: (b, i, 0))
    weights = [prm[n] for n in ("gup", "gb", "nw")]
    return pl.pallas_call(
        functools.partial(_gla_kernel, tb),
        grid=(bsz, seq // tb),
        in_specs=[blk(GLA_PAD)] + [_resident(w.shape) for w in weights],
        out_specs=blk(512),
        out_shape=jax.ShapeDtypeStruct((bsz, seq, 512), BF16),
        scratch_shapes=[pltpu.VMEM((512, 256), F32), pltpu.VMEM((tb, 512), F32)],
        compiler_params=_params("parallel", "arbitrary"),
        name="gla",
    )(zd, *weights)


def _merge_kernel(x_ref, h_ref, oa_ref, ob_ref, oc_ref, od_ref, wg_ref, wb_ref, wo_ref, nf_ref,
                  x1_ref, hf_ref):
    h = h_ref[...]
    acc = jnp.zeros(x_ref.shape, F32)
    for i, o_ref in enumerate((oa_ref, ob_ref, oc_ref, od_ref)):
        gate = _sigmoid(jnp.dot(h, wg_ref[:, i * D_MODEL:(i + 1) * D_MODEL], preferred_element_type=F32))
        acc = acc + gate * jnp.dot(o_ref[...], wb_ref[i], preferred_element_type=F32)
    x1 = x_ref[...] + jnp.dot(acc.astype(BF16), wo_ref[...], preferred_element_type=F32)
    x1_ref[...] = x1
    ms = jnp.mean(x1 * x1, axis=-1, keepdims=True)
    hf_ref[...] = (x1 * lax.rsqrt(ms + NORM_EPS) * nf_ref[...]).astype(BF16)


def _merge(x2, h, outs, wg, wb, wo, nf, tm=256):
    n_tok = x2.shape[0]
    row = lambda n: pl.BlockSpec((tm, n), lambda i: (i, 0))
    return pl.pallas_call(
        _merge_kernel,
        grid=(n_tok // tm,),
        in_specs=[row(D_MODEL), row(D_MODEL)] + [row(BRANCH_DIM)] * 4
        + [_resident(wg.shape), _resident(wb.shape), _resident(wo.shape), _resident(nf.shape)],
        out_specs=[row(D_MODEL), row(D_MODEL)],
        out_shape=[jax.ShapeDtypeStruct((n_tok, D_MODEL), F32), jax.ShapeDtypeStruct((n_tok, D_MODEL), BF16)],
        compiler_params=_params("parallel"),
        name="merge",
    )(x2, h, *outs, wg, wb, wo, nf)


def _ffn_kernel(tf, x1_ref, hf_ref, w1_ref, w3_ref, w2_ref, o_ref):
    hf = hf_ref[...]
    acc = x1_ref[...]
    for lo in range(0, FFN_DIM, tf):
        cols = slice(lo, min(lo + tf, FFN_DIM))
        a = jnp.dot(hf, w1_ref[:, cols], preferred_element_type=F32)
        b = jnp.dot(hf, w3_ref[:, cols], preferred_element_type=F32)
        acc = acc + jnp.dot((_silu(a) * b).astype(BF16), w2_ref[cols, :], preferred_element_type=F32)
    o_ref[...] = acc


def _ffn(x1, hf, w1, w3, w2, tm=512, tf=512):
    n_tok = x1.shape[0]
    row = pl.BlockSpec((tm, D_MODEL), lambda i: (i, 0))
    return pl.pallas_call(
        functools.partial(_ffn_kernel, tf),
        grid=(n_tok // tm,),
        in_specs=[row, row, _resident(w1.shape), _resident(w3.shape), _resident(w2.shape)],
        out_specs=row,
        out_shape=jax.ShapeDtypeStruct((n_tok, D_MODEL), F32),
        compiler_params=_params("parallel"),
        name="ffn",
    )(x1, hf, w1, w3, w2)


def _router_kernel(tm, x1_ref, nf_ref, wr_ref, idx_ref, gate_ref, rank_ref, cnt_ref, hf_ref, carry_ref):
    @pl.when(pl.program_id(0) == 0)
    def _():
        carry_ref[...] = jnp.zeros_like(carry_ref)

    x1 = x1_ref[...]
    ms = jnp.mean(x1 * x1, axis=-1, keepdims=True)
    hf = x1 * lax.rsqrt(ms + NORM_EPS) * nf_ref[...]
    hf_ref[...] = hf
    logits = lax.dot_general(wr_ref[...], hf, (((1,), (1,)), ((), ())), precision=HIGHEST,
                             preferred_element_type=F32)
    e_iota = _iota2((N_EXPERTS, tm), 0)
    m1 = jnp.max(logits, axis=0, keepdims=True)
    i1 = jnp.min(jnp.where(logits == m1, e_iota, N_EXPERTS), axis=0, keepdims=True)
    rest = jnp.where(e_iota == i1, -jnp.inf, logits)
    m2 = jnp.max(rest, axis=0, keepdims=True)
    i2 = jnp.min(jnp.where(rest == m2, e_iota, N_EXPERTS), axis=0, keepdims=True)
    e2 = jnp.exp(m2 - m1)
    gate_ref[0:1, :] = 1.0 / (1.0 + e2)
    gate_ref[1:2, :] = e2 / (1.0 + e2)
    idx_ref[0:1, :] = i1
    idx_ref[1:2, :] = i2
    hit1 = jnp.where(e_iota == i1, 1.0, 0.0)
    hit2 = jnp.where(e_iota == i2, 1.0, 0.0)
    before = jnp.where(_iota2((tm, tm), 0) < _iota2((tm, tm), 1), 1.0, 0.0)
    prior = _dot(hit1 + hit2, before) + carry_ref[:, 0:1]
    rank_ref[0:1, :] = jnp.sum(hit1 * prior, axis=0, keepdims=True).astype(jnp.int32)
    rank_ref[1:2, :] = jnp.sum(hit2 * prior, axis=0, keepdims=True).astype(jnp.int32)
    carry_ref[...] = carry_ref[...] + jnp.sum(hit1 + hit2, axis=1, keepdims=True)
    cnt_ref[...] = carry_ref[...]


def _router(x1, nf, wr_t, tm=256):
    n_tok = x1.shape[0]
    col = pl.BlockSpec((2, tm), lambda i: (0, i))
    return pl.pallas_call(
        functools.partial(_router_kernel, tm),
        grid=(n_tok // tm,),
        in_specs=[pl.BlockSpec((tm, D_MODEL), lambda i: (i, 0)), _resident(nf.shape), _resident(wr_t.shape)],
        out_specs=[col, col, col, pl.BlockSpec((N_EXPERTS, LANES), lambda i: (0, 0)),
                   pl.BlockSpec((tm, D_MODEL), lambda i: (i, 0))],
        out_shape=[jax.ShapeDtypeStruct((2, n_tok), jnp.int32), jax.ShapeDtypeStruct((2, n_tok), F32),
                   jax.ShapeDtypeStruct((2, n_tok), jnp.int32), jax.ShapeDtypeStruct((N_EXPERTS, LANES), F32),
                   jax.ShapeDtypeStruct((n_tok, D_MODEL), F32)],
        scratch_shapes=[pltpu.VMEM((N_EXPERTS, LANES), F32)],
        compiler_params=_params("arbitrary"),
        name="moe_router",
    )(x1, nf, wr_t)


def _gather_rows(table, idx, window=128, split=4):
    n_rows, width = idx.shape[0], table.shape[1]
    table = table.reshape(table.shape[0] * split, width // split)
    idx = (idx[:, None] * split + jnp.arange(split, dtype=jnp.int32)[None, :]).reshape(-1)
    return _gather_subrows(table, idx, window).reshape(n_rows, width)


def _gather_subrows(table, idx, window):
    n = idx.shape[0]
    d = table.shape[1]
    mesh = plsc.VectorSubcoreMesh(core_axis_name="core", subcore_axis_name="subcore")

    @functools.partial(pl.kernel, out_type=jax.ShapeDtypeStruct((n, d), table.dtype), mesh=mesh,
                       name="gather_rows")
    def gather(table_hbm, idx_hbm, out_hbm):
        def body(idx_vmem, out_vmem):
            pltpu.sync_copy(table_hbm.at[idx_vmem.at[0]], out_vmem)

        pltpu.emit_pipeline(
            body,
            grid=(n // window,),
            in_specs=[pl.BlockSpec((1, window), index_map=lambda i: (0, i))],
            out_specs=[pl.BlockSpec((window, d), index_map=lambda i: (i, 0))],
            core_axis_name=("core", "subcore"),
            dimension_semantics=(pltpu.PARALLEL,),
        )(idx_hbm, out_hbm)

    return gather(table, idx.reshape(1, n))


def _expert_kernel(tf, ge_ref, nv_ref, x_ref, w1_ref, w3_ref, w2_ref, y_ref):
    g = pl.program_id(0)

    @pl.when(g < nv_ref[0])
    def _():
        x = x_ref[...].astype(BF16)
        acc = jnp.zeros(y_ref.shape, F32)
        for lo in range(0, EXPERT_DIM, tf):
            a = jnp.dot(x, w1_ref[:, lo:lo + tf], preferred_element_type=F32)
            b = jnp.dot(x, w3_ref[:, lo:lo + tf], preferred_element_type=F32)
            acc = acc + jnp.dot((_silu(a) * b).astype(BF16), w2_ref[lo:lo + tf, :], preferred_element_type=F32)
        y_ref[...] = acc

    @pl.when(g >= nv_ref[0])
    def _():
        y_ref[...] = jnp.zeros_like(y_ref)


def _experts(xg, group_expert, n_valid, w1, w3, w2, tf=512):
    n_rows = xg.shape[0]
    return pl.pallas_call(
        functools.partial(_expert_kernel, tf),
        grid_spec=pltpu.PrefetchScalarGridSpec(
            num_scalar_prefetch=2,
            grid=(n_rows // MOE_ROWS,),
            in_specs=[pl.BlockSpec((MOE_ROWS, D_MODEL), lambda g, ge, nv: (g, 0)),
                      pl.BlockSpec((None, D_MODEL, EXPERT_DIM), lambda g, ge, nv: (ge[g], 0, 0)),
                      pl.BlockSpec((None, D_MODEL, EXPERT_DIM), lambda g, ge, nv: (ge[g], 0, 0)),
                      pl.BlockSpec((None, EXPERT_DIM, D_MODEL), lambda g, ge, nv: (ge[g], 0, 0))],
            out_specs=pl.BlockSpec((MOE_ROWS, D_MODEL), lambda g, ge, nv: (g, 0)),
        ),
        out_shape=jax.ShapeDtypeStruct((n_rows, D_MODEL), F32),
        compiler_params=pltpu.CompilerParams(dimension_semantics=("arbitrary",),
                                             vmem_limit_bytes=EXPERT_VMEM_LIMIT),
        name="experts",
    )(group_expert, n_valid, xg, w1, w3, w2)


def _combine_kernel(x1_ref, y0_ref, y1_ref, gate_ref, o_ref):
    gate = gate_ref[...]
    o_ref[...] = x1_ref[...] + gate[:, 0:1] * y0_ref[...] + gate[:, 1:2] * y1_ref[...]


def _combine(x1, yg, gate_t, tm=512):
    n_tok = x1.shape[0]
    nb = n_tok // tm
    return pl.pallas_call(
        _combine_kernel,
        grid=(nb,),
        in_specs=[pl.BlockSpec((tm, D_MODEL), lambda i: (i, 0)), pl.BlockSpec((tm, D_MODEL), lambda i: (i, 0)),
                  pl.BlockSpec((tm, D_MODEL), lambda i: (i + nb, 0)), pl.BlockSpec((tm, 2), lambda i: (i, 0))],
        out_specs=pl.BlockSpec((tm, D_MODEL), lambda i: (i, 0)),
        out_shape=jax.ShapeDtypeStruct((n_tok, D_MODEL), F32),
        compiler_params=_params("parallel"),
        name="moe_combine",
    )(x1, yg, yg, gate_t)


def _moe(x1, nf, w_router, w1, w3, w2):
    n_tok = x1.shape[0]
    idx, gate, rank, cnt, hf = _router(x1, nf, w_router.T)
    counts = cnt[:, 0].astype(jnp.int32)
    padded = (counts + MOE_ROWS - 1) // MOE_ROWS * MOE_ROWS
    end_padded = jnp.cumsum(padded)
    start_padded = end_padded - padded
    dest = (start_padded[idx] + rank).reshape(-1)
    n_groups = (n_tok * 2 + MOE_ROWS - 1) // MOE_ROWS + N_EXPERTS
    n_rows = n_groups * MOE_ROWS
    group_expert = jnp.minimum(
        jnp.searchsorted(end_padded, jnp.arange(n_groups) * MOE_ROWS, side="right"), N_EXPERTS - 1).astype(jnp.int32)
    n_valid = (end_padded[-1:] // MOE_ROWS).astype(jnp.int32)
    tok = jnp.tile(jnp.arange(n_tok, dtype=jnp.int32), 2)
    row_token = jnp.zeros((n_rows,), jnp.int32).at[dest].set(tok)
    xg = _gather_rows(hf, row_token)
    y = _experts(xg, group_expert, n_valid, w1, w3, w2)
    return _combine(x1, _gather_rows(y, dest), gate.T)


def _row(v, pad=0):
    v = v.reshape(1, -1).astype(F32)
    return jnp.pad(v, ((0, 0), (0, pad))) if pad else v


def _rows_at(w, start, total):
    return jnp.pad(w.astype(F32), ((start, total - start - w.shape[0]), (0, 0)))


def kernel(x, w_in, norm_mix, rwkv_mu, rwkv_decay_up, rwkv_w0, rwkv_a_up, rwkv_a0, rwkv_gate_up, rwkv_k_k, rwkv_k_a, rwkv_r_k, rwkv_ln_w, rwkv_ln_b, vres_down, vres_up, vres_v0, ssm_conv_w, ssm_conv_b, ssm_dt_bias, ssm_a_log, ssm_d, ssm_norm_w, att_q_gain, att_k_gain, att_rel_bias, gla_gate_up, gla_gate_bias, gla_norm_w, w_branch, w_out, norm_ffn, ffn_w1, ffn_w3, ffn_w2, moe_router, moe_w1, moe_w3, moe_w2):
    bsz, seq, _ = x.shape
    n_tok = bsz * seq
    depth = w_in.shape[0]
    x2 = x.reshape(n_tok, D_MODEL)
    v_first = None
    o_rw, o_ss, o_at = RWKV_COLS, RWKV_COLS + SSM_COLS, RWKV_COLS + SSM_COLS + ATT_COLS
    o_gl = o_at + GLA_COLS
    for l in range(depth):
        w = w_in[l]
        vd = vres_down[l - 1] if l > 0 else jnp.zeros((D_MODEL, 32), F32)
        wa = jnp.concatenate([w[:, :o_rw], vd, jnp.zeros((D_MODEL, RWKV_PAD - RWKV_COLS - 32), F32)], axis=1)
        wb = jnp.pad(w[:, o_rw:o_ss], ((0, 0), (0, SSM_PAD - SSM_COLS)))
        wc = w[:, o_ss:o_at]
        gl = w[:, o_at:o_gl]
        wd = jnp.concatenate([gl[:, 0:1024], gl[:, 1040:1552], gl[:, 1024:1040],
                              jnp.zeros((D_MODEL, GLA_PAD - GLA_COLS), F32)], axis=1)
        wg = w[:, o_gl:]
        h, za, zb, zc, zd = _inproj(x2, _row(norm_mix[l]), wa.astype(BF16), wb.astype(BF16),
                                    wc.astype(BF16), wd.astype(BF16))
        shp = lambda z: z.reshape(bsz, seq, z.shape[-1])

        rw = dict(mu=_row(rwkv_mu[l], RWKV_PAD - RWKV_COLS),
                  dup=_rows_at(rwkv_decay_up[l], 0, LANES), w0=_row(rwkv_w0[l]),
                  aup=_rows_at(rwkv_a_up[l], 64, LANES), a0=_row(rwkv_a0[l]),
                  gup=_rows_at(rwkv_gate_up[l], 0, 256).astype(BF16),
                  kk=_row(rwkv_k_k[l]), ka=_row(rwkv_k_a[l]), rk=_row(rwkv_r_k[l]),
                  lnw=_row(rwkv_ln_w[l]), lnb=_row(rwkv_ln_b[l]))
        if l > 0:
            rw.update(vup=_rows_at(vres_up[l - 1], 160, 256), v0=_row(vres_v0[l - 1]))
        o_a, v_first = _rwkv(shp(za), v_first, rw)
        o_b = _mamba(shp(zb), dict(cw=ssm_conv_w[l], cb=_row(ssm_conv_b[l]),
                                   dtb=_row(ssm_dt_bias[l], LANES - SSM_HEADS),
                                   alog=_row(ssm_a_log[l], LANES - SSM_HEADS),
                                   dexp=_row(jnp.repeat(ssm_d[l], CHUNK)), nw=_row(ssm_norm_w[l])))
        o_c = _attention(shp(zc), att_q_gain[l], att_k_gain[l], att_rel_bias)
        o_d = _gla(shp(zd), dict(gup=_rows_at(gla_gate_up[l], 0, LANES), gb=_row(gla_gate_bias[l]),
                                 nw=_row(gla_norm_w[l])))
        outs = [o.reshape(n_tok, BRANCH_DIM) for o in (o_a, o_b, o_c, o_d)]
        x1, hf = _merge(x2, h, outs, wg.astype(BF16), w_branch[l].astype(BF16), w_out[l].astype(BF16),
                        _row(norm_ffn[l]))
        if l % 2 == 0:
            x2 = _ffn(x1, hf, ffn_w1[l // 2].astype(BF16), ffn_w3[l // 2].astype(BF16),
                      ffn_w2[l // 2].astype(BF16))
        else:
            x2 = _moe(x1, _row(norm_ffn[l]), moe_router[l // 2], moe_w1[l // 2].astype(BF16),
                      moe_w3[l // 2].astype(BF16), moe_w2[l // 2].astype(BF16))
    return x2.reshape(bsz, seq, D_MODEL)
```

```python
import functools

import jax
import jax.numpy as jnp
from jax import lax
from jax.experimental import pallas as pl
from jax.experimental.pallas import tpu as pltpu
from jax.experimental.pallas import tpu_sc as plsc

F32 = jnp.float32
BF16 = jnp.bfloat16
HIGHEST = lax.Precision.HIGHEST

D_MODEL = 1024
CHUNK = 64
BRANCH_DIM = 512
NORM_EPS = 1e-6
LANES = 128
VMEM_LIMIT = 56 * 1024 * 1024
EXPERT_VMEM_LIMIT = 60 * 1024 * 1024

RWKV_LN_EPS = 64e-5
RWKV_COLS = 1824
RWKV_PAD = 1920
SSM_COLS = 1544
SSM_PAD = 1664
SSM_HEADS = 8
SSM_STATE = 128
ATT_COLS = 1536
ATT_HEADS = 8
ATT_LEFT = 8 * CHUNK
REL_CLIP = 2 * CHUNK
GLA_COLS = 1552
GLA_PAD = 1664
GLA_GATE_NORM = 16.0
FFN_DIM = 2816
N_EXPERTS = 8
EXPERT_DIM = 3584
MOE_ROWS = 256
ROW_SPLIT = 4
SUBROW = D_MODEL // ROW_SPLIT


def _dot(a, b):
    return jnp.dot(a.astype(BF16), b.astype(BF16), preferred_element_type=F32)


def _dot_nt(a, b):
    return lax.dot_general(a.astype(BF16), b.astype(BF16), (((1,), (1,)), ((), ())),
                           preferred_element_type=F32)


def _dot_tn(a, b):
    return lax.dot_general(a.astype(BF16), b.astype(BF16), (((0,), (0,)), ((), ())),
                           preferred_element_type=F32)


def _dot_f32(a, b):
    return jnp.dot(a, b, precision=HIGHEST, preferred_element_type=F32)


def _dot_split(a, m):
    m = m.astype(BF16)
    hi = a.astype(BF16)
    r1 = a - hi.astype(F32)
    mid = r1.astype(BF16)
    lo = (r1 - mid.astype(F32)).astype(BF16)
    return (jnp.dot(hi, m, preferred_element_type=F32) + jnp.dot(mid, m, preferred_element_type=F32)
            + jnp.dot(lo, m, preferred_element_type=F32))


def _dot_split_l(m, a):
    m = m.astype(BF16)
    hi = a.astype(BF16)
    r1 = a - hi.astype(F32)
    mid = r1.astype(BF16)
    lo = (r1 - mid.astype(F32)).astype(BF16)
    return (jnp.dot(m, hi, preferred_element_type=F32) + jnp.dot(m, mid, preferred_element_type=F32)
            + jnp.dot(m, lo, preferred_element_type=F32))


def _softplus(x):
    return jnp.maximum(x, 0.0) + jnp.log(1.0 + jnp.exp(-jnp.abs(x)))


def _sigmoid(x):
    return 1.0 / (1.0 + jnp.exp(-x))


def _silu(x):
    return x * _sigmoid(x)


def _iota2(shape, axis):
    return lax.broadcasted_iota(jnp.int32, shape, axis)


def _group_mean_matrix(n, group):
    r = _iota2((n, n), 0) // group
    c = _iota2((n, n), 1) // group
    return jnp.where(r == c, 1.0 / group, 0.0).astype(F32)


def _chunk_tri(n):
    r = _iota2((n, n), 0)
    c = _iota2((n, n), 1)
    return jnp.where((r // CHUNK == c // CHUNK) & (r >= c), 1.0, 0.0).astype(F32)


def _resident(shape):
    nd = len(shape)
    return pl.BlockSpec(shape, lambda *_: (0,) * nd, pipeline_mode=pl.Buffered(1))


def _params(*sem):
    return pltpu.CompilerParams(dimension_semantics=sem, vmem_limit_bytes=VMEM_LIMIT)


def _inproj_kernel(x_ref, g_ref, wa_ref, wb_ref, wc_ref, wd_ref,
                   h_ref, za_ref, zb_ref, zc_ref, zd_ref):
    x = x_ref[...]
    ms = jnp.mean(x * x, axis=-1, keepdims=True)
    h = (x * lax.rsqrt(ms + NORM_EPS) * g_ref[...]).astype(BF16)
    h_ref[...] = h
    za_ref[...] = jnp.dot(h, wa_ref[...], preferred_element_type=F32)
    zb_ref[...] = jnp.dot(h, wb_ref[...], preferred_element_type=F32)
    zc_ref[...] = jnp.dot(h, wc_ref[...], preferred_element_type=F32)
    zd_ref[...] = jnp.dot(h, wd_ref[...], preferred_element_type=F32)


def _inproj(x2, g, wa, wb, wc, wd, tm=256):
    n_tok = x2.shape[0]
    row = lambda n: pl.BlockSpec((tm, n), lambda i: (i, 0))
    outs = (D_MODEL, RWKV_PAD, SSM_PAD, ATT_COLS, GLA_PAD)
    return pl.pallas_call(
        _inproj_kernel,
        grid=(n_tok // tm,),
        in_specs=[row(D_MODEL), _resident((1, D_MODEL)), _resident(wa.shape), _resident(wb.shape),
                  _resident(wc.shape), _resident(wd.shape)],
        out_specs=[row(n) for n in outs],
        out_shape=[jax.ShapeDtypeStruct((n_tok, outs[0]), BF16)]
        + [jax.ShapeDtypeStruct((n_tok, n), F32) for n in outs[1:]],
        compiler_params=_params("parallel"),
        name="inproj",
    )(x2, g, wa, wb, wc, wd)


def _rwkv_block(r, k, v, kk, ka, ld, tb, state_ref, y_ref):
    n_chunks = tb // CHUNK
    cum = _dot_split_l(_chunk_tri(tb), ld)
    p_inv = jnp.exp(-cum)
    a_t = -kk * jnp.exp(cum - ld)
    b_t = kk * ka * p_inv
    k_t = k * p_inv
    r_t = r * jnp.exp(cum)

    lane_head = _iota2((CHUNK, LANES), 1) // CHUNK
    row = _iota2((CHUNK, LANES), 0)
    col = _iota2((CHUNK, LANES), 1) % CHUNK
    strict = row > col
    incl = row >= col
    bd_mask = (_iota2((LANES, LANES), 0) // CHUNK) == (_iota2((LANES, LANES), 1) // CHUNK)
    eye = jnp.where(_iota2((LANES, LANES), 0) == _iota2((LANES, LANES), 1), 1.0, 0.0).astype(F32)

    def stack(z):
        return jnp.concatenate([jnp.where(lane_head == 0, z, 0.0), jnp.where(lane_head == 1, z, 0.0)], axis=0)

    def sub(z, u):
        c, p = u
        return z[c * CHUNK:(c + 1) * CHUNK, p * LANES:(p + 1) * LANES]

    units = [(c, p) for c in range(n_chunks) for p in range(4)]
    bf = lambda z: z.astype(BF16)
    ar = {u: bf(jnp.concatenate([sub(a_t, u), sub(r_t, u)], axis=0)) for u in units}
    g_b = {u: _dot_nt(ar[u], stack(sub(b_t, u))) for u in units}
    g_k = {u: _dot_nt(ar[u], stack(sub(k_t, u))) for u in units}
    a_rb = {u: bf(jnp.where(incl, g_b[u][CHUNK:], 0.0)) for u in units}
    a_rk = {u: bf(jnp.where(incl, g_k[u][CHUNK:], 0.0)) for u in units}
    a_ak = {u: bf(jnp.where(strict, g_k[u][:CHUNK], 0.0)) for u in units}
    pw = {u: stack(jnp.where(strict, g_b[u][:CHUNK], 0.0)) for u in units}
    t_bd = {u: eye + pw[u] for u in units}
    for _ in range(CHUNK.bit_length() - 2):
        pw = {u: _dot(pw[u], pw[u]) for u in units}
        t_bd = {u: t_bd[u] + _dot(t_bd[u], pw[u]) for u in units}
    t_bd = {u: bf(t_bd[u]) for u in units}
    v_bd = {u: bf(stack(sub(v, u))) for u in units}
    w_bd = {u: bf(_dot(t_bd[u], stack(sub(a_t, u)))) for u in units}
    av = {u: _dot(a_ak[u], v_bd[u]) for u in units}
    uv_bd = {u: _dot(t_bd[u], stack(av[u])) for u in units}
    y_v = {u: _dot(a_rk[u], v_bd[u]) for u in units}

    h = [state_ref[p] for p in range(4)]
    for c in range(n_chunks):
        last = c * CHUNK + CHUNK - 1
        p_last = jnp.exp(cum[last:last + 1, :])
        hb = [bf(h[p]) for p in range(4)]
        u2 = [_dot_nt(w_bd[(c, p)], hb[p]) + uv_bd[(c, p)] for p in range(4)]
        u_p = [u2[p][:CHUNK] + u2[p][CHUNK:] for p in range(4)]
        upd = [_dot_tn(jnp.concatenate([u_p[p], sub(v, (c, p))], axis=0),
                       jnp.concatenate([sub(b_t, (c, p)), sub(k_t, (c, p))], axis=0)
                       * p_last[:, p * LANES:(p + 1) * LANES]) for p in range(4)]
        for p in range(4):
            y_ref[c * CHUNK:(c + 1) * CHUNK, p * LANES:(p + 1) * LANES] = (
                _dot_nt(sub(r_t, (c, p)), hb[p]) + _dot(a_rb[(c, p)], stack(u_p[p])) + y_v[(c, p)])
        h = [h[p] * p_last[:, p * LANES:(p + 1) * LANES] + jnp.where(bd_mask, upd[p], 0.0) for p in range(4)]
    for p in range(4):
        state_ref[p] = h[p]


def _rwkv_kernel(has_vres, tb, *refs):
    if has_vres:
        (za_ref, vfirst_ref, mu_ref, dup_ref, w0_ref, aup_ref, a0_ref, gup_ref, vup_ref, v0_ref,
         kk_ref, ka_ref, rk_ref, lnw_ref, lnb_ref, o_ref, prev_ref, state_ref, y_ref) = refs
    else:
        (za_ref, mu_ref, dup_ref, w0_ref, aup_ref, a0_ref, gup_ref,
         kk_ref, ka_ref, rk_ref, lnw_ref, lnb_ref, o_ref, vraw_ref, prev_ref, state_ref, y_ref) = refs

    @pl.when(pl.program_id(1) == 0)
    def _():
        prev_ref[...] = jnp.zeros_like(prev_ref)
        state_ref[...] = jnp.zeros_like(state_ref)

    za = za_ref[...]
    shifted = pltpu.roll(za, 1, axis=0)
    shifted = jnp.where(_iota2(za.shape, 0) == 0, prev_ref[7:8, :], shifted)
    prev_ref[...] = za[tb - 8:, :]
    za = za + (shifted - za) * mu_ref[...]

    r = za[:, 0:512]
    k = za[:, 512:1024]
    v = za[:, 1024:1536]
    xwa = za[:, 1536:1664]
    xg = za[:, 1664:1920]
    w_log = -_softplus(-(w0_ref[...] + _dot_f32(jnp.tanh(xwa), dup_ref[...]))) - 0.5
    ld = -jnp.exp(w_log)
    a = _sigmoid(a0_ref[...] + _dot_f32(xwa, aup_ref[...]))
    g = _dot(_sigmoid(xg), gup_ref[...])
    if has_vres:
        v_mix = _sigmoid(v0_ref[...] + _dot_f32(xg, vup_ref[...]))
        v = v + (vfirst_ref[...] - v) * v_mix
    else:
        vraw_ref[...] = v
    kk = k * kk_ref[...]
    k = k * (1.0 + (a - 1.0) * ka_ref[...])
    head_sum = _group_mean_matrix(512, CHUNK) * float(CHUNK)
    kk = kk / jnp.maximum(jnp.sqrt(_dot_split(kk * kk, head_sum)), 1e-12)

    _rwkv_block(r, k, v, kk, a, ld, tb, state_ref, y_ref)
    y = y_ref[...]
    head_mean = _group_mean_matrix(512, CHUNK)
    mean = _dot_split(y, head_mean)
    yc = y - mean
    var = _dot_split(yc * yc, head_mean)
    y = yc * lax.rsqrt(var + RWKV_LN_EPS) * lnw_ref[...] + lnb_ref[...]
    y = y + _dot_split(r * k * rk_ref[...], head_sum) * v
    o_ref[...] = (y * g).astype(o_ref.dtype)


def _rwkv(za, v_first, prm, tb=256):
    bsz, seq, _ = za.shape
    has_vres = v_first is not None
    blk = lambda n: pl.BlockSpec((None, tb, n), lambda b, i: (b, i, 0))
    names = (["mu", "dup", "w0", "aup", "a0", "gup"] + (["vup", "v0"] if has_vres else [])
             + ["kk", "ka", "rk", "lnw", "lnb"])
    weights = [prm[n] for n in names]
    ins = [za] + ([v_first] if has_vres else []) + weights
    in_specs = [blk(RWKV_PAD)] + ([blk(512)] if has_vres else []) + [_resident(w.shape) for w in weights]
    out_shape = [jax.ShapeDtypeStruct((bsz, seq, 512), BF16)]
    out_specs = [blk(512)]
    if not has_vres:
        out_shape.append(jax.ShapeDtypeStruct((bsz, seq, 512), F32))
        out_specs.append(blk(512))
    res = pl.pallas_call(
        functools.partial(_rwkv_kernel, has_vres, tb),
        grid=(bsz, seq // tb),
        in_specs=in_specs,
        out_specs=out_specs,
        out_shape=out_shape,
        scratch_shapes=[pltpu.VMEM((8, RWKV_PAD), F32), pltpu.VMEM((4, LANES, LANES), F32),
                        pltpu.VMEM((tb, 512), F32)],
        compiler_params=_params("parallel", "arbitrary"),
        name="rwkv7",
    )(*ins)
    return (res[0], v_first) if has_vres else (res[0], res[1])


def _mamba_kernel(tb, zb_ref, cw_ref, cb_ref, dtb_ref, alog_ref, dexp_ref, nw_ref,
                  o_ref, xbuf_ref, state_ref, y_ref):
    @pl.when(pl.program_id(1) == 0)
    def _():
        xbuf_ref[0:8, :] = jnp.zeros((8, 1024), F32)
        state_ref[...] = jnp.zeros_like(state_ref)

    xbuf_ref[8:8 + tb, :] = zb_ref[:, 512:1536]
    conv = cb_ref[...]
    for i in range(4):
        conv = conv + cw_ref[i:i + 1, :] * xbuf_ref[5 + i:5 + i + tb, :]
    xbuf_ref[0:8, :] = xbuf_ref[tb:tb + 8, :]
    xbc = _silu(conv)
    xs = xbc[:, 0:512]
    bm = xbc[:, 512:768]
    cm = xbc[:, 768:1024]

    lane = _iota2((1, LANES), 1)
    dt = _softplus(zb_ref[:, 1536:1664] + dtb_ref[...])
    a_neg = jnp.where(lane < SSM_HEADS, -jnp.exp(alog_ref[...]), 0.0)
    acs = _dot_split_l(_chunk_tri(tb), dt * a_neg)
    expand = jnp.where(_iota2((LANES, 512), 0) == _iota2((LANES, 512), 1) // CHUNK, 1.0, 0.0)
    dt_e = _dot_split(dt, expand)
    acs_e = _dot_split(acs, expand)
    xdt = xs * dt_e

    causal = _iota2((CHUNK, CHUNK), 0) >= _iota2((CHUNK, CHUNK), 1)
    lane_head = _iota2((CHUNK, 512), 1) // CHUNK
    for c in range(tb // CHUNK):
        rows = slice(c * CHUNK, (c + 1) * CHUNK)
        acs_c = acs[rows]
        acs_t = acs_c.T
        acs_ec = acs_e[rows]
        last_e = acs_ec[CHUNK - 1:CHUNK, :]
        xdt_c = xdt[rows]
        y = jnp.zeros((CHUNK, 512), F32)
        for g in range(2):
            gs = slice(g * SSM_STATE, (g + 1) * SSM_STATE)
            cb = _dot_nt(cm[rows, gs], bm[rows, gs])
            for e in range(4):
                h = g * 4 + e
                seg = acs_c[:, h:h + 1] - acs_t[h:h + 1, :]
                m = cb * jnp.exp(jnp.where(causal, seg, -jnp.inf))
                y = y + _dot(m, jnp.where(lane_head == h, xdt_c, 0.0))
        x_out = xdt_c * jnp.exp(last_e - acs_ec)
        e_in = jnp.exp(acs_ec)
        e_last = jnp.exp(last_e)
        for g in range(2):
            gs = slice(g * SSM_STATE, (g + 1) * SSM_STATE)
            ls = slice(g * 256, (g + 1) * 256)
            s_g = state_ref[g]
            y_ref[rows, ls] = y[:, ls] + _dot(cm[rows, gs], s_g) * e_in[:, ls]
            state_ref[g] = s_g * e_last[:, ls] + _dot_tn(bm[rows, gs], x_out[:, ls])

    y = (y_ref[...] + xs * dexp_ref[...]) * _silu(zb_ref[:, 0:512])
    for g in range(2):
        ls = slice(g * 256, (g + 1) * 256)
        yg = y[:, ls]
        ms = jnp.mean(yg * yg, axis=-1, keepdims=True)
        o_ref[:, ls] = (yg * lax.rsqrt(ms + NORM_EPS) * nw_ref[:, ls]).astype(o_ref.dtype)


def _mamba(zb, prm, tb=128):
    bsz, seq, _ = zb.shape
    blk = lambda n: pl.BlockSpec((None, tb, n), lambda b, i: (b, i, 0))
    weights = [prm[n] for n in ("cw", "cb", "dtb", "alog", "dexp", "nw")]
    return pl.pallas_call(
        functools.partial(_mamba_kernel, tb),
        grid=(bsz, seq // tb),
        in_specs=[blk(SSM_PAD)] + [_resident(w.shape) for w in weights],
        out_specs=blk(512),
        out_shape=jax.ShapeDtypeStruct((bsz, seq, 512), BF16),
        scratch_shapes=[pltpu.VMEM((tb + 8, 1024), F32), pltpu.VMEM((2, SSM_STATE, 256), F32),
                        pltpu.VMEM((tb, 512), F32)],
        compiler_params=_params("parallel", "arbitrary"),
        name="mamba2",
    )(zb, *weights)


def _attn_kernel(tq, seq, q_ref, k_ref, v_ref, qg_ref, kg_ref, bias_ref, o_ref, kn_ref, vb_ref):
    i = pl.program_id(1)
    head_mean = _group_mean_matrix(512, CHUNK)
    win = tq + ATT_LEFT

    @pl.when(i == 0)
    def _():
        kn_ref[0:ATT_LEFT, :] = jnp.zeros((ATT_LEFT, 512), BF16)
        vb_ref[0:ATT_LEFT, :] = jnp.zeros((ATT_LEFT, 512), BF16)
        for j in range(seq // tq):
            kj = k_ref[j * tq:(j + 1) * tq, :]
            ms = _dot_split(kj * kj, head_mean)
            kn_ref[ATT_LEFT + j * tq:ATT_LEFT + (j + 1) * tq, :] = (
                kj * lax.rsqrt(ms + NORM_EPS) * kg_ref[...]).astype(BF16)
            vb_ref[ATT_LEFT + j * tq:ATT_LEFT + (j + 1) * tq, :] = v_ref[j * tq:(j + 1) * tq, :].astype(BF16)

    q = q_ref[...]
    ms = _dot_split(q * q, head_mean)
    qn = q * lax.rsqrt(ms + NORM_EPS) * qg_ref[...] * (CHUNK ** -0.5)
    start = pl.multiple_of(i * tq, tq)
    kwin = kn_ref[pl.ds(start, win), :]
    vwin = vb_ref[pl.ds(start, win), :]
    qc = ATT_LEFT // CHUNK + _iota2((tq, win), 0) // CHUNK
    kc = _iota2((tq, win), 1) // CHUNK
    first_kc = jnp.maximum(qc - ATT_LEFT // CHUNK, ATT_LEFT // CHUNK - i * (tq // CHUNK))
    valid = (kc <= qc) & (kc >= first_kc)
    lane_head = _iota2((tq, LANES), 1) // CHUNK
    for p in range(4):
        ls = slice(p * LANES, (p + 1) * LANES)
        acc = jnp.zeros((tq, LANES), F32)
        for s in range(2):
            qh = jnp.where(lane_head == s, qn[:, ls], 0.0)
            sc = _dot_nt(qh, kwin[:, ls]) + bias_ref[2 * p + s]
            sc = jnp.where(valid, sc, -jnp.inf)
            e = jnp.exp(sc - jnp.max(sc, axis=-1, keepdims=True))
            pr = e / jnp.sum(e, axis=-1, keepdims=True)
            acc = jnp.where(lane_head == s, _dot(pr, vwin[:, ls]), acc)
        o_ref[:, ls] = acc.astype(o_ref.dtype)


def _band_bias(rel_bias, tq, win):
    period = tq + win
    m = jnp.arange(period)
    d = jnp.where(m < win, m, m - period) - ATT_LEFT
    f = rel_bias.astype(F32)[:, jnp.clip(d, -REL_CLIP, REL_CLIP) + REL_CLIP]
    g = jnp.tile(f, (1, tq))[:, :tq * (period - 1)].reshape(-1, tq, period - 1)
    return g[:, :, :win]


def _attention(zc, q_gain, k_gain, rel_bias, tq=256):
    bsz, seq, _ = zc.shape
    win = tq + ATT_LEFT
    bias = _band_bias(rel_bias, tq, win)
    qg = jnp.tile(q_gain, ATT_HEADS)[None, :]
    kg = jnp.tile(k_gain, ATT_HEADS)[None, :]
    full = lambda col: pl.BlockSpec((None, seq, 512), lambda b, i: (b, 0, col))
    return pl.pallas_call(
        functools.partial(_attn_kernel, tq, seq),
        grid=(bsz, seq // tq),
        in_specs=[pl.BlockSpec((None, tq, 512), lambda b, i: (b, i, 0)), full(1), full(2),
                  _resident(qg.shape), _resident(kg.shape), _resident(bias.shape)],
        out_specs=pl.BlockSpec((None, tq, 512), lambda b, i: (b, i, 0)),
        out_shape=jax.ShapeDtypeStruct((bsz, seq, 512), BF16),
        scratch_shapes=[pltpu.VMEM((ATT_LEFT + seq, 512), BF16), pltpu.VMEM((ATT_LEFT + seq, 512), BF16)],
        compiler_params=_params("parallel", "arbitrary"),
        name="band_attention",
    )(zc, zc, zc, qg, kg, bias)


def _gla_kernel(tb, zd_ref, gup_ref, gb_ref, nw_ref, o_ref, state_ref, y_ref):
    @pl.when(pl.program_id(1) == 0)
    def _():
        state_ref[...] = jnp.zeros_like(state_ref)

    q = zd_ref[:, 0:256] * (CHUNK ** -0.5)
    k = zd_ref[:, 256:512]
    v = zd_ref[:, 512:1024]
    log_a = -_softplus(-(_dot_f32(zd_ref[:, 1536:1664], gup_ref[...]) + gb_ref[...])) / GLA_GATE_NORM
    bcum = _dot_split_l(_chunk_tri(tb), log_a)

    k_head = _iota2((CHUNK, 256), 1) // CHUNK
    v_head = _iota2((CHUNK, 512), 1) // LANES
    causal = _iota2((CHUNK, 256), 0) >= _iota2((CHUNK, 256), 1) % CHUNK
    bd = (_iota2((512, 256), 0) // LANES) == (_iota2((512, 256), 1) // CHUNK)
    for c in range(tb // CHUNK):
        rows = slice(c * CHUNK, (c + 1) * CHUNK)
        bc = bcum[rows]
        blast = bc[CHUNK - 1:CHUNK, :]
        qg = q[rows] * jnp.exp(bc)
        kg = k[rows] * jnp.exp(-bc)
        kd = k[rows] * jnp.exp(blast - bc)
        v_c = v[rows]
        kg_bd = jnp.concatenate([jnp.where(k_head == h, kg, 0.0) for h in range(4)], axis=0)
        v_bd = jnp.concatenate([jnp.where(v_head == h, v_c, 0.0) for h in range(4)], axis=0)
        att = jnp.where(causal, _dot_nt(qg, kg_bd), 0.0)
        st = state_ref[...]
        y_ref[rows, :] = _dot(att, v_bd) + _dot_nt(qg, st)
        state_ref[...] = st * jnp.exp(blast) + jnp.where(bd, _dot_tn(v_c, kd), 0.0)

    o = y_ref[...]
    for h in range(4):
        ls = slice(h * LANES, (h + 1) * LANES)
        oh = o[:, ls]
        ms = jnp.mean(oh * oh, axis=-1, keepdims=True)
        o_ref[:, ls] = (oh * lax.rsqrt(ms + NORM_EPS) * nw_ref[...]
                        * _silu(zd_ref[:, 1024 + h * LANES:1024 + (h + 1) * LANES])).astype(o_ref.dtype)


def _gla(zd, prm, tb=128):
    bsz, seq, _ = zd.shape
    blk = lambda n: pl.BlockSpec((None, tb, n), lambda b, i: (b, i, 0))
    weights = [prm[n] for n in ("gup", "gb", "nw")]
    return pl.pallas_call(
        functools.partial(_gla_kernel, tb),
        grid=(bsz, seq // tb),
        in_specs=[blk(GLA_PAD)] + [_resident(w.shape) for w in weights],
        out_specs=blk(512),
        out_shape=jax.ShapeDtypeStruct((bsz, seq, 512), BF16),
        scratch_shapes=[pltpu.VMEM((512, 256), F32), pltpu.VMEM((tb, 512), F32)],
        compiler_params=_params("parallel", "arbitrary"),
        name="gla",
    )(zd, *weights)


def _merge_kernel(x_ref, h_ref, oa_ref, ob_ref, oc_ref, od_ref, wg_ref, wb_ref, wo_ref, nf_ref,
                  x1_ref, hf_ref):
    h = h_ref[...]
    acc = jnp.zeros(x_ref.shape, F32)
    for i, o_ref in enumerate((oa_ref, ob_ref, oc_ref, od_ref)):
        gate = _sigmoid(jnp.dot(h, wg_ref[:, i * D_MODEL:(i + 1) * D_MODEL], preferred_element_type=F32))
        acc = acc + gate * jnp.dot(o_ref[...], wb_ref[i], preferred_element_type=F32)
    x1 = x_ref[...] + jnp.dot(acc.astype(BF16), wo_ref[...], preferred_element_type=F32)
    x1_ref[...] = x1
    ms = jnp.mean(x1 * x1, axis=-1, keepdims=True)
    hf_ref[...] = (x1 * lax.rsqrt(ms + NORM_EPS) * nf_ref[...]).astype(BF16)


def _merge(x2, h, outs, wg, wb, wo, nf, tm=256):
    n_tok = x2.shape[0]
    row = lambda n: pl.BlockSpec((tm, n), lambda i: (i, 0))
    return pl.pallas_call(
        _merge_kernel,
        grid=(n_tok // tm,),
        in_specs=[row(D_MODEL), row(D_MODEL)] + [row(BRANCH_DIM)] * 4
        + [_resident(wg.shape), _resident(wb.shape), _resident(wo.shape), _resident(nf.shape)],
        out_specs=[row(D_MODEL), row(D_MODEL)],
        out_shape=[jax.ShapeDtypeStruct((n_tok, D_MODEL), F32), jax.ShapeDtypeStruct((n_tok, D_MODEL), BF16)],
        compiler_params=_params("parallel"),
        name="merge",
    )(x2, h, *outs, wg, wb, wo, nf)


def _ffn_kernel(tf, x1_ref, hf_ref, w1_ref, w3_ref, w2_ref, o_ref):
    hf = hf_ref[...]
    acc = x1_ref[...]
    for lo in range(0, FFN_DIM, tf):
        cols = slice(lo, min(lo + tf, FFN_DIM))
        a = jnp.dot(hf, w1_ref[:, cols], preferred_element_type=F32)
        b = jnp.dot(hf, w3_ref[:, cols], preferred_element_type=F32)
        acc = acc + jnp.dot((_silu(a) * b).astype(BF16), w2_ref[cols, :], preferred_element_type=F32)
    o_ref[...] = acc


def _ffn(x1, hf, w1, w3, w2, tm=512, tf=512):
    n_tok = x1.shape[0]
    row = pl.BlockSpec((tm, D_MODEL), lambda i: (i, 0))
    return pl.pallas_call(
        functools.partial(_ffn_kernel, tf),
        grid=(n_tok // tm,),
        in_specs=[row, row, _resident(w1.shape), _resident(w3.shape), _resident(w2.shape)],
        out_specs=row,
        out_shape=jax.ShapeDtypeStruct((n_tok, D_MODEL), F32),
        compiler_params=_params("parallel"),
        name="ffn",
    )(x1, hf, w1, w3, w2)


def _router_kernel(tm, x1_ref, nf_ref, wr_ref, idx_ref, gate_ref, rank_ref, cnt_ref, hf_ref, carry_ref):
    @pl.when(pl.program_id(0) == 0)
    def _():
        carry_ref[...] = jnp.zeros_like(carry_ref)

    x1 = x1_ref[...]
    ms = jnp.mean(x1 * x1, axis=-1, keepdims=True)
    hf = x1 * lax.rsqrt(ms + NORM_EPS) * nf_ref[...]
    for j in range(ROW_SPLIT):
        hf_ref[j] = hf[:, j * SUBROW:(j + 1) * SUBROW]
    logits = lax.dot_general(wr_ref[...], hf, (((1,), (1,)), ((), ())), precision=HIGHEST,
                             preferred_element_type=F32)
    e_iota = _iota2((N_EXPERTS, tm), 0)
    m1 = jnp.max(logits, axis=0, keepdims=True)
    i1 = jnp.min(jnp.where(logits == m1, e_iota, N_EXPERTS), axis=0, keepdims=True)
    rest = jnp.where(e_iota == i1, -jnp.inf, logits)
    m2 = jnp.max(rest, axis=0, keepdims=True)
    i2 = jnp.min(jnp.where(rest == m2, e_iota, N_EXPERTS), axis=0, keepdims=True)
    e2 = jnp.exp(m2 - m1)
    gate_ref[0:1, :] = 1.0 / (1.0 + e2)
    gate_ref[1:2, :] = e2 / (1.0 + e2)
    idx_ref[0:1, :] = i1
    idx_ref[1:2, :] = i2
    hit1 = jnp.where(e_iota == i1, 1.0, 0.0)
    hit2 = jnp.where(e_iota == i2, 1.0, 0.0)
    before = jnp.where(_iota2((tm, tm), 0) < _iota2((tm, tm), 1), 1.0, 0.0)
    prior = _dot(hit1 + hit2, before) + carry_ref[:, 0:1]
    rank_ref[0:1, :] = jnp.sum(hit1 * prior, axis=0, keepdims=True).astype(jnp.int32)
    rank_ref[1:2, :] = jnp.sum(hit2 * prior, axis=0, keepdims=True).astype(jnp.int32)
    carry_ref[...] = carry_ref[...] + jnp.sum(hit1 + hit2, axis=1, keepdims=True)
    cnt_ref[...] = carry_ref[...]


def _router(x1, nf, wr_t, tm=256):
    n_tok = x1.shape[0]
    col = pl.BlockSpec((2, tm), lambda i: (0, i))
    return pl.pallas_call(
        functools.partial(_router_kernel, tm),
        grid=(n_tok // tm,),
        in_specs=[pl.BlockSpec((tm, D_MODEL), lambda i: (i, 0)), _resident(nf.shape), _resident(wr_t.shape)],
        out_specs=[col, col, col, pl.BlockSpec((N_EXPERTS, LANES), lambda i: (0, 0)),
                   pl.BlockSpec((ROW_SPLIT, tm, SUBROW), lambda i: (0, i, 0))],
        out_shape=[jax.ShapeDtypeStruct((2, n_tok), jnp.int32), jax.ShapeDtypeStruct((2, n_tok), F32),
                   jax.ShapeDtypeStruct((2, n_tok), jnp.int32), jax.ShapeDtypeStruct((N_EXPERTS, LANES), F32),
                   jax.ShapeDtypeStruct((ROW_SPLIT, n_tok, SUBROW), F32)],
        scratch_shapes=[pltpu.VMEM((N_EXPERTS, LANES), F32)],
        compiler_params=_params("arbitrary"),
        name="moe_router",
    )(x1, nf, wr_t)


def _gather_rows(table, idx, window=128):
    split, n_table, width = table.shape
    n = idx.shape[0]
    flat_idx = (idx[None, :] + n_table * jnp.arange(split, dtype=jnp.int32)[:, None]).reshape(-1)
    return _gather_subrows(table.reshape(split * n_table, width), flat_idx, window).reshape(split, n, width)


def _gather_subrows(table, idx, window):
    n = idx.shape[0]
    d = table.shape[1]
    mesh = plsc.VectorSubcoreMesh(core_axis_name="core", subcore_axis_name="subcore")

    @functools.partial(pl.kernel, out_type=jax.ShapeDtypeStruct((n, d), table.dtype), mesh=mesh,
                       name="gather_rows")
    def gather(table_hbm, idx_hbm, out_hbm):
        def body(idx_vmem, out_vmem):
            pltpu.sync_copy(table_hbm.at[idx_vmem.at[0]], out_vmem)

        pltpu.emit_pipeline(
            body,
            grid=(n // window,),
            in_specs=[pl.BlockSpec((1, window), index_map=lambda i: (0, i))],
            out_specs=[pl.BlockSpec((window, d), index_map=lambda i: (i, 0))],
            core_axis_name=("core", "subcore"),
            dimension_semantics=(pltpu.PARALLEL,),
        )(idx_hbm, out_hbm)

    return gather(table, idx.reshape(1, n))


def _expert_kernel(tf, ge_ref, nv_ref, x_ref, w1_ref, w3_ref, w2_ref, y_ref):
    g = pl.program_id(0)

    @pl.when(g < nv_ref[0])
    def _():
        x = jnp.concatenate([x_ref[j] for j in range(ROW_SPLIT)], axis=-1).astype(BF16)
        acc = jnp.zeros((MOE_ROWS, D_MODEL), F32)
        for lo in range(0, EXPERT_DIM, tf):
            a = jnp.dot(x, w1_ref[:, lo:lo + tf], preferred_element_type=F32)
            b = jnp.dot(x, w3_ref[:, lo:lo + tf], preferred_element_type=F32)
            acc = acc + jnp.dot((_silu(a) * b).astype(BF16), w2_ref[lo:lo + tf, :], preferred_element_type=F32)
        for j in range(ROW_SPLIT):
            y_ref[j] = acc[:, j * SUBROW:(j + 1) * SUBROW]

    @pl.when(g >= nv_ref[0])
    def _():
        y_ref[...] = jnp.zeros_like(y_ref)


def _experts(xg, group_expert, n_valid, w1, w3, w2, tf=512):
    n_rows = xg.shape[1]
    rows = pl.BlockSpec((ROW_SPLIT, MOE_ROWS, SUBROW), lambda g, ge, nv: (0, g, 0))
    return pl.pallas_call(
        functools.partial(_expert_kernel, tf),
        grid_spec=pltpu.PrefetchScalarGridSpec(
            num_scalar_prefetch=2,
            grid=(n_rows // MOE_ROWS,),
            in_specs=[rows,
                      pl.BlockSpec((None, D_MODEL, EXPERT_DIM), lambda g, ge, nv: (ge[g], 0, 0)),
                      pl.BlockSpec((None, D_MODEL, EXPERT_DIM), lambda g, ge, nv: (ge[g], 0, 0)),
                      pl.BlockSpec((None, EXPERT_DIM, D_MODEL), lambda g, ge, nv: (ge[g], 0, 0))],
            out_specs=rows,
        ),
        out_shape=jax.ShapeDtypeStruct((ROW_SPLIT, n_rows, SUBROW), F32),
        compiler_params=pltpu.CompilerParams(dimension_semantics=("arbitrary",),
                                             vmem_limit_bytes=EXPERT_VMEM_LIMIT),
        name="experts",
    )(group_expert, n_valid, xg, w1, w3, w2)


def _combine_kernel(x1_ref, y0_ref, y1_ref, gate_ref, o_ref):
    gate = gate_ref[...]
    for j in range(ROW_SPLIT):
        cols = slice(j * SUBROW, (j + 1) * SUBROW)
        o_ref[:, cols] = x1_ref[:, cols] + gate[:, 0:1] * y0_ref[j] + gate[:, 1:2] * y1_ref[j]


def _combine(x1, yg, gate_t, tm=512):
    n_tok = x1.shape[0]
    nb = n_tok // tm
    return pl.pallas_call(
        _combine_kernel,
        grid=(nb,),
        in_specs=[pl.BlockSpec((tm, D_MODEL), lambda i: (i, 0)),
                  pl.BlockSpec((ROW_SPLIT, tm, SUBROW), lambda i: (0, i, 0)),
                  pl.BlockSpec((ROW_SPLIT, tm, SUBROW), lambda i: (0, i + nb, 0)),
                  pl.BlockSpec((tm, 2), lambda i: (i, 0))],
        out_specs=pl.BlockSpec((tm, D_MODEL), lambda i: (i, 0)),
        out_shape=jax.ShapeDtypeStruct((n_tok, D_MODEL), F32),
        compiler_params=_params("parallel"),
        name="moe_combine",
    )(x1, yg, yg, gate_t)


def _moe(x1, nf, w_router, w1, w3, w2):
    n_tok = x1.shape[0]
    idx, gate, rank, cnt, hf = _router(x1, nf, w_router.T)
    counts = cnt[:, 0].astype(jnp.int32)
    padded = (counts + MOE_ROWS - 1) // MOE_ROWS * MOE_ROWS
    end_padded = jnp.cumsum(padded)
    start_padded = end_padded - padded
    start = sum(jnp.where(idx == e, start_padded[e], 0) for e in range(N_EXPERTS))
    dest = (start + rank).reshape(-1)
    n_groups = (n_tok * 2 + MOE_ROWS - 1) // MOE_ROWS + N_EXPERTS
    n_rows = n_groups * MOE_ROWS
    group_row = jnp.arange(n_groups, dtype=jnp.int32)[:, None] * MOE_ROWS
    group_expert = jnp.minimum(jnp.sum(group_row >= end_padded[None, :], axis=1), N_EXPERTS - 1).astype(jnp.int32)
    n_valid = (end_padded[-1:] // MOE_ROWS).astype(jnp.int32)
    tok = jnp.tile(jnp.arange(n_tok, dtype=jnp.int32), 2)
    row_token = jnp.zeros((n_rows,), jnp.int32).at[dest].set(tok)
    xg = _gather_rows(hf, row_token)
    y = _experts(xg, group_expert, n_valid, w1, w3, w2)
    return _combine(x1, _gather_rows(y, dest), gate.T)


def _row(v, pad=0):
    v = v.reshape(1, -1).astype(F32)
    return jnp.pad(v, ((0, 0), (0, pad))) if pad else v


def _rows_at(w, start, total):
    return jnp.pad(w.astype(F32), ((start, total - start - w.shape[0]), (0, 0)))


def kernel(x, w_in, norm_mix, rwkv_mu, rwkv_decay_up, rwkv_w0, rwkv_a_up, rwkv_a0, rwkv_gate_up, rwkv_k_k, rwkv_k_a, rwkv_r_k, rwkv_ln_w, rwkv_ln_b, vres_down, vres_up, vres_v0, ssm_conv_w, ssm_conv_b, ssm_dt_bias, ssm_a_log, ssm_d, ssm_norm_w, att_q_gain, att_k_gain, att_rel_bias, gla_gate_up, gla_gate_bias, gla_norm_w, w_branch, w_out, norm_ffn, ffn_w1, ffn_w3, ffn_w2, moe_router, moe_w1, moe_w3, moe_w2):
    bsz, seq, _ = x.shape
    n_tok = bsz * seq
    depth = w_in.shape[0]
    x2 = x.reshape(n_tok, D_MODEL)
    v_first = None
    o_rw, o_ss, o_at = RWKV_COLS, RWKV_COLS + SSM_COLS, RWKV_COLS + SSM_COLS + ATT_COLS
    o_gl = o_at + GLA_COLS
    for l in range(depth):
        w = w_in[l]
        vd = vres_down[l - 1] if l > 0 else jnp.zeros((D_MODEL, 32), F32)
        wa = jnp.concatenate([w[:, :o_rw], vd, jnp.zeros((D_MODEL, RWKV_PAD - RWKV_COLS - 32), F32)], axis=1)
        wb = jnp.pad(w[:, o_rw:o_ss], ((0, 0), (0, SSM_PAD - SSM_COLS)))
        wc = w[:, o_ss:o_at]
        gl = w[:, o_at:o_gl]
        wd = jnp.concatenate([gl[:, 0:1024], gl[:, 1040:1552], gl[:, 1024:1040],
                              jnp.zeros((D_MODEL, GLA_PAD - GLA_COLS), F32)], axis=1)
        wg = w[:, o_gl:]
        h, za, zb, zc, zd = _inproj(x2, _row(norm_mix[l]), wa.astype(BF16), wb.astype(BF16),
                                    wc.astype(BF16), wd.astype(BF16))
        shp = lambda z: z.reshape(bsz, seq, z.shape[-1])

        rw = dict(mu=_row(rwkv_mu[l], RWKV_PAD - RWKV_COLS),
                  dup=_rows_at(rwkv_decay_up[l], 0, LANES), w0=_row(rwkv_w0[l]),
                  aup=_rows_at(rwkv_a_up[l], 64, LANES), a0=_row(rwkv_a0[l]),
                  gup=_rows_at(rwkv_gate_up[l], 0, 256).astype(BF16),
                  kk=_row(rwkv_k_k[l]), ka=_row(rwkv_k_a[l]), rk=_row(rwkv_r_k[l]),
                  lnw=_row(rwkv_ln_w[l]), lnb=_row(rwkv_ln_b[l]))
        if l > 0:
            rw.update(vup=_rows_at(vres_up[l - 1], 160, 256), v0=_row(vres_v0[l - 1]))
        o_a, v_first = _rwkv(shp(za), v_first, rw)
        o_b = _mamba(shp(zb), dict(cw=ssm_conv_w[l], cb=_row(ssm_conv_b[l]),
                                   dtb=_row(ssm_dt_bias[l], LANES - SSM_HEADS),
                                   alog=_row(ssm_a_log[l], LANES - SSM_HEADS),
                                   dexp=_row(jnp.repeat(ssm_d[l], CHUNK)), nw=_row(ssm_norm_w[l])))
        o_c = _attention(shp(zc), att_q_gain[l], att_k_gain[l], att_rel_bias)
        o_d = _gla(shp(zd), dict(gup=_rows_at(gla_gate_up[l], 0, LANES), gb=_row(gla_gate_bias[l]),
                                 nw=_row(gla_norm_w[l])))
        outs = [o.reshape(n_tok, BRANCH_DIM) for o in (o_a, o_b, o_c, o_d)]
        x1, hf = _merge(x2, h, outs, wg.astype(BF16), w_branch[l].astype(BF16), w_out[l].astype(BF16),
                        _row(norm_ffn[l]))
        if l % 2 == 0:
            x2 = _ffn(x1, hf, ffn_w1[l // 2].astype(BF16), ffn_w3[l // 2].astype(BF16),
                      ffn_w2[l // 2].astype(BF16))
        else:
            x2 = _moe(x1, _row(norm_ffn[l]), moe_router[l // 2], moe_w1[l // 2].astype(BF16),
                      moe_w3[l // 2].astype(BF16), moe_w2[l // 2].astype(BF16))
    return x2.reshape(bsz, seq, D_MODEL)
```

```python
import functools

import jax
import jax.numpy as jnp
from jax import lax
from jax.experimental import pallas as pl
from jax.experimental.pallas import tpu as pltpu
from jax.experimental.pallas import tpu_sc as plsc

F32 = jnp.float32
BF16 = jnp.bfloat16
HIGHEST = lax.Precision.HIGHEST

D_MODEL = 1024
CHUNK = 64
BRANCH_DIM = 512
NORM_EPS = 1e-6
LANES = 128
VMEM_LIMIT = 56 * 1024 * 1024
EXPERT_VMEM_LIMIT = 60 * 1024 * 1024

RWKV_LN_EPS = 64e-5
RWKV_COLS = 1824
RWKV_PAD = 1920
SSM_COLS = 1544
SSM_PAD = 1664
SSM_HEADS = 8
SSM_STATE = 128
ATT_COLS = 1536
ATT_HEADS = 8
ATT_LEFT = 8 * CHUNK
REL_CLIP = 2 * CHUNK
GLA_COLS = 1552
GLA_PAD = 1664
GLA_GATE_NORM = 16.0
FFN_DIM = 2816
N_EXPERTS = 8
EXPERT_DIM = 3584
MOE_ROWS = 256
ROW_SPLIT = 4
SUBROW = D_MODEL // ROW_SPLIT


def _dot(a, b):
    return jnp.dot(a.astype(BF16), b.astype(BF16), preferred_element_type=F32)


def _dot_nt(a, b):
    return lax.dot_general(a.astype(BF16), b.astype(BF16), (((1,), (1,)), ((), ())),
                           preferred_element_type=F32)


def _dot_tn(a, b):
    return lax.dot_general(a.astype(BF16), b.astype(BF16), (((0,), (0,)), ((), ())),
                           preferred_element_type=F32)


def _hi_lo(a):
    hi = a.astype(BF16)
    return hi, (a - hi.astype(F32)).astype(BF16)


def _dot_3x(a, b):
    a_hi, a_lo = _hi_lo(a)
    b_hi, b_lo = _hi_lo(b)
    return (jnp.dot(a_hi, b_hi, preferred_element_type=F32) + jnp.dot(a_lo, b_hi, preferred_element_type=F32)
            + jnp.dot(a_hi, b_lo, preferred_element_type=F32))


def _dot_split(a, m):
    m = m.astype(BF16)
    hi, lo = _hi_lo(a)
    return jnp.dot(hi, m, preferred_element_type=F32) + jnp.dot(lo, m, preferred_element_type=F32)


def _dot_split_l(m, a):
    m = m.astype(BF16)
    hi, lo = _hi_lo(a)
    return jnp.dot(m, hi, preferred_element_type=F32) + jnp.dot(m, lo, preferred_element_type=F32)


def _softplus(x):
    return jnp.maximum(x, 0.0) + jnp.log(1.0 + jnp.exp(-jnp.abs(x)))


def _sigmoid(x):
    return 1.0 / (1.0 + jnp.exp(-x))


def _silu(x):
    return x * _sigmoid(x)


def _iota2(shape, axis):
    return lax.broadcasted_iota(jnp.int32, shape, axis)


def _group_mean_matrix(n, group):
    r = _iota2((n, n), 0) // group
    c = _iota2((n, n), 1) // group
    return jnp.where(r == c, 1.0 / group, 0.0).astype(F32)


def _chunk_tri(n):
    r = _iota2((n, n), 0)
    c = _iota2((n, n), 1)
    return jnp.where((r // CHUNK == c // CHUNK) & (r >= c), 1.0, 0.0).astype(F32)


def _resident(shape):
    nd = len(shape)
    return pl.BlockSpec(shape, lambda *_: (0,) * nd, pipeline_mode=pl.Buffered(1))


def _params(*sem):
    return pltpu.CompilerParams(dimension_semantics=sem, vmem_limit_bytes=VMEM_LIMIT)


def _inproj_kernel(x_ref, g_ref, wa_ref, wb_ref, wc_ref, wd_ref,
                   h_ref, za_ref, zb_ref, zc_ref, zd_ref):
    x = x_ref[...]
    ms = jnp.mean(x * x, axis=-1, keepdims=True)
    h = (x * lax.rsqrt(ms + NORM_EPS) * g_ref[...]).astype(BF16)
    h_ref[...] = h
    za_ref[...] = jnp.dot(h, wa_ref[...], preferred_element_type=F32)
    zb_ref[...] = jnp.dot(h, wb_ref[...], preferred_element_type=F32)
    zc_ref[...] = jnp.dot(h, wc_ref[...], preferred_element_type=F32)
    zd_ref[...] = jnp.dot(h, wd_ref[...], preferred_element_type=F32)


def _inproj(x2, g, wa, wb, wc, wd, tm=512):
    n_tok = x2.shape[0]
    row = lambda n: pl.BlockSpec((tm, n), lambda i: (i, 0))
    outs = (D_MODEL, RWKV_PAD, SSM_PAD, ATT_COLS, GLA_PAD)
    return pl.pallas_call(
        _inproj_kernel,
        grid=(n_tok // tm,),
        in_specs=[row(D_MODEL), _resident((1, D_MODEL)), _resident(wa.shape), _resident(wb.shape),
                  _resident(wc.shape), _resident(wd.shape)],
        out_specs=[row(n) for n in outs],
        out_shape=[jax.ShapeDtypeStruct((n_tok, outs[0]), BF16)]
        + [jax.ShapeDtypeStruct((n_tok, n), F32) for n in outs[1:]],
        compiler_params=_params("parallel"),
        name="inproj",
    )(x2, g, wa, wb, wc, wd)


def _rwkv_block(r, k, v, kk, ka, ld, tb, state_ref, y_ref):
    n_chunks = tb // CHUNK
    cum = _dot_split_l(_chunk_tri(tb), ld)
    p_inv = jnp.exp(-cum)
    a_t = -kk * jnp.exp(cum - ld)
    b_t = kk * ka * p_inv
    k_t = k * p_inv
    r_t = r * jnp.exp(cum)

    lane_head = _iota2((CHUNK, LANES), 1) // CHUNK
    row = _iota2((CHUNK, LANES), 0)
    col = _iota2((CHUNK, LANES), 1) % CHUNK
    strict = row > col
    incl = row >= col
    bd_mask = (_iota2((LANES, LANES), 0) // CHUNK) == (_iota2((LANES, LANES), 1) // CHUNK)
    eye = jnp.where(_iota2((LANES, LANES), 0) == _iota2((LANES, LANES), 1), 1.0, 0.0).astype(F32)

    def stack(z):
        return jnp.concatenate([jnp.where(lane_head == 0, z, 0.0), jnp.where(lane_head == 1, z, 0.0)], axis=0)

    def sub(z, u):
        c, p = u
        return z[c * CHUNK:(c + 1) * CHUNK, p * LANES:(p + 1) * LANES]

    units = [(c, p) for c in range(n_chunks) for p in range(4)]
    bf = lambda z: z.astype(BF16)
    ar = {u: bf(jnp.concatenate([sub(a_t, u), sub(r_t, u)], axis=0)) for u in units}
    g_b = {u: _dot_nt(ar[u], stack(sub(b_t, u))) for u in units}
    g_k = {u: _dot_nt(ar[u], stack(sub(k_t, u))) for u in units}
    a_rb = {u: bf(jnp.where(incl, g_b[u][CHUNK:], 0.0)) for u in units}
    a_rk = {u: bf(jnp.where(incl, g_k[u][CHUNK:], 0.0)) for u in units}
    a_ak = {u: bf(jnp.where(strict, g_k[u][:CHUNK], 0.0)) for u in units}
    pw = {u: stack(jnp.where(strict, g_b[u][:CHUNK], 0.0)) for u in units}
    t_bd = {u: eye + pw[u] for u in units}
    for _ in range(CHUNK.bit_length() - 2):
        pw = {u: _dot(pw[u], pw[u]) for u in units}
        t_bd = {u: t_bd[u] + _dot(t_bd[u], pw[u]) for u in units}
    t_bd = {u: bf(t_bd[u]) for u in units}
    v_bd = {u: bf(stack(sub(v, u))) for u in units}
    w_bd = {u: bf(_dot(t_bd[u], stack(sub(a_t, u)))) for u in units}
    av = {u: _dot(a_ak[u], v_bd[u]) for u in units}
    uv_bd = {u: _dot(t_bd[u], stack(av[u])) for u in units}
    y_v = {u: _dot(a_rk[u], v_bd[u]) for u in units}

    h = [state_ref[p] for p in range(4)]
    for c in range(n_chunks):
        last = c * CHUNK + CHUNK - 1
        p_last = jnp.exp(cum[last:last + 1, :])
        hb = [bf(h[p]) for p in range(4)]
        u2 = [_dot_nt(w_bd[(c, p)], hb[p]) + uv_bd[(c, p)] for p in range(4)]
        u_p = [u2[p][:CHUNK] + u2[p][CHUNK:] for p in range(4)]
        upd = [_dot_tn(jnp.concatenate([u_p[p], sub(v, (c, p))], axis=0),
                       jnp.concatenate([sub(b_t, (c, p)), sub(k_t, (c, p))], axis=0)
                       * p_last[:, p * LANES:(p + 1) * LANES]) for p in range(4)]
        for p in range(4):
            y_ref[c * CHUNK:(c + 1) * CHUNK, p * LANES:(p + 1) * LANES] = (
                _dot_nt(sub(r_t, (c, p)), hb[p]) + _dot(a_rb[(c, p)], stack(u_p[p])) + y_v[(c, p)])
        h = [h[p] * p_last[:, p * LANES:(p + 1) * LANES] + jnp.where(bd_mask, upd[p], 0.0) for p in range(4)]
    for p in range(4):
        state_ref[p] = h[p]


def _rwkv_kernel(has_vres, tb, *refs):
    if has_vres:
        (za_ref, vfirst_ref, mu_ref, dup_ref, w0_ref, aup_ref, a0_ref, gup_ref, vup_ref, v0_ref,
         kk_ref, ka_ref, rk_ref, lnw_ref, lnb_ref, o_ref, prev_ref, state_ref, y_ref) = refs
    else:
        (za_ref, mu_ref, dup_ref, w0_ref, aup_ref, a0_ref, gup_ref,
         kk_ref, ka_ref, rk_ref, lnw_ref, lnb_ref, o_ref, vraw_ref, prev_ref, state_ref, y_ref) = refs

    @pl.when(pl.program_id(1) == 0)
    def _():
        prev_ref[...] = jnp.zeros_like(prev_ref)
        state_ref[...] = jnp.zeros_like(state_ref)

    za = za_ref[...]
    shifted = pltpu.roll(za, 1, axis=0)
    shifted = jnp.where(_iota2(za.shape, 0) == 0, prev_ref[7:8, :], shifted)
    prev_ref[...] = za[tb - 8:, :]
    za = za + (shifted - za) * mu_ref[...]

    r = za[:, 0:512]
    k = za[:, 512:1024]
    v = za[:, 1024:1536]
    xwa = za[:, 1536:1664]
    xg = za[:, 1664:1920]
    w_log = -_softplus(-(w0_ref[...] + _dot_3x(jnp.tanh(xwa), dup_ref[...]))) - 0.5
    ld = -jnp.exp(w_log)
    a = _sigmoid(a0_ref[...] + _dot(xwa, aup_ref[...]))
    g = _dot(_sigmoid(xg), gup_ref[...])
    if has_vres:
        v_mix = _sigmoid(v0_ref[...] + _dot(xg, vup_ref[...]))
        v = v + (vfirst_ref[...] - v) * v_mix
    else:
        vraw_ref[...] = v
    kk = k * kk_ref[...]
    k = k * (1.0 + (a - 1.0) * ka_ref[...])
    head_sum = _group_mean_matrix(512, CHUNK) * float(CHUNK)
    kk = kk / jnp.maximum(jnp.sqrt(_dot(kk * kk, head_sum)), 1e-12)

    _rwkv_block(r, k, v, kk, a, ld, tb, state_ref, y_ref)
    y = y_ref[...]
    head_mean = _group_mean_matrix(512, CHUNK)
    mean = _dot_split(y, head_mean)
    yc = y - mean
    var = _dot(yc * yc, head_mean)
    y = yc * lax.rsqrt(var + RWKV_LN_EPS) * lnw_ref[...] + lnb_ref[...]
    y = y + _dot(r * k * rk_ref[...], head_sum) * v
    o_ref[...] = (y * g).astype(o_ref.dtype)


def _rwkv(za, v_first, prm, tb=256):
    bsz, seq, _ = za.shape
    has_vres = v_first is not None
    blk = lambda n: pl.BlockSpec((None, tb, n), lambda b, i: (b, i, 0))
    names = (["mu", "dup", "w0", "aup", "a0", "gup"] + (["vup", "v0"] if has_vres else [])
             + ["kk", "ka", "rk", "lnw", "lnb"])
    weights = [prm[n] for n in names]
    ins = [za] + ([v_first] if has_vres else []) + weights
    in_specs = [blk(RWKV_PAD)] + ([blk(512)] if has_vres else []) + [_resident(w.shape) for w in weights]
    out_shape = [jax.ShapeDtypeStruct((bsz, seq, 512), BF16)]
    out_specs = [blk(512)]
    if not has_vres:
        out_shape.append(jax.ShapeDtypeStruct((bsz, seq, 512), F32))
        out_specs.append(blk(512))
    res = pl.pallas_call(
        functools.partial(_rwkv_kernel, has_vres, tb),
        grid=(bsz, seq // tb),
        in_specs=in_specs,
        out_specs=out_specs,
        out_shape=out_shape,
        scratch_shapes=[pltpu.VMEM((8, RWKV_PAD), F32), pltpu.VMEM((4, LANES, LANES), F32),
                        pltpu.VMEM((tb, 512), F32)],
        compiler_params=_params("parallel", "arbitrary"),
        name="rwkv7",
    )(*ins)
    return (res[0], v_first) if has_vres else (res[0], res[1])


def _mamba_kernel(tb, zb_ref, cw_ref, cb_ref, dtb_ref, alog_ref, dexp_ref, nw_ref,
                  o_ref, xbuf_ref, state_ref, y_ref):
    @pl.when(pl.program_id(1) == 0)
    def _():
        xbuf_ref[0:8, :] = jnp.zeros((8, 1024), F32)
        state_ref[...] = jnp.zeros_like(state_ref)

    xbuf_ref[8:8 + tb, :] = zb_ref[:, 512:1536]
    conv = cb_ref[...]
    for i in range(4):
        conv = conv + cw_ref[i:i + 1, :] * xbuf_ref[5 + i:5 + i + tb, :]
    xbuf_ref[0:8, :] = xbuf_ref[tb:tb + 8, :]
    xbc = _silu(conv)
    xs = xbc[:, 0:512]
    bm = xbc[:, 512:768]
    cm = xbc[:, 768:1024]

    lane = _iota2((1, LANES), 1)
    dt = _softplus(zb_ref[:, 1536:1664] + dtb_ref[...])
    a_neg = jnp.where(lane < SSM_HEADS, -jnp.exp(alog_ref[...]), 0.0)
    acs = _dot_split_l(_chunk_tri(tb), dt * a_neg)
    expand = jnp.where(_iota2((LANES, 512), 0) == _iota2((LANES, 512), 1) // CHUNK, 1.0, 0.0)
    dt_e = _dot(dt, expand)
    acs_e = _dot_split(acs, expand)
    xdt = xs * dt_e

    n_chunks = tb // CHUNK
    causal = _iota2((CHUNK, 512), 0) >= _iota2((CHUNK, 512), 1) % CHUNK
    pair_head = _iota2((CHUNK, LANES), 1) // CHUNK
    spread = jnp.where(_iota2((CHUNK, 512), 0) == _iota2((CHUNK, 512), 1) % CHUNK, 1.0, 0.0)
    own_head = _iota2((SSM_HEADS, 512), 0) == _iota2((SSM_HEADS, 512), 1) // CHUNK
    rows_of = lambda c: slice(c * CHUNK, (c + 1) * CHUNK)
    grp = lambda g: slice(g * SSM_STATE, (g + 1) * SSM_STATE)

    def stack(z):
        return jnp.concatenate([jnp.where(pair_head == 0, z, 0.0), jnp.where(pair_head == 1, z, 0.0)], axis=0)

    y_intra, upd, e_in, e_last = [], [], [], []
    for c in range(n_chunks):
        rows = rows_of(c)
        acs_ec = acs_e[rows]
        acs_t = acs[rows].T[0:SSM_HEADS]
        acs_row = jnp.sum(jnp.where(own_head, _dot_split(acs_t, spread), 0.0), axis=0, keepdims=True)
        decay = jnp.exp(jnp.where(causal, acs_ec - acs_row, -jnp.inf))
        cb = jnp.concatenate(
            [_dot_nt(cm[rows, grp(g)], jnp.concatenate([bm[rows, grp(g)]] * 4, axis=0)) for g in range(2)], axis=1)
        m = cb * decay
        xdt_c = xdt[rows]
        y_intra.append([_dot(m[:, p * LANES:(p + 1) * LANES], stack(xdt_c[:, p * LANES:(p + 1) * LANES]))
                        for p in range(4)])
        last_e = acs_ec[CHUNK - 1:CHUNK, :]
        x_out = xdt_c * jnp.exp(last_e - acs_ec)
        upd.append([_dot_tn(bm[rows, grp(g)], x_out[:, g * 256:(g + 1) * 256]) for g in range(2)])
        e_in.append(jnp.exp(acs_ec))
        e_last.append(jnp.exp(last_e))

    state = [state_ref[g] for g in range(2)]
    for c in range(n_chunks):
        rows = rows_of(c)
        for g in range(2):
            ls = slice(g * 256, (g + 1) * 256)
            y_in = jnp.concatenate(y_intra[c][2 * g:2 * g + 2], axis=1)
            y_ref[rows, ls] = y_in + _dot(cm[rows, grp(g)], state[g]) * e_in[c][:, ls]
            state[g] = state[g] * e_last[c][:, ls] + upd[c][g]
    for g in range(2):
        state_ref[g] = state[g]

    y = (y_ref[...] + xs * dexp_ref[...]) * _silu(zb_ref[:, 0:512])
    for g in range(2):
        ls = slice(g * 256, (g + 1) * 256)
        yg = y[:, ls]
        ms = jnp.mean(yg * yg, axis=-1, keepdims=True)
        o_ref[:, ls] = (yg * lax.rsqrt(ms + NORM_EPS) * nw_ref[:, ls]).astype(o_ref.dtype)


def _mamba(zb, prm, tb=256):
    bsz, seq, _ = zb.shape
    blk = lambda n: pl.BlockSpec((None, tb, n), lambda b, i: (b, i, 0))
    weights = [prm[n] for n in ("cw", "cb", "dtb", "alog", "dexp", "nw")]
    return pl.pallas_call(
        functools.partial(_mamba_kernel, tb),
        grid=(bsz, seq // tb),
        in_specs=[blk(SSM_PAD)] + [_resident(w.shape) for w in weights],
        out_specs=blk(512),
        out_shape=jax.ShapeDtypeStruct((bsz, seq, 512), BF16),
        scratch_shapes=[pltpu.VMEM((tb + 8, 1024), F32), pltpu.VMEM((2, SSM_STATE, 256), F32),
                        pltpu.VMEM((tb, 512), F32)],
        compiler_params=_params("parallel", "arbitrary"),
        name="mamba2",
    )(zb, *weights)


def _attn_kernel(tq, seq, q_ref, k_ref, v_ref, qg_ref, kg_ref, bias_ref, o_ref, kn_ref, vb_ref):
    i = pl.program_id(1)
    head_mean = _group_mean_matrix(512, CHUNK)
    win = tq + ATT_LEFT

    @pl.when(i == 0)
    def _():
        kn_ref[0:ATT_LEFT, :] = jnp.zeros((ATT_LEFT, 512), BF16)
        vb_ref[0:ATT_LEFT, :] = jnp.zeros((ATT_LEFT, 512), BF16)
        for j in range(seq // tq):
            kj = k_ref[j * tq:(j + 1) * tq, :]
            ms = _dot(kj * kj, head_mean)
            kn_ref[ATT_LEFT + j * tq:ATT_LEFT + (j + 1) * tq, :] = (
                kj * lax.rsqrt(ms + NORM_EPS) * kg_ref[...]).astype(BF16)
            vb_ref[ATT_LEFT + j * tq:ATT_LEFT + (j + 1) * tq, :] = v_ref[j * tq:(j + 1) * tq, :].astype(BF16)

    q = q_ref[...]
    ms = _dot(q * q, head_mean)
    qn = q * lax.rsqrt(ms + NORM_EPS) * qg_ref[...] * (CHUNK ** -0.5)
    start = pl.multiple_of(i * tq, tq)
    kwin = kn_ref[pl.ds(start, win), :]
    vwin = vb_ref[pl.ds(start, win), :]
    qc = ATT_LEFT // CHUNK + _iota2((tq, win), 0) // CHUNK
    kc = _iota2((tq, win), 1) // CHUNK
    first_kc = jnp.maximum(qc - ATT_LEFT // CHUNK, ATT_LEFT // CHUNK - i * (tq // CHUNK))
    valid = (kc <= qc) & (kc >= first_kc)
    lane_head = _iota2((tq, LANES), 1) // CHUNK
    for p in range(4):
        ls = slice(p * LANES, (p + 1) * LANES)
        acc = jnp.zeros((tq, LANES), F32)
        for s in range(2):
            qh = jnp.where(lane_head == s, qn[:, ls], 0.0)
            sc = _dot_nt(qh, kwin[:, ls]) + bias_ref[2 * p + s]
            sc = jnp.where(valid, sc, -jnp.inf)
            e = jnp.exp(sc - jnp.max(sc, axis=-1, keepdims=True))
            oh = _dot(e, vwin[:, ls]) / jnp.sum(e, axis=-1, keepdims=True)
            acc = jnp.where(lane_head == s, oh, acc)
        o_ref[:, ls] = acc.astype(o_ref.dtype)


def _band_bias(rel_bias, tq, win):
    period = tq + win
    m = jnp.arange(period)
    d = jnp.where(m < win, m, m - period) - ATT_LEFT
    f = rel_bias.astype(F32)[:, jnp.clip(d, -REL_CLIP, REL_CLIP) + REL_CLIP]
    g = jnp.tile(f, (1, tq))[:, :tq * (period - 1)].reshape(-1, tq, period - 1)
    return g[:, :, :win]


def _attention(zc, q_gain, k_gain, rel_bias, tq=256):
    bsz, seq, _ = zc.shape
    win = tq + ATT_LEFT
    bias = _band_bias(rel_bias, tq, win)
    qg = jnp.tile(q_gain, ATT_HEADS)[None, :]
    kg = jnp.tile(k_gain, ATT_HEADS)[None, :]
    full = lambda col: pl.BlockSpec((None, seq, 512), lambda b, i: (b, 0, col))
    return pl.pallas_call(
        functools.partial(_attn_kernel, tq, seq),
        grid=(bsz, seq // tq),
        in_specs=[pl.BlockSpec((None, tq, 512), lambda b, i: (b, i, 0)), full(1), full(2),
                  _resident(qg.shape), _resident(kg.shape), _resident(bias.shape)],
        out_specs=pl.BlockSpec((None, tq, 512), lambda b, i: (b, i, 0)),
        out_shape=jax.ShapeDtypeStruct((bsz, seq, 512), BF16),
        scratch_shapes=[pltpu.VMEM((ATT_LEFT + seq, 512), BF16), pltpu.VMEM((ATT_LEFT + seq, 512), BF16)],
        compiler_params=_params("parallel", "arbitrary"),
        name="band_attention",
    )(zc, zc, zc, qg, kg, bias)


def _gla_kernel(tb, zd_ref, gup_ref, gb_ref, nw_ref, o_ref, state_ref, y_ref):
    @pl.when(pl.program_id(1) == 0)
    def _():
        state_ref[...] = jnp.zeros_like(state_ref)

    q = zd_ref[:, 0:256] * (CHUNK ** -0.5)
    k = zd_ref[:, 256:512]
    v = zd_ref[:, 512:1024]
    log_a = -_softplus(-(_dot_3x(zd_ref[:, 1536:1664], gup_ref[...]) + gb_ref[...])) / GLA_GATE_NORM
    bcum = _dot_split_l(_chunk_tri(tb), log_a)

    k_head = _iota2((CHUNK, 256), 1) // CHUNK
    v_head = _iota2((CHUNK, 512), 1) // LANES
    causal = _iota2((CHUNK, 256), 0) >= _iota2((CHUNK, 256), 1) % CHUNK
    bd = (_iota2((512, 256), 0) // LANES) == (_iota2((512, 256), 1) // CHUNK)
    n_chunks = tb // CHUNK
    qg_all = (q * jnp.exp(bcum)).astype(BF16)
    kg_all = k * jnp.exp(-bcum)
    qg, att, o_intra, upd, e_last = [], [], [], [], []
    for c in range(n_chunks):
        rows = slice(c * CHUNK, (c + 1) * CHUNK)
        kg_bd = jnp.concatenate([jnp.where(k_head == h, kg_all[rows], 0.0) for h in range(4)], axis=0)
        qg.append(qg_all[rows])
        att.append(jnp.where(causal, _dot_nt(qg[c], kg_bd), 0.0))
    for c in range(n_chunks):
        rows = slice(c * CHUNK, (c + 1) * CHUNK)
        bc = bcum[rows]
        blast = bc[CHUNK - 1:CHUNK, :]
        v_c = v[rows]
        v_bd = jnp.concatenate([jnp.where(v_head == h, v_c, 0.0) for h in range(4)], axis=0)
        o_intra.append(_dot(att[c], v_bd))
        upd.append(jnp.where(bd, _dot_tn(v_c, k[rows] * jnp.exp(blast - bc)), 0.0))
        e_last.append(jnp.exp(blast))
    st = state_ref[...]
    for c in range(n_chunks):
        y_ref[c * CHUNK:(c + 1) * CHUNK, :] = o_intra[c] + _dot_nt(qg[c], st)
        st = st * e_last[c] + upd[c]
    state_ref[...] = st

    o = y_ref[...]
    for h in range(4):
        ls = slice(h * LANES, (h + 1) * LANES)
        oh = o[:, ls]
        ms = jnp.mean(oh * oh, axis=-1, keepdims=True)
        o_ref[:, ls] = (oh * lax.rsqrt(ms + NORM_EPS) * nw_ref[...]
                        * _silu(zd_ref[:, 1024 + h * LANES:1024 + (h + 1) * LANES])).astype(o_ref.dtype)


def _gla(zd, prm, tb=256):
    bsz, seq, _ = zd.shape
    blk = lambda n: pl.BlockSpec((None, tb, n), lambda b, i: (b, i, 0))
    weights = [prm[n] for n in ("gup", "gb", "nw")]
    return pl.pallas_call(
        functools.partial(_gla_kernel, tb),
        grid=(bsz, seq // tb),
        in_specs=[blk(GLA_PAD)] + [_resident(w.shape) for w in weights],
        out_specs=blk(512),
        out_shape=jax.ShapeDtypeStruct((bsz, seq, 512), BF16),
        scratch_shapes=[pltpu.VMEM((512, 256), F32), pltpu.VMEM((tb, 512), F32)],
        compiler_params=_params("parallel", "arbitrary"),
        name="gla",
    )(zd, *weights)


def _merge_kernel(x_ref, h_ref, oa_ref, ob_ref, oc_ref, od_ref, wg_ref, wb_ref, wo_ref, nf_ref,
                  x1_ref, hf_ref):
    h = h_ref[...]
    acc = jnp.zeros(x_ref.shape, F32)
    for i, o_ref in enumerate((oa_ref, ob_ref, oc_ref, od_ref)):
        gate = _sigmoid(jnp.dot(h, wg_ref[:, i * D_MODEL:(i + 1) * D_MODEL], preferred_element_type=F32))
        acc = acc + gate * jnp.dot(o_ref[...], wb_ref[i], preferred_element_type=F32)
    x1 = x_ref[...] + jnp.dot(acc.astype(BF16), wo_ref[...], preferred_element_type=F32)
    x1_ref[...] = x1
    ms = jnp.mean(x1 * x1, axis=-1, keepdims=True)
    hf_ref[...] = (x1 * lax.rsqrt(ms + NORM_EPS) * nf_ref[...]).astype(BF16)


def _merge(x2, h, outs, wg, wb, wo, nf, tm=512):
    n_tok = x2.shape[0]
    row = lambda n: pl.BlockSpec((tm, n), lambda i: (i, 0))
    return pl.pallas_call(
        _merge_kernel,
        grid=(n_tok // tm,),
        in_specs=[row(D_MODEL), row(D_MODEL)] + [row(BRANCH_DIM)] * 4
        + [_resident(wg.shape), _resident(wb.shape), _resident(wo.shape), _resident(nf.shape)],
        out_specs=[row(D_MODEL), row(D_MODEL)],
        out_shape=[jax.ShapeDtypeStruct((n_tok, D_MODEL), F32), jax.ShapeDtypeStruct((n_tok, D_MODEL), BF16)],
        compiler_params=_params("parallel"),
        name="merge",
    )(x2, h, *outs, wg, wb, wo, nf)


def _ffn_kernel(tf, x1_ref, hf_ref, w1_ref, w3_ref, w2_ref, o_ref):
    hf = hf_ref[...]
    acc = x1_ref[...]
    for lo in range(0, FFN_DIM, tf):
        cols = slice(lo, min(lo + tf, FFN_DIM))
        a = jnp.dot(hf, w1_ref[:, cols], preferred_element_type=F32)
        b = jnp.dot(hf, w3_ref[:, cols], preferred_element_type=F32)
        acc = acc + jnp.dot((_silu(a) * b).astype(BF16), w2_ref[cols, :], preferred_element_type=F32)
    o_ref[...] = acc


def _ffn(x1, hf, w1, w3, w2, tm=512, tf=512):
    n_tok = x1.shape[0]
    row = pl.BlockSpec((tm, D_MODEL), lambda i: (i, 0))
    return pl.pallas_call(
        functools.partial(_ffn_kernel, tf),
        grid=(n_tok // tm,),
        in_specs=[row, row, _resident(w1.shape), _resident(w3.shape), _resident(w2.shape)],
        out_specs=row,
        out_shape=jax.ShapeDtypeStruct((n_tok, D_MODEL), F32),
        compiler_params=_params("parallel"),
        name="ffn",
    )(x1, hf, w1, w3, w2)


def _router_kernel(tm, x1_ref, nf_ref, wr_ref, idx_ref, gate_ref, rank_ref, cnt_ref, hf_ref, carry_ref):
    @pl.when(pl.program_id(0) == 0)
    def _():
        carry_ref[...] = jnp.zeros_like(carry_ref)

    x1 = x1_ref[...]
    ms = jnp.mean(x1 * x1, axis=-1, keepdims=True)
    hf = x1 * lax.rsqrt(ms + NORM_EPS) * nf_ref[...]
    for j in range(ROW_SPLIT):
        hf_ref[j] = hf[:, j * SUBROW:(j + 1) * SUBROW]
    logits = lax.dot_general(wr_ref[...], hf, (((1,), (1,)), ((), ())), precision=HIGHEST,
                             preferred_element_type=F32)
    e_iota = _iota2((N_EXPERTS, tm), 0)
    m1 = jnp.max(logits, axis=0, keepdims=True)
    i1 = jnp.min(jnp.where(logits == m1, e_iota, N_EXPERTS), axis=0, keepdims=True)
    rest = jnp.where(e_iota == i1, -jnp.inf, logits)
    m2 = jnp.max(rest, axis=0, keepdims=True)
    i2 = jnp.min(jnp.where(rest == m2, e_iota, N_EXPERTS), axis=0, keepdims=True)
    e2 = jnp.exp(m2 - m1)
    gate_ref[0:1, :] = 1.0 / (1.0 + e2)
    gate_ref[1:2, :] = e2 / (1.0 + e2)
    idx_ref[0:1, :] = i1
    idx_ref[1:2, :] = i2
    hit1 = jnp.where(e_iota == i1, 1.0, 0.0)
    hit2 = jnp.where(e_iota == i2, 1.0, 0.0)
    before = jnp.where(_iota2((tm, tm), 0) < _iota2((tm, tm), 1), 1.0, 0.0)
    prior = _dot(hit1 + hit2, before) + carry_ref[:, 0:1]
    rank_ref[0:1, :] = jnp.sum(hit1 * prior, axis=0, keepdims=True).astype(jnp.int32)
    rank_ref[1:2, :] = jnp.sum(hit2 * prior, axis=0, keepdims=True).astype(jnp.int32)
    carry_ref[...] = carry_ref[...] + jnp.sum(hit1 + hit2, axis=1, keepdims=True)
    cnt_ref[...] = carry_ref[...]


def _router(x1, nf, wr_t, tm=256):
    n_tok = x1.shape[0]
    col = pl.BlockSpec((2, tm), lambda i: (0, i))
    return pl.pallas_call(
        functools.partial(_router_kernel, tm),
        grid=(n_tok // tm,),
        in_specs=[pl.BlockSpec((tm, D_MODEL), lambda i: (i, 0)), _resident(nf.shape), _resident(wr_t.shape)],
        out_specs=[col, col, col, pl.BlockSpec((N_EXPERTS, LANES), lambda i: (0, 0)),
                   pl.BlockSpec((ROW_SPLIT, tm, SUBROW), lambda i: (0, i, 0))],
        out_shape=[jax.ShapeDtypeStruct((2, n_tok), jnp.int32), jax.ShapeDtypeStruct((2, n_tok), F32),
                   jax.ShapeDtypeStruct((2, n_tok), jnp.int32), jax.ShapeDtypeStruct((N_EXPERTS, LANES), F32),
                   jax.ShapeDtypeStruct((ROW_SPLIT, n_tok, SUBROW), F32)],
        scratch_shapes=[pltpu.VMEM((N_EXPERTS, LANES), F32)],
        compiler_params=_params("arbitrary"),
        name="moe_router",
    )(x1, nf, wr_t)


def _gather_rows(table, idx, window=128):
    split, n_table, width = table.shape
    n = idx.shape[0]
    flat_idx = (idx[None, :] + n_table * jnp.arange(split, dtype=jnp.int32)[:, None]).reshape(-1)
    return _gather_subrows(table.reshape(split * n_table, width), flat_idx, window).reshape(split, n, width)


def _gather_subrows(table, idx, window):
    n = idx.shape[0]
    d = table.shape[1]
    mesh = plsc.VectorSubcoreMesh(core_axis_name="core", subcore_axis_name="subcore")

    @functools.partial(pl.kernel, out_type=jax.ShapeDtypeStruct((n, d), table.dtype), mesh=mesh,
                       name="gather_rows")
    def gather(table_hbm, idx_hbm, out_hbm):
        def body(idx_vmem, out_vmem):
            pltpu.sync_copy(table_hbm.at[idx_vmem.at[0]], out_vmem)

        pltpu.emit_pipeline(
            body,
            grid=(n // window,),
            in_specs=[pl.BlockSpec((1, window), index_map=lambda i: (0, i))],
            out_specs=[pl.BlockSpec((window, d), index_map=lambda i: (i, 0))],
            core_axis_name=("core", "subcore"),
            dimension_semantics=(pltpu.PARALLEL,),
        )(idx_hbm, out_hbm)

    return gather(table, idx.reshape(1, n))


def _expert_kernel(tf, ge_ref, nv_ref, x_ref, w1_ref, w3_ref, w2_ref, y_ref):
    g = pl.program_id(0)

    @pl.when(g < nv_ref[0])
    def _():
        x = jnp.concatenate([x_ref[j] for j in range(ROW_SPLIT)], axis=-1).astype(BF16)
        acc = jnp.zeros((MOE_ROWS, D_MODEL), F32)
        for lo in range(0, EXPERT_DIM, tf):
            a = jnp.dot(x, w1_ref[:, lo:lo + tf], preferred_element_type=F32)
            b = jnp.dot(x, w3_ref[:, lo:lo + tf], preferred_element_type=F32)
            acc = acc + jnp.dot((_silu(a) * b).astype(BF16), w2_ref[lo:lo + tf, :], preferred_element_type=F32)
        for j in range(ROW_SPLIT):
            y_ref[j] = acc[:, j * SUBROW:(j + 1) * SUBROW]

    @pl.when(g >= nv_ref[0])
    def _():
        y_ref[...] = jnp.zeros_like(y_ref)


def _experts(xg, group_expert, n_valid, w1, w3, w2, tf=512):
    n_rows = xg.shape[1]
    rows = pl.BlockSpec((ROW_SPLIT, MOE_ROWS, SUBROW), lambda g, ge, nv: (0, g, 0))
    return pl.pallas_call(
        functools.partial(_expert_kernel, tf),
        grid_spec=pltpu.PrefetchScalarGridSpec(
            num_scalar_prefetch=2,
            grid=(n_rows // MOE_ROWS,),
            in_specs=[rows,
                      pl.BlockSpec((None, D_MODEL, EXPERT_DIM), lambda g, ge, nv: (ge[g], 0, 0)),
                      pl.BlockSpec((None, D_MODEL, EXPERT_DIM), lambda g, ge, nv: (ge[g], 0, 0)),
                      pl.BlockSpec((None, EXPERT_DIM, D_MODEL), lambda g, ge, nv: (ge[g], 0, 0))],
            out_specs=rows,
        ),
        out_shape=jax.ShapeDtypeStruct((ROW_SPLIT, n_rows, SUBROW), F32),
        compiler_params=pltpu.CompilerParams(dimension_semantics=("arbitrary",),
                                             vmem_limit_bytes=EXPERT_VMEM_LIMIT),
        name="experts",
    )(group_expert, n_valid, xg, w1, w3, w2)


def _combine_kernel(x1_ref, y0_ref, y1_ref, gate_ref, o_ref):
    gate = gate_ref[...]
    for j in range(ROW_SPLIT):
        cols = slice(j * SUBROW, (j + 1) * SUBROW)
        o_ref[:, cols] = x1_ref[:, cols] + gate[:, 0:1] * y0_ref[j] + gate[:, 1:2] * y1_ref[j]


def _combine(x1, yg, gate_t, tm=512):
    n_tok = x1.shape[0]
    nb = n_tok // tm
    return pl.pallas_call(
        _combine_kernel,
        grid=(nb,),
        in_specs=[pl.BlockSpec((tm, D_MODEL), lambda i: (i, 0)),
                  pl.BlockSpec((ROW_SPLIT, tm, SUBROW), lambda i: (0, i, 0)),
                  pl.BlockSpec((ROW_SPLIT, tm, SUBROW), lambda i: (0, i + nb, 0)),
                  pl.BlockSpec((tm, 2), lambda i: (i, 0))],
        out_specs=pl.BlockSpec((tm, D_MODEL), lambda i: (i, 0)),
        out_shape=jax.ShapeDtypeStruct((n_tok, D_MODEL), F32),
        compiler_params=_params("parallel"),
        name="moe_combine",
    )(x1, yg, yg, gate_t)


def _moe(x1, nf, w_router, w1, w3, w2):
    n_tok = x1.shape[0]
    idx, gate, rank, cnt, hf = _router(x1, nf, w_router.T)
    counts = cnt[:, 0].astype(jnp.int32)
    padded = (counts + MOE_ROWS - 1) // MOE_ROWS * MOE_ROWS
    end_padded = jnp.cumsum(padded)
    start_padded = end_padded - padded
    start = sum(jnp.where(idx == e, start_padded[e], 0) for e in range(N_EXPERTS))
    dest = (start + rank).reshape(-1)
    n_groups = (n_tok * 2 + MOE_ROWS - 1) // MOE_ROWS + N_EXPERTS
    n_rows = n_groups * MOE_ROWS
    group_row = jnp.arange(n_groups, dtype=jnp.int32)[:, None] * MOE_ROWS
    group_expert = jnp.minimum(jnp.sum(group_row >= end_padded[None, :], axis=1), N_EXPERTS - 1).astype(jnp.int32)
    n_valid = (end_padded[-1:] // MOE_ROWS).astype(jnp.int32)
    tok = jnp.tile(jnp.arange(n_tok, dtype=jnp.int32), 2)
    row_token = jnp.zeros((n_rows,), jnp.int32).at[dest].set(tok)
    xg = _gather_rows(hf, row_token)
    y = _experts(xg, group_expert, n_valid, w1, w3, w2)
    return _combine(x1, _gather_rows(y, dest), gate.T)


def _row(v, pad=0):
    v = v.reshape(1, -1).astype(F32)
    return jnp.pad(v, ((0, 0), (0, pad))) if pad else v


def _rows_at(w, start, total):
    return jnp.pad(w.astype(F32), ((start, total - start - w.shape[0]), (0, 0)))


def kernel(x, w_in, norm_mix, rwkv_mu, rwkv_decay_up, rwkv_w0, rwkv_a_up, rwkv_a0, rwkv_gate_up, rwkv_k_k, rwkv_k_a, rwkv_r_k, rwkv_ln_w, rwkv_ln_b, vres_down, vres_up, vres_v0, ssm_conv_w, ssm_conv_b, ssm_dt_bias, ssm_a_log, ssm_d, ssm_norm_w, att_q_gain, att_k_gain, att_rel_bias, gla_gate_up, gla_gate_bias, gla_norm_w, w_branch, w_out, norm_ffn, ffn_w1, ffn_w3, ffn_w2, moe_router, moe_w1, moe_w3, moe_w2):
    bsz, seq, _ = x.shape
    n_tok = bsz * seq
    depth = w_in.shape[0]
    x2 = x.reshape(n_tok, D_MODEL)
    v_first = None
    o_rw, o_ss, o_at = RWKV_COLS, RWKV_COLS + SSM_COLS, RWKV_COLS + SSM_COLS + ATT_COLS
    o_gl = o_at + GLA_COLS
    for l in range(depth):
        w = w_in[l]
        vd = vres_down[l - 1] if l > 0 else jnp.zeros((D_MODEL, 32), F32)
        wa = jnp.concatenate([w[:, :o_rw], vd, jnp.zeros((D_MODEL, RWKV_PAD - RWKV_COLS - 32), F32)], axis=1)
        wb = jnp.pad(w[:, o_rw:o_ss], ((0, 0), (0, SSM_PAD - SSM_COLS)))
        wc = w[:, o_ss:o_at]
        gl = w[:, o_at:o_gl]
        wd = jnp.concatenate([gl[:, 0:1024], gl[:, 1040:1552], gl[:, 1024:1040],
                              jnp.zeros((D_MODEL, GLA_PAD - GLA_COLS), F32)], axis=1)
        wg = w[:, o_gl:]
        h, za, zb, zc, zd = _inproj(x2, _row(norm_mix[l]), wa.astype(BF16), wb.astype(BF16),
                                    wc.astype(BF16), wd.astype(BF16))
        shp = lambda z: z.reshape(bsz, seq, z.shape[-1])

        rw = dict(mu=_row(rwkv_mu[l], RWKV_PAD - RWKV_COLS),
                  dup=_rows_at(rwkv_decay_up[l], 0, LANES), w0=_row(rwkv_w0[l]),
                  aup=_rows_at(rwkv_a_up[l], 64, LANES), a0=_row(rwkv_a0[l]),
                  gup=_rows_at(rwkv_gate_up[l], 0, 256).astype(BF16),
                  kk=_row(rwkv_k_k[l]), ka=_row(rwkv_k_a[l]), rk=_row(rwkv_r_k[l]),
                  lnw=_row(rwkv_ln_w[l]), lnb=_row(rwkv_ln_b[l]))
        if l > 0:
            rw.update(vup=_rows_at(vres_up[l - 1], 160, 256), v0=_row(vres_v0[l - 1]))
        o_a, v_first = _rwkv(shp(za), v_first, rw)
        o_b = _mamba(shp(zb), dict(cw=ssm_conv_w[l], cb=_row(ssm_conv_b[l]),
                                   dtb=_row(ssm_dt_bias[l], LANES - SSM_HEADS),
                                   alog=_row(ssm_a_log[l], LANES - SSM_HEADS),
                                   dexp=_row(jnp.repeat(ssm_d[l], CHUNK)), nw=_row(ssm_norm_w[l])))
        o_c = _attention(shp(zc), att_q_gain[l], att_k_gain[l], att_rel_bias)
        o_d = _gla(shp(zd), dict(gup=_rows_at(gla_gate_up[l], 0, LANES), gb=_row(gla_gate_bias[l]),
                                 nw=_row(gla_norm_w[l])))
        outs = [o.reshape(n_tok, BRANCH_DIM) for o in (o_a, o_b, o_c, o_d)]
        x1, hf = _merge(x2, h, outs, wg.astype(BF16), w_branch[l].astype(BF16), w_out[l].astype(BF16),
                        _row(norm_ffn[l]))
        if l % 2 == 0:
            x2 = _ffn(x1, hf, ffn_w1[l // 2].astype(BF16), ffn_w3[l // 2].astype(BF16),
                      ffn_w2[l // 2].astype(BF16))
        else:
            x2 = _moe(x1, _row(norm_ffn[l]), moe_router[l // 2], moe_w1[l // 2].astype(BF16),
                      moe_w3[l // 2].astype(BF16), moe_w2[l // 2].astype(BF16))
    return x2.reshape(bsz, seq, D_MODEL)
```

```python
import functools

import jax
import jax.numpy as jnp
from jax import lax
from jax.experimental import pallas as pl
from jax.experimental.pallas import tpu as pltpu
from jax.experimental.pallas import tpu_sc as plsc

F32 = jnp.float32
BF16 = jnp.bfloat16
HIGHEST = lax.Precision.HIGHEST

D_MODEL = 1024
CHUNK = 64
BRANCH_DIM = 512
NORM_EPS = 1e-6
LANES = 128
VMEM_LIMIT = 56 * 1024 * 1024
EXPERT_VMEM_LIMIT = 60 * 1024 * 1024

LOG2_E = 1.4426950408889634

RWKV_LN_EPS = 64e-5
RWKV_COLS = 1824
RWKV_PAD = 1920
SSM_COLS = 1544
SSM_PAD = 1664
SSM_HEADS = 8
SSM_STATE = 128
ATT_COLS = 1536
ATT_HEADS = 8
ATT_LEFT = 8 * CHUNK
REL_CLIP = 2 * CHUNK
GLA_COLS = 1552
GLA_PAD = 1664
GLA_GATE_NORM = 16.0
FFN_DIM = 2816
N_EXPERTS = 8
EXPERT_DIM = 3584
MOE_ROWS = 256
ROW_SPLIT = 4
SUBROW = D_MODEL // ROW_SPLIT


def _dot(a, b):
    return jnp.dot(a.astype(BF16), b.astype(BF16), preferred_element_type=F32)


def _dot_nt(a, b):
    return lax.dot_general(a.astype(BF16), b.astype(BF16), (((1,), (1,)), ((), ())),
                           preferred_element_type=F32)


def _dot_tn(a, b):
    return lax.dot_general(a.astype(BF16), b.astype(BF16), (((0,), (0,)), ((), ())),
                           preferred_element_type=F32)


def _hi_lo(a):
    hi = a.astype(BF16)
    return hi, (a - hi.astype(F32)).astype(BF16)


def _dot_3x(a, b):
    a_hi, a_lo = _hi_lo(a)
    b_hi, b_lo = _hi_lo(b)
    return (jnp.dot(a_hi, b_hi, preferred_element_type=F32) + jnp.dot(a_lo, b_hi, preferred_element_type=F32)
            + jnp.dot(a_hi, b_lo, preferred_element_type=F32))


def _dot_split(a, m):
    m = m.astype(BF16)
    hi, lo = _hi_lo(a)
    return jnp.dot(hi, m, preferred_element_type=F32) + jnp.dot(lo, m, preferred_element_type=F32)


def _dot_split_l(m, a):
    m = m.astype(BF16)
    hi, lo = _hi_lo(a)
    return jnp.dot(m, hi, preferred_element_type=F32) + jnp.dot(m, lo, preferred_element_type=F32)


def _softplus(x):
    return jnp.maximum(x, 0.0) + jnp.log(1.0 + jnp.exp(-jnp.abs(x)))


def _sigmoid(x):
    return 1.0 / (1.0 + jnp.exp(-x))


def _silu(x):
    return x * _sigmoid(x)


def _iota2(shape, axis):
    return lax.broadcasted_iota(jnp.int32, shape, axis)


def _group_mean_matrix(n, group):
    r = _iota2((n, n), 0) // group
    c = _iota2((n, n), 1) // group
    return jnp.where(r == c, 1.0 / group, 0.0).astype(F32)


def _chunk_tri(n):
    r = _iota2((n, n), 0)
    c = _iota2((n, n), 1)
    return jnp.where((r // CHUNK == c // CHUNK) & (r >= c), 1.0, 0.0).astype(F32)


def _resident(shape):
    nd = len(shape)
    return pl.BlockSpec(shape, lambda *_: (0,) * nd, pipeline_mode=pl.Buffered(1))


def _params(*sem):
    return pltpu.CompilerParams(dimension_semantics=sem, vmem_limit_bytes=VMEM_LIMIT)


def _inproj_kernel(x_ref, g_ref, wa_ref, wb_ref, wc_ref, wd_ref,
                   h_ref, za_ref, zb_ref, zc_ref, zd_ref):
    x = x_ref[...]
    ms = jnp.mean(x * x, axis=-1, keepdims=True)
    h = (x * lax.rsqrt(ms + NORM_EPS) * g_ref[...]).astype(BF16)
    h_ref[...] = h
    za_ref[...] = jnp.dot(h, wa_ref[...], preferred_element_type=F32)
    zb_ref[...] = jnp.dot(h, wb_ref[...], preferred_element_type=F32)
    zc_ref[...] = jnp.dot(h, wc_ref[...], preferred_element_type=F32)
    zd_ref[...] = jnp.dot(h, wd_ref[...], preferred_element_type=F32)


def _inproj(x2, g, wa, wb, wc, wd, tm=512):
    n_tok = x2.shape[0]
    row = lambda n: pl.BlockSpec((tm, n), lambda i: (i, 0))
    outs = (D_MODEL, RWKV_PAD, SSM_PAD, ATT_COLS, GLA_PAD)
    return pl.pallas_call(
        _inproj_kernel,
        grid=(n_tok // tm,),
        in_specs=[row(D_MODEL), _resident((1, D_MODEL)), _resident(wa.shape), _resident(wb.shape),
                  _resident(wc.shape), _resident(wd.shape)],
        out_specs=[row(n) for n in outs],
        out_shape=[jax.ShapeDtypeStruct((n_tok, outs[0]), BF16)]
        + [jax.ShapeDtypeStruct((n_tok, n), F32) for n in outs[1:]],
        compiler_params=_params("parallel"),
        name="inproj",
    )(x2, g, wa, wb, wc, wd)


def _rwkv_block(r, k, v, kk, ka, ld, tb, state_ref, y_ref):
    n_chunks = tb // CHUNK
    cum = _dot_split_l(_chunk_tri(tb), ld)
    p_inv = jnp.exp(-cum)
    a_t = -kk * jnp.exp(cum - ld)
    b_t = kk * ka * p_inv
    k_t = k * p_inv
    r_t = r * jnp.exp(cum)

    lane_head = _iota2((CHUNK, LANES), 1) // CHUNK
    row = _iota2((CHUNK, LANES), 0)
    col = _iota2((CHUNK, LANES), 1) % CHUNK
    strict = row > col
    incl = row >= col
    bd_mask = (_iota2((LANES, LANES), 0) // CHUNK) == (_iota2((LANES, LANES), 1) // CHUNK)
    eye = jnp.where(_iota2((LANES, LANES), 0) == _iota2((LANES, LANES), 1), 1.0, 0.0).astype(F32)

    def stack(z):
        return jnp.concatenate([jnp.where(lane_head == 0, z, 0.0), jnp.where(lane_head == 1, z, 0.0)], axis=0)

    def sub(z, u):
        c, p = u
        return z[c * CHUNK:(c + 1) * CHUNK, p * LANES:(p + 1) * LANES]

    units = [(c, p) for c in range(n_chunks) for p in range(4)]
    bf = lambda z: z.astype(BF16)
    gram = {u: _dot_nt(jnp.concatenate([sub(a_t, u), sub(r_t, u)], axis=0),
                       jnp.concatenate([stack(sub(b_t, u)), stack(sub(k_t, u))], axis=0))
            for u in units}
    a_rb = {u: bf(jnp.where(incl, gram[u][CHUNK:, :LANES], 0.0)) for u in units}
    a_akrk = {u: bf(jnp.concatenate([jnp.where(strict, gram[u][:CHUNK, LANES:], 0.0),
                                     jnp.where(incl, gram[u][CHUNK:, LANES:], 0.0)], axis=0)) for u in units}
    pw = {u: stack(jnp.where(strict, gram[u][:CHUNK, :LANES], 0.0)) for u in units}
    t_bd = {u: eye + pw[u] for u in units}
    pw = {u: _dot(pw[u], pw[u]) for u in units}
    for _ in range(CHUNK.bit_length() - 3):
        sp = {u: _dot(jnp.concatenate([t_bd[u], pw[u]], axis=0), pw[u]) for u in units}
        t_bd = {u: t_bd[u] + sp[u][:LANES] for u in units}
        pw = {u: sp[u][LANES:] for u in units}
    t_bd = {u: bf(t_bd[u] + _dot(t_bd[u], pw[u])) for u in units}
    v_bd = {u: bf(stack(sub(v, u))) for u in units}
    avyv = {u: _dot(a_akrk[u], v_bd[u]) for u in units}
    wuv = {u: _dot(t_bd[u], jnp.concatenate([stack(sub(a_t, u)), stack(avyv[u][:CHUNK])], axis=1))
           for u in units}
    wr = {u: bf(jnp.concatenate([wuv[u][:, :LANES], sub(r_t, u)], axis=0)) for u in units}

    h = [state_ref[p] for p in range(4)]
    for c in range(n_chunks):
        last = c * CHUNK + CHUNK - 1
        p_last = jnp.exp(cum[last:last + 1, :])
        hs = [_dot_nt(wr[(c, p)], h[p]) for p in range(4)]
        u2 = [hs[p][:LANES] + wuv[(c, p)][:, LANES:] for p in range(4)]
        u_p = [u2[p][:CHUNK] + u2[p][CHUNK:] for p in range(4)]
        upd = [_dot_tn(jnp.concatenate([u_p[p], sub(v, (c, p))], axis=0),
                       jnp.concatenate([sub(b_t, (c, p)), sub(k_t, (c, p))], axis=0)
                       * p_last[:, p * LANES:(p + 1) * LANES]) for p in range(4)]
        for p in range(4):
            y_ref[c * CHUNK:(c + 1) * CHUNK, p * LANES:(p + 1) * LANES] = (
                hs[p][LANES:] + _dot(a_rb[(c, p)], stack(u_p[p])) + avyv[(c, p)][CHUNK:])
        h = [h[p] * p_last[:, p * LANES:(p + 1) * LANES] + jnp.where(bd_mask, upd[p], 0.0) for p in range(4)]
    for p in range(4):
        state_ref[p] = h[p]


def _rwkv_kernel(has_vres, tb, *refs):
    if has_vres:
        (za_ref, vfirst_ref, mu_ref, dup_ref, w0_ref, aup_ref, a0_ref, gup_ref, vup_ref, v0_ref,
         kk_ref, ka_ref, rk_ref, lnw_ref, lnb_ref, o_ref, prev_ref, state_ref, y_ref) = refs
    else:
        (za_ref, mu_ref, dup_ref, w0_ref, aup_ref, a0_ref, gup_ref,
         kk_ref, ka_ref, rk_ref, lnw_ref, lnb_ref, o_ref, vraw_ref, prev_ref, state_ref, y_ref) = refs

    @pl.when(pl.program_id(1) == 0)
    def _():
        prev_ref[...] = jnp.zeros_like(prev_ref)
        state_ref[...] = jnp.zeros_like(state_ref)

    za = za_ref[...]
    shifted = pltpu.roll(za, 1, axis=0)
    shifted = jnp.where(_iota2(za.shape, 0) == 0, prev_ref[7:8, :], shifted)
    prev_ref[...] = za[tb - 8:, :]
    za = za + (shifted - za) * mu_ref[...]

    r = za[:, 0:512]
    k = za[:, 512:1024]
    v = za[:, 1024:1536]
    xwa = za[:, 1536:1664]
    xg = za[:, 1664:1920]
    w_log = -_softplus(-(w0_ref[...] + _dot_3x(jnp.tanh(xwa), dup_ref[...]))) - 0.5
    ld = -jnp.exp(w_log)
    a = _sigmoid(a0_ref[...] + _dot(xwa, aup_ref[...]))
    g = _dot(_sigmoid(xg), gup_ref[...])
    if has_vres:
        v_mix = _sigmoid(v0_ref[...] + _dot(xg, vup_ref[...]))
        v = v + (vfirst_ref[...] - v) * v_mix
    else:
        vraw_ref[...] = v
    kk = k * kk_ref[...]
    k = k * (1.0 + (a - 1.0) * ka_ref[...])
    head_sum = _group_mean_matrix(512, CHUNK) * float(CHUNK)
    kk = kk / jnp.maximum(jnp.sqrt(_dot(kk * kk, head_sum)), 1e-12)

    _rwkv_block(r, k, v, kk, a, ld, tb, state_ref, y_ref)
    y = y_ref[...]
    head_mean = _group_mean_matrix(512, CHUNK)
    mean = _dot_split(y, head_mean)
    yc = y - mean
    var = _dot(yc * yc, head_mean)
    y = yc * lax.rsqrt(var + RWKV_LN_EPS) * lnw_ref[...] + lnb_ref[...]
    y = y + _dot(r * k * rk_ref[...], head_sum) * v
    o_ref[...] = (y * g).astype(o_ref.dtype)


def _rwkv(za, v_first, prm, tb=256):
    bsz, seq, _ = za.shape
    has_vres = v_first is not None
    blk = lambda n: pl.BlockSpec((None, tb, n), lambda b, i: (b, i, 0))
    names = (["mu", "dup", "w0", "aup", "a0", "gup"] + (["vup", "v0"] if has_vres else [])
             + ["kk", "ka", "rk", "lnw", "lnb"])
    weights = [prm[n] for n in names]
    ins = [za] + ([v_first] if has_vres else []) + weights
    in_specs = [blk(RWKV_PAD)] + ([blk(512)] if has_vres else []) + [_resident(w.shape) for w in weights]
    out_shape = [jax.ShapeDtypeStruct((bsz, seq, 512), BF16)]
    out_specs = [blk(512)]
    if not has_vres:
        out_shape.append(jax.ShapeDtypeStruct((bsz, seq, 512), F32))
        out_specs.append(blk(512))
    res = pl.pallas_call(
        functools.partial(_rwkv_kernel, has_vres, tb),
        grid=(bsz, seq // tb),
        in_specs=in_specs,
        out_specs=out_specs,
        out_shape=out_shape,
        scratch_shapes=[pltpu.VMEM((8, RWKV_PAD), F32), pltpu.VMEM((4, LANES, LANES), F32),
                        pltpu.VMEM((tb, 512), F32)],
        compiler_params=_params("parallel", "arbitrary"),
        name="rwkv7",
    )(*ins)
    return (res[0], v_first) if has_vres else (res[0], res[1])


def _mamba_kernel(tb, zb_ref, cw_ref, cb_ref, dtb_ref, alog_ref, dexp_ref, nw_ref,
                  o_ref, xbuf_ref, state_ref, y_ref):
    @pl.when(pl.program_id(1) == 0)
    def _():
        xbuf_ref[0:8, :] = jnp.zeros((8, 1024), F32)
        state_ref[...] = jnp.zeros_like(state_ref)

    xbuf_ref[8:8 + tb, :] = zb_ref[:, 512:1536]
    conv = cb_ref[...]
    for i in range(4):
        conv = conv + cw_ref[i:i + 1, :] * xbuf_ref[5 + i:5 + i + tb, :]
    xbuf_ref[0:8, :] = xbuf_ref[tb:tb + 8, :]
    xbc = _silu(conv)
    xs = xbc[:, 0:512]
    bm = xbc[:, 512:768]
    cm = xbc[:, 768:1024]

    lane = _iota2((1, LANES), 1)
    dt = _softplus(zb_ref[:, 1536:1664] + dtb_ref[...])
    a_neg = jnp.where(lane < SSM_HEADS, -jnp.exp(alog_ref[...]), 0.0)
    acs = _dot_split_l(_chunk_tri(tb), dt * a_neg)
    expand = jnp.where(_iota2((LANES, 512), 0) == _iota2((LANES, 512), 1) // CHUNK, 1.0, 0.0)
    dt_e = _dot(dt, expand)
    acs_e = _dot_split(acs, expand)
    xdt = xs * dt_e

    n_chunks = tb // CHUNK
    causal = _iota2((CHUNK, 512), 0) >= _iota2((CHUNK, 512), 1) % CHUNK
    pair_head = _iota2((CHUNK, LANES), 1) // CHUNK
    spread = jnp.where(_iota2((CHUNK, 512), 0) == _iota2((CHUNK, 512), 1) % CHUNK, 1.0, 0.0)
    own_head = _iota2((SSM_HEADS, 512), 0) == _iota2((SSM_HEADS, 512), 1) // CHUNK
    rows_of = lambda c: slice(c * CHUNK, (c + 1) * CHUNK)
    grp = lambda g: slice(g * SSM_STATE, (g + 1) * SSM_STATE)

    def stack(z):
        return jnp.concatenate([jnp.where(pair_head == 0, z, 0.0), jnp.where(pair_head == 1, z, 0.0)], axis=0)

    y_intra, upd, e_in, e_last = [], [], [], []
    for c in range(n_chunks):
        rows = rows_of(c)
        acs_ec = acs_e[rows]
        acs_t = acs[rows].T[0:SSM_HEADS]
        acs_row = jnp.sum(jnp.where(own_head, _dot_split(acs_t, spread), 0.0), axis=0, keepdims=True)
        decay = jnp.exp(jnp.where(causal, acs_ec - acs_row, -jnp.inf))
        cb = jnp.concatenate(
            [_dot_nt(cm[rows, grp(g)], jnp.concatenate([bm[rows, grp(g)]] * 4, axis=0)) for g in range(2)], axis=1)
        m = cb * decay
        xdt_c = xdt[rows]
        y_intra.append([_dot(m[:, p * LANES:(p + 1) * LANES], stack(xdt_c[:, p * LANES:(p + 1) * LANES]))
                        for p in range(4)])
        last_e = acs_ec[CHUNK - 1:CHUNK, :]
        x_out = xdt_c * jnp.exp(last_e - acs_ec)
        upd.append([_dot_tn(bm[rows, grp(g)], x_out[:, g * 256:(g + 1) * 256]) for g in range(2)])
        e_in.append(jnp.exp(acs_ec))
        e_last.append(jnp.exp(last_e))

    state = [state_ref[g] for g in range(2)]
    for c in range(n_chunks):
        rows = rows_of(c)
        for g in range(2):
            ls = slice(g * 256, (g + 1) * 256)
            y_in = jnp.concatenate(y_intra[c][2 * g:2 * g + 2], axis=1)
            y_ref[rows, ls] = y_in + _dot(cm[rows, grp(g)], state[g]) * e_in[c][:, ls]
            state[g] = state[g] * e_last[c][:, ls] + upd[c][g]
    for g in range(2):
        state_ref[g] = state[g]

    y = (y_ref[...] + xs * dexp_ref[...]) * _silu(zb_ref[:, 0:512])
    for g in range(2):
        ls = slice(g * 256, (g + 1) * 256)
        yg = y[:, ls]
        ms = jnp.mean(yg * yg, axis=-1, keepdims=True)
        o_ref[:, ls] = (yg * lax.rsqrt(ms + NORM_EPS) * nw_ref[:, ls]).astype(o_ref.dtype)


def _mamba(zb, prm, tb=256):
    bsz, seq, _ = zb.shape
    blk = lambda n: pl.BlockSpec((None, tb, n), lambda b, i: (b, i, 0))
    weights = [prm[n] for n in ("cw", "cb", "dtb", "alog", "dexp", "nw")]
    return pl.pallas_call(
        functools.partial(_mamba_kernel, tb),
        grid=(bsz, seq // tb),
        in_specs=[blk(SSM_PAD)] + [_resident(w.shape) for w in weights],
        out_specs=blk(512),
        out_shape=jax.ShapeDtypeStruct((bsz, seq, 512), BF16),
        scratch_shapes=[pltpu.VMEM((tb + 8, 1024), F32), pltpu.VMEM((2, SSM_STATE, 256), F32),
                        pltpu.VMEM((tb, 512), F32)],
        compiler_params=_params("parallel", "arbitrary"),
        name="mamba2",
    )(zb, *weights)


def _attn_kernel(tq, seq, q_ref, k_ref, v_ref, qg_ref, kg_ref, bias_ref, o_ref, kn_ref, vb_ref):
    i = pl.program_id(1)
    head_mean = _group_mean_matrix(512, CHUNK)
    win = tq + ATT_LEFT

    @pl.when(i == 0)
    def _():
        kn_ref[0:ATT_LEFT, :] = jnp.zeros((ATT_LEFT, 512), BF16)
        vb_ref[0:ATT_LEFT, :] = jnp.zeros((ATT_LEFT, 512), BF16)
        for j in range(seq // tq):
            kj = k_ref[j * tq:(j + 1) * tq, :]
            ms = _dot(kj * kj, head_mean)
            kn_ref[ATT_LEFT + j * tq:ATT_LEFT + (j + 1) * tq, :] = (
                kj * lax.rsqrt(ms + NORM_EPS) * kg_ref[...]).astype(BF16)
            vb_ref[ATT_LEFT + j * tq:ATT_LEFT + (j + 1) * tq, :] = v_ref[j * tq:(j + 1) * tq, :].astype(BF16)

    q = q_ref[...]
    ms = _dot(q * q, head_mean)
    qn = q * lax.rsqrt(ms + NORM_EPS) * qg_ref[...] * (CHUNK ** -0.5 * LOG2_E)
    start = pl.multiple_of(i * tq, tq)
    kwin = kn_ref[pl.ds(start, win), :]
    vwin = vb_ref[pl.ds(start, win), :]
    lane_head = _iota2((tq, LANES), 1) // CHUNK
    for p in range(4):
        ls = slice(p * LANES, (p + 1) * LANES)
        acc = jnp.zeros((tq, LANES), F32)
        for s in range(2):
            qh = jnp.where(lane_head == s, qn[:, ls], 0.0)
            sc = _dot_nt(qh, kwin[:, ls]) + bias_ref[2 * p + s]
            e = jnp.exp2(sc - jnp.max(sc, axis=-1, keepdims=True))
            oh = _dot(e, vwin[:, ls]) / jnp.sum(e, axis=-1, keepdims=True)
            acc = jnp.where(lane_head == s, oh, acc)
        o_ref[:, ls] = acc.astype(o_ref.dtype)


def _band_bias(rel_bias, tq, win):
    period = tq + win
    m = jnp.arange(period)
    d = jnp.where(m < win, m, m - period) - ATT_LEFT
    f = rel_bias.astype(F32)[:, jnp.clip(d, -REL_CLIP, REL_CLIP) + REL_CLIP]
    g = jnp.tile(f, (1, tq))[:, :tq * (period - 1)].reshape(-1, tq, period - 1)
    bias = g[:, :, :win] * LOG2_E
    left = ATT_LEFT // CHUNK
    qc = left + jnp.arange(tq)[:, None] // CHUNK
    kc = jnp.arange(win)[None, :] // CHUNK
    tiles = []
    for blk in range(ATT_LEFT // tq + 1):
        first_kc = jnp.maximum(qc - left, left - blk * (tq // CHUNK))
        tiles.append(jnp.where((kc <= qc) & (kc >= first_kc), bias, -jnp.inf))
    return jnp.stack(tiles)


def _attention(zc, q_gain, k_gain, rel_bias, tq=256):
    bsz, seq, _ = zc.shape
    win = tq + ATT_LEFT
    bias = _band_bias(rel_bias, tq, win)
    qg = jnp.tile(q_gain, ATT_HEADS)[None, :]
    kg = jnp.tile(k_gain, ATT_HEADS)[None, :]
    full = lambda col: pl.BlockSpec((None, seq, 512), lambda b, i: (b, 0, col))
    last_tile = bias.shape[0] - 1
    bias_spec = pl.BlockSpec((None,) + bias.shape[1:], lambda b, i: (jnp.minimum(i, last_tile), 0, 0, 0))
    return pl.pallas_call(
        functools.partial(_attn_kernel, tq, seq),
        grid=(bsz, seq // tq),
        in_specs=[pl.BlockSpec((None, tq, 512), lambda b, i: (b, i, 0)), full(1), full(2),
                  _resident(qg.shape), _resident(kg.shape), bias_spec],
        out_specs=pl.BlockSpec((None, tq, 512), lambda b, i: (b, i, 0)),
        out_shape=jax.ShapeDtypeStruct((bsz, seq, 512), BF16),
        scratch_shapes=[pltpu.VMEM((ATT_LEFT + seq, 512), BF16), pltpu.VMEM((ATT_LEFT + seq, 512), BF16)],
        compiler_params=_params("parallel", "arbitrary"),
        name="band_attention",
    )(zc, zc, zc, qg, kg, bias)


def _gla_kernel(tb, zd_ref, gup_ref, gb_ref, nw_ref, o_ref, state_ref, y_ref):
    @pl.when(pl.program_id(1) == 0)
    def _():
        state_ref[...] = jnp.zeros_like(state_ref)

    q = zd_ref[:, 0:256] * (CHUNK ** -0.5)
    k = zd_ref[:, 256:512]
    v = zd_ref[:, 512:1024]
    log_a = -_softplus(-(_dot_3x(zd_ref[:, 1536:1664], gup_ref[...]) + gb_ref[...])) / GLA_GATE_NORM
    bcum = _dot_split_l(_chunk_tri(tb), log_a)

    k_head = _iota2((CHUNK, 256), 1) // CHUNK
    v_head = _iota2((CHUNK, 512), 1) // LANES
    causal = _iota2((CHUNK, 256), 0) >= _iota2((CHUNK, 256), 1) % CHUNK
    bd = (_iota2((512, 256), 0) // LANES) == (_iota2((512, 256), 1) // CHUNK)
    n_chunks = tb // CHUNK
    qg_all = (q * jnp.exp(bcum)).astype(BF16)
    kg_all = k * jnp.exp(-bcum)
    qg, att, o_intra, upd, e_last = [], [], [], [], []
    for c in range(n_chunks):
        rows = slice(c * CHUNK, (c + 1) * CHUNK)
        kg_bd = jnp.concatenate([jnp.where(k_head == h, kg_all[rows], 0.0) for h in range(4)], axis=0)
        qg.append(qg_all[rows])
        att.append(jnp.where(causal, _dot_nt(qg[c], kg_bd), 0.0))
    for c in range(n_chunks):
        rows = slice(c * CHUNK, (c + 1) * CHUNK)
        bc = bcum[rows]
        blast = bc[CHUNK - 1:CHUNK, :]
        v_c = v[rows]
        v_bd = jnp.concatenate([jnp.where(v_head == h, v_c, 0.0) for h in range(4)], axis=0)
        o_intra.append(_dot(att[c], v_bd))
        upd.append(jnp.where(bd, _dot_tn(v_c, k[rows] * jnp.exp(blast - bc)), 0.0))
        e_last.append(jnp.exp(blast))
    st = state_ref[...]
    for c in range(n_chunks):
        y_ref[c * CHUNK:(c + 1) * CHUNK, :] = o_intra[c] + _dot_nt(qg[c], st)
        st = st * e_last[c] + upd[c]
    state_ref[...] = st

    o = y_ref[...]
    for h in range(4):
        ls = slice(h * LANES, (h + 1) * LANES)
        oh = o[:, ls]
        ms = jnp.mean(oh * oh, axis=-1, keepdims=True)
        o_ref[:, ls] = (oh * lax.rsqrt(ms + NORM_EPS) * nw_ref[...]
                        * _silu(zd_ref[:, 1024 + h * LANES:1024 + (h + 1) * LANES])).astype(o_ref.dtype)


def _gla(zd, prm, tb=256):
    bsz, seq, _ = zd.shape
    blk = lambda n: pl.BlockSpec((None, tb, n), lambda b, i: (b, i, 0))
    weights = [prm[n] for n in ("gup", "gb", "nw")]
    return pl.pallas_call(
        functools.partial(_gla_kernel, tb),
        grid=(bsz, seq // tb),
        in_specs=[blk(GLA_PAD)] + [_resident(w.shape) for w in weights],
        out_specs=blk(512),
        out_shape=jax.ShapeDtypeStruct((bsz, seq, 512), BF16),
        scratch_shapes=[pltpu.VMEM((512, 256), F32), pltpu.VMEM((tb, 512), F32)],
        compiler_params=_params("parallel", "arbitrary"),
        name="gla",
    )(zd, *weights)


def _merge_kernel(x_ref, h_ref, oa_ref, ob_ref, oc_ref, od_ref, wg_ref, wb_ref, wo_ref, nf_ref,
                  x1_ref, hf_ref):
    h = h_ref[...]
    acc = jnp.zeros(x_ref.shape, F32)
    for i, o_ref in enumerate((oa_ref, ob_ref, oc_ref, od_ref)):
        gate = _sigmoid(jnp.dot(h, wg_ref[:, i * D_MODEL:(i + 1) * D_MODEL], preferred_element_type=F32))
        acc = acc + gate * jnp.dot(o_ref[...], wb_ref[i], preferred_element_type=F32)
    x1 = x_ref[...] + jnp.dot(acc.astype(BF16), wo_ref[...], preferred_element_type=F32)
    x1_ref[...] = x1
    ms = jnp.mean(x1 * x1, axis=-1, keepdims=True)
    hf_ref[...] = (x1 * lax.rsqrt(ms + NORM_EPS) * nf_ref[...]).astype(BF16)


def _merge(x2, h, outs, wg, wb, wo, nf, tm=512):
    n_tok = x2.shape[0]
    row = lambda n: pl.BlockSpec((tm, n), lambda i: (i, 0))
    return pl.pallas_call(
        _merge_kernel,
        grid=(n_tok // tm,),
        in_specs=[row(D_MODEL), row(D_MODEL)] + [row(BRANCH_DIM)] * 4
        + [_resident(wg.shape), _resident(wb.shape), _resident(wo.shape), _resident(nf.shape)],
        out_specs=[row(D_MODEL), row(D_MODEL)],
        out_shape=[jax.ShapeDtypeStruct((n_tok, D_MODEL), F32), jax.ShapeDtypeStruct((n_tok, D_MODEL), BF16)],
        compiler_params=_params("parallel"),
        name="merge",
    )(x2, h, *outs, wg, wb, wo, nf)


def _ffn_kernel(tf, x1_ref, hf_ref, w1_ref, w3_ref, w2_ref, o_ref):
    hf = hf_ref[...]
    acc = x1_ref[...]
    for lo in range(0, FFN_DIM, tf):
        cols = slice(lo, min(lo + tf, FFN_DIM))
        a = jnp.dot(hf, w1_ref[:, cols], preferred_element_type=F32)
        b = jnp.dot(hf, w3_ref[:, cols], preferred_element_type=F32)
        acc = acc + jnp.dot((_silu(a) * b).astype(BF16), w2_ref[cols, :], preferred_element_type=F32)
    o_ref[...] = acc


def _ffn(x1, hf, w1, w3, w2, tm=512, tf=512):
    n_tok = x1.shape[0]
    row = pl.BlockSpec((tm, D_MODEL), lambda i: (i, 0))
    return pl.pallas_call(
        functools.partial(_ffn_kernel, tf),
        grid=(n_tok // tm,),
        in_specs=[row, row, _resident(w1.shape), _resident(w3.shape), _resident(w2.shape)],
        out_specs=row,
        out_shape=jax.ShapeDtypeStruct((n_tok, D_MODEL), F32),
        compiler_params=_params("parallel"),
        name="ffn",
    )(x1, hf, w1, w3, w2)


def _router_kernel(tm, x1_ref, nf_ref, wr_ref, idx_ref, gate_ref, rank_ref, cnt_ref, hf_ref, carry_ref):
    @pl.when(pl.program_id(0) == 0)
    def _():
        carry_ref[...] = jnp.zeros_like(carry_ref)

    x1 = x1_ref[...]
    ms = jnp.mean(x1 * x1, axis=-1, keepdims=True)
    hf = x1 * lax.rsqrt(ms + NORM_EPS) * nf_ref[...]
    for j in range(ROW_SPLIT):
        hf_ref[j] = hf[:, j * SUBROW:(j + 1) * SUBROW]
    logits = lax.dot_general(wr_ref[...], hf, (((1,), (1,)), ((), ())), precision=HIGHEST,
                             preferred_element_type=F32)
    e_iota = _iota2((N_EXPERTS, tm), 0)
    m1 = jnp.max(logits, axis=0, keepdims=True)
    i1 = jnp.min(jnp.where(logits == m1, e_iota, N_EXPERTS), axis=0, keepdims=True)
    rest = jnp.where(e_iota == i1, -jnp.inf, logits)
    m2 = jnp.max(rest, axis=0, keepdims=True)
    i2 = jnp.min(jnp.where(rest == m2, e_iota, N_EXPERTS), axis=0, keepdims=True)
    e2 = jnp.exp(m2 - m1)
    gate_ref[0:1, :] = 1.0 / (1.0 + e2)
    gate_ref[1:2, :] = e2 / (1.0 + e2)
    idx_ref[0:1, :] = i1
    idx_ref[1:2, :] = i2
    hit1 = jnp.where(e_iota == i1, 1.0, 0.0)
    hit2 = jnp.where(e_iota == i2, 1.0, 0.0)
    before = jnp.where(_iota2((tm, tm), 0) < _iota2((tm, tm), 1), 1.0, 0.0)
    prior = _dot(hit1 + hit2, before) + carry_ref[:, 0:1]
    rank_ref[0:1, :] = jnp.sum(hit1 * prior, axis=0, keepdims=True).astype(jnp.int32)
    rank_ref[1:2, :] = jnp.sum(hit2 * prior, axis=0, keepdims=True).astype(jnp.int32)
    carry_ref[...] = carry_ref[...] + jnp.sum(hit1 + hit2, axis=1, keepdims=True)
    cnt_ref[...] = carry_ref[...]


def _router(x1, nf, wr_t, tm=512):
    n_tok = x1.shape[0]
    col = pl.BlockSpec((2, tm), lambda i: (0, i))
    return pl.pallas_call(
        functools.partial(_router_kernel, tm),
        grid=(n_tok // tm,),
        in_specs=[pl.BlockSpec((tm, D_MODEL), lambda i: (i, 0)), _resident(nf.shape), _resident(wr_t.shape)],
        out_specs=[col, col, col, pl.BlockSpec((N_EXPERTS, LANES), lambda i: (0, 0)),
                   pl.BlockSpec((ROW_SPLIT, tm, SUBROW), lambda i: (0, i, 0))],
        out_shape=[jax.ShapeDtypeStruct((2, n_tok), jnp.int32), jax.ShapeDtypeStruct((2, n_tok), F32),
                   jax.ShapeDtypeStruct((2, n_tok), jnp.int32), jax.ShapeDtypeStruct((N_EXPERTS, LANES), F32),
                   jax.ShapeDtypeStruct((ROW_SPLIT, n_tok, SUBROW), F32)],
        scratch_shapes=[pltpu.VMEM((N_EXPERTS, LANES), F32)],
        compiler_params=_params("arbitrary"),
        name="moe_router",
    )(x1, nf, wr_t)


def _gather_rows(table, idx, window=128):
    split, n_table, width = table.shape
    n = idx.shape[0]
    flat_idx = (idx[None, :] + n_table * jnp.arange(split, dtype=jnp.int32)[:, None]).reshape(-1)
    return _gather_subrows(table.reshape(split * n_table, width), flat_idx, window).reshape(split, n, width)


def _gather_subrows(table, idx, window):
    n = idx.shape[0]
    d = table.shape[1]
    mesh = plsc.VectorSubcoreMesh(core_axis_name="core", subcore_axis_name="subcore")

    @functools.partial(pl.kernel, out_type=jax.ShapeDtypeStruct((n, d), table.dtype), mesh=mesh,
                       name="gather_rows")
    def gather(table_hbm, idx_hbm, out_hbm):
        def body(idx_vmem, out_vmem):
            pltpu.sync_copy(table_hbm.at[idx_vmem.at[0]], out_vmem)

        pltpu.emit_pipeline(
            body,
            grid=(n // window,),
            in_specs=[pl.BlockSpec((1, window), index_map=lambda i: (0, i))],
            out_specs=[pl.BlockSpec((window, d), index_map=lambda i: (i, 0))],
            core_axis_name=("core", "subcore"),
            dimension_semantics=(pltpu.PARALLEL,),
        )(idx_hbm, out_hbm)

    return gather(table, idx.reshape(1, n))


def _expert_kernel(tf, ge_ref, nv_ref, x_ref, w1_ref, w3_ref, w2_ref, y_ref):
    g = pl.program_id(0)

    @pl.when(g < nv_ref[0])
    def _():
        x = jnp.concatenate([x_ref[j] for j in range(ROW_SPLIT)], axis=-1).astype(BF16)
        acc = jnp.zeros((MOE_ROWS, D_MODEL), F32)
        for lo in range(0, EXPERT_DIM, tf):
            a = jnp.dot(x, w1_ref[:, lo:lo + tf], preferred_element_type=F32)
            b = jnp.dot(x, w3_ref[:, lo:lo + tf], preferred_element_type=F32)
            acc = acc + jnp.dot((_silu(a) * b).astype(BF16), w2_ref[lo:lo + tf, :], preferred_element_type=F32)
        for j in range(ROW_SPLIT):
            y_ref[j] = acc[:, j * SUBROW:(j + 1) * SUBROW]

    @pl.when(g >= nv_ref[0])
    def _():
        y_ref[...] = jnp.zeros_like(y_ref)


def _experts(xg, group_expert, n_valid, w1, w3, w2, tf=512):
    n_rows = xg.shape[1]
    rows = pl.BlockSpec((ROW_SPLIT, MOE_ROWS, SUBROW), lambda g, ge, nv: (0, g, 0))
    return pl.pallas_call(
        functools.partial(_expert_kernel, tf),
        grid_spec=pltpu.PrefetchScalarGridSpec(
            num_scalar_prefetch=2,
            grid=(n_rows // MOE_ROWS,),
            in_specs=[rows,
                      pl.BlockSpec((None, D_MODEL, EXPERT_DIM), lambda g, ge, nv: (ge[g], 0, 0)),
                      pl.BlockSpec((None, D_MODEL, EXPERT_DIM), lambda g, ge, nv: (ge[g], 0, 0)),
                      pl.BlockSpec((None, EXPERT_DIM, D_MODEL), lambda g, ge, nv: (ge[g], 0, 0))],
            out_specs=rows,
        ),
        out_shape=jax.ShapeDtypeStruct((ROW_SPLIT, n_rows, SUBROW), F32),
        compiler_params=pltpu.CompilerParams(dimension_semantics=("arbitrary",),
                                             vmem_limit_bytes=EXPERT_VMEM_LIMIT),
        name="experts",
    )(group_expert, n_valid, xg, w1, w3, w2)


def _cast_kernel(w_ref, o_ref):
    o_ref[...] = w_ref[...].astype(o_ref.dtype)


def _to_bf16(w):
    n, rows, cols = w.shape
    blk = pl.BlockSpec((None, rows // 2, cols), lambda i, j: (i, j, 0))
    return pl.pallas_call(
        _cast_kernel,
        grid=(n, 2),
        in_specs=[blk],
        out_specs=blk,
        out_shape=jax.ShapeDtypeStruct(w.shape, BF16),
        compiler_params=_params("parallel", "parallel"),
        name="cast_bf16",
    )(w)


def _combine_kernel(x1_ref, y0_ref, y1_ref, gate_ref, o_ref):
    gate = gate_ref[...]
    for j in range(ROW_SPLIT):
        cols = slice(j * SUBROW, (j + 1) * SUBROW)
        o_ref[:, cols] = x1_ref[:, cols] + gate[:, 0:1] * y0_ref[j] + gate[:, 1:2] * y1_ref[j]


def _combine(x1, yg, gate_t, tm=512):
    n_tok = x1.shape[0]
    nb = n_tok // tm
    return pl.pallas_call(
        _combine_kernel,
        grid=(nb,),
        in_specs=[pl.BlockSpec((tm, D_MODEL), lambda i: (i, 0)),
                  pl.BlockSpec((ROW_SPLIT, tm, SUBROW), lambda i: (0, i, 0)),
                  pl.BlockSpec((ROW_SPLIT, tm, SUBROW), lambda i: (0, i + nb, 0)),
                  pl.BlockSpec((tm, 2), lambda i: (i, 0))],
        out_specs=pl.BlockSpec((tm, D_MODEL), lambda i: (i, 0)),
        out_shape=jax.ShapeDtypeStruct((n_tok, D_MODEL), F32),
        compiler_params=_params("parallel"),
        name="moe_combine",
    )(x1, yg, yg, gate_t)


def _moe(x1, nf, w_router, w1, w3, w2):
    n_tok = x1.shape[0]
    idx, gate, rank, cnt, hf = _router(x1, nf, w_router.T)
    counts = cnt[:, 0].astype(jnp.int32)
    padded = (counts + MOE_ROWS - 1) // MOE_ROWS * MOE_ROWS
    end_padded = jnp.cumsum(padded)
    start_padded = end_padded - padded
    start = sum(jnp.where(idx == e, start_padded[e], 0) for e in range(N_EXPERTS))
    dest = (start + rank).reshape(-1)
    n_groups = (n_tok * 2 + MOE_ROWS - 1) // MOE_ROWS + N_EXPERTS
    n_rows = n_groups * MOE_ROWS
    group_row = jnp.arange(n_groups, dtype=jnp.int32)[:, None] * MOE_ROWS
    group_expert = jnp.minimum(jnp.sum(group_row >= end_padded[None, :], axis=1), N_EXPERTS - 1).astype(jnp.int32)
    n_valid = (end_padded[-1:] // MOE_ROWS).astype(jnp.int32)
    tok = jnp.tile(jnp.arange(n_tok, dtype=jnp.int32), 2)
    row_token = jnp.zeros((n_rows,), jnp.int32).at[dest].set(tok)
    xg = _gather_rows(hf, row_token)
    y = _experts(xg, group_expert, n_valid, w1, w3, w2)
    return _combine(x1, _gather_rows(y, dest), gate.T)


def _row(v, pad=0):
    v = v.reshape(1, -1).astype(F32)
    return jnp.pad(v, ((0, 0), (0, pad))) if pad else v


def _rows_at(w, start, total):
    return jnp.pad(w.astype(F32), ((start, total - start - w.shape[0]), (0, 0)))


def kernel(x, w_in, norm_mix, rwkv_mu, rwkv_decay_up, rwkv_w0, rwkv_a_up, rwkv_a0, rwkv_gate_up, rwkv_k_k, rwkv_k_a, rwkv_r_k, rwkv_ln_w, rwkv_ln_b, vres_down, vres_up, vres_v0, ssm_conv_w, ssm_conv_b, ssm_dt_bias, ssm_a_log, ssm_d, ssm_norm_w, att_q_gain, att_k_gain, att_rel_bias, gla_gate_up, gla_gate_bias, gla_norm_w, w_branch, w_out, norm_ffn, ffn_w1, ffn_w3, ffn_w2, moe_router, moe_w1, moe_w3, moe_w2):
    bsz, seq, _ = x.shape
    n_tok = bsz * seq
    depth = w_in.shape[0]
    x2 = x.reshape(n_tok, D_MODEL)
    v_first = None
    o_rw, o_ss, o_at = RWKV_COLS, RWKV_COLS + SSM_COLS, RWKV_COLS + SSM_COLS + ATT_COLS
    o_gl = o_at + GLA_COLS
    for l in range(depth):
        w = w_in[l]
        vd = vres_down[l - 1] if l > 0 else jnp.zeros((D_MODEL, 32), F32)
        wa = jnp.concatenate([w[:, :o_rw], vd, jnp.zeros((D_MODEL, RWKV_PAD - RWKV_COLS - 32), F32)], axis=1)
        wb = jnp.pad(w[:, o_rw:o_ss], ((0, 0), (0, SSM_PAD - SSM_COLS)))
        wc = w[:, o_ss:o_at]
        gl = w[:, o_at:o_gl]
        wd = jnp.concatenate([gl[:, 0:1024], gl[:, 1040:1552], gl[:, 1024:1040],
                              jnp.zeros((D_MODEL, GLA_PAD - GLA_COLS), F32)], axis=1)
        wg = w[:, o_gl:]
        h, za, zb, zc, zd = _inproj(x2, _row(norm_mix[l]), wa.astype(BF16), wb.astype(BF16),
                                    wc.astype(BF16), wd.astype(BF16))
        shp = lambda z: z.reshape(bsz, seq, z.shape[-1])

        rw = dict(mu=_row(rwkv_mu[l], RWKV_PAD - RWKV_COLS),
                  dup=_rows_at(rwkv_decay_up[l], 0, LANES), w0=_row(rwkv_w0[l]),
                  aup=_rows_at(rwkv_a_up[l], 64, LANES), a0=_row(rwkv_a0[l]),
                  gup=_rows_at(rwkv_gate_up[l], 0, 256).astype(BF16),
                  kk=_row(rwkv_k_k[l]), ka=_row(rwkv_k_a[l]), rk=_row(rwkv_r_k[l]),
                  lnw=_row(rwkv_ln_w[l]), lnb=_row(rwkv_ln_b[l]))
        if l > 0:
            rw.update(vup=_rows_at(vres_up[l - 1], 160, 256), v0=_row(vres_v0[l - 1]))
        o_a, v_first = _rwkv(shp(za), v_first, rw)
        o_b = _mamba(shp(zb), dict(cw=ssm_conv_w[l], cb=_row(ssm_conv_b[l]),
                                   dtb=_row(ssm_dt_bias[l], LANES - SSM_HEADS),
                                   alog=_row(ssm_a_log[l], LANES - SSM_HEADS),
                                   dexp=_row(jnp.repeat(ssm_d[l], CHUNK)), nw=_row(ssm_norm_w[l])))
        o_c = _attention(shp(zc), att_q_gain[l], att_k_gain[l], att_rel_bias)
        o_d = _gla(shp(zd), dict(gup=_rows_at(gla_gate_up[l], 0, LANES), gb=_row(gla_gate_bias[l]),
                                 nw=_row(gla_norm_w[l])))
        outs = [o.reshape(n_tok, BRANCH_DIM) for o in (o_a, o_b, o_c, o_d)]
        x1, hf = _merge(x2, h, outs, wg.astype(BF16), w_branch[l].astype(BF16), w_out[l].astype(BF16),
                        _row(norm_ffn[l]))
        if l % 2 == 0:
            x2 = _ffn(x1, hf, ffn_w1[l // 2].astype(BF16), ffn_w3[l // 2].astype(BF16),
                      ffn_w2[l // 2].astype(BF16))
        else:
            x2 = _moe(x1, _row(norm_ffn[l]), moe_router[l // 2], _to_bf16(moe_w1[l // 2]),
                      _to_bf16(moe_w3[l // 2]), _to_bf16(moe_w2[l // 2]))
    return x2.reshape(bsz, seq, D_MODEL)
```

```python
import functools

import jax
import jax.numpy as jnp
from jax import lax
from jax.experimental import pallas as pl
from jax.experimental.pallas import tpu as pltpu
from jax.experimental.pallas import tpu_sc as plsc

F32 = jnp.float32
BF16 = jnp.bfloat16
HIGHEST = lax.Precision.HIGHEST

D_MODEL = 1024
CHUNK = 64
BRANCH_DIM = 512
NORM_EPS = 1e-6
LANES = 128
VMEM_LIMIT = 56 * 1024 * 1024
EXPERT_VMEM_LIMIT = 60 * 1024 * 1024

LOG2_E = 1.4426950408889634

RWKV_LN_EPS = 64e-5
RWKV_COLS = 1824
RWKV_PAD = 1920
SSM_COLS = 1544
SSM_PAD = 1664
SSM_HEADS = 8
SSM_STATE = 128
ATT_COLS = 1536
ATT_HEADS = 8
ATT_LEFT = 8 * CHUNK
REL_CLIP = 2 * CHUNK
GLA_COLS = 1552
GLA_PAD = 1664
GLA_GATE_NORM = 16.0
FFN_DIM = 2816
N_EXPERTS = 8
EXPERT_DIM = 3584
MOE_ROWS = 256
ROW_SPLIT = 4
SUBROW = D_MODEL // ROW_SPLIT


def _dot(a, b):
    return jnp.dot(a.astype(BF16), b.astype(BF16), preferred_element_type=F32)


def _dot_nt(a, b):
    return lax.dot_general(a.astype(BF16), b.astype(BF16), (((1,), (1,)), ((), ())),
                           preferred_element_type=F32)


def _dot_tn(a, b):
    return lax.dot_general(a.astype(BF16), b.astype(BF16), (((0,), (0,)), ((), ())),
                           preferred_element_type=F32)


def _hi_lo(a):
    hi = a.astype(BF16)
    return hi, (a - hi.astype(F32)).astype(BF16)


def _dot_3x(a, b):
    a_hi, a_lo = _hi_lo(a)
    b_hi, b_lo = _hi_lo(b)
    return (jnp.dot(a_hi, b_hi, preferred_element_type=F32) + jnp.dot(a_lo, b_hi, preferred_element_type=F32)
            + jnp.dot(a_hi, b_lo, preferred_element_type=F32))


def _dot_split(a, m):
    m = m.astype(BF16)
    hi, lo = _hi_lo(a)
    return jnp.dot(hi, m, preferred_element_type=F32) + jnp.dot(lo, m, preferred_element_type=F32)


def _dot_split_l(m, a):
    m = m.astype(BF16)
    hi, lo = _hi_lo(a)
    return jnp.dot(m, hi, preferred_element_type=F32) + jnp.dot(m, lo, preferred_element_type=F32)


def _softplus(x):
    return jnp.maximum(x, 0.0) + jnp.log(1.0 + jnp.exp(-jnp.abs(x)))


def _sigmoid(x):
    return 1.0 / (1.0 + jnp.exp(-x))


def _silu(x):
    return x * _sigmoid(x)


def _iota2(shape, axis):
    return lax.broadcasted_iota(jnp.int32, shape, axis)


def _group_mean_matrix(n, group):
    r = _iota2((n, n), 0) // group
    c = _iota2((n, n), 1) // group
    return jnp.where(r == c, 1.0 / group, 0.0).astype(F32)


def _chunk_tri(n):
    r = _iota2((n, n), 0)
    c = _iota2((n, n), 1)
    return jnp.where((r // CHUNK == c // CHUNK) & (r >= c), 1.0, 0.0).astype(F32)


def _resident(shape):
    nd = len(shape)
    return pl.BlockSpec(shape, lambda *_: (0,) * nd, pipeline_mode=pl.Buffered(1))


def _params(*sem):
    return pltpu.CompilerParams(dimension_semantics=sem, vmem_limit_bytes=VMEM_LIMIT)


MIXER_COLS = (RWKV_PAD, SSM_PAD, ATT_COLS, GLA_PAD)


def _inproj_kernel(x_ref, g_ref, w_ref, h_ref, *z_refs):
    x = x_ref[...]
    ms = jnp.mean(x * x, axis=-1, keepdims=True)
    h = (x * lax.rsqrt(ms + NORM_EPS) * g_ref[...]).astype(BF16)
    h_ref[...] = h
    lo = 0
    for z_ref, n in zip(z_refs, MIXER_COLS):
        z_ref[...] = jnp.dot(h, w_ref[:, lo:lo + n], preferred_element_type=F32)
        lo += n


def _inproj(x2, g, w_mix, tm=512):
    n_tok = x2.shape[0]
    row = lambda n: pl.BlockSpec((tm, n), lambda i: (i, 0))
    return pl.pallas_call(
        _inproj_kernel,
        grid=(n_tok // tm,),
        in_specs=[row(D_MODEL), _resident((1, D_MODEL)), _resident(w_mix.shape)],
        out_specs=[row(D_MODEL)] + [row(n) for n in MIXER_COLS],
        out_shape=[jax.ShapeDtypeStruct((n_tok, D_MODEL), BF16)]
        + [jax.ShapeDtypeStruct((n_tok, n), F32) for n in MIXER_COLS],
        compiler_params=_params("parallel"),
        name="inproj",
    )(x2, g, w_mix)


def _rwkv_block(r, k, v, kk, ka, ld, tb, state_ref, y_ref):
    n_chunks = tb // CHUNK
    cum = _dot_split_l(_chunk_tri(tb), ld)
    p_inv = jnp.exp(-cum)
    a_t = -kk * jnp.exp(cum - ld)
    b_t = kk * ka * p_inv
    k_t = k * p_inv
    r_t = r * jnp.exp(cum)

    lane_head = _iota2((CHUNK, LANES), 1) // CHUNK
    row = _iota2((CHUNK, LANES), 0)
    col = _iota2((CHUNK, LANES), 1) % CHUNK
    strict = row > col
    incl = row >= col
    bd_mask = (_iota2((LANES, LANES), 0) // CHUNK) == (_iota2((LANES, LANES), 1) // CHUNK)
    eye = jnp.where(_iota2((LANES, LANES), 0) == _iota2((LANES, LANES), 1), 1.0, 0.0).astype(F32)

    def stack(z):
        return jnp.concatenate([jnp.where(lane_head == 0, z, 0.0), jnp.where(lane_head == 1, z, 0.0)], axis=0)

    def sub(z, u):
        c, p = u
        return z[c * CHUNK:(c + 1) * CHUNK, p * LANES:(p + 1) * LANES]

    units = [(c, p) for c in range(n_chunks) for p in range(4)]
    bf = lambda z: z.astype(BF16)
    gram = {u: _dot_nt(jnp.concatenate([sub(a_t, u), sub(r_t, u)], axis=0),
                       jnp.concatenate([stack(sub(b_t, u)), stack(sub(k_t, u))], axis=0))
            for u in units}
    a_rb = {u: bf(jnp.where(incl, gram[u][CHUNK:, :LANES], 0.0)) for u in units}
    a_akrk = {u: bf(jnp.concatenate([jnp.where(strict, gram[u][:CHUNK, LANES:], 0.0),
                                     jnp.where(incl, gram[u][CHUNK:, LANES:], 0.0)], axis=0)) for u in units}
    pw = {u: stack(jnp.where(strict, gram[u][:CHUNK, :LANES], 0.0)) for u in units}
    t_bd = {u: eye + pw[u] for u in units}
    pw = {u: _dot(pw[u], pw[u]) for u in units}
    for _ in range(CHUNK.bit_length() - 3):
        sp = {u: _dot(jnp.concatenate([t_bd[u], pw[u]], axis=0), pw[u]) for u in units}
        t_bd = {u: t_bd[u] + sp[u][:LANES] for u in units}
        pw = {u: sp[u][LANES:] for u in units}
    t_bd = {u: bf(t_bd[u] + _dot(t_bd[u], pw[u])) for u in units}
    v_bd = {u: bf(stack(sub(v, u))) for u in units}
    avyv = {u: _dot(a_akrk[u], v_bd[u]) for u in units}
    wuv = {u: _dot(t_bd[u], jnp.concatenate([stack(sub(a_t, u)), stack(avyv[u][:CHUNK])], axis=1))
           for u in units}
    wr = {u: bf(jnp.concatenate([wuv[u][:, :LANES], sub(r_t, u)], axis=0)) for u in units}

    h = [state_ref[p] for p in range(4)]
    for c in range(n_chunks):
        last = c * CHUNK + CHUNK - 1
        p_last = jnp.exp(cum[last:last + 1, :])
        hs = [_dot_nt(wr[(c, p)], h[p]) for p in range(4)]
        u2 = [hs[p][:LANES] + wuv[(c, p)][:, LANES:] for p in range(4)]
        u_p = [u2[p][:CHUNK] + u2[p][CHUNK:] for p in range(4)]
        upd = [_dot_tn(jnp.concatenate([u_p[p], sub(v, (c, p))], axis=0),
                       jnp.concatenate([sub(b_t, (c, p)), sub(k_t, (c, p))], axis=0)
                       * p_last[:, p * LANES:(p + 1) * LANES]) for p in range(4)]
        for p in range(4):
            y_ref[c * CHUNK:(c + 1) * CHUNK, p * LANES:(p + 1) * LANES] = (
                hs[p][LANES:] + _dot(a_rb[(c, p)], stack(u_p[p])) + avyv[(c, p)][CHUNK:])
        h = [h[p] * p_last[:, p * LANES:(p + 1) * LANES] + jnp.where(bd_mask, upd[p], 0.0) for p in range(4)]
    for p in range(4):
        state_ref[p] = h[p]


def _rwkv_kernel(has_vres, tb, *refs):
    if has_vres:
        (za_ref, vfirst_ref, mu_ref, dup_ref, w0_ref, aup_ref, a0_ref, gup_ref, vup_ref, v0_ref,
         kk_ref, ka_ref, rk_ref, lnw_ref, lnb_ref, o_ref, prev_ref, state_ref, y_ref) = refs
    else:
        (za_ref, mu_ref, dup_ref, w0_ref, aup_ref, a0_ref, gup_ref,
         kk_ref, ka_ref, rk_ref, lnw_ref, lnb_ref, o_ref, vraw_ref, prev_ref, state_ref, y_ref) = refs

    @pl.when(pl.program_id(1) == 0)
    def _():
        prev_ref[...] = jnp.zeros_like(prev_ref)
        state_ref[...] = jnp.zeros_like(state_ref)

    za = za_ref[...]
    shifted = pltpu.roll(za, 1, axis=0)
    shifted = jnp.where(_iota2(za.shape, 0) == 0, prev_ref[7:8, :], shifted)
    prev_ref[...] = za[tb - 8:, :]
    za = za + (shifted - za) * mu_ref[...]

    r = za[:, 0:512]
    k = za[:, 512:1024]
    v = za[:, 1024:1536]
    xwa = za[:, 1536:1664]
    xg = za[:, 1664:1920]
    w_log = -_softplus(-(w0_ref[...] + _dot_3x(jnp.tanh(xwa), dup_ref[...]))) - 0.5
    ld = -jnp.exp(w_log)
    a = _sigmoid(a0_ref[...] + _dot(xwa, aup_ref[...]))
    g = _dot(_sigmoid(xg), gup_ref[...])
    if has_vres:
        v_mix = _sigmoid(v0_ref[...] + _dot(xg, vup_ref[...]))
        v = v + (vfirst_ref[...] - v) * v_mix
    else:
        vraw_ref[...] = v
    kk = k * kk_ref[...]
    k = k * (1.0 + (a - 1.0) * ka_ref[...])
    head_sum = _group_mean_matrix(512, CHUNK) * float(CHUNK)
    kk = kk / jnp.maximum(jnp.sqrt(_dot(kk * kk, head_sum)), 1e-12)

    _rwkv_block(r, k, v, kk, a, ld, tb, state_ref, y_ref)
    y = y_ref[...]
    head_mean = _group_mean_matrix(512, CHUNK)
    mean = _dot_split(y, head_mean)
    yc = y - mean
    var = _dot(yc * yc, head_mean)
    y = yc * lax.rsqrt(var + RWKV_LN_EPS) * lnw_ref[...] + lnb_ref[...]
    y = y + _dot(r * k * rk_ref[...], head_sum) * v
    o_ref[...] = (y * g).astype(o_ref.dtype)


def _rwkv(za, v_first, prm, tb=256):
    bsz, seq, _ = za.shape
    has_vres = v_first is not None
    blk = lambda n: pl.BlockSpec((None, tb, n), lambda b, i: (b, i, 0))
    names = (["mu", "dup", "w0", "aup", "a0", "gup"] + (["vup", "v0"] if has_vres else [])
             + ["kk", "ka", "rk", "lnw", "lnb"])
    weights = [prm[n] for n in names]
    ins = [za] + ([v_first] if has_vres else []) + weights
    in_specs = [blk(RWKV_PAD)] + ([blk(512)] if has_vres else []) + [_resident(w.shape) for w in weights]
    out_shape = [jax.ShapeDtypeStruct((bsz, seq, 512), BF16)]
    out_specs = [blk(512)]
    if not has_vres:
        out_shape.append(jax.ShapeDtypeStruct((bsz, seq, 512), F32))
        out_specs.append(blk(512))
    res = pl.pallas_call(
        functools.partial(_rwkv_kernel, has_vres, tb),
        grid=(bsz, seq // tb),
        in_specs=in_specs,
        out_specs=out_specs,
        out_shape=out_shape,
        scratch_shapes=[pltpu.VMEM((8, RWKV_PAD), F32), pltpu.VMEM((4, LANES, LANES), F32),
                        pltpu.VMEM((tb, 512), F32)],
        compiler_params=_params("parallel", "arbitrary"),
        name="rwkv7",
    )(*ins)
    return (res[0], v_first) if has_vres else (res[0], res[1])


def _mamba_kernel(tb, zb_ref, cw_ref, cb_ref, dtb_ref, alog_ref, dexp_ref, nw_ref,
                  o_ref, xbuf_ref, state_ref, y_ref):
    @pl.when(pl.program_id(1) == 0)
    def _():
        xbuf_ref[0:8, :] = jnp.zeros((8, 1024), F32)
        state_ref[...] = jnp.zeros_like(state_ref)

    xbuf_ref[8:8 + tb, :] = zb_ref[:, 512:1536]
    conv = cb_ref[...]
    for i in range(4):
        conv = conv + cw_ref[i:i + 1, :] * xbuf_ref[5 + i:5 + i + tb, :]
    xbuf_ref[0:8, :] = xbuf_ref[tb:tb + 8, :]
    xbc = _silu(conv)
    xs = xbc[:, 0:512]
    bm = xbc[:, 512:768]
    cm = xbc[:, 768:1024]

    lane = _iota2((1, LANES), 1)
    dt = _softplus(zb_ref[:, 1536:1664] + dtb_ref[...])
    a_neg = jnp.where(lane < SSM_HEADS, -jnp.exp(alog_ref[...]), 0.0)
    acs = _dot_split_l(_chunk_tri(tb), dt * a_neg)
    expand = jnp.where(_iota2((LANES, 512), 0) == _iota2((LANES, 512), 1) // CHUNK, 1.0, 0.0)
    dt_e = _dot(dt, expand)
    acs_e = _dot_split(acs, expand)
    xdt = xs * dt_e

    n_chunks = tb // CHUNK
    causal = _iota2((CHUNK, 512), 0) >= _iota2((CHUNK, 512), 1) % CHUNK
    pair_head = _iota2((CHUNK, LANES), 1) // CHUNK
    spread = jnp.where(_iota2((CHUNK, 512), 0) == _iota2((CHUNK, 512), 1) % CHUNK, 1.0, 0.0)
    own_head = _iota2((SSM_HEADS, 512), 0) == _iota2((SSM_HEADS, 512), 1) // CHUNK
    rows_of = lambda c: slice(c * CHUNK, (c + 1) * CHUNK)
    grp = lambda g: slice(g * SSM_STATE, (g + 1) * SSM_STATE)

    def stack(z):
        return jnp.concatenate([jnp.where(pair_head == 0, z, 0.0), jnp.where(pair_head == 1, z, 0.0)], axis=0)

    y_intra, upd, e_in, e_last = [], [], [], []
    for c in range(n_chunks):
        rows = rows_of(c)
        acs_ec = acs_e[rows]
        acs_t = acs[rows].T[0:SSM_HEADS]
        acs_row = jnp.sum(jnp.where(own_head, _dot_split(acs_t, spread), 0.0), axis=0, keepdims=True)
        decay = jnp.exp(jnp.where(causal, acs_ec - acs_row, -jnp.inf))
        cb = jnp.concatenate(
            [_dot_nt(cm[rows, grp(g)], jnp.concatenate([bm[rows, grp(g)]] * 4, axis=0)) for g in range(2)], axis=1)
        m = cb * decay
        xdt_c = xdt[rows]
        y_intra.append([_dot(m[:, p * LANES:(p + 1) * LANES], stack(xdt_c[:, p * LANES:(p + 1) * LANES]))
                        for p in range(4)])
        last_e = acs_ec[CHUNK - 1:CHUNK, :]
        x_out = xdt_c * jnp.exp(last_e - acs_ec)
        upd.append([_dot_tn(bm[rows, grp(g)], x_out[:, g * 256:(g + 1) * 256]) for g in range(2)])
        e_in.append(jnp.exp(acs_ec))
        e_last.append(jnp.exp(last_e))

    state = [state_ref[g] for g in range(2)]
    for c in range(n_chunks):
        rows = rows_of(c)
        for g in range(2):
            ls = slice(g * 256, (g + 1) * 256)
            y_in = jnp.concatenate(y_intra[c][2 * g:2 * g + 2], axis=1)
            y_ref[rows, ls] = y_in + _dot(cm[rows, grp(g)], state[g]) * e_in[c][:, ls]
            state[g] = state[g] * e_last[c][:, ls] + upd[c][g]
    for g in range(2):
        state_ref[g] = state[g]

    y = (y_ref[...] + xs * dexp_ref[...]) * _silu(zb_ref[:, 0:512])
    for g in range(2):
        ls = slice(g * 256, (g + 1) * 256)
        yg = y[:, ls]
        ms = jnp.mean(yg * yg, axis=-1, keepdims=True)
        o_ref[:, ls] = (yg * lax.rsqrt(ms + NORM_EPS) * nw_ref[:, ls]).astype(o_ref.dtype)


def _mamba(zb, prm, tb=256):
    bsz, seq, _ = zb.shape
    blk = lambda n: pl.BlockSpec((None, tb, n), lambda b, i: (b, i, 0))
    weights = [prm[n] for n in ("cw", "cb", "dtb", "alog", "dexp", "nw")]
    return pl.pallas_call(
        functools.partial(_mamba_kernel, tb),
        grid=(bsz, seq // tb),
        in_specs=[blk(SSM_PAD)] + [_resident(w.shape) for w in weights],
        out_specs=blk(512),
        out_shape=jax.ShapeDtypeStruct((bsz, seq, 512), BF16),
        scratch_shapes=[pltpu.VMEM((tb + 8, 1024), F32), pltpu.VMEM((2, SSM_STATE, 256), F32),
                        pltpu.VMEM((tb, 512), F32)],
        compiler_params=_params("parallel", "arbitrary"),
        name="mamba2",
    )(zb, *weights)


def _attn_kernel(tq, seq, q_ref, k_ref, v_ref, qg_ref, kg_ref, bias_ref, o_ref, kn_ref, vb_ref):
    i = pl.program_id(1)
    head_mean = _group_mean_matrix(512, CHUNK)
    win = tq + ATT_LEFT

    @pl.when(i == 0)
    def _():
        kn_ref[0:ATT_LEFT, :] = jnp.zeros((ATT_LEFT, 512), BF16)
        vb_ref[0:ATT_LEFT, :] = jnp.zeros((ATT_LEFT, 512), BF16)
        for j in range(seq // tq):
            kj = k_ref[j * tq:(j + 1) * tq, :]
            ms = _dot(kj * kj, head_mean)
            kn_ref[ATT_LEFT + j * tq:ATT_LEFT + (j + 1) * tq, :] = (
                kj * lax.rsqrt(ms + NORM_EPS) * kg_ref[...]).astype(BF16)
            vb_ref[ATT_LEFT + j * tq:ATT_LEFT + (j + 1) * tq, :] = v_ref[j * tq:(j + 1) * tq, :].astype(BF16)

    q = q_ref[...]
    ms = _dot(q * q, head_mean)
    qn = q * lax.rsqrt(ms + NORM_EPS) * qg_ref[...] * (CHUNK ** -0.5 * LOG2_E)
    start = pl.multiple_of(i * tq, tq)
    kwin = kn_ref[pl.ds(start, win), :]
    vwin = vb_ref[pl.ds(start, win), :]
    lane_head = _iota2((tq, LANES), 1) // CHUNK
    for p in range(4):
        ls = slice(p * LANES, (p + 1) * LANES)
        q2 = jnp.concatenate([jnp.where(lane_head == s, qn[:, ls], 0.0) for s in range(2)], axis=0)
        sc = _dot_nt(q2, kwin[:, ls]) + bias_ref[2 * p:2 * p + 2].reshape(2 * tq, win)
        e = jnp.exp2(sc - jnp.max(sc, axis=-1, keepdims=True))
        o2 = _dot(e, vwin[:, ls]) / jnp.sum(e, axis=-1, keepdims=True)
        o_ref[:, ls] = jnp.where(lane_head == 0, o2[:tq], o2[tq:]).astype(o_ref.dtype)


def _band_bias(rel_bias, tq, win):
    period = tq + win
    m = jnp.arange(period)
    d = jnp.where(m < win, m, m - period) - ATT_LEFT
    f = rel_bias.astype(F32)[:, jnp.clip(d, -REL_CLIP, REL_CLIP) + REL_CLIP]
    g = jnp.tile(f, (1, tq))[:, :tq * (period - 1)].reshape(-1, tq, period - 1)
    bias = g[:, :, :win] * LOG2_E
    left = ATT_LEFT // CHUNK
    qc = left + jnp.arange(tq)[:, None] // CHUNK
    kc = jnp.arange(win)[None, :] // CHUNK
    tiles = []
    for blk in range(ATT_LEFT // tq + 1):
        first_kc = jnp.maximum(qc - left, left - blk * (tq // CHUNK))
        tiles.append(jnp.where((kc <= qc) & (kc >= first_kc), bias, -jnp.inf))
    return jnp.stack(tiles)


def _attention(zc, q_gain, k_gain, rel_bias, tq=256):
    bsz, seq, _ = zc.shape
    win = tq + ATT_LEFT
    bias = _band_bias(rel_bias, tq, win)
    qg = jnp.tile(q_gain, ATT_HEADS)[None, :]
    kg = jnp.tile(k_gain, ATT_HEADS)[None, :]
    full = lambda col: pl.BlockSpec((None, seq, 512), lambda b, i: (b, 0, col))
    last_tile = bias.shape[0] - 1
    bias_spec = pl.BlockSpec((None,) + bias.shape[1:], lambda b, i: (jnp.minimum(i, last_tile), 0, 0, 0))
    return pl.pallas_call(
        functools.partial(_attn_kernel, tq, seq),
        grid=(bsz, seq // tq),
        in_specs=[pl.BlockSpec((None, tq, 512), lambda b, i: (b, i, 0)), full(1), full(2),
                  _resident(qg.shape), _resident(kg.shape), bias_spec],
        out_specs=pl.BlockSpec((None, tq, 512), lambda b, i: (b, i, 0)),
        out_shape=jax.ShapeDtypeStruct((bsz, seq, 512), BF16),
        scratch_shapes=[pltpu.VMEM((ATT_LEFT + seq, 512), BF16), pltpu.VMEM((ATT_LEFT + seq, 512), BF16)],
        compiler_params=_params("parallel", "arbitrary"),
        name="band_attention",
    )(zc, zc, zc, qg, kg, bias)


def _gla_kernel(tb, zd_ref, gup_ref, gb_ref, nw_ref, o_ref, state_ref, y_ref):
    @pl.when(pl.program_id(1) == 0)
    def _():
        state_ref[...] = jnp.zeros_like(state_ref)

    q = zd_ref[:, 0:256] * (CHUNK ** -0.5)
    k = zd_ref[:, 256:512]
    v = zd_ref[:, 512:1024]
    log_a = -_softplus(-(_dot_3x(zd_ref[:, 1536:1664], gup_ref[...]) + gb_ref[...])) / GLA_GATE_NORM
    bcum = _dot_split_l(_chunk_tri(tb), log_a)

    k_head = _iota2((CHUNK, 256), 1) // CHUNK
    v_head = _iota2((CHUNK, 512), 1) // LANES
    causal = _iota2((CHUNK, 256), 0) >= _iota2((CHUNK, 256), 1) % CHUNK
    bd = (_iota2((512, 256), 0) // LANES) == (_iota2((512, 256), 1) // CHUNK)
    n_chunks = tb // CHUNK
    qg_all = (q * jnp.exp(bcum)).astype(BF16)
    kg_all = k * jnp.exp(-bcum)
    qg, att, o_intra, upd, e_last = [], [], [], [], []
    for c in range(n_chunks):
        rows = slice(c * CHUNK, (c + 1) * CHUNK)
        kg_bd = jnp.concatenate([jnp.where(k_head == h, kg_all[rows], 0.0) for h in range(4)], axis=0)
        qg.append(qg_all[rows])
        att.append(jnp.where(causal, _dot_nt(qg[c], kg_bd), 0.0))
    for c in range(n_chunks):
        rows = slice(c * CHUNK, (c + 1) * CHUNK)
        bc = bcum[rows]
        blast = bc[CHUNK - 1:CHUNK, :]
        v_c = v[rows]
        v_bd = jnp.concatenate([jnp.where(v_head == h, v_c, 0.0) for h in range(4)], axis=0)
        o_intra.append(_dot(att[c], v_bd))
        upd.append(jnp.where(bd, _dot_tn(v_c, k[rows] * jnp.exp(blast - bc)), 0.0))
        e_last.append(jnp.exp(blast))
    st = state_ref[...]
    for c in range(n_chunks):
        y_ref[c * CHUNK:(c + 1) * CHUNK, :] = o_intra[c] + _dot_nt(qg[c], st)
        st = st * e_last[c] + upd[c]
    state_ref[...] = st

    o = y_ref[...]
    for h in range(4):
        ls = slice(h * LANES, (h + 1) * LANES)
        oh = o[:, ls]
        ms = jnp.mean(oh * oh, axis=-1, keepdims=True)
        o_ref[:, ls] = (oh * lax.rsqrt(ms + NORM_EPS) * nw_ref[...]
                        * _silu(zd_ref[:, 1024 + h * LANES:1024 + (h + 1) * LANES])).astype(o_ref.dtype)


def _gla(zd, prm, tb=256):
    bsz, seq, _ = zd.shape
    blk = lambda n: pl.BlockSpec((None, tb, n), lambda b, i: (b, i, 0))
    weights = [prm[n] for n in ("gup", "gb", "nw")]
    return pl.pallas_call(
        functools.partial(_gla_kernel, tb),
        grid=(bsz, seq // tb),
        in_specs=[blk(GLA_PAD)] + [_resident(w.shape) for w in weights],
        out_specs=blk(512),
        out_shape=jax.ShapeDtypeStruct((bsz, seq, 512), BF16),
        scratch_shapes=[pltpu.VMEM((512, 256), F32), pltpu.VMEM((tb, 512), F32)],
        compiler_params=_params("parallel", "arbitrary"),
        name="gla",
    )(zd, *weights)


def _merge_kernel(x_ref, h_ref, oa_ref, ob_ref, oc_ref, od_ref, wg_ref, wb_ref, wo_ref, nf_ref,
                  x1_ref, hf_ref):
    h = h_ref[...]
    acc = jnp.zeros(x_ref.shape, F32)
    for i, o_ref in enumerate((oa_ref, ob_ref, oc_ref, od_ref)):
        gate = _sigmoid(jnp.dot(h, wg_ref[:, i * D_MODEL:(i + 1) * D_MODEL], preferred_element_type=F32))
        acc = acc + gate * jnp.dot(o_ref[...], wb_ref[i], preferred_element_type=F32)
    x1 = x_ref[...] + jnp.dot(acc.astype(BF16), wo_ref[...], preferred_element_type=F32)
    x1_ref[...] = x1
    ms = jnp.mean(x1 * x1, axis=-1, keepdims=True)
    hf_ref[...] = (x1 * lax.rsqrt(ms + NORM_EPS) * nf_ref[...]).astype(BF16)


def _merge(x2, h, outs, wg, wb, wo, nf, tm=512):
    n_tok = x2.shape[0]
    row = lambda n: pl.BlockSpec((tm, n), lambda i: (i, 0))
    return pl.pallas_call(
        _merge_kernel,
        grid=(n_tok // tm,),
        in_specs=[row(D_MODEL), row(D_MODEL)] + [row(BRANCH_DIM)] * 4
        + [_resident(wg.shape), _resident(wb.shape), _resident(wo.shape), _resident(nf.shape)],
        out_specs=[row(D_MODEL), row(D_MODEL)],
        out_shape=[jax.ShapeDtypeStruct((n_tok, D_MODEL), F32), jax.ShapeDtypeStruct((n_tok, D_MODEL), BF16)],
        compiler_params=_params("parallel"),
        name="merge",
    )(x2, h, *outs, wg, wb, wo, nf)


def _ffn_kernel(tf, x1_ref, hf_ref, w1_ref, w3_ref, w2_ref, o_ref):
    hf = hf_ref[...]
    acc = x1_ref[...]
    for lo in range(0, FFN_DIM, tf):
        cols = slice(lo, min(lo + tf, FFN_DIM))
        a = jnp.dot(hf, w1_ref[:, cols], preferred_element_type=F32)
        b = jnp.dot(hf, w3_ref[:, cols], preferred_element_type=F32)
        acc = acc + jnp.dot((_silu(a) * b).astype(BF16), w2_ref[cols, :], preferred_element_type=F32)
    o_ref[...] = acc


def _ffn(x1, hf, w1, w3, w2, tm=512, tf=512):
    n_tok = x1.shape[0]
    row = pl.BlockSpec((tm, D_MODEL), lambda i: (i, 0))
    return pl.pallas_call(
        functools.partial(_ffn_kernel, tf),
        grid=(n_tok // tm,),
        in_specs=[row, row, _resident(w1.shape), _resident(w3.shape), _resident(w2.shape)],
        out_specs=row,
        out_shape=jax.ShapeDtypeStruct((n_tok, D_MODEL), F32),
        compiler_params=_params("parallel"),
        name="ffn",
    )(x1, hf, w1, w3, w2)


def _router_kernel(tm, x1_ref, nf_ref, wr_ref, idx_ref, gate_ref, rank_ref, cnt_ref, hf_ref, carry_ref):
    @pl.when(pl.program_id(0) == 0)
    def _():
        carry_ref[...] = jnp.zeros_like(carry_ref)

    x1 = x1_ref[...]
    ms = jnp.mean(x1 * x1, axis=-1, keepdims=True)
    hf = x1 * lax.rsqrt(ms + NORM_EPS) * nf_ref[...]
    for j in range(ROW_SPLIT):
        hf_ref[j] = hf[:, j * SUBROW:(j + 1) * SUBROW]
    logits = lax.dot_general(wr_ref[...], hf, (((1,), (1,)), ((), ())), precision=HIGHEST,
                             preferred_element_type=F32)
    e_iota = _iota2((N_EXPERTS, tm), 0)
    m1 = jnp.max(logits, axis=0, keepdims=True)
    i1 = jnp.min(jnp.where(logits == m1, e_iota, N_EXPERTS), axis=0, keepdims=True)
    rest = jnp.where(e_iota == i1, -jnp.inf, logits)
    m2 = jnp.max(rest, axis=0, keepdims=True)
    i2 = jnp.min(jnp.where(rest == m2, e_iota, N_EXPERTS), axis=0, keepdims=True)
    e2 = jnp.exp(m2 - m1)
    gate_ref[0:1, :] = 1.0 / (1.0 + e2)
    gate_ref[1:2, :] = e2 / (1.0 + e2)
    idx_ref[0:1, :] = i1
    idx_ref[1:2, :] = i2
    hit1 = jnp.where(e_iota == i1, 1.0, 0.0)
    hit2 = jnp.where(e_iota == i2, 1.0, 0.0)
    before = jnp.where(_iota2((tm, tm), 0) < _iota2((tm, tm), 1), 1.0, 0.0)
    prior = _dot(hit1 + hit2, before) + carry_ref[:, 0:1]
    rank_ref[0:1, :] = jnp.sum(hit1 * prior, axis=0, keepdims=True).astype(jnp.int32)
    rank_ref[1:2, :] = jnp.sum(hit2 * prior, axis=0, keepdims=True).astype(jnp.int32)
    carry_ref[...] = carry_ref[...] + jnp.sum(hit1 + hit2, axis=1, keepdims=True)
    cnt_ref[...] = carry_ref[...]


def _router(x1, nf, wr_t, tm=512):
    n_tok = x1.shape[0]
    col = pl.BlockSpec((2, tm), lambda i: (0, i))
    return pl.pallas_call(
        functools.partial(_router_kernel, tm),
        grid=(n_tok // tm,),
        in_specs=[pl.BlockSpec((tm, D_MODEL), lambda i: (i, 0)), _resident(nf.shape), _resident(wr_t.shape)],
        out_specs=[col, col, col, pl.BlockSpec((N_EXPERTS, LANES), lambda i: (0, 0)),
                   pl.BlockSpec((ROW_SPLIT, tm, SUBROW), lambda i: (0, i, 0))],
        out_shape=[jax.ShapeDtypeStruct((2, n_tok), jnp.int32), jax.ShapeDtypeStruct((2, n_tok), F32),
                   jax.ShapeDtypeStruct((2, n_tok), jnp.int32), jax.ShapeDtypeStruct((N_EXPERTS, LANES), F32),
                   jax.ShapeDtypeStruct((ROW_SPLIT, n_tok, SUBROW), F32)],
        scratch_shapes=[pltpu.VMEM((N_EXPERTS, LANES), F32)],
        compiler_params=_params("arbitrary"),
        name="moe_router",
    )(x1, nf, wr_t)


def _gather_rows(table, idx, window=128):
    split, n_table, width = table.shape
    n = idx.shape[0]
    flat_idx = (idx[None, :] + n_table * jnp.arange(split, dtype=jnp.int32)[:, None]).reshape(-1)
    return _gather_subrows(table.reshape(split * n_table, width), flat_idx, window).reshape(split, n, width)


def _gather_subrows(table, idx, window):
    n = idx.shape[0]
    d = table.shape[1]
    mesh = plsc.VectorSubcoreMesh(core_axis_name="core", subcore_axis_name="subcore")

    @functools.partial(pl.kernel, out_type=jax.ShapeDtypeStruct((n, d), table.dtype), mesh=mesh,
                       name="gather_rows")
    def gather(table_hbm, idx_hbm, out_hbm):
        def body(idx_vmem, out_vmem):
            pltpu.sync_copy(table_hbm.at[idx_vmem.at[0]], out_vmem)

        pltpu.emit_pipeline(
            body,
            grid=(n // window,),
            in_specs=[pl.BlockSpec((1, window), index_map=lambda i: (0, i))],
            out_specs=[pl.BlockSpec((window, d), index_map=lambda i: (i, 0))],
            core_axis_name=("core", "subcore"),
            dimension_semantics=(pltpu.PARALLEL,),
        )(idx_hbm, out_hbm)

    return gather(table, idx.reshape(1, n))


def _expert_kernel(tf, ge_ref, nv_ref, x_ref, w1_ref, w3_ref, w2_ref, y_ref):
    g = pl.program_id(0)

    @pl.when(g < nv_ref[0])
    def _():
        x = jnp.concatenate([x_ref[j] for j in range(ROW_SPLIT)], axis=-1).astype(BF16)
        acc = jnp.zeros((MOE_ROWS, D_MODEL), F32)
        for lo in range(0, EXPERT_DIM, tf):
            a = jnp.dot(x, w1_ref[:, lo:lo + tf], preferred_element_type=F32)
            b = jnp.dot(x, w3_ref[:, lo:lo + tf], preferred_element_type=F32)
            acc = acc + jnp.dot((_silu(a) * b).astype(BF16), w2_ref[lo:lo + tf, :], preferred_element_type=F32)
        for j in range(ROW_SPLIT):
            y_ref[j] = acc[:, j * SUBROW:(j + 1) * SUBROW]

    @pl.when(g >= nv_ref[0])
    def _():
        y_ref[...] = jnp.zeros_like(y_ref)


def _experts(xg, group_expert, n_valid, w1, w3, w2, tf=512):
    n_rows = xg.shape[1]
    rows = pl.BlockSpec((ROW_SPLIT, MOE_ROWS, SUBROW), lambda g, ge, nv: (0, g, 0))
    return pl.pallas_call(
        functools.partial(_expert_kernel, tf),
        grid_spec=pltpu.PrefetchScalarGridSpec(
            num_scalar_prefetch=2,
            grid=(n_rows // MOE_ROWS,),
            in_specs=[rows,
                      pl.BlockSpec((None, D_MODEL, EXPERT_DIM), lambda g, ge, nv: (ge[g], 0, 0)),
                      pl.BlockSpec((None, D_MODEL, EXPERT_DIM), lambda g, ge, nv: (ge[g], 0, 0)),
                      pl.BlockSpec((None, EXPERT_DIM, D_MODEL), lambda g, ge, nv: (ge[g], 0, 0))],
            out_specs=rows,
        ),
        out_shape=jax.ShapeDtypeStruct((ROW_SPLIT, n_rows, SUBROW), F32),
        compiler_params=pltpu.CompilerParams(dimension_semantics=("arbitrary",),
                                             vmem_limit_bytes=EXPERT_VMEM_LIMIT),
        name="experts",
    )(group_expert, n_valid, xg, w1, w3, w2)


def _cast_kernel(w_ref, o_ref):
    o_ref[...] = w_ref[...].astype(o_ref.dtype)


def _to_bf16(w):
    n, rows, cols = w.shape
    blk = pl.BlockSpec((None, rows // 2, cols), lambda i, j: (i, j, 0))
    return pl.pallas_call(
        _cast_kernel,
        grid=(n, 2),
        in_specs=[blk],
        out_specs=blk,
        out_shape=jax.ShapeDtypeStruct(w.shape, BF16),
        compiler_params=_params("parallel", "parallel"),
        name="cast_bf16",
    )(w)


def _combine_kernel(x1_ref, y0_ref, y1_ref, gate_ref, o_ref):
    gate = gate_ref[...]
    for j in range(ROW_SPLIT):
        cols = slice(j * SUBROW, (j + 1) * SUBROW)
        o_ref[:, cols] = x1_ref[:, cols] + gate[:, 0:1] * y0_ref[j] + gate[:, 1:2] * y1_ref[j]


def _combine(x1, yg, gate_t, tm=512):
    n_tok = x1.shape[0]
    nb = n_tok // tm
    return pl.pallas_call(
        _combine_kernel,
        grid=(nb,),
        in_specs=[pl.BlockSpec((tm, D_MODEL), lambda i: (i, 0)),
                  pl.BlockSpec((ROW_SPLIT, tm, SUBROW), lambda i: (0, i, 0)),
                  pl.BlockSpec((ROW_SPLIT, tm, SUBROW), lambda i: (0, i + nb, 0)),
                  pl.BlockSpec((tm, 2), lambda i: (i, 0))],
        out_specs=pl.BlockSpec((tm, D_MODEL), lambda i: (i, 0)),
        out_shape=jax.ShapeDtypeStruct((n_tok, D_MODEL), F32),
        compiler_params=_params("parallel"),
        name="moe_combine",
    )(x1, yg, yg, gate_t)


def _moe(x1, nf, w_router, w1, w3, w2):
    n_tok = x1.shape[0]
    idx, gate, rank, cnt, hf = _router(x1, nf, w_router.T)
    counts = cnt[:, 0].astype(jnp.int32)
    padded = (counts + MOE_ROWS - 1) // MOE_ROWS * MOE_ROWS
    end_padded = jnp.cumsum(padded)
    start_padded = end_padded - padded
    start = sum(jnp.where(idx == e, start_padded[e], 0) for e in range(N_EXPERTS))
    dest = (start + rank).reshape(-1)
    n_groups = (n_tok * 2 + MOE_ROWS - 1) // MOE_ROWS + N_EXPERTS
    n_rows = n_groups * MOE_ROWS
    group_row = jnp.arange(n_groups, dtype=jnp.int32)[:, None] * MOE_ROWS
    group_expert = jnp.minimum(jnp.sum(group_row >= end_padded[None, :], axis=1), N_EXPERTS - 1).astype(jnp.int32)
    n_valid = (end_padded[-1:] // MOE_ROWS).astype(jnp.int32)
    tok = jnp.tile(jnp.arange(n_tok, dtype=jnp.int32), 2)
    row_token = jnp.zeros((n_rows,), jnp.int32).at[dest].set(tok)
    xg = _gather_rows(hf, row_token)
    y = _experts(xg, group_expert, n_valid, w1, w3, w2)
    return _combine(x1, _gather_rows(y, dest), gate.T)


def _row(v, pad=0):
    v = v.reshape(1, -1).astype(F32)
    return jnp.pad(v, ((0, 0), (0, pad))) if pad else v


def _rows_at(w, start, total):
    return jnp.pad(w.astype(F32), ((start, total - start - w.shape[0]), (0, 0)))


def kernel(x, w_in, norm_mix, rwkv_mu, rwkv_decay_up, rwkv_w0, rwkv_a_up, rwkv_a0, rwkv_gate_up, rwkv_k_k, rwkv_k_a, rwkv_r_k, rwkv_ln_w, rwkv_ln_b, vres_down, vres_up, vres_v0, ssm_conv_w, ssm_conv_b, ssm_dt_bias, ssm_a_log, ssm_d, ssm_norm_w, att_q_gain, att_k_gain, att_rel_bias, gla_gate_up, gla_gate_bias, gla_norm_w, w_branch, w_out, norm_ffn, ffn_w1, ffn_w3, ffn_w2, moe_router, moe_w1, moe_w3, moe_w2):
    bsz, seq, _ = x.shape
    n_tok = bsz * seq
    depth = w_in.shape[0]
    x2 = x.reshape(n_tok, D_MODEL)
    v_first = None
    o_rw, o_ss, o_at = RWKV_COLS, RWKV_COLS + SSM_COLS, RWKV_COLS + SSM_COLS + ATT_COLS
    o_gl = o_at + GLA_COLS
    for l in range(depth):
        w = w_in[l]
        zeros = lambda n: jnp.zeros((D_MODEL, n), BF16)
        cols = lambda lo, hi: w[:, lo:hi].astype(BF16)
        vd = vres_down[l - 1].astype(BF16) if l > 0 else zeros(32)
        w_mix = jnp.concatenate(
            [cols(0, o_rw), vd, zeros(RWKV_PAD - RWKV_COLS - 32),
             cols(o_rw, o_ss), zeros(SSM_PAD - SSM_COLS),
             cols(o_ss, o_at),
             cols(o_at, o_at + 1024), cols(o_at + 1040, o_gl), cols(o_at + 1024, o_at + 1040),
             zeros(GLA_PAD - GLA_COLS)], axis=1)
        wg = w[:, o_gl:]
        h, za, zb, zc, zd = _inproj(x2, _row(norm_mix[l]), w_mix)
        shp = lambda z: z.reshape(bsz, seq, z.shape[-1])

        rw = dict(mu=_row(rwkv_mu[l], RWKV_PAD - RWKV_COLS),
                  dup=_rows_at(rwkv_decay_up[l], 0, LANES), w0=_row(rwkv_w0[l]),
                  aup=_rows_at(rwkv_a_up[l], 64, LANES), a0=_row(rwkv_a0[l]),
                  gup=_rows_at(rwkv_gate_up[l], 0, 256).astype(BF16),
                  kk=_row(rwkv_k_k[l]), ka=_row(rwkv_k_a[l]), rk=_row(rwkv_r_k[l]),
                  lnw=_row(rwkv_ln_w[l]), lnb=_row(rwkv_ln_b[l]))
        if l > 0:
            rw.update(vup=_rows_at(vres_up[l - 1], 160, 256), v0=_row(vres_v0[l - 1]))
        o_a, v_first = _rwkv(shp(za), v_first, rw)
        o_b = _mamba(shp(zb), dict(cw=ssm_conv_w[l], cb=_row(ssm_conv_b[l]),
                                   dtb=_row(ssm_dt_bias[l], LANES - SSM_HEADS),
                                   alog=_row(ssm_a_log[l], LANES - SSM_HEADS),
                                   dexp=_row(jnp.repeat(ssm_d[l], CHUNK)), nw=_row(ssm_norm_w[l])))
        o_c = _attention(shp(zc), att_q_gain[l], att_k_gain[l], att_rel_bias)
        o_d = _gla(shp(zd), dict(gup=_rows_at(gla_gate_up[l], 0, LANES), gb=_row(gla_gate_bias[l]),
                                 nw=_row(gla_norm_w[l])))
        outs = [o.reshape(n_tok, BRANCH_DIM) for o in (o_a, o_b, o_c, o_d)]
        x1, hf = _merge(x2, h, outs, wg.astype(BF16), w_branch[l].astype(BF16), w_out[l].astype(BF16),
                        _row(norm_ffn[l]))
        if l % 2 == 0:
            x2 = _ffn(x1, hf, ffn_w1[l // 2].astype(BF16), ffn_w3[l // 2].astype(BF16),
                      ffn_w2[l // 2].astype(BF16))
        else:
            x2 = _moe(x1, _row(norm_ffn[l]), moe_router[l // 2], _to_bf16(moe_w1[l // 2]),
                      _to_bf16(moe_w3[l // 2]), _to_bf16(moe_w2[l // 2]))
    return x2.reshape(bsz, seq, D_MODEL)
```

```python
import functools

import jax
import jax.numpy as jnp
from jax import lax
from jax.experimental import pallas as pl
from jax.experimental.pallas import tpu as pltpu
from jax.experimental.pallas import tpu_sc as plsc

F32 = jnp.float32
BF16 = jnp.bfloat16
HIGHEST = lax.Precision.HIGHEST

D_MODEL = 1024
CHUNK = 64
BRANCH_DIM = 512
NORM_EPS = 1e-6
LANES = 128
VMEM_LIMIT = 56 * 1024 * 1024
EXPERT_VMEM_LIMIT = 60 * 1024 * 1024

LOG2_E = 1.4426950408889634

RWKV_LN_EPS = 64e-5
RWKV_UNIT_GROUP = 16
RWKV_COLS = 1824
RWKV_PAD = 1920
SSM_COLS = 1544
SSM_PAD = 1664
SSM_HEADS = 8
SSM_STATE = 128
ATT_COLS = 1536
ATT_HEADS = 8
ATT_LEFT = 8 * CHUNK
REL_CLIP = 2 * CHUNK
GLA_COLS = 1552
GLA_PAD = 1664
GLA_GATE_NORM = 16.0
FFN_DIM = 2816
N_EXPERTS = 8
EXPERT_DIM = 3584
MOE_ROWS = 256
ROW_SPLIT = 4
SUBROW = D_MODEL // ROW_SPLIT


def _dot(a, b):
    return jnp.dot(a.astype(BF16), b.astype(BF16), preferred_element_type=F32)


def _dot_nt(a, b):
    return lax.dot_general(a.astype(BF16), b.astype(BF16), (((1,), (1,)), ((), ())),
                           preferred_element_type=F32)


def _dot_tn(a, b):
    return lax.dot_general(a.astype(BF16), b.astype(BF16), (((0,), (0,)), ((), ())),
                           preferred_element_type=F32)


def _hi_lo(a):
    hi = a.astype(BF16)
    return hi, (a - hi.astype(F32)).astype(BF16)


def _dot_3x(a, b):
    a_hi, a_lo = _hi_lo(a)
    b_hi, b_lo = _hi_lo(b)
    return (jnp.dot(a_hi, b_hi, preferred_element_type=F32) + jnp.dot(a_lo, b_hi, preferred_element_type=F32)
            + jnp.dot(a_hi, b_lo, preferred_element_type=F32))


def _dot_split(a, m):
    m = m.astype(BF16)
    hi, lo = _hi_lo(a)
    return jnp.dot(hi, m, preferred_element_type=F32) + jnp.dot(lo, m, preferred_element_type=F32)


def _dot_split_l(m, a):
    m = m.astype(BF16)
    hi, lo = _hi_lo(a)
    return jnp.dot(m, hi, preferred_element_type=F32) + jnp.dot(m, lo, preferred_element_type=F32)


def _softplus(x):
    return jnp.maximum(x, 0.0) + jnp.log(1.0 + jnp.exp(-jnp.abs(x)))


def _sigmoid(x):
    return 1.0 / (1.0 + jnp.exp(-x))


def _silu(x):
    return x * _sigmoid(x)


def _iota2(shape, axis):
    return lax.broadcasted_iota(jnp.int32, shape, axis)


def _group_mean_matrix(n, group):
    r = _iota2((n, n), 0) // group
    c = _iota2((n, n), 1) // group
    return jnp.where(r == c, 1.0 / group, 0.0).astype(F32)


def _chunk_tri(n):
    r = _iota2((n, n), 0)
    c = _iota2((n, n), 1)
    return jnp.where((r // CHUNK == c // CHUNK) & (r >= c), 1.0, 0.0).astype(F32)


def _resident(shape):
    nd = len(shape)
    return pl.BlockSpec(shape, lambda *_: (0,) * nd, pipeline_mode=pl.Buffered(1))


def _params(*sem):
    return pltpu.CompilerParams(dimension_semantics=sem, vmem_limit_bytes=VMEM_LIMIT)


MIXER_COLS = (RWKV_PAD, SSM_PAD, ATT_COLS, GLA_PAD)


def _inproj_kernel(x_ref, g_ref, w_ref, h_ref, *z_refs):
    x = x_ref[...]
    ms = jnp.mean(x * x, axis=-1, keepdims=True)
    h = (x * lax.rsqrt(ms + NORM_EPS) * g_ref[...]).astype(BF16)
    h_ref[...] = h
    lo = 0
    for z_ref, n in zip(z_refs, MIXER_COLS):
        z_ref[...] = jnp.dot(h, w_ref[:, lo:lo + n], preferred_element_type=F32)
        lo += n


def _inproj(x2, g, w_mix, tm=512):
    n_tok = x2.shape[0]
    row = lambda n: pl.BlockSpec((tm, n), lambda i: (i, 0))
    return pl.pallas_call(
        _inproj_kernel,
        grid=(n_tok // tm,),
        in_specs=[row(D_MODEL), _resident((1, D_MODEL)), _resident(w_mix.shape)],
        out_specs=[row(D_MODEL)] + [row(n) for n in MIXER_COLS],
        out_shape=[jax.ShapeDtypeStruct((n_tok, D_MODEL), BF16)]
        + [jax.ShapeDtypeStruct((n_tok, n), F32) for n in MIXER_COLS],
        compiler_params=_params("parallel"),
        name="inproj",
    )(x2, g, w_mix)


def _rwkv_block(r, k, v, kk, ka, ld, tb, state_ref, y_ref):
    n_chunks = tb // CHUNK
    cum = _dot_split_l(_chunk_tri(tb), ld)
    p_inv = jnp.exp(-cum)
    a_t = -kk * jnp.exp(cum - ld)
    b_t = kk * ka * p_inv
    k_t = k * p_inv
    r_t = r * jnp.exp(cum)

    lane_head = _iota2((CHUNK, LANES), 1) // CHUNK
    row = _iota2((CHUNK, LANES), 0)
    col = _iota2((CHUNK, LANES), 1) % CHUNK
    strict = row > col
    incl = row >= col
    bd_mask = (_iota2((LANES, LANES), 0) // CHUNK) == (_iota2((LANES, LANES), 1) // CHUNK)
    eye = jnp.where(_iota2((LANES, LANES), 0) == _iota2((LANES, LANES), 1), 1.0, 0.0).astype(F32)

    def stack(z):
        return jnp.concatenate([jnp.where(lane_head == 0, z, 0.0), jnp.where(lane_head == 1, z, 0.0)], axis=0)

    def sub(z, u):
        c, p = u
        return z[c * CHUNK:(c + 1) * CHUNK, p * LANES:(p + 1) * LANES]

    bf = lambda z: z.astype(BF16)

    def prepare(units):
        gram = {u: _dot_nt(jnp.concatenate([sub(a_t, u), sub(r_t, u)], axis=0),
                           jnp.concatenate([stack(sub(b_t, u)), stack(sub(k_t, u))], axis=0))
                for u in units}
        a_rb = {u: bf(jnp.where(incl, gram[u][CHUNK:, :LANES], 0.0)) for u in units}
        a_akrk = {u: bf(jnp.concatenate([jnp.where(strict, gram[u][:CHUNK, LANES:], 0.0),
                                         jnp.where(incl, gram[u][CHUNK:, LANES:], 0.0)], axis=0)) for u in units}
        pw = {u: stack(jnp.where(strict, gram[u][:CHUNK, :LANES], 0.0)) for u in units}
        t_bd = {u: eye + pw[u] for u in units}
        pw = {u: _dot(pw[u], pw[u]) for u in units}
        for _ in range(CHUNK.bit_length() - 3):
            sp = {u: _dot(jnp.concatenate([t_bd[u], pw[u]], axis=0), pw[u]) for u in units}
            t_bd = {u: t_bd[u] + sp[u][:LANES] for u in units}
            pw = {u: sp[u][LANES:] for u in units}
        t_bd = {u: bf(t_bd[u] + _dot(t_bd[u], pw[u])) for u in units}
        v_bd = {u: bf(stack(sub(v, u))) for u in units}
        avyv = {u: _dot(a_akrk[u], v_bd[u]) for u in units}
        wuv = {u: _dot(t_bd[u], jnp.concatenate([stack(sub(a_t, u)), stack(avyv[u][:CHUNK])], axis=1))
               for u in units}
        wr = {u: bf(jnp.concatenate([wuv[u][:, :LANES], sub(r_t, u)], axis=0)) for u in units}
        return a_rb, avyv, wuv, wr

    a_rb, avyv, wuv, wr = {}, {}, {}, {}
    all_units = [(c, p) for c in range(n_chunks) for p in range(4)]
    for i in range(0, len(all_units), RWKV_UNIT_GROUP):
        for acc, part in zip((a_rb, avyv, wuv, wr), prepare(all_units[i:i + RWKV_UNIT_GROUP])):
            acc.update(part)

    h = [state_ref[p] for p in range(4)]
    for c in range(n_chunks):
        last = c * CHUNK + CHUNK - 1
        p_last = jnp.exp(cum[last:last + 1, :])
        hs = [_dot_nt(wr[(c, p)], h[p]) for p in range(4)]
        u2 = [hs[p][:LANES] + wuv[(c, p)][:, LANES:] for p in range(4)]
        u_p = [u2[p][:CHUNK] + u2[p][CHUNK:] for p in range(4)]
        upd = [_dot_tn(jnp.concatenate([u_p[p], sub(v, (c, p))], axis=0),
                       jnp.concatenate([sub(b_t, (c, p)), sub(k_t, (c, p))], axis=0)
                       * p_last[:, p * LANES:(p + 1) * LANES]) for p in range(4)]
        for p in range(4):
            y_ref[c * CHUNK:(c + 1) * CHUNK, p * LANES:(p + 1) * LANES] = (
                hs[p][LANES:] + _dot(a_rb[(c, p)], stack(u_p[p])) + avyv[(c, p)][CHUNK:])
        h = [h[p] * p_last[:, p * LANES:(p + 1) * LANES] + jnp.where(bd_mask, upd[p], 0.0) for p in range(4)]
    for p in range(4):
        state_ref[p] = h[p]


def _rwkv_kernel(has_vres, tb, *refs):
    if has_vres:
        (za_ref, vfirst_ref, mu_ref, dup_ref, w0_ref, aup_ref, a0_ref, gup_ref, vup_ref, v0_ref,
         kk_ref, ka_ref, rk_ref, lnw_ref, lnb_ref, o_ref, prev_ref, state_ref, y_ref) = refs
    else:
        (za_ref, mu_ref, dup_ref, w0_ref, aup_ref, a0_ref, gup_ref,
         kk_ref, ka_ref, rk_ref, lnw_ref, lnb_ref, o_ref, vraw_ref, prev_ref, state_ref, y_ref) = refs

    @pl.when(pl.program_id(1) == 0)
    def _():
        prev_ref[...] = jnp.zeros_like(prev_ref)
        state_ref[...] = jnp.zeros_like(state_ref)

    za = za_ref[...]
    shifted = pltpu.roll(za, 1, axis=0)
    shifted = jnp.where(_iota2(za.shape, 0) == 0, prev_ref[7:8, :], shifted)
    prev_ref[...] = za[tb - 8:, :]
    za = za + (shifted - za) * mu_ref[...]

    r = za[:, 0:512]
    k = za[:, 512:1024]
    v = za[:, 1024:1536]
    xwa = za[:, 1536:1664]
    xg = za[:, 1664:1920]
    w_log = -_softplus(-(w0_ref[...] + _dot_3x(jnp.tanh(xwa), dup_ref[...]))) - 0.5
    ld = -jnp.exp(w_log)
    a = _sigmoid(a0_ref[...] + _dot(xwa, aup_ref[...]))
    g = _dot(_sigmoid(xg), gup_ref[...])
    if has_vres:
        v_mix = _sigmoid(v0_ref[...] + _dot(xg, vup_ref[...]))
        v = v + (vfirst_ref[...] - v) * v_mix
    else:
        vraw_ref[...] = v
    kk = k * kk_ref[...]
    k = k * (1.0 + (a - 1.0) * ka_ref[...])
    head_sum = _group_mean_matrix(512, CHUNK) * float(CHUNK)
    kk = kk / jnp.maximum(jnp.sqrt(_dot(kk * kk, head_sum)), 1e-12)

    _rwkv_block(r, k, v, kk, a, ld, tb, state_ref, y_ref)
    y = y_ref[...]
    head_mean = _group_mean_matrix(512, CHUNK)
    mean = _dot_split(y, head_mean)
    yc = y - mean
    var = _dot(yc * yc, head_mean)
    y = yc * lax.rsqrt(var + RWKV_LN_EPS) * lnw_ref[...] + lnb_ref[...]
    y = y + _dot(r * k * rk_ref[...], head_sum) * v
    o_ref[...] = (y * g).astype(o_ref.dtype)


def _rwkv(za, v_first, prm, tb=256):
    bsz, seq, _ = za.shape
    has_vres = v_first is not None
    blk = lambda n: pl.BlockSpec((None, tb, n), lambda b, i: (b, i, 0))
    names = (["mu", "dup", "w0", "aup", "a0", "gup"] + (["vup", "v0"] if has_vres else [])
             + ["kk", "ka", "rk", "lnw", "lnb"])
    weights = [prm[n] for n in names]
    ins = [za] + ([v_first] if has_vres else []) + weights
    in_specs = [blk(RWKV_PAD)] + ([blk(512)] if has_vres else []) + [_resident(w.shape) for w in weights]
    out_shape = [jax.ShapeDtypeStruct((bsz, seq, 512), BF16)]
    out_specs = [blk(512)]
    if not has_vres:
        out_shape.append(jax.ShapeDtypeStruct((bsz, seq, 512), F32))
        out_specs.append(blk(512))
    res = pl.pallas_call(
        functools.partial(_rwkv_kernel, has_vres, tb),
        grid=(bsz, seq // tb),
        in_specs=in_specs,
        out_specs=out_specs,
        out_shape=out_shape,
        scratch_shapes=[pltpu.VMEM((8, RWKV_PAD), F32), pltpu.VMEM((4, LANES, LANES), F32),
                        pltpu.VMEM((tb, 512), F32)],
        compiler_params=_params("parallel", "arbitrary"),
        name="rwkv7",
    )(*ins)
    return (res[0], v_first) if has_vres else (res[0], res[1])


def _mamba_kernel(tb, zb_ref, cw_ref, cb_ref, dtb_ref, alog_ref, dexp_ref, nw_ref,
                  o_ref, xbuf_ref, state_ref, y_ref):
    @pl.when(pl.program_id(1) == 0)
    def _():
        xbuf_ref[0:8, :] = jnp.zeros((8, 1024), F32)
        state_ref[...] = jnp.zeros_like(state_ref)

    xbuf_ref[8:8 + tb, :] = zb_ref[:, 512:1536]
    conv = cb_ref[...]
    for i in range(4):
        conv = conv + cw_ref[i:i + 1, :] * xbuf_ref[5 + i:5 + i + tb, :]
    xbuf_ref[0:8, :] = xbuf_ref[tb:tb + 8, :]
    xbc = _silu(conv)
    xs = xbc[:, 0:512]
    bm = xbc[:, 512:768]
    cm = xbc[:, 768:1024]

    lane = _iota2((1, LANES), 1)
    dt = _softplus(zb_ref[:, 1536:1664] + dtb_ref[...])
    a_neg = jnp.where(lane < SSM_HEADS, -jnp.exp(alog_ref[...]), 0.0)
    acs = _dot_split_l(_chunk_tri(tb), dt * a_neg)
    expand = jnp.where(_iota2((LANES, 512), 0) == _iota2((LANES, 512), 1) // CHUNK, 1.0, 0.0)
    dt_e = _dot(dt, expand)
    acs_e = _dot_split(acs, expand)
    xdt = xs * dt_e

    n_chunks = tb // CHUNK
    causal = _iota2((CHUNK, 512), 0) >= _iota2((CHUNK, 512), 1) % CHUNK
    pair_head = _iota2((CHUNK, LANES), 1) // CHUNK
    spread = jnp.where(_iota2((CHUNK, 512), 0) == _iota2((CHUNK, 512), 1) % CHUNK, 1.0, 0.0)
    own_head = _iota2((SSM_HEADS, 512), 0) == _iota2((SSM_HEADS, 512), 1) // CHUNK
    rows_of = lambda c: slice(c * CHUNK, (c + 1) * CHUNK)
    grp = lambda g: slice(g * SSM_STATE, (g + 1) * SSM_STATE)

    def stack(z):
        return jnp.concatenate([jnp.where(pair_head == 0, z, 0.0), jnp.where(pair_head == 1, z, 0.0)], axis=0)

    y_intra, upd, e_in, e_last = [], [], [], []
    for c in range(n_chunks):
        rows = rows_of(c)
        acs_ec = acs_e[rows]
        acs_t = acs[rows].T[0:SSM_HEADS]
        acs_row = jnp.sum(jnp.where(own_head, _dot_split(acs_t, spread), 0.0), axis=0, keepdims=True)
        decay = jnp.exp(jnp.where(causal, acs_ec - acs_row, -jnp.inf))
        cb = jnp.concatenate(
            [_dot_nt(cm[rows, grp(g)], jnp.concatenate([bm[rows, grp(g)]] * 4, axis=0)) for g in range(2)], axis=1)
        m = cb * decay
        xdt_c = xdt[rows]
        y_intra.append([_dot(m[:, p * LANES:(p + 1) * LANES], stack(xdt_c[:, p * LANES:(p + 1) * LANES]))
                        for p in range(4)])
        last_e = acs_ec[CHUNK - 1:CHUNK, :]
        x_out = xdt_c * jnp.exp(last_e - acs_ec)
        upd.append([_dot_tn(bm[rows, grp(g)], x_out[:, g * 256:(g + 1) * 256]) for g in range(2)])
        e_in.append(jnp.exp(acs_ec))
        e_last.append(jnp.exp(last_e))

    state = [state_ref[g] for g in range(2)]
    for c in range(n_chunks):
        rows = rows_of(c)
        for g in range(2):
            ls = slice(g * 256, (g + 1) * 256)
            y_in = jnp.concatenate(y_intra[c][2 * g:2 * g + 2], axis=1)
            y_ref[rows, ls] = y_in + _dot(cm[rows, grp(g)], state[g]) * e_in[c][:, ls]
            state[g] = state[g] * e_last[c][:, ls] + upd[c][g]
    for g in range(2):
        state_ref[g] = state[g]

    y = (y_ref[...] + xs * dexp_ref[...]) * _silu(zb_ref[:, 0:512])
    for g in range(2):
        ls = slice(g * 256, (g + 1) * 256)
        yg = y[:, ls]
        ms = jnp.mean(yg * yg, axis=-1, keepdims=True)
        o_ref[:, ls] = (yg * lax.rsqrt(ms + NORM_EPS) * nw_ref[:, ls]).astype(o_ref.dtype)


def _mamba(zb, prm, tb=256):
    bsz, seq, _ = zb.shape
    blk = lambda n: pl.BlockSpec((None, tb, n), lambda b, i: (b, i, 0))
    weights = [prm[n] for n in ("cw", "cb", "dtb", "alog", "dexp", "nw")]
    return pl.pallas_call(
        functools.partial(_mamba_kernel, tb),
        grid=(bsz, seq // tb),
        in_specs=[blk(SSM_PAD)] + [_resident(w.shape) for w in weights],
        out_specs=blk(512),
        out_shape=jax.ShapeDtypeStruct((bsz, seq, 512), BF16),
        scratch_shapes=[pltpu.VMEM((tb + 8, 1024), F32), pltpu.VMEM((2, SSM_STATE, 256), F32),
                        pltpu.VMEM((tb, 512), F32)],
        compiler_params=_params("parallel", "arbitrary"),
        name="mamba2",
    )(zb, *weights)


def _attn_kernel(tq, seq, q_ref, k_ref, v_ref, qg_ref, kg_ref, bias_ref, o_ref, kn_ref, vb_ref):
    i = pl.program_id(1)
    head_mean = _group_mean_matrix(512, CHUNK)
    win = tq + ATT_LEFT

    @pl.when(i == 0)
    def _():
        kn_ref[0:ATT_LEFT, :] = jnp.zeros((ATT_LEFT, 512), BF16)
        vb_ref[0:ATT_LEFT, :] = jnp.zeros((ATT_LEFT, 512), BF16)
        for j in range(seq // tq):
            kj = k_ref[j * tq:(j + 1) * tq, :]
            ms = _dot(kj * kj, head_mean)
            kn_ref[ATT_LEFT + j * tq:ATT_LEFT + (j + 1) * tq, :] = (
                kj * lax.rsqrt(ms + NORM_EPS) * kg_ref[...]).astype(BF16)
            vb_ref[ATT_LEFT + j * tq:ATT_LEFT + (j + 1) * tq, :] = v_ref[j * tq:(j + 1) * tq, :].astype(BF16)

    q = q_ref[...]
    ms = _dot(q * q, head_mean)
    qn = q * lax.rsqrt(ms + NORM_EPS) * qg_ref[...] * (CHUNK ** -0.5 * LOG2_E)
    start = pl.multiple_of(i * tq, tq)
    kwin = kn_ref[pl.ds(start, win), :]
    vwin = vb_ref[pl.ds(start, win), :]
    lane_head = _iota2((tq, LANES), 1) // CHUNK
    for p in range(4):
        ls = slice(p * LANES, (p + 1) * LANES)
        q2 = jnp.concatenate([jnp.where(lane_head == s, qn[:, ls], 0.0) for s in range(2)], axis=0)
        sc = _dot_nt(q2, kwin[:, ls]) + bias_ref[2 * p:2 * p + 2].reshape(2 * tq, win)
        e = jnp.exp2(sc - jnp.max(sc, axis=-1, keepdims=True))
        o2 = _dot(e, vwin[:, ls]) / jnp.sum(e, axis=-1, keepdims=True)
        o_ref[:, ls] = jnp.where(lane_head == 0, o2[:tq], o2[tq:]).astype(o_ref.dtype)


def _band_bias(rel_bias, tq, win):
    period = tq + win
    m = jnp.arange(period)
    d = jnp.where(m < win, m, m - period) - ATT_LEFT
    f = rel_bias.astype(F32)[:, jnp.clip(d, -REL_CLIP, REL_CLIP) + REL_CLIP]
    g = jnp.tile(f, (1, tq))[:, :tq * (period - 1)].reshape(-1, tq, period - 1)
    bias = g[:, :, :win] * LOG2_E
    left = ATT_LEFT // CHUNK
    qc = left + jnp.arange(tq)[:, None] // CHUNK
    kc = jnp.arange(win)[None, :] // CHUNK
    tiles = []
    for blk in range(ATT_LEFT // tq + 1):
        first_kc = jnp.maximum(qc - left, left - blk * (tq // CHUNK))
        tiles.append(jnp.where((kc <= qc) & (kc >= first_kc), bias, -jnp.inf))
    return jnp.stack(tiles)


def _attention(zc, q_gain, k_gain, rel_bias, tq=256):
    bsz, seq, _ = zc.shape
    win = tq + ATT_LEFT
    bias = _band_bias(rel_bias, tq, win)
    qg = jnp.tile(q_gain, ATT_HEADS)[None, :]
    kg = jnp.tile(k_gain, ATT_HEADS)[None, :]
    full = lambda col: pl.BlockSpec((None, seq, 512), lambda b, i: (b, 0, col))
    last_tile = bias.shape[0] - 1
    bias_spec = pl.BlockSpec((None,) + bias.shape[1:], lambda b, i: (jnp.minimum(i, last_tile), 0, 0, 0))
    return pl.pallas_call(
        functools.partial(_attn_kernel, tq, seq),
        grid=(bsz, seq // tq),
        in_specs=[pl.BlockSpec((None, tq, 512), lambda b, i: (b, i, 0)), full(1), full(2),
                  _resident(qg.shape), _resident(kg.shape), bias_spec],
        out_specs=pl.BlockSpec((None, tq, 512), lambda b, i: (b, i, 0)),
        out_shape=jax.ShapeDtypeStruct((bsz, seq, 512), BF16),
        scratch_shapes=[pltpu.VMEM((ATT_LEFT + seq, 512), BF16), pltpu.VMEM((ATT_LEFT + seq, 512), BF16)],
        compiler_params=_params("parallel", "arbitrary"),
        name="band_attention",
    )(zc, zc, zc, qg, kg, bias)


def _gla_kernel(tb, zd_ref, gup_ref, gb_ref, nw_ref, o_ref, state_ref, y_ref):
    @pl.when(pl.program_id(1) == 0)
    def _():
        state_ref[...] = jnp.zeros_like(state_ref)

    q = zd_ref[:, 0:256] * (CHUNK ** -0.5)
    k = zd_ref[:, 256:512]
    v = zd_ref[:, 512:1024]
    log_a = -_softplus(-(_dot_3x(zd_ref[:, 1536:1664], gup_ref[...]) + gb_ref[...])) / GLA_GATE_NORM
    bcum = _dot_split_l(_chunk_tri(tb), log_a)

    k_head = _iota2((CHUNK, 256), 1) // CHUNK
    v_head = _iota2((CHUNK, 512), 1) // LANES
    causal = _iota2((CHUNK, 256), 0) >= _iota2((CHUNK, 256), 1) % CHUNK
    bd = (_iota2((512, 256), 0) // LANES) == (_iota2((512, 256), 1) // CHUNK)
    n_chunks = tb // CHUNK
    qg_all = (q * jnp.exp(bcum)).astype(BF16)
    kg_all = k * jnp.exp(-bcum)
    qg, att, o_intra, upd, e_last = [], [], [], [], []
    for c in range(n_chunks):
        rows = slice(c * CHUNK, (c + 1) * CHUNK)
        kg_bd = jnp.concatenate([jnp.where(k_head == h, kg_all[rows], 0.0) for h in range(4)], axis=0)
        qg.append(qg_all[rows])
        att.append(jnp.where(causal, _dot_nt(qg[c], kg_bd), 0.0))
    for c in range(n_chunks):
        rows = slice(c * CHUNK, (c + 1) * CHUNK)
        bc = bcum[rows]
        blast = bc[CHUNK - 1:CHUNK, :]
        v_c = v[rows]
        v_bd = jnp.concatenate([jnp.where(v_head == h, v_c, 0.0) for h in range(4)], axis=0)
        o_intra.append(_dot(att[c], v_bd))
        upd.append(jnp.where(bd, _dot_tn(v_c, k[rows] * jnp.exp(blast - bc)), 0.0))
        e_last.append(jnp.exp(blast))
    st = state_ref[...]
    for c in range(n_chunks):
        y_ref[c * CHUNK:(c + 1) * CHUNK, :] = o_intra[c] + _dot_nt(qg[c], st)
        st = st * e_last[c] + upd[c]
    state_ref[...] = st

    o = y_ref[...]
    for h in range(4):
        ls = slice(h * LANES, (h + 1) * LANES)
        oh = o[:, ls]
        ms = jnp.mean(oh * oh, axis=-1, keepdims=True)
        o_ref[:, ls] = (oh * lax.rsqrt(ms + NORM_EPS) * nw_ref[...]
                        * _silu(zd_ref[:, 1024 + h * LANES:1024 + (h + 1) * LANES])).astype(o_ref.dtype)


def _gla(zd, prm, tb=256):
    bsz, seq, _ = zd.shape
    blk = lambda n: pl.BlockSpec((None, tb, n), lambda b, i: (b, i, 0))
    weights = [prm[n] for n in ("gup", "gb", "nw")]
    return pl.pallas_call(
        functools.partial(_gla_kernel, tb),
        grid=(bsz, seq // tb),
        in_specs=[blk(GLA_PAD)] + [_resident(w.shape) for w in weights],
        out_specs=blk(512),
        out_shape=jax.ShapeDtypeStruct((bsz, seq, 512), BF16),
        scratch_shapes=[pltpu.VMEM((512, 256), F32), pltpu.VMEM((tb, 512), F32)],
        compiler_params=_params("parallel", "arbitrary"),
        name="gla",
    )(zd, *weights)


def _merge_kernel(x_ref, h_ref, oa_ref, ob_ref, oc_ref, od_ref, wg_ref, wb_ref, wo_ref, nf_ref,
                  x1_ref, hf_ref):
    h = h_ref[...]
    acc = jnp.zeros(x_ref.shape, F32)
    for i, o_ref in enumerate((oa_ref, ob_ref, oc_ref, od_ref)):
        gate = _sigmoid(jnp.dot(h, wg_ref[:, i * D_MODEL:(i + 1) * D_MODEL], preferred_element_type=F32))
        acc = acc + gate * jnp.dot(o_ref[...], wb_ref[i], preferred_element_type=F32)
    x1 = x_ref[...] + jnp.dot(acc.astype(BF16), wo_ref[...], preferred_element_type=F32)
    x1_ref[...] = x1
    ms = jnp.mean(x1 * x1, axis=-1, keepdims=True)
    hf_ref[...] = (x1 * lax.rsqrt(ms + NORM_EPS) * nf_ref[...]).astype(BF16)


def _merge(x2, h, outs, wg, wb, wo, nf, tm=512):
    n_tok = x2.shape[0]
    row = lambda n: pl.BlockSpec((tm, n), lambda i: (i, 0))
    return pl.pallas_call(
        _merge_kernel,
        grid=(n_tok // tm,),
        in_specs=[row(D_MODEL), row(D_MODEL)] + [row(BRANCH_DIM)] * 4
        + [_resident(wg.shape), _resident(wb.shape), _resident(wo.shape), _resident(nf.shape)],
        out_specs=[row(D_MODEL), row(D_MODEL)],
        out_shape=[jax.ShapeDtypeStruct((n_tok, D_MODEL), F32), jax.ShapeDtypeStruct((n_tok, D_MODEL), BF16)],
        compiler_params=_params("parallel"),
        name="merge",
    )(x2, h, *outs, wg, wb, wo, nf)


def _ffn_kernel(tf, x1_ref, hf_ref, w1_ref, w3_ref, w2_ref, o_ref):
    hf = hf_ref[...]
    acc = x1_ref[...]
    for lo in range(0, FFN_DIM, tf):
        cols = slice(lo, min(lo + tf, FFN_DIM))
        a = jnp.dot(hf, w1_ref[:, cols], preferred_element_type=F32)
        b = jnp.dot(hf, w3_ref[:, cols], preferred_element_type=F32)
        acc = acc + jnp.dot((_silu(a) * b).astype(BF16), w2_ref[cols, :], preferred_element_type=F32)
    o_ref[...] = acc


def _ffn(x1, hf, w1, w3, w2, tm=512, tf=512):
    n_tok = x1.shape[0]
    row = pl.BlockSpec((tm, D_MODEL), lambda i: (i, 0))
    return pl.pallas_call(
        functools.partial(_ffn_kernel, tf),
        grid=(n_tok // tm,),
        in_specs=[row, row, _resident(w1.shape), _resident(w3.shape), _resident(w2.shape)],
        out_specs=row,
        out_shape=jax.ShapeDtypeStruct((n_tok, D_MODEL), F32),
        compiler_params=_params("parallel"),
        name="ffn",
    )(x1, hf, w1, w3, w2)


def _router_kernel(tm, x1_ref, nf_ref, wr_ref, idx_ref, gate_ref, rank_ref, cnt_ref, hf_ref, carry_ref):
    @pl.when(pl.program_id(0) == 0)
    def _():
        carry_ref[...] = jnp.zeros_like(carry_ref)

    x1 = x1_ref[...]
    ms = jnp.mean(x1 * x1, axis=-1, keepdims=True)
    hf = x1 * lax.rsqrt(ms + NORM_EPS) * nf_ref[...]
    for j in range(ROW_SPLIT):
        hf_ref[j] = hf[:, j * SUBROW:(j + 1) * SUBROW]
    logits = lax.dot_general(wr_ref[...], hf, (((1,), (1,)), ((), ())), precision=HIGHEST,
                             preferred_element_type=F32)
    e_iota = _iota2((N_EXPERTS, tm), 0)
    m1 = jnp.max(logits, axis=0, keepdims=True)
    i1 = jnp.min(jnp.where(logits == m1, e_iota, N_EXPERTS), axis=0, keepdims=True)
    rest = jnp.where(e_iota == i1, -jnp.inf, logits)
    m2 = jnp.max(rest, axis=0, keepdims=True)
    i2 = jnp.min(jnp.where(rest == m2, e_iota, N_EXPERTS), axis=0, keepdims=True)
    e2 = jnp.exp(m2 - m1)
    gate_ref[0:1, :] = 1.0 / (1.0 + e2)
    gate_ref[1:2, :] = e2 / (1.0 + e2)
    idx_ref[0:1, :] = i1
    idx_ref[1:2, :] = i2
    hit1 = jnp.where(e_iota == i1, 1.0, 0.0)
    hit2 = jnp.where(e_iota == i2, 1.0, 0.0)
    before = jnp.where(_iota2((tm, tm), 0) < _iota2((tm, tm), 1), 1.0, 0.0)
    prior = _dot(hit1 + hit2, before) + carry_ref[:, 0:1]
    rank_ref[0:1, :] = jnp.sum(hit1 * prior, axis=0, keepdims=True).astype(jnp.int32)
    rank_ref[1:2, :] = jnp.sum(hit2 * prior, axis=0, keepdims=True).astype(jnp.int32)
    carry_ref[...] = carry_ref[...] + jnp.sum(hit1 + hit2, axis=1, keepdims=True)
    cnt_ref[...] = carry_ref[...]


def _router(x1, nf, wr_t, tm=512):
    n_tok = x1.shape[0]
    col = pl.BlockSpec((2, tm), lambda i: (0, i))
    return pl.pallas_call(
        functools.partial(_router_kernel, tm),
        grid=(n_tok // tm,),
        in_specs=[pl.BlockSpec((tm, D_MODEL), lambda i: (i, 0)), _resident(nf.shape), _resident(wr_t.shape)],
        out_specs=[col, col, col, pl.BlockSpec((N_EXPERTS, LANES), lambda i: (0, 0)),
                   pl.BlockSpec((ROW_SPLIT, tm, SUBROW), lambda i: (0, i, 0))],
        out_shape=[jax.ShapeDtypeStruct((2, n_tok), jnp.int32), jax.ShapeDtypeStruct((2, n_tok), F32),
                   jax.ShapeDtypeStruct((2, n_tok), jnp.int32), jax.ShapeDtypeStruct((N_EXPERTS, LANES), F32),
                   jax.ShapeDtypeStruct((ROW_SPLIT, n_tok, SUBROW), F32)],
        scratch_shapes=[pltpu.VMEM((N_EXPERTS, LANES), F32)],
        compiler_params=_params("arbitrary"),
        name="moe_router",
    )(x1, nf, wr_t)


def _gather_rows(table, idx, window=128):
    split, n_table, width = table.shape
    n = idx.shape[0]
    flat_idx = (idx[None, :] + n_table * jnp.arange(split, dtype=jnp.int32)[:, None]).reshape(-1)
    return _gather_subrows(table.reshape(split * n_table, width), flat_idx, window).reshape(split, n, width)


def _gather_subrows(table, idx, window):
    n = idx.shape[0]
    d = table.shape[1]
    mesh = plsc.VectorSubcoreMesh(core_axis_name="core", subcore_axis_name="subcore")

    @functools.partial(pl.kernel, out_type=jax.ShapeDtypeStruct((n, d), table.dtype), mesh=mesh,
                       name="gather_rows")
    def gather(table_hbm, idx_hbm, out_hbm):
        def body(idx_vmem, out_vmem):
            pltpu.sync_copy(table_hbm.at[idx_vmem.at[0]], out_vmem)

        pltpu.emit_pipeline(
            body,
            grid=(n // window,),
            in_specs=[pl.BlockSpec((1, window), index_map=lambda i: (0, i))],
            out_specs=[pl.BlockSpec((window, d), index_map=lambda i: (i, 0))],
            core_axis_name=("core", "subcore"),
            dimension_semantics=(pltpu.PARALLEL,),
        )(idx_hbm, out_hbm)

    return gather(table, idx.reshape(1, n))


def _scatter_rows(table, dest, n_out, window=128):
    split, n_table, width = table.shape
    n = dest.shape[0]
    flat_dest = (dest[None, :] + n_out * jnp.arange(split, dtype=jnp.int32)[:, None]).reshape(1, split * n)
    src_blocks, blocks = n_table // window, n // window
    mesh = plsc.VectorSubcoreMesh(core_axis_name="core", subcore_axis_name="subcore")

    @functools.partial(pl.kernel, out_type=jax.ShapeDtypeStruct((split * n_out, width), table.dtype), mesh=mesh,
                       name="scatter_rows")
    def scatter(table_hbm, dest_hbm, out_hbm):
        def body(rows_vmem, dest_vmem):
            pltpu.sync_copy(rows_vmem, out_hbm.at[dest_vmem.at[0]])

        pltpu.emit_pipeline(
            body,
            grid=(split * blocks,),
            in_specs=[pl.BlockSpec((window, width),
                                   index_map=lambda i: ((i // blocks) * src_blocks + (i % blocks) % src_blocks, 0)),
                      pl.BlockSpec((1, window), index_map=lambda i: (0, i))],
            out_specs=[],
            core_axis_name=("core", "subcore"),
            dimension_semantics=(pltpu.PARALLEL,),
        )(table_hbm, dest_hbm)

    return scatter(table.reshape(split * n_table, width), flat_dest).reshape(split, n_out, width)


def _expert_kernel(tf, ge_ref, nv_ref, x_ref, w1_ref, w3_ref, w2_ref, y_ref):
    g = pl.program_id(0)

    @pl.when(g < nv_ref[0])
    def _():
        x = jnp.concatenate([x_ref[j] for j in range(ROW_SPLIT)], axis=-1).astype(BF16)
        acc = jnp.zeros((MOE_ROWS, D_MODEL), F32)
        for lo in range(0, EXPERT_DIM, tf):
            a = jnp.dot(x, w1_ref[:, lo:lo + tf], preferred_element_type=F32)
            b = jnp.dot(x, w3_ref[:, lo:lo + tf], preferred_element_type=F32)
            acc = acc + jnp.dot((_silu(a) * b).astype(BF16), w2_ref[lo:lo + tf, :], preferred_element_type=F32)
        for j in range(ROW_SPLIT):
            y_ref[j] = acc[:, j * SUBROW:(j + 1) * SUBROW]

    @pl.when(g >= nv_ref[0])
    def _():
        y_ref[...] = jnp.zeros_like(y_ref)


def _experts(xg, group_expert, n_valid, w1, w3, w2, tf=512):
    n_rows = xg.shape[1]
    rows = pl.BlockSpec((ROW_SPLIT, MOE_ROWS, SUBROW), lambda g, ge, nv: (0, g, 0))
    return pl.pallas_call(
        functools.partial(_expert_kernel, tf),
        grid_spec=pltpu.PrefetchScalarGridSpec(
            num_scalar_prefetch=2,
            grid=(n_rows // MOE_ROWS,),
            in_specs=[rows,
                      pl.BlockSpec((None, D_MODEL, EXPERT_DIM), lambda g, ge, nv: (ge[g], 0, 0)),
                      pl.BlockSpec((None, D_MODEL, EXPERT_DIM), lambda g, ge, nv: (ge[g], 0, 0)),
                      pl.BlockSpec((None, EXPERT_DIM, D_MODEL), lambda g, ge, nv: (ge[g], 0, 0))],
            out_specs=rows,
        ),
        out_shape=jax.ShapeDtypeStruct((ROW_SPLIT, n_rows, SUBROW), F32),
        compiler_params=pltpu.CompilerParams(dimension_semantics=("arbitrary",),
                                             vmem_limit_bytes=EXPERT_VMEM_LIMIT),
        name="experts",
    )(group_expert, n_valid, xg, w1, w3, w2)


def _cast_kernel(w_ref, o_ref):
    o_ref[...] = w_ref[...].astype(o_ref.dtype)


def _to_bf16(w):
    n, rows, cols = w.shape
    blk = pl.BlockSpec((None, rows // 2, cols), lambda i, j: (i, j, 0))
    return pl.pallas_call(
        _cast_kernel,
        grid=(n, 2),
        in_specs=[blk],
        out_specs=blk,
        out_shape=jax.ShapeDtypeStruct(w.shape, BF16),
        compiler_params=_params("parallel", "parallel"),
        name="cast_bf16",
    )(w)


def _combine_kernel(x1_ref, y0_ref, y1_ref, gate_ref, o_ref):
    gate = gate_ref[...]
    for j in range(ROW_SPLIT):
        cols = slice(j * SUBROW, (j + 1) * SUBROW)
        o_ref[:, cols] = x1_ref[:, cols] + gate[:, 0:1] * y0_ref[j] + gate[:, 1:2] * y1_ref[j]


def _combine(x1, yg, gate_t, tm=512):
    n_tok = x1.shape[0]
    nb = n_tok // tm
    return pl.pallas_call(
        _combine_kernel,
        grid=(nb,),
        in_specs=[pl.BlockSpec((tm, D_MODEL), lambda i: (i, 0)),
                  pl.BlockSpec((ROW_SPLIT, tm, SUBROW), lambda i: (0, i, 0)),
                  pl.BlockSpec((ROW_SPLIT, tm, SUBROW), lambda i: (0, i + nb, 0)),
                  pl.BlockSpec((tm, 2), lambda i: (i, 0))],
        out_specs=pl.BlockSpec((tm, D_MODEL), lambda i: (i, 0)),
        out_shape=jax.ShapeDtypeStruct((n_tok, D_MODEL), F32),
        compiler_params=_params("parallel"),
        name="moe_combine",
    )(x1, yg, yg, gate_t)


def _moe(x1, nf, w_router, w1, w3, w2):
    n_tok = x1.shape[0]
    idx, gate, rank, cnt, hf = _router(x1, nf, w_router.T)
    counts = cnt[:, 0].astype(jnp.int32)
    padded = (counts + MOE_ROWS - 1) // MOE_ROWS * MOE_ROWS
    end_padded = jnp.cumsum(padded)
    start_padded = end_padded - padded
    start = sum(jnp.where(idx == e, start_padded[e], 0) for e in range(N_EXPERTS))
    dest = (start + rank).reshape(-1)
    n_groups = (n_tok * 2 + MOE_ROWS - 1) // MOE_ROWS + N_EXPERTS
    n_rows = n_groups * MOE_ROWS
    group_row = jnp.arange(n_groups, dtype=jnp.int32)[:, None] * MOE_ROWS
    group_expert = jnp.minimum(jnp.sum(group_row >= end_padded[None, :], axis=1), N_EXPERTS - 1).astype(jnp.int32)
    n_valid = (end_padded[-1:] // MOE_ROWS).astype(jnp.int32)
    fill = jnp.arange(MOE_ROWS, dtype=jnp.int32)[None, :]
    pad_dest = jnp.where(fill < (padded - counts)[:, None], (start_padded + counts)[:, None] + fill,
                         n_rows - MOE_ROWS + fill).reshape(-1)
    xg = _scatter_rows(hf, jnp.concatenate([dest, pad_dest]), n_rows)
    y = _experts(xg, group_expert, n_valid, w1, w3, w2)
    return _combine(x1, _gather_rows(y, dest), gate.T)


def _row(v, pad=0):
    v = v.reshape(1, -1).astype(F32)
    return jnp.pad(v, ((0, 0), (0, pad))) if pad else v


def _rows_at(w, start, total):
    return jnp.pad(w.astype(F32), ((start, total - start - w.shape[0]), (0, 0)))


def kernel(x, w_in, norm_mix, rwkv_mu, rwkv_decay_up, rwkv_w0, rwkv_a_up, rwkv_a0, rwkv_gate_up, rwkv_k_k, rwkv_k_a, rwkv_r_k, rwkv_ln_w, rwkv_ln_b, vres_down, vres_up, vres_v0, ssm_conv_w, ssm_conv_b, ssm_dt_bias, ssm_a_log, ssm_d, ssm_norm_w, att_q_gain, att_k_gain, att_rel_bias, gla_gate_up, gla_gate_bias, gla_norm_w, w_branch, w_out, norm_ffn, ffn_w1, ffn_w3, ffn_w2, moe_router, moe_w1, moe_w3, moe_w2):
    bsz, seq, _ = x.shape
    n_tok = bsz * seq
    depth = w_in.shape[0]
    x2 = x.reshape(n_tok, D_MODEL)
    v_first = None
    o_rw, o_ss, o_at = RWKV_COLS, RWKV_COLS + SSM_COLS, RWKV_COLS + SSM_COLS + ATT_COLS
    o_gl = o_at + GLA_COLS
    for l in range(depth):
        w = w_in[l]
        zeros = lambda n: jnp.zeros((D_MODEL, n), BF16)
        cols = lambda lo, hi: w[:, lo:hi].astype(BF16)
        vd = vres_down[l - 1].astype(BF16) if l > 0 else zeros(32)
        w_mix = jnp.concatenate(
            [cols(0, o_rw), vd, zeros(RWKV_PAD - RWKV_COLS - 32),
             cols(o_rw, o_ss), zeros(SSM_PAD - SSM_COLS),
             cols(o_ss, o_at),
             cols(o_at, o_at + 1024), cols(o_at + 1040, o_gl), cols(o_at + 1024, o_at + 1040),
             zeros(GLA_PAD - GLA_COLS)], axis=1)
        wg = w[:, o_gl:]
        h, za, zb, zc, zd = _inproj(x2, _row(norm_mix[l]), w_mix)
        shp = lambda z: z.reshape(bsz, seq, z.shape[-1])

        rw = dict(mu=_row(rwkv_mu[l], RWKV_PAD - RWKV_COLS),
                  dup=_rows_at(rwkv_decay_up[l], 0, LANES), w0=_row(rwkv_w0[l]),
                  aup=_rows_at(rwkv_a_up[l], 64, LANES), a0=_row(rwkv_a0[l]),
                  gup=_rows_at(rwkv_gate_up[l], 0, 256).astype(BF16),
                  kk=_row(rwkv_k_k[l]), ka=_row(rwkv_k_a[l]), rk=_row(rwkv_r_k[l]),
                  lnw=_row(rwkv_ln_w[l]), lnb=_row(rwkv_ln_b[l]))
        if l > 0:
            rw.update(vup=_rows_at(vres_up[l - 1], 160, 256), v0=_row(vres_v0[l - 1]))
        o_a, v_first = _rwkv(shp(za), v_first, rw)
        o_b = _mamba(shp(zb), dict(cw=ssm_conv_w[l], cb=_row(ssm_conv_b[l]),
                                   dtb=_row(ssm_dt_bias[l], LANES - SSM_HEADS),
                                   alog=_row(ssm_a_log[l], LANES - SSM_HEADS),
                                   dexp=_row(jnp.repeat(ssm_d[l], CHUNK)), nw=_row(ssm_norm_w[l])))
        o_c = _attention(shp(zc), att_q_gain[l], att_k_gain[l], att_rel_bias)
        o_d = _gla(shp(zd), dict(gup=_rows_at(gla_gate_up[l], 0, LANES), gb=_row(gla_gate_bias[l]),
                                 nw=_row(gla_norm_w[l])))
        outs = [o.reshape(n_tok, BRANCH_DIM) for o in (o_a, o_b, o_c, o_d)]
        x1, hf = _merge(x2, h, outs, wg.astype(BF16), w_branch[l].astype(BF16), w_out[l].astype(BF16),
                        _row(norm_ffn[l]))
        if l % 2 == 0:
            x2 = _ffn(x1, hf, ffn_w1[l // 2].astype(BF16), ffn_w3[l // 2].astype(BF16),
                      ffn_w2[l // 2].astype(BF16))
        else:
            x2 = _moe(x1, _row(norm_ffn[l]), moe_router[l // 2], _to_bf16(moe_w1[l // 2]),
                      _to_bf16(moe_w3[l // 2]), _to_bf16(moe_w2[l // 2]))
    return x2.reshape(bsz, seq, D_MODEL)
```

```python
import functools
from typing import Any, NamedTuple

import jax
import jax.numpy as jnp
from jax import lax
from jax.experimental import pallas as pl
from jax.experimental.pallas import tpu as pltpu
from jax.experimental.pallas import tpu_sc as plsc

F32 = jnp.float32
BF16 = jnp.bfloat16
HIGHEST = lax.Precision.HIGHEST

D_MODEL = 1024
CHUNK = 64
BRANCH_DIM = 512
NORM_EPS = 1e-6
LANES = 128
VMEM_LIMIT = 56 * 1024 * 1024
EXPERT_VMEM_LIMIT = 60 * 1024 * 1024

LOG2_E = 1.4426950408889634

RWKV_LN_EPS = 64e-5
RWKV_UNIT_GROUP = 16
RWKV_COLS = 1824
RWKV_PAD = 1920
SSM_COLS = 1544
SSM_PAD = 1664
SSM_HEADS = 8
SSM_STATE = 128
ATT_COLS = 1536
ATT_HEADS = 8
ATT_LEFT = 8 * CHUNK
REL_CLIP = 2 * CHUNK
GLA_COLS = 1552
GLA_PAD = 1664
GLA_GATE_NORM = 16.0
FFN_DIM = 2816
N_EXPERTS = 8
EXPERT_DIM = 3584
MOE_ROWS = 256
ROW_SPLIT = 4
SUBROW = D_MODEL // ROW_SPLIT


def _dot(a, b):
    return jnp.dot(a.astype(BF16), b.astype(BF16), preferred_element_type=F32)


def _dot_nt(a, b):
    return lax.dot_general(a.astype(BF16), b.astype(BF16), (((1,), (1,)), ((), ())),
                           preferred_element_type=F32)


def _dot_tn(a, b):
    return lax.dot_general(a.astype(BF16), b.astype(BF16), (((0,), (0,)), ((), ())),
                           preferred_element_type=F32)


def _hi_lo(a):
    hi = a.astype(BF16)
    return hi, (a - hi.astype(F32)).astype(BF16)


def _dot_3x(a, b):
    a_hi, a_lo = _hi_lo(a)
    b_hi, b_lo = _hi_lo(b)
    return (jnp.dot(a_hi, b_hi, preferred_element_type=F32) + jnp.dot(a_lo, b_hi, preferred_element_type=F32)
            + jnp.dot(a_hi, b_lo, preferred_element_type=F32))


def _dot_split(a, m):
    m = m.astype(BF16)
    hi, lo = _hi_lo(a)
    return jnp.dot(hi, m, preferred_element_type=F32) + jnp.dot(lo, m, preferred_element_type=F32)


def _dot_split_l(m, a):
    m = m.astype(BF16)
    hi, lo = _hi_lo(a)
    return jnp.dot(m, hi, preferred_element_type=F32) + jnp.dot(m, lo, preferred_element_type=F32)


def _softplus(x):
    return jnp.maximum(x, 0.0) + jnp.log(1.0 + jnp.exp(-jnp.abs(x)))


def _sigmoid(x):
    return 1.0 / (1.0 + jnp.exp(-x))


def _silu(x):
    return x * _sigmoid(x)


def _iota2(shape, axis):
    return lax.broadcasted_iota(jnp.int32, shape, axis)


def _group_mean_matrix(n, group):
    r = _iota2((n, n), 0) // group
    c = _iota2((n, n), 1) // group
    return jnp.where(r == c, 1.0 / group, 0.0).astype(F32)


def _chunk_tri(n):
    r = _iota2((n, n), 0)
    c = _iota2((n, n), 1)
    return jnp.where((r // CHUNK == c // CHUNK) & (r >= c), 1.0, 0.0).astype(F32)


def _resident(shape):
    nd = len(shape)
    return pl.BlockSpec(shape, lambda *_: (0,) * nd, pipeline_mode=pl.Buffered(1))


def _params(*sem):
    return pltpu.CompilerParams(dimension_semantics=sem, vmem_limit_bytes=VMEM_LIMIT)


class _Part(NamedTuple):
    body: Any
    inputs: list
    in_specs: list
    out_shapes: list
    out_specs: list
    scratch_shapes: list


def _run_parts(parts, grid, name):
    def split(refs, counts):
        out, lo = [], 0
        for n in counts:
            out.append(refs[lo:lo + n])
            lo += n
        return out

    n_in = [len(p.inputs) for p in parts]
    n_out = [len(p.out_shapes) for p in parts]
    n_scr = [len(p.scratch_shapes) for p in parts]

    def kernel(*refs):
        ins = split(refs[:sum(n_in)], n_in)
        outs = split(refs[sum(n_in):sum(n_in) + sum(n_out)], n_out)
        scr = split(refs[sum(n_in) + sum(n_out):], n_scr)
        for part, i, o, c in zip(parts, ins, outs, scr):
            part.body(*i, *o, *c)

    flat = lambda field: [x for p in parts for x in getattr(p, field)]
    res = pl.pallas_call(
        kernel,
        grid=grid,
        in_specs=flat("in_specs"),
        out_specs=flat("out_specs"),
        out_shape=flat("out_shapes"),
        scratch_shapes=flat("scratch_shapes"),
        compiler_params=_params("parallel", "arbitrary"),
        name=name,
    )(*flat("inputs"))
    return split(list(res), n_out)


MIXER_COLS = (RWKV_PAD, SSM_PAD, ATT_COLS, GLA_PAD)
MIXER_BLOCK = 256


def _inproj_kernel(x_ref, g_ref, w_ref, h_ref, *z_refs):
    x = x_ref[...]
    ms = jnp.mean(x * x, axis=-1, keepdims=True)
    h = (x * lax.rsqrt(ms + NORM_EPS) * g_ref[...]).astype(BF16)
    h_ref[...] = h
    lo = 0
    for z_ref, n in zip(z_refs, MIXER_COLS):
        z_ref[...] = jnp.dot(h, w_ref[:, lo:lo + n], preferred_element_type=F32)
        lo += n


def _inproj(x2, g, w_mix, tm=512):
    n_tok = x2.shape[0]
    row = lambda n: pl.BlockSpec((tm, n), lambda i: (i, 0))
    return pl.pallas_call(
        _inproj_kernel,
        grid=(n_tok // tm,),
        in_specs=[row(D_MODEL), _resident((1, D_MODEL)), _resident(w_mix.shape)],
        out_specs=[row(D_MODEL)] + [row(n) for n in MIXER_COLS],
        out_shape=[jax.ShapeDtypeStruct((n_tok, D_MODEL), BF16)]
        + [jax.ShapeDtypeStruct((n_tok, n), F32) for n in MIXER_COLS],
        compiler_params=_params("parallel"),
        name="inproj",
    )(x2, g, w_mix)


def _rwkv_block(r, k, v, kk, ka, ld, tb, state_ref, y_ref):
    n_chunks = tb // CHUNK
    cum = _dot_split_l(_chunk_tri(tb), ld)
    p_inv = jnp.exp(-cum)
    a_t = -kk * jnp.exp(cum - ld)
    b_t = kk * ka * p_inv
    k_t = k * p_inv
    r_t = r * jnp.exp(cum)

    lane_head = _iota2((CHUNK, LANES), 1) // CHUNK
    row = _iota2((CHUNK, LANES), 0)
    col = _iota2((CHUNK, LANES), 1) % CHUNK
    strict = row > col
    incl = row >= col
    bd_mask = (_iota2((LANES, LANES), 0) // CHUNK) == (_iota2((LANES, LANES), 1) // CHUNK)
    eye = jnp.where(_iota2((LANES, LANES), 0) == _iota2((LANES, LANES), 1), 1.0, 0.0).astype(F32)

    def stack(z):
        return jnp.concatenate([jnp.where(lane_head == 0, z, 0.0), jnp.where(lane_head == 1, z, 0.0)], axis=0)

    def sub(z, u):
        c, p = u
        return z[c * CHUNK:(c + 1) * CHUNK, p * LANES:(p + 1) * LANES]

    bf = lambda z: z.astype(BF16)

    def prepare(units):
        gram = {u: _dot_nt(jnp.concatenate([sub(a_t, u), sub(r_t, u)], axis=0),
                           jnp.concatenate([stack(sub(b_t, u)), stack(sub(k_t, u))], axis=0))
                for u in units}
        a_rb = {u: bf(jnp.where(incl, gram[u][CHUNK:, :LANES], 0.0)) for u in units}
        a_akrk = {u: bf(jnp.concatenate([jnp.where(strict, gram[u][:CHUNK, LANES:], 0.0),
                                         jnp.where(incl, gram[u][CHUNK:, LANES:], 0.0)], axis=0)) for u in units}
        pw = {u: stack(jnp.where(strict, gram[u][:CHUNK, :LANES], 0.0)) for u in units}
        t_bd = {u: eye + pw[u] for u in units}
        pw = {u: _dot(pw[u], pw[u]) for u in units}
        for _ in range(CHUNK.bit_length() - 3):
            sp = {u: _dot(jnp.concatenate([t_bd[u], pw[u]], axis=0), pw[u]) for u in units}
            t_bd = {u: t_bd[u] + sp[u][:LANES] for u in units}
            pw = {u: sp[u][LANES:] for u in units}
        t_bd = {u: bf(t_bd[u] + _dot(t_bd[u], pw[u])) for u in units}
        v_bd = {u: bf(stack(sub(v, u))) for u in units}
        avyv = {u: _dot(a_akrk[u], v_bd[u]) for u in units}
        wuv = {u: _dot(t_bd[u], jnp.concatenate([stack(sub(a_t, u)), stack(avyv[u][:CHUNK])], axis=1))
               for u in units}
        wr = {u: bf(jnp.concatenate([wuv[u][:, :LANES], sub(r_t, u)], axis=0)) for u in units}
        return a_rb, avyv, wuv, wr

    a_rb, avyv, wuv, wr = {}, {}, {}, {}
    all_units = [(c, p) for c in range(n_chunks) for p in range(4)]
    for i in range(0, len(all_units), RWKV_UNIT_GROUP):
        for acc, part in zip((a_rb, avyv, wuv, wr), prepare(all_units[i:i + RWKV_UNIT_GROUP])):
            acc.update(part)

    h = [state_ref[p] for p in range(4)]
    for c in range(n_chunks):
        last = c * CHUNK + CHUNK - 1
        p_last = jnp.exp(cum[last:last + 1, :])
        hs = [_dot_nt(wr[(c, p)], h[p]) for p in range(4)]
        u2 = [hs[p][:LANES] + wuv[(c, p)][:, LANES:] for p in range(4)]
        u_p = [u2[p][:CHUNK] + u2[p][CHUNK:] for p in range(4)]
        upd = [_dot_tn(jnp.concatenate([u_p[p], sub(v, (c, p))], axis=0),
                       jnp.concatenate([sub(b_t, (c, p)), sub(k_t, (c, p))], axis=0)
                       * p_last[:, p * LANES:(p + 1) * LANES]) for p in range(4)]
        for p in range(4):
            y_ref[c * CHUNK:(c + 1) * CHUNK, p * LANES:(p + 1) * LANES] = (
                hs[p][LANES:] + _dot(a_rb[(c, p)], stack(u_p[p])) + avyv[(c, p)][CHUNK:])
        h = [h[p] * p_last[:, p * LANES:(p + 1) * LANES] + jnp.where(bd_mask, upd[p], 0.0) for p in range(4)]
    for p in range(4):
        state_ref[p] = h[p]


def _rwkv_kernel(has_vres, tb, *refs):
    if has_vres:
        (za_ref, vfirst_ref, mu_ref, dup_ref, w0_ref, aup_ref, a0_ref, gup_ref, vup_ref, v0_ref,
         kk_ref, ka_ref, rk_ref, lnw_ref, lnb_ref, o_ref, prev_ref, state_ref, y_ref) = refs
    else:
        (za_ref, mu_ref, dup_ref, w0_ref, aup_ref, a0_ref, gup_ref,
         kk_ref, ka_ref, rk_ref, lnw_ref, lnb_ref, o_ref, vraw_ref, prev_ref, state_ref, y_ref) = refs

    @pl.when(pl.program_id(1) == 0)
    def _():
        prev_ref[...] = jnp.zeros_like(prev_ref)
        state_ref[...] = jnp.zeros_like(state_ref)

    za = za_ref[...]
    shifted = pltpu.roll(za, 1, axis=0)
    shifted = jnp.where(_iota2(za.shape, 0) == 0, prev_ref[7:8, :], shifted)
    prev_ref[...] = za[tb - 8:, :]
    za = za + (shifted - za) * mu_ref[...]

    r = za[:, 0:512]
    k = za[:, 512:1024]
    v = za[:, 1024:1536]
    xwa = za[:, 1536:1664]
    xg = za[:, 1664:1920]
    w_log = -_softplus(-(w0_ref[...] + _dot_3x(jnp.tanh(xwa), dup_ref[...]))) - 0.5
    ld = -jnp.exp(w_log)
    a = _sigmoid(a0_ref[...] + _dot(xwa, aup_ref[...]))
    g = _dot(_sigmoid(xg), gup_ref[...])
    if has_vres:
        v_mix = _sigmoid(v0_ref[...] + _dot(xg, vup_ref[...]))
        v = v + (vfirst_ref[...] - v) * v_mix
    else:
        vraw_ref[...] = v
    kk = k * kk_ref[...]
    k = k * (1.0 + (a - 1.0) * ka_ref[...])
    head_sum = _group_mean_matrix(512, CHUNK) * float(CHUNK)
    kk = kk / jnp.maximum(jnp.sqrt(_dot(kk * kk, head_sum)), 1e-12)

    _rwkv_block(r, k, v, kk, a, ld, tb, state_ref, y_ref)
    y = y_ref[...]
    head_mean = _group_mean_matrix(512, CHUNK)
    mean = _dot_split(y, head_mean)
    yc = y - mean
    var = _dot(yc * yc, head_mean)
    y = yc * lax.rsqrt(var + RWKV_LN_EPS) * lnw_ref[...] + lnb_ref[...]
    y = y + _dot(r * k * rk_ref[...], head_sum) * v
    o_ref[...] = (y * g).astype(o_ref.dtype)


def _rwkv_part(za, v_first, prm, tb):
    bsz, seq, _ = za.shape
    has_vres = v_first is not None
    blk = lambda n: pl.BlockSpec((None, tb, n), lambda b, i: (b, i, 0))
    names = (["mu", "dup", "w0", "aup", "a0", "gup"] + (["vup", "v0"] if has_vres else [])
             + ["kk", "ka", "rk", "lnw", "lnb"])
    weights = [prm[n] for n in names]
    ins = [za] + ([v_first] if has_vres else []) + weights
    in_specs = [blk(RWKV_PAD)] + ([blk(512)] if has_vres else []) + [_resident(w.shape) for w in weights]
    out_shape = [jax.ShapeDtypeStruct((bsz, seq, 512), BF16)]
    out_specs = [blk(512)]
    if not has_vres:
        out_shape.append(jax.ShapeDtypeStruct((bsz, seq, 512), F32))
        out_specs.append(blk(512))
    return _Part(functools.partial(_rwkv_kernel, has_vres, tb), ins, in_specs, out_shape, out_specs,
                 [pltpu.VMEM((8, RWKV_PAD), F32), pltpu.VMEM((4, LANES, LANES), F32), pltpu.VMEM((tb, 512), F32)])


def _mamba_kernel(tb, zb_ref, cw_ref, cb_ref, dtb_ref, alog_ref, dexp_ref, nw_ref,
                  o_ref, xbuf_ref, state_ref, y_ref):
    @pl.when(pl.program_id(1) == 0)
    def _():
        xbuf_ref[0:8, :] = jnp.zeros((8, 1024), F32)
        state_ref[...] = jnp.zeros_like(state_ref)

    xbuf_ref[8:8 + tb, :] = zb_ref[:, 512:1536]
    conv = cb_ref[...]
    for i in range(4):
        conv = conv + cw_ref[i:i + 1, :] * xbuf_ref[5 + i:5 + i + tb, :]
    xbuf_ref[0:8, :] = xbuf_ref[tb:tb + 8, :]
    xbc = _silu(conv)
    xs = xbc[:, 0:512]
    bm = xbc[:, 512:768]
    cm = xbc[:, 768:1024]

    lane = _iota2((1, LANES), 1)
    dt = _softplus(zb_ref[:, 1536:1664] + dtb_ref[...])
    a_neg = jnp.where(lane < SSM_HEADS, -jnp.exp(alog_ref[...]), 0.0)
    acs = _dot_split_l(_chunk_tri(tb), dt * a_neg)
    expand = jnp.where(_iota2((LANES, 512), 0) == _iota2((LANES, 512), 1) // CHUNK, 1.0, 0.0)
    dt_e = _dot(dt, expand)
    acs_e = _dot_split(acs, expand)
    xdt = xs * dt_e

    n_chunks = tb // CHUNK
    causal = _iota2((CHUNK, 512), 0) >= _iota2((CHUNK, 512), 1) % CHUNK
    pair_head = _iota2((CHUNK, LANES), 1) // CHUNK
    spread = jnp.where(_iota2((CHUNK, 512), 0) == _iota2((CHUNK, 512), 1) % CHUNK, 1.0, 0.0)
    own_head = _iota2((SSM_HEADS, 512), 0) == _iota2((SSM_HEADS, 512), 1) // CHUNK
    rows_of = lambda c: slice(c * CHUNK, (c + 1) * CHUNK)
    grp = lambda g: slice(g * SSM_STATE, (g + 1) * SSM_STATE)

    def stack(z):
        return jnp.concatenate([jnp.where(pair_head == 0, z, 0.0), jnp.where(pair_head == 1, z, 0.0)], axis=0)

    y_intra, upd, e_in, e_last = [], [], [], []
    for c in range(n_chunks):
        rows = rows_of(c)
        acs_ec = acs_e[rows]
        acs_t = acs[rows].T[0:SSM_HEADS]
        acs_row = jnp.sum(jnp.where(own_head, _dot_split(acs_t, spread), 0.0), axis=0, keepdims=True)
        decay = jnp.exp(jnp.where(causal, acs_ec - acs_row, -jnp.inf))
        cb = jnp.concatenate(
            [_dot_nt(cm[rows, grp(g)], jnp.concatenate([bm[rows, grp(g)]] * 4, axis=0)) for g in range(2)], axis=1)
        m = cb * decay
        xdt_c = xdt[rows]
        y_intra.append([_dot(m[:, p * LANES:(p + 1) * LANES], stack(xdt_c[:, p * LANES:(p + 1) * LANES]))
                        for p in range(4)])
        last_e = acs_ec[CHUNK - 1:CHUNK, :]
        x_out = xdt_c * jnp.exp(last_e - acs_ec)
        upd.append([_dot_tn(bm[rows, grp(g)], x_out[:, g * 256:(g + 1) * 256]) for g in range(2)])
        e_in.append(jnp.exp(acs_ec))
        e_last.append(jnp.exp(last_e))

    state = [state_ref[g] for g in range(2)]
    for c in range(n_chunks):
        rows = rows_of(c)
        for g in range(2):
            ls = slice(g * 256, (g + 1) * 256)
            y_in = jnp.concatenate(y_intra[c][2 * g:2 * g + 2], axis=1)
            y_ref[rows, ls] = y_in + _dot(cm[rows, grp(g)], state[g]) * e_in[c][:, ls]
            state[g] = state[g] * e_last[c][:, ls] + upd[c][g]
    for g in range(2):
        state_ref[g] = state[g]

    y = (y_ref[...] + xs * dexp_ref[...]) * _silu(zb_ref[:, 0:512])
    for g in range(2):
        ls = slice(g * 256, (g + 1) * 256)
        yg = y[:, ls]
        ms = jnp.mean(yg * yg, axis=-1, keepdims=True)
        o_ref[:, ls] = (yg * lax.rsqrt(ms + NORM_EPS) * nw_ref[:, ls]).astype(o_ref.dtype)


def _mamba_part(zb, prm, tb):
    bsz, seq, _ = zb.shape
    blk = lambda n: pl.BlockSpec((None, tb, n), lambda b, i: (b, i, 0))
    weights = [prm[n] for n in ("cw", "cb", "dtb", "alog", "dexp", "nw")]
    return _Part(functools.partial(_mamba_kernel, tb), [zb] + weights,
                 [blk(SSM_PAD)] + [_resident(w.shape) for w in weights],
                 [jax.ShapeDtypeStruct((bsz, seq, 512), BF16)], [blk(512)],
                 [pltpu.VMEM((tb + 8, 1024), F32), pltpu.VMEM((2, SSM_STATE, 256), F32), pltpu.VMEM((tb, 512), F32)])


def _attn_kernel(tq, seq, q_ref, k_ref, v_ref, qg_ref, kg_ref, bias_ref, o_ref, kn_ref, vb_ref):
    i = pl.program_id(1)
    head_mean = _group_mean_matrix(512, CHUNK)
    win = tq + ATT_LEFT

    @pl.when(i == 0)
    def _():
        kn_ref[0:ATT_LEFT, :] = jnp.zeros((ATT_LEFT, 512), BF16)
        vb_ref[0:ATT_LEFT, :] = jnp.zeros((ATT_LEFT, 512), BF16)
        for j in range(seq // tq):
            kj = k_ref[j * tq:(j + 1) * tq, :]
            ms = _dot(kj * kj, head_mean)
            kn_ref[ATT_LEFT + j * tq:ATT_LEFT + (j + 1) * tq, :] = (
                kj * lax.rsqrt(ms + NORM_EPS) * kg_ref[...]).astype(BF16)
            vb_ref[ATT_LEFT + j * tq:ATT_LEFT + (j + 1) * tq, :] = v_ref[j * tq:(j + 1) * tq, :].astype(BF16)

    q = q_ref[...]
    ms = _dot(q * q, head_mean)
    qn = q * lax.rsqrt(ms + NORM_EPS) * qg_ref[...] * (CHUNK ** -0.5 * LOG2_E)
    start = pl.multiple_of(i * tq, tq)
    kwin = kn_ref[pl.ds(start, win), :]
    vwin = vb_ref[pl.ds(start, win), :]
    lane_head = _iota2((tq, LANES), 1) // CHUNK
    for p in range(4):
        ls = slice(p * LANES, (p + 1) * LANES)
        q2 = jnp.concatenate([jnp.where(lane_head == s, qn[:, ls], 0.0) for s in range(2)], axis=0)
        sc = _dot_nt(q2, kwin[:, ls]) + bias_ref[2 * p:2 * p + 2].reshape(2 * tq, win)
        e = jnp.exp2(sc - jnp.max(sc, axis=-1, keepdims=True))
        o2 = _dot(e, vwin[:, ls]) / jnp.sum(e, axis=-1, keepdims=True)
        o_ref[:, ls] = jnp.where(lane_head == 0, o2[:tq], o2[tq:]).astype(o_ref.dtype)


def _band_bias(rel_bias, tq, win):
    period = tq + win
    m = jnp.arange(period)
    d = jnp.where(m < win, m, m - period) - ATT_LEFT
    f = rel_bias.astype(F32)[:, jnp.clip(d, -REL_CLIP, REL_CLIP) + REL_CLIP]
    g = jnp.tile(f, (1, tq))[:, :tq * (period - 1)].reshape(-1, tq, period - 1)
    bias = g[:, :, :win] * LOG2_E
    left = ATT_LEFT // CHUNK
    qc = left + jnp.arange(tq)[:, None] // CHUNK
    kc = jnp.arange(win)[None, :] // CHUNK
    tiles = []
    for blk in range(ATT_LEFT // tq + 1):
        first_kc = jnp.maximum(qc - left, left - blk * (tq // CHUNK))
        tiles.append(jnp.where((kc <= qc) & (kc >= first_kc), bias, -jnp.inf))
    return jnp.stack(tiles)


def _attention_part(zc, q_gain, k_gain, rel_bias, tq):
    bsz, seq, _ = zc.shape
    win = tq + ATT_LEFT
    bias = _band_bias(rel_bias, tq, win)
    qg = jnp.tile(q_gain, ATT_HEADS)[None, :]
    kg = jnp.tile(k_gain, ATT_HEADS)[None, :]
    full = lambda col: pl.BlockSpec((None, seq, 512), lambda b, i: (b, 0, col), pipeline_mode=pl.Buffered(1))
    last_tile = bias.shape[0] - 1
    bias_spec = pl.BlockSpec((None,) + bias.shape[1:], lambda b, i: (jnp.minimum(i, last_tile), 0, 0, 0))
    blk = pl.BlockSpec((None, tq, 512), lambda b, i: (b, i, 0))
    return _Part(functools.partial(_attn_kernel, tq, seq), [zc, zc, zc, qg, kg, bias],
                 [blk, full(1), full(2), _resident(qg.shape), _resident(kg.shape), bias_spec],
                 [jax.ShapeDtypeStruct((bsz, seq, 512), BF16)], [blk],
                 [pltpu.VMEM((ATT_LEFT + seq, 512), BF16), pltpu.VMEM((ATT_LEFT + seq, 512), BF16)])


def _gla_kernel(tb, zd_ref, gup_ref, gb_ref, nw_ref, o_ref, state_ref, y_ref):
    @pl.when(pl.program_id(1) == 0)
    def _():
        state_ref[...] = jnp.zeros_like(state_ref)

    q = zd_ref[:, 0:256] * (CHUNK ** -0.5)
    k = zd_ref[:, 256:512]
    v = zd_ref[:, 512:1024]
    log_a = -_softplus(-(_dot_3x(zd_ref[:, 1536:1664], gup_ref[...]) + gb_ref[...])) / GLA_GATE_NORM
    bcum = _dot_split_l(_chunk_tri(tb), log_a)

    k_head = _iota2((CHUNK, 256), 1) // CHUNK
    v_head = _iota2((CHUNK, 512), 1) // LANES
    causal = _iota2((CHUNK, 256), 0) >= _iota2((CHUNK, 256), 1) % CHUNK
    bd = (_iota2((512, 256), 0) // LANES) == (_iota2((512, 256), 1) // CHUNK)
    n_chunks = tb // CHUNK
    qg_all = (q * jnp.exp(bcum)).astype(BF16)
    kg_all = k * jnp.exp(-bcum)
    qg, att, o_intra, upd, e_last = [], [], [], [], []
    for c in range(n_chunks):
        rows = slice(c * CHUNK, (c + 1) * CHUNK)
        kg_bd = jnp.concatenate([jnp.where(k_head == h, kg_all[rows], 0.0) for h in range(4)], axis=0)
        qg.append(qg_all[rows])
        att.append(jnp.where(causal, _dot_nt(qg[c], kg_bd), 0.0))
    for c in range(n_chunks):
        rows = slice(c * CHUNK, (c + 1) * CHUNK)
        bc = bcum[rows]
        blast = bc[CHUNK - 1:CHUNK, :]
        v_c = v[rows]
        v_bd = jnp.concatenate([jnp.where(v_head == h, v_c, 0.0) for h in range(4)], axis=0)
        o_intra.append(_dot(att[c], v_bd))
        upd.append(jnp.where(bd, _dot_tn(v_c, k[rows] * jnp.exp(blast - bc)), 0.0))
        e_last.append(jnp.exp(blast))
    st = state_ref[...]
    for c in range(n_chunks):
        y_ref[c * CHUNK:(c + 1) * CHUNK, :] = o_intra[c] + _dot_nt(qg[c], st)
        st = st * e_last[c] + upd[c]
    state_ref[...] = st

    o = y_ref[...]
    for h in range(4):
        ls = slice(h * LANES, (h + 1) * LANES)
        oh = o[:, ls]
        ms = jnp.mean(oh * oh, axis=-1, keepdims=True)
        o_ref[:, ls] = (oh * lax.rsqrt(ms + NORM_EPS) * nw_ref[...]
                        * _silu(zd_ref[:, 1024 + h * LANES:1024 + (h + 1) * LANES])).astype(o_ref.dtype)


def _gla_part(zd, prm, tb):
    bsz, seq, _ = zd.shape
    blk = lambda n: pl.BlockSpec((None, tb, n), lambda b, i: (b, i, 0))
    weights = [prm[n] for n in ("gup", "gb", "nw")]
    return _Part(functools.partial(_gla_kernel, tb), [zd] + weights,
                 [blk(GLA_PAD)] + [_resident(w.shape) for w in weights],
                 [jax.ShapeDtypeStruct((bsz, seq, 512), BF16)], [blk(512)],
                 [pltpu.VMEM((512, 256), F32), pltpu.VMEM((tb, 512), F32)])


def _merge_kernel(x_ref, h_ref, oa_ref, ob_ref, oc_ref, od_ref, wg_ref, wb_ref, wo_ref, nf_ref,
                  x1_ref, hf_ref):
    h = h_ref[...]
    acc = jnp.zeros(x_ref.shape, F32)
    for i, o_ref in enumerate((oa_ref, ob_ref, oc_ref, od_ref)):
        gate = _sigmoid(jnp.dot(h, wg_ref[:, i * D_MODEL:(i + 1) * D_MODEL], preferred_element_type=F32))
        acc = acc + gate * jnp.dot(o_ref[...], wb_ref[i], preferred_element_type=F32)
    x1 = x_ref[...] + jnp.dot(acc.astype(BF16), wo_ref[...], preferred_element_type=F32)
    x1_ref[...] = x1
    ms = jnp.mean(x1 * x1, axis=-1, keepdims=True)
    hf_ref[...] = (x1 * lax.rsqrt(ms + NORM_EPS) * nf_ref[...]).astype(BF16)


def _merge(x2, h, outs, wg, wb, wo, nf, tm=512):
    n_tok = x2.shape[0]
    row = lambda n: pl.BlockSpec((tm, n), lambda i: (i, 0))
    return pl.pallas_call(
        _merge_kernel,
        grid=(n_tok // tm,),
        in_specs=[row(D_MODEL), row(D_MODEL)] + [row(BRANCH_DIM)] * 4
        + [_resident(wg.shape), _resident(wb.shape), _resident(wo.shape), _resident(nf.shape)],
        out_specs=[row(D_MODEL), row(D_MODEL)],
        out_shape=[jax.ShapeDtypeStruct((n_tok, D_MODEL), F32), jax.ShapeDtypeStruct((n_tok, D_MODEL), BF16)],
        compiler_params=_params("parallel"),
        name="merge",
    )(x2, h, *outs, wg, wb, wo, nf)


def _ffn_kernel(tf, x1_ref, hf_ref, w1_ref, w3_ref, w2_ref, o_ref):
    hf = hf_ref[...]
    acc = x1_ref[...]
    for lo in range(0, FFN_DIM, tf):
        cols = slice(lo, min(lo + tf, FFN_DIM))
        a = jnp.dot(hf, w1_ref[:, cols], preferred_element_type=F32)
        b = jnp.dot(hf, w3_ref[:, cols], preferred_element_type=F32)
        acc = acc + jnp.dot((_silu(a) * b).astype(BF16), w2_ref[cols, :], preferred_element_type=F32)
    o_ref[...] = acc


def _ffn(x1, hf, w1, w3, w2, tm=512, tf=512):
    n_tok = x1.shape[0]
    row = pl.BlockSpec((tm, D_MODEL), lambda i: (i, 0))
    return pl.pallas_call(
        functools.partial(_ffn_kernel, tf),
        grid=(n_tok // tm,),
        in_specs=[row, row, _resident(w1.shape), _resident(w3.shape), _resident(w2.shape)],
        out_specs=row,
        out_shape=jax.ShapeDtypeStruct((n_tok, D_MODEL), F32),
        compiler_params=_params("parallel"),
        name="ffn",
    )(x1, hf, w1, w3, w2)


def _router_kernel(tm, x1_ref, nf_ref, wr_ref, idx_ref, gate_ref, rank_ref, cnt_ref, hf_ref, carry_ref):
    @pl.when(pl.program_id(0) == 0)
    def _():
        carry_ref[...] = jnp.zeros_like(carry_ref)

    x1 = x1_ref[...]
    ms = jnp.mean(x1 * x1, axis=-1, keepdims=True)
    hf = x1 * lax.rsqrt(ms + NORM_EPS) * nf_ref[...]
    for j in range(ROW_SPLIT):
        hf_ref[j] = hf[:, j * SUBROW:(j + 1) * SUBROW]
    logits = lax.dot_general(wr_ref[...], hf, (((1,), (1,)), ((), ())), precision=HIGHEST,
                             preferred_element_type=F32)
    e_iota = _iota2((N_EXPERTS, tm), 0)
    m1 = jnp.max(logits, axis=0, keepdims=True)
    i1 = jnp.min(jnp.where(logits == m1, e_iota, N_EXPERTS), axis=0, keepdims=True)
    rest = jnp.where(e_iota == i1, -jnp.inf, logits)
    m2 = jnp.max(rest, axis=0, keepdims=True)
    i2 = jnp.min(jnp.where(rest == m2, e_iota, N_EXPERTS), axis=0, keepdims=True)
    e2 = jnp.exp(m2 - m1)
    gate_ref[0:1, :] = 1.0 / (1.0 + e2)
    gate_ref[1:2, :] = e2 / (1.0 + e2)
    idx_ref[0:1, :] = i1
    idx_ref[1:2, :] = i2
    hit1 = jnp.where(e_iota == i1, 1.0, 0.0)
    hit2 = jnp.where(e_iota == i2, 1.0, 0.0)
    before = jnp.where(_iota2((tm, tm), 0) < _iota2((tm, tm), 1), 1.0, 0.0)
    prior = _dot(hit1 + hit2, before) + carry_ref[:, 0:1]
    rank_ref[0:1, :] = jnp.sum(hit1 * prior, axis=0, keepdims=True).astype(jnp.int32)
    rank_ref[1:2, :] = jnp.sum(hit2 * prior, axis=0, keepdims=True).astype(jnp.int32)
    carry_ref[...] = carry_ref[...] + jnp.sum(hit1 + hit2, axis=1, keepdims=True)
    cnt_ref[...] = carry_ref[...]


def _router(x1, nf, wr_t, tm=512):
    n_tok = x1.shape[0]
    col = pl.BlockSpec((2, tm), lambda i: (0, i))
    return pl.pallas_call(
        functools.partial(_router_kernel, tm),
        grid=(n_tok // tm,),
        in_specs=[pl.BlockSpec((tm, D_MODEL), lambda i: (i, 0)), _resident(nf.shape), _resident(wr_t.shape)],
        out_specs=[col, col, col, pl.BlockSpec((N_EXPERTS, LANES), lambda i: (0, 0)),
                   pl.BlockSpec((ROW_SPLIT, tm, SUBROW), lambda i: (0, i, 0))],
        out_shape=[jax.ShapeDtypeStruct((2, n_tok), jnp.int32), jax.ShapeDtypeStruct((2, n_tok), F32),
                   jax.ShapeDtypeStruct((2, n_tok), jnp.int32), jax.ShapeDtypeStruct((N_EXPERTS, LANES), F32),
                   jax.ShapeDtypeStruct((ROW_SPLIT, n_tok, SUBROW), F32)],
        scratch_shapes=[pltpu.VMEM((N_EXPERTS, LANES), F32)],
        compiler_params=_params("arbitrary"),
        name="moe_router",
    )(x1, nf, wr_t)


def _gather_rows(table, idx, window=128):
    split, n_table, width = table.shape
    n = idx.shape[0]
    flat_idx = (idx[None, :] + n_table * jnp.arange(split, dtype=jnp.int32)[:, None]).reshape(-1)
    return _gather_subrows(table.reshape(split * n_table, width), flat_idx, window).reshape(split, n, width)


def _gather_subrows(table, idx, window):
    n = idx.shape[0]
    d = table.shape[1]
    mesh = plsc.VectorSubcoreMesh(core_axis_name="core", subcore_axis_name="subcore")

    @functools.partial(pl.kernel, out_type=jax.ShapeDtypeStruct((n, d), table.dtype), mesh=mesh,
                       name="gather_rows")
    def gather(table_hbm, idx_hbm, out_hbm):
        def body(idx_vmem, out_vmem):
            pltpu.sync_copy(table_hbm.at[idx_vmem.at[0]], out_vmem)

        pltpu.emit_pipeline(
            body,
            grid=(n // window,),
            in_specs=[pl.BlockSpec((1, window), index_map=lambda i: (0, i))],
            out_specs=[pl.BlockSpec((window, d), index_map=lambda i: (i, 0))],
            core_axis_name=("core", "subcore"),
            dimension_semantics=(pltpu.PARALLEL,),
        )(idx_hbm, out_hbm)

    return gather(table, idx.reshape(1, n))


def _scatter_rows(table, dest, n_out, window=128):
    split, n_table, width = table.shape
    n = dest.shape[0]
    flat_dest = (dest[None, :] + n_out * jnp.arange(split, dtype=jnp.int32)[:, None]).reshape(1, split * n)
    src_blocks, blocks = n_table // window, n // window
    mesh = plsc.VectorSubcoreMesh(core_axis_name="core", subcore_axis_name="subcore")

    @functools.partial(pl.kernel, out_type=jax.ShapeDtypeStruct((split * n_out, width), table.dtype), mesh=mesh,
                       name="scatter_rows")
    def scatter(table_hbm, dest_hbm, out_hbm):
        def body(rows_vmem, dest_vmem):
            pltpu.sync_copy(rows_vmem, out_hbm.at[dest_vmem.at[0]])

        pltpu.emit_pipeline(
            body,
            grid=(split * blocks,),
            in_specs=[pl.BlockSpec((window, width),
                                   index_map=lambda i: ((i // blocks) * src_blocks + (i % blocks) % src_blocks, 0)),
                      pl.BlockSpec((1, window), index_map=lambda i: (0, i))],
            out_specs=[],
            core_axis_name=("core", "subcore"),
            dimension_semantics=(pltpu.PARALLEL,),
        )(table_hbm, dest_hbm)

    return scatter(table.reshape(split * n_table, width), flat_dest).reshape(split, n_out, width)


def _expert_kernel(tf, ge_ref, nv_ref, x_ref, w1_ref, w3_ref, w2_ref, y_ref):
    g = pl.program_id(0)

    @pl.when(g < nv_ref[0])
    def _():
        x = jnp.concatenate([x_ref[j] for j in range(ROW_SPLIT)], axis=-1).astype(BF16)
        acc = jnp.zeros((MOE_ROWS, D_MODEL), F32)
        for lo in range(0, EXPERT_DIM, tf):
            a = jnp.dot(x, w1_ref[:, lo:lo + tf], preferred_element_type=F32)
            b = jnp.dot(x, w3_ref[:, lo:lo + tf], preferred_element_type=F32)
            acc = acc + jnp.dot((_silu(a) * b).astype(BF16), w2_ref[lo:lo + tf, :], preferred_element_type=F32)
        for j in range(ROW_SPLIT):
            y_ref[j] = acc[:, j * SUBROW:(j + 1) * SUBROW]

    @pl.when(g >= nv_ref[0])
    def _():
        y_ref[...] = jnp.zeros_like(y_ref)


def _experts(xg, group_expert, n_valid, w1, w3, w2, tf=512):
    n_rows = xg.shape[1]
    rows = pl.BlockSpec((ROW_SPLIT, MOE_ROWS, SUBROW), lambda g, ge, nv: (0, g, 0))
    return pl.pallas_call(
        functools.partial(_expert_kernel, tf),
        grid_spec=pltpu.PrefetchScalarGridSpec(
            num_scalar_prefetch=2,
            grid=(n_rows // MOE_ROWS,),
            in_specs=[rows,
                      pl.BlockSpec((None, D_MODEL, EXPERT_DIM), lambda g, ge, nv: (ge[g], 0, 0)),
                      pl.BlockSpec((None, D_MODEL, EXPERT_DIM), lambda g, ge, nv: (ge[g], 0, 0)),
                      pl.BlockSpec((None, EXPERT_DIM, D_MODEL), lambda g, ge, nv: (ge[g], 0, 0))],
            out_specs=rows,
        ),
        out_shape=jax.ShapeDtypeStruct((ROW_SPLIT, n_rows, SUBROW), F32),
        compiler_params=pltpu.CompilerParams(dimension_semantics=("arbitrary",),
                                             vmem_limit_bytes=EXPERT_VMEM_LIMIT),
        name="experts",
    )(group_expert, n_valid, xg, w1, w3, w2)


def _cast_kernel(w_ref, o_ref):
    o_ref[...] = w_ref[...].astype(o_ref.dtype)


def _to_bf16(w):
    n, rows, cols = w.shape
    blk = pl.BlockSpec((None, rows // 2, cols), lambda i, j: (i, j, 0))
    return pl.pallas_call(
        _cast_kernel,
        grid=(n, 2),
        in_specs=[blk],
        out_specs=blk,
        out_shape=jax.ShapeDtypeStruct(w.shape, BF16),
        compiler_params=_params("parallel", "parallel"),
        name="cast_bf16",
    )(w)


def _combine_kernel(x1_ref, y0_ref, y1_ref, gate_ref, o_ref):
    gate = gate_ref[...]
    for j in range(ROW_SPLIT):
        cols = slice(j * SUBROW, (j + 1) * SUBROW)
        o_ref[:, cols] = x1_ref[:, cols] + gate[:, 0:1] * y0_ref[j] + gate[:, 1:2] * y1_ref[j]


def _combine(x1, yg, gate_t, tm=512):
    n_tok = x1.shape[0]
    nb = n_tok // tm
    return pl.pallas_call(
        _combine_kernel,
        grid=(nb,),
        in_specs=[pl.BlockSpec((tm, D_MODEL), lambda i: (i, 0)),
                  pl.BlockSpec((ROW_SPLIT, tm, SUBROW), lambda i: (0, i, 0)),
                  pl.BlockSpec((ROW_SPLIT, tm, SUBROW), lambda i: (0, i + nb, 0)),
                  pl.BlockSpec((tm, 2), lambda i: (i, 0))],
        out_specs=pl.BlockSpec((tm, D_MODEL), lambda i: (i, 0)),
        out_shape=jax.ShapeDtypeStruct((n_tok, D_MODEL), F32),
        compiler_params=_params("parallel"),
        name="moe_combine",
    )(x1, yg, yg, gate_t)


def _moe(x1, nf, w_router, w1, w3, w2):
    n_tok = x1.shape[0]
    idx, gate, rank, cnt, hf = _router(x1, nf, w_router.T)
    counts = cnt[:, 0].astype(jnp.int32)
    padded = (counts + MOE_ROWS - 1) // MOE_ROWS * MOE_ROWS
    end_padded = jnp.cumsum(padded)
    start_padded = end_padded - padded
    start = sum(jnp.where(idx == e, start_padded[e], 0) for e in range(N_EXPERTS))
    dest = (start + rank).reshape(-1)
    n_groups = (n_tok * 2 + MOE_ROWS - 1) // MOE_ROWS + N_EXPERTS
    n_rows = n_groups * MOE_ROWS
    group_row = jnp.arange(n_groups, dtype=jnp.int32)[:, None] * MOE_ROWS
    group_expert = jnp.minimum(jnp.sum(group_row >= end_padded[None, :], axis=1), N_EXPERTS - 1).astype(jnp.int32)
    n_valid = (end_padded[-1:] // MOE_ROWS).astype(jnp.int32)
    fill = jnp.arange(MOE_ROWS, dtype=jnp.int32)[None, :]
    pad_dest = jnp.where(fill < (padded - counts)[:, None], (start_padded + counts)[:, None] + fill,
                         n_rows - MOE_ROWS + fill).reshape(-1)
    xg = _scatter_rows(hf, jnp.concatenate([dest, pad_dest]), n_rows)
    y = _experts(xg, group_expert, n_valid, w1, w3, w2)
    return _combine(x1, _gather_rows(y, dest), gate.T)


def _row(v, pad=0):
    v = v.reshape(1, -1).astype(F32)
    return jnp.pad(v, ((0, 0), (0, pad))) if pad else v


def _rows_at(w, start, total):
    return jnp.pad(w.astype(F32), ((start, total - start - w.shape[0]), (0, 0)))


def kernel(x, w_in, norm_mix, rwkv_mu, rwkv_decay_up, rwkv_w0, rwkv_a_up, rwkv_a0, rwkv_gate_up, rwkv_k_k, rwkv_k_a, rwkv_r_k, rwkv_ln_w, rwkv_ln_b, vres_down, vres_up, vres_v0, ssm_conv_w, ssm_conv_b, ssm_dt_bias, ssm_a_log, ssm_d, ssm_norm_w, att_q_gain, att_k_gain, att_rel_bias, gla_gate_up, gla_gate_bias, gla_norm_w, w_branch, w_out, norm_ffn, ffn_w1, ffn_w3, ffn_w2, moe_router, moe_w1, moe_w3, moe_w2):
    bsz, seq, _ = x.shape
    n_tok = bsz * seq
    depth = w_in.shape[0]
    x2 = x.reshape(n_tok, D_MODEL)
    v_first = None
    o_rw, o_ss, o_at = RWKV_COLS, RWKV_COLS + SSM_COLS, RWKV_COLS + SSM_COLS + ATT_COLS
    o_gl = o_at + GLA_COLS
    for l in range(depth):
        w = w_in[l]
        zeros = lambda n: jnp.zeros((D_MODEL, n), BF16)
        cols = lambda lo, hi: w[:, lo:hi].astype(BF16)
        vd = vres_down[l - 1].astype(BF16) if l > 0 else zeros(32)
        w_mix = jnp.concatenate(
            [cols(0, o_rw), vd, zeros(RWKV_PAD - RWKV_COLS - 32),
             cols(o_rw, o_ss), zeros(SSM_PAD - SSM_COLS),
             cols(o_ss, o_at),
             cols(o_at, o_at + 1024), cols(o_at + 1040, o_gl), cols(o_at + 1024, o_at + 1040),
             zeros(GLA_PAD - GLA_COLS)], axis=1)
        wg = w[:, o_gl:]
        h, za, zb, zc, zd = _inproj(x2, _row(norm_mix[l]), w_mix)
        shp = lambda z: z.reshape(bsz, seq, z.shape[-1])

        rw = dict(mu=_row(rwkv_mu[l], RWKV_PAD - RWKV_COLS),
                  dup=_rows_at(rwkv_decay_up[l], 0, LANES), w0=_row(rwkv_w0[l]),
                  aup=_rows_at(rwkv_a_up[l], 64, LANES), a0=_row(rwkv_a0[l]),
                  gup=_rows_at(rwkv_gate_up[l], 0, 256).astype(BF16),
                  kk=_row(rwkv_k_k[l]), ka=_row(rwkv_k_a[l]), rk=_row(rwkv_r_k[l]),
                  lnw=_row(rwkv_ln_w[l]), lnb=_row(rwkv_ln_b[l]))
        if l > 0:
            rw.update(vup=_rows_at(vres_up[l - 1], 160, 256), v0=_row(vres_v0[l - 1]))
        ssm = dict(cw=ssm_conv_w[l], cb=_row(ssm_conv_b[l]), dtb=_row(ssm_dt_bias[l], LANES - SSM_HEADS),
                   alog=_row(ssm_a_log[l], LANES - SSM_HEADS), dexp=_row(jnp.repeat(ssm_d[l], CHUNK)),
                   nw=_row(ssm_norm_w[l]))
        gla = dict(gup=_rows_at(gla_gate_up[l], 0, LANES), gb=_row(gla_gate_bias[l]), nw=_row(gla_norm_w[l]))
        (o_c,), (o_d,), (o_b,), res_a = _run_parts(
            [_attention_part(shp(zc), att_q_gain[l], att_k_gain[l], att_rel_bias, MIXER_BLOCK),
             _gla_part(shp(zd), gla, MIXER_BLOCK), _mamba_part(shp(zb), ssm, MIXER_BLOCK),
             _rwkv_part(shp(za), v_first, rw, MIXER_BLOCK)],
            grid=(bsz, seq // MIXER_BLOCK), name="mixers")
        o_a = res_a[0]
        if l == 0:
            v_first = res_a[1]
        outs = [o.reshape(n_tok, BRANCH_DIM) for o in (o_a, o_b, o_c, o_d)]
        x1, hf = _merge(x2, h, outs, wg.astype(BF16), w_branch[l].astype(BF16), w_out[l].astype(BF16),
                        _row(norm_ffn[l]))
        if l % 2 == 0:
            x2 = _ffn(x1, hf, ffn_w1[l // 2].astype(BF16), ffn_w3[l // 2].astype(BF16),
                      ffn_w2[l // 2].astype(BF16))
        else:
            x2 = _moe(x1, _row(norm_ffn[l]), moe_router[l // 2], _to_bf16(moe_w1[l // 2]),
                      _to_bf16(moe_w3[l // 2]), _to_bf16(moe_w2[l // 2]))
    return x2.reshape(bsz, seq, D_MODEL)
```

```python
import functools
from typing import Any, NamedTuple

import jax
import jax.numpy as jnp
from jax import lax
from jax.experimental import pallas as pl
from jax.experimental.pallas import tpu as pltpu
from jax.experimental.pallas import tpu_sc as plsc

F32 = jnp.float32
BF16 = jnp.bfloat16
HIGHEST = lax.Precision.HIGHEST

D_MODEL = 1024
CHUNK = 64
BRANCH_DIM = 512
NORM_EPS = 1e-6
LANES = 128
VMEM_LIMIT = 56 * 1024 * 1024
BIG_VMEM_LIMIT = 61 * 1024 * 1024

LOG2_E = 1.4426950408889634

RWKV_LN_EPS = 64e-5
RWKV_COLS = 1824
RWKV_PAD = 1920
SSM_COLS = 1544
SSM_PAD = 1664
SSM_HEADS = 8
SSM_STATE = 128
ATT_COLS = 1536
ATT_HEADS = 8
ATT_LEFT = 8 * CHUNK
REL_CLIP = 2 * CHUNK
GLA_COLS = 1552
GLA_PAD = 1664
GLA_GATE_NORM = 16.0
FFN_DIM = 2816
N_EXPERTS = 8
EXPERT_DIM = 3584
MOE_ROWS = 256
ROW_SPLIT = 4
SUBROW = D_MODEL // ROW_SPLIT


def _dot(a, b):
    return jnp.dot(a.astype(BF16), b.astype(BF16), preferred_element_type=F32)


def _dot_nt(a, b):
    return lax.dot_general(a.astype(BF16), b.astype(BF16), (((1,), (1,)), ((), ())),
                           preferred_element_type=F32)


def _dot_tn(a, b):
    return lax.dot_general(a.astype(BF16), b.astype(BF16), (((0,), (0,)), ((), ())),
                           preferred_element_type=F32)


def _hi_lo(a):
    hi = a.astype(BF16)
    return hi, (a - hi.astype(F32)).astype(BF16)


def _dot_3x(a, b):
    a_hi, a_lo = _hi_lo(a)
    b_hi, b_lo = _hi_lo(b)
    return (jnp.dot(a_hi, b_hi, preferred_element_type=F32) + jnp.dot(a_lo, b_hi, preferred_element_type=F32)
            + jnp.dot(a_hi, b_lo, preferred_element_type=F32))


def _dot_split(a, m):
    m = m.astype(BF16)
    hi, lo = _hi_lo(a)
    return jnp.dot(hi, m, preferred_element_type=F32) + jnp.dot(lo, m, preferred_element_type=F32)


def _dot_split_l(m, a):
    m = m.astype(BF16)
    hi, lo = _hi_lo(a)
    return jnp.dot(m, hi, preferred_element_type=F32) + jnp.dot(m, lo, preferred_element_type=F32)


def _softplus(x):
    return jnp.maximum(x, 0.0) + jnp.log(1.0 + jnp.exp(-jnp.abs(x)))


def _sigmoid(x):
    return 1.0 / (1.0 + jnp.exp(-x))


def _silu(x):
    return x * _sigmoid(x)


def _iota2(shape, axis):
    return lax.broadcasted_iota(jnp.int32, shape, axis)


def _group_mean_matrix(n, group):
    r = _iota2((n, n), 0) // group
    c = _iota2((n, n), 1) // group
    return jnp.where(r == c, 1.0 / group, 0.0).astype(F32)


def _chunk_tri(n):
    r = _iota2((n, n), 0)
    c = _iota2((n, n), 1)
    return jnp.where((r // CHUNK == c // CHUNK) & (r >= c), 1.0, 0.0).astype(F32)


def _resident(shape):
    nd = len(shape)
    return pl.BlockSpec(shape, lambda *_: (0,) * nd, pipeline_mode=pl.Buffered(1))


def _params(*sem, vmem_limit=VMEM_LIMIT):
    return pltpu.CompilerParams(dimension_semantics=sem, vmem_limit_bytes=vmem_limit)


_DONE = object()


class _Part(NamedTuple):
    body: Any
    inputs: list
    in_specs: list
    out_shapes: list
    out_specs: list
    scratch_shapes: list


def _run_parts(parts, grid, name):
    def split(refs, counts):
        out, lo = [], 0
        for n in counts:
            out.append(refs[lo:lo + n])
            lo += n
        return out

    n_in = [len(p.inputs) for p in parts]
    n_out = [len(p.out_shapes) for p in parts]
    n_scr = [len(p.scratch_shapes) for p in parts]

    def kernel(*refs):
        ins = split(refs[:sum(n_in)], n_in)
        outs = split(refs[sum(n_in):sum(n_in) + sum(n_out)], n_out)
        scr = split(refs[sum(n_in) + sum(n_out):], n_scr)
        stages = [part.body(*i, *o, *c) for part, i, o, c in zip(parts, ins, outs, scr)]
        while stages:
            for stage in list(stages):
                if next(stage, _DONE) is _DONE:
                    stages.remove(stage)

    flat = lambda field: [x for p in parts for x in getattr(p, field)]
    res = pl.pallas_call(
        kernel,
        grid=grid,
        in_specs=flat("in_specs"),
        out_specs=flat("out_specs"),
        out_shape=flat("out_shapes"),
        scratch_shapes=flat("scratch_shapes"),
        compiler_params=_params("parallel", "arbitrary", vmem_limit=BIG_VMEM_LIMIT),
        name=name,
    )(*flat("inputs"))
    return split(list(res), n_out)


MIXER_COLS = (RWKV_PAD, SSM_PAD, ATT_COLS, GLA_PAD)
MIXER_BLOCK = 256


def _inproj_kernel(x_ref, g_ref, w_ref, h_ref, *z_refs):
    x = x_ref[...]
    ms = jnp.mean(x * x, axis=-1, keepdims=True)
    h = (x * lax.rsqrt(ms + NORM_EPS) * g_ref[...]).astype(BF16)
    h_ref[...] = h
    lo = 0
    for z_ref, n in zip(z_refs, MIXER_COLS):
        z_ref[...] = jnp.dot(h, w_ref[:, lo:lo + n], preferred_element_type=F32)
        lo += n


def _inproj(x2, g, w_mix, tm=512):
    n_tok = x2.shape[0]
    row = lambda n: pl.BlockSpec((tm, n), lambda i: (i, 0))
    return pl.pallas_call(
        _inproj_kernel,
        grid=(n_tok // tm,),
        in_specs=[row(D_MODEL), _resident((1, D_MODEL)), _resident(w_mix.shape)],
        out_specs=[row(D_MODEL)] + [row(n) for n in MIXER_COLS],
        out_shape=[jax.ShapeDtypeStruct((n_tok, D_MODEL), BF16)]
        + [jax.ShapeDtypeStruct((n_tok, n), F32) for n in MIXER_COLS],
        compiler_params=_params("parallel"),
        name="inproj",
    )(x2, g, w_mix)


def _rwkv_block(r, k, v, kk, ka, ld, tb, state_ref, y_ref):
    n_chunks = tb // CHUNK
    cum = _dot_split_l(_chunk_tri(tb), ld)
    p_inv = jnp.exp(-cum)
    a_t = -kk * jnp.exp(cum - ld)
    b_t = kk * ka * p_inv
    k_t = k * p_inv
    r_t = r * jnp.exp(cum)
    yield

    lane_head = _iota2((CHUNK, LANES), 1) // CHUNK
    row = _iota2((CHUNK, LANES), 0)
    col = _iota2((CHUNK, LANES), 1) % CHUNK
    strict = row > col
    incl = row >= col
    bd_mask = (_iota2((LANES, LANES), 0) // CHUNK) == (_iota2((LANES, LANES), 1) // CHUNK)
    eye = jnp.where(_iota2((LANES, LANES), 0) == _iota2((LANES, LANES), 1), 1.0, 0.0).astype(F32)

    def stack(z):
        return jnp.concatenate([jnp.where(lane_head == 0, z, 0.0), jnp.where(lane_head == 1, z, 0.0)], axis=0)

    def sub(z, u):
        c, p = u
        return z[c * CHUNK:(c + 1) * CHUNK, p * LANES:(p + 1) * LANES]

    bf = lambda z: z.astype(BF16)

    def prepare(units):
        gram = {u: _dot_nt(jnp.concatenate([sub(a_t, u), sub(r_t, u)], axis=0),
                           jnp.concatenate([stack(sub(b_t, u)), stack(sub(k_t, u))], axis=0))
                for u in units}
        yield
        a_rb = {u: bf(jnp.where(incl, gram[u][CHUNK:, :LANES], 0.0)) for u in units}
        a_akrk = {u: bf(jnp.concatenate([jnp.where(strict, gram[u][:CHUNK, LANES:], 0.0),
                                         jnp.where(incl, gram[u][CHUNK:, LANES:], 0.0)], axis=0)) for u in units}
        pw = {u: stack(jnp.where(strict, gram[u][:CHUNK, :LANES], 0.0)) for u in units}
        t_bd = {u: eye + pw[u] for u in units}
        pw = {u: _dot(pw[u], pw[u]) for u in units}
        yield
        for _ in range(CHUNK.bit_length() - 3):
            sp = {u: _dot(jnp.concatenate([t_bd[u], pw[u]], axis=0), pw[u]) for u in units}
            t_bd = {u: t_bd[u] + sp[u][:LANES] for u in units}
            pw = {u: sp[u][LANES:] for u in units}
            yield
        t_bd = {u: bf(t_bd[u] + _dot(t_bd[u], pw[u])) for u in units}
        v_bd = {u: bf(stack(sub(v, u))) for u in units}
        avyv = {u: _dot(a_akrk[u], v_bd[u]) for u in units}
        yield
        wuv = {u: _dot(t_bd[u], jnp.concatenate([stack(sub(a_t, u)), stack(avyv[u][:CHUNK])], axis=1))
               for u in units}
        wr = {u: bf(jnp.concatenate([wuv[u][:, :LANES], sub(r_t, u)], axis=0)) for u in units}
        yield
        return a_rb, avyv, wuv, wr

    a_rb, avyv, wuv, wr = yield from prepare([(c, p) for c in range(n_chunks) for p in range(4)])

    h = [state_ref[p] for p in range(4)]
    for c in range(n_chunks):
        last = c * CHUNK + CHUNK - 1
        p_last = jnp.exp(cum[last:last + 1, :])
        hs = [_dot_nt(wr[(c, p)], h[p]) for p in range(4)]
        u2 = [hs[p][:LANES] + wuv[(c, p)][:, LANES:] for p in range(4)]
        u_p = [u2[p][:CHUNK] + u2[p][CHUNK:] for p in range(4)]
        upd = [_dot_tn(jnp.concatenate([u_p[p], sub(v, (c, p))], axis=0),
                       jnp.concatenate([sub(b_t, (c, p)), sub(k_t, (c, p))], axis=0)
                       * p_last[:, p * LANES:(p + 1) * LANES]) for p in range(4)]
        for p in range(4):
            y_ref[c * CHUNK:(c + 1) * CHUNK, p * LANES:(p + 1) * LANES] = (
                hs[p][LANES:] + _dot(a_rb[(c, p)], stack(u_p[p])) + avyv[(c, p)][CHUNK:])
        h = [h[p] * p_last[:, p * LANES:(p + 1) * LANES] + jnp.where(bd_mask, upd[p], 0.0) for p in range(4)]
        yield
    for p in range(4):
        state_ref[p] = h[p]


def _rwkv_kernel(has_vres, tb, *refs):
    if has_vres:
        (za_ref, vfirst_ref, mu_ref, dup_ref, w0_ref, aup_ref, a0_ref, gup_ref, vup_ref, v0_ref,
         kk_ref, ka_ref, rk_ref, lnw_ref, lnb_ref, o_ref, prev_ref, state_ref, y_ref) = refs
    else:
        (za_ref, mu_ref, dup_ref, w0_ref, aup_ref, a0_ref, gup_ref,
         kk_ref, ka_ref, rk_ref, lnw_ref, lnb_ref, o_ref, vraw_ref, prev_ref, state_ref, y_ref) = refs

    @pl.when(pl.program_id(1) == 0)
    def _():
        prev_ref[...] = jnp.zeros_like(prev_ref)
        state_ref[...] = jnp.zeros_like(state_ref)

    yield
    za = za_ref[...]
    shifted = pltpu.roll(za, 1, axis=0)
    shifted = jnp.where(_iota2(za.shape, 0) == 0, prev_ref[7:8, :], shifted)
    prev_ref[...] = za[tb - 8:, :]
    za = za + (shifted - za) * mu_ref[...]

    r = za[:, 0:512]
    k = za[:, 512:1024]
    v = za[:, 1024:1536]
    xwa = za[:, 1536:1664]
    xg = za[:, 1664:1920]
    w_log = -_softplus(-(w0_ref[...] + _dot_3x(jnp.tanh(xwa), dup_ref[...]))) - 0.5
    ld = -jnp.exp(w_log)
    a = _sigmoid(a0_ref[...] + _dot(xwa, aup_ref[...]))
    g = _dot(_sigmoid(xg), gup_ref[...])
    if has_vres:
        v_mix = _sigmoid(v0_ref[...] + _dot(xg, vup_ref[...]))
        v = v + (vfirst_ref[...] - v) * v_mix
    else:
        vraw_ref[...] = v
    kk = k * kk_ref[...]
    k = k * (1.0 + (a - 1.0) * ka_ref[...])
    head_sum = _group_mean_matrix(512, CHUNK) * float(CHUNK)
    kk = kk / jnp.maximum(jnp.sqrt(_dot(kk * kk, head_sum)), 1e-12)
    yield

    yield from _rwkv_block(r, k, v, kk, a, ld, tb, state_ref, y_ref)
    y = y_ref[...]
    head_mean = _group_mean_matrix(512, CHUNK)
    mean = _dot_split(y, head_mean)
    yc = y - mean
    var = _dot(yc * yc, head_mean)
    y = yc * lax.rsqrt(var + RWKV_LN_EPS) * lnw_ref[...] + lnb_ref[...]
    y = y + _dot(r * k * rk_ref[...], head_sum) * v
    o_ref[...] = (y * g).astype(o_ref.dtype)


def _rwkv_part(za, v_first, prm, tb):
    bsz, seq, _ = za.shape
    has_vres = v_first is not None
    blk = lambda n: pl.BlockSpec((None, tb, n), lambda b, i: (b, i, 0))
    names = (["mu", "dup", "w0", "aup", "a0", "gup"] + (["vup", "v0"] if has_vres else [])
             + ["kk", "ka", "rk", "lnw", "lnb"])
    weights = [prm[n] for n in names]
    ins = [za] + ([v_first] if has_vres else []) + weights
    in_specs = [blk(RWKV_PAD)] + ([blk(512)] if has_vres else []) + [_resident(w.shape) for w in weights]
    out_shape = [jax.ShapeDtypeStruct((bsz, seq, 512), BF16)]
    out_specs = [blk(512)]
    if not has_vres:
        out_shape.append(jax.ShapeDtypeStruct((bsz, seq, 512), F32))
        out_specs.append(blk(512))
    return _Part(functools.partial(_rwkv_kernel, has_vres, tb), ins, in_specs, out_shape, out_specs,
                 [pltpu.VMEM((8, RWKV_PAD), F32), pltpu.VMEM((4, LANES, LANES), F32), pltpu.VMEM((tb, 512), F32)])


def _mamba_kernel(tb, zb_ref, cw_ref, cb_ref, dtb_ref, alog_ref, dexp_ref, nw_ref,
                  o_ref, xbuf_ref, state_ref, y_ref):
    @pl.when(pl.program_id(1) == 0)
    def _():
        xbuf_ref[0:8, :] = jnp.zeros((8, 1024), F32)
        state_ref[...] = jnp.zeros_like(state_ref)

    yield
    xbuf_ref[8:8 + tb, :] = zb_ref[:, 512:1536]
    conv = cb_ref[...]
    for i in range(4):
        conv = conv + cw_ref[i:i + 1, :] * xbuf_ref[5 + i:5 + i + tb, :]
    xbuf_ref[0:8, :] = xbuf_ref[tb:tb + 8, :]
    xbc = _silu(conv)
    xs = xbc[:, 0:512]
    bm = xbc[:, 512:768]
    cm = xbc[:, 768:1024]
    yield

    lane = _iota2((1, LANES), 1)
    dt = _softplus(zb_ref[:, 1536:1664] + dtb_ref[...])
    a_neg = jnp.where(lane < SSM_HEADS, -jnp.exp(alog_ref[...]), 0.0)
    acs = _dot_split_l(_chunk_tri(tb), dt * a_neg)
    expand = jnp.where(_iota2((LANES, 512), 0) == _iota2((LANES, 512), 1) // CHUNK, 1.0, 0.0)
    dt_e = _dot(dt, expand)
    acs_e = _dot_split(acs, expand)
    xdt = xs * dt_e
    yield

    n_chunks = tb // CHUNK
    causal = _iota2((CHUNK, 512), 0) >= _iota2((CHUNK, 512), 1) % CHUNK
    pair_head = _iota2((CHUNK, LANES), 1) // CHUNK
    spread = jnp.where(_iota2((CHUNK, 512), 0) == _iota2((CHUNK, 512), 1) % CHUNK, 1.0, 0.0)
    own_head = _iota2((SSM_HEADS, 512), 0) == _iota2((SSM_HEADS, 512), 1) // CHUNK
    rows_of = lambda c: slice(c * CHUNK, (c + 1) * CHUNK)
    grp = lambda g: slice(g * SSM_STATE, (g + 1) * SSM_STATE)

    def stack(z):
        return jnp.concatenate([jnp.where(pair_head == 0, z, 0.0), jnp.where(pair_head == 1, z, 0.0)], axis=0)

    y_intra, upd, e_in, e_last = [], [], [], []
    for c in range(n_chunks):
        rows = rows_of(c)
        acs_ec = acs_e[rows]
        acs_t = acs[rows].T[0:SSM_HEADS]
        acs_row = jnp.sum(jnp.where(own_head, _dot_split(acs_t, spread), 0.0), axis=0, keepdims=True)
        decay = jnp.exp(jnp.where(causal, acs_ec - acs_row, -jnp.inf))
        cb = jnp.concatenate(
            [_dot_nt(cm[rows, grp(g)], jnp.concatenate([bm[rows, grp(g)]] * 4, axis=0)) for g in range(2)], axis=1)
        m = cb * decay
        yield
        xdt_c = xdt[rows]
        y_intra.append([_dot(m[:, p * LANES:(p + 1) * LANES], stack(xdt_c[:, p * LANES:(p + 1) * LANES]))
                        for p in range(4)])
        last_e = acs_ec[CHUNK - 1:CHUNK, :]
        x_out = xdt_c * jnp.exp(last_e - acs_ec)
        upd.append([_dot_tn(bm[rows, grp(g)], x_out[:, g * 256:(g + 1) * 256]) for g in range(2)])
        e_in.append(jnp.exp(acs_ec))
        e_last.append(jnp.exp(last_e))
        yield

    state = [state_ref[g] for g in range(2)]
    for c in range(n_chunks):
        rows = rows_of(c)
        for g in range(2):
            ls = slice(g * 256, (g + 1) * 256)
            y_in = jnp.concatenate(y_intra[c][2 * g:2 * g + 2], axis=1)
            y_ref[rows, ls] = y_in + _dot(cm[rows, grp(g)], state[g]) * e_in[c][:, ls]
            state[g] = state[g] * e_last[c][:, ls] + upd[c][g]
        yield
    for g in range(2):
        state_ref[g] = state[g]

    y = (y_ref[...] + xs * dexp_ref[...]) * _silu(zb_ref[:, 0:512])
    for g in range(2):
        ls = slice(g * 256, (g + 1) * 256)
        yg = y[:, ls]
        ms = jnp.mean(yg * yg, axis=-1, keepdims=True)
        o_ref[:, ls] = (yg * lax.rsqrt(ms + NORM_EPS) * nw_ref[:, ls]).astype(o_ref.dtype)


def _mamba_part(zb, prm, tb):
    bsz, seq, _ = zb.shape
    blk = lambda n: pl.BlockSpec((None, tb, n), lambda b, i: (b, i, 0))
    weights = [prm[n] for n in ("cw", "cb", "dtb", "alog", "dexp", "nw")]
    return _Part(functools.partial(_mamba_kernel, tb), [zb] + weights,
                 [blk(SSM_PAD)] + [_resident(w.shape) for w in weights],
                 [jax.ShapeDtypeStruct((bsz, seq, 512), BF16)], [blk(512)],
                 [pltpu.VMEM((tb + 8, 1024), F32), pltpu.VMEM((2, SSM_STATE, 256), F32), pltpu.VMEM((tb, 512), F32)])


def _attn_kernel(tq, seq, q_ref, k_ref, v_ref, qg_ref, kg_ref, bias_ref, o_ref, kn_ref, vb_ref):
    i = pl.program_id(1)
    head_mean = _group_mean_matrix(512, CHUNK)
    win = tq + ATT_LEFT

    @pl.when(i == 0)
    def _():
        kn_ref[0:ATT_LEFT, :] = jnp.zeros((ATT_LEFT, 512), BF16)
        vb_ref[0:ATT_LEFT, :] = jnp.zeros((ATT_LEFT, 512), BF16)
        for j in range(seq // tq):
            kj = k_ref[j * tq:(j + 1) * tq, :]
            ms = _dot(kj * kj, head_mean)
            kn_ref[ATT_LEFT + j * tq:ATT_LEFT + (j + 1) * tq, :] = (
                kj * lax.rsqrt(ms + NORM_EPS) * kg_ref[...]).astype(BF16)
            vb_ref[ATT_LEFT + j * tq:ATT_LEFT + (j + 1) * tq, :] = v_ref[j * tq:(j + 1) * tq, :].astype(BF16)

    yield
    q = q_ref[...]
    ms = _dot(q * q, head_mean)
    qn = q * lax.rsqrt(ms + NORM_EPS) * qg_ref[...] * (CHUNK ** -0.5 * LOG2_E)
    start = pl.multiple_of(i * tq, tq)
    kwin = kn_ref[pl.ds(start, win), :]
    vwin = vb_ref[pl.ds(start, win), :]
    lane_head = _iota2((tq, LANES), 1) // CHUNK
    yield
    for p in range(4):
        ls = slice(p * LANES, (p + 1) * LANES)
        q2 = jnp.concatenate([jnp.where(lane_head == s, qn[:, ls], 0.0) for s in range(2)], axis=0)
        sc = _dot_nt(q2, kwin[:, ls]) + bias_ref[2 * p:2 * p + 2].reshape(2 * tq, win).astype(F32)
        yield
        e = jnp.exp2(sc - jnp.max(sc, axis=-1, keepdims=True))
        yield
        o2 = _dot(e, vwin[:, ls]) / jnp.sum(e, axis=-1, keepdims=True)
        o_ref[:, ls] = jnp.where(lane_head == 0, o2[:tq], o2[tq:]).astype(o_ref.dtype)
        yield


def _band_bias(rel_bias, tq, win):
    period = tq + win
    m = jnp.arange(period)
    d = jnp.where(m < win, m, m - period) - ATT_LEFT
    f = rel_bias.astype(F32)[:, jnp.clip(d, -REL_CLIP, REL_CLIP) + REL_CLIP]
    g = jnp.tile(f, (1, tq))[:, :tq * (period - 1)].reshape(-1, tq, period - 1)
    bias = g[:, :, :win] * LOG2_E
    left = ATT_LEFT // CHUNK
    qc = left + jnp.arange(tq)[:, None] // CHUNK
    kc = jnp.arange(win)[None, :] // CHUNK
    tiles = []
    for blk in range(ATT_LEFT // tq + 1):
        first_kc = jnp.maximum(qc - left, left - blk * (tq // CHUNK))
        tiles.append(jnp.where((kc <= qc) & (kc >= first_kc), bias, -jnp.inf))
    return jnp.stack(tiles).astype(BF16)


def _attention_part(zc, q_gain, k_gain, rel_bias, tq):
    bsz, seq, _ = zc.shape
    win = tq + ATT_LEFT
    bias = _band_bias(rel_bias, tq, win)
    qg = jnp.tile(q_gain, ATT_HEADS)[None, :]
    kg = jnp.tile(k_gain, ATT_HEADS)[None, :]
    full = lambda col: pl.BlockSpec((None, seq, 512), lambda b, i: (b, 0, col), pipeline_mode=pl.Buffered(1))
    last_tile = bias.shape[0] - 1
    bias_spec = pl.BlockSpec((None,) + bias.shape[1:], lambda b, i: (jnp.minimum(i, last_tile), 0, 0, 0))
    blk = pl.BlockSpec((None, tq, 512), lambda b, i: (b, i, 0))
    return _Part(functools.partial(_attn_kernel, tq, seq), [zc, zc, zc, qg, kg, bias],
                 [blk, full(1), full(2), _resident(qg.shape), _resident(kg.shape), bias_spec],
                 [jax.ShapeDtypeStruct((bsz, seq, 512), BF16)], [blk],
                 [pltpu.VMEM((ATT_LEFT + seq, 512), BF16), pltpu.VMEM((ATT_LEFT + seq, 512), BF16)])


def _gla_kernel(tb, zd_ref, gup_ref, gb_ref, nw_ref, o_ref, state_ref, y_ref):
    @pl.when(pl.program_id(1) == 0)
    def _():
        state_ref[...] = jnp.zeros_like(state_ref)

    yield
    q = zd_ref[:, 0:256] * (CHUNK ** -0.5)
    k = zd_ref[:, 256:512]
    v = zd_ref[:, 512:1024]
    log_a = -_softplus(-(_dot_3x(zd_ref[:, 1536:1664], gup_ref[...]) + gb_ref[...])) / GLA_GATE_NORM
    bcum = _dot_split_l(_chunk_tri(tb), log_a)

    k_head = _iota2((CHUNK, 256), 1) // CHUNK
    v_head = _iota2((CHUNK, 512), 1) // LANES
    causal = _iota2((CHUNK, 256), 0) >= _iota2((CHUNK, 256), 1) % CHUNK
    bd = (_iota2((512, 256), 0) // LANES) == (_iota2((512, 256), 1) // CHUNK)
    n_chunks = tb // CHUNK
    qg_all = (q * jnp.exp(bcum)).astype(BF16)
    kg_all = k * jnp.exp(-bcum)
    yield
    qg, att, o_intra, upd, e_last = [], [], [], [], []
    for c in range(n_chunks):
        rows = slice(c * CHUNK, (c + 1) * CHUNK)
        kg_bd = jnp.concatenate([jnp.where(k_head == h, kg_all[rows], 0.0) for h in range(4)], axis=0)
        qg.append(qg_all[rows])
        att.append(jnp.where(causal, _dot_nt(qg[c], kg_bd), 0.0))
        yield
    for c in range(n_chunks):
        rows = slice(c * CHUNK, (c + 1) * CHUNK)
        bc = bcum[rows]
        blast = bc[CHUNK - 1:CHUNK, :]
        v_c = v[rows]
        v_bd = jnp.concatenate([jnp.where(v_head == h, v_c, 0.0) for h in range(4)], axis=0)
        o_intra.append(_dot(att[c], v_bd))
        upd.append(jnp.where(bd, _dot_tn(v_c, k[rows] * jnp.exp(blast - bc)), 0.0))
        e_last.append(jnp.exp(blast))
        yield
    st = state_ref[...]
    for c in range(n_chunks):
        y_ref[c * CHUNK:(c + 1) * CHUNK, :] = o_intra[c] + _dot_nt(qg[c], st)
        st = st * e_last[c] + upd[c]
        yield
    state_ref[...] = st

    o = y_ref[...]
    for h in range(4):
        ls = slice(h * LANES, (h + 1) * LANES)
        oh = o[:, ls]
        ms = jnp.mean(oh * oh, axis=-1, keepdims=True)
        o_ref[:, ls] = (oh * lax.rsqrt(ms + NORM_EPS) * nw_ref[...]
                        * _silu(zd_ref[:, 1024 + h * LANES:1024 + (h + 1) * LANES])).astype(o_ref.dtype)


def _gla_part(zd, prm, tb):
    bsz, seq, _ = zd.shape
    blk = lambda n: pl.BlockSpec((None, tb, n), lambda b, i: (b, i, 0))
    weights = [prm[n] for n in ("gup", "gb", "nw")]
    return _Part(functools.partial(_gla_kernel, tb), [zd] + weights,
                 [blk(GLA_PAD)] + [_resident(w.shape) for w in weights],
                 [jax.ShapeDtypeStruct((bsz, seq, 512), BF16)], [blk(512)],
                 [pltpu.VMEM((512, 256), F32), pltpu.VMEM((tb, 512), F32)])


def _merge_kernel(x_ref, h_ref, oa_ref, ob_ref, oc_ref, od_ref, wg_ref, wb_ref, wo_ref, nf_ref,
                  x1_ref, hf_ref):
    h = h_ref[...]
    acc = jnp.zeros(x_ref.shape, F32)
    for i, o_ref in enumerate((oa_ref, ob_ref, oc_ref, od_ref)):
        gate = _sigmoid(jnp.dot(h, wg_ref[:, i * D_MODEL:(i + 1) * D_MODEL], preferred_element_type=F32))
        acc = acc + gate * jnp.dot(o_ref[...], wb_ref[i], preferred_element_type=F32)
    x1 = x_ref[...] + jnp.dot(acc.astype(BF16), wo_ref[...], preferred_element_type=F32)
    x1_ref[...] = x1
    ms = jnp.mean(x1 * x1, axis=-1, keepdims=True)
    hf_ref[...] = (x1 * lax.rsqrt(ms + NORM_EPS) * nf_ref[...]).astype(BF16)


def _merge(x2, h, outs, wg, wb, wo, nf, tm=512):
    n_tok = x2.shape[0]
    row = lambda n: pl.BlockSpec((tm, n), lambda i: (i, 0))
    return pl.pallas_call(
        _merge_kernel,
        grid=(n_tok // tm,),
        in_specs=[row(D_MODEL), row(D_MODEL)] + [row(BRANCH_DIM)] * 4
        + [_resident(wg.shape), _resident(wb.shape), _resident(wo.shape), _resident(nf.shape)],
        out_specs=[row(D_MODEL), row(D_MODEL)],
        out_shape=[jax.ShapeDtypeStruct((n_tok, D_MODEL), F32), jax.ShapeDtypeStruct((n_tok, D_MODEL), BF16)],
        compiler_params=_params("parallel"),
        name="merge",
    )(x2, h, *outs, wg, wb, wo, nf)


def _ffn_kernel(tf, x1_ref, hf_ref, w1_ref, w3_ref, w2_ref, o_ref):
    hf = hf_ref[...]
    acc = x1_ref[...]
    for lo in range(0, FFN_DIM, tf):
        cols = slice(lo, min(lo + tf, FFN_DIM))
        a = jnp.dot(hf, w1_ref[:, cols], preferred_element_type=F32)
        b = jnp.dot(hf, w3_ref[:, cols], preferred_element_type=F32)
        acc = acc + jnp.dot((_silu(a) * b).astype(BF16), w2_ref[cols, :], preferred_element_type=F32)
    o_ref[...] = acc


def _ffn(x1, hf, w1, w3, w2, tm=512, tf=512):
    n_tok = x1.shape[0]
    row = pl.BlockSpec((tm, D_MODEL), lambda i: (i, 0))
    return pl.pallas_call(
        functools.partial(_ffn_kernel, tf),
        grid=(n_tok // tm,),
        in_specs=[row, row, _resident(w1.shape), _resident(w3.shape), _resident(w2.shape)],
        out_specs=row,
        out_shape=jax.ShapeDtypeStruct((n_tok, D_MODEL), F32),
        compiler_params=_params("parallel"),
        name="ffn",
    )(x1, hf, w1, w3, w2)


def _router_kernel(tm, x1_ref, nf_ref, wr_ref, idx_ref, gate_ref, rank_ref, cnt_ref, hf_ref, carry_ref):
    @pl.when(pl.program_id(0) == 0)
    def _():
        carry_ref[...] = jnp.zeros_like(carry_ref)

    x1 = x1_ref[...]
    ms = jnp.mean(x1 * x1, axis=-1, keepdims=True)
    hf = x1 * lax.rsqrt(ms + NORM_EPS) * nf_ref[...]
    for j in range(ROW_SPLIT):
        hf_ref[j] = hf[:, j * SUBROW:(j + 1) * SUBROW]
    logits = lax.dot_general(wr_ref[...], hf, (((1,), (1,)), ((), ())), precision=HIGHEST,
                             preferred_element_type=F32)
    e_iota = _iota2((N_EXPERTS, tm), 0)
    m1 = jnp.max(logits, axis=0, keepdims=True)
    i1 = jnp.min(jnp.where(logits == m1, e_iota, N_EXPERTS), axis=0, keepdims=True)
    rest = jnp.where(e_iota == i1, -jnp.inf, logits)
    m2 = jnp.max(rest, axis=0, keepdims=True)
    i2 = jnp.min(jnp.where(rest == m2, e_iota, N_EXPERTS), axis=0, keepdims=True)
    e2 = jnp.exp(m2 - m1)
    gate_ref[0:1, :] = 1.0 / (1.0 + e2)
    gate_ref[1:2, :] = e2 / (1.0 + e2)
    idx_ref[0:1, :] = i1
    idx_ref[1:2, :] = i2
    hit1 = jnp.where(e_iota == i1, 1.0, 0.0)
    hit2 = jnp.where(e_iota == i2, 1.0, 0.0)
    before = jnp.where(_iota2((tm, tm), 0) < _iota2((tm, tm), 1), 1.0, 0.0)
    prior = _dot(hit1 + hit2, before) + carry_ref[:, 0:1]
    rank_ref[0:1, :] = jnp.sum(hit1 * prior, axis=0, keepdims=True).astype(jnp.int32)
    rank_ref[1:2, :] = jnp.sum(hit2 * prior, axis=0, keepdims=True).astype(jnp.int32)
    carry_ref[...] = carry_ref[...] + jnp.sum(hit1 + hit2, axis=1, keepdims=True)
    cnt_ref[...] = carry_ref[...]


def _router(x1, nf, wr_t, tm=512):
    n_tok = x1.shape[0]
    col = pl.BlockSpec((2, tm), lambda i: (0, i))
    return pl.pallas_call(
        functools.partial(_router_kernel, tm),
        grid=(n_tok // tm,),
        in_specs=[pl.BlockSpec((tm, D_MODEL), lambda i: (i, 0)), _resident(nf.shape), _resident(wr_t.shape)],
        out_specs=[col, col, col, pl.BlockSpec((N_EXPERTS, LANES), lambda i: (0, 0)),
                   pl.BlockSpec((ROW_SPLIT, tm, SUBROW), lambda i: (0, i, 0))],
        out_shape=[jax.ShapeDtypeStruct((2, n_tok), jnp.int32), jax.ShapeDtypeStruct((2, n_tok), F32),
                   jax.ShapeDtypeStruct((2, n_tok), jnp.int32), jax.ShapeDtypeStruct((N_EXPERTS, LANES), F32),
                   jax.ShapeDtypeStruct((ROW_SPLIT, n_tok, SUBROW), F32)],
        scratch_shapes=[pltpu.VMEM((N_EXPERTS, LANES), F32)],
        compiler_params=_params("arbitrary"),
        name="moe_router",
    )(x1, nf, wr_t)


def _gather_rows(table, idx, window=128):
    split, n_table, width = table.shape
    n = idx.shape[0]
    flat_idx = (idx[None, :] + n_table * jnp.arange(split, dtype=jnp.int32)[:, None]).reshape(-1)
    return _gather_subrows(table.reshape(split * n_table, width), flat_idx, window).reshape(split, n, width)


def _gather_subrows(table, idx, window):
    n = idx.shape[0]
    d = table.shape[1]
    mesh = plsc.VectorSubcoreMesh(core_axis_name="core", subcore_axis_name="subcore")

    @functools.partial(pl.kernel, out_type=jax.ShapeDtypeStruct((n, d), table.dtype), mesh=mesh,
                       name="gather_rows")
    def gather(table_hbm, idx_hbm, out_hbm):
        def body(idx_vmem, out_vmem):
            pltpu.sync_copy(table_hbm.at[idx_vmem.at[0]], out_vmem)

        pltpu.emit_pipeline(
            body,
            grid=(n // window,),
            in_specs=[pl.BlockSpec((1, window), index_map=lambda i: (0, i))],
            out_specs=[pl.BlockSpec((window, d), index_map=lambda i: (i, 0))],
            core_axis_name=("core", "subcore"),
            dimension_semantics=(pltpu.PARALLEL,),
        )(idx_hbm, out_hbm)

    return gather(table, idx.reshape(1, n))


def _scatter_rows(table, dest, n_out, window=128):
    split, n_table, width = table.shape
    n = dest.shape[0]
    flat_dest = (dest[None, :] + n_out * jnp.arange(split, dtype=jnp.int32)[:, None]).reshape(1, split * n)
    src_blocks, blocks = n_table // window, n // window
    mesh = plsc.VectorSubcoreMesh(core_axis_name="core", subcore_axis_name="subcore")

    @functools.partial(pl.kernel, out_type=jax.ShapeDtypeStruct((split * n_out, width), table.dtype), mesh=mesh,
                       name="scatter_rows")
    def scatter(table_hbm, dest_hbm, out_hbm):
        def body(rows_vmem, dest_vmem):
            pltpu.sync_copy(rows_vmem, out_hbm.at[dest_vmem.at[0]])

        pltpu.emit_pipeline(
            body,
            grid=(split * blocks,),
            in_specs=[pl.BlockSpec((window, width),
                                   index_map=lambda i: ((i // blocks) * src_blocks + (i % blocks) % src_blocks, 0)),
                      pl.BlockSpec((1, window), index_map=lambda i: (0, i))],
            out_specs=[],
            core_axis_name=("core", "subcore"),
            dimension_semantics=(pltpu.PARALLEL,),
        )(table_hbm, dest_hbm)

    return scatter(table.reshape(split * n_table, width), flat_dest).reshape(split, n_out, width)


def _expert_kernel(tf, ge_ref, nv_ref, x_ref, w1_ref, w3_ref, w2_ref, y_ref):
    g = pl.program_id(0)

    @pl.when(g < nv_ref[0])
    def _():
        x = jnp.concatenate([x_ref[j] for j in range(ROW_SPLIT)], axis=-1).astype(BF16)
        acc = jnp.zeros((MOE_ROWS, D_MODEL), F32)
        for lo in range(0, EXPERT_DIM, tf):
            a = jnp.dot(x, w1_ref[:, lo:lo + tf], preferred_element_type=F32)
            b = jnp.dot(x, w3_ref[:, lo:lo + tf], preferred_element_type=F32)
            acc = acc + jnp.dot((_silu(a) * b).astype(BF16), w2_ref[lo:lo + tf, :], preferred_element_type=F32)
        for j in range(ROW_SPLIT):
            y_ref[j] = acc[:, j * SUBROW:(j + 1) * SUBROW]

    @pl.when(g >= nv_ref[0])
    def _():
        y_ref[...] = jnp.zeros_like(y_ref)


def _experts(xg, group_expert, n_valid, w1, w3, w2, tf=512):
    n_rows = xg.shape[1]
    rows = pl.BlockSpec((ROW_SPLIT, MOE_ROWS, SUBROW), lambda g, ge, nv: (0, g, 0))
    return pl.pallas_call(
        functools.partial(_expert_kernel, tf),
        grid_spec=pltpu.PrefetchScalarGridSpec(
            num_scalar_prefetch=2,
            grid=(n_rows // MOE_ROWS,),
            in_specs=[rows,
                      pl.BlockSpec((None, D_MODEL, EXPERT_DIM), lambda g, ge, nv: (ge[g], 0, 0)),
                      pl.BlockSpec((None, D_MODEL, EXPERT_DIM), lambda g, ge, nv: (ge[g], 0, 0)),
                      pl.BlockSpec((None, EXPERT_DIM, D_MODEL), lambda g, ge, nv: (ge[g], 0, 0))],
            out_specs=rows,
        ),
        out_shape=jax.ShapeDtypeStruct((ROW_SPLIT, n_rows, SUBROW), F32),
        compiler_params=_params("arbitrary", vmem_limit=BIG_VMEM_LIMIT),
        name="experts",
    )(group_expert, n_valid, xg, w1, w3, w2)


def _cast_kernel(w_ref, o_ref):
    o_ref[...] = w_ref[...].astype(o_ref.dtype)


def _to_bf16(w):
    n, rows, cols = w.shape
    blk = pl.BlockSpec((None, rows // 2, cols), lambda i, j: (i, j, 0))
    return pl.pallas_call(
        _cast_kernel,
        grid=(n, 2),
        in_specs=[blk],
        out_specs=blk,
        out_shape=jax.ShapeDtypeStruct(w.shape, BF16),
        compiler_params=_params("parallel", "parallel"),
        name="cast_bf16",
    )(w)


def _combine_kernel(x1_ref, y0_ref, y1_ref, gate_ref, o_ref):
    gate = gate_ref[...]
    for j in range(ROW_SPLIT):
        cols = slice(j * SUBROW, (j + 1) * SUBROW)
        o_ref[:, cols] = x1_ref[:, cols] + gate[:, 0:1] * y0_ref[j] + gate[:, 1:2] * y1_ref[j]


def _combine(x1, yg, gate_t, tm=512):
    n_tok = x1.shape[0]
    nb = n_tok // tm
    return pl.pallas_call(
        _combine_kernel,
        grid=(nb,),
        in_specs=[pl.BlockSpec((tm, D_MODEL), lambda i: (i, 0)),
                  pl.BlockSpec((ROW_SPLIT, tm, SUBROW), lambda i: (0, i, 0)),
                  pl.BlockSpec((ROW_SPLIT, tm, SUBROW), lambda i: (0, i + nb, 0)),
                  pl.BlockSpec((tm, 2), lambda i: (i, 0))],
        out_specs=pl.BlockSpec((tm, D_MODEL), lambda i: (i, 0)),
        out_shape=jax.ShapeDtypeStruct((n_tok, D_MODEL), F32),
        compiler_params=_params("parallel"),
        name="moe_combine",
    )(x1, yg, yg, gate_t)


def _moe(x1, nf, w_router, w1, w3, w2):
    n_tok = x1.shape[0]
    idx, gate, rank, cnt, hf = _router(x1, nf, w_router.T)
    counts = cnt[:, 0].astype(jnp.int32)
    padded = (counts + MOE_ROWS - 1) // MOE_ROWS * MOE_ROWS
    end_padded = jnp.cumsum(padded)
    start_padded = end_padded - padded
    start = sum(jnp.where(idx == e, start_padded[e], 0) for e in range(N_EXPERTS))
    dest = (start + rank).reshape(-1)
    n_groups = (n_tok * 2 + MOE_ROWS - 1) // MOE_ROWS + N_EXPERTS
    n_rows = n_groups * MOE_ROWS
    group_row = jnp.arange(n_groups, dtype=jnp.int32)[:, None] * MOE_ROWS
    group_expert = jnp.minimum(jnp.sum(group_row >= end_padded[None, :], axis=1), N_EXPERTS - 1).astype(jnp.int32)
    n_valid = (end_padded[-1:] // MOE_ROWS).astype(jnp.int32)
    fill = jnp.arange(MOE_ROWS, dtype=jnp.int32)[None, :]
    pad_dest = jnp.where(fill < (padded - counts)[:, None], (start_padded + counts)[:, None] + fill,
                         n_rows - MOE_ROWS + fill).reshape(-1)
    xg = _scatter_rows(hf, jnp.concatenate([dest, pad_dest]), n_rows)
    y = _experts(xg, group_expert, n_valid, w1, w3, w2)
    return _combine(x1, _gather_rows(y, dest), gate.T)


def _row(v, pad=0):
    v = v.reshape(1, -1).astype(F32)
    return jnp.pad(v, ((0, 0), (0, pad))) if pad else v


def _rows_at(w, start, total):
    return jnp.pad(w.astype(F32), ((start, total - start - w.shape[0]), (0, 0)))


def kernel(x, w_in, norm_mix, rwkv_mu, rwkv_decay_up, rwkv_w0, rwkv_a_up, rwkv_a0, rwkv_gate_up, rwkv_k_k, rwkv_k_a, rwkv_r_k, rwkv_ln_w, rwkv_ln_b, vres_down, vres_up, vres_v0, ssm_conv_w, ssm_conv_b, ssm_dt_bias, ssm_a_log, ssm_d, ssm_norm_w, att_q_gain, att_k_gain, att_rel_bias, gla_gate_up, gla_gate_bias, gla_norm_w, w_branch, w_out, norm_ffn, ffn_w1, ffn_w3, ffn_w2, moe_router, moe_w1, moe_w3, moe_w2):
    bsz, seq, _ = x.shape
    n_tok = bsz * seq
    depth = w_in.shape[0]
    x2 = x.reshape(n_tok, D_MODEL)
    v_first = None
    o_rw, o_ss, o_at = RWKV_COLS, RWKV_COLS + SSM_COLS, RWKV_COLS + SSM_COLS + ATT_COLS
    o_gl = o_at + GLA_COLS
    for l in range(depth):
        w = w_in[l]
        zeros = lambda n: jnp.zeros((D_MODEL, n), BF16)
        cols = lambda lo, hi: w[:, lo:hi].astype(BF16)
        vd = vres_down[l - 1].astype(BF16) if l > 0 else zeros(32)
        w_mix = jnp.concatenate(
            [cols(0, o_rw), vd, zeros(RWKV_PAD - RWKV_COLS - 32),
             cols(o_rw, o_ss), zeros(SSM_PAD - SSM_COLS),
             cols(o_ss, o_at),
             cols(o_at, o_at + 1024), cols(o_at + 1040, o_gl), cols(o_at + 1024, o_at + 1040),
             zeros(GLA_PAD - GLA_COLS)], axis=1)
        wg = w[:, o_gl:]
        h, za, zb, zc, zd = _inproj(x2, _row(norm_mix[l]), w_mix)
        shp = lambda z: z.reshape(bsz, seq, z.shape[-1])

        rw = dict(mu=_row(rwkv_mu[l], RWKV_PAD - RWKV_COLS),
                  dup=_rows_at(rwkv_decay_up[l], 0, LANES), w0=_row(rwkv_w0[l]),
                  aup=_rows_at(rwkv_a_up[l], 64, LANES), a0=_row(rwkv_a0[l]),
                  gup=_rows_at(rwkv_gate_up[l], 0, 256).astype(BF16),
                  kk=_row(rwkv_k_k[l]), ka=_row(rwkv_k_a[l]), rk=_row(rwkv_r_k[l]),
                  lnw=_row(rwkv_ln_w[l]), lnb=_row(rwkv_ln_b[l]))
        if l > 0:
            rw.update(vup=_rows_at(vres_up[l - 1], 160, 256), v0=_row(vres_v0[l - 1]))
        ssm = dict(cw=ssm_conv_w[l], cb=_row(ssm_conv_b[l]), dtb=_row(ssm_dt_bias[l], LANES - SSM_HEADS),
                   alog=_row(ssm_a_log[l], LANES - SSM_HEADS), dexp=_row(jnp.repeat(ssm_d[l], CHUNK)),
                   nw=_row(ssm_norm_w[l]))
        gla = dict(gup=_rows_at(gla_gate_up[l], 0, LANES), gb=_row(gla_gate_bias[l]), nw=_row(gla_norm_w[l]))
        (o_c,), (o_d,), (o_b,), res_a = _run_parts(
            [_attention_part(shp(zc), att_q_gain[l], att_k_gain[l], att_rel_bias, MIXER_BLOCK),
             _gla_part(shp(zd), gla, MIXER_BLOCK), _mamba_part(shp(zb), ssm, MIXER_BLOCK),
             _rwkv_part(shp(za), v_first, rw, MIXER_BLOCK)],
            grid=(bsz, seq // MIXER_BLOCK), name="mixers")
        o_a = res_a[0]
        if l == 0:
            v_first = res_a[1]
        outs = [o.reshape(n_tok, BRANCH_DIM) for o in (o_a, o_b, o_c, o_d)]
        x1, hf = _merge(x2, h, outs, wg.astype(BF16), w_branch[l].astype(BF16), w_out[l].astype(BF16),
                        _row(norm_ffn[l]))
        if l % 2 == 0:
            x2 = _ffn(x1, hf, ffn_w1[l // 2].astype(BF16), ffn_w3[l // 2].astype(BF16),
                      ffn_w2[l // 2].astype(BF16))
        else:
            x2 = _moe(x1, _row(norm_ffn[l]), moe_router[l // 2], _to_bf16(moe_w1[l // 2]),
                      _to_bf16(moe_w3[l // 2]), _to_bf16(moe_w2[l // 2]))
    return x2.reshape(bsz, seq, D_MODEL)
```

```python
import functools
from typing import Any, NamedTuple

import jax
import jax.numpy as jnp
from jax import lax
from jax.experimental import pallas as pl
from jax.experimental.pallas import tpu as pltpu
from jax.experimental.pallas import tpu_sc as plsc

F32 = jnp.float32
BF16 = jnp.bfloat16

D_MODEL = 1024
CHUNK = 64
BRANCH_DIM = 512
NORM_EPS = 1e-6
LANES = 128
VMEM_LIMIT = 56 * 1024 * 1024
BIG_VMEM_LIMIT = 61 * 1024 * 1024

LOG2_E = 1.4426950408889634

RWKV_LN_EPS = 64e-5
RWKV_DECAY_SCALE = 0.6065306597126334
RWKV_COLS = 1824
RWKV_PAD = 1920
SSM_COLS = 1544
SSM_PAD = 1664
SSM_HEADS = 8
SSM_STATE = 128
ATT_COLS = 1536
ATT_HEADS = 8
ATT_LEFT = 8 * CHUNK
REL_CLIP = 2 * CHUNK
GLA_COLS = 1552
GLA_PAD = 1664
GLA_GATE_NORM = 16.0
FFN_DIM = 2816
N_EXPERTS = 8
EXPERT_DIM = 3584
MOE_ROWS = 256
ROW_SPLIT = 4
SUBROW = D_MODEL // ROW_SPLIT


def _dot(a, b):
    return jnp.dot(a.astype(BF16), b.astype(BF16), preferred_element_type=F32)


def _dot_nt(a, b):
    return lax.dot_general(a.astype(BF16), b.astype(BF16), (((1,), (1,)), ((), ())),
                           preferred_element_type=F32)


def _dot_tn(a, b):
    return lax.dot_general(a.astype(BF16), b.astype(BF16), (((0,), (0,)), ((), ())),
                           preferred_element_type=F32)


def _hi_lo(a):
    hi = a.astype(BF16)
    return hi, (a - hi.astype(F32)).astype(BF16)


def _dot_3x(a, b):
    a_hi, a_lo = _hi_lo(a)
    b_hi, b_lo = _hi_lo(b)
    return (jnp.dot(a_hi, b_hi, preferred_element_type=F32) + jnp.dot(a_lo, b_hi, preferred_element_type=F32)
            + jnp.dot(a_hi, b_lo, preferred_element_type=F32))


def _dot_split(a, m):
    m = m.astype(BF16)
    hi, lo = _hi_lo(a)
    return jnp.dot(hi, m, preferred_element_type=F32) + jnp.dot(lo, m, preferred_element_type=F32)


def _dot_split_l(m, a):
    m = m.astype(BF16)
    hi, lo = _hi_lo(a)
    return jnp.dot(m, hi, preferred_element_type=F32) + jnp.dot(m, lo, preferred_element_type=F32)


def _softplus(x):
    return jnp.maximum(x, 0.0) + jnp.log(1.0 + jnp.exp(-jnp.abs(x)))


def _sigmoid(x):
    return 1.0 / (1.0 + jnp.exp(-x))


def _silu(x):
    return x * _sigmoid(x)


def _iota2(shape, axis):
    return lax.broadcasted_iota(jnp.int32, shape, axis)


def _group_mean_matrix(n, group):
    r = _iota2((n, n), 0) // group
    c = _iota2((n, n), 1) // group
    return jnp.where(r == c, 1.0 / group, 0.0).astype(F32)


def _chunk_tri(n):
    r = _iota2((n, n), 0)
    c = _iota2((n, n), 1)
    return jnp.where((r // CHUNK == c // CHUNK) & (r >= c), 1.0, 0.0).astype(F32)


def _resident(shape):
    nd = len(shape)
    return pl.BlockSpec(shape, lambda *_: (0,) * nd, pipeline_mode=pl.Buffered(1))


def _params(*sem, vmem_limit=VMEM_LIMIT):
    return pltpu.CompilerParams(dimension_semantics=sem, vmem_limit_bytes=vmem_limit)


_DONE = object()


class _Part(NamedTuple):
    body: Any
    inputs: list
    in_specs: list
    out_shapes: list
    out_specs: list
    scratch_shapes: list


def _run_parts(parts, grid, name):
    def split(refs, counts):
        out, lo = [], 0
        for n in counts:
            out.append(refs[lo:lo + n])
            lo += n
        return out

    n_in = [len(p.inputs) for p in parts]
    n_out = [len(p.out_shapes) for p in parts]
    n_scr = [len(p.scratch_shapes) for p in parts]

    def kernel(*refs):
        ins = split(refs[:sum(n_in)], n_in)
        outs = split(refs[sum(n_in):sum(n_in) + sum(n_out)], n_out)
        scr = split(refs[sum(n_in) + sum(n_out):], n_scr)
        stages = [part.body(*i, *o, *c) for part, i, o, c in zip(parts, ins, outs, scr)]
        while stages:
            for stage in list(stages):
                if next(stage, _DONE) is _DONE:
                    stages.remove(stage)

    flat = lambda field: [x for p in parts for x in getattr(p, field)]
    res = pl.pallas_call(
        kernel,
        grid=grid,
        in_specs=flat("in_specs"),
        out_specs=flat("out_specs"),
        out_shape=flat("out_shapes"),
        scratch_shapes=flat("scratch_shapes"),
        compiler_params=_params("parallel", "arbitrary", vmem_limit=BIG_VMEM_LIMIT),
        name=name,
    )(*flat("inputs"))
    return split(list(res), n_out)


MIXER_COLS = (RWKV_PAD, SSM_PAD, ATT_COLS, GLA_PAD)
MIXER_BLOCK = 256


def _inproj_kernel(x_ref, g_ref, w_ref, h_ref, *z_refs):
    x = x_ref[...]
    ms = jnp.mean(x * x, axis=-1, keepdims=True)
    h = (x * lax.rsqrt(ms + NORM_EPS) * g_ref[...]).astype(BF16)
    h_ref[...] = h
    lo = 0
    for z_ref, n in zip(z_refs, MIXER_COLS):
        z_ref[...] = jnp.dot(h, w_ref[:, lo:lo + n], preferred_element_type=F32)
        lo += n


def _inproj(x2, g, w_mix, tm=512):
    n_tok = x2.shape[0]
    row = lambda n: pl.BlockSpec((tm, n), lambda i: (i, 0))
    return pl.pallas_call(
        _inproj_kernel,
        grid=(n_tok // tm,),
        in_specs=[row(D_MODEL), _resident((1, D_MODEL)), _resident(w_mix.shape)],
        out_specs=[row(D_MODEL)] + [row(n) for n in MIXER_COLS],
        out_shape=[jax.ShapeDtypeStruct((n_tok, D_MODEL), BF16)]
        + [jax.ShapeDtypeStruct((n_tok, n), F32) for n in MIXER_COLS],
        compiler_params=_params("parallel"),
        name="inproj",
    )(x2, g, w_mix)


def _rwkv_block(r, k, v, kk, ka, ld, tb, state_ref, y_ref):
    n_chunks = tb // CHUNK
    cum = _dot_split_l(_chunk_tri(tb), ld)
    p_inv = jnp.exp(-cum)
    a_t = -kk * jnp.exp(cum - ld)
    b_t = kk * ka * p_inv
    k_t = k * p_inv
    r_t = r * jnp.exp(cum)
    yield

    lane_head = _iota2((CHUNK, LANES), 1) // CHUNK
    row = _iota2((CHUNK, LANES), 0)
    col = _iota2((CHUNK, LANES), 1) % CHUNK
    strict = row > col
    incl = row >= col
    bd_mask = (_iota2((LANES, LANES), 0) // CHUNK) == (_iota2((LANES, LANES), 1) // CHUNK)
    eye = jnp.where(_iota2((LANES, LANES), 0) == _iota2((LANES, LANES), 1), 1.0, 0.0).astype(F32)

    def stack(z):
        return jnp.concatenate([jnp.where(lane_head == 0, z, 0.0), jnp.where(lane_head == 1, z, 0.0)], axis=0)

    def sub(z, u):
        c, p = u
        return z[c * CHUNK:(c + 1) * CHUNK, p * LANES:(p + 1) * LANES]

    bf = lambda z: z.astype(BF16)

    def prepare(units):
        gram = {u: _dot_nt(jnp.concatenate([sub(a_t, u), sub(r_t, u)], axis=0),
                           jnp.concatenate([stack(sub(b_t, u)), stack(sub(k_t, u))], axis=0))
                for u in units}
        yield
        a_rb = {u: bf(jnp.where(incl, gram[u][CHUNK:, :LANES], 0.0)) for u in units}
        a_akrk = {u: bf(jnp.concatenate([jnp.where(strict, gram[u][:CHUNK, LANES:], 0.0),
                                         jnp.where(incl, gram[u][CHUNK:, LANES:], 0.0)], axis=0)) for u in units}
        pw = {u: stack(jnp.where(strict, gram[u][:CHUNK, :LANES], 0.0)) for u in units}
        t_bd = {u: eye + pw[u] for u in units}
        pw = {u: _dot(pw[u], pw[u]) for u in units}
        yield
        for _ in range(CHUNK.bit_length() - 3):
            sp = {u: _dot(jnp.concatenate([t_bd[u], pw[u]], axis=0), pw[u]) for u in units}
            t_bd = {u: t_bd[u] + sp[u][:LANES] for u in units}
            pw = {u: sp[u][LANES:] for u in units}
            yield
        t_bd = {u: bf(t_bd[u] + _dot(t_bd[u], pw[u])) for u in units}
        v_bd = {u: bf(stack(sub(v, u))) for u in units}
        avyv = {u: _dot(a_akrk[u], v_bd[u]) for u in units}
        yield
        wuv = {u: _dot(t_bd[u], jnp.concatenate([stack(sub(a_t, u)), stack(avyv[u][:CHUNK])], axis=1))
               for u in units}
        wr = {u: bf(jnp.concatenate([wuv[u][:, :LANES], sub(r_t, u)], axis=0)) for u in units}
        yield
        return a_rb, avyv, wuv, wr

    a_rb, avyv, wuv, wr = yield from prepare([(c, p) for c in range(n_chunks) for p in range(4)])

    h = [state_ref[p] for p in range(4)]
    for c in range(n_chunks):
        last = c * CHUNK + CHUNK - 1
        p_last = jnp.exp(cum[last:last + 1, :])
        hs = [_dot_nt(wr[(c, p)], h[p]) for p in range(4)]
        u2 = [hs[p][:LANES] + wuv[(c, p)][:, LANES:] for p in range(4)]
        u_p = [u2[p][:CHUNK] + u2[p][CHUNK:] for p in range(4)]
        upd = [_dot_tn(jnp.concatenate([u_p[p], sub(v, (c, p))], axis=0),
                       jnp.concatenate([sub(b_t, (c, p)), sub(k_t, (c, p))], axis=0)
                       * p_last[:, p * LANES:(p + 1) * LANES]) for p in range(4)]
        for p in range(4):
            y_ref[c * CHUNK:(c + 1) * CHUNK, p * LANES:(p + 1) * LANES] = (
                hs[p][LANES:] + _dot(a_rb[(c, p)], stack(u_p[p])) + avyv[(c, p)][CHUNK:])
        h = [h[p] * p_last[:, p * LANES:(p + 1) * LANES] + jnp.where(bd_mask, upd[p], 0.0) for p in range(4)]
        yield
    for p in range(4):
        state_ref[p] = h[p]


def _rwkv_kernel(has_vres, tb, *refs):
    if has_vres:
        (za_ref, vfirst_ref, mu_ref, dup_ref, w0_ref, aup_ref, a0_ref, gup_ref, vup_ref, v0_ref,
         kk_ref, ka_ref, rk_ref, lnw_ref, lnb_ref, o_ref, prev_ref, state_ref, y_ref) = refs
    else:
        (za_ref, mu_ref, dup_ref, w0_ref, aup_ref, a0_ref, gup_ref,
         kk_ref, ka_ref, rk_ref, lnw_ref, lnb_ref, o_ref, vraw_ref, prev_ref, state_ref, y_ref) = refs

    @pl.when(pl.program_id(1) == 0)
    def _():
        prev_ref[...] = jnp.zeros_like(prev_ref)
        state_ref[...] = jnp.zeros_like(state_ref)

    yield
    za = za_ref[...]
    shifted = pltpu.roll(za, 1, axis=0)
    shifted = jnp.where(_iota2(za.shape, 0) == 0, prev_ref[7:8, :], shifted)
    prev_ref[...] = za[tb - 8:, :]
    za = za + (shifted - za) * mu_ref[...]

    r = za[:, 0:512]
    k = za[:, 512:1024]
    v = za[:, 1024:1536]
    xwa = za[:, 1536:1664]
    xg = za[:, 1664:1920]
    ld = -RWKV_DECAY_SCALE * _sigmoid(w0_ref[...] + _dot_3x(jnp.tanh(xwa), dup_ref[...]))
    a = _sigmoid(a0_ref[...] + _dot(xwa, aup_ref[...]))
    g = _dot(_sigmoid(xg), gup_ref[...])
    if has_vres:
        v_mix = _sigmoid(v0_ref[...] + _dot(xg, vup_ref[...]))
        v = v + (vfirst_ref[...] - v) * v_mix
    else:
        vraw_ref[...] = v
    kk = k * kk_ref[...]
    k = k * (1.0 + (a - 1.0) * ka_ref[...])
    head_sum = _group_mean_matrix(512, CHUNK) * float(CHUNK)
    kk = kk / jnp.maximum(jnp.sqrt(_dot(kk * kk, head_sum)), 1e-12)
    yield

    yield from _rwkv_block(r, k, v, kk, a, ld, tb, state_ref, y_ref)
    y = y_ref[...]
    head_mean = _group_mean_matrix(512, CHUNK)
    mean = _dot_split(y, head_mean)
    yc = y - mean
    var = _dot(yc * yc, head_mean)
    y = yc * lax.rsqrt(var + RWKV_LN_EPS) * lnw_ref[...] + lnb_ref[...]
    y = y + _dot(r * k * rk_ref[...], head_sum) * v
    o_ref[...] = (y * g).astype(o_ref.dtype)


def _rwkv_part(za, v_first, prm, tb):
    bsz, seq, _ = za.shape
    has_vres = v_first is not None
    blk = lambda n: pl.BlockSpec((None, tb, n), lambda b, i: (b, i, 0))
    names = (["mu", "dup", "w0", "aup", "a0", "gup"] + (["vup", "v0"] if has_vres else [])
             + ["kk", "ka", "rk", "lnw", "lnb"])
    weights = [prm[n] for n in names]
    ins = [za] + ([v_first] if has_vres else []) + weights
    in_specs = [blk(RWKV_PAD)] + ([blk(512)] if has_vres else []) + [_resident(w.shape) for w in weights]
    out_shape = [jax.ShapeDtypeStruct((bsz, seq, 512), BF16)]
    out_specs = [blk(512)]
    if not has_vres:
        out_shape.append(jax.ShapeDtypeStruct((bsz, seq, 512), F32))
        out_specs.append(blk(512))
    return _Part(functools.partial(_rwkv_kernel, has_vres, tb), ins, in_specs, out_shape, out_specs,
                 [pltpu.VMEM((8, RWKV_PAD), F32), pltpu.VMEM((4, LANES, LANES), F32), pltpu.VMEM((tb, 512), F32)])


def _mamba_kernel(tb, zb_ref, cw_ref, cb_ref, dtb_ref, alog_ref, dexp_ref, nw_ref,
                  o_ref, xbuf_ref, state_ref, y_ref):
    @pl.when(pl.program_id(1) == 0)
    def _():
        xbuf_ref[0:8, :] = jnp.zeros((8, 1024), F32)
        state_ref[...] = jnp.zeros_like(state_ref)

    yield
    xbuf_ref[8:8 + tb, :] = zb_ref[:, 512:1536]
    conv = cb_ref[...]
    for i in range(4):
        conv = conv + cw_ref[i:i + 1, :] * xbuf_ref[5 + i:5 + i + tb, :]
    xbuf_ref[0:8, :] = xbuf_ref[tb:tb + 8, :]
    xbc = _silu(conv)
    xs = xbc[:, 0:512]
    bm = xbc[:, 512:768]
    cm = xbc[:, 768:1024]
    yield

    lane = _iota2((1, LANES), 1)
    dt = _softplus(zb_ref[:, 1536:1664] + dtb_ref[...])
    a_neg = jnp.where(lane < SSM_HEADS, -jnp.exp(alog_ref[...]), 0.0)
    acs = _dot_split_l(_chunk_tri(tb), dt * a_neg)
    expand = jnp.where(_iota2((LANES, 512), 0) == _iota2((LANES, 512), 1) // CHUNK, 1.0, 0.0)
    dt_e = _dot(dt, expand)
    acs_e = _dot_split(acs, expand)
    xdt = xs * dt_e
    yield

    n_chunks = tb // CHUNK
    causal = _iota2((CHUNK, 512), 0) >= _iota2((CHUNK, 512), 1) % CHUNK
    pair_head = _iota2((CHUNK, LANES), 1) // CHUNK
    spread = jnp.where(_iota2((CHUNK, 512), 0) == _iota2((CHUNK, 512), 1) % CHUNK, 1.0, 0.0)
    own_head = _iota2((SSM_HEADS, 512), 0) == _iota2((SSM_HEADS, 512), 1) // CHUNK
    rows_of = lambda c: slice(c * CHUNK, (c + 1) * CHUNK)
    grp = lambda g: slice(g * SSM_STATE, (g + 1) * SSM_STATE)

    def stack(z):
        return jnp.concatenate([jnp.where(pair_head == 0, z, 0.0), jnp.where(pair_head == 1, z, 0.0)], axis=0)

    y_intra, upd, e_in, e_last = [], [], [], []
    for c in range(n_chunks):
        rows = rows_of(c)
        acs_ec = acs_e[rows]
        acs_t = acs[rows].T[0:SSM_HEADS]
        acs_row = jnp.sum(jnp.where(own_head, _dot_split(acs_t, spread), 0.0), axis=0, keepdims=True)
        decay = jnp.exp(jnp.where(causal, acs_ec - acs_row, -jnp.inf))
        cb = jnp.concatenate(
            [_dot_nt(cm[rows, grp(g)], jnp.concatenate([bm[rows, grp(g)]] * 4, axis=0)) for g in range(2)], axis=1)
        m = cb * decay
        yield
        xdt_c = xdt[rows]
        y_intra.append([_dot(m[:, p * LANES:(p + 1) * LANES], stack(xdt_c[:, p * LANES:(p + 1) * LANES]))
                        for p in range(4)])
        last_e = acs_ec[CHUNK - 1:CHUNK, :]
        x_out = xdt_c * jnp.exp(last_e - acs_ec)
        upd.append([_dot_tn(bm[rows, grp(g)], x_out[:, g * 256:(g + 1) * 256]) for g in range(2)])
        e_in.append(jnp.exp(acs_ec))
        e_last.append(jnp.exp(last_e))
        yield

    state = [state_ref[g] for g in range(2)]
    for c in range(n_chunks):
        rows = rows_of(c)
        for g in range(2):
            ls = slice(g * 256, (g + 1) * 256)
            y_in = jnp.concatenate(y_intra[c][2 * g:2 * g + 2], axis=1)
            y_ref[rows, ls] = y_in + _dot(cm[rows, grp(g)], state[g]) * e_in[c][:, ls]
            state[g] = state[g] * e_last[c][:, ls] + upd[c][g]
        yield
    for g in range(2):
        state_ref[g] = state[g]

    y = (y_ref[...] + xs * dexp_ref[...]) * _silu(zb_ref[:, 0:512])
    for g in range(2):
        ls = slice(g * 256, (g + 1) * 256)
        yg = y[:, ls]
        ms = jnp.mean(yg * yg, axis=-1, keepdims=True)
        o_ref[:, ls] = (yg * lax.rsqrt(ms + NORM_EPS) * nw_ref[:, ls]).astype(o_ref.dtype)


def _mamba_part(zb, prm, tb):
    bsz, seq, _ = zb.shape
    blk = lambda n: pl.BlockSpec((None, tb, n), lambda b, i: (b, i, 0))
    weights = [prm[n] for n in ("cw", "cb", "dtb", "alog", "dexp", "nw")]
    return _Part(functools.partial(_mamba_kernel, tb), [zb] + weights,
                 [blk(SSM_PAD)] + [_resident(w.shape) for w in weights],
                 [jax.ShapeDtypeStruct((bsz, seq, 512), BF16)], [blk(512)],
                 [pltpu.VMEM((tb + 8, 1024), F32), pltpu.VMEM((2, SSM_STATE, 256), F32), pltpu.VMEM((tb, 512), F32)])


def _attn_kernel(tq, seq, q_ref, k_ref, v_ref, qg_ref, kg_ref, bias_ref, o_ref, kn_ref, vb_ref):
    i = pl.program_id(1)
    head_mean = _group_mean_matrix(512, CHUNK)
    win = tq + ATT_LEFT

    @pl.when(i == 0)
    def _():
        kn_ref[0:ATT_LEFT, :] = jnp.zeros((ATT_LEFT, 512), BF16)
        vb_ref[0:ATT_LEFT, :] = jnp.zeros((ATT_LEFT, 512), BF16)
        for j in range(seq // tq):
            kj = k_ref[j * tq:(j + 1) * tq, :]
            ms = _dot(kj * kj, head_mean)
            kn_ref[ATT_LEFT + j * tq:ATT_LEFT + (j + 1) * tq, :] = (
                kj * lax.rsqrt(ms + NORM_EPS) * kg_ref[...]).astype(BF16)
            vb_ref[ATT_LEFT + j * tq:ATT_LEFT + (j + 1) * tq, :] = v_ref[j * tq:(j + 1) * tq, :].astype(BF16)

    yield
    q = q_ref[...]
    ms = _dot(q * q, head_mean)
    qn = q * lax.rsqrt(ms + NORM_EPS) * qg_ref[...] * (CHUNK ** -0.5 * LOG2_E)
    start = pl.multiple_of(i * tq, tq)
    kwin = kn_ref[pl.ds(start, win), :]
    vwin = vb_ref[pl.ds(start, win), :]
    lane_head = _iota2((tq, LANES), 1) // CHUNK
    yield
    for p in range(4):
        ls = slice(p * LANES, (p + 1) * LANES)
        q2 = jnp.concatenate([jnp.where(lane_head == s, qn[:, ls], 0.0) for s in range(2)], axis=0)
        sc = _dot_nt(q2, kwin[:, ls]) + bias_ref[2 * p:2 * p + 2].reshape(2 * tq, win).astype(F32)
        yield
        e = jnp.exp2(sc - jnp.max(sc, axis=-1, keepdims=True))
        yield
        o2 = _dot(e, vwin[:, ls]) / jnp.sum(e, axis=-1, keepdims=True)
        o_ref[:, ls] = jnp.where(lane_head == 0, o2[:tq], o2[tq:]).astype(o_ref.dtype)
        yield


def _band_bias(rel_bias, tq, win):
    period = tq + win
    m = jnp.arange(period)
    d = jnp.where(m < win, m, m - period) - ATT_LEFT
    f = (rel_bias.astype(F32)[:, jnp.clip(d, -REL_CLIP, REL_CLIP) + REL_CLIP] * LOG2_E).astype(BF16)
    g = jnp.tile(f, (1, tq))[:, :tq * (period - 1)].reshape(-1, tq, period - 1)
    bias = g[:, :, :win]
    left = ATT_LEFT // CHUNK
    qc = left + jnp.arange(tq)[:, None] // CHUNK
    kc = jnp.arange(win)[None, :] // CHUNK
    tiles = []
    for blk in range(ATT_LEFT // tq + 1):
        first_kc = jnp.maximum(qc - left, left - blk * (tq // CHUNK))
        tiles.append(jnp.where((kc <= qc) & (kc >= first_kc), bias, -jnp.inf))
    return jnp.stack(tiles)


def _attention_part(zc, q_gain, k_gain, rel_bias, tq):
    bsz, seq, _ = zc.shape
    win = tq + ATT_LEFT
    bias = _band_bias(rel_bias, tq, win)
    qg = jnp.tile(q_gain, ATT_HEADS)[None, :]
    kg = jnp.tile(k_gain, ATT_HEADS)[None, :]
    full = lambda col: pl.BlockSpec((None, seq, 512), lambda b, i: (b, 0, col), pipeline_mode=pl.Buffered(1))
    last_tile = bias.shape[0] - 1
    bias_spec = pl.BlockSpec((None,) + bias.shape[1:], lambda b, i: (jnp.minimum(i, last_tile), 0, 0, 0))
    blk = pl.BlockSpec((None, tq, 512), lambda b, i: (b, i, 0))
    return _Part(functools.partial(_attn_kernel, tq, seq), [zc, zc, zc, qg, kg, bias],
                 [blk, full(1), full(2), _resident(qg.shape), _resident(kg.shape), bias_spec],
                 [jax.ShapeDtypeStruct((bsz, seq, 512), BF16)], [blk],
                 [pltpu.VMEM((ATT_LEFT + seq, 512), BF16), pltpu.VMEM((ATT_LEFT + seq, 512), BF16)])


def _gla_kernel(tb, zd_ref, gup_ref, gb_ref, nw_ref, o_ref, state_ref, y_ref):
    @pl.when(pl.program_id(1) == 0)
    def _():
        state_ref[...] = jnp.zeros_like(state_ref)

    yield
    q = zd_ref[:, 0:256] * (CHUNK ** -0.5)
    k = zd_ref[:, 256:512]
    v = zd_ref[:, 512:1024]
    log_a = -_softplus(-(_dot_3x(zd_ref[:, 1536:1664], gup_ref[...]) + gb_ref[...])) / GLA_GATE_NORM
    bcum = _dot_split_l(_chunk_tri(tb), log_a)

    k_head = _iota2((CHUNK, 256), 1) // CHUNK
    v_head = _iota2((CHUNK, 512), 1) // LANES
    causal = _iota2((CHUNK, 256), 0) >= _iota2((CHUNK, 256), 1) % CHUNK
    bd = (_iota2((512, 256), 0) // LANES) == (_iota2((512, 256), 1) // CHUNK)
    n_chunks = tb // CHUNK
    qg_all = (q * jnp.exp(bcum)).astype(BF16)
    kg_all = k * jnp.exp(-bcum)
    yield
    qg, att, o_intra, upd, e_last = [], [], [], [], []
    for c in range(n_chunks):
        rows = slice(c * CHUNK, (c + 1) * CHUNK)
        kg_bd = jnp.concatenate([jnp.where(k_head == h, kg_all[rows], 0.0) for h in range(4)], axis=0)
        qg.append(qg_all[rows])
        att.append(jnp.where(causal, _dot_nt(qg[c], kg_bd), 0.0))
        yield
    for c in range(n_chunks):
        rows = slice(c * CHUNK, (c + 1) * CHUNK)
        bc = bcum[rows]
        blast = bc[CHUNK - 1:CHUNK, :]
        v_c = v[rows]
        v_bd = jnp.concatenate([jnp.where(v_head == h, v_c, 0.0) for h in range(4)], axis=0)
        o_intra.append(_dot(att[c], v_bd))
        upd.append(jnp.where(bd, _dot_tn(v_c, k[rows] * jnp.exp(blast - bc)), 0.0))
        e_last.append(jnp.exp(blast))
        yield
    st = state_ref[...]
    for c in range(n_chunks):
        y_ref[c * CHUNK:(c + 1) * CHUNK, :] = o_intra[c] + _dot_nt(qg[c], st)
        st = st * e_last[c] + upd[c]
        yield
    state_ref[...] = st

    o = y_ref[...]
    for h in range(4):
        ls = slice(h * LANES, (h + 1) * LANES)
        oh = o[:, ls]
        ms = jnp.mean(oh * oh, axis=-1, keepdims=True)
        o_ref[:, ls] = (oh * lax.rsqrt(ms + NORM_EPS) * nw_ref[...]
                        * _silu(zd_ref[:, 1024 + h * LANES:1024 + (h + 1) * LANES])).astype(o_ref.dtype)


def _gla_part(zd, prm, tb):
    bsz, seq, _ = zd.shape
    blk = lambda n: pl.BlockSpec((None, tb, n), lambda b, i: (b, i, 0))
    weights = [prm[n] for n in ("gup", "gb", "nw")]
    return _Part(functools.partial(_gla_kernel, tb), [zd] + weights,
                 [blk(GLA_PAD)] + [_resident(w.shape) for w in weights],
                 [jax.ShapeDtypeStruct((bsz, seq, 512), BF16)], [blk(512)],
                 [pltpu.VMEM((512, 256), F32), pltpu.VMEM((tb, 512), F32)])


def _route(tm, hf, wr_ref, idx_ref, gate_ref, rank_ref, cnt_ref, hf_ref, carry_ref):
    @pl.when(pl.program_id(0) == 0)
    def _():
        carry_ref[...] = jnp.zeros_like(carry_ref)

    for j in range(ROW_SPLIT):
        hf_ref[j] = hf[:, j * SUBROW:(j + 1) * SUBROW]
    nt = lambda a, b: lax.dot_general(a, b, (((1,), (1,)), ((), ())), preferred_element_type=F32)
    w_hi, w_lo = _hi_lo(wr_ref[...])
    h_hi, h_lo = _hi_lo(hf)
    logits = nt(w_hi, h_hi) + nt(w_lo, h_hi) + nt(w_hi, h_lo)
    e_iota = _iota2((N_EXPERTS, tm), 0)
    m1 = jnp.max(logits, axis=0, keepdims=True)
    i1 = jnp.min(jnp.where(logits == m1, e_iota, N_EXPERTS), axis=0, keepdims=True)
    rest = jnp.where(e_iota == i1, -jnp.inf, logits)
    m2 = jnp.max(rest, axis=0, keepdims=True)
    i2 = jnp.min(jnp.where(rest == m2, e_iota, N_EXPERTS), axis=0, keepdims=True)
    e2 = jnp.exp(m2 - m1)
    gate_ref[0:1, :] = 1.0 / (1.0 + e2)
    gate_ref[1:2, :] = e2 / (1.0 + e2)
    idx_ref[0:1, :] = i1
    idx_ref[1:2, :] = i2
    hit1 = jnp.where(e_iota == i1, 1.0, 0.0)
    hit2 = jnp.where(e_iota == i2, 1.0, 0.0)
    before = jnp.where(_iota2((tm, tm), 0) < _iota2((tm, tm), 1), 1.0, 0.0)
    prior = _dot(hit1 + hit2, before) + carry_ref[:, 0:1]
    rank_ref[0:1, :] = jnp.sum(hit1 * prior, axis=0, keepdims=True).astype(jnp.int32)
    rank_ref[1:2, :] = jnp.sum(hit2 * prior, axis=0, keepdims=True).astype(jnp.int32)
    carry_ref[...] = carry_ref[...] + jnp.sum(hit1 + hit2, axis=1, keepdims=True)
    cnt_ref[...] = carry_ref[...]


def _merge_kernel(tm, route, x_ref, h_ref, oa_ref, ob_ref, oc_ref, od_ref, wg_ref, wb_ref, wo_ref, nf_ref, *rest):
    h = h_ref[...]
    acc = jnp.zeros(x_ref.shape, F32)
    for i, o_ref in enumerate((oa_ref, ob_ref, oc_ref, od_ref)):
        gate = _sigmoid(jnp.dot(h, wg_ref[:, i * D_MODEL:(i + 1) * D_MODEL], preferred_element_type=F32))
        acc = acc + gate * jnp.dot(o_ref[...], wb_ref[i], preferred_element_type=F32)
    x1 = x_ref[...] + jnp.dot(acc.astype(BF16), wo_ref[...], preferred_element_type=F32)
    ms = jnp.mean(x1 * x1, axis=-1, keepdims=True)
    hf = x1 * lax.rsqrt(ms + NORM_EPS) * nf_ref[...]
    if route:
        wr_ref, x1_ref = rest[:2]
        _route(tm, hf, wr_ref, *rest[2:])
    else:
        x1_ref, hf_ref = rest
        hf_ref[...] = hf.astype(BF16)
    x1_ref[...] = x1


def _merge(x2, h, outs, wg, wb, wo, nf, wr_t=None, tm=512):
    n_tok = x2.shape[0]
    route = wr_t is not None
    row = lambda n: pl.BlockSpec((tm, n), lambda i: (i, 0))
    ins = [x2, h, *outs, wg, wb, wo, nf]
    in_specs = ([row(D_MODEL), row(D_MODEL)] + [row(BRANCH_DIM)] * 4
                + [_resident(wg.shape), _resident(wb.shape), _resident(wo.shape), _resident(nf.shape)])
    out_specs = [row(D_MODEL)]
    out_shape = [jax.ShapeDtypeStruct((n_tok, D_MODEL), F32)]
    scratch = []
    if route:
        ins.append(wr_t)
        in_specs.append(_resident(wr_t.shape))
        col = pl.BlockSpec((2, tm), lambda i: (0, i))
        out_specs += [col, col, col, pl.BlockSpec((N_EXPERTS, LANES), lambda i: (0, 0)),
                      pl.BlockSpec((ROW_SPLIT, tm, SUBROW), lambda i: (0, i, 0))]
        out_shape += [jax.ShapeDtypeStruct((2, n_tok), jnp.int32), jax.ShapeDtypeStruct((2, n_tok), F32),
                      jax.ShapeDtypeStruct((2, n_tok), jnp.int32), jax.ShapeDtypeStruct((N_EXPERTS, LANES), F32),
                      jax.ShapeDtypeStruct((ROW_SPLIT, n_tok, SUBROW), F32)]
        scratch = [pltpu.VMEM((N_EXPERTS, LANES), F32)]
    else:
        out_specs.append(row(D_MODEL))
        out_shape.append(jax.ShapeDtypeStruct((n_tok, D_MODEL), BF16))
    return pl.pallas_call(
        functools.partial(_merge_kernel, tm, route),
        grid=(n_tok // tm,),
        in_specs=in_specs,
        out_specs=out_specs,
        out_shape=out_shape,
        scratch_shapes=scratch,
        compiler_params=_params("arbitrary" if route else "parallel"),
        name="merge",
    )(*ins)


def _ffn_kernel(tf, x1_ref, hf_ref, w1_ref, w3_ref, w2_ref, o_ref):
    hf = hf_ref[...]
    acc = x1_ref[...]
    for lo in range(0, FFN_DIM, tf):
        cols = slice(lo, min(lo + tf, FFN_DIM))
        a = jnp.dot(hf, w1_ref[:, cols], preferred_element_type=F32)
        b = jnp.dot(hf, w3_ref[:, cols], preferred_element_type=F32)
        acc = acc + jnp.dot((_silu(a) * b).astype(BF16), w2_ref[cols, :], preferred_element_type=F32)
    o_ref[...] = acc


def _ffn(x1, hf, w1, w3, w2, tm=512, tf=512):
    n_tok = x1.shape[0]
    row = pl.BlockSpec((tm, D_MODEL), lambda i: (i, 0))
    return pl.pallas_call(
        functools.partial(_ffn_kernel, tf),
        grid=(n_tok // tm,),
        in_specs=[row, row, _resident(w1.shape), _resident(w3.shape), _resident(w2.shape)],
        out_specs=row,
        out_shape=jax.ShapeDtypeStruct((n_tok, D_MODEL), F32),
        compiler_params=_params("parallel"),
        name="ffn",
    )(x1, hf, w1, w3, w2)


def _gather_rows(table, idx, window=128):
    split, n_table, width = table.shape
    n = idx.shape[0]
    flat_idx = (idx[None, :] + n_table * jnp.arange(split, dtype=jnp.int32)[:, None]).reshape(-1)
    return _gather_subrows(table.reshape(split * n_table, width), flat_idx, window).reshape(split, n, width)


def _gather_subrows(table, idx, window):
    n = idx.shape[0]
    d = table.shape[1]
    mesh = plsc.VectorSubcoreMesh(core_axis_name="core", subcore_axis_name="subcore")

    @functools.partial(pl.kernel, out_type=jax.ShapeDtypeStruct((n, d), table.dtype), mesh=mesh,
                       name="gather_rows")
    def gather(table_hbm, idx_hbm, out_hbm):
        def body(idx_vmem, out_vmem):
            pltpu.sync_copy(table_hbm.at[idx_vmem.at[0]], out_vmem)

        pltpu.emit_pipeline(
            body,
            grid=(n // window,),
            in_specs=[pl.BlockSpec((1, window), index_map=lambda i: (0, i))],
            out_specs=[pl.BlockSpec((window, d), index_map=lambda i: (i, 0))],
            core_axis_name=("core", "subcore"),
            dimension_semantics=(pltpu.PARALLEL,),
        )(idx_hbm, out_hbm)

    return gather(table, idx.reshape(1, n))


def _scatter_rows(table, dest, n_out, window=128):
    split, n_table, width = table.shape
    n = dest.shape[0]
    flat_dest = (dest[None, :] + n_out * jnp.arange(split, dtype=jnp.int32)[:, None]).reshape(1, split * n)
    src_blocks, blocks = n_table // window, n // window
    mesh = plsc.VectorSubcoreMesh(core_axis_name="core", subcore_axis_name="subcore")

    @functools.partial(pl.kernel, out_type=jax.ShapeDtypeStruct((split * n_out, width), table.dtype), mesh=mesh,
                       name="scatter_rows")
    def scatter(table_hbm, dest_hbm, out_hbm):
        def body(rows_vmem, dest_vmem):
            pltpu.sync_copy(rows_vmem, out_hbm.at[dest_vmem.at[0]])

        pltpu.emit_pipeline(
            body,
            grid=(split * blocks,),
            in_specs=[pl.BlockSpec((window, width),
                                   index_map=lambda i: ((i // blocks) * src_blocks + (i % blocks) % src_blocks, 0)),
                      pl.BlockSpec((1, window), index_map=lambda i: (0, i))],
            out_specs=[],
            core_axis_name=("core", "subcore"),
            dimension_semantics=(pltpu.PARALLEL,),
        )(table_hbm, dest_hbm)

    return scatter(table.reshape(split * n_table, width), flat_dest).reshape(split, n_out, width)


def _expert_kernel(tf, ge_ref, nv_ref, x_ref, w1_ref, w3_ref, w2_ref, y_ref):
    g = pl.program_id(0)

    @pl.when(g < nv_ref[0])
    def _():
        x = jnp.concatenate([x_ref[j] for j in range(ROW_SPLIT)], axis=-1).astype(BF16)
        acc = jnp.zeros((MOE_ROWS, D_MODEL), F32)
        for lo in range(0, EXPERT_DIM, tf):
            a = jnp.dot(x, w1_ref[:, lo:lo + tf], preferred_element_type=F32)
            b = jnp.dot(x, w3_ref[:, lo:lo + tf], preferred_element_type=F32)
            acc = acc + jnp.dot((_silu(a) * b).astype(BF16), w2_ref[lo:lo + tf, :], preferred_element_type=F32)
        for j in range(ROW_SPLIT):
            y_ref[j] = acc[:, j * SUBROW:(j + 1) * SUBROW]

    @pl.when(g >= nv_ref[0])
    def _():
        y_ref[...] = jnp.zeros_like(y_ref)


def _experts(xg, group_expert, n_valid, w1, w3, w2, tf=512):
    n_rows = xg.shape[1]
    rows = pl.BlockSpec((ROW_SPLIT, MOE_ROWS, SUBROW), lambda g, ge, nv: (0, g, 0))
    return pl.pallas_call(
        functools.partial(_expert_kernel, tf),
        grid_spec=pltpu.PrefetchScalarGridSpec(
            num_scalar_prefetch=2,
            grid=(n_rows // MOE_ROWS,),
            in_specs=[rows,
                      pl.BlockSpec((None, D_MODEL, EXPERT_DIM), lambda g, ge, nv: (ge[g], 0, 0)),
                      pl.BlockSpec((None, D_MODEL, EXPERT_DIM), lambda g, ge, nv: (ge[g], 0, 0)),
                      pl.BlockSpec((None, EXPERT_DIM, D_MODEL), lambda g, ge, nv: (ge[g], 0, 0))],
            out_specs=rows,
        ),
        out_shape=jax.ShapeDtypeStruct((ROW_SPLIT, n_rows, SUBROW), F32),
        compiler_params=_params("arbitrary", vmem_limit=BIG_VMEM_LIMIT),
        name="experts",
    )(group_expert, n_valid, xg, w1, w3, w2)


def _cast_kernel(w_ref, o_ref):
    o_ref[...] = w_ref[...].astype(o_ref.dtype)


def _to_bf16(w):
    n, rows, cols = w.shape
    blk = pl.BlockSpec((None, rows // 2, cols), lambda i, j: (i, j, 0))
    return pl.pallas_call(
        _cast_kernel,
        grid=(n, 2),
        in_specs=[blk],
        out_specs=blk,
        out_shape=jax.ShapeDtypeStruct(w.shape, BF16),
        compiler_params=_params("parallel", "parallel"),
        name="cast_bf16",
    )(w)


def _combine_kernel(x1_ref, y0_ref, y1_ref, gate_ref, o_ref):
    gate = gate_ref[...]
    for j in range(ROW_SPLIT):
        cols = slice(j * SUBROW, (j + 1) * SUBROW)
        o_ref[:, cols] = x1_ref[:, cols] + gate[:, 0:1] * y0_ref[j] + gate[:, 1:2] * y1_ref[j]


def _combine(x1, yg, gate_t, tm=512):
    n_tok = x1.shape[0]
    nb = n_tok // tm
    return pl.pallas_call(
        _combine_kernel,
        grid=(nb,),
        in_specs=[pl.BlockSpec((tm, D_MODEL), lambda i: (i, 0)),
                  pl.BlockSpec((ROW_SPLIT, tm, SUBROW), lambda i: (0, i, 0)),
                  pl.BlockSpec((ROW_SPLIT, tm, SUBROW), lambda i: (0, i + nb, 0)),
                  pl.BlockSpec((tm, 2), lambda i: (i, 0))],
        out_specs=pl.BlockSpec((tm, D_MODEL), lambda i: (i, 0)),
        out_shape=jax.ShapeDtypeStruct((n_tok, D_MODEL), F32),
        compiler_params=_params("parallel"),
        name="moe_combine",
    )(x1, yg, yg, gate_t)


def _moe(x1, routing, w1, w3, w2):
    n_tok = x1.shape[0]
    idx, gate, rank, cnt, hf = routing
    counts = cnt[:, 0].astype(jnp.int32)
    padded = (counts + MOE_ROWS - 1) // MOE_ROWS * MOE_ROWS
    end_padded = jnp.cumsum(padded)
    start_padded = end_padded - padded
    start = sum(jnp.where(idx == e, start_padded[e], 0) for e in range(N_EXPERTS))
    dest = (start + rank).reshape(-1)
    n_groups = (n_tok * 2 + MOE_ROWS - 1) // MOE_ROWS + N_EXPERTS
    n_rows = n_groups * MOE_ROWS
    group_row = jnp.arange(n_groups, dtype=jnp.int32)[:, None] * MOE_ROWS
    group_expert = jnp.minimum(jnp.sum(group_row >= end_padded[None, :], axis=1), N_EXPERTS - 1).astype(jnp.int32)
    n_valid = (end_padded[-1:] // MOE_ROWS).astype(jnp.int32)
    fill = jnp.arange(MOE_ROWS, dtype=jnp.int32)[None, :]
    pad_dest = jnp.where(fill < (padded - counts)[:, None], (start_padded + counts)[:, None] + fill,
                         n_rows - MOE_ROWS + fill).reshape(-1)
    xg = _scatter_rows(hf, jnp.concatenate([dest, pad_dest]), n_rows)
    y = _experts(xg, group_expert, n_valid, w1, w3, w2)
    return _combine(x1, _gather_rows(y, dest), gate.T)


def _row(v, pad=0):
    v = v.reshape(1, -1).astype(F32)
    return jnp.pad(v, ((0, 0), (0, pad))) if pad else v


def _rows_at(w, start, total):
    return jnp.pad(w.astype(F32), ((start, total - start - w.shape[0]), (0, 0)))


def kernel(x, w_in, norm_mix, rwkv_mu, rwkv_decay_up, rwkv_w0, rwkv_a_up, rwkv_a0, rwkv_gate_up, rwkv_k_k, rwkv_k_a, rwkv_r_k, rwkv_ln_w, rwkv_ln_b, vres_down, vres_up, vres_v0, ssm_conv_w, ssm_conv_b, ssm_dt_bias, ssm_a_log, ssm_d, ssm_norm_w, att_q_gain, att_k_gain, att_rel_bias, gla_gate_up, gla_gate_bias, gla_norm_w, w_branch, w_out, norm_ffn, ffn_w1, ffn_w3, ffn_w2, moe_router, moe_w1, moe_w3, moe_w2):
    bsz, seq, _ = x.shape
    n_tok = bsz * seq
    depth = w_in.shape[0]
    x2 = x.reshape(n_tok, D_MODEL)
    v_first = None
    o_rw, o_ss, o_at = RWKV_COLS, RWKV_COLS + SSM_COLS, RWKV_COLS + SSM_COLS + ATT_COLS
    o_gl = o_at + GLA_COLS
    for l in range(depth):
        w = w_in[l]
        zeros = lambda n: jnp.zeros((D_MODEL, n), BF16)
        cols = lambda lo, hi: w[:, lo:hi].astype(BF16)
        vd = vres_down[l - 1].astype(BF16) if l > 0 else zeros(32)
        w_mix = jnp.concatenate(
            [cols(0, o_rw), vd, zeros(RWKV_PAD - RWKV_COLS - 32),
             cols(o_rw, o_ss), zeros(SSM_PAD - SSM_COLS),
             cols(o_ss, o_at),
             cols(o_at, o_at + 1024), cols(o_at + 1040, o_gl), cols(o_at + 1024, o_at + 1040),
             zeros(GLA_PAD - GLA_COLS)], axis=1)
        wg = w[:, o_gl:]
        h, za, zb, zc, zd = _inproj(x2, _row(norm_mix[l]), w_mix)
        shp = lambda z: z.reshape(bsz, seq, z.shape[-1])

        rw = dict(mu=_row(rwkv_mu[l], RWKV_PAD - RWKV_COLS),
                  dup=_rows_at(rwkv_decay_up[l], 0, LANES), w0=_row(rwkv_w0[l]),
                  aup=_rows_at(rwkv_a_up[l], 64, LANES), a0=_row(rwkv_a0[l]),
                  gup=_rows_at(rwkv_gate_up[l], 0, 256).astype(BF16),
                  kk=_row(rwkv_k_k[l]), ka=_row(rwkv_k_a[l]), rk=_row(rwkv_r_k[l]),
                  lnw=_row(rwkv_ln_w[l]), lnb=_row(rwkv_ln_b[l]))
        if l > 0:
            rw.update(vup=_rows_at(vres_up[l - 1], 160, 256), v0=_row(vres_v0[l - 1]))
        ssm = dict(cw=ssm_conv_w[l], cb=_row(ssm_conv_b[l]), dtb=_row(ssm_dt_bias[l], LANES - SSM_HEADS),
                   alog=_row(ssm_a_log[l], LANES - SSM_HEADS), dexp=_row(jnp.repeat(ssm_d[l], CHUNK)),
                   nw=_row(ssm_norm_w[l]))
        gla = dict(gup=_rows_at(gla_gate_up[l], 0, LANES), gb=_row(gla_gate_bias[l]), nw=_row(gla_norm_w[l]))
        (o_c,), (o_d,), (o_b,), res_a = _run_parts(
            [_attention_part(shp(zc), att_q_gain[l], att_k_gain[l], att_rel_bias, MIXER_BLOCK),
             _gla_part(shp(zd), gla, MIXER_BLOCK), _mamba_part(shp(zb), ssm, MIXER_BLOCK),
             _rwkv_part(shp(za), v_first, rw, MIXER_BLOCK)],
            grid=(bsz, seq // MIXER_BLOCK), name="mixers")
        o_a = res_a[0]
        if l == 0:
            v_first = res_a[1]
        outs = [o.reshape(n_tok, BRANCH_DIM) for o in (o_a, o_b, o_c, o_d)]
        merge_w = (wg.astype(BF16), w_branch[l].astype(BF16), w_out[l].astype(BF16), _row(norm_ffn[l]))
        if l % 2 == 0:
            x1, hf = _merge(x2, h, outs, *merge_w)
            x2 = _ffn(x1, hf, ffn_w1[l // 2].astype(BF16), ffn_w3[l // 2].astype(BF16),
                      ffn_w2[l // 2].astype(BF16))
        else:
            x1, *routing = _merge(x2, h, outs, *merge_w, wr_t=moe_router[l // 2].T)
            x2 = _moe(x1, routing, _to_bf16(moe_w1[l // 2]), _to_bf16(moe_w3[l // 2]), _to_bf16(moe_w2[l // 2]))
    return x2.reshape(bsz, seq, D_MODEL)
```

```python
import functools
from typing import Any, NamedTuple

import jax
import jax.numpy as jnp
from jax import lax
from jax.experimental import pallas as pl
from jax.experimental.pallas import tpu as pltpu
from jax.experimental.pallas import tpu_sc as plsc

F32 = jnp.float32
BF16 = jnp.bfloat16

D_MODEL = 1024
CHUNK = 64
BRANCH_DIM = 512
NORM_EPS = 1e-6
LANES = 128
VMEM_LIMIT = 56 * 1024 * 1024
BIG_VMEM_LIMIT = 61 * 1024 * 1024

LOG2_E = 1.4426950408889634

RWKV_LN_EPS = 64e-5
RWKV_DECAY_SCALE = 0.6065306597126334
RWKV_COLS = 1824
RWKV_PAD = 1920
SSM_COLS = 1544
SSM_PAD = 1664
SSM_HEADS = 8
SSM_STATE = 128
ATT_COLS = 1536
ATT_HEADS = 8
ATT_LEFT = 8 * CHUNK
REL_CLIP = 2 * CHUNK
GLA_COLS = 1552
GLA_PAD = 1664
GLA_GATE_NORM = 16.0
FFN_DIM = 2816
N_EXPERTS = 8
EXPERT_DIM = 3584
MOE_ROWS = 256
ROW_SPLIT = 2
SUBROW = D_MODEL // (2 * ROW_SPLIT)


def _dot(a, b):
    return jnp.dot(a.astype(BF16), b.astype(BF16), preferred_element_type=F32)


def _dot_nt(a, b):
    return lax.dot_general(a.astype(BF16), b.astype(BF16), (((1,), (1,)), ((), ())),
                           preferred_element_type=F32)


def _dot_tn(a, b):
    return lax.dot_general(a.astype(BF16), b.astype(BF16), (((0,), (0,)), ((), ())),
                           preferred_element_type=F32)


def _hi_lo(a):
    hi = a.astype(BF16)
    return hi, (a - hi.astype(F32)).astype(BF16)


def _dot_3x(a, b):
    a_hi, a_lo = _hi_lo(a)
    b_hi, b_lo = _hi_lo(b)
    return (jnp.dot(a_hi, b_hi, preferred_element_type=F32) + jnp.dot(a_lo, b_hi, preferred_element_type=F32)
            + jnp.dot(a_hi, b_lo, preferred_element_type=F32))


def _dot_split(a, m):
    m = m.astype(BF16)
    hi, lo = _hi_lo(a)
    return jnp.dot(hi, m, preferred_element_type=F32) + jnp.dot(lo, m, preferred_element_type=F32)


def _dot_split_l(m, a):
    m = m.astype(BF16)
    hi, lo = _hi_lo(a)
    return jnp.dot(m, hi, preferred_element_type=F32) + jnp.dot(m, lo, preferred_element_type=F32)


def _pack_rows(x):
    bits = pltpu.bitcast(x.astype(BF16).astype(F32), jnp.uint32)
    half = D_MODEL // 2
    return [(bits[:, p * SUBROW:(p + 1) * SUBROW] & jnp.uint32(0xFFFF0000))
            | (bits[:, half + p * SUBROW:half + (p + 1) * SUBROW] >> jnp.uint32(16)) for p in range(ROW_SPLIT)]


def _unpack_rows(planes):
    hi = [pltpu.bitcast(w & jnp.uint32(0xFFFF0000), F32) for w in planes]
    lo = [pltpu.bitcast(w << jnp.uint32(16), F32) for w in planes]
    return jnp.concatenate(hi + lo, axis=-1)


def _softplus(x):
    return jnp.maximum(x, 0.0) + jnp.log(1.0 + jnp.exp(-jnp.abs(x)))


def _sigmoid(x):
    return 1.0 / (1.0 + jnp.exp(-x))


def _silu(x):
    return x * _sigmoid(x)


def _iota2(shape, axis):
    return lax.broadcasted_iota(jnp.int32, shape, axis)


def _group_mean_matrix(n, group):
    r = _iota2((n, n), 0) // group
    c = _iota2((n, n), 1) // group
    return jnp.where(r == c, 1.0 / group, 0.0).astype(F32)


def _chunk_tri(n):
    r = _iota2((n, n), 0)
    c = _iota2((n, n), 1)
    return jnp.where((r // CHUNK == c // CHUNK) & (r >= c), 1.0, 0.0).astype(F32)


def _resident(shape):
    nd = len(shape)
    return pl.BlockSpec(shape, lambda *_: (0,) * nd, pipeline_mode=pl.Buffered(1))


def _params(*sem, vmem_limit=VMEM_LIMIT):
    return pltpu.CompilerParams(dimension_semantics=sem, vmem_limit_bytes=vmem_limit)


_DONE = object()


class _Part(NamedTuple):
    body: Any
    inputs: list
    in_specs: list
    out_shapes: list
    out_specs: list
    scratch_shapes: list


def _run_parts(parts, grid, name):
    def split(refs, counts):
        out, lo = [], 0
        for n in counts:
            out.append(refs[lo:lo + n])
            lo += n
        return out

    n_in = [len(p.inputs) for p in parts]
    n_out = [len(p.out_shapes) for p in parts]
    n_scr = [len(p.scratch_shapes) for p in parts]

    def kernel(*refs):
        ins = split(refs[:sum(n_in)], n_in)
        outs = split(refs[sum(n_in):sum(n_in) + sum(n_out)], n_out)
        scr = split(refs[sum(n_in) + sum(n_out):], n_scr)
        stages = [part.body(*i, *o, *c) for part, i, o, c in zip(parts, ins, outs, scr)]
        while stages:
            for stage in list(stages):
                if next(stage, _DONE) is _DONE:
                    stages.remove(stage)

    flat = lambda field: [x for p in parts for x in getattr(p, field)]
    res = pl.pallas_call(
        kernel,
        grid=grid,
        in_specs=flat("in_specs"),
        out_specs=flat("out_specs"),
        out_shape=flat("out_shapes"),
        scratch_shapes=flat("scratch_shapes"),
        compiler_params=_params("parallel", "arbitrary", vmem_limit=BIG_VMEM_LIMIT),
        name=name,
    )(*flat("inputs"))
    return split(list(res), n_out)


MIXER_COLS = (RWKV_PAD, SSM_PAD, ATT_COLS, GLA_PAD)
MIXER_BLOCK = 256


def _inproj_kernel(x_ref, g_ref, w_ref, h_ref, *z_refs):
    x = x_ref[...]
    ms = jnp.mean(x * x, axis=-1, keepdims=True)
    h = (x * lax.rsqrt(ms + NORM_EPS) * g_ref[...]).astype(BF16)
    h_ref[...] = h
    lo = 0
    for z_ref, n in zip(z_refs, MIXER_COLS):
        z_ref[...] = jnp.dot(h, w_ref[:, lo:lo + n], preferred_element_type=F32)
        lo += n


def _inproj(x2, g, w_mix, tm=512):
    n_tok = x2.shape[0]
    row = lambda n: pl.BlockSpec((tm, n), lambda i: (i, 0))
    return pl.pallas_call(
        _inproj_kernel,
        grid=(n_tok // tm,),
        in_specs=[row(D_MODEL), _resident((1, D_MODEL)), _resident(w_mix.shape)],
        out_specs=[row(D_MODEL)] + [row(n) for n in MIXER_COLS],
        out_shape=[jax.ShapeDtypeStruct((n_tok, D_MODEL), BF16)]
        + [jax.ShapeDtypeStruct((n_tok, n), F32) for n in MIXER_COLS],
        compiler_params=_params("parallel"),
        name="inproj",
    )(x2, g, w_mix)


def _rwkv_block(r, k, v, kk, ka, ld, tb, state_ref, y_ref):
    n_chunks = tb // CHUNK
    cum = _dot_split_l(_chunk_tri(tb), ld)
    p_inv = jnp.exp(-cum)
    a_t = -kk * jnp.exp(cum - ld)
    b_t = kk * ka * p_inv
    k_t = k * p_inv
    r_t = r * jnp.exp(cum)
    yield

    lane_head = _iota2((CHUNK, LANES), 1) // CHUNK
    row = _iota2((CHUNK, LANES), 0)
    col = _iota2((CHUNK, LANES), 1) % CHUNK
    strict = row > col
    incl = row >= col
    bd_mask = (_iota2((LANES, LANES), 0) // CHUNK) == (_iota2((LANES, LANES), 1) // CHUNK)
    eye = jnp.where(_iota2((LANES, LANES), 0) == _iota2((LANES, LANES), 1), 1.0, 0.0).astype(F32)

    def stack(z):
        return jnp.concatenate([jnp.where(lane_head == 0, z, 0.0), jnp.where(lane_head == 1, z, 0.0)], axis=0)

    def sub(z, u):
        c, p = u
        return z[c * CHUNK:(c + 1) * CHUNK, p * LANES:(p + 1) * LANES]

    bf = lambda z: z.astype(BF16)

    def prepare(units):
        gram = {u: _dot_nt(jnp.concatenate([sub(a_t, u), sub(r_t, u)], axis=0),
                           jnp.concatenate([stack(sub(b_t, u)), stack(sub(k_t, u))], axis=0))
                for u in units}
        yield
        a_rb = {u: bf(jnp.where(incl, gram[u][CHUNK:, :LANES], 0.0)) for u in units}
        a_akrk = {u: bf(jnp.concatenate([jnp.where(strict, gram[u][:CHUNK, LANES:], 0.0),
                                         jnp.where(incl, gram[u][CHUNK:, LANES:], 0.0)], axis=0)) for u in units}
        pw = {u: stack(jnp.where(strict, gram[u][:CHUNK, :LANES], 0.0)) for u in units}
        t_bd = {u: eye + pw[u] for u in units}
        pw = {u: _dot(pw[u], pw[u]) for u in units}
        yield
        for _ in range(CHUNK.bit_length() - 3):
            sp = {u: _dot(jnp.concatenate([t_bd[u], pw[u]], axis=0), pw[u]) for u in units}
            t_bd = {u: t_bd[u] + sp[u][:LANES] for u in units}
            pw = {u: sp[u][LANES:] for u in units}
            yield
        t_bd = {u: bf(t_bd[u] + _dot(t_bd[u], pw[u])) for u in units}
        v_bd = {u: bf(stack(sub(v, u))) for u in units}
        avyv = {u: _dot(a_akrk[u], v_bd[u]) for u in units}
        yield
        wuv = {u: _dot(t_bd[u], jnp.concatenate([stack(sub(a_t, u)), stack(avyv[u][:CHUNK])], axis=1))
               for u in units}
        wr = {u: bf(jnp.concatenate([wuv[u][:, :LANES], sub(r_t, u)], axis=0)) for u in units}
        yield
        return a_rb, avyv, wuv, wr

    a_rb, avyv, wuv, wr = yield from prepare([(c, p) for c in range(n_chunks) for p in range(4)])

    h = [state_ref[p] for p in range(4)]
    for c in range(n_chunks):
        last = c * CHUNK + CHUNK - 1
        p_last = jnp.exp(cum[last:last + 1, :])
        hs = [_dot_nt(wr[(c, p)], h[p]) for p in range(4)]
        u2 = [hs[p][:LANES] + wuv[(c, p)][:, LANES:] for p in range(4)]
        u_p = [u2[p][:CHUNK] + u2[p][CHUNK:] for p in range(4)]
        upd = [_dot_tn(jnp.concatenate([u_p[p], sub(v, (c, p))], axis=0),
                       jnp.concatenate([sub(b_t, (c, p)), sub(k_t, (c, p))], axis=0)
                       * p_last[:, p * LANES:(p + 1) * LANES]) for p in range(4)]
        for p in range(4):
            y_ref[c * CHUNK:(c + 1) * CHUNK, p * LANES:(p + 1) * LANES] = (
                hs[p][LANES:] + _dot(a_rb[(c, p)], stack(u_p[p])) + avyv[(c, p)][CHUNK:])
        h = [h[p] * p_last[:, p * LANES:(p + 1) * LANES] + jnp.where(bd_mask, upd[p], 0.0) for p in range(4)]
        yield
    for p in range(4):
        state_ref[p] = h[p]


def _rwkv_kernel(has_vres, tb, *refs):
    if has_vres:
        (za_ref, vfirst_ref, mu_ref, dup_ref, w0_ref, aup_ref, a0_ref, gup_ref, vup_ref, v0_ref,
         kk_ref, ka_ref, rk_ref, lnw_ref, lnb_ref, o_ref, prev_ref, state_ref, y_ref) = refs
    else:
        (za_ref, mu_ref, dup_ref, w0_ref, aup_ref, a0_ref, gup_ref,
         kk_ref, ka_ref, rk_ref, lnw_ref, lnb_ref, o_ref, vraw_ref, prev_ref, state_ref, y_ref) = refs

    @pl.when(pl.program_id(1) == 0)
    def _():
        prev_ref[...] = jnp.zeros_like(prev_ref)
        state_ref[...] = jnp.zeros_like(state_ref)

    yield
    za = za_ref[...]
    shifted = pltpu.roll(za, 1, axis=0)
    shifted = jnp.where(_iota2(za.shape, 0) == 0, prev_ref[7:8, :], shifted)
    prev_ref[...] = za[tb - 8:, :]
    za = za + (shifted - za) * mu_ref[...]

    r = za[:, 0:512]
    k = za[:, 512:1024]
    v = za[:, 1024:1536]
    xwa = za[:, 1536:1664]
    xg = za[:, 1664:1920]
    ld = -RWKV_DECAY_SCALE * _sigmoid(w0_ref[...] + _dot_3x(jnp.tanh(xwa), dup_ref[...]))
    a = _sigmoid(a0_ref[...] + _dot(xwa, aup_ref[...]))
    g = _dot(_sigmoid(xg), gup_ref[...])
    if has_vres:
        v_mix = _sigmoid(v0_ref[...] + _dot(xg, vup_ref[...]))
        v = v + (vfirst_ref[...] - v) * v_mix
    else:
        vraw_ref[...] = v
    kk = k * kk_ref[...]
    k = k * (1.0 + (a - 1.0) * ka_ref[...])
    head_sum = _group_mean_matrix(512, CHUNK) * float(CHUNK)
    kk = kk / jnp.maximum(jnp.sqrt(_dot(kk * kk, head_sum)), 1e-12)
    yield

    yield from _rwkv_block(r, k, v, kk, a, ld, tb, state_ref, y_ref)
    y = y_ref[...]
    head_mean = _group_mean_matrix(512, CHUNK)
    mean = _dot_split(y, head_mean)
    yc = y - mean
    var = _dot(yc * yc, head_mean)
    y = yc * lax.rsqrt(var + RWKV_LN_EPS) * lnw_ref[...] + lnb_ref[...]
    y = y + _dot(r * k * rk_ref[...], head_sum) * v
    o_ref[...] = (y * g).astype(o_ref.dtype)


def _rwkv_part(za, v_first, prm, tb):
    bsz, seq, _ = za.shape
    has_vres = v_first is not None
    blk = lambda n: pl.BlockSpec((None, tb, n), lambda b, i: (b, i, 0))
    names = (["mu", "dup", "w0", "aup", "a0", "gup"] + (["vup", "v0"] if has_vres else [])
             + ["kk", "ka", "rk", "lnw", "lnb"])
    weights = [prm[n] for n in names]
    ins = [za] + ([v_first] if has_vres else []) + weights
    in_specs = [blk(RWKV_PAD)] + ([blk(512)] if has_vres else []) + [_resident(w.shape) for w in weights]
    out_shape = [jax.ShapeDtypeStruct((bsz, seq, 512), BF16)]
    out_specs = [blk(512)]
    if not has_vres:
        out_shape.append(jax.ShapeDtypeStruct((bsz, seq, 512), F32))
        out_specs.append(blk(512))
    return _Part(functools.partial(_rwkv_kernel, has_vres, tb), ins, in_specs, out_shape, out_specs,
                 [pltpu.VMEM((8, RWKV_PAD), F32), pltpu.VMEM((4, LANES, LANES), F32), pltpu.VMEM((tb, 512), F32)])


def _mamba_kernel(tb, zb_ref, cw_ref, cb_ref, dtb_ref, alog_ref, dexp_ref, nw_ref,
                  o_ref, xbuf_ref, state_ref, y_ref):
    @pl.when(pl.program_id(1) == 0)
    def _():
        xbuf_ref[0:8, :] = jnp.zeros((8, 1024), F32)
        state_ref[...] = jnp.zeros_like(state_ref)

    yield
    xbuf_ref[8:8 + tb, :] = zb_ref[:, 512:1536]
    conv = cb_ref[...]
    for i in range(4):
        conv = conv + cw_ref[i:i + 1, :] * xbuf_ref[5 + i:5 + i + tb, :]
    xbuf_ref[0:8, :] = xbuf_ref[tb:tb + 8, :]
    xbc = _silu(conv)
    xs = xbc[:, 0:512]
    bm = xbc[:, 512:768]
    cm = xbc[:, 768:1024]
    yield

    lane = _iota2((1, LANES), 1)
    dt = _softplus(zb_ref[:, 1536:1664] + dtb_ref[...])
    a_neg = jnp.where(lane < SSM_HEADS, -jnp.exp(alog_ref[...]), 0.0)
    acs = _dot_split_l(_chunk_tri(tb), dt * a_neg)
    expand = jnp.where(_iota2((LANES, 512), 0) == _iota2((LANES, 512), 1) // CHUNK, 1.0, 0.0)
    dt_e = _dot(dt, expand)
    acs_e = _dot_split(acs, expand)
    xdt = xs * dt_e
    yield

    n_chunks = tb // CHUNK
    causal = _iota2((CHUNK, 512), 0) >= _iota2((CHUNK, 512), 1) % CHUNK
    pair_head = _iota2((CHUNK, LANES), 1) // CHUNK
    spread = jnp.where(_iota2((CHUNK, 512), 0) == _iota2((CHUNK, 512), 1) % CHUNK, 1.0, 0.0)
    own_head = _iota2((SSM_HEADS, 512), 0) == _iota2((SSM_HEADS, 512), 1) // CHUNK
    rows_of = lambda c: slice(c * CHUNK, (c + 1) * CHUNK)
    grp = lambda g: slice(g * SSM_STATE, (g + 1) * SSM_STATE)

    def stack(z):
        return jnp.concatenate([jnp.where(pair_head == 0, z, 0.0), jnp.where(pair_head == 1, z, 0.0)], axis=0)

    y_intra, upd, e_in, e_last = [], [], [], []
    for c in range(n_chunks):
        rows = rows_of(c)
        acs_ec = acs_e[rows]
        acs_t = acs[rows].T[0:SSM_HEADS]
        acs_row = jnp.sum(jnp.where(own_head, _dot_split(acs_t, spread), 0.0), axis=0, keepdims=True)
        decay = jnp.exp(jnp.where(causal, acs_ec - acs_row, -jnp.inf))
        cb = jnp.concatenate(
            [_dot_nt(cm[rows, grp(g)], jnp.concatenate([bm[rows, grp(g)]] * 4, axis=0)) for g in range(2)], axis=1)
        m = cb * decay
        yield
        xdt_c = xdt[rows]
        y_intra.append([_dot(m[:, p * LANES:(p + 1) * LANES], stack(xdt_c[:, p * LANES:(p + 1) * LANES]))
                        for p in range(4)])
        last_e = acs_ec[CHUNK - 1:CHUNK, :]
        x_out = xdt_c * jnp.exp(last_e - acs_ec)
        upd.append([_dot_tn(bm[rows, grp(g)], x_out[:, g * 256:(g + 1) * 256]) for g in range(2)])
        e_in.append(jnp.exp(acs_ec))
        e_last.append(jnp.exp(last_e))
        yield

    state = [state_ref[g] for g in range(2)]
    for c in range(n_chunks):
        rows = rows_of(c)
        for g in range(2):
            ls = slice(g * 256, (g + 1) * 256)
            y_in = jnp.concatenate(y_intra[c][2 * g:2 * g + 2], axis=1)
            y_ref[rows, ls] = y_in + _dot(cm[rows, grp(g)], state[g]) * e_in[c][:, ls]
            state[g] = state[g] * e_last[c][:, ls] + upd[c][g]
        yield
    for g in range(2):
        state_ref[g] = state[g]

    y = (y_ref[...] + xs * dexp_ref[...]) * _silu(zb_ref[:, 0:512])
    for g in range(2):
        ls = slice(g * 256, (g + 1) * 256)
        yg = y[:, ls]
        ms = jnp.mean(yg * yg, axis=-1, keepdims=True)
        o_ref[:, ls] = (yg * lax.rsqrt(ms + NORM_EPS) * nw_ref[:, ls]).astype(o_ref.dtype)


def _mamba_part(zb, prm, tb):
    bsz, seq, _ = zb.shape
    blk = lambda n: pl.BlockSpec((None, tb, n), lambda b, i: (b, i, 0))
    weights = [prm[n] for n in ("cw", "cb", "dtb", "alog", "dexp", "nw")]
    return _Part(functools.partial(_mamba_kernel, tb), [zb] + weights,
                 [blk(SSM_PAD)] + [_resident(w.shape) for w in weights],
                 [jax.ShapeDtypeStruct((bsz, seq, 512), BF16)], [blk(512)],
                 [pltpu.VMEM((tb + 8, 1024), F32), pltpu.VMEM((2, SSM_STATE, 256), F32), pltpu.VMEM((tb, 512), F32)])


def _attn_kernel(tq, seq, q_ref, k_ref, v_ref, qg_ref, kg_ref, bias_ref, o_ref, kn_ref, vb_ref):
    i = pl.program_id(1)
    head_mean = _group_mean_matrix(512, CHUNK)
    win = tq + ATT_LEFT

    @pl.when(i == 0)
    def _():
        kn_ref[0:ATT_LEFT, :] = jnp.zeros((ATT_LEFT, 512), BF16)
        vb_ref[0:ATT_LEFT, :] = jnp.zeros((ATT_LEFT, 512), BF16)
        for j in range(seq // tq):
            kj = k_ref[j * tq:(j + 1) * tq, :]
            ms = _dot(kj * kj, head_mean)
            kn_ref[ATT_LEFT + j * tq:ATT_LEFT + (j + 1) * tq, :] = (
                kj * lax.rsqrt(ms + NORM_EPS) * kg_ref[...]).astype(BF16)
            vb_ref[ATT_LEFT + j * tq:ATT_LEFT + (j + 1) * tq, :] = v_ref[j * tq:(j + 1) * tq, :].astype(BF16)

    yield
    q = q_ref[...]
    ms = _dot(q * q, head_mean)
    qn = q * lax.rsqrt(ms + NORM_EPS) * qg_ref[...] * (CHUNK ** -0.5 * LOG2_E)
    start = pl.multiple_of(i * tq, tq)
    kwin = kn_ref[pl.ds(start, win), :]
    vwin = vb_ref[pl.ds(start, win), :]
    lane_head = _iota2((tq, LANES), 1) // CHUNK
    yield
    for p in range(4):
        ls = slice(p * LANES, (p + 1) * LANES)
        q2 = jnp.concatenate([jnp.where(lane_head == s, qn[:, ls], 0.0) for s in range(2)], axis=0)
        sc = _dot_nt(q2, kwin[:, ls]) + bias_ref[2 * p:2 * p + 2].reshape(2 * tq, win).astype(F32)
        yield
        e = jnp.exp2(sc - jnp.max(sc, axis=-1, keepdims=True))
        yield
        o2 = _dot(e, vwin[:, ls]) / jnp.sum(e, axis=-1, keepdims=True)
        o_ref[:, ls] = jnp.where(lane_head == 0, o2[:tq], o2[tq:]).astype(o_ref.dtype)
        yield


def _band_bias(rel_bias, tq, win):
    period = tq + win
    m = jnp.arange(period)
    d = jnp.where(m < win, m, m - period) - ATT_LEFT
    f = (rel_bias.astype(F32)[:, jnp.clip(d, -REL_CLIP, REL_CLIP) + REL_CLIP] * LOG2_E).astype(BF16)
    g = jnp.tile(f, (1, tq))[:, :tq * (period - 1)].reshape(-1, tq, period - 1)
    bias = g[:, :, :win]
    left = ATT_LEFT // CHUNK
    qc = left + jnp.arange(tq)[:, None] // CHUNK
    kc = jnp.arange(win)[None, :] // CHUNK
    tiles = []
    for blk in range(ATT_LEFT // tq + 1):
        first_kc = jnp.maximum(qc - left, left - blk * (tq // CHUNK))
        tiles.append(jnp.where((kc <= qc) & (kc >= first_kc), bias, -jnp.inf))
    return jnp.stack(tiles)


def _attention_part(zc, q_gain, k_gain, rel_bias, tq):
    bsz, seq, _ = zc.shape
    win = tq + ATT_LEFT
    bias = _band_bias(rel_bias, tq, win)
    qg = jnp.tile(q_gain, ATT_HEADS)[None, :]
    kg = jnp.tile(k_gain, ATT_HEADS)[None, :]
    full = lambda col: pl.BlockSpec((None, seq, 512), lambda b, i: (b, 0, col), pipeline_mode=pl.Buffered(1))
    last_tile = bias.shape[0] - 1
    bias_spec = pl.BlockSpec((None,) + bias.shape[1:], lambda b, i: (jnp.minimum(i, last_tile), 0, 0, 0))
    blk = pl.BlockSpec((None, tq, 512), lambda b, i: (b, i, 0))
    return _Part(functools.partial(_attn_kernel, tq, seq), [zc, zc, zc, qg, kg, bias],
                 [blk, full(1), full(2), _resident(qg.shape), _resident(kg.shape), bias_spec],
                 [jax.ShapeDtypeStruct((bsz, seq, 512), BF16)], [blk],
                 [pltpu.VMEM((ATT_LEFT + seq, 512), BF16), pltpu.VMEM((ATT_LEFT + seq, 512), BF16)])


def _gla_kernel(tb, zd_ref, gup_ref, gb_ref, nw_ref, o_ref, state_ref, y_ref):
    @pl.when(pl.program_id(1) == 0)
    def _():
        state_ref[...] = jnp.zeros_like(state_ref)

    yield
    q = zd_ref[:, 0:256] * (CHUNK ** -0.5)
    k = zd_ref[:, 256:512]
    v = zd_ref[:, 512:1024]
    log_a = -_softplus(-(_dot_3x(zd_ref[:, 1536:1664], gup_ref[...]) + gb_ref[...])) / GLA_GATE_NORM
    bcum = _dot_split_l(_chunk_tri(tb), log_a)

    k_head = _iota2((CHUNK, 256), 1) // CHUNK
    v_head = _iota2((CHUNK, 512), 1) // LANES
    causal = _iota2((CHUNK, 256), 0) >= _iota2((CHUNK, 256), 1) % CHUNK
    bd = (_iota2((512, 256), 0) // LANES) == (_iota2((512, 256), 1) // CHUNK)
    n_chunks = tb // CHUNK
    qg_all = (q * jnp.exp(bcum)).astype(BF16)
    kg_all = k * jnp.exp(-bcum)
    yield
    qg, att, o_intra, upd, e_last = [], [], [], [], []
    for c in range(n_chunks):
        rows = slice(c * CHUNK, (c + 1) * CHUNK)
        kg_bd = jnp.concatenate([jnp.where(k_head == h, kg_all[rows], 0.0) for h in range(4)], axis=0)
        qg.append(qg_all[rows])
        att.append(jnp.where(causal, _dot_nt(qg[c], kg_bd), 0.0))
        yield
    for c in range(n_chunks):
        rows = slice(c * CHUNK, (c + 1) * CHUNK)
        bc = bcum[rows]
        blast = bc[CHUNK - 1:CHUNK, :]
        v_c = v[rows]
        v_bd = jnp.concatenate([jnp.where(v_head == h, v_c, 0.0) for h in range(4)], axis=0)
        o_intra.append(_dot(att[c], v_bd))
        upd.append(jnp.where(bd, _dot_tn(v_c, k[rows] * jnp.exp(blast - bc)), 0.0))
        e_last.append(jnp.exp(blast))
        yield
    st = state_ref[...]
    for c in range(n_chunks):
        y_ref[c * CHUNK:(c + 1) * CHUNK, :] = o_intra[c] + _dot_nt(qg[c], st)
        st = st * e_last[c] + upd[c]
        yield
    state_ref[...] = st

    o = y_ref[...]
    for h in range(4):
        ls = slice(h * LANES, (h + 1) * LANES)
        oh = o[:, ls]
        ms = jnp.mean(oh * oh, axis=-1, keepdims=True)
        o_ref[:, ls] = (oh * lax.rsqrt(ms + NORM_EPS) * nw_ref[...]
                        * _silu(zd_ref[:, 1024 + h * LANES:1024 + (h + 1) * LANES])).astype(o_ref.dtype)


def _gla_part(zd, prm, tb):
    bsz, seq, _ = zd.shape
    blk = lambda n: pl.BlockSpec((None, tb, n), lambda b, i: (b, i, 0))
    weights = [prm[n] for n in ("gup", "gb", "nw")]
    return _Part(functools.partial(_gla_kernel, tb), [zd] + weights,
                 [blk(GLA_PAD)] + [_resident(w.shape) for w in weights],
                 [jax.ShapeDtypeStruct((bsz, seq, 512), BF16)], [blk(512)],
                 [pltpu.VMEM((512, 256), F32), pltpu.VMEM((tb, 512), F32)])


def _route(tm, hf, wr_ref, idx_ref, gate_ref, rank_ref, cnt_ref, hf_ref, carry_ref):
    @pl.when(pl.program_id(0) == 0)
    def _():
        carry_ref[...] = jnp.zeros_like(carry_ref)

    for j, words in enumerate(_pack_rows(hf)):
        hf_ref[j] = words
    nt = lambda a, b: lax.dot_general(a, b, (((1,), (1,)), ((), ())), preferred_element_type=F32)
    w_hi, w_lo = _hi_lo(wr_ref[...])
    h_hi, h_lo = _hi_lo(hf)
    logits = nt(w_hi, h_hi) + nt(w_lo, h_hi) + nt(w_hi, h_lo)
    e_iota = _iota2((N_EXPERTS, tm), 0)
    m1 = jnp.max(logits, axis=0, keepdims=True)
    i1 = jnp.min(jnp.where(logits == m1, e_iota, N_EXPERTS), axis=0, keepdims=True)
    rest = jnp.where(e_iota == i1, -jnp.inf, logits)
    m2 = jnp.max(rest, axis=0, keepdims=True)
    i2 = jnp.min(jnp.where(rest == m2, e_iota, N_EXPERTS), axis=0, keepdims=True)
    e2 = jnp.exp(m2 - m1)
    gate_ref[0:1, :] = 1.0 / (1.0 + e2)
    gate_ref[1:2, :] = e2 / (1.0 + e2)
    idx_ref[0:1, :] = i1
    idx_ref[1:2, :] = i2
    hit1 = jnp.where(e_iota == i1, 1.0, 0.0)
    hit2 = jnp.where(e_iota == i2, 1.0, 0.0)
    before = jnp.where(_iota2((tm, tm), 0) < _iota2((tm, tm), 1), 1.0, 0.0)
    prior = _dot(hit1 + hit2, before) + carry_ref[:, 0:1]
    rank_ref[0:1, :] = jnp.sum(hit1 * prior, axis=0, keepdims=True).astype(jnp.int32)
    rank_ref[1:2, :] = jnp.sum(hit2 * prior, axis=0, keepdims=True).astype(jnp.int32)
    carry_ref[...] = carry_ref[...] + jnp.sum(hit1 + hit2, axis=1, keepdims=True)
    cnt_ref[...] = carry_ref[...]


def _merge_kernel(tm, route, x_ref, h_ref, oa_ref, ob_ref, oc_ref, od_ref, wg_ref, wb_ref, wo_ref, nf_ref, *rest):
    h = h_ref[...]
    acc = jnp.zeros(x_ref.shape, F32)
    for i, o_ref in enumerate((oa_ref, ob_ref, oc_ref, od_ref)):
        gate = _sigmoid(jnp.dot(h, wg_ref[:, i * D_MODEL:(i + 1) * D_MODEL], preferred_element_type=F32))
        acc = acc + gate * jnp.dot(o_ref[...], wb_ref[i], preferred_element_type=F32)
    x1 = x_ref[...] + jnp.dot(acc.astype(BF16), wo_ref[...], preferred_element_type=F32)
    ms = jnp.mean(x1 * x1, axis=-1, keepdims=True)
    hf = x1 * lax.rsqrt(ms + NORM_EPS) * nf_ref[...]
    if route:
        wr_ref, x1_ref = rest[:2]
        _route(tm, hf, wr_ref, *rest[2:])
    else:
        x1_ref, hf_ref = rest
        hf_ref[...] = hf.astype(BF16)
    x1_ref[...] = x1


def _merge(x2, h, outs, wg, wb, wo, nf, wr_t=None, tm=512):
    n_tok = x2.shape[0]
    route = wr_t is not None
    row = lambda n: pl.BlockSpec((tm, n), lambda i: (i, 0))
    ins = [x2, h, *outs, wg, wb, wo, nf]
    in_specs = ([row(D_MODEL), row(D_MODEL)] + [row(BRANCH_DIM)] * 4
                + [_resident(wg.shape), _resident(wb.shape), _resident(wo.shape), _resident(nf.shape)])
    out_specs = [row(D_MODEL)]
    out_shape = [jax.ShapeDtypeStruct((n_tok, D_MODEL), F32)]
    scratch = []
    if route:
        ins.append(wr_t)
        in_specs.append(_resident(wr_t.shape))
        col = pl.BlockSpec((2, tm), lambda i: (0, i))
        out_specs += [col, col, col, pl.BlockSpec((N_EXPERTS, LANES), lambda i: (0, 0)),
                      pl.BlockSpec((ROW_SPLIT, tm, SUBROW), lambda i: (0, i, 0))]
        out_shape += [jax.ShapeDtypeStruct((2, n_tok), jnp.int32), jax.ShapeDtypeStruct((2, n_tok), F32),
                      jax.ShapeDtypeStruct((2, n_tok), jnp.int32), jax.ShapeDtypeStruct((N_EXPERTS, LANES), F32),
                      jax.ShapeDtypeStruct((ROW_SPLIT, n_tok, SUBROW), jnp.uint32)]
        scratch = [pltpu.VMEM((N_EXPERTS, LANES), F32)]
    else:
        out_specs.append(row(D_MODEL))
        out_shape.append(jax.ShapeDtypeStruct((n_tok, D_MODEL), BF16))
    return pl.pallas_call(
        functools.partial(_merge_kernel, tm, route),
        grid=(n_tok // tm,),
        in_specs=in_specs,
        out_specs=out_specs,
        out_shape=out_shape,
        scratch_shapes=scratch,
        compiler_params=_params("arbitrary" if route else "parallel"),
        name="merge",
    )(*ins)


def _ffn_kernel(tf, x1_ref, hf_ref, w1_ref, w3_ref, w2_ref, o_ref):
    hf = hf_ref[...]
    acc = x1_ref[...]
    for lo in range(0, FFN_DIM, tf):
        cols = slice(lo, min(lo + tf, FFN_DIM))
        a = jnp.dot(hf, w1_ref[:, cols], preferred_element_type=F32)
        b = jnp.dot(hf, w3_ref[:, cols], preferred_element_type=F32)
        acc = acc + jnp.dot((_silu(a) * b).astype(BF16), w2_ref[cols, :], preferred_element_type=F32)
    o_ref[...] = acc


def _ffn(x1, hf, w1, w3, w2, tm=512, tf=512):
    n_tok = x1.shape[0]
    row = pl.BlockSpec((tm, D_MODEL), lambda i: (i, 0))
    return pl.pallas_call(
        functools.partial(_ffn_kernel, tf),
        grid=(n_tok // tm,),
        in_specs=[row, row, _resident(w1.shape), _resident(w3.shape), _resident(w2.shape)],
        out_specs=row,
        out_shape=jax.ShapeDtypeStruct((n_tok, D_MODEL), F32),
        compiler_params=_params("parallel"),
        name="ffn",
    )(x1, hf, w1, w3, w2)


def _gather_rows(table, idx, window=128):
    split, n_table, width = table.shape
    n = idx.shape[0]
    flat_idx = (idx[None, :] + n_table * jnp.arange(split, dtype=jnp.int32)[:, None]).reshape(-1)
    return _gather_subrows(table.reshape(split * n_table, width), flat_idx, window).reshape(split, n, width)


def _gather_subrows(table, idx, window):
    n = idx.shape[0]
    d = table.shape[1]
    mesh = plsc.VectorSubcoreMesh(core_axis_name="core", subcore_axis_name="subcore")

    @functools.partial(pl.kernel, out_type=jax.ShapeDtypeStruct((n, d), table.dtype), mesh=mesh,
                       name="gather_rows")
    def gather(table_hbm, idx_hbm, out_hbm):
        def body(idx_vmem, out_vmem):
            pltpu.sync_copy(table_hbm.at[idx_vmem.at[0]], out_vmem)

        pltpu.emit_pipeline(
            body,
            grid=(n // window,),
            in_specs=[pl.BlockSpec((1, window), index_map=lambda i: (0, i))],
            out_specs=[pl.BlockSpec((window, d), index_map=lambda i: (i, 0))],
            core_axis_name=("core", "subcore"),
            dimension_semantics=(pltpu.PARALLEL,),
        )(idx_hbm, out_hbm)

    return gather(table, idx.reshape(1, n))


def _scatter_rows(table, dest, n_out, window=128):
    split, n_table, width = table.shape
    n = dest.shape[0]
    flat_dest = (dest[None, :] + n_out * jnp.arange(split, dtype=jnp.int32)[:, None]).reshape(1, split * n)
    src_blocks, blocks = n_table // window, n // window
    mesh = plsc.VectorSubcoreMesh(core_axis_name="core", subcore_axis_name="subcore")

    @functools.partial(pl.kernel, out_type=jax.ShapeDtypeStruct((split * n_out, width), table.dtype), mesh=mesh,
                       name="scatter_rows")
    def scatter(table_hbm, dest_hbm, out_hbm):
        def body(rows_vmem, dest_vmem):
            pltpu.sync_copy(rows_vmem, out_hbm.at[dest_vmem.at[0]])

        pltpu.emit_pipeline(
            body,
            grid=(split * blocks,),
            in_specs=[pl.BlockSpec((window, width),
                                   index_map=lambda i: ((i // blocks) * src_blocks + (i % blocks) % src_blocks, 0)),
                      pl.BlockSpec((1, window), index_map=lambda i: (0, i))],
            out_specs=[],
            core_axis_name=("core", "subcore"),
            dimension_semantics=(pltpu.PARALLEL,),
        )(table_hbm, dest_hbm)

    return scatter(table.reshape(split * n_table, width), flat_dest).reshape(split, n_out, width)


def _expert_kernel(tf, ge_ref, nv_ref, x_ref, w1_ref, w3_ref, w2_ref, y_ref):
    g = pl.program_id(0)

    @pl.when(g < nv_ref[0])
    def _():
        x = _unpack_rows([x_ref[j] for j in range(ROW_SPLIT)]).astype(BF16)
        acc = jnp.zeros((MOE_ROWS, D_MODEL), F32)
        for lo in range(0, EXPERT_DIM, tf):
            a = jnp.dot(x, w1_ref[:, lo:lo + tf], preferred_element_type=F32)
            b = jnp.dot(x, w3_ref[:, lo:lo + tf], preferred_element_type=F32)
            acc = acc + jnp.dot((_silu(a) * b).astype(BF16), w2_ref[lo:lo + tf, :], preferred_element_type=F32)
        for j, words in enumerate(_pack_rows(acc)):
            y_ref[j] = words

    @pl.when(g >= nv_ref[0])
    def _():
        y_ref[...] = jnp.zeros_like(y_ref)


def _experts(xg, group_expert, n_valid, w1, w3, w2, tf=512):
    n_rows = xg.shape[1]
    rows = pl.BlockSpec((ROW_SPLIT, MOE_ROWS, SUBROW), lambda g, ge, nv: (0, g, 0))
    return pl.pallas_call(
        functools.partial(_expert_kernel, tf),
        grid_spec=pltpu.PrefetchScalarGridSpec(
            num_scalar_prefetch=2,
            grid=(n_rows // MOE_ROWS,),
            in_specs=[rows,
                      pl.BlockSpec((None, D_MODEL, EXPERT_DIM), lambda g, ge, nv: (ge[g], 0, 0)),
                      pl.BlockSpec((None, D_MODEL, EXPERT_DIM), lambda g, ge, nv: (ge[g], 0, 0)),
                      pl.BlockSpec((None, EXPERT_DIM, D_MODEL), lambda g, ge, nv: (ge[g], 0, 0))],
            out_specs=rows,
        ),
        out_shape=jax.ShapeDtypeStruct((ROW_SPLIT, n_rows, SUBROW), jnp.uint32),
        compiler_params=_params("arbitrary", vmem_limit=BIG_VMEM_LIMIT),
        name="experts",
    )(group_expert, n_valid, xg, w1, w3, w2)


def _cast_kernel(w_ref, o_ref):
    o_ref[...] = w_ref[...].astype(o_ref.dtype)


def _to_bf16(w):
    n, rows, cols = w.shape
    blk = pl.BlockSpec((None, rows // 2, cols), lambda i, j: (i, j, 0))
    return pl.pallas_call(
        _cast_kernel,
        grid=(n, 2),
        in_specs=[blk],
        out_specs=blk,
        out_shape=jax.ShapeDtypeStruct(w.shape, BF16),
        compiler_params=_params("parallel", "parallel"),
        name="cast_bf16",
    )(w)


def _combine_kernel(x1_ref, y0_ref, y1_ref, gate_ref, o_ref):
    gate = gate_ref[...]
    y0 = _unpack_rows([y0_ref[j] for j in range(ROW_SPLIT)])
    y1 = _unpack_rows([y1_ref[j] for j in range(ROW_SPLIT)])
    o_ref[...] = x1_ref[...] + gate[:, 0:1] * y0 + gate[:, 1:2] * y1


def _combine(x1, yg, gate_t, tm=512):
    n_tok = x1.shape[0]
    nb = n_tok // tm
    return pl.pallas_call(
        _combine_kernel,
        grid=(nb,),
        in_specs=[pl.BlockSpec((tm, D_MODEL), lambda i: (i, 0)),
                  pl.BlockSpec((ROW_SPLIT, tm, SUBROW), lambda i: (0, i, 0)),
                  pl.BlockSpec((ROW_SPLIT, tm, SUBROW), lambda i: (0, i + nb, 0)),
                  pl.BlockSpec((tm, 2), lambda i: (i, 0))],
        out_specs=pl.BlockSpec((tm, D_MODEL), lambda i: (i, 0)),
        out_shape=jax.ShapeDtypeStruct((n_tok, D_MODEL), F32),
        compiler_params=_params("parallel"),
        name="moe_combine",
    )(x1, yg, yg, gate_t)


def _moe(x1, routing, w1, w3, w2):
    n_tok = x1.shape[0]
    idx, gate, rank, cnt, hf = routing
    counts = cnt[:, 0].astype(jnp.int32)
    padded = (counts + MOE_ROWS - 1) // MOE_ROWS * MOE_ROWS
    end_padded = jnp.cumsum(padded)
    start_padded = end_padded - padded
    start = sum(jnp.where(idx == e, start_padded[e], 0) for e in range(N_EXPERTS))
    dest = (start + rank).reshape(-1)
    n_groups = (n_tok * 2 + MOE_ROWS - 1) // MOE_ROWS + N_EXPERTS
    n_rows = n_groups * MOE_ROWS
    group_row = jnp.arange(n_groups, dtype=jnp.int32)[:, None] * MOE_ROWS
    group_expert = jnp.minimum(jnp.sum(group_row >= end_padded[None, :], axis=1), N_EXPERTS - 1).astype(jnp.int32)
    n_valid = (end_padded[-1:] // MOE_ROWS).astype(jnp.int32)
    fill = jnp.arange(MOE_ROWS, dtype=jnp.int32)[None, :]
    pad_dest = jnp.where(fill < (padded - counts)[:, None], (start_padded + counts)[:, None] + fill,
                         n_rows - MOE_ROWS + fill).reshape(-1)
    xg = _scatter_rows(hf, jnp.concatenate([dest, pad_dest]), n_rows)
    y = _experts(xg, group_expert, n_valid, w1, w3, w2)
    return _combine(x1, _gather_rows(y, dest), gate.T)


def _row(v, pad=0):
    v = v.reshape(1, -1).astype(F32)
    return jnp.pad(v, ((0, 0), (0, pad))) if pad else v


def _rows_at(w, start, total):
    return jnp.pad(w.astype(F32), ((start, total - start - w.shape[0]), (0, 0)))


def kernel(x, w_in, norm_mix, rwkv_mu, rwkv_decay_up, rwkv_w0, rwkv_a_up, rwkv_a0, rwkv_gate_up, rwkv_k_k, rwkv_k_a, rwkv_r_k, rwkv_ln_w, rwkv_ln_b, vres_down, vres_up, vres_v0, ssm_conv_w, ssm_conv_b, ssm_dt_bias, ssm_a_log, ssm_d, ssm_norm_w, att_q_gain, att_k_gain, att_rel_bias, gla_gate_up, gla_gate_bias, gla_norm_w, w_branch, w_out, norm_ffn, ffn_w1, ffn_w3, ffn_w2, moe_router, moe_w1, moe_w3, moe_w2):
    bsz, seq, _ = x.shape
    n_tok = bsz * seq
    depth = w_in.shape[0]
    x2 = x.reshape(n_tok, D_MODEL)
    v_first = None
    o_rw, o_ss, o_at = RWKV_COLS, RWKV_COLS + SSM_COLS, RWKV_COLS + SSM_COLS + ATT_COLS
    o_gl = o_at + GLA_COLS
    for l in range(depth):
        w = w_in[l]
        zeros = lambda n: jnp.zeros((D_MODEL, n), BF16)
        cols = lambda lo, hi: w[:, lo:hi].astype(BF16)
        vd = vres_down[l - 1].astype(BF16) if l > 0 else zeros(32)
        w_mix = jnp.concatenate(
            [cols(0, o_rw), vd, zeros(RWKV_PAD - RWKV_COLS - 32),
             cols(o_rw, o_ss), zeros(SSM_PAD - SSM_COLS),
             cols(o_ss, o_at),
             cols(o_at, o_at + 1024), cols(o_at + 1040, o_gl), cols(o_at + 1024, o_at + 1040),
             zeros(GLA_PAD - GLA_COLS)], axis=1)
        wg = w[:, o_gl:]
        h, za, zb, zc, zd = _inproj(x2, _row(norm_mix[l]), w_mix)
        shp = lambda z: z.reshape(bsz, seq, z.shape[-1])

        rw = dict(mu=_row(rwkv_mu[l], RWKV_PAD - RWKV_COLS),
                  dup=_rows_at(rwkv_decay_up[l], 0, LANES), w0=_row(rwkv_w0[l]),
                  aup=_rows_at(rwkv_a_up[l], 64, LANES), a0=_row(rwkv_a0[l]),
                  gup=_rows_at(rwkv_gate_up[l], 0, 256).astype(BF16),
                  kk=_row(rwkv_k_k[l]), ka=_row(rwkv_k_a[l]), rk=_row(rwkv_r_k[l]),
                  lnw=_row(rwkv_ln_w[l]), lnb=_row(rwkv_ln_b[l]))
        if l > 0:
            rw.update(vup=_rows_at(vres_up[l - 1], 160, 256), v0=_row(vres_v0[l - 1]))
        ssm = dict(cw=ssm_conv_w[l], cb=_row(ssm_conv_b[l]), dtb=_row(ssm_dt_bias[l], LANES - SSM_HEADS),
                   alog=_row(ssm_a_log[l], LANES - SSM_HEADS), dexp=_row(jnp.repeat(ssm_d[l], CHUNK)),
                   nw=_row(ssm_norm_w[l]))
        gla = dict(gup=_rows_at(gla_gate_up[l], 0, LANES), gb=_row(gla_gate_bias[l]), nw=_row(gla_norm_w[l]))
        (o_c,), (o_d,), (o_b,), res_a = _run_parts(
            [_attention_part(shp(zc), att_q_gain[l], att_k_gain[l], att_rel_bias, MIXER_BLOCK),
             _gla_part(shp(zd), gla, MIXER_BLOCK), _mamba_part(shp(zb), ssm, MIXER_BLOCK),
             _rwkv_part(shp(za), v_first, rw, MIXER_BLOCK)],
            grid=(bsz, seq // MIXER_BLOCK), name="mixers")
        o_a = res_a[0]
        if l == 0:
            v_first = res_a[1]
        outs = [o.reshape(n_tok, BRANCH_DIM) for o in (o_a, o_b, o_c, o_d)]
        merge_w = (wg.astype(BF16), w_branch[l].astype(BF16), w_out[l].astype(BF16), _row(norm_ffn[l]))
        if l % 2 == 0:
            x1, hf = _merge(x2, h, outs, *merge_w)
            x2 = _ffn(x1, hf, ffn_w1[l // 2].astype(BF16), ffn_w3[l // 2].astype(BF16),
                      ffn_w2[l // 2].astype(BF16))
        else:
            x1, *routing = _merge(x2, h, outs, *merge_w, wr_t=moe_router[l // 2].T)
            x2 = _moe(x1, routing, _to_bf16(moe_w1[l // 2]), _to_bf16(moe_w3[l // 2]), _to_bf16(moe_w2[l // 2]))
    return x2.reshape(bsz, seq, D_MODEL)
```

```python
import functools
from typing import Any, NamedTuple

import jax
import jax.numpy as jnp
from jax import lax
from jax.experimental import pallas as pl
from jax.experimental.pallas import tpu as pltpu
from jax.experimental.pallas import tpu_sc as plsc

F32 = jnp.float32
BF16 = jnp.bfloat16

D_MODEL = 1024
CHUNK = 64
BRANCH_DIM = 512
NORM_EPS = 1e-6
LANES = 128
VMEM_LIMIT = 56 * 1024 * 1024
BIG_VMEM_LIMIT = 61 * 1024 * 1024

LOG2_E = 1.4426950408889634

RWKV_LN_EPS = 64e-5
RWKV_DECAY_SCALE = 0.6065306597126334
RWKV_COLS = 1824
RWKV_PAD = 1920
SSM_COLS = 1544
SSM_PAD = 1664
SSM_HEADS = 8
SSM_STATE = 128
ATT_COLS = 1536
ATT_HEADS = 8
ATT_LEFT = 8 * CHUNK
REL_CLIP = 2 * CHUNK
GLA_COLS = 1552
GLA_PAD = 1664
GLA_GATE_NORM = 16.0
FFN_DIM = 2816
N_EXPERTS = 8
EXPERT_DIM = 3584
MOE_ROWS = 512
ROW_SPLIT = 2
SUBROW = D_MODEL // (2 * ROW_SPLIT)


def _dot(a, b):
    return jnp.dot(a.astype(BF16), b.astype(BF16), preferred_element_type=F32)


def _dot_nt(a, b):
    return lax.dot_general(a.astype(BF16), b.astype(BF16), (((1,), (1,)), ((), ())),
                           preferred_element_type=F32)


def _dot_tn(a, b):
    return lax.dot_general(a.astype(BF16), b.astype(BF16), (((0,), (0,)), ((), ())),
                           preferred_element_type=F32)


def _hi_lo(a):
    hi = a.astype(BF16)
    return hi, (a - hi.astype(F32)).astype(BF16)


def _dot_3x(a, b):
    a_hi, a_lo = _hi_lo(a)
    b_hi, b_lo = _hi_lo(b)
    return (jnp.dot(a_hi, b_hi, preferred_element_type=F32) + jnp.dot(a_lo, b_hi, preferred_element_type=F32)
            + jnp.dot(a_hi, b_lo, preferred_element_type=F32))


def _dot_split(a, m):
    m = m.astype(BF16)
    hi, lo = _hi_lo(a)
    return jnp.dot(hi, m, preferred_element_type=F32) + jnp.dot(lo, m, preferred_element_type=F32)


def _dot_split_l(m, a):
    m = m.astype(BF16)
    hi, lo = _hi_lo(a)
    return jnp.dot(m, hi, preferred_element_type=F32) + jnp.dot(m, lo, preferred_element_type=F32)


def _pack_rows(x):
    bits = pltpu.bitcast(x.astype(BF16).astype(F32), jnp.uint32)
    half = D_MODEL // 2
    return [(bits[:, p * SUBROW:(p + 1) * SUBROW] & jnp.uint32(0xFFFF0000))
            | (bits[:, half + p * SUBROW:half + (p + 1) * SUBROW] >> jnp.uint32(16)) for p in range(ROW_SPLIT)]


def _unpack_rows(planes):
    hi = [pltpu.bitcast(w & jnp.uint32(0xFFFF0000), F32) for w in planes]
    lo = [pltpu.bitcast(w << jnp.uint32(16), F32) for w in planes]
    return jnp.concatenate(hi + lo, axis=-1)


def _softplus(x):
    return jnp.maximum(x, 0.0) + jnp.log(1.0 + jnp.exp(-jnp.abs(x)))


def _sigmoid(x):
    return 1.0 / (1.0 + jnp.exp(-x))


def _silu(x):
    return x * _sigmoid(x)


def _iota2(shape, axis):
    return lax.broadcasted_iota(jnp.int32, shape, axis)


def _group_mean_matrix(n, group):
    r = _iota2((n, n), 0) // group
    c = _iota2((n, n), 1) // group
    return jnp.where(r == c, 1.0 / group, 0.0).astype(F32)


def _chunk_tri(n):
    r = _iota2((n, n), 0)
    c = _iota2((n, n), 1)
    return jnp.where((r // CHUNK == c // CHUNK) & (r >= c), 1.0, 0.0).astype(F32)


def _resident(shape):
    nd = len(shape)
    return pl.BlockSpec(shape, lambda *_: (0,) * nd, pipeline_mode=pl.Buffered(1))


def _params(*sem, vmem_limit=VMEM_LIMIT):
    return pltpu.CompilerParams(dimension_semantics=sem, vmem_limit_bytes=vmem_limit)


_DONE = object()


class _Part(NamedTuple):
    body: Any
    inputs: list
    in_specs: list
    out_shapes: list
    out_specs: list
    scratch_shapes: list


def _run_parts(parts, grid, name):
    def split(refs, counts):
        out, lo = [], 0
        for n in counts:
            out.append(refs[lo:lo + n])
            lo += n
        return out

    n_in = [len(p.inputs) for p in parts]
    n_out = [len(p.out_shapes) for p in parts]
    n_scr = [len(p.scratch_shapes) for p in parts]

    def kernel(*refs):
        ins = split(refs[:sum(n_in)], n_in)
        outs = split(refs[sum(n_in):sum(n_in) + sum(n_out)], n_out)
        scr = split(refs[sum(n_in) + sum(n_out):], n_scr)
        stages = [part.body(*i, *o, *c) for part, i, o, c in zip(parts, ins, outs, scr)]
        while stages:
            for stage in list(stages):
                if next(stage, _DONE) is _DONE:
                    stages.remove(stage)

    flat = lambda field: [x for p in parts for x in getattr(p, field)]
    res = pl.pallas_call(
        kernel,
        grid=grid,
        in_specs=flat("in_specs"),
        out_specs=flat("out_specs"),
        out_shape=flat("out_shapes"),
        scratch_shapes=flat("scratch_shapes"),
        compiler_params=_params("parallel", "arbitrary", vmem_limit=BIG_VMEM_LIMIT),
        name=name,
    )(*flat("inputs"))
    return split(list(res), n_out)


MIXER_COLS = (RWKV_PAD, SSM_PAD, ATT_COLS, GLA_PAD)
MIXER_BLOCK = 256


def _inproj_kernel(x_ref, g_ref, w_ref, h_ref, *z_refs):
    x = x_ref[...]
    ms = jnp.mean(x * x, axis=-1, keepdims=True)
    h = (x * lax.rsqrt(ms + NORM_EPS) * g_ref[...]).astype(BF16)
    h_ref[...] = h
    lo = 0
    for pair in (z_refs[:2], z_refs[2:]):
        widths = [z_ref.shape[-1] for z_ref in pair]
        z = jnp.dot(h, w_ref[:, lo:lo + sum(widths)], preferred_element_type=F32)
        pair[0][...] = z[:, :widths[0]]
        pair[1][...] = z[:, widths[0]:]
        lo += sum(widths)


def _inproj(x2, g, w_mix, tm=512):
    n_tok = x2.shape[0]
    row = lambda n: pl.BlockSpec((tm, n), lambda i: (i, 0))
    return pl.pallas_call(
        _inproj_kernel,
        grid=(n_tok // tm,),
        in_specs=[row(D_MODEL), _resident((1, D_MODEL)), _resident(w_mix.shape)],
        out_specs=[row(D_MODEL)] + [row(n) for n in MIXER_COLS],
        out_shape=[jax.ShapeDtypeStruct((n_tok, D_MODEL), BF16)]
        + [jax.ShapeDtypeStruct((n_tok, n), F32) for n in MIXER_COLS],
        compiler_params=_params("parallel"),
        name="inproj",
    )(x2, g, w_mix)


def _rwkv_block(r, k, v, kk, ka, ld, tb, state_ref, y_ref):
    n_chunks = tb // CHUNK
    cum = _dot_split_l(_chunk_tri(tb), ld)
    p_inv = jnp.exp(-cum)
    a_t = -kk * jnp.exp(cum - ld)
    b_t = kk * ka * p_inv
    k_t = k * p_inv
    r_t = r * jnp.exp(cum)
    bf = lambda z: z.astype(BF16)
    a_b, b_b, k_b, r_b, v_b = bf(a_t), bf(b_t), bf(k_t), bf(r_t), bf(v)
    yield

    lane_head = _iota2((CHUNK, LANES), 1) // CHUNK
    row = _iota2((CHUNK, LANES), 0)
    col = _iota2((CHUNK, LANES), 1) % CHUNK
    strict = row > col
    incl = row >= col
    bd_mask = (_iota2((LANES, LANES), 0) // CHUNK) == (_iota2((LANES, LANES), 1) // CHUNK)
    eye = jnp.where(_iota2((LANES, LANES), 0) == _iota2((LANES, LANES), 1), 1.0, 0.0).astype(F32)

    def stack(z):
        return jnp.concatenate([jnp.where(lane_head == 0, z, 0.0), jnp.where(lane_head == 1, z, 0.0)], axis=0)

    def sub(z, u):
        c, p = u
        return z[c * CHUNK:(c + 1) * CHUNK, p * LANES:(p + 1) * LANES]

    def prepare(units):
        gram = {u: _dot_nt(jnp.concatenate([sub(a_b, u), sub(r_b, u)], axis=0),
                           jnp.concatenate([stack(sub(b_b, u)), stack(sub(k_b, u))], axis=0))
                for u in units}
        yield
        a_rb = {u: bf(jnp.where(incl, gram[u][CHUNK:, :LANES], 0.0)) for u in units}
        a_akrk = {u: bf(jnp.concatenate([jnp.where(strict, gram[u][:CHUNK, LANES:], 0.0),
                                         jnp.where(incl, gram[u][CHUNK:, LANES:], 0.0)], axis=0)) for u in units}
        pw = {u: stack(jnp.where(strict, gram[u][:CHUNK, :LANES], 0.0)) for u in units}
        t_bd = {u: eye + pw[u] for u in units}
        pw = {u: _dot(pw[u], pw[u]) for u in units}
        yield
        for _ in range(CHUNK.bit_length() - 3):
            sp = {u: _dot(jnp.concatenate([t_bd[u], pw[u]], axis=0), pw[u]) for u in units}
            t_bd = {u: t_bd[u] + sp[u][:LANES] for u in units}
            pw = {u: sp[u][LANES:] for u in units}
            yield
        t_bd = {u: bf(t_bd[u] + _dot(t_bd[u], pw[u])) for u in units}
        v_bd = {u: stack(sub(v_b, u)) for u in units}
        avyv = {u: _dot(a_akrk[u], v_bd[u]) for u in units}
        yield
        wuv = {u: _dot(t_bd[u], jnp.concatenate([stack(sub(a_b, u)), stack(bf(avyv[u][:CHUNK]))], axis=1))
               for u in units}
        wr = {u: jnp.concatenate([bf(wuv[u][:, :LANES]), sub(r_b, u)], axis=0) for u in units}
        yield
        return a_rb, avyv, wuv, wr

    a_rb, avyv, wuv, wr = yield from prepare([(c, p) for c in range(n_chunks) for p in range(4)])

    h = [state_ref[p] for p in range(4)]
    for c in range(n_chunks):
        last = c * CHUNK + CHUNK - 1
        p_last = jnp.exp(cum[last:last + 1, :])
        hs = [_dot_nt(wr[(c, p)], h[p]) for p in range(4)]
        u2 = [hs[p][:LANES] + wuv[(c, p)][:, LANES:] for p in range(4)]
        u_p = [u2[p][:CHUNK] + u2[p][CHUNK:] for p in range(4)]
        upd = [_dot_tn(jnp.concatenate([u_p[p], sub(v, (c, p))], axis=0),
                       jnp.concatenate([sub(b_t, (c, p)), sub(k_t, (c, p))], axis=0)
                       * p_last[:, p * LANES:(p + 1) * LANES]) for p in range(4)]
        for p in range(4):
            y_ref[c * CHUNK:(c + 1) * CHUNK, p * LANES:(p + 1) * LANES] = (
                hs[p][LANES:] + _dot(a_rb[(c, p)], stack(u_p[p])) + avyv[(c, p)][CHUNK:])
        h = [h[p] * p_last[:, p * LANES:(p + 1) * LANES] + jnp.where(bd_mask, upd[p], 0.0) for p in range(4)]
        yield
    for p in range(4):
        state_ref[p] = h[p]


def _rwkv_kernel(has_vres, tb, *refs):
    if has_vres:
        (za_ref, vfirst_ref, mu_ref, dup_ref, w0_ref, aup_ref, a0_ref, gup_ref, vup_ref, v0_ref,
         kk_ref, ka_ref, rk_ref, lnw_ref, lnb_ref, o_ref, prev_ref, state_ref, y_ref) = refs
    else:
        (za_ref, mu_ref, dup_ref, w0_ref, aup_ref, a0_ref, gup_ref,
         kk_ref, ka_ref, rk_ref, lnw_ref, lnb_ref, o_ref, vraw_ref, prev_ref, state_ref, y_ref) = refs

    @pl.when(pl.program_id(1) == 0)
    def _():
        prev_ref[...] = jnp.zeros_like(prev_ref)
        state_ref[...] = jnp.zeros_like(state_ref)

    yield
    za = za_ref[...]
    shifted = pltpu.roll(za, 1, axis=0)
    shifted = jnp.where(_iota2(za.shape, 0) == 0, prev_ref[7:8, :], shifted)
    prev_ref[...] = za[tb - 8:, :]
    za = za + (shifted - za) * mu_ref[...]

    r = za[:, 0:512]
    k = za[:, 512:1024]
    v = za[:, 1024:1536]
    xwa = za[:, 1536:1664]
    xg = za[:, 1664:1920]
    ld = -RWKV_DECAY_SCALE * _sigmoid(w0_ref[...] + _dot_3x(jnp.tanh(xwa), dup_ref[...]))
    a = _sigmoid(a0_ref[...] + _dot(xwa, aup_ref[...]))
    g = _dot(_sigmoid(xg), gup_ref[...])
    if has_vres:
        v_mix = _sigmoid(v0_ref[...] + _dot(xg, vup_ref[...]))
        v = v + (vfirst_ref[...] - v) * v_mix
    else:
        vraw_ref[...] = v
    kk = k * kk_ref[...]
    k = k * (1.0 + (a - 1.0) * ka_ref[...])
    head_sum = _group_mean_matrix(512, CHUNK) * float(CHUNK)
    kk = kk / jnp.maximum(jnp.sqrt(_dot(kk * kk, head_sum)), 1e-12)
    yield

    yield from _rwkv_block(r, k, v, kk, a, ld, tb, state_ref, y_ref)
    y = y_ref[...]
    head_mean = _group_mean_matrix(512, CHUNK)
    mean = _dot_split(y, head_mean)
    yc = y - mean
    var = _dot(yc * yc, head_mean)
    y = yc * lax.rsqrt(var + RWKV_LN_EPS) * lnw_ref[...] + lnb_ref[...]
    y = y + _dot(r * k * rk_ref[...], head_sum) * v
    o_ref[...] = (y * g).astype(o_ref.dtype)


def _rwkv_part(za, v_first, prm, tb):
    bsz, seq, _ = za.shape
    has_vres = v_first is not None
    blk = lambda n: pl.BlockSpec((None, tb, n), lambda b, i: (b, i, 0))
    names = (["mu", "dup", "w0", "aup", "a0", "gup"] + (["vup", "v0"] if has_vres else [])
             + ["kk", "ka", "rk", "lnw", "lnb"])
    weights = [prm[n] for n in names]
    ins = [za] + ([v_first] if has_vres else []) + weights
    in_specs = [blk(RWKV_PAD)] + ([blk(512)] if has_vres else []) + [_resident(w.shape) for w in weights]
    out_shape = [jax.ShapeDtypeStruct((bsz, seq, 512), BF16)]
    out_specs = [blk(512)]
    if not has_vres:
        out_shape.append(jax.ShapeDtypeStruct((bsz, seq, 512), F32))
        out_specs.append(blk(512))
    return _Part(functools.partial(_rwkv_kernel, has_vres, tb), ins, in_specs, out_shape, out_specs,
                 [pltpu.VMEM((8, RWKV_PAD), F32), pltpu.VMEM((4, LANES, LANES), F32), pltpu.VMEM((tb, 512), F32)])


def _mamba_kernel(tb, zb_ref, cw_ref, cb_ref, dtb_ref, alog_ref, dexp_ref, nw_ref,
                  o_ref, xbuf_ref, state_ref, y_ref):
    @pl.when(pl.program_id(1) == 0)
    def _():
        xbuf_ref[0:8, :] = jnp.zeros((8, 1024), F32)
        state_ref[...] = jnp.zeros_like(state_ref)

    yield
    xbuf_ref[8:8 + tb, :] = zb_ref[:, 512:1536]
    conv = cb_ref[...]
    for i in range(4):
        conv = conv + cw_ref[i:i + 1, :] * xbuf_ref[5 + i:5 + i + tb, :]
    xbuf_ref[0:8, :] = xbuf_ref[tb:tb + 8, :]
    xbc = _silu(conv)
    xs = xbc[:, 0:512]
    bm = xbc[:, 512:768]
    cm = xbc[:, 768:1024]
    yield

    lane = _iota2((1, LANES), 1)
    dt = _softplus(zb_ref[:, 1536:1664] + dtb_ref[...])
    a_neg = jnp.where(lane < SSM_HEADS, -jnp.exp(alog_ref[...]), 0.0)
    acs = _dot_split_l(_chunk_tri(tb), dt * a_neg)
    expand = jnp.where(_iota2((LANES, 512), 0) == _iota2((LANES, 512), 1) // CHUNK, 1.0, 0.0)
    dt_e = _dot(dt, expand)
    acs_e = _dot_split(acs, expand)
    xdt = xs * dt_e
    yield

    n_chunks = tb // CHUNK
    causal = _iota2((CHUNK, 512), 0) >= _iota2((CHUNK, 512), 1) % CHUNK
    pair_head = _iota2((CHUNK, LANES), 1) // CHUNK
    spread = jnp.where(_iota2((CHUNK, 512), 0) == _iota2((CHUNK, 512), 1) % CHUNK, 1.0, 0.0)
    own_head = _iota2((SSM_HEADS, 512), 0) == _iota2((SSM_HEADS, 512), 1) // CHUNK
    rows_of = lambda c: slice(c * CHUNK, (c + 1) * CHUNK)
    grp = lambda g: slice(g * SSM_STATE, (g + 1) * SSM_STATE)

    def stack(z):
        return jnp.concatenate([jnp.where(pair_head == 0, z, 0.0), jnp.where(pair_head == 1, z, 0.0)], axis=0)

    y_intra, upd, e_in, e_last = [], [], [], []
    for c in range(n_chunks):
        rows = rows_of(c)
        acs_ec = acs_e[rows]
        acs_t = acs[rows].T[0:SSM_HEADS]
        acs_row = jnp.sum(jnp.where(own_head, _dot_split(acs_t, spread), 0.0), axis=0, keepdims=True)
        decay = jnp.exp(jnp.where(causal, acs_ec - acs_row, -jnp.inf))
        cb = jnp.concatenate(
            [_dot_nt(cm[rows, grp(g)], jnp.concatenate([bm[rows, grp(g)]] * 4, axis=0)) for g in range(2)], axis=1)
        m = cb * decay
        yield
        xdt_c = xdt[rows]
        y_intra.append([_dot(m[:, p * LANES:(p + 1) * LANES], stack(xdt_c[:, p * LANES:(p + 1) * LANES]))
                        for p in range(4)])
        last_e = acs_ec[CHUNK - 1:CHUNK, :]
        x_out = xdt_c * jnp.exp(last_e - acs_ec)
        upd.append([_dot_tn(bm[rows, grp(g)], x_out[:, g * 256:(g + 1) * 256]) for g in range(2)])
        e_in.append(jnp.exp(acs_ec))
        e_last.append(jnp.exp(last_e))
        yield

    state = [state_ref[g] for g in range(2)]
    for c in range(n_chunks):
        rows = rows_of(c)
        for g in range(2):
            ls = slice(g * 256, (g + 1) * 256)
            y_in = jnp.concatenate(y_intra[c][2 * g:2 * g + 2], axis=1)
            y_ref[rows, ls] = y_in + _dot(cm[rows, grp(g)], state[g]) * e_in[c][:, ls]
            state[g] = state[g] * e_last[c][:, ls] + upd[c][g]
        yield
    for g in range(2):
        state_ref[g] = state[g]

    y = (y_ref[...] + xs * dexp_ref[...]) * _silu(zb_ref[:, 0:512])
    for g in range(2):
        ls = slice(g * 256, (g + 1) * 256)
        yg = y[:, ls]
        ms = jnp.mean(yg * yg, axis=-1, keepdims=True)
        o_ref[:, ls] = (yg * lax.rsqrt(ms + NORM_EPS) * nw_ref[:, ls]).astype(o_ref.dtype)


def _mamba_part(zb, prm, tb):
    bsz, seq, _ = zb.shape
    blk = lambda n: pl.BlockSpec((None, tb, n), lambda b, i: (b, i, 0))
    weights = [prm[n] for n in ("cw", "cb", "dtb", "alog", "dexp", "nw")]
    return _Part(functools.partial(_mamba_kernel, tb), [zb] + weights,
                 [blk(SSM_PAD)] + [_resident(w.shape) for w in weights],
                 [jax.ShapeDtypeStruct((bsz, seq, 512), BF16)], [blk(512)],
                 [pltpu.VMEM((tb + 8, 1024), F32), pltpu.VMEM((2, SSM_STATE, 256), F32), pltpu.VMEM((tb, 512), F32)])


def _attn_kernel(tq, seq, q_ref, k_ref, v_ref, qg_ref, kg_ref, bias_ref, o_ref, kn_ref, vb_ref):
    i = pl.program_id(1)
    head_mean = _group_mean_matrix(512, CHUNK)
    win = tq + ATT_LEFT

    @pl.when(i == 0)
    def _():
        kn_ref[0:ATT_LEFT, :] = jnp.zeros((ATT_LEFT, 512), BF16)
        vb_ref[0:ATT_LEFT, :] = jnp.zeros((ATT_LEFT, 512), BF16)
        for j in range(seq // tq):
            kj = k_ref[j * tq:(j + 1) * tq, :]
            ms = _dot(kj * kj, head_mean)
            kn_ref[ATT_LEFT + j * tq:ATT_LEFT + (j + 1) * tq, :] = (
                kj * lax.rsqrt(ms + NORM_EPS) * kg_ref[...]).astype(BF16)
            vb_ref[ATT_LEFT + j * tq:ATT_LEFT + (j + 1) * tq, :] = v_ref[j * tq:(j + 1) * tq, :].astype(BF16)

    yield
    q = q_ref[...]
    ms = _dot(q * q, head_mean)
    qn = q * lax.rsqrt(ms + NORM_EPS) * qg_ref[...] * (CHUNK ** -0.5 * LOG2_E)
    start = pl.multiple_of(i * tq, tq)
    kwin = kn_ref[pl.ds(start, win), :]
    vwin = vb_ref[pl.ds(start, win), :]
    lane_head = _iota2((tq, LANES), 1) // CHUNK
    yield
    for p in range(4):
        ls = slice(p * LANES, (p + 1) * LANES)
        q2 = jnp.concatenate([jnp.where(lane_head == s, qn[:, ls], 0.0) for s in range(2)], axis=0)
        sc = _dot_nt(q2, kwin[:, ls]) + bias_ref[2 * p:2 * p + 2].reshape(2 * tq, win).astype(F32)
        yield
        e = jnp.exp2(sc - jnp.max(sc, axis=-1, keepdims=True))
        yield
        o2 = _dot(e, vwin[:, ls]) / jnp.sum(e, axis=-1, keepdims=True)
        o_ref[:, ls] = jnp.where(lane_head == 0, o2[:tq], o2[tq:]).astype(o_ref.dtype)
        yield


def _band_bias(rel_bias, tq, win):
    period = tq + win
    m = jnp.arange(period)
    d = jnp.where(m < win, m, m - period) - ATT_LEFT
    f = (rel_bias.astype(F32)[:, jnp.clip(d, -REL_CLIP, REL_CLIP) + REL_CLIP] * LOG2_E).astype(BF16)
    g = jnp.tile(f, (1, tq))[:, :tq * (period - 1)].reshape(-1, tq, period - 1)
    bias = g[:, :, :win]
    left = ATT_LEFT // CHUNK
    qc = left + jnp.arange(tq)[:, None] // CHUNK
    kc = jnp.arange(win)[None, :] // CHUNK
    tiles = []
    for blk in range(ATT_LEFT // tq + 1):
        first_kc = jnp.maximum(qc - left, left - blk * (tq // CHUNK))
        tiles.append(jnp.where((kc <= qc) & (kc >= first_kc), bias, -jnp.inf))
    return jnp.stack(tiles)


def _attention_part(zc, q_gain, k_gain, rel_bias, tq):
    bsz, seq, _ = zc.shape
    win = tq + ATT_LEFT
    bias = _band_bias(rel_bias, tq, win)
    qg = jnp.tile(q_gain, ATT_HEADS)[None, :]
    kg = jnp.tile(k_gain, ATT_HEADS)[None, :]
    full = lambda col: pl.BlockSpec((None, seq, 512), lambda b, i: (b, 0, col), pipeline_mode=pl.Buffered(1))
    last_tile = bias.shape[0] - 1
    bias_spec = pl.BlockSpec((None,) + bias.shape[1:], lambda b, i: (jnp.minimum(i, last_tile), 0, 0, 0))
    blk = pl.BlockSpec((None, tq, 512), lambda b, i: (b, i, 0))
    return _Part(functools.partial(_attn_kernel, tq, seq), [zc, zc, zc, qg, kg, bias],
                 [blk, full(1), full(2), _resident(qg.shape), _resident(kg.shape), bias_spec],
                 [jax.ShapeDtypeStruct((bsz, seq, 512), BF16)], [blk],
                 [pltpu.VMEM((ATT_LEFT + seq, 512), BF16), pltpu.VMEM((ATT_LEFT + seq, 512), BF16)])


def _gla_kernel(tb, zd_ref, gup_ref, gb_ref, nw_ref, o_ref, state_ref, y_ref):
    @pl.when(pl.program_id(1) == 0)
    def _():
        state_ref[...] = jnp.zeros_like(state_ref)

    yield
    q = zd_ref[:, 0:256] * (CHUNK ** -0.5)
    k = zd_ref[:, 256:512]
    v = zd_ref[:, 512:1024]
    log_a = -_softplus(-(_dot_3x(zd_ref[:, 1536:1664], gup_ref[...]) + gb_ref[...])) / GLA_GATE_NORM
    bcum = _dot_split_l(_chunk_tri(tb), log_a)

    k_head = _iota2((CHUNK, 256), 1) // CHUNK
    v_head = _iota2((CHUNK, 512), 1) // LANES
    causal = _iota2((CHUNK, 256), 0) >= _iota2((CHUNK, 256), 1) % CHUNK
    bd = (_iota2((512, 256), 0) // LANES) == (_iota2((512, 256), 1) // CHUNK)
    n_chunks = tb // CHUNK
    qg_all = (q * jnp.exp(bcum)).astype(BF16)
    kg_all = k * jnp.exp(-bcum)
    yield
    qg, att, o_intra, upd, e_last = [], [], [], [], []
    for c in range(n_chunks):
        rows = slice(c * CHUNK, (c + 1) * CHUNK)
        kg_bd = jnp.concatenate([jnp.where(k_head == h, kg_all[rows], 0.0) for h in range(4)], axis=0)
        qg.append(qg_all[rows])
        att.append(jnp.where(causal, _dot_nt(qg[c], kg_bd), 0.0))
        yield
    for c in range(n_chunks):
        rows = slice(c * CHUNK, (c + 1) * CHUNK)
        bc = bcum[rows]
        blast = bc[CHUNK - 1:CHUNK, :]
        v_c = v[rows]
        v_bd = jnp.concatenate([jnp.where(v_head == h, v_c, 0.0) for h in range(4)], axis=0)
        o_intra.append(_dot(att[c], v_bd))
        upd.append(jnp.where(bd, _dot_tn(v_c, k[rows] * jnp.exp(blast - bc)), 0.0))
        e_last.append(jnp.exp(blast))
        yield
    st = state_ref[...]
    for c in range(n_chunks):
        y_ref[c * CHUNK:(c + 1) * CHUNK, :] = o_intra[c] + _dot_nt(qg[c], st)
        st = st * e_last[c] + upd[c]
        yield
    state_ref[...] = st

    o = y_ref[...]
    for h in range(4):
        ls = slice(h * LANES, (h + 1) * LANES)
        oh = o[:, ls]
        ms = jnp.mean(oh * oh, axis=-1, keepdims=True)
        o_ref[:, ls] = (oh * lax.rsqrt(ms + NORM_EPS) * nw_ref[...]
                        * _silu(zd_ref[:, 1024 + h * LANES:1024 + (h + 1) * LANES])).astype(o_ref.dtype)


def _gla_part(zd, prm, tb):
    bsz, seq, _ = zd.shape
    blk = lambda n: pl.BlockSpec((None, tb, n), lambda b, i: (b, i, 0))
    weights = [prm[n] for n in ("gup", "gb", "nw")]
    return _Part(functools.partial(_gla_kernel, tb), [zd] + weights,
                 [blk(GLA_PAD)] + [_resident(w.shape) for w in weights],
                 [jax.ShapeDtypeStruct((bsz, seq, 512), BF16)], [blk(512)],
                 [pltpu.VMEM((512, 256), F32), pltpu.VMEM((tb, 512), F32)])


def _route(tm, hf, wr_ref, idx_ref, gate_ref, rank_ref, cnt_ref, hf_ref, carry_ref):
    @pl.when(pl.program_id(0) == 0)
    def _():
        carry_ref[...] = jnp.zeros_like(carry_ref)

    for j, words in enumerate(_pack_rows(hf)):
        hf_ref[j] = words
    nt = lambda a, b: lax.dot_general(a, b, (((1,), (1,)), ((), ())), preferred_element_type=F32)
    w_hi, w_lo = _hi_lo(wr_ref[...])
    h_hi, h_lo = _hi_lo(hf)
    logits = nt(w_hi, h_hi) + nt(w_lo, h_hi) + nt(w_hi, h_lo)
    e_iota = _iota2((N_EXPERTS, tm), 0)
    m1 = jnp.max(logits, axis=0, keepdims=True)
    i1 = jnp.min(jnp.where(logits == m1, e_iota, N_EXPERTS), axis=0, keepdims=True)
    rest = jnp.where(e_iota == i1, -jnp.inf, logits)
    m2 = jnp.max(rest, axis=0, keepdims=True)
    i2 = jnp.min(jnp.where(rest == m2, e_iota, N_EXPERTS), axis=0, keepdims=True)
    e2 = jnp.exp(m2 - m1)
    gate_ref[0:1, :] = 1.0 / (1.0 + e2)
    gate_ref[1:2, :] = e2 / (1.0 + e2)
    idx_ref[0:1, :] = i1
    idx_ref[1:2, :] = i2
    hit1 = jnp.where(e_iota == i1, 1.0, 0.0)
    hit2 = jnp.where(e_iota == i2, 1.0, 0.0)
    before = jnp.where(_iota2((tm, tm), 0) < _iota2((tm, tm), 1), 1.0, 0.0)
    prior = _dot(hit1 + hit2, before) + carry_ref[:, 0:1]
    rank_ref[0:1, :] = jnp.sum(hit1 * prior, axis=0, keepdims=True).astype(jnp.int32)
    rank_ref[1:2, :] = jnp.sum(hit2 * prior, axis=0, keepdims=True).astype(jnp.int32)
    carry_ref[...] = carry_ref[...] + jnp.sum(hit1 + hit2, axis=1, keepdims=True)
    cnt_ref[...] = carry_ref[...]


def _merge_kernel(tm, route, x_ref, h_ref, oa_ref, ob_ref, oc_ref, od_ref, wg_ref, wb_ref, wo_ref, nf_ref, *rest):
    h = h_ref[...]
    acc = jnp.zeros(x_ref.shape, F32)
    for i, o_ref in enumerate((oa_ref, ob_ref, oc_ref, od_ref)):
        gate = _sigmoid(jnp.dot(h, wg_ref[:, i * D_MODEL:(i + 1) * D_MODEL], preferred_element_type=F32))
        acc = acc + gate * jnp.dot(o_ref[...], wb_ref[i], preferred_element_type=F32)
    x1 = x_ref[...] + jnp.dot(acc.astype(BF16), wo_ref[...], preferred_element_type=F32)
    ms = jnp.mean(x1 * x1, axis=-1, keepdims=True)
    hf = x1 * lax.rsqrt(ms + NORM_EPS) * nf_ref[...]
    if route:
        wr_ref, x1_ref = rest[:2]
        _route(tm, hf, wr_ref, *rest[2:])
    else:
        x1_ref, hf_ref = rest
        hf_ref[...] = hf.astype(BF16)
    x1_ref[...] = x1


def _merge(x2, h, outs, wg, wb, wo, nf, wr_t=None, tm=512):
    n_tok = x2.shape[0]
    route = wr_t is not None
    row = lambda n: pl.BlockSpec((tm, n), lambda i: (i, 0))
    ins = [x2, h, *outs, wg, wb, wo, nf]
    in_specs = ([row(D_MODEL), row(D_MODEL)] + [row(BRANCH_DIM)] * 4
                + [_resident(wg.shape), _resident(wb.shape), _resident(wo.shape), _resident(nf.shape)])
    out_specs = [row(D_MODEL)]
    out_shape = [jax.ShapeDtypeStruct((n_tok, D_MODEL), F32)]
    scratch = []
    if route:
        ins.append(wr_t)
        in_specs.append(_resident(wr_t.shape))
        col = pl.BlockSpec((2, tm), lambda i: (0, i))
        out_specs += [col, col, col, pl.BlockSpec((N_EXPERTS, LANES), lambda i: (0, 0)),
                      pl.BlockSpec((ROW_SPLIT, tm, SUBROW), lambda i: (0, i, 0))]
        out_shape += [jax.ShapeDtypeStruct((2, n_tok), jnp.int32), jax.ShapeDtypeStruct((2, n_tok), F32),
                      jax.ShapeDtypeStruct((2, n_tok), jnp.int32), jax.ShapeDtypeStruct((N_EXPERTS, LANES), F32),
                      jax.ShapeDtypeStruct((ROW_SPLIT, n_tok, SUBROW), jnp.uint32)]
        scratch = [pltpu.VMEM((N_EXPERTS, LANES), F32)]
    else:
        out_specs.append(row(D_MODEL))
        out_shape.append(jax.ShapeDtypeStruct((n_tok, D_MODEL), BF16))
    return pl.pallas_call(
        functools.partial(_merge_kernel, tm, route),
        grid=(n_tok // tm,),
        in_specs=in_specs,
        out_specs=out_specs,
        out_shape=out_shape,
        scratch_shapes=scratch,
        compiler_params=_params("arbitrary" if route else "parallel"),
        name="merge",
    )(*ins)


def _ffn_kernel(tf, x1_ref, hf_ref, w1_ref, w3_ref, w2_ref, o_ref):
    hf = hf_ref[...]
    acc = x1_ref[...]
    for lo in range(0, FFN_DIM, tf):
        cols = slice(lo, min(lo + tf, FFN_DIM))
        a = jnp.dot(hf, w1_ref[:, cols], preferred_element_type=F32)
        b = jnp.dot(hf, w3_ref[:, cols], preferred_element_type=F32)
        acc = acc + jnp.dot((_silu(a) * b).astype(BF16), w2_ref[cols, :], preferred_element_type=F32)
    o_ref[...] = acc


def _ffn(x1, hf, w1, w3, w2, tm=512, tf=512):
    n_tok = x1.shape[0]
    row = pl.BlockSpec((tm, D_MODEL), lambda i: (i, 0))
    return pl.pallas_call(
        functools.partial(_ffn_kernel, tf),
        grid=(n_tok // tm,),
        in_specs=[row, row, _resident(w1.shape), _resident(w3.shape), _resident(w2.shape)],
        out_specs=row,
        out_shape=jax.ShapeDtypeStruct((n_tok, D_MODEL), F32),
        compiler_params=_params("parallel"),
        name="ffn",
    )(x1, hf, w1, w3, w2)


def _gather_rows(table, idx, window=128):
    split, n_table, width = table.shape
    n = idx.shape[0]
    flat_idx = (idx[None, :] + n_table * jnp.arange(split, dtype=jnp.int32)[:, None]).reshape(-1)
    return _gather_subrows(table.reshape(split * n_table, width), flat_idx, window).reshape(split, n, width)


def _gather_subrows(table, idx, window):
    n = idx.shape[0]
    d = table.shape[1]
    mesh = plsc.VectorSubcoreMesh(core_axis_name="core", subcore_axis_name="subcore")

    @functools.partial(pl.kernel, out_type=jax.ShapeDtypeStruct((n, d), table.dtype), mesh=mesh,
                       name="gather_rows")
    def gather(table_hbm, idx_hbm, out_hbm):
        def body(idx_vmem, out_vmem):
            pltpu.sync_copy(table_hbm.at[idx_vmem.at[0]], out_vmem)

        pltpu.emit_pipeline(
            body,
            grid=(n // window,),
            in_specs=[pl.BlockSpec((1, window), index_map=lambda i: (0, i))],
            out_specs=[pl.BlockSpec((window, d), index_map=lambda i: (i, 0))],
            core_axis_name=("core", "subcore"),
            dimension_semantics=(pltpu.PARALLEL,),
        )(idx_hbm, out_hbm)

    return gather(table, idx.reshape(1, n))


def _scatter_rows(table, dest, n_out, window=128):
    split, n_table, width = table.shape
    n = dest.shape[0]
    flat_dest = (dest[None, :] + n_out * jnp.arange(split, dtype=jnp.int32)[:, None]).reshape(1, split * n)
    src_blocks, blocks = n_table // window, n // window
    mesh = plsc.VectorSubcoreMesh(core_axis_name="core", subcore_axis_name="subcore")

    @functools.partial(pl.kernel, out_type=jax.ShapeDtypeStruct((split * n_out, width), table.dtype), mesh=mesh,
                       name="scatter_rows")
    def scatter(table_hbm, dest_hbm, out_hbm):
        def body(rows_vmem, dest_vmem):
            pltpu.sync_copy(rows_vmem, out_hbm.at[dest_vmem.at[0]])

        pltpu.emit_pipeline(
            body,
            grid=(split * blocks,),
            in_specs=[pl.BlockSpec((window, width),
                                   index_map=lambda i: ((i // blocks) * src_blocks + (i % blocks) % src_blocks, 0)),
                      pl.BlockSpec((1, window), index_map=lambda i: (0, i))],
            out_specs=[],
            core_axis_name=("core", "subcore"),
            dimension_semantics=(pltpu.PARALLEL,),
        )(table_hbm, dest_hbm)

    return scatter(table.reshape(split * n_table, width), flat_dest).reshape(split, n_out, width)


def _expert_kernel(tf, ge_ref, nv_ref, x_ref, w1_ref, w3_ref, w2_ref, y_ref):
    g = pl.program_id(0)

    @pl.when(g < nv_ref[0])
    def _():
        x = _unpack_rows([x_ref[j] for j in range(ROW_SPLIT)]).astype(BF16)
        acc = jnp.zeros((MOE_ROWS, D_MODEL), F32)
        for lo in range(0, EXPERT_DIM, tf):
            a = jnp.dot(x, w1_ref[:, lo:lo + tf], preferred_element_type=F32)
            b = jnp.dot(x, w3_ref[:, lo:lo + tf], preferred_element_type=F32)
            acc = acc + jnp.dot((_silu(a) * b).astype(BF16), w2_ref[lo:lo + tf, :], preferred_element_type=F32)
        for j, words in enumerate(_pack_rows(acc)):
            y_ref[j] = words

    @pl.when(g >= nv_ref[0])
    def _():
        y_ref[...] = jnp.zeros_like(y_ref)


def _experts(xg, group_expert, n_valid, w1, w3, w2, tf=512):
    n_rows = xg.shape[1]
    rows = pl.BlockSpec((ROW_SPLIT, MOE_ROWS, SUBROW), lambda g, ge, nv: (0, g, 0))
    return pl.pallas_call(
        functools.partial(_expert_kernel, tf),
        grid_spec=pltpu.PrefetchScalarGridSpec(
            num_scalar_prefetch=2,
            grid=(n_rows // MOE_ROWS,),
            in_specs=[rows,
                      pl.BlockSpec((None, D_MODEL, EXPERT_DIM), lambda g, ge, nv: (ge[g], 0, 0)),
                      pl.BlockSpec((None, D_MODEL, EXPERT_DIM), lambda g, ge, nv: (ge[g], 0, 0)),
                      pl.BlockSpec((None, EXPERT_DIM, D_MODEL), lambda g, ge, nv: (ge[g], 0, 0))],
            out_specs=rows,
        ),
        out_shape=jax.ShapeDtypeStruct((ROW_SPLIT, n_rows, SUBROW), jnp.uint32),
        compiler_params=_params("arbitrary", vmem_limit=BIG_VMEM_LIMIT),
        name="experts",
    )(group_expert, n_valid, xg, w1, w3, w2)


def _cast_kernel(w_ref, o_ref):
    o_ref[...] = w_ref[...].astype(o_ref.dtype)


def _to_bf16(w):
    n, rows, cols = w.shape
    blk = pl.BlockSpec((None, rows // 2, cols), lambda i, j: (i, j, 0))
    return pl.pallas_call(
        _cast_kernel,
        grid=(n, 2),
        in_specs=[blk],
        out_specs=blk,
        out_shape=jax.ShapeDtypeStruct(w.shape, BF16),
        compiler_params=_params("parallel", "parallel"),
        name="cast_bf16",
    )(w)


def _combine_kernel(x1_ref, y0_ref, y1_ref, gate_ref, o_ref):
    gate = gate_ref[...]
    y0 = _unpack_rows([y0_ref[j] for j in range(ROW_SPLIT)])
    y1 = _unpack_rows([y1_ref[j] for j in range(ROW_SPLIT)])
    o_ref[...] = x1_ref[...] + gate[:, 0:1] * y0 + gate[:, 1:2] * y1


def _combine(x1, yg, gate_t, tm=512):
    n_tok = x1.shape[0]
    nb = n_tok // tm
    return pl.pallas_call(
        _combine_kernel,
        grid=(nb,),
        in_specs=[pl.BlockSpec((tm, D_MODEL), lambda i: (i, 0)),
                  pl.BlockSpec((ROW_SPLIT, tm, SUBROW), lambda i: (0, i, 0)),
                  pl.BlockSpec((ROW_SPLIT, tm, SUBROW), lambda i: (0, i + nb, 0)),
                  pl.BlockSpec((tm, 2), lambda i: (i, 0))],
        out_specs=pl.BlockSpec((tm, D_MODEL), lambda i: (i, 0)),
        out_shape=jax.ShapeDtypeStruct((n_tok, D_MODEL), F32),
        compiler_params=_params("parallel"),
        name="moe_combine",
    )(x1, yg, yg, gate_t)


def _moe(x1, routing, w1, w3, w2):
    n_tok = x1.shape[0]
    idx, gate, rank, cnt, hf = routing
    counts = cnt[:, 0].astype(jnp.int32)
    padded = (counts + MOE_ROWS - 1) // MOE_ROWS * MOE_ROWS
    end_padded = jnp.cumsum(padded)
    start_padded = end_padded - padded
    start = sum(jnp.where(idx == e, start_padded[e], 0) for e in range(N_EXPERTS))
    dest = (start + rank).reshape(-1)
    n_groups = (n_tok * 2 + MOE_ROWS - 1) // MOE_ROWS + N_EXPERTS
    n_rows = n_groups * MOE_ROWS
    group_row = jnp.arange(n_groups, dtype=jnp.int32)[:, None] * MOE_ROWS
    group_expert = jnp.minimum(jnp.sum(group_row >= end_padded[None, :], axis=1), N_EXPERTS - 1).astype(jnp.int32)
    n_valid = (end_padded[-1:] // MOE_ROWS).astype(jnp.int32)
    fill = jnp.arange(MOE_ROWS, dtype=jnp.int32)[None, :]
    pad_dest = jnp.where(fill < (padded - counts)[:, None], (start_padded + counts)[:, None] + fill,
                         n_rows - MOE_ROWS + fill).reshape(-1)
    xg = _scatter_rows(hf, jnp.concatenate([dest, pad_dest]), n_rows)
    y = _experts(xg, group_expert, n_valid, w1, w3, w2)
    return _combine(x1, _gather_rows(y, dest), gate.T)


def _row(v, pad=0):
    v = v.reshape(1, -1).astype(F32)
    return jnp.pad(v, ((0, 0), (0, pad))) if pad else v


def _rows_at(w, start, total):
    return jnp.pad(w.astype(F32), ((start, total - start - w.shape[0]), (0, 0)))


def kernel(x, w_in, norm_mix, rwkv_mu, rwkv_decay_up, rwkv_w0, rwkv_a_up, rwkv_a0, rwkv_gate_up, rwkv_k_k, rwkv_k_a, rwkv_r_k, rwkv_ln_w, rwkv_ln_b, vres_down, vres_up, vres_v0, ssm_conv_w, ssm_conv_b, ssm_dt_bias, ssm_a_log, ssm_d, ssm_norm_w, att_q_gain, att_k_gain, att_rel_bias, gla_gate_up, gla_gate_bias, gla_norm_w, w_branch, w_out, norm_ffn, ffn_w1, ffn_w3, ffn_w2, moe_router, moe_w1, moe_w3, moe_w2):
    bsz, seq, _ = x.shape
    n_tok = bsz * seq
    depth = w_in.shape[0]
    x2 = x.reshape(n_tok, D_MODEL)
    v_first = None
    o_rw, o_ss, o_at = RWKV_COLS, RWKV_COLS + SSM_COLS, RWKV_COLS + SSM_COLS + ATT_COLS
    o_gl = o_at + GLA_COLS
    for l in range(depth):
        w = w_in[l]
        zeros = lambda n: jnp.zeros((D_MODEL, n), BF16)
        cols = lambda lo, hi: w[:, lo:hi].astype(BF16)
        vd = vres_down[l - 1].astype(BF16) if l > 0 else zeros(32)
        w_mix = jnp.concatenate(
            [cols(0, o_rw), vd, zeros(RWKV_PAD - RWKV_COLS - 32),
             cols(o_rw, o_ss), zeros(SSM_PAD - SSM_COLS),
             cols(o_ss, o_at),
             cols(o_at, o_at + 1024), cols(o_at + 1040, o_gl), cols(o_at + 1024, o_at + 1040),
             zeros(GLA_PAD - GLA_COLS)], axis=1)
        wg = w[:, o_gl:]
        h, za, zb, zc, zd = _inproj(x2, _row(norm_mix[l]), w_mix)
        shp = lambda z: z.reshape(bsz, seq, z.shape[-1])

        rw = dict(mu=_row(rwkv_mu[l], RWKV_PAD - RWKV_COLS),
                  dup=_rows_at(rwkv_decay_up[l], 0, LANES), w0=_row(rwkv_w0[l]),
                  aup=_rows_at(rwkv_a_up[l], 64, LANES), a0=_row(rwkv_a0[l]),
                  gup=_rows_at(rwkv_gate_up[l], 0, 256).astype(BF16),
                  kk=_row(rwkv_k_k[l]), ka=_row(rwkv_k_a[l]), rk=_row(rwkv_r_k[l]),
                  lnw=_row(rwkv_ln_w[l]), lnb=_row(rwkv_ln_b[l]))
        if l > 0:
            rw.update(vup=_rows_at(vres_up[l - 1], 160, 256), v0=_row(vres_v0[l - 1]))
        ssm = dict(cw=ssm_conv_w[l], cb=_row(ssm_conv_b[l]), dtb=_row(ssm_dt_bias[l], LANES - SSM_HEADS),
                   alog=_row(ssm_a_log[l], LANES - SSM_HEADS), dexp=_row(jnp.repeat(ssm_d[l], CHUNK)),
                   nw=_row(ssm_norm_w[l]))
        gla = dict(gup=_rows_at(gla_gate_up[l], 0, LANES), gb=_row(gla_gate_bias[l]), nw=_row(gla_norm_w[l]))
        (o_c,), (o_d,), (o_b,), res_a = _run_parts(
            [_attention_part(shp(zc), att_q_gain[l], att_k_gain[l], att_rel_bias, MIXER_BLOCK),
             _gla_part(shp(zd), gla, MIXER_BLOCK), _mamba_part(shp(zb), ssm, MIXER_BLOCK),
             _rwkv_part(shp(za), v_first, rw, MIXER_BLOCK)],
            grid=(bsz, seq // MIXER_BLOCK), name="mixers")
        o_a = res_a[0]
        if l == 0:
            v_first = res_a[1]
        outs = [o.reshape(n_tok, BRANCH_DIM) for o in (o_a, o_b, o_c, o_d)]
        merge_w = (wg.astype(BF16), w_branch[l].astype(BF16), w_out[l].astype(BF16), _row(norm_ffn[l]))
        if l % 2 == 0:
            x1, hf = _merge(x2, h, outs, *merge_w)
            x2 = _ffn(x1, hf, ffn_w1[l // 2].astype(BF16), ffn_w3[l // 2].astype(BF16),
                      ffn_w2[l // 2].astype(BF16))
        else:
            x1, *routing = _merge(x2, h, outs, *merge_w, wr_t=moe_router[l // 2].T)
            x2 = _moe(x1, routing, _to_bf16(moe_w1[l // 2]), _to_bf16(moe_w3[l // 2]), _to_bf16(moe_w2[l // 2]))
    return x2.reshape(bsz, seq, D_MODEL)
```

```python
import functools
from typing import Any, NamedTuple

import jax
import jax.numpy as jnp
from jax import lax
from jax.experimental import pallas as pl
from jax.experimental.pallas import tpu as pltpu
from jax.experimental.pallas import tpu_sc as plsc

F32 = jnp.float32
BF16 = jnp.bfloat16

D_MODEL = 1024
CHUNK = 64
BRANCH_DIM = 512
NORM_EPS = 1e-6
LANES = 128
VMEM_LIMIT = 56 * 1024 * 1024
BIG_VMEM_LIMIT = 61 * 1024 * 1024

LOG2_E = 1.4426950408889634

RWKV_LN_EPS = 64e-5
RWKV_DECAY_SCALE = 0.6065306597126334
RWKV_COLS = 1824
RWKV_PAD = 1920
SSM_COLS = 1544
SSM_PAD = 1664
SSM_HEADS = 8
SSM_STATE = 128
ATT_COLS = 1536
ATT_HEADS = 8
ATT_LEFT = 8 * CHUNK
REL_CLIP = 2 * CHUNK
GLA_COLS = 1552
GLA_PAD = 1664
GLA_GATE_NORM = 16.0
FFN_DIM = 2816
N_EXPERTS = 8
EXPERT_DIM = 3584
MOE_MERGE_ROWS = 256
MOE_ROWS = 512
ROW_SPLIT = 2
SUBROW = D_MODEL // (2 * ROW_SPLIT)


def _dot(a, b):
    return jnp.dot(a.astype(BF16), b.astype(BF16), preferred_element_type=F32)


def _dot_nt(a, b):
    return lax.dot_general(a.astype(BF16), b.astype(BF16), (((1,), (1,)), ((), ())),
                           preferred_element_type=F32)


def _dot_tn(a, b):
    return lax.dot_general(a.astype(BF16), b.astype(BF16), (((0,), (0,)), ((), ())),
                           preferred_element_type=F32)


def _hi_lo(a):
    hi = a.astype(BF16)
    return hi, (a - hi.astype(F32)).astype(BF16)


def _dot_3x(a, b):
    a_hi, a_lo = _hi_lo(a)
    b_hi, b_lo = _hi_lo(b)
    return (jnp.dot(a_hi, b_hi, preferred_element_type=F32) + jnp.dot(a_lo, b_hi, preferred_element_type=F32)
            + jnp.dot(a_hi, b_lo, preferred_element_type=F32))


def _dot_split(a, m):
    m = m.astype(BF16)
    hi, lo = _hi_lo(a)
    return jnp.dot(hi, m, preferred_element_type=F32) + jnp.dot(lo, m, preferred_element_type=F32)


def _dot_split_l(m, a):
    m = m.astype(BF16)
    hi, lo = _hi_lo(a)
    return jnp.dot(m, hi, preferred_element_type=F32) + jnp.dot(m, lo, preferred_element_type=F32)


def _pack_rows(x):
    bits = pltpu.bitcast(x.astype(BF16).astype(F32), jnp.uint32)
    half = D_MODEL // 2
    return [(bits[:, p * SUBROW:(p + 1) * SUBROW] & jnp.uint32(0xFFFF0000))
            | (bits[:, half + p * SUBROW:half + (p + 1) * SUBROW] >> jnp.uint32(16)) for p in range(ROW_SPLIT)]


def _unpack_rows(planes):
    hi = [pltpu.bitcast(w & jnp.uint32(0xFFFF0000), F32) for w in planes]
    lo = [pltpu.bitcast(w << jnp.uint32(16), F32) for w in planes]
    return jnp.concatenate(hi + lo, axis=-1)


def _softplus(x):
    return jnp.maximum(x, 0.0) + jnp.log(1.0 + jnp.exp(-jnp.abs(x)))


def _sigmoid(x):
    return 1.0 / (1.0 + jnp.exp(-x))


def _silu(x):
    return x * _sigmoid(x)


def _iota2(shape, axis):
    return lax.broadcasted_iota(jnp.int32, shape, axis)


def _group_mean_matrix(n, group):
    r = _iota2((n, n), 0) // group
    c = _iota2((n, n), 1) // group
    return jnp.where(r == c, 1.0 / group, 0.0).astype(F32)


def _chunk_tri(n):
    r = _iota2((n, n), 0)
    c = _iota2((n, n), 1)
    return jnp.where((r // CHUNK == c // CHUNK) & (r >= c), 1.0, 0.0).astype(F32)


def _resident(shape):
    nd = len(shape)
    return pl.BlockSpec(shape, lambda *_: (0,) * nd, pipeline_mode=pl.Buffered(1))


def _params(*sem, vmem_limit=VMEM_LIMIT):
    return pltpu.CompilerParams(dimension_semantics=sem, vmem_limit_bytes=vmem_limit)


_DONE = object()


class _Part(NamedTuple):
    body: Any
    inputs: list
    in_specs: list
    out_shapes: list
    out_specs: list
    scratch_shapes: list


def _run_parts(parts, grid, name):
    def split(refs, counts):
        out, lo = [], 0
        for n in counts:
            out.append(refs[lo:lo + n])
            lo += n
        return out

    n_in = [len(p.inputs) for p in parts]
    n_out = [len(p.out_shapes) for p in parts]
    n_scr = [len(p.scratch_shapes) for p in parts]

    def kernel(*refs):
        ins = split(refs[:sum(n_in)], n_in)
        outs = split(refs[sum(n_in):sum(n_in) + sum(n_out)], n_out)
        scr = split(refs[sum(n_in) + sum(n_out):], n_scr)
        stages = [part.body(*i, *o, *c) for part, i, o, c in zip(parts, ins, outs, scr)]
        while stages:
            for stage in list(stages):
                if next(stage, _DONE) is _DONE:
                    stages.remove(stage)

    flat = lambda field: [x for p in parts for x in getattr(p, field)]
    res = pl.pallas_call(
        kernel,
        grid=grid,
        in_specs=flat("in_specs"),
        out_specs=flat("out_specs"),
        out_shape=flat("out_shapes"),
        scratch_shapes=flat("scratch_shapes"),
        compiler_params=_params("parallel", "arbitrary", vmem_limit=BIG_VMEM_LIMIT),
        name=name,
    )(*flat("inputs"))
    return split(list(res), n_out)


MIXER_COLS = (RWKV_PAD, SSM_PAD, ATT_COLS, GLA_PAD)
MIXER_BLOCK = 256


def _inproj_kernel(x_ref, g_ref, w_ref, h_ref, *z_refs):
    x = x_ref[...]
    ms = jnp.mean(x * x, axis=-1, keepdims=True)
    h = (x * lax.rsqrt(ms + NORM_EPS) * g_ref[...]).astype(BF16)
    h_ref[...] = h
    lo = 0
    for pair in (z_refs[:2], z_refs[2:]):
        widths = [z_ref.shape[-1] for z_ref in pair]
        z = jnp.dot(h, w_ref[:, lo:lo + sum(widths)], preferred_element_type=F32)
        pair[0][...] = z[:, :widths[0]]
        pair[1][...] = z[:, widths[0]:]
        lo += sum(widths)


def _inproj(x2, g, w_mix, tm=512):
    n_tok = x2.shape[0]
    row = lambda n: pl.BlockSpec((tm, n), lambda i: (i, 0))
    return pl.pallas_call(
        _inproj_kernel,
        grid=(n_tok // tm,),
        in_specs=[row(D_MODEL), _resident((1, D_MODEL)), _resident(w_mix.shape)],
        out_specs=[row(D_MODEL)] + [row(n) for n in MIXER_COLS],
        out_shape=[jax.ShapeDtypeStruct((n_tok, D_MODEL), BF16)]
        + [jax.ShapeDtypeStruct((n_tok, n), F32) for n in MIXER_COLS],
        compiler_params=_params("parallel"),
        name="inproj",
    )(x2, g, w_mix)


def _rwkv_block(r, k, v, kk, ka, ld, tb, state_ref, y_ref):
    n_chunks = tb // CHUNK
    cum = _dot_split_l(_chunk_tri(tb), ld)
    p_inv = jnp.exp(-cum)
    a_t = -kk * jnp.exp(cum - ld)
    b_t = kk * ka * p_inv
    k_t = k * p_inv
    r_t = r * jnp.exp(cum)
    bf = lambda z: z.astype(BF16)
    a_b, b_b, k_b, r_b, v_b = bf(a_t), bf(b_t), bf(k_t), bf(r_t), bf(v)
    yield

    lane_head = _iota2((CHUNK, LANES), 1) // CHUNK
    row = _iota2((CHUNK, LANES), 0)
    col = _iota2((CHUNK, LANES), 1) % CHUNK
    strict = row > col
    incl = row >= col
    bd_mask = (_iota2((LANES, LANES), 0) // CHUNK) == (_iota2((LANES, LANES), 1) // CHUNK)
    eye = jnp.where(_iota2((LANES, LANES), 0) == _iota2((LANES, LANES), 1), 1.0, 0.0).astype(F32)

    def stack(z):
        return jnp.concatenate([jnp.where(lane_head == 0, z, 0.0), jnp.where(lane_head == 1, z, 0.0)], axis=0)

    def sub(z, u):
        c, p = u
        return z[c * CHUNK:(c + 1) * CHUNK, p * LANES:(p + 1) * LANES]

    def prepare(units):
        gram = {u: _dot_nt(jnp.concatenate([sub(a_b, u), sub(r_b, u)], axis=0),
                           jnp.concatenate([stack(sub(b_b, u)), stack(sub(k_b, u))], axis=0))
                for u in units}
        yield
        a_rb = {u: bf(jnp.where(incl, gram[u][CHUNK:, :LANES], 0.0)) for u in units}
        a_akrk = {u: bf(jnp.concatenate([jnp.where(strict, gram[u][:CHUNK, LANES:], 0.0),
                                         jnp.where(incl, gram[u][CHUNK:, LANES:], 0.0)], axis=0)) for u in units}
        pw = {u: stack(jnp.where(strict, gram[u][:CHUNK, :LANES], 0.0)) for u in units}
        t_bd = {u: eye + pw[u] for u in units}
        pw = {u: _dot(pw[u], pw[u]) for u in units}
        yield
        for _ in range(CHUNK.bit_length() - 3):
            sp = {u: _dot(jnp.concatenate([t_bd[u], pw[u]], axis=0), pw[u]) for u in units}
            t_bd = {u: t_bd[u] + sp[u][:LANES] for u in units}
            pw = {u: sp[u][LANES:] for u in units}
            yield
        t_bd = {u: bf(t_bd[u] + _dot(t_bd[u], pw[u])) for u in units}
        v_bd = {u: stack(sub(v_b, u)) for u in units}
        avyv = {u: _dot(a_akrk[u], v_bd[u]) for u in units}
        yield
        wuv = {u: _dot(t_bd[u], jnp.concatenate([stack(sub(a_b, u)), stack(bf(avyv[u][:CHUNK]))], axis=1))
               for u in units}
        wr = {u: jnp.concatenate([bf(wuv[u][:, :LANES]), sub(r_b, u)], axis=0) for u in units}
        yield
        return a_rb, avyv, wuv, wr

    a_rb, avyv, wuv, wr = yield from prepare([(c, p) for c in range(n_chunks) for p in range(4)])

    h = [state_ref[p] for p in range(4)]
    for c in range(n_chunks):
        last = c * CHUNK + CHUNK - 1
        p_last = jnp.exp(cum[last:last + 1, :])
        hs = [_dot_nt(wr[(c, p)], h[p]) for p in range(4)]
        u2 = [hs[p][:LANES] + wuv[(c, p)][:, LANES:] for p in range(4)]
        u_p = [u2[p][:CHUNK] + u2[p][CHUNK:] for p in range(4)]
        upd = [_dot_tn(jnp.concatenate([u_p[p], sub(v, (c, p))], axis=0),
                       jnp.concatenate([sub(b_t, (c, p)), sub(k_t, (c, p))], axis=0)
                       * p_last[:, p * LANES:(p + 1) * LANES]) for p in range(4)]
        for p in range(4):
            y_ref[c * CHUNK:(c + 1) * CHUNK, p * LANES:(p + 1) * LANES] = (
                hs[p][LANES:] + _dot(a_rb[(c, p)], stack(u_p[p])) + avyv[(c, p)][CHUNK:])
        h = [h[p] * p_last[:, p * LANES:(p + 1) * LANES] + jnp.where(bd_mask, upd[p], 0.0) for p in range(4)]
        yield
    for p in range(4):
        state_ref[p] = h[p]


def _rwkv_kernel(has_vres, tb, *refs):
    if has_vres:
        (za_ref, vfirst_ref, mu_ref, dup_ref, w0_ref, aup_ref, a0_ref, gup_ref, vup_ref, v0_ref,
         kk_ref, ka_ref, rk_ref, lnw_ref, lnb_ref, o_ref, prev_ref, state_ref, y_ref) = refs
    else:
        (za_ref, mu_ref, dup_ref, w0_ref, aup_ref, a0_ref, gup_ref,
         kk_ref, ka_ref, rk_ref, lnw_ref, lnb_ref, o_ref, vraw_ref, prev_ref, state_ref, y_ref) = refs

    @pl.when(pl.program_id(1) == 0)
    def _():
        prev_ref[...] = jnp.zeros_like(prev_ref)
        state_ref[...] = jnp.zeros_like(state_ref)

    yield
    za = za_ref[...]
    shifted = pltpu.roll(za, 1, axis=0)
    shifted = jnp.where(_iota2(za.shape, 0) == 0, prev_ref[7:8, :], shifted)
    prev_ref[...] = za[tb - 8:, :]
    za = za + (shifted - za) * mu_ref[...]

    r = za[:, 0:512]
    k = za[:, 512:1024]
    v = za[:, 1024:1536]
    xwa = za[:, 1536:1664]
    xg = za[:, 1664:1920]
    ld = -RWKV_DECAY_SCALE * _sigmoid(w0_ref[...] + _dot_3x(jnp.tanh(xwa), dup_ref[...]))
    a = _sigmoid(a0_ref[...] + _dot(xwa, aup_ref[...]))
    g = _dot(_sigmoid(xg), gup_ref[...])
    if has_vres:
        v_mix = _sigmoid(v0_ref[...] + _dot(xg, vup_ref[...]))
        v = v + (vfirst_ref[...] - v) * v_mix
    else:
        vraw_ref[...] = v
    kk = k * kk_ref[...]
    k = k * (1.0 + (a - 1.0) * ka_ref[...])
    head_sum = _group_mean_matrix(512, CHUNK) * float(CHUNK)
    kk = kk / jnp.maximum(jnp.sqrt(_dot(kk * kk, head_sum)), 1e-12)
    yield

    yield from _rwkv_block(r, k, v, kk, a, ld, tb, state_ref, y_ref)
    y = y_ref[...]
    head_mean = _group_mean_matrix(512, CHUNK)
    mean = _dot_split(y, head_mean)
    yc = y - mean
    var = _dot(yc * yc, head_mean)
    y = yc * lax.rsqrt(var + RWKV_LN_EPS) * lnw_ref[...] + lnb_ref[...]
    y = y + _dot(r * k * rk_ref[...], head_sum) * v
    o_ref[...] = (y * g).astype(o_ref.dtype)


def _rwkv_part(za, v_first, prm, tb):
    bsz, seq, _ = za.shape
    has_vres = v_first is not None
    blk = lambda n: pl.BlockSpec((None, tb, n), lambda b, i: (b, i, 0))
    names = (["mu", "dup", "w0", "aup", "a0", "gup"] + (["vup", "v0"] if has_vres else [])
             + ["kk", "ka", "rk", "lnw", "lnb"])
    weights = [prm[n] for n in names]
    ins = [za] + ([v_first] if has_vres else []) + weights
    in_specs = [blk(RWKV_PAD)] + ([blk(512)] if has_vres else []) + [_resident(w.shape) for w in weights]
    out_shape = [jax.ShapeDtypeStruct((bsz, seq, 512), BF16)]
    out_specs = [blk(512)]
    if not has_vres:
        out_shape.append(jax.ShapeDtypeStruct((bsz, seq, 512), F32))
        out_specs.append(blk(512))
    return _Part(functools.partial(_rwkv_kernel, has_vres, tb), ins, in_specs, out_shape, out_specs,
                 [pltpu.VMEM((8, RWKV_PAD), F32), pltpu.VMEM((4, LANES, LANES), F32), pltpu.VMEM((tb, 512), F32)])


def _mamba_kernel(tb, zb_ref, cw_ref, cb_ref, dtb_ref, alog_ref, dexp_ref, nw_ref,
                  o_ref, xbuf_ref, state_ref, y_ref):
    @pl.when(pl.program_id(1) == 0)
    def _():
        xbuf_ref[0:8, :] = jnp.zeros((8, 1024), F32)
        state_ref[...] = jnp.zeros_like(state_ref)

    yield
    xbuf_ref[8:8 + tb, :] = zb_ref[:, 512:1536]
    conv = cb_ref[...]
    for i in range(4):
        conv = conv + cw_ref[i:i + 1, :] * xbuf_ref[5 + i:5 + i + tb, :]
    xbuf_ref[0:8, :] = xbuf_ref[tb:tb + 8, :]
    xbc = _silu(conv)
    xs = xbc[:, 0:512]
    bm = xbc[:, 512:768]
    cm = xbc[:, 768:1024]
    yield

    lane = _iota2((1, LANES), 1)
    dt = _softplus(zb_ref[:, 1536:1664] + dtb_ref[...])
    a_neg = jnp.where(lane < SSM_HEADS, -jnp.exp(alog_ref[...]), 0.0)
    acs = _dot_split_l(_chunk_tri(tb), dt * a_neg)
    expand = jnp.where(_iota2((LANES, 512), 0) == _iota2((LANES, 512), 1) // CHUNK, 1.0, 0.0)
    dt_e = _dot(dt, expand)
    acs_e = _dot_split(acs, expand)
    xdt = xs * dt_e
    yield

    n_chunks = tb // CHUNK
    causal = _iota2((CHUNK, 512), 0) >= _iota2((CHUNK, 512), 1) % CHUNK
    pair_head = _iota2((CHUNK, LANES), 1) // CHUNK
    spread = jnp.where(_iota2((CHUNK, 512), 0) == _iota2((CHUNK, 512), 1) % CHUNK, 1.0, 0.0)
    own_head = _iota2((SSM_HEADS, 512), 0) == _iota2((SSM_HEADS, 512), 1) // CHUNK
    rows_of = lambda c: slice(c * CHUNK, (c + 1) * CHUNK)
    grp = lambda g: slice(g * SSM_STATE, (g + 1) * SSM_STATE)

    def stack(z):
        return jnp.concatenate([jnp.where(pair_head == 0, z, 0.0), jnp.where(pair_head == 1, z, 0.0)], axis=0)

    y_intra, upd, e_in, e_last = [], [], [], []
    for c in range(n_chunks):
        rows = rows_of(c)
        acs_ec = acs_e[rows]
        acs_t = acs[rows].T[0:SSM_HEADS]
        acs_row = jnp.sum(jnp.where(own_head, _dot_split(acs_t, spread), 0.0), axis=0, keepdims=True)
        decay = jnp.exp(jnp.where(causal, acs_ec - acs_row, -jnp.inf))
        cb = jnp.concatenate(
            [_dot_nt(cm[rows, grp(g)], jnp.concatenate([bm[rows, grp(g)]] * 4, axis=0)) for g in range(2)], axis=1)
        m = cb * decay
        yield
        xdt_c = xdt[rows]
        y_intra.append([_dot(m[:, p * LANES:(p + 1) * LANES], stack(xdt_c[:, p * LANES:(p + 1) * LANES]))
                        for p in range(4)])
        last_e = acs_ec[CHUNK - 1:CHUNK, :]
        x_out = xdt_c * jnp.exp(last_e - acs_ec)
        upd.append([_dot_tn(bm[rows, grp(g)], x_out[:, g * 256:(g + 1) * 256]) for g in range(2)])
        e_in.append(jnp.exp(acs_ec))
        e_last.append(jnp.exp(last_e))
        yield

    state = [state_ref[g] for g in range(2)]
    for c in range(n_chunks):
        rows = rows_of(c)
        for g in range(2):
            ls = slice(g * 256, (g + 1) * 256)
            y_in = jnp.concatenate(y_intra[c][2 * g:2 * g + 2], axis=1)
            y_ref[rows, ls] = y_in + _dot(cm[rows, grp(g)], state[g]) * e_in[c][:, ls]
            state[g] = state[g] * e_last[c][:, ls] + upd[c][g]
        yield
    for g in range(2):
        state_ref[g] = state[g]

    y = (y_ref[...] + xs * dexp_ref[...]) * _silu(zb_ref[:, 0:512])
    for g in range(2):
        ls = slice(g * 256, (g + 1) * 256)
        yg = y[:, ls]
        ms = jnp.mean(yg * yg, axis=-1, keepdims=True)
        o_ref[:, ls] = (yg * lax.rsqrt(ms + NORM_EPS) * nw_ref[:, ls]).astype(o_ref.dtype)


def _mamba_part(zb, prm, tb):
    bsz, seq, _ = zb.shape
    blk = lambda n: pl.BlockSpec((None, tb, n), lambda b, i: (b, i, 0))
    weights = [prm[n] for n in ("cw", "cb", "dtb", "alog", "dexp", "nw")]
    return _Part(functools.partial(_mamba_kernel, tb), [zb] + weights,
                 [blk(SSM_PAD)] + [_resident(w.shape) for w in weights],
                 [jax.ShapeDtypeStruct((bsz, seq, 512), BF16)], [blk(512)],
                 [pltpu.VMEM((tb + 8, 1024), F32), pltpu.VMEM((2, SSM_STATE, 256), F32), pltpu.VMEM((tb, 512), F32)])


def _attn_kernel(tq, seq, q_ref, k_ref, v_ref, qg_ref, kg_ref, bias_ref, o_ref, kn_ref, vb_ref):
    i = pl.program_id(1)
    head_mean = _group_mean_matrix(512, CHUNK)
    win = tq + ATT_LEFT

    @pl.when(i == 0)
    def _():
        kn_ref[0:ATT_LEFT, :] = jnp.zeros((ATT_LEFT, 512), BF16)
        vb_ref[0:ATT_LEFT, :] = jnp.zeros((ATT_LEFT, 512), BF16)
        for j in range(seq // tq):
            kj = k_ref[j * tq:(j + 1) * tq, :]
            ms = _dot(kj * kj, head_mean)
            kn_ref[ATT_LEFT + j * tq:ATT_LEFT + (j + 1) * tq, :] = (
                kj * lax.rsqrt(ms + NORM_EPS) * kg_ref[...]).astype(BF16)
            vb_ref[ATT_LEFT + j * tq:ATT_LEFT + (j + 1) * tq, :] = v_ref[j * tq:(j + 1) * tq, :].astype(BF16)

    yield
    q = q_ref[...]
    ms = _dot(q * q, head_mean)
    qn = q * lax.rsqrt(ms + NORM_EPS) * qg_ref[...] * (CHUNK ** -0.5 * LOG2_E)
    start = pl.multiple_of(i * tq, tq)
    kwin = kn_ref[pl.ds(start, win), :]
    vwin = vb_ref[pl.ds(start, win), :]
    lane_head = _iota2((tq, LANES), 1) // CHUNK
    yield
    for p in range(4):
        ls = slice(p * LANES, (p + 1) * LANES)
        q2 = jnp.concatenate([jnp.where(lane_head == s, qn[:, ls], 0.0) for s in range(2)], axis=0)
        sc = _dot_nt(q2, kwin[:, ls]) + bias_ref[2 * p:2 * p + 2].reshape(2 * tq, win).astype(F32)
        yield
        e = jnp.exp2(sc - jnp.max(sc, axis=-1, keepdims=True))
        yield
        o2 = _dot(e, vwin[:, ls]) / jnp.sum(e, axis=-1, keepdims=True)
        o_ref[:, ls] = jnp.where(lane_head == 0, o2[:tq], o2[tq:]).astype(o_ref.dtype)
        yield


def _band_bias(rel_bias, tq, win):
    period = tq + win
    m = jnp.arange(period)
    d = jnp.where(m < win, m, m - period) - ATT_LEFT
    f = (rel_bias.astype(F32)[:, jnp.clip(d, -REL_CLIP, REL_CLIP) + REL_CLIP] * LOG2_E).astype(BF16)
    g = jnp.tile(f, (1, tq))[:, :tq * (period - 1)].reshape(-1, tq, period - 1)
    bias = g[:, :, :win]
    left = ATT_LEFT // CHUNK
    qc = left + jnp.arange(tq)[:, None] // CHUNK
    kc = jnp.arange(win)[None, :] // CHUNK
    tiles = []
    for blk in range(ATT_LEFT // tq + 1):
        first_kc = jnp.maximum(qc - left, left - blk * (tq // CHUNK))
        tiles.append(jnp.where((kc <= qc) & (kc >= first_kc), bias, -jnp.inf))
    return jnp.stack(tiles)


def _attention_part(zc, q_gain, k_gain, rel_bias, tq):
    bsz, seq, _ = zc.shape
    win = tq + ATT_LEFT
    bias = _band_bias(rel_bias, tq, win)
    qg = jnp.tile(q_gain, ATT_HEADS)[None, :]
    kg = jnp.tile(k_gain, ATT_HEADS)[None, :]
    full = lambda col: pl.BlockSpec((None, seq, 512), lambda b, i: (b, 0, col), pipeline_mode=pl.Buffered(1))
    last_tile = bias.shape[0] - 1
    bias_spec = pl.BlockSpec((None,) + bias.shape[1:], lambda b, i: (jnp.minimum(i, last_tile), 0, 0, 0))
    blk = pl.BlockSpec((None, tq, 512), lambda b, i: (b, i, 0))
    return _Part(functools.partial(_attn_kernel, tq, seq), [zc, zc, zc, qg, kg, bias],
                 [blk, full(1), full(2), _resident(qg.shape), _resident(kg.shape), bias_spec],
                 [jax.ShapeDtypeStruct((bsz, seq, 512), BF16)], [blk],
                 [pltpu.VMEM((ATT_LEFT + seq, 512), BF16), pltpu.VMEM((ATT_LEFT + seq, 512), BF16)])


def _gla_kernel(tb, zd_ref, gup_ref, gb_ref, nw_ref, o_ref, state_ref, y_ref):
    @pl.when(pl.program_id(1) == 0)
    def _():
        state_ref[...] = jnp.zeros_like(state_ref)

    yield
    q = zd_ref[:, 0:256] * (CHUNK ** -0.5)
    k = zd_ref[:, 256:512]
    v = zd_ref[:, 512:1024]
    log_a = -_softplus(-(_dot_3x(zd_ref[:, 1536:1664], gup_ref[...]) + gb_ref[...])) / GLA_GATE_NORM
    bcum = _dot_split_l(_chunk_tri(tb), log_a)

    k_head = _iota2((CHUNK, 256), 1) // CHUNK
    v_head = _iota2((CHUNK, 512), 1) // LANES
    causal = _iota2((CHUNK, 256), 0) >= _iota2((CHUNK, 256), 1) % CHUNK
    bd = (_iota2((512, 256), 0) // LANES) == (_iota2((512, 256), 1) // CHUNK)
    n_chunks = tb // CHUNK
    qg_all = (q * jnp.exp(bcum)).astype(BF16)
    kg_all = k * jnp.exp(-bcum)
    yield
    qg, att, o_intra, upd, e_last = [], [], [], [], []
    for c in range(n_chunks):
        rows = slice(c * CHUNK, (c + 1) * CHUNK)
        kg_bd = jnp.concatenate([jnp.where(k_head == h, kg_all[rows], 0.0) for h in range(4)], axis=0)
        qg.append(qg_all[rows])
        att.append(jnp.where(causal, _dot_nt(qg[c], kg_bd), 0.0))
        yield
    for c in range(n_chunks):
        rows = slice(c * CHUNK, (c + 1) * CHUNK)
        bc = bcum[rows]
        blast = bc[CHUNK - 1:CHUNK, :]
        v_c = v[rows]
        v_bd = jnp.concatenate([jnp.where(v_head == h, v_c, 0.0) for h in range(4)], axis=0)
        o_intra.append(_dot(att[c], v_bd))
        upd.append(jnp.where(bd, _dot_tn(v_c, k[rows] * jnp.exp(blast - bc)), 0.0))
        e_last.append(jnp.exp(blast))
        yield
    st = state_ref[...]
    for c in range(n_chunks):
        y_ref[c * CHUNK:(c + 1) * CHUNK, :] = o_intra[c] + _dot_nt(qg[c], st)
        st = st * e_last[c] + upd[c]
        yield
    state_ref[...] = st

    o = y_ref[...]
    for h in range(4):
        ls = slice(h * LANES, (h + 1) * LANES)
        oh = o[:, ls]
        ms = jnp.mean(oh * oh, axis=-1, keepdims=True)
        o_ref[:, ls] = (oh * lax.rsqrt(ms + NORM_EPS) * nw_ref[...]
                        * _silu(zd_ref[:, 1024 + h * LANES:1024 + (h + 1) * LANES])).astype(o_ref.dtype)


def _gla_part(zd, prm, tb):
    bsz, seq, _ = zd.shape
    blk = lambda n: pl.BlockSpec((None, tb, n), lambda b, i: (b, i, 0))
    weights = [prm[n] for n in ("gup", "gb", "nw")]
    return _Part(functools.partial(_gla_kernel, tb), [zd] + weights,
                 [blk(GLA_PAD)] + [_resident(w.shape) for w in weights],
                 [jax.ShapeDtypeStruct((bsz, seq, 512), BF16)], [blk(512)],
                 [pltpu.VMEM((512, 256), F32), pltpu.VMEM((tb, 512), F32)])


def _route(tm, hf, wr_ref, idx_ref, gate_ref, rank_ref, cnt_ref, hf_ref, carry_ref):
    @pl.when(pl.program_id(0) == 0)
    def _():
        carry_ref[...] = jnp.zeros_like(carry_ref)

    for j, words in enumerate(_pack_rows(hf)):
        hf_ref[j] = words
    nt = lambda a, b: lax.dot_general(a, b, (((1,), (1,)), ((), ())), preferred_element_type=F32)
    w_hi, w_lo = _hi_lo(wr_ref[...])
    h_hi, h_lo = _hi_lo(hf)
    logits = nt(w_hi, h_hi) + nt(w_lo, h_hi) + nt(w_hi, h_lo)
    e_iota = _iota2((N_EXPERTS, tm), 0)
    m1 = jnp.max(logits, axis=0, keepdims=True)
    i1 = jnp.min(jnp.where(logits == m1, e_iota, N_EXPERTS), axis=0, keepdims=True)
    rest = jnp.where(e_iota == i1, -jnp.inf, logits)
    m2 = jnp.max(rest, axis=0, keepdims=True)
    i2 = jnp.min(jnp.where(rest == m2, e_iota, N_EXPERTS), axis=0, keepdims=True)
    e2 = jnp.exp(m2 - m1)
    gate_ref[0:1, :] = 1.0 / (1.0 + e2)
    gate_ref[1:2, :] = e2 / (1.0 + e2)
    idx_ref[0:1, :] = i1
    idx_ref[1:2, :] = i2
    hit1 = jnp.where(e_iota == i1, 1.0, 0.0)
    hit2 = jnp.where(e_iota == i2, 1.0, 0.0)
    before = jnp.where(_iota2((tm, tm), 0) < _iota2((tm, tm), 1), 1.0, 0.0)
    prior = _dot(hit1 + hit2, before) + carry_ref[:, 0:1]
    rank_ref[0:1, :] = jnp.sum(hit1 * prior, axis=0, keepdims=True).astype(jnp.int32)
    rank_ref[1:2, :] = jnp.sum(hit2 * prior, axis=0, keepdims=True).astype(jnp.int32)
    carry_ref[...] = carry_ref[...] + jnp.sum(hit1 + hit2, axis=1, keepdims=True)
    cnt_ref[...] = carry_ref[...]


def _merge_kernel(tm, route, x_ref, h_ref, oa_ref, ob_ref, oc_ref, od_ref, wg_ref, wb_ref, wo_ref, nf_ref, *rest):
    h = h_ref[...]
    acc = jnp.zeros(x_ref.shape, F32)
    for i, o_ref in enumerate((oa_ref, ob_ref, oc_ref, od_ref)):
        gate = _sigmoid(jnp.dot(h, wg_ref[:, i * D_MODEL:(i + 1) * D_MODEL], preferred_element_type=F32))
        acc = acc + gate * jnp.dot(o_ref[...], wb_ref[i], preferred_element_type=F32)
    x1 = x_ref[...] + jnp.dot(acc.astype(BF16), wo_ref[...], preferred_element_type=F32)
    ms = jnp.mean(x1 * x1, axis=-1, keepdims=True)
    hf = x1 * lax.rsqrt(ms + NORM_EPS) * nf_ref[...]
    if route:
        wr_ref, x1_ref = rest[0], rest[4]
        idx_ref, gate_ref, rank_ref, cnt_ref, hf_ref = rest[5:10]
        _route(tm, hf, wr_ref, idx_ref, gate_ref, rank_ref, cnt_ref, hf_ref, rest[13])
        for w_ref, wb16_ref in zip(rest[1:4], rest[10:13]):
            wb16_ref[...] = w_ref[...].astype(BF16)
    else:
        x1_ref, hf_ref = rest
        hf_ref[...] = hf.astype(BF16)
    x1_ref[...] = x1


def _merge(x2, h, outs, wg, wb, wo, nf, wr_t=None, expert_w=(), tm=512):
    n_tok = x2.shape[0]
    route = wr_t is not None
    steps = n_tok // tm
    row = lambda n: pl.BlockSpec((tm, n), lambda i: (i, 0))
    ins = [x2, h, *outs, wg, wb, wo, nf]
    in_specs = ([row(D_MODEL), row(D_MODEL)] + [row(BRANCH_DIM)] * 4
                + [_resident(wg.shape), _resident(wb.shape), _resident(wo.shape), _resident(nf.shape)])
    out_specs = [row(D_MODEL)]
    out_shape = [jax.ShapeDtypeStruct((n_tok, D_MODEL), F32)]
    scratch = []
    if route:
        ins.append(wr_t)
        in_specs.append(_resident(wr_t.shape))
        col = pl.BlockSpec((2, tm), lambda i: (0, i))
        out_specs += [col, col, col, pl.BlockSpec((N_EXPERTS, LANES), lambda i: (0, 0)),
                      pl.BlockSpec((ROW_SPLIT, tm, SUBROW), lambda i: (0, i, 0))]
        out_shape += [jax.ShapeDtypeStruct((2, n_tok), jnp.int32), jax.ShapeDtypeStruct((2, n_tok), F32),
                      jax.ShapeDtypeStruct((2, n_tok), jnp.int32), jax.ShapeDtypeStruct((N_EXPERTS, LANES), F32),
                      jax.ShapeDtypeStruct((ROW_SPLIT, n_tok, SUBROW), jnp.uint32)]
        scratch = [pltpu.VMEM((N_EXPERTS, LANES), F32)]
        for w in expert_w:
            n, rows, cols = w.shape
            sliced = (steps, n * rows // steps, cols)
            assert n * rows % steps == 0 and sliced[1] % 16 == 0
            ins.append(w.reshape(sliced))
            in_specs.append(pl.BlockSpec((None,) + sliced[1:], lambda i: (i, 0, 0)))
            out_specs.append(pl.BlockSpec((None,) + sliced[1:], lambda i: (i, 0, 0)))
            out_shape.append(jax.ShapeDtypeStruct(sliced, BF16))
    else:
        out_specs.append(row(D_MODEL))
        out_shape.append(jax.ShapeDtypeStruct((n_tok, D_MODEL), BF16))
    res = pl.pallas_call(
        functools.partial(_merge_kernel, tm, route),
        grid=(steps,),
        in_specs=in_specs,
        out_specs=out_specs,
        out_shape=out_shape,
        scratch_shapes=scratch,
        compiler_params=_params("arbitrary" if route else "parallel"),
        name="merge",
    )(*ins)
    n_plain = len(res) - len(expert_w)
    return list(res[:n_plain]) + [wb16.reshape(w.shape) for wb16, w in zip(res[n_plain:], expert_w)]


def _ffn_kernel(tf, x1_ref, hf_ref, w1_ref, w3_ref, w2_ref, o_ref):
    hf = hf_ref[...]
    acc = x1_ref[...]
    for lo in range(0, FFN_DIM, tf):
        cols = slice(lo, min(lo + tf, FFN_DIM))
        a = jnp.dot(hf, w1_ref[:, cols], preferred_element_type=F32)
        b = jnp.dot(hf, w3_ref[:, cols], preferred_element_type=F32)
        acc = acc + jnp.dot((_silu(a) * b).astype(BF16), w2_ref[cols, :], preferred_element_type=F32)
    o_ref[...] = acc


def _ffn(x1, hf, w1, w3, w2, tm=512, tf=512):
    n_tok = x1.shape[0]
    row = pl.BlockSpec((tm, D_MODEL), lambda i: (i, 0))
    return pl.pallas_call(
        functools.partial(_ffn_kernel, tf),
        grid=(n_tok // tm,),
        in_specs=[row, row, _resident(w1.shape), _resident(w3.shape), _resident(w2.shape)],
        out_specs=row,
        out_shape=jax.ShapeDtypeStruct((n_tok, D_MODEL), F32),
        compiler_params=_params("parallel"),
        name="ffn",
    )(x1, hf, w1, w3, w2)


def _gather_rows(table, idx, window=128):
    split, n_table, width = table.shape
    n = idx.shape[0]
    flat_idx = (idx[None, :] + n_table * jnp.arange(split, dtype=jnp.int32)[:, None]).reshape(-1)
    return _gather_subrows(table.reshape(split * n_table, width), flat_idx, window).reshape(split, n, width)


def _gather_subrows(table, idx, window):
    n = idx.shape[0]
    d = table.shape[1]
    mesh = plsc.VectorSubcoreMesh(core_axis_name="core", subcore_axis_name="subcore")

    @functools.partial(pl.kernel, out_type=jax.ShapeDtypeStruct((n, d), table.dtype), mesh=mesh,
                       name="gather_rows")
    def gather(table_hbm, idx_hbm, out_hbm):
        def body(idx_vmem, out_vmem):
            pltpu.sync_copy(table_hbm.at[idx_vmem.at[0]], out_vmem)

        pltpu.emit_pipeline(
            body,
            grid=(n // window,),
            in_specs=[pl.BlockSpec((1, window), index_map=lambda i: (0, i))],
            out_specs=[pl.BlockSpec((window, d), index_map=lambda i: (i, 0))],
            core_axis_name=("core", "subcore"),
            dimension_semantics=(pltpu.PARALLEL,),
        )(idx_hbm, out_hbm)

    return gather(table, idx.reshape(1, n))


def _scatter_rows(table, dest, n_out, window=128):
    split, n_table, width = table.shape
    n = dest.shape[0]
    flat_dest = (dest[None, :] + n_out * jnp.arange(split, dtype=jnp.int32)[:, None]).reshape(1, split * n)
    src_blocks, blocks = n_table // window, n // window
    mesh = plsc.VectorSubcoreMesh(core_axis_name="core", subcore_axis_name="subcore")

    @functools.partial(pl.kernel, out_type=jax.ShapeDtypeStruct((split * n_out, width), table.dtype), mesh=mesh,
                       name="scatter_rows")
    def scatter(table_hbm, dest_hbm, out_hbm):
        def body(rows_vmem, dest_vmem):
            pltpu.sync_copy(rows_vmem, out_hbm.at[dest_vmem.at[0]])

        pltpu.emit_pipeline(
            body,
            grid=(split * blocks,),
            in_specs=[pl.BlockSpec((window, width),
                                   index_map=lambda i: ((i // blocks) * src_blocks + (i % blocks) % src_blocks, 0)),
                      pl.BlockSpec((1, window), index_map=lambda i: (0, i))],
            out_specs=[],
            core_axis_name=("core", "subcore"),
            dimension_semantics=(pltpu.PARALLEL,),
        )(table_hbm, dest_hbm)

    return scatter(table.reshape(split * n_table, width), flat_dest).reshape(split, n_out, width)


def _expert_kernel(tf, ge_ref, nv_ref, x_ref, w1_ref, w3_ref, w2_ref, y_ref):
    g = pl.program_id(0)

    @pl.when(g < nv_ref[0])
    def _():
        x = _unpack_rows([x_ref[j] for j in range(ROW_SPLIT)]).astype(BF16)
        acc = jnp.zeros((MOE_ROWS, D_MODEL), F32)
        for lo in range(0, EXPERT_DIM, tf):
            a = jnp.dot(x, w1_ref[:, lo:lo + tf], preferred_element_type=F32)
            b = jnp.dot(x, w3_ref[:, lo:lo + tf], preferred_element_type=F32)
            acc = acc + jnp.dot((_silu(a) * b).astype(BF16), w2_ref[lo:lo + tf, :], preferred_element_type=F32)
        for j, words in enumerate(_pack_rows(acc)):
            y_ref[j] = words

    @pl.when(g >= nv_ref[0])
    def _():
        y_ref[...] = jnp.zeros_like(y_ref)


def _experts(xg, group_expert, n_valid, w1, w3, w2, tf=512):
    n_rows = xg.shape[1]
    rows = pl.BlockSpec((ROW_SPLIT, MOE_ROWS, SUBROW), lambda g, ge, nv: (0, g, 0))
    return pl.pallas_call(
        functools.partial(_expert_kernel, tf),
        grid_spec=pltpu.PrefetchScalarGridSpec(
            num_scalar_prefetch=2,
            grid=(n_rows // MOE_ROWS,),
            in_specs=[rows,
                      pl.BlockSpec((None, D_MODEL, EXPERT_DIM), lambda g, ge, nv: (ge[g], 0, 0)),
                      pl.BlockSpec((None, D_MODEL, EXPERT_DIM), lambda g, ge, nv: (ge[g], 0, 0)),
                      pl.BlockSpec((None, EXPERT_DIM, D_MODEL), lambda g, ge, nv: (ge[g], 0, 0))],
            out_specs=rows,
        ),
        out_shape=jax.ShapeDtypeStruct((ROW_SPLIT, n_rows, SUBROW), jnp.uint32),
        compiler_params=_params("arbitrary", vmem_limit=BIG_VMEM_LIMIT),
        name="experts",
    )(group_expert, n_valid, xg, w1, w3, w2)


def _combine_kernel(x1_ref, y0_ref, y1_ref, gate_ref, o_ref):
    gate = gate_ref[...]
    y0 = _unpack_rows([y0_ref[j] for j in range(ROW_SPLIT)])
    y1 = _unpack_rows([y1_ref[j] for j in range(ROW_SPLIT)])
    o_ref[...] = x1_ref[...] + gate[:, 0:1] * y0 + gate[:, 1:2] * y1


def _combine(x1, yg, gate_t, tm=512):
    n_tok = x1.shape[0]
    nb = n_tok // tm
    return pl.pallas_call(
        _combine_kernel,
        grid=(nb,),
        in_specs=[pl.BlockSpec((tm, D_MODEL), lambda i: (i, 0)),
                  pl.BlockSpec((ROW_SPLIT, tm, SUBROW), lambda i: (0, i, 0)),
                  pl.BlockSpec((ROW_SPLIT, tm, SUBROW), lambda i: (0, i + nb, 0)),
                  pl.BlockSpec((tm, 2), lambda i: (i, 0))],
        out_specs=pl.BlockSpec((tm, D_MODEL), lambda i: (i, 0)),
        out_shape=jax.ShapeDtypeStruct((n_tok, D_MODEL), F32),
        compiler_params=_params("parallel"),
        name="moe_combine",
    )(x1, yg, yg, gate_t)


def _moe(x1, routing, w1, w3, w2):
    n_tok = x1.shape[0]
    idx, gate, rank, cnt, hf = routing
    counts = cnt[:, 0].astype(jnp.int32)
    padded = (counts + MOE_ROWS - 1) // MOE_ROWS * MOE_ROWS
    end_padded = jnp.cumsum(padded)
    start_padded = end_padded - padded
    start = sum(jnp.where(idx == e, start_padded[e], 0) for e in range(N_EXPERTS))
    dest = (start + rank).reshape(-1)
    n_groups = (n_tok * 2 + MOE_ROWS - 1) // MOE_ROWS + N_EXPERTS
    n_rows = n_groups * MOE_ROWS
    group_row = jnp.arange(n_groups, dtype=jnp.int32)[:, None] * MOE_ROWS
    group_expert = jnp.minimum(jnp.sum(group_row >= end_padded[None, :], axis=1), N_EXPERTS - 1).astype(jnp.int32)
    n_valid = (end_padded[-1:] // MOE_ROWS).astype(jnp.int32)
    fill = jnp.arange(MOE_ROWS, dtype=jnp.int32)[None, :]
    pad_dest = jnp.where(fill < (padded - counts)[:, None], (start_padded + counts)[:, None] + fill,
                         n_rows - MOE_ROWS + fill).reshape(-1)
    xg = _scatter_rows(hf, jnp.concatenate([dest, pad_dest]), n_rows)
    y = _experts(xg, group_expert, n_valid, w1, w3, w2)
    return _combine(x1, _gather_rows(y, dest), gate.T)


def _row(v, pad=0):
    v = v.reshape(1, -1).astype(F32)
    return jnp.pad(v, ((0, 0), (0, pad))) if pad else v


def _rows_at(w, start, total):
    return jnp.pad(w.astype(F32), ((start, total - start - w.shape[0]), (0, 0)))


def kernel(x, w_in, norm_mix, rwkv_mu, rwkv_decay_up, rwkv_w0, rwkv_a_up, rwkv_a0, rwkv_gate_up, rwkv_k_k, rwkv_k_a, rwkv_r_k, rwkv_ln_w, rwkv_ln_b, vres_down, vres_up, vres_v0, ssm_conv_w, ssm_conv_b, ssm_dt_bias, ssm_a_log, ssm_d, ssm_norm_w, att_q_gain, att_k_gain, att_rel_bias, gla_gate_up, gla_gate_bias, gla_norm_w, w_branch, w_out, norm_ffn, ffn_w1, ffn_w3, ffn_w2, moe_router, moe_w1, moe_w3, moe_w2):
    bsz, seq, _ = x.shape
    n_tok = bsz * seq
    depth = w_in.shape[0]
    x2 = x.reshape(n_tok, D_MODEL)
    v_first = None
    o_rw, o_ss, o_at = RWKV_COLS, RWKV_COLS + SSM_COLS, RWKV_COLS + SSM_COLS + ATT_COLS
    o_gl = o_at + GLA_COLS
    for l in range(depth):
        w = w_in[l]
        zeros = lambda n: jnp.zeros((D_MODEL, n), BF16)
        cols = lambda lo, hi: w[:, lo:hi].astype(BF16)
        vd = vres_down[l - 1].astype(BF16) if l > 0 else zeros(32)
        w_mix = jnp.concatenate(
            [cols(0, o_rw), vd, zeros(RWKV_PAD - RWKV_COLS - 32),
             cols(o_rw, o_ss), zeros(SSM_PAD - SSM_COLS),
             cols(o_ss, o_at),
             cols(o_at, o_at + 1024), cols(o_at + 1040, o_gl), cols(o_at + 1024, o_at + 1040),
             zeros(GLA_PAD - GLA_COLS)], axis=1)
        wg = w[:, o_gl:]
        h, za, zb, zc, zd = _inproj(x2, _row(norm_mix[l]), w_mix)
        shp = lambda z: z.reshape(bsz, seq, z.shape[-1])

        rw = dict(mu=_row(rwkv_mu[l], RWKV_PAD - RWKV_COLS),
                  dup=_rows_at(rwkv_decay_up[l], 0, LANES), w0=_row(rwkv_w0[l]),
                  aup=_rows_at(rwkv_a_up[l], 64, LANES), a0=_row(rwkv_a0[l]),
                  gup=_rows_at(rwkv_gate_up[l], 0, 256).astype(BF16),
                  kk=_row(rwkv_k_k[l]), ka=_row(rwkv_k_a[l]), rk=_row(rwkv_r_k[l]),
                  lnw=_row(rwkv_ln_w[l]), lnb=_row(rwkv_ln_b[l]))
        if l > 0:
            rw.update(vup=_rows_at(vres_up[l - 1], 160, 256), v0=_row(vres_v0[l - 1]))
        ssm = dict(cw=ssm_conv_w[l], cb=_row(ssm_conv_b[l]), dtb=_row(ssm_dt_bias[l], LANES - SSM_HEADS),
                   alog=_row(ssm_a_log[l], LANES - SSM_HEADS), dexp=_row(jnp.repeat(ssm_d[l], CHUNK)),
                   nw=_row(ssm_norm_w[l]))
        gla = dict(gup=_rows_at(gla_gate_up[l], 0, LANES), gb=_row(gla_gate_bias[l]), nw=_row(gla_norm_w[l]))
        (o_c,), (o_d,), (o_b,), res_a = _run_parts(
            [_attention_part(shp(zc), att_q_gain[l], att_k_gain[l], att_rel_bias, MIXER_BLOCK),
             _gla_part(shp(zd), gla, MIXER_BLOCK), _mamba_part(shp(zb), ssm, MIXER_BLOCK),
             _rwkv_part(shp(za), v_first, rw, MIXER_BLOCK)],
            grid=(bsz, seq // MIXER_BLOCK), name="mixers")
        o_a = res_a[0]
        if l == 0:
            v_first = res_a[1]
        outs = [o.reshape(n_tok, BRANCH_DIM) for o in (o_a, o_b, o_c, o_d)]
        merge_w = (wg.astype(BF16), w_branch[l].astype(BF16), w_out[l].astype(BF16), _row(norm_ffn[l]))
        if l % 2 == 0:
            x1, hf = _merge(x2, h, outs, *merge_w)
            x2 = _ffn(x1, hf, ffn_w1[l // 2].astype(BF16), ffn_w3[l // 2].astype(BF16),
                      ffn_w2[l // 2].astype(BF16))
        else:
            x1, *rest = _merge(x2, h, outs, *merge_w, wr_t=moe_router[l // 2].T,
                               expert_w=(moe_w1[l // 2], moe_w3[l // 2], moe_w2[l // 2]), tm=MOE_MERGE_ROWS)
            x2 = _moe(x1, rest[:5], *rest[5:])
    return x2.reshape(bsz, seq, D_MODEL)
```

```python
import functools
from typing import Any, NamedTuple

import jax
import jax.numpy as jnp
from jax import lax
from jax.experimental import pallas as pl
from jax.experimental.pallas import tpu as pltpu
from jax.experimental.pallas import tpu_sc as plsc

F32 = jnp.float32
BF16 = jnp.bfloat16

D_MODEL = 1024
CHUNK = 64
BRANCH_DIM = 512
NORM_EPS = 1e-6
LANES = 128
VMEM_LIMIT = 56 * 1024 * 1024
BIG_VMEM_LIMIT = 61 * 1024 * 1024

LOG2_E = 1.4426950408889634

RWKV_LN_EPS = 64e-5
RWKV_DECAY_SCALE = 0.6065306597126334
RWKV_COLS = 1824
RWKV_PAD = 1920
SSM_COLS = 1544
SSM_PAD = 1664
SSM_HEADS = 8
SSM_STATE = 128
ATT_COLS = 1536
ATT_HEADS = 8
ATT_LEFT = 8 * CHUNK
REL_CLIP = 2 * CHUNK
GLA_COLS = 1552
GLA_PAD = 1664
GLA_GATE_NORM = 16.0
FFN_DIM = 2816
N_EXPERTS = 8
EXPERT_DIM = 3584
MOE_ROWS = 512
ROW_SPLIT = 2
SUBROW = D_MODEL // (2 * ROW_SPLIT)


def _dot(a, b):
    return jnp.dot(a.astype(BF16), b.astype(BF16), preferred_element_type=F32)


def _dot_nt(a, b):
    return lax.dot_general(a.astype(BF16), b.astype(BF16), (((1,), (1,)), ((), ())),
                           preferred_element_type=F32)


def _dot_tn(a, b):
    return lax.dot_general(a.astype(BF16), b.astype(BF16), (((0,), (0,)), ((), ())),
                           preferred_element_type=F32)


def _hi_lo(a):
    hi = a.astype(BF16)
    return hi, (a - hi.astype(F32)).astype(BF16)


def _dot_3x(a, b):
    a_hi, a_lo = _hi_lo(a)
    b_hi, b_lo = _hi_lo(b)
    return (jnp.dot(a_hi, b_hi, preferred_element_type=F32) + jnp.dot(a_lo, b_hi, preferred_element_type=F32)
            + jnp.dot(a_hi, b_lo, preferred_element_type=F32))


def _dot_split(a, m):
    m = m.astype(BF16)
    hi, lo = _hi_lo(a)
    return jnp.dot(hi, m, preferred_element_type=F32) + jnp.dot(lo, m, preferred_element_type=F32)


def _dot_split_l(m, a):
    m = m.astype(BF16)
    hi, lo = _hi_lo(a)
    return jnp.dot(m, hi, preferred_element_type=F32) + jnp.dot(m, lo, preferred_element_type=F32)


def _pack_rows(x):
    bits = pltpu.bitcast(x.astype(BF16).astype(F32), jnp.uint32)
    half = D_MODEL // 2
    return [(bits[:, p * SUBROW:(p + 1) * SUBROW] & jnp.uint32(0xFFFF0000))
            | (bits[:, half + p * SUBROW:half + (p + 1) * SUBROW] >> jnp.uint32(16)) for p in range(ROW_SPLIT)]


def _unpack_rows(planes):
    hi = [pltpu.bitcast(w & jnp.uint32(0xFFFF0000), F32) for w in planes]
    lo = [pltpu.bitcast(w << jnp.uint32(16), F32) for w in planes]
    return jnp.concatenate(hi + lo, axis=-1)


def _softplus(x):
    return jnp.maximum(x, 0.0) + jnp.log(1.0 + jnp.exp(-jnp.abs(x)))


def _sigmoid(x):
    return 1.0 / (1.0 + jnp.exp(-x))


def _silu(x):
    return x * _sigmoid(x)


def _iota2(shape, axis):
    return lax.broadcasted_iota(jnp.int32, shape, axis)


def _group_mean_matrix(n, group):
    r = _iota2((n, n), 0) // group
    c = _iota2((n, n), 1) // group
    return jnp.where(r == c, 1.0 / group, 0.0).astype(F32)


def _chunk_tri(n):
    r = _iota2((n, n), 0)
    c = _iota2((n, n), 1)
    return jnp.where((r // CHUNK == c // CHUNK) & (r >= c), 1.0, 0.0).astype(F32)


def _resident(shape):
    nd = len(shape)
    return pl.BlockSpec(shape, lambda *_: (0,) * nd, pipeline_mode=pl.Buffered(1))


def _params(*sem, vmem_limit=VMEM_LIMIT):
    return pltpu.CompilerParams(dimension_semantics=sem, vmem_limit_bytes=vmem_limit)


_DONE = object()


class _Part(NamedTuple):
    body: Any
    inputs: list
    in_specs: list
    out_shapes: list
    out_specs: list
    scratch_shapes: list


def _run_parts(parts, grid, name):
    def split(refs, counts):
        out, lo = [], 0
        for n in counts:
            out.append(refs[lo:lo + n])
            lo += n
        return out

    n_in = [len(p.inputs) for p in parts]
    n_out = [len(p.out_shapes) for p in parts]
    n_scr = [len(p.scratch_shapes) for p in parts]

    def kernel(*refs):
        ins = split(refs[:sum(n_in)], n_in)
        outs = split(refs[sum(n_in):sum(n_in) + sum(n_out)], n_out)
        scr = split(refs[sum(n_in) + sum(n_out):], n_scr)
        stages = [part.body(*i, *o, *c) for part, i, o, c in zip(parts, ins, outs, scr)]
        while stages:
            for stage in list(stages):
                if next(stage, _DONE) is _DONE:
                    stages.remove(stage)

    flat = lambda field: [x for p in parts for x in getattr(p, field)]
    res = pl.pallas_call(
        kernel,
        grid=grid,
        in_specs=flat("in_specs"),
        out_specs=flat("out_specs"),
        out_shape=flat("out_shapes"),
        scratch_shapes=flat("scratch_shapes"),
        compiler_params=_params("parallel", "arbitrary", vmem_limit=BIG_VMEM_LIMIT),
        name=name,
    )(*flat("inputs"))
    return split(list(res), n_out)


def _cast_rider(w, steps):
    n, rows, cols = w.shape
    sliced = (steps, n * rows // steps, cols)
    assert n * rows % steps == 0 and sliced[1] % 16 == 0
    return w.reshape(sliced), pl.BlockSpec((None,) + sliced[1:], lambda i: (i, 0, 0)), jax.ShapeDtypeStruct(sliced, BF16)


def _cast_slices(src_refs, dst_refs):
    for src, dst in zip(src_refs, dst_refs):
        dst[...] = src[...].astype(dst.dtype)


MIXER_COLS = (RWKV_PAD, SSM_PAD, ATT_COLS, GLA_PAD)
MIXER_BLOCK = 256


def _inproj_kernel(n_cast, x_ref, g_ref, w_ref, *refs):
    cast_in, (h_ref, *z_refs), cast_out = refs[:n_cast], refs[n_cast:len(refs) - n_cast], refs[len(refs) - n_cast:]
    _cast_slices(cast_in, cast_out)
    x = x_ref[...]
    ms = jnp.mean(x * x, axis=-1, keepdims=True)
    h = (x * lax.rsqrt(ms + NORM_EPS) * g_ref[...]).astype(BF16)
    h_ref[...] = h
    lo = 0
    for pair in (z_refs[:2], z_refs[2:]):
        widths = [z_ref.shape[-1] for z_ref in pair]
        z = jnp.dot(h, w_ref[:, lo:lo + sum(widths)], preferred_element_type=F32)
        pair[0][...] = z[:, :widths[0]]
        pair[1][...] = z[:, widths[0]:]
        lo += sum(widths)


def _inproj(x2, g, w_mix, cast=(), tm=512):
    n_tok = x2.shape[0]
    row = lambda n: pl.BlockSpec((tm, n), lambda i: (i, 0))
    riders = [_cast_rider(w, n_tok // tm) for w in cast]
    res = pl.pallas_call(
        functools.partial(_inproj_kernel, len(cast)),
        grid=(n_tok // tm,),
        in_specs=[row(D_MODEL), _resident((1, D_MODEL)), _resident(w_mix.shape)] + [r[1] for r in riders],
        out_specs=[row(D_MODEL)] + [row(n) for n in MIXER_COLS] + [r[1] for r in riders],
        out_shape=[jax.ShapeDtypeStruct((n_tok, D_MODEL), BF16)]
        + [jax.ShapeDtypeStruct((n_tok, n), F32) for n in MIXER_COLS] + [r[2] for r in riders],
        compiler_params=_params("parallel", vmem_limit=BIG_VMEM_LIMIT if cast else VMEM_LIMIT),
        name="inproj",
    )(x2, g, w_mix, *[r[0] for r in riders])
    n_plain = len(res) - len(cast)
    return list(res[:n_plain]) + [wb16.reshape(w.shape) for wb16, w in zip(res[n_plain:], cast)]


def _rwkv_block(r, k, v, kk, ka, ld, tb, state_ref, y_ref):
    n_chunks = tb // CHUNK
    cum = _dot_split_l(_chunk_tri(tb), ld)
    p_inv = jnp.exp(-cum)
    a_t = -kk * jnp.exp(cum - ld)
    b_t = kk * ka * p_inv
    k_t = k * p_inv
    r_t = r * jnp.exp(cum)
    bf = lambda z: z.astype(BF16)
    a_b, b_b, k_b, r_b, v_b = bf(a_t), bf(b_t), bf(k_t), bf(r_t), bf(v)
    yield

    lane_head = _iota2((CHUNK, LANES), 1) // CHUNK
    row = _iota2((CHUNK, LANES), 0)
    col = _iota2((CHUNK, LANES), 1) % CHUNK
    strict = row > col
    incl = row >= col
    bd_mask = (_iota2((LANES, LANES), 0) // CHUNK) == (_iota2((LANES, LANES), 1) // CHUNK)
    eye = jnp.where(_iota2((LANES, LANES), 0) == _iota2((LANES, LANES), 1), 1.0, 0.0).astype(F32)

    def stack(z):
        return jnp.concatenate([jnp.where(lane_head == 0, z, 0.0), jnp.where(lane_head == 1, z, 0.0)], axis=0)

    def sub(z, u):
        c, p = u
        return z[c * CHUNK:(c + 1) * CHUNK, p * LANES:(p + 1) * LANES]

    def prepare(units):
        gram = {u: _dot_nt(jnp.concatenate([sub(a_b, u), sub(r_b, u)], axis=0),
                           jnp.concatenate([stack(sub(b_b, u)), stack(sub(k_b, u))], axis=0))
                for u in units}
        yield
        a_rb = {u: bf(jnp.where(incl, gram[u][CHUNK:, :LANES], 0.0)) for u in units}
        a_akrk = {u: bf(jnp.concatenate([jnp.where(strict, gram[u][:CHUNK, LANES:], 0.0),
                                         jnp.where(incl, gram[u][CHUNK:, LANES:], 0.0)], axis=0)) for u in units}
        pw = {u: stack(jnp.where(strict, gram[u][:CHUNK, :LANES], 0.0)) for u in units}
        t_bd = {u: eye + pw[u] for u in units}
        pw = {u: _dot(pw[u], pw[u]) for u in units}
        yield
        for _ in range(CHUNK.bit_length() - 3):
            sp = {u: _dot(jnp.concatenate([t_bd[u], pw[u]], axis=0), pw[u]) for u in units}
            t_bd = {u: t_bd[u] + sp[u][:LANES] for u in units}
            pw = {u: sp[u][LANES:] for u in units}
            yield
        t_bd = {u: bf(t_bd[u] + _dot(t_bd[u], pw[u])) for u in units}
        v_bd = {u: stack(sub(v_b, u)) for u in units}
        avyv = {u: _dot(a_akrk[u], v_bd[u]) for u in units}
        yield
        wuv = {u: _dot(t_bd[u], jnp.concatenate([stack(sub(a_b, u)), stack(bf(avyv[u][:CHUNK]))], axis=1))
               for u in units}
        wr = {u: jnp.concatenate([bf(wuv[u][:, :LANES]), sub(r_b, u)], axis=0) for u in units}
        yield
        return a_rb, avyv, wuv, wr

    a_rb, avyv, wuv, wr = yield from prepare([(c, p) for c in range(n_chunks) for p in range(4)])

    h = [state_ref[p] for p in range(4)]
    for c in range(n_chunks):
        last = c * CHUNK + CHUNK - 1
        p_last = jnp.exp(cum[last:last + 1, :])
        hs = [_dot_nt(wr[(c, p)], h[p]) for p in range(4)]
        u2 = [hs[p][:LANES] + wuv[(c, p)][:, LANES:] for p in range(4)]
        u_p = [u2[p][:CHUNK] + u2[p][CHUNK:] for p in range(4)]
        upd = [_dot_tn(jnp.concatenate([u_p[p], sub(v, (c, p))], axis=0),
                       jnp.concatenate([sub(b_t, (c, p)), sub(k_t, (c, p))], axis=0)
                       * p_last[:, p * LANES:(p + 1) * LANES]) for p in range(4)]
        for p in range(4):
            y_ref[c * CHUNK:(c + 1) * CHUNK, p * LANES:(p + 1) * LANES] = (
                hs[p][LANES:] + _dot(a_rb[(c, p)], stack(u_p[p])) + avyv[(c, p)][CHUNK:])
        h = [h[p] * p_last[:, p * LANES:(p + 1) * LANES] + jnp.where(bd_mask, upd[p], 0.0) for p in range(4)]
        yield
    for p in range(4):
        state_ref[p] = h[p]


def _rwkv_kernel(has_vres, tb, *refs):
    if has_vres:
        (za_ref, vfirst_ref, mu_ref, dup_ref, w0_ref, aup_ref, a0_ref, gup_ref, vup_ref, v0_ref,
         kk_ref, ka_ref, rk_ref, lnw_ref, lnb_ref, o_ref, prev_ref, state_ref, y_ref) = refs
    else:
        (za_ref, mu_ref, dup_ref, w0_ref, aup_ref, a0_ref, gup_ref,
         kk_ref, ka_ref, rk_ref, lnw_ref, lnb_ref, o_ref, vraw_ref, prev_ref, state_ref, y_ref) = refs

    @pl.when(pl.program_id(1) == 0)
    def _():
        prev_ref[...] = jnp.zeros_like(prev_ref)
        state_ref[...] = jnp.zeros_like(state_ref)

    yield
    za = za_ref[...]
    shifted = pltpu.roll(za, 1, axis=0)
    shifted = jnp.where(_iota2(za.shape, 0) == 0, prev_ref[7:8, :], shifted)
    prev_ref[...] = za[tb - 8:, :]
    za = za + (shifted - za) * mu_ref[...]

    r = za[:, 0:512]
    k = za[:, 512:1024]
    v = za[:, 1024:1536]
    xwa = za[:, 1536:1664]
    xg = za[:, 1664:1920]
    ld = -RWKV_DECAY_SCALE * _sigmoid(w0_ref[...] + _dot_3x(jnp.tanh(xwa), dup_ref[...]))
    a = _sigmoid(a0_ref[...] + _dot(xwa, aup_ref[...]))
    g = _dot(_sigmoid(xg), gup_ref[...])
    if has_vres:
        v_mix = _sigmoid(v0_ref[...] + _dot(xg, vup_ref[...]))
        v = v + (vfirst_ref[...] - v) * v_mix
    else:
        vraw_ref[...] = v
    kk = k * kk_ref[...]
    k = k * (1.0 + (a - 1.0) * ka_ref[...])
    head_sum = _group_mean_matrix(512, CHUNK) * float(CHUNK)
    kk = kk / jnp.maximum(jnp.sqrt(_dot(kk * kk, head_sum)), 1e-12)
    yield

    yield from _rwkv_block(r, k, v, kk, a, ld, tb, state_ref, y_ref)
    y = y_ref[...]
    head_mean = _group_mean_matrix(512, CHUNK)
    mean = _dot_split(y, head_mean)
    yc = y - mean
    var = _dot(yc * yc, head_mean)
    y = yc * lax.rsqrt(var + RWKV_LN_EPS) * lnw_ref[...] + lnb_ref[...]
    y = y + _dot(r * k * rk_ref[...], head_sum) * v
    o_ref[...] = (y * g).astype(o_ref.dtype)


def _rwkv_part(za, v_first, prm, tb):
    bsz, seq, _ = za.shape
    has_vres = v_first is not None
    blk = lambda n: pl.BlockSpec((None, tb, n), lambda b, i: (b, i, 0))
    names = (["mu", "dup", "w0", "aup", "a0", "gup"] + (["vup", "v0"] if has_vres else [])
             + ["kk", "ka", "rk", "lnw", "lnb"])
    weights = [prm[n] for n in names]
    ins = [za] + ([v_first] if has_vres else []) + weights
    in_specs = [blk(RWKV_PAD)] + ([blk(512)] if has_vres else []) + [_resident(w.shape) for w in weights]
    out_shape = [jax.ShapeDtypeStruct((bsz, seq, 512), BF16)]
    out_specs = [blk(512)]
    if not has_vres:
        out_shape.append(jax.ShapeDtypeStruct((bsz, seq, 512), F32))
        out_specs.append(blk(512))
    return _Part(functools.partial(_rwkv_kernel, has_vres, tb), ins, in_specs, out_shape, out_specs,
                 [pltpu.VMEM((8, RWKV_PAD), F32), pltpu.VMEM((4, LANES, LANES), F32), pltpu.VMEM((tb, 512), F32)])


def _mamba_kernel(tb, zb_ref, cw_ref, cb_ref, dtb_ref, alog_ref, dexp_ref, nw_ref,
                  o_ref, xbuf_ref, state_ref, y_ref):
    @pl.when(pl.program_id(1) == 0)
    def _():
        xbuf_ref[0:8, :] = jnp.zeros((8, 1024), F32)
        state_ref[...] = jnp.zeros_like(state_ref)

    yield
    xbuf_ref[8:8 + tb, :] = zb_ref[:, 512:1536]
    conv = cb_ref[...]
    for i in range(4):
        conv = conv + cw_ref[i:i + 1, :] * xbuf_ref[5 + i:5 + i + tb, :]
    xbuf_ref[0:8, :] = xbuf_ref[tb:tb + 8, :]
    xbc = _silu(conv)
    xs = xbc[:, 0:512]
    bm = xbc[:, 512:768]
    cm = xbc[:, 768:1024]
    yield

    lane = _iota2((1, LANES), 1)
    dt = _softplus(zb_ref[:, 1536:1664] + dtb_ref[...])
    a_neg = jnp.where(lane < SSM_HEADS, -jnp.exp(alog_ref[...]), 0.0)
    acs = _dot_split_l(_chunk_tri(tb), dt * a_neg)
    expand = jnp.where(_iota2((LANES, 512), 0) == _iota2((LANES, 512), 1) // CHUNK, 1.0, 0.0)
    dt_e = _dot(dt, expand)
    acs_e = _dot_split(acs, expand)
    xdt = xs * dt_e
    yield

    n_chunks = tb // CHUNK
    causal = _iota2((CHUNK, 512), 0) >= _iota2((CHUNK, 512), 1) % CHUNK
    pair_head = _iota2((CHUNK, LANES), 1) // CHUNK
    spread = jnp.where(_iota2((CHUNK, 512), 0) == _iota2((CHUNK, 512), 1) % CHUNK, 1.0, 0.0)
    own_head = _iota2((SSM_HEADS, 512), 0) == _iota2((SSM_HEADS, 512), 1) // CHUNK
    rows_of = lambda c: slice(c * CHUNK, (c + 1) * CHUNK)
    grp = lambda g: slice(g * SSM_STATE, (g + 1) * SSM_STATE)

    def stack(z):
        return jnp.concatenate([jnp.where(pair_head == 0, z, 0.0), jnp.where(pair_head == 1, z, 0.0)], axis=0)

    y_intra, upd, e_in, e_last = [], [], [], []
    for c in range(n_chunks):
        rows = rows_of(c)
        acs_ec = acs_e[rows]
        acs_t = acs[rows].T[0:SSM_HEADS]
        acs_row = jnp.sum(jnp.where(own_head, _dot_split(acs_t, spread), 0.0), axis=0, keepdims=True)
        decay = jnp.exp(jnp.where(causal, acs_ec - acs_row, -jnp.inf))
        cb = jnp.concatenate(
            [_dot_nt(cm[rows, grp(g)], jnp.concatenate([bm[rows, grp(g)]] * 4, axis=0)) for g in range(2)], axis=1)
        m = cb * decay
        yield
        xdt_c = xdt[rows]
        y_intra.append([_dot(m[:, p * LANES:(p + 1) * LANES], stack(xdt_c[:, p * LANES:(p + 1) * LANES]))
                        for p in range(4)])
        last_e = acs_ec[CHUNK - 1:CHUNK, :]
        x_out = xdt_c * jnp.exp(last_e - acs_ec)
        upd.append([_dot_tn(bm[rows, grp(g)], x_out[:, g * 256:(g + 1) * 256]) for g in range(2)])
        e_in.append(jnp.exp(acs_ec))
        e_last.append(jnp.exp(last_e))
        yield

    state = [state_ref[g] for g in range(2)]
    for c in range(n_chunks):
        rows = rows_of(c)
        for g in range(2):
            ls = slice(g * 256, (g + 1) * 256)
            y_in = jnp.concatenate(y_intra[c][2 * g:2 * g + 2], axis=1)
            y_ref[rows, ls] = y_in + _dot(cm[rows, grp(g)], state[g]) * e_in[c][:, ls]
            state[g] = state[g] * e_last[c][:, ls] + upd[c][g]
        yield
    for g in range(2):
        state_ref[g] = state[g]

    y = (y_ref[...] + xs * dexp_ref[...]) * _silu(zb_ref[:, 0:512])
    for g in range(2):
        ls = slice(g * 256, (g + 1) * 256)
        yg = y[:, ls]
        ms = jnp.mean(yg * yg, axis=-1, keepdims=True)
        o_ref[:, ls] = (yg * lax.rsqrt(ms + NORM_EPS) * nw_ref[:, ls]).astype(o_ref.dtype)


def _mamba_part(zb, prm, tb):
    bsz, seq, _ = zb.shape
    blk = lambda n: pl.BlockSpec((None, tb, n), lambda b, i: (b, i, 0))
    weights = [prm[n] for n in ("cw", "cb", "dtb", "alog", "dexp", "nw")]
    return _Part(functools.partial(_mamba_kernel, tb), [zb] + weights,
                 [blk(SSM_PAD)] + [_resident(w.shape) for w in weights],
                 [jax.ShapeDtypeStruct((bsz, seq, 512), BF16)], [blk(512)],
                 [pltpu.VMEM((tb + 8, 1024), F32), pltpu.VMEM((2, SSM_STATE, 256), F32), pltpu.VMEM((tb, 512), F32)])


def _attn_kernel(tq, seq, q_ref, k_ref, v_ref, qg_ref, kg_ref, bias_ref, o_ref, kn_ref, vb_ref):
    i = pl.program_id(1)
    head_mean = _group_mean_matrix(512, CHUNK)
    win = tq + ATT_LEFT

    @pl.when(i == 0)
    def _():
        kn_ref[0:ATT_LEFT, :] = jnp.zeros((ATT_LEFT, 512), BF16)
        vb_ref[0:ATT_LEFT, :] = jnp.zeros((ATT_LEFT, 512), BF16)
        for j in range(seq // tq):
            kj = k_ref[j * tq:(j + 1) * tq, :]
            ms = _dot(kj * kj, head_mean)
            kn_ref[ATT_LEFT + j * tq:ATT_LEFT + (j + 1) * tq, :] = (
                kj * lax.rsqrt(ms + NORM_EPS) * kg_ref[...]).astype(BF16)
            vb_ref[ATT_LEFT + j * tq:ATT_LEFT + (j + 1) * tq, :] = v_ref[j * tq:(j + 1) * tq, :].astype(BF16)

    yield
    q = q_ref[...]
    ms = _dot(q * q, head_mean)
    qn = q * lax.rsqrt(ms + NORM_EPS) * qg_ref[...] * (CHUNK ** -0.5 * LOG2_E)
    start = pl.multiple_of(i * tq, tq)
    kwin = kn_ref[pl.ds(start, win), :]
    vwin = vb_ref[pl.ds(start, win), :]
    lane_head = _iota2((tq, LANES), 1) // CHUNK
    yield
    for p in range(4):
        ls = slice(p * LANES, (p + 1) * LANES)
        q2 = jnp.concatenate([jnp.where(lane_head == s, qn[:, ls], 0.0) for s in range(2)], axis=0)
        sc = _dot_nt(q2, kwin[:, ls]) + bias_ref[2 * p:2 * p + 2].reshape(2 * tq, win).astype(F32)
        yield
        e = jnp.exp2(sc - jnp.max(sc, axis=-1, keepdims=True))
        yield
        o2 = _dot(e, vwin[:, ls]) / jnp.sum(e, axis=-1, keepdims=True)
        o_ref[:, ls] = jnp.where(lane_head == 0, o2[:tq], o2[tq:]).astype(o_ref.dtype)
        yield


def _band_bias(rel_bias, tq, win):
    period = tq + win
    m = jnp.arange(period)
    d = jnp.where(m < win, m, m - period) - ATT_LEFT
    f = (rel_bias.astype(F32)[:, jnp.clip(d, -REL_CLIP, REL_CLIP) + REL_CLIP] * LOG2_E).astype(BF16)
    g = jnp.tile(f, (1, tq))[:, :tq * (period - 1)].reshape(-1, tq, period - 1)
    bias = g[:, :, :win]
    left = ATT_LEFT // CHUNK
    qc = left + jnp.arange(tq)[:, None] // CHUNK
    kc = jnp.arange(win)[None, :] // CHUNK
    tiles = []
    for blk in range(ATT_LEFT // tq + 1):
        first_kc = jnp.maximum(qc - left, left - blk * (tq // CHUNK))
        tiles.append(jnp.where((kc <= qc) & (kc >= first_kc), bias, -jnp.inf))
    return jnp.stack(tiles)


def _attention_part(zc, q_gain, k_gain, rel_bias, tq):
    bsz, seq, _ = zc.shape
    win = tq + ATT_LEFT
    bias = _band_bias(rel_bias, tq, win)
    qg = jnp.tile(q_gain, ATT_HEADS)[None, :]
    kg = jnp.tile(k_gain, ATT_HEADS)[None, :]
    full = lambda col: pl.BlockSpec((None, seq, 512), lambda b, i: (b, 0, col), pipeline_mode=pl.Buffered(1))
    last_tile = bias.shape[0] - 1
    bias_spec = pl.BlockSpec((None,) + bias.shape[1:], lambda b, i: (jnp.minimum(i, last_tile), 0, 0, 0))
    blk = pl.BlockSpec((None, tq, 512), lambda b, i: (b, i, 0))
    return _Part(functools.partial(_attn_kernel, tq, seq), [zc, zc, zc, qg, kg, bias],
                 [blk, full(1), full(2), _resident(qg.shape), _resident(kg.shape), bias_spec],
                 [jax.ShapeDtypeStruct((bsz, seq, 512), BF16)], [blk],
                 [pltpu.VMEM((ATT_LEFT + seq, 512), BF16), pltpu.VMEM((ATT_LEFT + seq, 512), BF16)])


def _gla_kernel(tb, zd_ref, gup_ref, gb_ref, nw_ref, o_ref, state_ref, y_ref):
    @pl.when(pl.program_id(1) == 0)
    def _():
        state_ref[...] = jnp.zeros_like(state_ref)

    yield
    q = zd_ref[:, 0:256] * (CHUNK ** -0.5)
    k = zd_ref[:, 256:512]
    v = zd_ref[:, 512:1024]
    log_a = -_softplus(-(_dot_3x(zd_ref[:, 1536:1664], gup_ref[...]) + gb_ref[...])) / GLA_GATE_NORM
    bcum = _dot_split_l(_chunk_tri(tb), log_a)

    k_head = _iota2((CHUNK, 256), 1) // CHUNK
    v_head = _iota2((CHUNK, 512), 1) // LANES
    causal = _iota2((CHUNK, 256), 0) >= _iota2((CHUNK, 256), 1) % CHUNK
    bd = (_iota2((512, 256), 0) // LANES) == (_iota2((512, 256), 1) // CHUNK)
    n_chunks = tb // CHUNK
    qg_all = (q * jnp.exp(bcum)).astype(BF16)
    kg_all = k * jnp.exp(-bcum)
    yield
    qg, att, o_intra, upd, e_last = [], [], [], [], []
    for c in range(n_chunks):
        rows = slice(c * CHUNK, (c + 1) * CHUNK)
        kg_bd = jnp.concatenate([jnp.where(k_head == h, kg_all[rows], 0.0) for h in range(4)], axis=0)
        qg.append(qg_all[rows])
        att.append(jnp.where(causal, _dot_nt(qg[c], kg_bd), 0.0))
        yield
    for c in range(n_chunks):
        rows = slice(c * CHUNK, (c + 1) * CHUNK)
        bc = bcum[rows]
        blast = bc[CHUNK - 1:CHUNK, :]
        v_c = v[rows]
        v_bd = jnp.concatenate([jnp.where(v_head == h, v_c, 0.0) for h in range(4)], axis=0)
        o_intra.append(_dot(att[c], v_bd))
        upd.append(jnp.where(bd, _dot_tn(v_c, k[rows] * jnp.exp(blast - bc)), 0.0))
        e_last.append(jnp.exp(blast))
        yield
    st = state_ref[...]
    for c in range(n_chunks):
        y_ref[c * CHUNK:(c + 1) * CHUNK, :] = o_intra[c] + _dot_nt(qg[c], st)
        st = st * e_last[c] + upd[c]
        yield
    state_ref[...] = st

    o = y_ref[...]
    for h in range(4):
        ls = slice(h * LANES, (h + 1) * LANES)
        oh = o[:, ls]
        ms = jnp.mean(oh * oh, axis=-1, keepdims=True)
        o_ref[:, ls] = (oh * lax.rsqrt(ms + NORM_EPS) * nw_ref[...]
                        * _silu(zd_ref[:, 1024 + h * LANES:1024 + (h + 1) * LANES])).astype(o_ref.dtype)


def _gla_part(zd, prm, tb):
    bsz, seq, _ = zd.shape
    blk = lambda n: pl.BlockSpec((None, tb, n), lambda b, i: (b, i, 0))
    weights = [prm[n] for n in ("gup", "gb", "nw")]
    return _Part(functools.partial(_gla_kernel, tb), [zd] + weights,
                 [blk(GLA_PAD)] + [_resident(w.shape) for w in weights],
                 [jax.ShapeDtypeStruct((bsz, seq, 512), BF16)], [blk(512)],
                 [pltpu.VMEM((512, 256), F32), pltpu.VMEM((tb, 512), F32)])


def _route(tm, hf, wr_ref, idx_ref, gate_ref, rank_ref, cnt_ref, hf_ref, carry_ref):
    @pl.when(pl.program_id(0) == 0)
    def _():
        carry_ref[...] = jnp.zeros_like(carry_ref)

    for j, words in enumerate(_pack_rows(hf)):
        hf_ref[j] = words
    nt = lambda a, b: lax.dot_general(a, b, (((1,), (1,)), ((), ())), preferred_element_type=F32)
    w_hi, w_lo = _hi_lo(wr_ref[...])
    h_hi, h_lo = _hi_lo(hf)
    logits = nt(w_hi, h_hi) + nt(w_lo, h_hi) + nt(w_hi, h_lo)
    e_iota = _iota2((N_EXPERTS, tm), 0)
    m1 = jnp.max(logits, axis=0, keepdims=True)
    i1 = jnp.min(jnp.where(logits == m1, e_iota, N_EXPERTS), axis=0, keepdims=True)
    rest = jnp.where(e_iota == i1, -jnp.inf, logits)
    m2 = jnp.max(rest, axis=0, keepdims=True)
    i2 = jnp.min(jnp.where(rest == m2, e_iota, N_EXPERTS), axis=0, keepdims=True)
    e2 = jnp.exp(m2 - m1)
    gate_ref[0:1, :] = 1.0 / (1.0 + e2)
    gate_ref[1:2, :] = e2 / (1.0 + e2)
    idx_ref[0:1, :] = i1
    idx_ref[1:2, :] = i2
    hit1 = jnp.where(e_iota == i1, 1.0, 0.0)
    hit2 = jnp.where(e_iota == i2, 1.0, 0.0)
    before = jnp.where(_iota2((tm, tm), 0) < _iota2((tm, tm), 1), 1.0, 0.0)
    prior = _dot(hit1 + hit2, before) + carry_ref[:, 0:1]
    rank_ref[0:1, :] = jnp.sum(hit1 * prior, axis=0, keepdims=True).astype(jnp.int32)
    rank_ref[1:2, :] = jnp.sum(hit2 * prior, axis=0, keepdims=True).astype(jnp.int32)
    carry_ref[...] = carry_ref[...] + jnp.sum(hit1 + hit2, axis=1, keepdims=True)
    cnt_ref[...] = carry_ref[...]


def _merge_kernel(tm, route, n_cast, *refs):
    n_in, n_out = (11, 6) if route else (10, 2)
    ins, cast_in = refs[:n_in], refs[n_in:n_in + n_cast]
    outs = refs[n_in + n_cast:n_in + n_cast + n_out]
    cast_out = refs[n_in + n_cast + n_out:n_in + 2 * n_cast + n_out]
    scratch = refs[n_in + 2 * n_cast + n_out:]
    x_ref, h_ref, oa_ref, ob_ref, oc_ref, od_ref, wg_ref, wb_ref, wo_ref, nf_ref = ins[:10]
    _cast_slices(cast_in, cast_out)
    h = h_ref[...]
    acc = jnp.zeros(x_ref.shape, F32)
    for i, o_ref in enumerate((oa_ref, ob_ref, oc_ref, od_ref)):
        gate = _sigmoid(jnp.dot(h, wg_ref[:, i * D_MODEL:(i + 1) * D_MODEL], preferred_element_type=F32))
        acc = acc + gate * jnp.dot(o_ref[...], wb_ref[i], preferred_element_type=F32)
    x1 = x_ref[...] + jnp.dot(acc.astype(BF16), wo_ref[...], preferred_element_type=F32)
    ms = jnp.mean(x1 * x1, axis=-1, keepdims=True)
    hf = x1 * lax.rsqrt(ms + NORM_EPS) * nf_ref[...]
    if route:
        _route(tm, hf, ins[10], *outs[1:], *scratch)
    else:
        outs[1][...] = hf.astype(BF16)
    outs[0][...] = x1


def _merge(x2, h, outs, wg, wb, wo, nf, wr_t=None, cast=(), tm=512):
    n_tok = x2.shape[0]
    route = wr_t is not None
    steps = n_tok // tm
    row = lambda n: pl.BlockSpec((tm, n), lambda i: (i, 0))
    ins = [x2, h, *outs, wg, wb, wo, nf]
    in_specs = ([row(D_MODEL), row(D_MODEL)] + [row(BRANCH_DIM)] * 4
                + [_resident(wg.shape), _resident(wb.shape), _resident(wo.shape), _resident(nf.shape)])
    out_specs = [row(D_MODEL)]
    out_shape = [jax.ShapeDtypeStruct((n_tok, D_MODEL), F32)]
    scratch = []
    if route:
        ins.append(wr_t)
        in_specs.append(_resident(wr_t.shape))
        col = pl.BlockSpec((2, tm), lambda i: (0, i))
        out_specs += [col, col, col, pl.BlockSpec((N_EXPERTS, LANES), lambda i: (0, 0)),
                      pl.BlockSpec((ROW_SPLIT, tm, SUBROW), lambda i: (0, i, 0))]
        out_shape += [jax.ShapeDtypeStruct((2, n_tok), jnp.int32), jax.ShapeDtypeStruct((2, n_tok), F32),
                      jax.ShapeDtypeStruct((2, n_tok), jnp.int32), jax.ShapeDtypeStruct((N_EXPERTS, LANES), F32),
                      jax.ShapeDtypeStruct((ROW_SPLIT, n_tok, SUBROW), jnp.uint32)]
        scratch = [pltpu.VMEM((N_EXPERTS, LANES), F32)]
    else:
        out_specs.append(row(D_MODEL))
        out_shape.append(jax.ShapeDtypeStruct((n_tok, D_MODEL), BF16))
    riders = [_cast_rider(w, steps) for w in cast]
    res = pl.pallas_call(
        functools.partial(_merge_kernel, tm, route, len(cast)),
        grid=(steps,),
        in_specs=in_specs + [r[1] for r in riders],
        out_specs=out_specs + [r[1] for r in riders],
        out_shape=out_shape + [r[2] for r in riders],
        scratch_shapes=scratch,
        compiler_params=_params("arbitrary" if route else "parallel"),
        name="merge",
    )(*ins, *[r[0] for r in riders])
    n_plain = len(res) - len(cast)
    return list(res[:n_plain]) + [wb16.reshape(w.shape) for wb16, w in zip(res[n_plain:], cast)]


def _ffn_kernel(tf, x1_ref, hf_ref, w1_ref, w3_ref, w2_ref, o_ref):
    hf = hf_ref[...]
    acc = x1_ref[...]
    for lo in range(0, FFN_DIM, tf):
        cols = slice(lo, min(lo + tf, FFN_DIM))
        a = jnp.dot(hf, w1_ref[:, cols], preferred_element_type=F32)
        b = jnp.dot(hf, w3_ref[:, cols], preferred_element_type=F32)
        acc = acc + jnp.dot((_silu(a) * b).astype(BF16), w2_ref[cols, :], preferred_element_type=F32)
    o_ref[...] = acc


def _ffn(x1, hf, w1, w3, w2, tm=512, tf=512):
    n_tok = x1.shape[0]
    row = pl.BlockSpec((tm, D_MODEL), lambda i: (i, 0))
    return pl.pallas_call(
        functools.partial(_ffn_kernel, tf),
        grid=(n_tok // tm,),
        in_specs=[row, row, _resident(w1.shape), _resident(w3.shape), _resident(w2.shape)],
        out_specs=row,
        out_shape=jax.ShapeDtypeStruct((n_tok, D_MODEL), F32),
        compiler_params=_params("parallel"),
        name="ffn",
    )(x1, hf, w1, w3, w2)


def _gather_rows(table, idx, window=128):
    split, n_table, width = table.shape
    n = idx.shape[0]
    flat_idx = (idx[None, :] + n_table * jnp.arange(split, dtype=jnp.int32)[:, None]).reshape(-1)
    return _gather_subrows(table.reshape(split * n_table, width), flat_idx, window).reshape(split, n, width)


def _gather_subrows(table, idx, window):
    n = idx.shape[0]
    d = table.shape[1]
    mesh = plsc.VectorSubcoreMesh(core_axis_name="core", subcore_axis_name="subcore")

    @functools.partial(pl.kernel, out_type=jax.ShapeDtypeStruct((n, d), table.dtype), mesh=mesh,
                       name="gather_rows")
    def gather(table_hbm, idx_hbm, out_hbm):
        def body(idx_vmem, out_vmem):
            pltpu.sync_copy(table_hbm.at[idx_vmem.at[0]], out_vmem)

        pltpu.emit_pipeline(
            body,
            grid=(n // window,),
            in_specs=[pl.BlockSpec((1, window), index_map=lambda i: (0, i))],
            out_specs=[pl.BlockSpec((window, d), index_map=lambda i: (i, 0))],
            core_axis_name=("core", "subcore"),
            dimension_semantics=(pltpu.PARALLEL,),
        )(idx_hbm, out_hbm)

    return gather(table, idx.reshape(1, n))


def _scatter_rows(table, dest, n_out, window=128):
    split, n_table, width = table.shape
    n = dest.shape[0]
    flat_dest = (dest[None, :] + n_out * jnp.arange(split, dtype=jnp.int32)[:, None]).reshape(1, split * n)
    src_blocks, blocks = n_table // window, n // window
    mesh = plsc.VectorSubcoreMesh(core_axis_name="core", subcore_axis_name="subcore")

    @functools.partial(pl.kernel, out_type=jax.ShapeDtypeStruct((split * n_out, width), table.dtype), mesh=mesh,
                       name="scatter_rows")
    def scatter(table_hbm, dest_hbm, out_hbm):
        def body(rows_vmem, dest_vmem):
            pltpu.sync_copy(rows_vmem, out_hbm.at[dest_vmem.at[0]])

        pltpu.emit_pipeline(
            body,
            grid=(split * blocks,),
            in_specs=[pl.BlockSpec((window, width),
                                   index_map=lambda i: ((i // blocks) * src_blocks + (i % blocks) % src_blocks, 0)),
                      pl.BlockSpec((1, window), index_map=lambda i: (0, i))],
            out_specs=[],
            core_axis_name=("core", "subcore"),
            dimension_semantics=(pltpu.PARALLEL,),
        )(table_hbm, dest_hbm)

    return scatter(table.reshape(split * n_table, width), flat_dest).reshape(split, n_out, width)


def _expert_kernel(tf, ge_ref, nv_ref, x_ref, w1_ref, w3_ref, w2_ref, y_ref):
    g = pl.program_id(0)

    @pl.when(g < nv_ref[0])
    def _():
        x = _unpack_rows([x_ref[j] for j in range(ROW_SPLIT)]).astype(BF16)
        acc = jnp.zeros((MOE_ROWS, D_MODEL), F32)
        for lo in range(0, EXPERT_DIM, tf):
            a = jnp.dot(x, w1_ref[:, lo:lo + tf], preferred_element_type=F32)
            b = jnp.dot(x, w3_ref[:, lo:lo + tf], preferred_element_type=F32)
            acc = acc + jnp.dot((_silu(a) * b).astype(BF16), w2_ref[lo:lo + tf, :], preferred_element_type=F32)
        for j, words in enumerate(_pack_rows(acc)):
            y_ref[j] = words

    @pl.when(g >= nv_ref[0])
    def _():
        y_ref[...] = jnp.zeros_like(y_ref)


def _experts(xg, group_expert, n_valid, w1, w3, w2, tf=512):
    n_rows = xg.shape[1]
    rows = pl.BlockSpec((ROW_SPLIT, MOE_ROWS, SUBROW), lambda g, ge, nv: (0, g, 0))
    return pl.pallas_call(
        functools.partial(_expert_kernel, tf),
        grid_spec=pltpu.PrefetchScalarGridSpec(
            num_scalar_prefetch=2,
            grid=(n_rows // MOE_ROWS,),
            in_specs=[rows,
                      pl.BlockSpec((None, D_MODEL, EXPERT_DIM), lambda g, ge, nv: (ge[g], 0, 0)),
                      pl.BlockSpec((None, D_MODEL, EXPERT_DIM), lambda g, ge, nv: (ge[g], 0, 0)),
                      pl.BlockSpec((None, EXPERT_DIM, D_MODEL), lambda g, ge, nv: (ge[g], 0, 0))],
            out_specs=rows,
        ),
        out_shape=jax.ShapeDtypeStruct((ROW_SPLIT, n_rows, SUBROW), jnp.uint32),
        compiler_params=_params("arbitrary", vmem_limit=BIG_VMEM_LIMIT),
        name="experts",
    )(group_expert, n_valid, xg, w1, w3, w2)


def _combine_kernel(x1_ref, y0_ref, y1_ref, gate_ref, o_ref):
    gate = gate_ref[...]
    y0 = _unpack_rows([y0_ref[j] for j in range(ROW_SPLIT)])
    y1 = _unpack_rows([y1_ref[j] for j in range(ROW_SPLIT)])
    o_ref[...] = x1_ref[...] + gate[:, 0:1] * y0 + gate[:, 1:2] * y1


def _combine(x1, yg, gate_t, tm=512):
    n_tok = x1.shape[0]
    nb = n_tok // tm
    return pl.pallas_call(
        _combine_kernel,
        grid=(nb,),
        in_specs=[pl.BlockSpec((tm, D_MODEL), lambda i: (i, 0)),
                  pl.BlockSpec((ROW_SPLIT, tm, SUBROW), lambda i: (0, i, 0)),
                  pl.BlockSpec((ROW_SPLIT, tm, SUBROW), lambda i: (0, i + nb, 0)),
                  pl.BlockSpec((tm, 2), lambda i: (i, 0))],
        out_specs=pl.BlockSpec((tm, D_MODEL), lambda i: (i, 0)),
        out_shape=jax.ShapeDtypeStruct((n_tok, D_MODEL), F32),
        compiler_params=_params("parallel"),
        name="moe_combine",
    )(x1, yg, yg, gate_t)


def _moe(x1, routing, w1, w3, w2):
    n_tok = x1.shape[0]
    idx, gate, rank, cnt, hf = routing
    counts = cnt[:, 0].astype(jnp.int32)
    padded = (counts + MOE_ROWS - 1) // MOE_ROWS * MOE_ROWS
    end_padded = jnp.cumsum(padded)
    start_padded = end_padded - padded
    start = sum(jnp.where(idx == e, start_padded[e], 0) for e in range(N_EXPERTS))
    dest = (start + rank).reshape(-1)
    n_groups = (n_tok * 2 + MOE_ROWS - 1) // MOE_ROWS + N_EXPERTS
    n_rows = n_groups * MOE_ROWS
    group_row = jnp.arange(n_groups, dtype=jnp.int32)[:, None] * MOE_ROWS
    group_expert = jnp.minimum(jnp.sum(group_row >= end_padded[None, :], axis=1), N_EXPERTS - 1).astype(jnp.int32)
    n_valid = (end_padded[-1:] // MOE_ROWS).astype(jnp.int32)
    fill = jnp.arange(MOE_ROWS, dtype=jnp.int32)[None, :]
    pad_dest = jnp.where(fill < (padded - counts)[:, None], (start_padded + counts)[:, None] + fill,
                         n_rows - MOE_ROWS + fill).reshape(-1)
    xg = _scatter_rows(hf, jnp.concatenate([dest, pad_dest]), n_rows)
    y = _experts(xg, group_expert, n_valid, w1, w3, w2)
    return _combine(x1, _gather_rows(y, dest), gate.T)


def _row(v, pad=0):
    v = v.reshape(1, -1).astype(F32)
    return jnp.pad(v, ((0, 0), (0, pad))) if pad else v


def _rows_at(w, start, total):
    return jnp.pad(w.astype(F32), ((start, total - start - w.shape[0]), (0, 0)))


def kernel(x, w_in, norm_mix, rwkv_mu, rwkv_decay_up, rwkv_w0, rwkv_a_up, rwkv_a0, rwkv_gate_up, rwkv_k_k, rwkv_k_a, rwkv_r_k, rwkv_ln_w, rwkv_ln_b, vres_down, vres_up, vres_v0, ssm_conv_w, ssm_conv_b, ssm_dt_bias, ssm_a_log, ssm_d, ssm_norm_w, att_q_gain, att_k_gain, att_rel_bias, gla_gate_up, gla_gate_bias, gla_norm_w, w_branch, w_out, norm_ffn, ffn_w1, ffn_w3, ffn_w2, moe_router, moe_w1, moe_w3, moe_w2):
    bsz, seq, _ = x.shape
    n_tok = bsz * seq
    depth = w_in.shape[0]
    x2 = x.reshape(n_tok, D_MODEL)
    v_first = None
    o_rw, o_ss, o_at = RWKV_COLS, RWKV_COLS + SSM_COLS, RWKV_COLS + SSM_COLS + ATT_COLS
    o_gl = o_at + GLA_COLS
    for l in range(depth):
        w = w_in[l]
        zeros = lambda n: jnp.zeros((D_MODEL, n), BF16)
        cols = lambda lo, hi: w[:, lo:hi].astype(BF16)
        vd = vres_down[l - 1].astype(BF16) if l > 0 else zeros(32)
        w_mix = jnp.concatenate(
            [cols(0, o_rw), vd, zeros(RWKV_PAD - RWKV_COLS - 32),
             cols(o_rw, o_ss), zeros(SSM_PAD - SSM_COLS),
             cols(o_ss, o_at),
             cols(o_at, o_at + 1024), cols(o_at + 1040, o_gl), cols(o_at + 1024, o_at + 1040),
             zeros(GLA_PAD - GLA_COLS)], axis=1)
        wg = w[:, o_gl:]
        moe = l % 2 == 1
        h, za, zb, zc, zd, *w3_b = _inproj(x2, _row(norm_mix[l]), w_mix, cast=(moe_w3[l // 2],) if moe else ())
        shp = lambda z: z.reshape(bsz, seq, z.shape[-1])

        rw = dict(mu=_row(rwkv_mu[l], RWKV_PAD - RWKV_COLS),
                  dup=_rows_at(rwkv_decay_up[l], 0, LANES), w0=_row(rwkv_w0[l]),
                  aup=_rows_at(rwkv_a_up[l], 64, LANES), a0=_row(rwkv_a0[l]),
                  gup=_rows_at(rwkv_gate_up[l], 0, 256).astype(BF16),
                  kk=_row(rwkv_k_k[l]), ka=_row(rwkv_k_a[l]), rk=_row(rwkv_r_k[l]),
                  lnw=_row(rwkv_ln_w[l]), lnb=_row(rwkv_ln_b[l]))
        if l > 0:
            rw.update(vup=_rows_at(vres_up[l - 1], 160, 256), v0=_row(vres_v0[l - 1]))
        ssm = dict(cw=ssm_conv_w[l], cb=_row(ssm_conv_b[l]), dtb=_row(ssm_dt_bias[l], LANES - SSM_HEADS),
                   alog=_row(ssm_a_log[l], LANES - SSM_HEADS), dexp=_row(jnp.repeat(ssm_d[l], CHUNK)),
                   nw=_row(ssm_norm_w[l]))
        gla = dict(gup=_rows_at(gla_gate_up[l], 0, LANES), gb=_row(gla_gate_bias[l]), nw=_row(gla_norm_w[l]))
        (o_c,), (o_d,), (o_b,), res_a = _run_parts(
            [_attention_part(shp(zc), att_q_gain[l], att_k_gain[l], att_rel_bias, MIXER_BLOCK),
             _gla_part(shp(zd), gla, MIXER_BLOCK), _mamba_part(shp(zb), ssm, MIXER_BLOCK),
             _rwkv_part(shp(za), v_first, rw, MIXER_BLOCK)],
            grid=(bsz, seq // MIXER_BLOCK), name="mixers")
        o_a = res_a[0]
        if l == 0:
            v_first = res_a[1]
        outs = [o.reshape(n_tok, BRANCH_DIM) for o in (o_a, o_b, o_c, o_d)]
        merge_w = (wg.astype(BF16), w_branch[l].astype(BF16), w_out[l].astype(BF16), _row(norm_ffn[l]))
        if not moe:
            x1, hf, *w1_b = _merge(x2, h, outs, *merge_w, cast=(moe_w1[l // 2],) if l + 1 < depth else ())
            x2 = _ffn(x1, hf, ffn_w1[l // 2].astype(BF16), ffn_w3[l // 2].astype(BF16),
                      ffn_w2[l // 2].astype(BF16))
        else:
            x1, *routing, w2_b = _merge(x2, h, outs, *merge_w, wr_t=moe_router[l // 2].T, cast=(moe_w2[l // 2],))
            x2 = _moe(x1, routing, w1_b[0], w3_b[0], w2_b)
    return x2.reshape(bsz, seq, D_MODEL)
```

```python
import functools
from typing import Any, NamedTuple

import jax
import jax.numpy as jnp
from jax import lax
from jax.experimental import pallas as pl
from jax.experimental.pallas import tpu as pltpu
from jax.experimental.pallas import tpu_sc as plsc

F32 = jnp.float32
BF16 = jnp.bfloat16

D_MODEL = 1024
CHUNK = 64
BRANCH_DIM = 512
NORM_EPS = 1e-6
LANES = 128
VMEM_LIMIT = 56 * 1024 * 1024
BIG_VMEM_LIMIT = 61 * 1024 * 1024

LOG2_E = 1.4426950408889634

RWKV_LN_EPS = 64e-5
RWKV_DECAY_SCALE = 0.6065306597126334
RWKV_COLS = 1824
RWKV_PAD = 1920
SSM_COLS = 1544
SSM_PAD = 1664
SSM_HEADS = 8
SSM_STATE = 128
ATT_COLS = 1536
ATT_HEADS = 8
ATT_LEFT = 8 * CHUNK
REL_CLIP = 2 * CHUNK
GLA_COLS = 1552
GLA_PAD = 1664
GLA_GATE_NORM = 16.0
FFN_DIM = 2816
N_EXPERTS = 8
EXPERT_DIM = 3584
MOE_ROWS = 512
ROW_SPLIT = 2
SUBROW = D_MODEL // (2 * ROW_SPLIT)


def _dot(a, b):
    return jnp.dot(a.astype(BF16), b.astype(BF16), preferred_element_type=F32)


def _dot_nt(a, b):
    return lax.dot_general(a.astype(BF16), b.astype(BF16), (((1,), (1,)), ((), ())),
                           preferred_element_type=F32)


def _dot_tn(a, b):
    return lax.dot_general(a.astype(BF16), b.astype(BF16), (((0,), (0,)), ((), ())),
                           preferred_element_type=F32)


def _hi_lo(a):
    hi = a.astype(BF16)
    return hi, (a - hi.astype(F32)).astype(BF16)


def _dot_3x(a, b):
    a_hi, a_lo = _hi_lo(a)
    b_hi, b_lo = _hi_lo(b)
    return (jnp.dot(a_hi, b_hi, preferred_element_type=F32) + jnp.dot(a_lo, b_hi, preferred_element_type=F32)
            + jnp.dot(a_hi, b_lo, preferred_element_type=F32))


def _dot_split(a, m):
    m = m.astype(BF16)
    hi, lo = _hi_lo(a)
    return jnp.dot(hi, m, preferred_element_type=F32) + jnp.dot(lo, m, preferred_element_type=F32)


def _dot_split_l(m, a):
    m = m.astype(BF16)
    hi, lo = _hi_lo(a)
    return jnp.dot(m, hi, preferred_element_type=F32) + jnp.dot(m, lo, preferred_element_type=F32)


def _pack_rows(x):
    bits = pltpu.bitcast(x.astype(BF16).astype(F32), jnp.uint32)
    half = D_MODEL // 2
    return [(bits[:, p * SUBROW:(p + 1) * SUBROW] & jnp.uint32(0xFFFF0000))
            | (bits[:, half + p * SUBROW:half + (p + 1) * SUBROW] >> jnp.uint32(16)) for p in range(ROW_SPLIT)]


def _unpack_rows(planes):
    hi = [pltpu.bitcast(w & jnp.uint32(0xFFFF0000), F32) for w in planes]
    lo = [pltpu.bitcast(w << jnp.uint32(16), F32) for w in planes]
    return jnp.concatenate(hi + lo, axis=-1)


def _softplus(x):
    return jnp.maximum(x, 0.0) + jnp.log(1.0 + jnp.exp(-jnp.abs(x)))


def _sigmoid(x):
    return 1.0 / (1.0 + jnp.exp(-x))


def _silu(x):
    return x * _sigmoid(x)


def _iota2(shape, axis):
    return lax.broadcasted_iota(jnp.int32, shape, axis)


def _group_mean_matrix(n, group):
    r = _iota2((n, n), 0) // group
    c = _iota2((n, n), 1) // group
    return jnp.where(r == c, 1.0 / group, 0.0).astype(F32)


def _chunk_tri(n):
    r = _iota2((n, n), 0)
    c = _iota2((n, n), 1)
    return jnp.where((r // CHUNK == c // CHUNK) & (r >= c), 1.0, 0.0).astype(F32)


def _resident(shape):
    nd = len(shape)
    return pl.BlockSpec(shape, lambda *_: (0,) * nd, pipeline_mode=pl.Buffered(1))


def _params(*sem, vmem_limit=VMEM_LIMIT):
    return pltpu.CompilerParams(dimension_semantics=sem, vmem_limit_bytes=vmem_limit)


_DONE = object()


class _Part(NamedTuple):
    body: Any
    inputs: list
    in_specs: list
    out_shapes: list
    out_specs: list
    scratch_shapes: list


def _run_parts(parts, grid, name):
    def split(refs, counts):
        out, lo = [], 0
        for n in counts:
            out.append(refs[lo:lo + n])
            lo += n
        return out

    n_in = [len(p.inputs) for p in parts]
    n_out = [len(p.out_shapes) for p in parts]
    n_scr = [len(p.scratch_shapes) for p in parts]

    def kernel(*refs):
        ins = split(refs[:sum(n_in)], n_in)
        outs = split(refs[sum(n_in):sum(n_in) + sum(n_out)], n_out)
        scr = split(refs[sum(n_in) + sum(n_out):], n_scr)
        stages = [part.body(*i, *o, *c) for part, i, o, c in zip(parts, ins, outs, scr)]
        while stages:
            for stage in list(stages):
                if next(stage, _DONE) is _DONE:
                    stages.remove(stage)

    flat = lambda field: [x for p in parts for x in getattr(p, field)]
    res = pl.pallas_call(
        kernel,
        grid=grid,
        in_specs=flat("in_specs"),
        out_specs=flat("out_specs"),
        out_shape=flat("out_shapes"),
        scratch_shapes=flat("scratch_shapes"),
        compiler_params=_params("parallel", "arbitrary", vmem_limit=BIG_VMEM_LIMIT),
        name=name,
    )(*flat("inputs"))
    return split(list(res), n_out)


def _cast_rider(w, steps):
    n, rows, cols = w.shape
    sliced = (steps, n * rows // steps, cols)
    assert n * rows % steps == 0 and sliced[1] % 16 == 0
    return w.reshape(sliced), pl.BlockSpec((None,) + sliced[1:], lambda i: (i, 0, 0)), jax.ShapeDtypeStruct(sliced, BF16)


def _cast_slices(src_refs, dst_refs):
    for src, dst in zip(src_refs, dst_refs):
        dst[...] = src[...].astype(dst.dtype)


MIXER_COLS = (RWKV_PAD, SSM_PAD, ATT_COLS, GLA_PAD)
MIXER_BLOCK = 256


def _inproj_kernel(n_cast, x_ref, g_ref, w_ref, *refs):
    cast_in, (h_ref, *z_refs), cast_out = refs[:n_cast], refs[n_cast:len(refs) - n_cast], refs[len(refs) - n_cast:]
    _cast_slices(cast_in, cast_out)
    x = x_ref[...]
    ms = jnp.mean(x * x, axis=-1, keepdims=True)
    h = (x * lax.rsqrt(ms + NORM_EPS) * g_ref[...]).astype(BF16)
    h_ref[...] = h
    lo = 0
    for pair in (z_refs[:2], z_refs[2:]):
        widths = [z_ref.shape[-1] for z_ref in pair]
        z = jnp.dot(h, w_ref[:, lo:lo + sum(widths)], preferred_element_type=F32)
        pair[0][...] = z[:, :widths[0]]
        pair[1][...] = z[:, widths[0]:]
        lo += sum(widths)


def _inproj(x2, g, w_mix, cast=(), tm=512):
    n_tok = x2.shape[0]
    row = lambda n: pl.BlockSpec((tm, n), lambda i: (i, 0))
    riders = [_cast_rider(w, n_tok // tm) for w in cast]
    res = pl.pallas_call(
        functools.partial(_inproj_kernel, len(cast)),
        grid=(n_tok // tm,),
        in_specs=[row(D_MODEL), _resident((1, D_MODEL)), _resident(w_mix.shape)] + [r[1] for r in riders],
        out_specs=[row(D_MODEL)] + [row(n) for n in MIXER_COLS] + [r[1] for r in riders],
        out_shape=[jax.ShapeDtypeStruct((n_tok, D_MODEL), BF16)]
        + [jax.ShapeDtypeStruct((n_tok, n), F32) for n in MIXER_COLS] + [r[2] for r in riders],
        compiler_params=_params("parallel", vmem_limit=BIG_VMEM_LIMIT if cast else VMEM_LIMIT),
        name="inproj",
    )(x2, g, w_mix, *[r[0] for r in riders])
    n_plain = len(res) - len(cast)
    return list(res[:n_plain]) + [wb16.reshape(w.shape) for wb16, w in zip(res[n_plain:], cast)]


def _rwkv_block(r, k, v, kk, ka, ld, tb, state_ref, y_ref):
    n_chunks = tb // CHUNK
    cum = _dot_split_l(_chunk_tri(tb), ld)
    p_inv = jnp.exp(-cum)
    a_t = -kk * jnp.exp(cum - ld)
    b_t = kk * ka * p_inv
    k_t = k * p_inv
    r_t = r * jnp.exp(cum)
    bf = lambda z: z.astype(BF16)
    a_b, b_b, k_b, r_b, v_b = bf(a_t), bf(b_t), bf(k_t), bf(r_t), bf(v)
    yield

    lane_head = _iota2((CHUNK, LANES), 1) // CHUNK
    row = _iota2((CHUNK, LANES), 0)
    col = _iota2((CHUNK, LANES), 1) % CHUNK
    strict = row > col
    incl = row >= col
    bd_mask = (_iota2((LANES, LANES), 0) // CHUNK) == (_iota2((LANES, LANES), 1) // CHUNK)
    eye = jnp.where(_iota2((LANES, LANES), 0) == _iota2((LANES, LANES), 1), 1.0, 0.0).astype(F32)

    def stack(z):
        return jnp.concatenate([jnp.where(lane_head == 0, z, 0.0), jnp.where(lane_head == 1, z, 0.0)], axis=0)

    def sub(z, u):
        c, p = u
        return z[c * CHUNK:(c + 1) * CHUNK, p * LANES:(p + 1) * LANES]

    def prepare(units):
        gram = {u: _dot_nt(jnp.concatenate([sub(a_b, u), sub(r_b, u)], axis=0),
                           jnp.concatenate([stack(sub(b_b, u)), stack(sub(k_b, u))], axis=0))
                for u in units}
        yield
        a_rb = {u: bf(jnp.where(incl, gram[u][CHUNK:, :LANES], 0.0)) for u in units}
        a_akrk = {u: bf(jnp.concatenate([jnp.where(strict, gram[u][:CHUNK, LANES:], 0.0),
                                         jnp.where(incl, gram[u][CHUNK:, LANES:], 0.0)], axis=0)) for u in units}
        pw = {u: stack(jnp.where(strict, gram[u][:CHUNK, :LANES], 0.0)) for u in units}
        t_bd = {u: eye + pw[u] for u in units}
        pw = {u: _dot(pw[u], pw[u]) for u in units}
        yield
        for _ in range(CHUNK.bit_length() - 3):
            sp = {u: _dot(jnp.concatenate([t_bd[u], pw[u]], axis=0), pw[u]) for u in units}
            t_bd = {u: t_bd[u] + sp[u][:LANES] for u in units}
            pw = {u: sp[u][LANES:] for u in units}
            yield
        t_bd = {u: bf(t_bd[u] + _dot(t_bd[u], pw[u])) for u in units}
        v_bd = {u: stack(sub(v_b, u)) for u in units}
        avyv = {u: _dot(a_akrk[u], v_bd[u]) for u in units}
        yield
        wuv = {u: _dot(t_bd[u], jnp.concatenate([stack(sub(a_b, u)), stack(bf(avyv[u][:CHUNK]))], axis=1))
               for u in units}
        wr = {u: jnp.concatenate([bf(wuv[u][:, :LANES]), sub(r_b, u)], axis=0) for u in units}
        yield
        return a_rb, avyv, wuv, wr

    a_rb, avyv, wuv, wr = yield from prepare([(c, p) for c in range(n_chunks) for p in range(4)])

    h = [state_ref[p] for p in range(4)]
    for c in range(n_chunks):
        last = c * CHUNK + CHUNK - 1
        p_last = jnp.exp(cum[last:last + 1, :])
        hs = [_dot_nt(wr[(c, p)], h[p]) for p in range(4)]
        u2 = [hs[p][:LANES] + wuv[(c, p)][:, LANES:] for p in range(4)]
        u_p = [u2[p][:CHUNK] + u2[p][CHUNK:] for p in range(4)]
        upd = [_dot_tn(jnp.concatenate([u_p[p], sub(v, (c, p))], axis=0),
                       jnp.concatenate([sub(b_t, (c, p)), sub(k_t, (c, p))], axis=0)
                       * p_last[:, p * LANES:(p + 1) * LANES]) for p in range(4)]
        for p in range(4):
            y_ref[c * CHUNK:(c + 1) * CHUNK, p * LANES:(p + 1) * LANES] = (
                hs[p][LANES:] + _dot(a_rb[(c, p)], stack(u_p[p])) + avyv[(c, p)][CHUNK:])
        h = [h[p] * p_last[:, p * LANES:(p + 1) * LANES] + jnp.where(bd_mask, upd[p], 0.0) for p in range(4)]
        yield
    for p in range(4):
        state_ref[p] = h[p]


def _rwkv_kernel(has_vres, tb, *refs):
    if has_vres:
        (za_ref, vfirst_ref, mu_ref, dup_ref, w0_ref, aup_ref, a0_ref, gup_ref, vup_ref, v0_ref,
         kk_ref, ka_ref, rk_ref, lnw_ref, lnb_ref, o_ref, prev_ref, state_ref, y_ref) = refs
    else:
        (za_ref, mu_ref, dup_ref, w0_ref, aup_ref, a0_ref, gup_ref,
         kk_ref, ka_ref, rk_ref, lnw_ref, lnb_ref, o_ref, vraw_ref, prev_ref, state_ref, y_ref) = refs

    @pl.when(pl.program_id(1) == 0)
    def _():
        prev_ref[...] = jnp.zeros_like(prev_ref)
        state_ref[...] = jnp.zeros_like(state_ref)

    yield
    za = za_ref[...]
    shifted = pltpu.roll(za, 1, axis=0)
    shifted = jnp.where(_iota2(za.shape, 0) == 0, prev_ref[7:8, :], shifted)
    prev_ref[...] = za[tb - 8:, :]
    za = za + (shifted - za) * mu_ref[...]

    r = za[:, 0:512]
    k = za[:, 512:1024]
    v = za[:, 1024:1536]
    xwa = za[:, 1536:1664]
    xg = za[:, 1664:1920]
    ld = -RWKV_DECAY_SCALE * _sigmoid(w0_ref[...] + _dot_3x(jnp.tanh(xwa), dup_ref[...]))
    a = _sigmoid(a0_ref[...] + _dot(xwa, aup_ref[...]))
    g = _dot(_sigmoid(xg), gup_ref[...])
    if has_vres:
        v_mix = _sigmoid(v0_ref[...] + _dot(xg, vup_ref[...]))
        v = v + (vfirst_ref[...] - v) * v_mix
    else:
        vraw_ref[...] = v
    kk = k * kk_ref[...]
    k = k * (1.0 + (a - 1.0) * ka_ref[...])
    head_sum = _group_mean_matrix(512, CHUNK) * float(CHUNK)
    kk = kk / jnp.maximum(jnp.sqrt(_dot(kk * kk, head_sum)), 1e-12)
    yield

    yield from _rwkv_block(r, k, v, kk, a, ld, tb, state_ref, y_ref)
    y = y_ref[...]
    head_mean = _group_mean_matrix(512, CHUNK)
    mean = _dot_split(y, head_mean)
    yc = y - mean
    var = _dot(yc * yc, head_mean)
    y = yc * lax.rsqrt(var + RWKV_LN_EPS) * lnw_ref[...] + lnb_ref[...]
    y = y + _dot(r * k * rk_ref[...], head_sum) * v
    o_ref[...] = (y * g).astype(o_ref.dtype)


def _rwkv_part(za, v_first, prm, tb):
    bsz, seq, _ = za.shape
    has_vres = v_first is not None
    blk = lambda n: pl.BlockSpec((None, tb, n), lambda b, i: (b, i, 0))
    names = (["mu", "dup", "w0", "aup", "a0", "gup"] + (["vup", "v0"] if has_vres else [])
             + ["kk", "ka", "rk", "lnw", "lnb"])
    weights = [prm[n] for n in names]
    ins = [za] + ([v_first] if has_vres else []) + weights
    in_specs = [blk(RWKV_PAD)] + ([blk(512)] if has_vres else []) + [_resident(w.shape) for w in weights]
    out_shape = [jax.ShapeDtypeStruct((bsz, seq, 512), BF16)]
    out_specs = [blk(512)]
    if not has_vres:
        out_shape.append(jax.ShapeDtypeStruct((bsz, seq, 512), F32))
        out_specs.append(blk(512))
    return _Part(functools.partial(_rwkv_kernel, has_vres, tb), ins, in_specs, out_shape, out_specs,
                 [pltpu.VMEM((8, RWKV_PAD), F32), pltpu.VMEM((4, LANES, LANES), F32), pltpu.VMEM((tb, 512), F32)])


def _mamba_kernel(tb, zb_ref, cw_ref, cb_ref, dtb_ref, alog_ref, dexp_ref, nw_ref,
                  o_ref, xbuf_ref, state_ref, y_ref):
    @pl.when(pl.program_id(1) == 0)
    def _():
        xbuf_ref[0:8, :] = jnp.zeros((8, 1024), F32)
        state_ref[...] = jnp.zeros_like(state_ref)

    yield
    xbuf_ref[8:8 + tb, :] = zb_ref[:, 512:1536]
    conv = cb_ref[...]
    for i in range(4):
        conv = conv + cw_ref[i:i + 1, :] * xbuf_ref[5 + i:5 + i + tb, :]
    xbuf_ref[0:8, :] = xbuf_ref[tb:tb + 8, :]
    xbc = _silu(conv)
    xs = xbc[:, 0:512]
    bm = xbc[:, 512:768]
    cm = xbc[:, 768:1024]
    yield

    lane = _iota2((1, LANES), 1)
    dt = _softplus(zb_ref[:, 1536:1664] + dtb_ref[...])
    a_neg = jnp.where(lane < SSM_HEADS, -jnp.exp(alog_ref[...]), 0.0)
    acs = _dot_split_l(_chunk_tri(tb), dt * a_neg)
    expand = jnp.where(_iota2((LANES, 512), 0) == _iota2((LANES, 512), 1) // CHUNK, 1.0, 0.0)
    dt_e = _dot(dt, expand)
    acs_e = _dot_split(acs, expand)
    xdt = xs * dt_e
    yield

    n_chunks = tb // CHUNK
    causal = _iota2((CHUNK, 512), 0) >= _iota2((CHUNK, 512), 1) % CHUNK
    pair_head = _iota2((CHUNK, LANES), 1) // CHUNK
    spread = jnp.where(_iota2((CHUNK, 512), 0) == _iota2((CHUNK, 512), 1) % CHUNK, 1.0, 0.0)
    own_head = _iota2((SSM_HEADS, 512), 0) == _iota2((SSM_HEADS, 512), 1) // CHUNK
    rows_of = lambda c: slice(c * CHUNK, (c + 1) * CHUNK)
    grp = lambda g: slice(g * SSM_STATE, (g + 1) * SSM_STATE)

    def stack(z):
        return jnp.concatenate([jnp.where(pair_head == 0, z, 0.0), jnp.where(pair_head == 1, z, 0.0)], axis=0)

    y_intra, upd, e_in, e_last = [], [], [], []
    for c in range(n_chunks):
        rows = rows_of(c)
        acs_ec = acs_e[rows]
        acs_t = acs[rows].T[0:SSM_HEADS]
        acs_row = jnp.sum(jnp.where(own_head, _dot_split(acs_t, spread), 0.0), axis=0, keepdims=True)
        decay = jnp.exp(jnp.where(causal, acs_ec - acs_row, -jnp.inf))
        cb = jnp.concatenate(
            [_dot_nt(cm[rows, grp(g)], jnp.concatenate([bm[rows, grp(g)]] * 4, axis=0)) for g in range(2)], axis=1)
        m = cb * decay
        yield
        xdt_c = xdt[rows]
        y_intra.append([_dot(m[:, p * LANES:(p + 1) * LANES], stack(xdt_c[:, p * LANES:(p + 1) * LANES]))
                        for p in range(4)])
        last_e = acs_ec[CHUNK - 1:CHUNK, :]
        x_out = xdt_c * jnp.exp(last_e - acs_ec)
        upd.append([_dot_tn(bm[rows, grp(g)], x_out[:, g * 256:(g + 1) * 256]) for g in range(2)])
        e_in.append(jnp.exp(acs_ec))
        e_last.append(jnp.exp(last_e))
        yield

    state = [state_ref[g] for g in range(2)]
    for c in range(n_chunks):
        rows = rows_of(c)
        for g in range(2):
            ls = slice(g * 256, (g + 1) * 256)
            y_in = jnp.concatenate(y_intra[c][2 * g:2 * g + 2], axis=1)
            y_ref[rows, ls] = y_in + _dot(cm[rows, grp(g)], state[g]) * e_in[c][:, ls]
            state[g] = state[g] * e_last[c][:, ls] + upd[c][g]
        yield
    for g in range(2):
        state_ref[g] = state[g]

    y = (y_ref[...] + xs * dexp_ref[...]) * _silu(zb_ref[:, 0:512])
    for g in range(2):
        ls = slice(g * 256, (g + 1) * 256)
        yg = y[:, ls]
        ms = jnp.mean(yg * yg, axis=-1, keepdims=True)
        o_ref[:, ls] = (yg * lax.rsqrt(ms + NORM_EPS) * nw_ref[:, ls]).astype(o_ref.dtype)


def _mamba_part(zb, prm, tb):
    bsz, seq, _ = zb.shape
    blk = lambda n: pl.BlockSpec((None, tb, n), lambda b, i: (b, i, 0))
    weights = [prm[n] for n in ("cw", "cb", "dtb", "alog", "dexp", "nw")]
    return _Part(functools.partial(_mamba_kernel, tb), [zb] + weights,
                 [blk(SSM_PAD)] + [_resident(w.shape) for w in weights],
                 [jax.ShapeDtypeStruct((bsz, seq, 512), BF16)], [blk(512)],
                 [pltpu.VMEM((tb + 8, 1024), F32), pltpu.VMEM((2, SSM_STATE, 256), F32), pltpu.VMEM((tb, 512), F32)])


def _attn_kernel(tq, qkv_ref, qg_ref, kg_ref, bias_ref, o_ref, kn_ref, vb_ref):
    i = pl.program_id(1)
    head_mean = _group_mean_matrix(512, CHUNK)
    win = tq + ATT_LEFT

    @pl.when(i == 0)
    def _():
        kn_ref[0:ATT_LEFT, :] = jnp.zeros((ATT_LEFT, 512), BF16)
        vb_ref[0:ATT_LEFT, :] = jnp.zeros((ATT_LEFT, 512), BF16)

    yield
    start = pl.multiple_of(i * tq, tq)
    k = qkv_ref[:, 512:1024]
    ms = _dot(k * k, head_mean)
    kn_ref[pl.ds(ATT_LEFT + start, tq), :] = (k * lax.rsqrt(ms + NORM_EPS) * kg_ref[...]).astype(BF16)
    vb_ref[pl.ds(ATT_LEFT + start, tq), :] = qkv_ref[:, 1024:1536].astype(BF16)
    q = qkv_ref[:, 0:512]
    ms = _dot(q * q, head_mean)
    qn = q * lax.rsqrt(ms + NORM_EPS) * qg_ref[...] * (CHUNK ** -0.5 * LOG2_E)
    kwin = kn_ref[pl.ds(start, win), :]
    vwin = vb_ref[pl.ds(start, win), :]
    lane_head = _iota2((tq, LANES), 1) // CHUNK
    yield
    for p in range(4):
        ls = slice(p * LANES, (p + 1) * LANES)
        q2 = jnp.concatenate([jnp.where(lane_head == s, qn[:, ls], 0.0) for s in range(2)], axis=0)
        sc = _dot_nt(q2, kwin[:, ls]) + bias_ref[2 * p:2 * p + 2].reshape(2 * tq, win).astype(F32)
        yield
        e = jnp.exp2(sc - jnp.max(sc, axis=-1, keepdims=True))
        yield
        o2 = _dot(e, vwin[:, ls]) / jnp.sum(e, axis=-1, keepdims=True)
        o_ref[:, ls] = jnp.where(lane_head == 0, o2[:tq], o2[tq:]).astype(o_ref.dtype)
        yield


def _band_bias(rel_bias, tq, win):
    period = tq + win
    m = jnp.arange(period)
    d = jnp.where(m < win, m, m - period) - ATT_LEFT
    f = (rel_bias.astype(F32)[:, jnp.clip(d, -REL_CLIP, REL_CLIP) + REL_CLIP] * LOG2_E).astype(BF16)
    g = jnp.tile(f, (1, tq))[:, :tq * (period - 1)].reshape(-1, tq, period - 1)
    bias = g[:, :, :win]
    left = ATT_LEFT // CHUNK
    qc = left + jnp.arange(tq)[:, None] // CHUNK
    kc = jnp.arange(win)[None, :] // CHUNK
    tiles = []
    for blk in range(ATT_LEFT // tq + 1):
        first_kc = jnp.maximum(qc - left, left - blk * (tq // CHUNK))
        tiles.append(jnp.where((kc <= qc) & (kc >= first_kc), bias, -jnp.inf))
    return jnp.stack(tiles)


def _attention_part(zc, q_gain, k_gain, rel_bias, tq):
    bsz, seq, _ = zc.shape
    win = tq + ATT_LEFT
    bias = _band_bias(rel_bias, tq, win)
    qg = jnp.tile(q_gain, ATT_HEADS)[None, :]
    kg = jnp.tile(k_gain, ATT_HEADS)[None, :]
    last_tile = bias.shape[0] - 1
    bias_spec = pl.BlockSpec((None,) + bias.shape[1:], lambda b, i: (jnp.minimum(i, last_tile), 0, 0, 0))
    blk = lambda n: pl.BlockSpec((None, tq, n), lambda b, i: (b, i, 0))
    return _Part(functools.partial(_attn_kernel, tq), [zc, qg, kg, bias],
                 [blk(ATT_COLS), _resident(qg.shape), _resident(kg.shape), bias_spec],
                 [jax.ShapeDtypeStruct((bsz, seq, 512), BF16)], [blk(512)],
                 [pltpu.VMEM((ATT_LEFT + seq, 512), BF16), pltpu.VMEM((ATT_LEFT + seq, 512), BF16)])


def _gla_kernel(tb, zd_ref, gup_ref, gb_ref, nw_ref, o_ref, state_ref, y_ref):
    @pl.when(pl.program_id(1) == 0)
    def _():
        state_ref[...] = jnp.zeros_like(state_ref)

    yield
    q = zd_ref[:, 0:256] * (CHUNK ** -0.5)
    k = zd_ref[:, 256:512]
    v = zd_ref[:, 512:1024]
    log_a = -_softplus(-(_dot_3x(zd_ref[:, 1536:1664], gup_ref[...]) + gb_ref[...])) / GLA_GATE_NORM
    bcum = _dot_split_l(_chunk_tri(tb), log_a)

    k_head = _iota2((CHUNK, 256), 1) // CHUNK
    v_head = _iota2((CHUNK, 512), 1) // LANES
    causal = _iota2((CHUNK, 256), 0) >= _iota2((CHUNK, 256), 1) % CHUNK
    bd = (_iota2((512, 256), 0) // LANES) == (_iota2((512, 256), 1) // CHUNK)
    n_chunks = tb // CHUNK
    qg_all = (q * jnp.exp(bcum)).astype(BF16)
    kg_all = k * jnp.exp(-bcum)
    yield
    qg, att, o_intra, upd, e_last = [], [], [], [], []
    for c in range(n_chunks):
        rows = slice(c * CHUNK, (c + 1) * CHUNK)
        kg_bd = jnp.concatenate([jnp.where(k_head == h, kg_all[rows], 0.0) for h in range(4)], axis=0)
        qg.append(qg_all[rows])
        att.append(jnp.where(causal, _dot_nt(qg[c], kg_bd), 0.0))
        yield
    for c in range(n_chunks):
        rows = slice(c * CHUNK, (c + 1) * CHUNK)
        bc = bcum[rows]
        blast = bc[CHUNK - 1:CHUNK, :]
        v_c = v[rows]
        v_bd = jnp.concatenate([jnp.where(v_head == h, v_c, 0.0) for h in range(4)], axis=0)
        o_intra.append(_dot(att[c], v_bd))
        upd.append(jnp.where(bd, _dot_tn(v_c, k[rows] * jnp.exp(blast - bc)), 0.0))
        e_last.append(jnp.exp(blast))
        yield
    st = state_ref[...]
    for c in range(n_chunks):
        y_ref[c * CHUNK:(c + 1) * CHUNK, :] = o_intra[c] + _dot_nt(qg[c], st)
        st = st * e_last[c] + upd[c]
        yield
    state_ref[...] = st

    o = y_ref[...]
    for h in range(4):
        ls = slice(h * LANES, (h + 1) * LANES)
        oh = o[:, ls]
        ms = jnp.mean(oh * oh, axis=-1, keepdims=True)
        o_ref[:, ls] = (oh * lax.rsqrt(ms + NORM_EPS) * nw_ref[...]
                        * _silu(zd_ref[:, 1024 + h * LANES:1024 + (h + 1) * LANES])).astype(o_ref.dtype)


def _gla_part(zd, prm, tb):
    bsz, seq, _ = zd.shape
    blk = lambda n: pl.BlockSpec((None, tb, n), lambda b, i: (b, i, 0))
    weights = [prm[n] for n in ("gup", "gb", "nw")]
    return _Part(functools.partial(_gla_kernel, tb), [zd] + weights,
                 [blk(GLA_PAD)] + [_resident(w.shape) for w in weights],
                 [jax.ShapeDtypeStruct((bsz, seq, 512), BF16)], [blk(512)],
                 [pltpu.VMEM((512, 256), F32), pltpu.VMEM((tb, 512), F32)])


def _route(tm, hf, wr_ref, idx_ref, gate_ref, rank_ref, cnt_ref, hf_ref, carry_ref):
    @pl.when(pl.program_id(0) == 0)
    def _():
        carry_ref[...] = jnp.zeros_like(carry_ref)

    for j, words in enumerate(_pack_rows(hf)):
        hf_ref[j] = words
    nt = lambda a, b: lax.dot_general(a, b, (((1,), (1,)), ((), ())), preferred_element_type=F32)
    w_hi, w_lo = _hi_lo(wr_ref[...])
    h_hi, h_lo = _hi_lo(hf)
    logits = nt(w_hi, h_hi) + nt(w_lo, h_hi) + nt(w_hi, h_lo)
    e_iota = _iota2((N_EXPERTS, tm), 0)
    m1 = jnp.max(logits, axis=0, keepdims=True)
    i1 = jnp.min(jnp.where(logits == m1, e_iota, N_EXPERTS), axis=0, keepdims=True)
    rest = jnp.where(e_iota == i1, -jnp.inf, logits)
    m2 = jnp.max(rest, axis=0, keepdims=True)
    i2 = jnp.min(jnp.where(rest == m2, e_iota, N_EXPERTS), axis=0, keepdims=True)
    e2 = jnp.exp(m2 - m1)
    gate_ref[0:1, :] = 1.0 / (1.0 + e2)
    gate_ref[1:2, :] = e2 / (1.0 + e2)
    idx_ref[0:1, :] = i1
    idx_ref[1:2, :] = i2
    hit1 = jnp.where(e_iota == i1, 1.0, 0.0)
    hit2 = jnp.where(e_iota == i2, 1.0, 0.0)
    before = jnp.where(_iota2((tm, tm), 0) < _iota2((tm, tm), 1), 1.0, 0.0)
    prior = _dot(hit1 + hit2, before) + carry_ref[:, 0:1]
    rank_ref[0:1, :] = jnp.sum(hit1 * prior, axis=0, keepdims=True).astype(jnp.int32)
    rank_ref[1:2, :] = jnp.sum(hit2 * prior, axis=0, keepdims=True).astype(jnp.int32)
    carry_ref[...] = carry_ref[...] + jnp.sum(hit1 + hit2, axis=1, keepdims=True)
    cnt_ref[...] = carry_ref[...]


def _merge_kernel(tm, route, n_cast, *refs):
    n_in, n_out = (11, 6) if route else (10, 2)
    ins, cast_in = refs[:n_in], refs[n_in:n_in + n_cast]
    outs = refs[n_in + n_cast:n_in + n_cast + n_out]
    cast_out = refs[n_in + n_cast + n_out:n_in + 2 * n_cast + n_out]
    scratch = refs[n_in + 2 * n_cast + n_out:]
    x_ref, h_ref, oa_ref, ob_ref, oc_ref, od_ref, wg_ref, wb_ref, wo_ref, nf_ref = ins[:10]
    _cast_slices(cast_in, cast_out)
    h = h_ref[...]
    acc = jnp.zeros(x_ref.shape, F32)
    for i, o_ref in enumerate((oa_ref, ob_ref, oc_ref, od_ref)):
        gate = _sigmoid(jnp.dot(h, wg_ref[:, i * D_MODEL:(i + 1) * D_MODEL], preferred_element_type=F32))
        acc = acc + gate * jnp.dot(o_ref[...], wb_ref[i], preferred_element_type=F32)
    x1 = x_ref[...] + jnp.dot(acc.astype(BF16), wo_ref[...], preferred_element_type=F32)
    ms = jnp.mean(x1 * x1, axis=-1, keepdims=True)
    hf = x1 * lax.rsqrt(ms + NORM_EPS) * nf_ref[...]
    if route:
        _route(tm, hf, ins[10], *outs[1:], *scratch)
    else:
        outs[1][...] = hf.astype(BF16)
    outs[0][...] = x1


def _merge(x2, h, outs, wg, wb, wo, nf, wr_t=None, cast=(), tm=512):
    n_tok = x2.shape[0]
    route = wr_t is not None
    steps = n_tok // tm
    row = lambda n: pl.BlockSpec((tm, n), lambda i: (i, 0))
    ins = [x2, h, *outs, wg, wb, wo, nf]
    in_specs = ([row(D_MODEL), row(D_MODEL)] + [row(BRANCH_DIM)] * 4
                + [_resident(wg.shape), _resident(wb.shape), _resident(wo.shape), _resident(nf.shape)])
    out_specs = [row(D_MODEL)]
    out_shape = [jax.ShapeDtypeStruct((n_tok, D_MODEL), F32)]
    scratch = []
    if route:
        ins.append(wr_t)
        in_specs.append(_resident(wr_t.shape))
        col = pl.BlockSpec((2, tm), lambda i: (0, i))
        out_specs += [col, col, col, pl.BlockSpec((N_EXPERTS, LANES), lambda i: (0, 0)),
                      pl.BlockSpec((ROW_SPLIT, tm, SUBROW), lambda i: (0, i, 0))]
        out_shape += [jax.ShapeDtypeStruct((2, n_tok), jnp.int32), jax.ShapeDtypeStruct((2, n_tok), F32),
                      jax.ShapeDtypeStruct((2, n_tok), jnp.int32), jax.ShapeDtypeStruct((N_EXPERTS, LANES), F32),
                      jax.ShapeDtypeStruct((ROW_SPLIT, n_tok, SUBROW), jnp.uint32)]
        scratch = [pltpu.VMEM((N_EXPERTS, LANES), F32)]
    else:
        out_specs.append(row(D_MODEL))
        out_shape.append(jax.ShapeDtypeStruct((n_tok, D_MODEL), BF16))
    riders = [_cast_rider(w, steps) for w in cast]
    res = pl.pallas_call(
        functools.partial(_merge_kernel, tm, route, len(cast)),
        grid=(steps,),
        in_specs=in_specs + [r[1] for r in riders],
        out_specs=out_specs + [r[1] for r in riders],
        out_shape=out_shape + [r[2] for r in riders],
        scratch_shapes=scratch,
        compiler_params=_params("arbitrary" if route else "parallel"),
        name="merge",
    )(*ins, *[r[0] for r in riders])
    n_plain = len(res) - len(cast)
    return list(res[:n_plain]) + [wb16.reshape(w.shape) for wb16, w in zip(res[n_plain:], cast)]


def _ffn_kernel(tf, x1_ref, hf_ref, w1_ref, w3_ref, w2_ref, o_ref):
    hf = hf_ref[...]
    acc = x1_ref[...]
    for lo in range(0, FFN_DIM, tf):
        cols = slice(lo, min(lo + tf, FFN_DIM))
        a = jnp.dot(hf, w1_ref[:, cols], preferred_element_type=F32)
        b = jnp.dot(hf, w3_ref[:, cols], preferred_element_type=F32)
        acc = acc + jnp.dot((_silu(a) * b).astype(BF16), w2_ref[cols, :], preferred_element_type=F32)
    o_ref[...] = acc


def _ffn(x1, hf, w1, w3, w2, tm=512, tf=512):
    n_tok = x1.shape[0]
    row = pl.BlockSpec((tm, D_MODEL), lambda i: (i, 0))
    return pl.pallas_call(
        functools.partial(_ffn_kernel, tf),
        grid=(n_tok // tm,),
        in_specs=[row, row, _resident(w1.shape), _resident(w3.shape), _resident(w2.shape)],
        out_specs=row,
        out_shape=jax.ShapeDtypeStruct((n_tok, D_MODEL), F32),
        compiler_params=_params("parallel"),
        name="ffn",
    )(x1, hf, w1, w3, w2)


def _gather_rows(table, idx, window=128):
    split, n_table, width = table.shape
    n = idx.shape[0]
    flat_idx = (idx[None, :] + n_table * jnp.arange(split, dtype=jnp.int32)[:, None]).reshape(-1)
    return _gather_subrows(table.reshape(split * n_table, width), flat_idx, window).reshape(split, n, width)


def _gather_subrows(table, idx, window):
    n = idx.shape[0]
    d = table.shape[1]
    mesh = plsc.VectorSubcoreMesh(core_axis_name="core", subcore_axis_name="subcore")

    @functools.partial(pl.kernel, out_type=jax.ShapeDtypeStruct((n, d), table.dtype), mesh=mesh,
                       name="gather_rows")
    def gather(table_hbm, idx_hbm, out_hbm):
        def body(idx_vmem, out_vmem):
            pltpu.sync_copy(table_hbm.at[idx_vmem.at[0]], out_vmem)

        pltpu.emit_pipeline(
            body,
            grid=(n // window,),
            in_specs=[pl.BlockSpec((1, window), index_map=lambda i: (0, i))],
            out_specs=[pl.BlockSpec((window, d), index_map=lambda i: (i, 0))],
            core_axis_name=("core", "subcore"),
            dimension_semantics=(pltpu.PARALLEL,),
        )(idx_hbm, out_hbm)

    return gather(table, idx.reshape(1, n))


def _scatter_rows(table, dest, n_out, window=128):
    split, n_table, width = table.shape
    n = dest.shape[0]
    flat_dest = (dest[None, :] + n_out * jnp.arange(split, dtype=jnp.int32)[:, None]).reshape(1, split * n)
    src_blocks, blocks = n_table // window, n // window
    mesh = plsc.VectorSubcoreMesh(core_axis_name="core", subcore_axis_name="subcore")

    @functools.partial(pl.kernel, out_type=jax.ShapeDtypeStruct((split * n_out, width), table.dtype), mesh=mesh,
                       name="scatter_rows")
    def scatter(table_hbm, dest_hbm, out_hbm):
        def body(rows_vmem, dest_vmem):
            pltpu.sync_copy(rows_vmem, out_hbm.at[dest_vmem.at[0]])

        pltpu.emit_pipeline(
            body,
            grid=(split * blocks,),
            in_specs=[pl.BlockSpec((window, width),
                                   index_map=lambda i: ((i // blocks) * src_blocks + (i % blocks) % src_blocks, 0)),
                      pl.BlockSpec((1, window), index_map=lambda i: (0, i))],
            out_specs=[],
            core_axis_name=("core", "subcore"),
            dimension_semantics=(pltpu.PARALLEL,),
        )(table_hbm, dest_hbm)

    return scatter(table.reshape(split * n_table, width), flat_dest).reshape(split, n_out, width)


def _expert_kernel(tf, ge_ref, nv_ref, x_ref, w1_ref, w3_ref, w2_ref, y_ref):
    g = pl.program_id(0)

    @pl.when(g < nv_ref[0])
    def _():
        x = _unpack_rows([x_ref[j] for j in range(ROW_SPLIT)]).astype(BF16)
        acc = jnp.zeros((MOE_ROWS, D_MODEL), F32)
        for lo in range(0, EXPERT_DIM, tf):
            a = jnp.dot(x, w1_ref[:, lo:lo + tf], preferred_element_type=F32)
            b = jnp.dot(x, w3_ref[:, lo:lo + tf], preferred_element_type=F32)
            acc = acc + jnp.dot((_silu(a) * b).astype(BF16), w2_ref[lo:lo + tf, :], preferred_element_type=F32)
        for j, words in enumerate(_pack_rows(acc)):
            y_ref[j] = words

    @pl.when(g >= nv_ref[0])
    def _():
        y_ref[...] = jnp.zeros_like(y_ref)


def _experts(xg, group_expert, n_valid, w1, w3, w2, tf=512):
    n_rows = xg.shape[1]
    rows = pl.BlockSpec((ROW_SPLIT, MOE_ROWS, SUBROW), lambda g, ge, nv: (0, g, 0))
    return pl.pallas_call(
        functools.partial(_expert_kernel, tf),
        grid_spec=pltpu.PrefetchScalarGridSpec(
            num_scalar_prefetch=2,
            grid=(n_rows // MOE_ROWS,),
            in_specs=[rows,
                      pl.BlockSpec((None, D_MODEL, EXPERT_DIM), lambda g, ge, nv: (ge[g], 0, 0)),
                      pl.BlockSpec((None, D_MODEL, EXPERT_DIM), lambda g, ge, nv: (ge[g], 0, 0)),
                      pl.BlockSpec((None, EXPERT_DIM, D_MODEL), lambda g, ge, nv: (ge[g], 0, 0))],
            out_specs=rows,
        ),
        out_shape=jax.ShapeDtypeStruct((ROW_SPLIT, n_rows, SUBROW), jnp.uint32),
        compiler_params=_params("arbitrary", vmem_limit=BIG_VMEM_LIMIT),
        name="experts",
    )(group_expert, n_valid, xg, w1, w3, w2)


def _combine_kernel(x1_ref, y0_ref, y1_ref, gate_ref, o_ref):
    gate = gate_ref[...]
    y0 = _unpack_rows([y0_ref[j] for j in range(ROW_SPLIT)])
    y1 = _unpack_rows([y1_ref[j] for j in range(ROW_SPLIT)])
    o_ref[...] = x1_ref[...] + gate[:, 0:1] * y0 + gate[:, 1:2] * y1


def _combine(x1, yg, gate_t, tm=512):
    n_tok = x1.shape[0]
    nb = n_tok // tm
    return pl.pallas_call(
        _combine_kernel,
        grid=(nb,),
        in_specs=[pl.BlockSpec((tm, D_MODEL), lambda i: (i, 0)),
                  pl.BlockSpec((ROW_SPLIT, tm, SUBROW), lambda i: (0, i, 0)),
                  pl.BlockSpec((ROW_SPLIT, tm, SUBROW), lambda i: (0, i + nb, 0)),
                  pl.BlockSpec((tm, 2), lambda i: (i, 0))],
        out_specs=pl.BlockSpec((tm, D_MODEL), lambda i: (i, 0)),
        out_shape=jax.ShapeDtypeStruct((n_tok, D_MODEL), F32),
        compiler_params=_params("parallel"),
        name="moe_combine",
    )(x1, yg, yg, gate_t)


def _moe(x1, routing, w1, w3, w2):
    n_tok = x1.shape[0]
    idx, gate, rank, cnt, hf = routing
    counts = cnt[:, 0].astype(jnp.int32)
    padded = (counts + MOE_ROWS - 1) // MOE_ROWS * MOE_ROWS
    end_padded = jnp.cumsum(padded)
    start_padded = end_padded - padded
    start = sum(jnp.where(idx == e, start_padded[e], 0) for e in range(N_EXPERTS))
    dest = (start + rank).reshape(-1)
    n_groups = (n_tok * 2 + MOE_ROWS - 1) // MOE_ROWS + N_EXPERTS
    n_rows = n_groups * MOE_ROWS
    group_row = jnp.arange(n_groups, dtype=jnp.int32)[:, None] * MOE_ROWS
    group_expert = jnp.minimum(jnp.sum(group_row >= end_padded[None, :], axis=1), N_EXPERTS - 1).astype(jnp.int32)
    n_valid = (end_padded[-1:] // MOE_ROWS).astype(jnp.int32)
    fill = jnp.arange(MOE_ROWS, dtype=jnp.int32)[None, :]
    pad_dest = jnp.where(fill < (padded - counts)[:, None], (start_padded + counts)[:, None] + fill,
                         n_rows - MOE_ROWS + fill).reshape(-1)
    xg = _scatter_rows(hf, jnp.concatenate([dest, pad_dest]), n_rows)
    y = _experts(xg, group_expert, n_valid, w1, w3, w2)
    return _combine(x1, _gather_rows(y, dest), gate.T)


def _row(v, pad=0):
    v = v.reshape(1, -1).astype(F32)
    return jnp.pad(v, ((0, 0), (0, pad))) if pad else v


def _rows_at(w, start, total):
    return jnp.pad(w.astype(F32), ((start, total - start - w.shape[0]), (0, 0)))


def kernel(x, w_in, norm_mix, rwkv_mu, rwkv_decay_up, rwkv_w0, rwkv_a_up, rwkv_a0, rwkv_gate_up, rwkv_k_k, rwkv_k_a, rwkv_r_k, rwkv_ln_w, rwkv_ln_b, vres_down, vres_up, vres_v0, ssm_conv_w, ssm_conv_b, ssm_dt_bias, ssm_a_log, ssm_d, ssm_norm_w, att_q_gain, att_k_gain, att_rel_bias, gla_gate_up, gla_gate_bias, gla_norm_w, w_branch, w_out, norm_ffn, ffn_w1, ffn_w3, ffn_w2, moe_router, moe_w1, moe_w3, moe_w2):
    bsz, seq, _ = x.shape
    n_tok = bsz * seq
    depth = w_in.shape[0]
    x2 = x.reshape(n_tok, D_MODEL)
    v_first = None
    o_rw, o_ss, o_at = RWKV_COLS, RWKV_COLS + SSM_COLS, RWKV_COLS + SSM_COLS + ATT_COLS
    o_gl = o_at + GLA_COLS
    for l in range(depth):
        w = w_in[l]
        zeros = lambda n: jnp.zeros((D_MODEL, n), BF16)
        cols = lambda lo, hi: w[:, lo:hi].astype(BF16)
        vd = vres_down[l - 1].astype(BF16) if l > 0 else zeros(32)
        w_mix = jnp.concatenate(
            [cols(0, o_rw), vd, zeros(RWKV_PAD - RWKV_COLS - 32),
             cols(o_rw, o_ss), zeros(SSM_PAD - SSM_COLS),
             cols(o_ss, o_at),
             cols(o_at, o_at + 1024), cols(o_at + 1040, o_gl), cols(o_at + 1024, o_at + 1040),
             zeros(GLA_PAD - GLA_COLS)], axis=1)
        wg = w[:, o_gl:]
        moe = l % 2 == 1
        h, za, zb, zc, zd, *w3_b = _inproj(x2, _row(norm_mix[l]), w_mix, cast=(moe_w3[l // 2],) if moe else ())
        shp = lambda z: z.reshape(bsz, seq, z.shape[-1])

        rw = dict(mu=_row(rwkv_mu[l], RWKV_PAD - RWKV_COLS),
                  dup=_rows_at(rwkv_decay_up[l], 0, LANES), w0=_row(rwkv_w0[l]),
                  aup=_rows_at(rwkv_a_up[l], 64, LANES), a0=_row(rwkv_a0[l]),
                  gup=_rows_at(rwkv_gate_up[l], 0, 256).astype(BF16),
                  kk=_row(rwkv_k_k[l]), ka=_row(rwkv_k_a[l]), rk=_row(rwkv_r_k[l]),
                  lnw=_row(rwkv_ln_w[l]), lnb=_row(rwkv_ln_b[l]))
        if l > 0:
            rw.update(vup=_rows_at(vres_up[l - 1], 160, 256), v0=_row(vres_v0[l - 1]))
        ssm = dict(cw=ssm_conv_w[l], cb=_row(ssm_conv_b[l]), dtb=_row(ssm_dt_bias[l], LANES - SSM_HEADS),
                   alog=_row(ssm_a_log[l], LANES - SSM_HEADS), dexp=_row(jnp.repeat(ssm_d[l], CHUNK)),
                   nw=_row(ssm_norm_w[l]))
        gla = dict(gup=_rows_at(gla_gate_up[l], 0, LANES), gb=_row(gla_gate_bias[l]), nw=_row(gla_norm_w[l]))
        (o_c,), (o_d,), (o_b,), res_a = _run_parts(
            [_attention_part(shp(zc), att_q_gain[l], att_k_gain[l], att_rel_bias, MIXER_BLOCK),
             _gla_part(shp(zd), gla, MIXER_BLOCK), _mamba_part(shp(zb), ssm, MIXER_BLOCK),
             _rwkv_part(shp(za), v_first, rw, MIXER_BLOCK)],
            grid=(bsz, seq // MIXER_BLOCK), name="mixers")
        o_a = res_a[0]
        if l == 0:
            v_first = res_a[1]
        outs = [o.reshape(n_tok, BRANCH_DIM) for o in (o_a, o_b, o_c, o_d)]
        merge_w = (wg.astype(BF16), w_branch[l].astype(BF16), w_out[l].astype(BF16), _row(norm_ffn[l]))
        if not moe:
            x1, hf, *w1_b = _merge(x2, h, outs, *merge_w, cast=(moe_w1[l // 2],) if l + 1 < depth else ())
            x2 = _ffn(x1, hf, ffn_w1[l // 2].astype(BF16), ffn_w3[l // 2].astype(BF16),
                      ffn_w2[l // 2].astype(BF16))
        else:
            x1, *routing, w2_b = _merge(x2, h, outs, *merge_w, wr_t=moe_router[l // 2].T, cast=(moe_w2[l // 2],))
            x2 = _moe(x1, routing, w1_b[0], w3_b[0], w2_b)
    return x2.reshape(bsz, seq, D_MODEL)
```

```python
import functools
from typing import Any, NamedTuple

import jax
import jax.numpy as jnp
from jax import lax
from jax.experimental import pallas as pl
from jax.experimental.pallas import tpu as pltpu
from jax.experimental.pallas import tpu_sc as plsc

F32 = jnp.float32
BF16 = jnp.bfloat16

D_MODEL = 1024
CHUNK = 64
BRANCH_DIM = 512
NORM_EPS = 1e-6
LANES = 128
VMEM_LIMIT = 56 * 1024 * 1024
BIG_VMEM_LIMIT = 61 * 1024 * 1024

LOG2_E = 1.4426950408889634

RWKV_LN_EPS = 64e-5
RWKV_DECAY_SCALE = 0.6065306597126334
RWKV_COLS = 1824
RWKV_PAD = 1920
SSM_COLS = 1544
SSM_PAD = 1664
SSM_HEADS = 8
SSM_STATE = 128
ATT_COLS = 1536
ATT_HEADS = 8
ATT_LEFT = 8 * CHUNK
REL_CLIP = 2 * CHUNK
GLA_COLS = 1552
GLA_PAD = 1664
GLA_GATE_NORM = 16.0
FFN_DIM = 2816
N_EXPERTS = 8
EXPERT_DIM = 3584
MOE_ROWS = 512
ROW_SPLIT = 2
SUBROW = D_MODEL // (2 * ROW_SPLIT)


def _dot(a, b):
    return jnp.dot(a.astype(BF16), b.astype(BF16), preferred_element_type=F32)


def _dot_nt(a, b):
    return lax.dot_general(a.astype(BF16), b.astype(BF16), (((1,), (1,)), ((), ())),
                           preferred_element_type=F32)


def _dot_tn(a, b):
    return lax.dot_general(a.astype(BF16), b.astype(BF16), (((0,), (0,)), ((), ())),
                           preferred_element_type=F32)


def _hi_lo(a):
    hi = a.astype(BF16)
    return hi, (a - hi.astype(F32)).astype(BF16)


def _dot_3x(a, b):
    a_hi, a_lo = _hi_lo(a)
    b_hi, b_lo = _hi_lo(b)
    return (jnp.dot(a_hi, b_hi, preferred_element_type=F32) + jnp.dot(a_lo, b_hi, preferred_element_type=F32)
            + jnp.dot(a_hi, b_lo, preferred_element_type=F32))


def _dot_split(a, m):
    m = m.astype(BF16)
    hi, lo = _hi_lo(a)
    return jnp.dot(hi, m, preferred_element_type=F32) + jnp.dot(lo, m, preferred_element_type=F32)


def _dot_split_l(m, a):
    m = m.astype(BF16)
    hi, lo = _hi_lo(a)
    return jnp.dot(m, hi, preferred_element_type=F32) + jnp.dot(m, lo, preferred_element_type=F32)


def _pack_rows(x):
    bits = pltpu.bitcast(x.astype(BF16).astype(F32), jnp.uint32)
    half = D_MODEL // 2
    return [(bits[:, p * SUBROW:(p + 1) * SUBROW] & jnp.uint32(0xFFFF0000))
            | (bits[:, half + p * SUBROW:half + (p + 1) * SUBROW] >> jnp.uint32(16)) for p in range(ROW_SPLIT)]


def _unpack_rows(planes):
    hi = [pltpu.bitcast(w & jnp.uint32(0xFFFF0000), F32) for w in planes]
    lo = [pltpu.bitcast(w << jnp.uint32(16), F32) for w in planes]
    return jnp.concatenate(hi + lo, axis=-1)


def _softplus(x):
    return jnp.maximum(x, 0.0) + jnp.log(1.0 + jnp.exp(-jnp.abs(x)))


def _sigmoid(x):
    return 1.0 / (1.0 + jnp.exp(-x))


def _silu(x):
    return x * _sigmoid(x)


def _iota2(shape, axis):
    return lax.broadcasted_iota(jnp.int32, shape, axis)


def _group_mean_matrix(n, group):
    r = _iota2((n, n), 0) // group
    c = _iota2((n, n), 1) // group
    return jnp.where(r == c, 1.0 / group, 0.0).astype(F32)


def _chunk_tri(n):
    r = _iota2((n, n), 0)
    c = _iota2((n, n), 1)
    return jnp.where((r // CHUNK == c // CHUNK) & (r >= c), 1.0, 0.0).astype(F32)


def _resident(shape):
    nd = len(shape)
    return pl.BlockSpec(shape, lambda *_: (0,) * nd, pipeline_mode=pl.Buffered(1))


def _params(*sem, vmem_limit=VMEM_LIMIT):
    return pltpu.CompilerParams(dimension_semantics=sem, vmem_limit_bytes=vmem_limit)


_DONE = object()


class _Part(NamedTuple):
    body: Any
    inputs: list
    in_specs: list
    out_shapes: list
    out_specs: list
    scratch_shapes: list


def _run_parts(parts, grid, name):
    def split(refs, counts):
        out, lo = [], 0
        for n in counts:
            out.append(refs[lo:lo + n])
            lo += n
        return out

    n_in = [len(p.inputs) for p in parts]
    n_out = [len(p.out_shapes) for p in parts]
    n_scr = [len(p.scratch_shapes) for p in parts]

    def kernel(*refs):
        ins = split(refs[:sum(n_in)], n_in)
        outs = split(refs[sum(n_in):sum(n_in) + sum(n_out)], n_out)
        scr = split(refs[sum(n_in) + sum(n_out):], n_scr)
        stages = [part.body(*i, *o, *c) for part, i, o, c in zip(parts, ins, outs, scr)]
        while stages:
            for stage in list(stages):
                if next(stage, _DONE) is _DONE:
                    stages.remove(stage)

    flat = lambda field: [x for p in parts for x in getattr(p, field)]
    res = pl.pallas_call(
        kernel,
        grid=grid,
        in_specs=flat("in_specs"),
        out_specs=flat("out_specs"),
        out_shape=flat("out_shapes"),
        scratch_shapes=flat("scratch_shapes"),
        compiler_params=_params("parallel", "arbitrary", vmem_limit=BIG_VMEM_LIMIT),
        name=name,
    )(*flat("inputs"))
    return split(list(res), n_out)


def _cast_rider(w, steps):
    n, rows, cols = w.shape
    sliced = (steps, n * rows // steps, cols)
    assert n * rows % steps == 0 and sliced[1] % 16 == 0
    return w.reshape(sliced), pl.BlockSpec((None,) + sliced[1:], lambda i: (i, 0, 0)), jax.ShapeDtypeStruct(sliced, BF16)


def _cast_slices(src_refs, dst_refs):
    for src, dst in zip(src_refs, dst_refs):
        dst[...] = src[...].astype(dst.dtype)


MIXER_COLS = (RWKV_PAD, SSM_PAD, ATT_COLS, GLA_PAD)
MIXER_BLOCK = 256


def _pack_kernel(w_ref, vd_ref, wmix_ref, wg_ref):
    w = w_ref[...]
    zeros = lambda n: jnp.zeros((w.shape[0], n), F32)
    o_ss, o_at = RWKV_COLS + SSM_COLS, RWKV_COLS + SSM_COLS + ATT_COLS
    o_gl = o_at + GLA_COLS
    pieces = [w[:, :RWKV_COLS], vd_ref[...], zeros(RWKV_PAD - RWKV_COLS - 32),
              w[:, RWKV_COLS:o_ss], zeros(SSM_PAD - SSM_COLS),
              w[:, o_ss:o_at],
              w[:, o_at:o_at + 1024], w[:, o_at + 1040:o_gl], w[:, o_at + 1024:o_at + 1040], zeros(GLA_PAD - GLA_COLS)]
    wmix_ref[...] = jnp.concatenate(pieces, axis=1).astype(BF16)
    wg_ref[...] = w[:, o_gl:].astype(BF16)


def _pack_weights(w, vd, rows=128):
    n_cols = w.shape[1]
    n_gate = n_cols - (RWKV_COLS + SSM_COLS + ATT_COLS + GLA_COLS)
    blk = lambda n: pl.BlockSpec((rows, n), lambda i: (i, 0))
    return pl.pallas_call(
        _pack_kernel,
        grid=(D_MODEL // rows,),
        in_specs=[blk(n_cols), blk(vd.shape[1])],
        out_specs=[blk(sum(MIXER_COLS)), blk(n_gate)],
        out_shape=[jax.ShapeDtypeStruct((D_MODEL, sum(MIXER_COLS)), BF16), jax.ShapeDtypeStruct((D_MODEL, n_gate), BF16)],
        compiler_params=_params("parallel"),
        name="pack_weights",
    )(w, vd)


def _inproj_kernel(n_cast, x_ref, g_ref, w_ref, *refs):
    cast_in, (h_ref, *z_refs), cast_out = refs[:n_cast], refs[n_cast:len(refs) - n_cast], refs[len(refs) - n_cast:]
    _cast_slices(cast_in, cast_out)
    x = x_ref[...]
    ms = jnp.mean(x * x, axis=-1, keepdims=True)
    h = (x * lax.rsqrt(ms + NORM_EPS) * g_ref[...]).astype(BF16)
    h_ref[...] = h
    lo = 0
    for pair in (z_refs[:2], z_refs[2:]):
        widths = [z_ref.shape[-1] for z_ref in pair]
        z = jnp.dot(h, w_ref[:, lo:lo + sum(widths)], preferred_element_type=F32)
        pair[0][...] = z[:, :widths[0]]
        pair[1][...] = z[:, widths[0]:]
        lo += sum(widths)


def _inproj(x2, g, w_mix, cast=(), tm=512):
    n_tok = x2.shape[0]
    row = lambda n: pl.BlockSpec((tm, n), lambda i: (i, 0))
    riders = [_cast_rider(w, n_tok // tm) for w in cast]
    res = pl.pallas_call(
        functools.partial(_inproj_kernel, len(cast)),
        grid=(n_tok // tm,),
        in_specs=[row(D_MODEL), _resident((1, D_MODEL)), _resident(w_mix.shape)] + [r[1] for r in riders],
        out_specs=[row(D_MODEL)] + [row(n) for n in MIXER_COLS] + [r[1] for r in riders],
        out_shape=[jax.ShapeDtypeStruct((n_tok, D_MODEL), BF16)]
        + [jax.ShapeDtypeStruct((n_tok, n), F32) for n in MIXER_COLS] + [r[2] for r in riders],
        compiler_params=_params("parallel", vmem_limit=BIG_VMEM_LIMIT if cast else VMEM_LIMIT),
        name="inproj",
    )(x2, g, w_mix, *[r[0] for r in riders])
    n_plain = len(res) - len(cast)
    return list(res[:n_plain]) + [wb16.reshape(w.shape) for wb16, w in zip(res[n_plain:], cast)]


def _rwkv_block(r, k, v, kk, ka, ld, tb, state_ref, y_ref):
    n_chunks = tb // CHUNK
    cum = _dot_split_l(_chunk_tri(tb), ld)
    p_inv = jnp.exp(-cum)
    a_t = -kk * jnp.exp(cum - ld)
    b_t = kk * ka * p_inv
    k_t = k * p_inv
    r_t = r * jnp.exp(cum)
    bf = lambda z: z.astype(BF16)
    a_b, b_b, k_b, r_b, v_b = bf(a_t), bf(b_t), bf(k_t), bf(r_t), bf(v)
    yield

    lane_head = _iota2((CHUNK, LANES), 1) // CHUNK
    row = _iota2((CHUNK, LANES), 0)
    col = _iota2((CHUNK, LANES), 1) % CHUNK
    strict = row > col
    incl = row >= col
    bd_mask = (_iota2((LANES, LANES), 0) // CHUNK) == (_iota2((LANES, LANES), 1) // CHUNK)
    eye = jnp.where(_iota2((LANES, LANES), 0) == _iota2((LANES, LANES), 1), 1.0, 0.0).astype(F32)

    def stack(z):
        return jnp.concatenate([jnp.where(lane_head == 0, z, 0.0), jnp.where(lane_head == 1, z, 0.0)], axis=0)

    def sub(z, u):
        c, p = u
        return z[c * CHUNK:(c + 1) * CHUNK, p * LANES:(p + 1) * LANES]

    def prepare(units):
        gram = {u: _dot_nt(jnp.concatenate([sub(a_b, u), sub(r_b, u)], axis=0),
                           jnp.concatenate([stack(sub(b_b, u)), stack(sub(k_b, u))], axis=0))
                for u in units}
        yield
        a_rb = {u: bf(jnp.where(incl, gram[u][CHUNK:, :LANES], 0.0)) for u in units}
        a_akrk = {u: bf(jnp.concatenate([jnp.where(strict, gram[u][:CHUNK, LANES:], 0.0),
                                         jnp.where(incl, gram[u][CHUNK:, LANES:], 0.0)], axis=0)) for u in units}
        pw = {u: stack(jnp.where(strict, gram[u][:CHUNK, :LANES], 0.0)) for u in units}
        t_bd = {u: eye + pw[u] for u in units}
        pw = {u: _dot(pw[u], pw[u]) for u in units}
        yield
        for _ in range(CHUNK.bit_length() - 3):
            sp = {u: _dot(jnp.concatenate([t_bd[u], pw[u]], axis=0), pw[u]) for u in units}
            t_bd = {u: t_bd[u] + sp[u][:LANES] for u in units}
            pw = {u: sp[u][LANES:] for u in units}
            yield
        t_bd = {u: bf(t_bd[u] + _dot(t_bd[u], pw[u])) for u in units}
        v_bd = {u: stack(sub(v_b, u)) for u in units}
        avyv = {u: _dot(a_akrk[u], v_bd[u]) for u in units}
        yield
        wuv = {u: _dot(t_bd[u], jnp.concatenate([stack(sub(a_b, u)), stack(bf(avyv[u][:CHUNK]))], axis=1))
               for u in units}
        wr = {u: jnp.concatenate([bf(wuv[u][:, :LANES]), sub(r_b, u)], axis=0) for u in units}
        yield
        return a_rb, avyv, wuv, wr

    a_rb, avyv, wuv, wr = yield from prepare([(c, p) for c in range(n_chunks) for p in range(4)])

    h = [state_ref[p] for p in range(4)]
    for c in range(n_chunks):
        last = c * CHUNK + CHUNK - 1
        p_last = jnp.exp(cum[last:last + 1, :])
        hs = [_dot_nt(wr[(c, p)], h[p]) for p in range(4)]
        u2 = [hs[p][:LANES] + wuv[(c, p)][:, LANES:] for p in range(4)]
        u_p = [u2[p][:CHUNK] + u2[p][CHUNK:] for p in range(4)]
        upd = [_dot_tn(jnp.concatenate([u_p[p], sub(v, (c, p))], axis=0),
                       jnp.concatenate([sub(b_t, (c, p)), sub(k_t, (c, p))], axis=0)
                       * p_last[:, p * LANES:(p + 1) * LANES]) for p in range(4)]
        for p in range(4):
            y_ref[c * CHUNK:(c + 1) * CHUNK, p * LANES:(p + 1) * LANES] = (
                hs[p][LANES:] + _dot(a_rb[(c, p)], stack(u_p[p])) + avyv[(c, p)][CHUNK:])
        h = [h[p] * p_last[:, p * LANES:(p + 1) * LANES] + jnp.where(bd_mask, upd[p], 0.0) for p in range(4)]
        yield
    for p in range(4):
        state_ref[p] = h[p]


def _rwkv_kernel(has_vres, tb, *refs):
    if has_vres:
        (za_ref, vfirst_ref, mu_ref, dup_ref, w0_ref, aup_ref, a0_ref, gup_ref, vup_ref, v0_ref,
         kk_ref, ka_ref, rk_ref, lnw_ref, lnb_ref, o_ref, prev_ref, state_ref, y_ref) = refs
    else:
        (za_ref, mu_ref, dup_ref, w0_ref, aup_ref, a0_ref, gup_ref,
         kk_ref, ka_ref, rk_ref, lnw_ref, lnb_ref, o_ref, vraw_ref, prev_ref, state_ref, y_ref) = refs

    @pl.when(pl.program_id(1) == 0)
    def _():
        prev_ref[...] = jnp.zeros_like(prev_ref)
        state_ref[...] = jnp.zeros_like(state_ref)

    yield
    za = za_ref[...]
    shifted = pltpu.roll(za, 1, axis=0)
    shifted = jnp.where(_iota2(za.shape, 0) == 0, prev_ref[7:8, :], shifted)
    prev_ref[...] = za[tb - 8:, :]
    za = za + (shifted - za) * mu_ref[...]

    r = za[:, 0:512]
    k = za[:, 512:1024]
    v = za[:, 1024:1536]
    xwa = za[:, 1536:1664]
    xg = za[:, 1664:1920]
    ld = -RWKV_DECAY_SCALE * _sigmoid(w0_ref[...] + _dot_3x(jnp.tanh(xwa), dup_ref[...]))
    a = _sigmoid(a0_ref[...] + _dot(xwa, aup_ref[...]))
    g = _dot(_sigmoid(xg), gup_ref[...])
    if has_vres:
        v_mix = _sigmoid(v0_ref[...] + _dot(xg, vup_ref[...]))
        v = v + (vfirst_ref[...] - v) * v_mix
    else:
        vraw_ref[...] = v
    kk = k * kk_ref[...]
    k = k * (1.0 + (a - 1.0) * ka_ref[...])
    head_sum = _group_mean_matrix(512, CHUNK) * float(CHUNK)
    kk = kk / jnp.maximum(jnp.sqrt(_dot(kk * kk, head_sum)), 1e-12)
    yield

    yield from _rwkv_block(r, k, v, kk, a, ld, tb, state_ref, y_ref)
    y = y_ref[...]
    head_mean = _group_mean_matrix(512, CHUNK)
    mean = _dot_split(y, head_mean)
    yc = y - mean
    var = _dot(yc * yc, head_mean)
    y = yc * lax.rsqrt(var + RWKV_LN_EPS) * lnw_ref[...] + lnb_ref[...]
    y = y + _dot(r * k * rk_ref[...], head_sum) * v
    o_ref[...] = (y * g).astype(o_ref.dtype)


def _rwkv_part(za, v_first, prm, tb):
    bsz, seq, _ = za.shape
    has_vres = v_first is not None
    blk = lambda n: pl.BlockSpec((None, tb, n), lambda b, i: (b, i, 0))
    names = (["mu", "dup", "w0", "aup", "a0", "gup"] + (["vup", "v0"] if has_vres else [])
             + ["kk", "ka", "rk", "lnw", "lnb"])
    weights = [prm[n] for n in names]
    ins = [za] + ([v_first] if has_vres else []) + weights
    in_specs = [blk(RWKV_PAD)] + ([blk(512)] if has_vres else []) + [_resident(w.shape) for w in weights]
    out_shape = [jax.ShapeDtypeStruct((bsz, seq, 512), BF16)]
    out_specs = [blk(512)]
    if not has_vres:
        out_shape.append(jax.ShapeDtypeStruct((bsz, seq, 512), F32))
        out_specs.append(blk(512))
    return _Part(functools.partial(_rwkv_kernel, has_vres, tb), ins, in_specs, out_shape, out_specs,
                 [pltpu.VMEM((8, RWKV_PAD), F32), pltpu.VMEM((4, LANES, LANES), F32), pltpu.VMEM((tb, 512), F32)])


def _mamba_kernel(tb, zb_ref, cw_ref, cb_ref, dtb_ref, alog_ref, dexp_ref, nw_ref,
                  o_ref, xbuf_ref, state_ref, y_ref):
    @pl.when(pl.program_id(1) == 0)
    def _():
        xbuf_ref[0:8, :] = jnp.zeros((8, 1024), F32)
        state_ref[...] = jnp.zeros_like(state_ref)

    yield
    xbuf_ref[8:8 + tb, :] = zb_ref[:, 512:1536]
    conv = cb_ref[...]
    for i in range(4):
        conv = conv + cw_ref[i:i + 1, :] * xbuf_ref[5 + i:5 + i + tb, :]
    xbuf_ref[0:8, :] = xbuf_ref[tb:tb + 8, :]
    xbc = _silu(conv)
    xs = xbc[:, 0:512]
    bm = xbc[:, 512:768]
    cm = xbc[:, 768:1024]
    yield

    lane = _iota2((1, LANES), 1)
    dt = _softplus(zb_ref[:, 1536:1664] + dtb_ref[...])
    a_neg = jnp.where(lane < SSM_HEADS, -jnp.exp(alog_ref[...]), 0.0)
    acs = _dot_split_l(_chunk_tri(tb), dt * a_neg)
    expand = jnp.where(_iota2((LANES, 512), 0) == _iota2((LANES, 512), 1) // CHUNK, 1.0, 0.0)
    dt_e = _dot(dt, expand)
    acs_e = _dot_split(acs, expand)
    xdt = xs * dt_e
    yield

    n_chunks = tb // CHUNK
    causal = _iota2((CHUNK, 512), 0) >= _iota2((CHUNK, 512), 1) % CHUNK
    pair_head = _iota2((CHUNK, LANES), 1) // CHUNK
    spread = jnp.where(_iota2((CHUNK, 512), 0) == _iota2((CHUNK, 512), 1) % CHUNK, 1.0, 0.0)
    own_head = _iota2((SSM_HEADS, 512), 0) == _iota2((SSM_HEADS, 512), 1) // CHUNK
    rows_of = lambda c: slice(c * CHUNK, (c + 1) * CHUNK)
    grp = lambda g: slice(g * SSM_STATE, (g + 1) * SSM_STATE)

    def stack(z):
        return jnp.concatenate([jnp.where(pair_head == 0, z, 0.0), jnp.where(pair_head == 1, z, 0.0)], axis=0)

    y_intra, upd, e_in, e_last = [], [], [], []
    for c in range(n_chunks):
        rows = rows_of(c)
        acs_ec = acs_e[rows]
        acs_t = acs[rows].T[0:SSM_HEADS]
        acs_row = jnp.sum(jnp.where(own_head, _dot_split(acs_t, spread), 0.0), axis=0, keepdims=True)
        decay = jnp.exp(jnp.where(causal, acs_ec - acs_row, -jnp.inf))
        cb = jnp.concatenate(
            [_dot_nt(cm[rows, grp(g)], jnp.concatenate([bm[rows, grp(g)]] * 4, axis=0)) for g in range(2)], axis=1)
        m = cb * decay
        yield
        xdt_c = xdt[rows]
        y_intra.append([_dot(m[:, p * LANES:(p + 1) * LANES], stack(xdt_c[:, p * LANES:(p + 1) * LANES]))
                        for p in range(4)])
        last_e = acs_ec[CHUNK - 1:CHUNK, :]
        x_out = xdt_c * jnp.exp(last_e - acs_ec)
        upd.append([_dot_tn(bm[rows, grp(g)], x_out[:, g * 256:(g + 1) * 256]) for g in range(2)])
        e_in.append(jnp.exp(acs_ec))
        e_last.append(jnp.exp(last_e))
        yield

    state = [state_ref[g] for g in range(2)]
    for c in range(n_chunks):
        rows = rows_of(c)
        for g in range(2):
            ls = slice(g * 256, (g + 1) * 256)
            y_in = jnp.concatenate(y_intra[c][2 * g:2 * g + 2], axis=1)
            y_ref[rows, ls] = y_in + _dot(cm[rows, grp(g)], state[g]) * e_in[c][:, ls]
            state[g] = state[g] * e_last[c][:, ls] + upd[c][g]
        yield
    for g in range(2):
        state_ref[g] = state[g]

    y = (y_ref[...] + xs * dexp_ref[...]) * _silu(zb_ref[:, 0:512])
    for g in range(2):
        ls = slice(g * 256, (g + 1) * 256)
        yg = y[:, ls]
        ms = jnp.mean(yg * yg, axis=-1, keepdims=True)
        o_ref[:, ls] = (yg * lax.rsqrt(ms + NORM_EPS) * nw_ref[:, ls]).astype(o_ref.dtype)


def _mamba_part(zb, prm, tb):
    bsz, seq, _ = zb.shape
    blk = lambda n: pl.BlockSpec((None, tb, n), lambda b, i: (b, i, 0))
    weights = [prm[n] for n in ("cw", "cb", "dtb", "alog", "dexp", "nw")]
    return _Part(functools.partial(_mamba_kernel, tb), [zb] + weights,
                 [blk(SSM_PAD)] + [_resident(w.shape) for w in weights],
                 [jax.ShapeDtypeStruct((bsz, seq, 512), BF16)], [blk(512)],
                 [pltpu.VMEM((tb + 8, 1024), F32), pltpu.VMEM((2, SSM_STATE, 256), F32), pltpu.VMEM((tb, 512), F32)])


def _attn_kernel(tq, qkv_ref, qg_ref, kg_ref, bias_ref, o_ref, kn_ref, vb_ref):
    i = pl.program_id(1)
    head_mean = _group_mean_matrix(512, CHUNK)
    win = tq + ATT_LEFT

    @pl.when(i == 0)
    def _():
        kn_ref[0:ATT_LEFT, :] = jnp.zeros((ATT_LEFT, 512), BF16)
        vb_ref[0:ATT_LEFT, :] = jnp.zeros((ATT_LEFT, 512), BF16)

    yield
    start = pl.multiple_of(i * tq, tq)
    k = qkv_ref[:, 512:1024]
    ms = _dot(k * k, head_mean)
    kn_ref[pl.ds(ATT_LEFT + start, tq), :] = (k * lax.rsqrt(ms + NORM_EPS) * kg_ref[...]).astype(BF16)
    vb_ref[pl.ds(ATT_LEFT + start, tq), :] = qkv_ref[:, 1024:1536].astype(BF16)
    q = qkv_ref[:, 0:512]
    ms = _dot(q * q, head_mean)
    qn = q * lax.rsqrt(ms + NORM_EPS) * qg_ref[...] * (CHUNK ** -0.5 * LOG2_E)
    kwin = kn_ref[pl.ds(start, win), :]
    vwin = vb_ref[pl.ds(start, win), :]
    lane_head = _iota2((tq, LANES), 1) // CHUNK
    yield
    for p in range(4):
        ls = slice(p * LANES, (p + 1) * LANES)
        q2 = jnp.concatenate([jnp.where(lane_head == s, qn[:, ls], 0.0) for s in range(2)], axis=0)
        sc = _dot_nt(q2, kwin[:, ls]) + bias_ref[2 * p:2 * p + 2].reshape(2 * tq, win).astype(F32)
        yield
        e = jnp.exp2(sc - jnp.max(sc, axis=-1, keepdims=True))
        yield
        o2 = _dot(e, vwin[:, ls]) / jnp.sum(e, axis=-1, keepdims=True)
        o_ref[:, ls] = jnp.where(lane_head == 0, o2[:tq], o2[tq:]).astype(o_ref.dtype)
        yield


def _band_bias(rel_bias, tq, win):
    period = tq + win
    m = jnp.arange(period)
    d = jnp.where(m < win, m, m - period) - ATT_LEFT
    f = (rel_bias.astype(F32)[:, jnp.clip(d, -REL_CLIP, REL_CLIP) + REL_CLIP] * LOG2_E).astype(BF16)
    g = jnp.tile(f, (1, tq))[:, :tq * (period - 1)].reshape(-1, tq, period - 1)
    bias = g[:, :, :win]
    left = ATT_LEFT // CHUNK
    qc = left + jnp.arange(tq)[:, None] // CHUNK
    kc = jnp.arange(win)[None, :] // CHUNK
    tiles = []
    for blk in range(ATT_LEFT // tq + 1):
        first_kc = jnp.maximum(qc - left, left - blk * (tq // CHUNK))
        tiles.append(jnp.where((kc <= qc) & (kc >= first_kc), bias, -jnp.inf))
    return jnp.stack(tiles)


def _attention_part(zc, q_gain, k_gain, rel_bias, tq):
    bsz, seq, _ = zc.shape
    win = tq + ATT_LEFT
    bias = _band_bias(rel_bias, tq, win)
    qg = jnp.tile(q_gain, ATT_HEADS)[None, :]
    kg = jnp.tile(k_gain, ATT_HEADS)[None, :]
    last_tile = bias.shape[0] - 1
    bias_spec = pl.BlockSpec((None,) + bias.shape[1:], lambda b, i: (jnp.minimum(i, last_tile), 0, 0, 0))
    blk = lambda n: pl.BlockSpec((None, tq, n), lambda b, i: (b, i, 0))
    return _Part(functools.partial(_attn_kernel, tq), [zc, qg, kg, bias],
                 [blk(ATT_COLS), _resident(qg.shape), _resident(kg.shape), bias_spec],
                 [jax.ShapeDtypeStruct((bsz, seq, 512), BF16)], [blk(512)],
                 [pltpu.VMEM((ATT_LEFT + seq, 512), BF16), pltpu.VMEM((ATT_LEFT + seq, 512), BF16)])


def _gla_kernel(tb, zd_ref, gup_ref, gb_ref, nw_ref, o_ref, state_ref, y_ref):
    @pl.when(pl.program_id(1) == 0)
    def _():
        state_ref[...] = jnp.zeros_like(state_ref)

    yield
    q = zd_ref[:, 0:256] * (CHUNK ** -0.5)
    k = zd_ref[:, 256:512]
    v = zd_ref[:, 512:1024]
    log_a = -_softplus(-(_dot_3x(zd_ref[:, 1536:1664], gup_ref[...]) + gb_ref[...])) / GLA_GATE_NORM
    bcum = _dot_split_l(_chunk_tri(tb), log_a)

    k_head = _iota2((CHUNK, 256), 1) // CHUNK
    v_head = _iota2((CHUNK, 512), 1) // LANES
    causal = _iota2((CHUNK, 256), 0) >= _iota2((CHUNK, 256), 1) % CHUNK
    bd = (_iota2((512, 256), 0) // LANES) == (_iota2((512, 256), 1) // CHUNK)
    n_chunks = tb // CHUNK
    qg_all = (q * jnp.exp(bcum)).astype(BF16)
    kg_all = k * jnp.exp(-bcum)
    yield
    qg, att, o_intra, upd, e_last = [], [], [], [], []
    for c in range(n_chunks):
        rows = slice(c * CHUNK, (c + 1) * CHUNK)
        kg_bd = jnp.concatenate([jnp.where(k_head == h, kg_all[rows], 0.0) for h in range(4)], axis=0)
        qg.append(qg_all[rows])
        att.append(jnp.where(causal, _dot_nt(qg[c], kg_bd), 0.0))
        yield
    for c in range(n_chunks):
        rows = slice(c * CHUNK, (c + 1) * CHUNK)
        bc = bcum[rows]
        blast = bc[CHUNK - 1:CHUNK, :]
        v_c = v[rows]
        v_bd = jnp.concatenate([jnp.where(v_head == h, v_c, 0.0) for h in range(4)], axis=0)
        o_intra.append(_dot(att[c], v_bd))
        upd.append(jnp.where(bd, _dot_tn(v_c, k[rows] * jnp.exp(blast - bc)), 0.0))
        e_last.append(jnp.exp(blast))
        yield
    st = state_ref[...]
    for c in range(n_chunks):
        y_ref[c * CHUNK:(c + 1) * CHUNK, :] = o_intra[c] + _dot_nt(qg[c], st)
        st = st * e_last[c] + upd[c]
        yield
    state_ref[...] = st

    o = y_ref[...]
    for h in range(4):
        ls = slice(h * LANES, (h + 1) * LANES)
        oh = o[:, ls]
        ms = jnp.mean(oh * oh, axis=-1, keepdims=True)
        o_ref[:, ls] = (oh * lax.rsqrt(ms + NORM_EPS) * nw_ref[...]
                        * _silu(zd_ref[:, 1024 + h * LANES:1024 + (h + 1) * LANES])).astype(o_ref.dtype)


def _gla_part(zd, prm, tb):
    bsz, seq, _ = zd.shape
    blk = lambda n: pl.BlockSpec((None, tb, n), lambda b, i: (b, i, 0))
    weights = [prm[n] for n in ("gup", "gb", "nw")]
    return _Part(functools.partial(_gla_kernel, tb), [zd] + weights,
                 [blk(GLA_PAD)] + [_resident(w.shape) for w in weights],
                 [jax.ShapeDtypeStruct((bsz, seq, 512), BF16)], [blk(512)],
                 [pltpu.VMEM((512, 256), F32), pltpu.VMEM((tb, 512), F32)])


def _route(tm, hf, wr_ref, idx_ref, gate_ref, rank_ref, cnt_ref, hf_ref, carry_ref):
    @pl.when(pl.program_id(0) == 0)
    def _():
        carry_ref[...] = jnp.zeros_like(carry_ref)

    for j, words in enumerate(_pack_rows(hf)):
        hf_ref[j] = words
    nt = lambda a, b: lax.dot_general(a, b, (((1,), (1,)), ((), ())), preferred_element_type=F32)
    w_hi, w_lo = _hi_lo(wr_ref[...])
    h_hi, h_lo = _hi_lo(hf)
    logits = nt(w_hi, h_hi) + nt(w_lo, h_hi) + nt(w_hi, h_lo)
    e_iota = _iota2((N_EXPERTS, tm), 0)
    m1 = jnp.max(logits, axis=0, keepdims=True)
    i1 = jnp.min(jnp.where(logits == m1, e_iota, N_EXPERTS), axis=0, keepdims=True)
    rest = jnp.where(e_iota == i1, -jnp.inf, logits)
    m2 = jnp.max(rest, axis=0, keepdims=True)
    i2 = jnp.min(jnp.where(rest == m2, e_iota, N_EXPERTS), axis=0, keepdims=True)
    e2 = jnp.exp(m2 - m1)
    gate_ref[0:1, :] = 1.0 / (1.0 + e2)
    gate_ref[1:2, :] = e2 / (1.0 + e2)
    idx_ref[0:1, :] = i1
    idx_ref[1:2, :] = i2
    hit1 = jnp.where(e_iota == i1, 1.0, 0.0)
    hit2 = jnp.where(e_iota == i2, 1.0, 0.0)
    before = jnp.where(_iota2((tm, tm), 0) < _iota2((tm, tm), 1), 1.0, 0.0)
    prior = _dot(hit1 + hit2, before) + carry_ref[:, 0:1]
    rank_ref[0:1, :] = jnp.sum(hit1 * prior, axis=0, keepdims=True).astype(jnp.int32)
    rank_ref[1:2, :] = jnp.sum(hit2 * prior, axis=0, keepdims=True).astype(jnp.int32)
    carry_ref[...] = carry_ref[...] + jnp.sum(hit1 + hit2, axis=1, keepdims=True)
    cnt_ref[...] = carry_ref[...]


def _merge_kernel(tm, route, n_cast, *refs):
    n_in, n_out = (11, 6) if route else (10, 2)
    ins, cast_in = refs[:n_in], refs[n_in:n_in + n_cast]
    outs = refs[n_in + n_cast:n_in + n_cast + n_out]
    cast_out = refs[n_in + n_cast + n_out:n_in + 2 * n_cast + n_out]
    scratch = refs[n_in + 2 * n_cast + n_out:]
    x_ref, h_ref, oa_ref, ob_ref, oc_ref, od_ref, wg_ref, wb_ref, wo_ref, nf_ref = ins[:10]
    _cast_slices(cast_in, cast_out)
    h = h_ref[...]
    acc = jnp.zeros(x_ref.shape, F32)
    for i, o_ref in enumerate((oa_ref, ob_ref, oc_ref, od_ref)):
        gate = _sigmoid(jnp.dot(h, wg_ref[:, i * D_MODEL:(i + 1) * D_MODEL], preferred_element_type=F32))
        acc = acc + gate * jnp.dot(o_ref[...], wb_ref[i], preferred_element_type=F32)
    x1 = x_ref[...] + jnp.dot(acc.astype(BF16), wo_ref[...], preferred_element_type=F32)
    ms = jnp.mean(x1 * x1, axis=-1, keepdims=True)
    hf = x1 * lax.rsqrt(ms + NORM_EPS) * nf_ref[...]
    if route:
        _route(tm, hf, ins[10], *outs[1:], *scratch)
    else:
        outs[1][...] = hf.astype(BF16)
    outs[0][...] = x1


def _merge(x2, h, outs, wg, wb, wo, nf, wr_t=None, cast=(), tm=512):
    n_tok = x2.shape[0]
    route = wr_t is not None
    steps = n_tok // tm
    row = lambda n: pl.BlockSpec((tm, n), lambda i: (i, 0))
    ins = [x2, h, *outs, wg, wb, wo, nf]
    in_specs = ([row(D_MODEL), row(D_MODEL)] + [row(BRANCH_DIM)] * 4
                + [_resident(wg.shape), _resident(wb.shape), _resident(wo.shape), _resident(nf.shape)])
    out_specs = [row(D_MODEL)]
    out_shape = [jax.ShapeDtypeStruct((n_tok, D_MODEL), F32)]
    scratch = []
    if route:
        ins.append(wr_t)
        in_specs.append(_resident(wr_t.shape))
        col = pl.BlockSpec((2, tm), lambda i: (0, i))
        out_specs += [col, col, col, pl.BlockSpec((N_EXPERTS, LANES), lambda i: (0, 0)),
                      pl.BlockSpec((ROW_SPLIT, tm, SUBROW), lambda i: (0, i, 0))]
        out_shape += [jax.ShapeDtypeStruct((2, n_tok), jnp.int32), jax.ShapeDtypeStruct((2, n_tok), F32),
                      jax.ShapeDtypeStruct((2, n_tok), jnp.int32), jax.ShapeDtypeStruct((N_EXPERTS, LANES), F32),
                      jax.ShapeDtypeStruct((ROW_SPLIT, n_tok, SUBROW), jnp.uint32)]
        scratch = [pltpu.VMEM((N_EXPERTS, LANES), F32)]
    else:
        out_specs.append(row(D_MODEL))
        out_shape.append(jax.ShapeDtypeStruct((n_tok, D_MODEL), BF16))
    riders = [_cast_rider(w, steps) for w in cast]
    res = pl.pallas_call(
        functools.partial(_merge_kernel, tm, route, len(cast)),
        grid=(steps,),
        in_specs=in_specs + [r[1] for r in riders],
        out_specs=out_specs + [r[1] for r in riders],
        out_shape=out_shape + [r[2] for r in riders],
        scratch_shapes=scratch,
        compiler_params=_params("arbitrary" if route else "parallel"),
        name="merge",
    )(*ins, *[r[0] for r in riders])
    n_plain = len(res) - len(cast)
    return list(res[:n_plain]) + [wb16.reshape(w.shape) for wb16, w in zip(res[n_plain:], cast)]


def _ffn_kernel(tf, x1_ref, hf_ref, w1_ref, w3_ref, w2_ref, o_ref):
    hf = hf_ref[...]
    acc = x1_ref[...]
    for lo in range(0, FFN_DIM, tf):
        cols = slice(lo, min(lo + tf, FFN_DIM))
        a = jnp.dot(hf, w1_ref[:, cols], preferred_element_type=F32)
        b = jnp.dot(hf, w3_ref[:, cols], preferred_element_type=F32)
        acc = acc + jnp.dot((_silu(a) * b).astype(BF16), w2_ref[cols, :], preferred_element_type=F32)
    o_ref[...] = acc


def _ffn(x1, hf, w1, w3, w2, tm=512, tf=512):
    n_tok = x1.shape[0]
    row = pl.BlockSpec((tm, D_MODEL), lambda i: (i, 0))
    return pl.pallas_call(
        functools.partial(_ffn_kernel, tf),
        grid=(n_tok // tm,),
        in_specs=[row, row, _resident(w1.shape), _resident(w3.shape), _resident(w2.shape)],
        out_specs=row,
        out_shape=jax.ShapeDtypeStruct((n_tok, D_MODEL), F32),
        compiler_params=_params("parallel"),
        name="ffn",
    )(x1, hf, w1, w3, w2)


def _gather_rows(table, idx, window=128):
    split, n_table, width = table.shape
    n = idx.shape[0]
    flat_idx = (idx[None, :] + n_table * jnp.arange(split, dtype=jnp.int32)[:, None]).reshape(-1)
    return _gather_subrows(table.reshape(split * n_table, width), flat_idx, window).reshape(split, n, width)


def _gather_subrows(table, idx, window):
    n = idx.shape[0]
    d = table.shape[1]
    mesh = plsc.VectorSubcoreMesh(core_axis_name="core", subcore_axis_name="subcore")

    @functools.partial(pl.kernel, out_type=jax.ShapeDtypeStruct((n, d), table.dtype), mesh=mesh,
                       name="gather_rows")
    def gather(table_hbm, idx_hbm, out_hbm):
        def body(idx_vmem, out_vmem):
            pltpu.sync_copy(table_hbm.at[idx_vmem.at[0]], out_vmem)

        pltpu.emit_pipeline(
            body,
            grid=(n // window,),
            in_specs=[pl.BlockSpec((1, window), index_map=lambda i: (0, i))],
            out_specs=[pl.BlockSpec((window, d), index_map=lambda i: (i, 0))],
            core_axis_name=("core", "subcore"),
            dimension_semantics=(pltpu.PARALLEL,),
        )(idx_hbm, out_hbm)

    return gather(table, idx.reshape(1, n))


def _scatter_rows(table, dest, n_out, window=128):
    split, n_table, width = table.shape
    n = dest.shape[0]
    flat_dest = (dest[None, :] + n_out * jnp.arange(split, dtype=jnp.int32)[:, None]).reshape(1, split * n)
    src_blocks, blocks = n_table // window, n // window
    mesh = plsc.VectorSubcoreMesh(core_axis_name="core", subcore_axis_name="subcore")

    @functools.partial(pl.kernel, out_type=jax.ShapeDtypeStruct((split * n_out, width), table.dtype), mesh=mesh,
                       name="scatter_rows")
    def scatter(table_hbm, dest_hbm, out_hbm):
        def body(rows_vmem, dest_vmem):
            pltpu.sync_copy(rows_vmem, out_hbm.at[dest_vmem.at[0]])

        pltpu.emit_pipeline(
            body,
            grid=(split * blocks,),
            in_specs=[pl.BlockSpec((window, width),
                                   index_map=lambda i: ((i // blocks) * src_blocks + (i % blocks) % src_blocks, 0)),
                      pl.BlockSpec((1, window), index_map=lambda i: (0, i))],
            out_specs=[],
            core_axis_name=("core", "subcore"),
            dimension_semantics=(pltpu.PARALLEL,),
        )(table_hbm, dest_hbm)

    return scatter(table.reshape(split * n_table, width), flat_dest).reshape(split, n_out, width)


def _expert_kernel(tf, ge_ref, nv_ref, x_ref, w1_ref, w3_ref, w2_ref, y_ref):
    g = pl.program_id(0)

    @pl.when(g < nv_ref[0])
    def _():
        x = _unpack_rows([x_ref[j] for j in range(ROW_SPLIT)]).astype(BF16)
        acc = jnp.zeros((MOE_ROWS, D_MODEL), F32)
        for lo in range(0, EXPERT_DIM, tf):
            a = jnp.dot(x, w1_ref[:, lo:lo + tf], preferred_element_type=F32)
            b = jnp.dot(x, w3_ref[:, lo:lo + tf], preferred_element_type=F32)
            acc = acc + jnp.dot((_silu(a) * b).astype(BF16), w2_ref[lo:lo + tf, :], preferred_element_type=F32)
        for j, words in enumerate(_pack_rows(acc)):
            y_ref[j] = words

    @pl.when(g >= nv_ref[0])
    def _():
        y_ref[...] = jnp.zeros_like(y_ref)


def _experts(xg, group_expert, n_valid, w1, w3, w2, tf=512):
    n_rows = xg.shape[1]
    rows = pl.BlockSpec((ROW_SPLIT, MOE_ROWS, SUBROW), lambda g, ge, nv: (0, g, 0))
    return pl.pallas_call(
        functools.partial(_expert_kernel, tf),
        grid_spec=pltpu.PrefetchScalarGridSpec(
            num_scalar_prefetch=2,
            grid=(n_rows // MOE_ROWS,),
            in_specs=[rows,
                      pl.BlockSpec((None, D_MODEL, EXPERT_DIM), lambda g, ge, nv: (ge[g], 0, 0)),
                      pl.BlockSpec((None, D_MODEL, EXPERT_DIM), lambda g, ge, nv: (ge[g], 0, 0)),
                      pl.BlockSpec((None, EXPERT_DIM, D_MODEL), lambda g, ge, nv: (ge[g], 0, 0))],
            out_specs=rows,
        ),
        out_shape=jax.ShapeDtypeStruct((ROW_SPLIT, n_rows, SUBROW), jnp.uint32),
        compiler_params=_params("arbitrary", vmem_limit=BIG_VMEM_LIMIT),
        name="experts",
    )(group_expert, n_valid, xg, w1, w3, w2)


def _combine_kernel(x1_ref, y0_ref, y1_ref, gate_ref, o_ref):
    gate = gate_ref[...]
    y0 = _unpack_rows([y0_ref[j] for j in range(ROW_SPLIT)])
    y1 = _unpack_rows([y1_ref[j] for j in range(ROW_SPLIT)])
    o_ref[...] = x1_ref[...] + gate[:, 0:1] * y0 + gate[:, 1:2] * y1


def _combine(x1, yg, gate_t, tm=512):
    n_tok = x1.shape[0]
    nb = n_tok // tm
    return pl.pallas_call(
        _combine_kernel,
        grid=(nb,),
        in_specs=[pl.BlockSpec((tm, D_MODEL), lambda i: (i, 0)),
                  pl.BlockSpec((ROW_SPLIT, tm, SUBROW), lambda i: (0, i, 0)),
                  pl.BlockSpec((ROW_SPLIT, tm, SUBROW), lambda i: (0, i + nb, 0)),
                  pl.BlockSpec((tm, 2), lambda i: (i, 0))],
        out_specs=pl.BlockSpec((tm, D_MODEL), lambda i: (i, 0)),
        out_shape=jax.ShapeDtypeStruct((n_tok, D_MODEL), F32),
        compiler_params=_params("parallel"),
        name="moe_combine",
    )(x1, yg, yg, gate_t)


def _moe(x1, routing, w1, w3, w2):
    n_tok = x1.shape[0]
    idx, gate, rank, cnt, hf = routing
    counts = cnt[:, 0].astype(jnp.int32)
    padded = (counts + MOE_ROWS - 1) // MOE_ROWS * MOE_ROWS
    end_padded = jnp.cumsum(padded)
    start_padded = end_padded - padded
    start = sum(jnp.where(idx == e, start_padded[e], 0) for e in range(N_EXPERTS))
    dest = (start + rank).reshape(-1)
    n_groups = (n_tok * 2 + MOE_ROWS - 1) // MOE_ROWS + N_EXPERTS
    n_rows = n_groups * MOE_ROWS
    group_row = jnp.arange(n_groups, dtype=jnp.int32)[:, None] * MOE_ROWS
    group_expert = jnp.minimum(jnp.sum(group_row >= end_padded[None, :], axis=1), N_EXPERTS - 1).astype(jnp.int32)
    n_valid = (end_padded[-1:] // MOE_ROWS).astype(jnp.int32)
    fill = jnp.arange(MOE_ROWS, dtype=jnp.int32)[None, :]
    pad_dest = jnp.where(fill < (padded - counts)[:, None], (start_padded + counts)[:, None] + fill,
                         n_rows - MOE_ROWS + fill).reshape(-1)
    xg = _scatter_rows(hf, jnp.concatenate([dest, pad_dest]), n_rows)
    y = _experts(xg, group_expert, n_valid, w1, w3, w2)
    return _combine(x1, _gather_rows(y, dest), gate.T)


def _row(v, pad=0):
    v = v.reshape(1, -1).astype(F32)
    return jnp.pad(v, ((0, 0), (0, pad))) if pad else v


def _rows_at(w, start, total):
    return jnp.pad(w.astype(F32), ((start, total - start - w.shape[0]), (0, 0)))


def kernel(x, w_in, norm_mix, rwkv_mu, rwkv_decay_up, rwkv_w0, rwkv_a_up, rwkv_a0, rwkv_gate_up, rwkv_k_k, rwkv_k_a, rwkv_r_k, rwkv_ln_w, rwkv_ln_b, vres_down, vres_up, vres_v0, ssm_conv_w, ssm_conv_b, ssm_dt_bias, ssm_a_log, ssm_d, ssm_norm_w, att_q_gain, att_k_gain, att_rel_bias, gla_gate_up, gla_gate_bias, gla_norm_w, w_branch, w_out, norm_ffn, ffn_w1, ffn_w3, ffn_w2, moe_router, moe_w1, moe_w3, moe_w2):
    bsz, seq, _ = x.shape
    n_tok = bsz * seq
    depth = w_in.shape[0]
    x2 = x.reshape(n_tok, D_MODEL)
    v_first = None
    o_rw, o_ss, o_at = RWKV_COLS, RWKV_COLS + SSM_COLS, RWKV_COLS + SSM_COLS + ATT_COLS
    o_gl = o_at + GLA_COLS
    for l in range(depth):
        w = w_in[l]
        vd = vres_down[l - 1] if l > 0 else jnp.zeros((D_MODEL, 32), F32)
        w_mix, wg = _pack_weights(w, vd)
        moe = l % 2 == 1
        h, za, zb, zc, zd, *w3_b = _inproj(x2, _row(norm_mix[l]), w_mix, cast=(moe_w3[l // 2],) if moe else ())
        shp = lambda z: z.reshape(bsz, seq, z.shape[-1])

        rw = dict(mu=_row(rwkv_mu[l], RWKV_PAD - RWKV_COLS),
                  dup=_rows_at(rwkv_decay_up[l], 0, LANES), w0=_row(rwkv_w0[l]),
                  aup=_rows_at(rwkv_a_up[l], 64, LANES), a0=_row(rwkv_a0[l]),
                  gup=_rows_at(rwkv_gate_up[l], 0, 256).astype(BF16),
                  kk=_row(rwkv_k_k[l]), ka=_row(rwkv_k_a[l]), rk=_row(rwkv_r_k[l]),
                  lnw=_row(rwkv_ln_w[l]), lnb=_row(rwkv_ln_b[l]))
        if l > 0:
            rw.update(vup=_rows_at(vres_up[l - 1], 160, 256), v0=_row(vres_v0[l - 1]))
        ssm = dict(cw=ssm_conv_w[l], cb=_row(ssm_conv_b[l]), dtb=_row(ssm_dt_bias[l], LANES - SSM_HEADS),
                   alog=_row(ssm_a_log[l], LANES - SSM_HEADS), dexp=_row(jnp.repeat(ssm_d[l], CHUNK)),
                   nw=_row(ssm_norm_w[l]))
        gla = dict(gup=_rows_at(gla_gate_up[l], 0, LANES), gb=_row(gla_gate_bias[l]), nw=_row(gla_norm_w[l]))
        (o_c,), (o_d,), (o_b,), res_a = _run_parts(
            [_attention_part(shp(zc), att_q_gain[l], att_k_gain[l], att_rel_bias, MIXER_BLOCK),
             _gla_part(shp(zd), gla, MIXER_BLOCK), _mamba_part(shp(zb), ssm, MIXER_BLOCK),
             _rwkv_part(shp(za), v_first, rw, MIXER_BLOCK)],
            grid=(bsz, seq // MIXER_BLOCK), name="mixers")
        o_a = res_a[0]
        if l == 0:
            v_first = res_a[1]
        outs = [o.reshape(n_tok, BRANCH_DIM) for o in (o_a, o_b, o_c, o_d)]
        merge_w = (wg.astype(BF16), w_branch[l].astype(BF16), w_out[l].astype(BF16), _row(norm_ffn[l]))
        if not moe:
            x1, hf, *w1_b = _merge(x2, h, outs, *merge_w, cast=(moe_w1[l // 2],) if l + 1 < depth else ())
            x2 = _ffn(x1, hf, ffn_w1[l // 2].astype(BF16), ffn_w3[l // 2].astype(BF16),
                      ffn_w2[l // 2].astype(BF16))
        else:
            x1, *routing, w2_b = _merge(x2, h, outs, *merge_w, wr_t=moe_router[l // 2].T, cast=(moe_w2[l // 2],))
            x2 = _moe(x1, routing, w1_b[0], w3_b[0], w2_b)
    return x2.reshape(bsz, seq, D_MODEL)
```

```python
import functools
from typing import Any, NamedTuple

import jax
import jax.numpy as jnp
from jax import lax
from jax.experimental import pallas as pl
from jax.experimental.pallas import tpu as pltpu
from jax.experimental.pallas import tpu_sc as plsc

F32 = jnp.float32
BF16 = jnp.bfloat16

D_MODEL = 1024
CHUNK = 64
BRANCH_DIM = 512
NORM_EPS = 1e-6
LANES = 128
VMEM_LIMIT = 56 * 1024 * 1024
BIG_VMEM_LIMIT = 61 * 1024 * 1024

LOG2_E = 1.4426950408889634

RWKV_LN_EPS = 64e-5
RWKV_DECAY_SCALE = 0.6065306597126334
RWKV_COLS = 1824
RWKV_PAD = 1920
SSM_COLS = 1544
SSM_PAD = 1664
SSM_HEADS = 8
SSM_STATE = 128
ATT_COLS = 1536
ATT_HEADS = 8
ATT_LEFT = 8 * CHUNK
REL_CLIP = 2 * CHUNK
GLA_COLS = 1552
GLA_PAD = 1664
GLA_GATE_NORM = 16.0
FFN_DIM = 2816
N_EXPERTS = 8
EXPERT_DIM = 3584
MOE_ROWS = 512
ROW_SPLIT = 2
SUBROW = D_MODEL // (2 * ROW_SPLIT)


def _dot(a, b):
    return jnp.dot(a.astype(BF16), b.astype(BF16), preferred_element_type=F32)


def _dot_nt(a, b):
    return lax.dot_general(a.astype(BF16), b.astype(BF16), (((1,), (1,)), ((), ())),
                           preferred_element_type=F32)


def _dot_tn(a, b):
    return lax.dot_general(a.astype(BF16), b.astype(BF16), (((0,), (0,)), ((), ())),
                           preferred_element_type=F32)


def _hi_lo(a):
    hi = a.astype(BF16)
    return hi, (a - hi.astype(F32)).astype(BF16)


def _dot_3x(a, b):
    a_hi, a_lo = _hi_lo(a)
    b_hi, b_lo = _hi_lo(b)
    return (jnp.dot(a_hi, b_hi, preferred_element_type=F32) + jnp.dot(a_lo, b_hi, preferred_element_type=F32)
            + jnp.dot(a_hi, b_lo, preferred_element_type=F32))


def _dot_split(a, m):
    m = m.astype(BF16)
    hi, lo = _hi_lo(a)
    return jnp.dot(hi, m, preferred_element_type=F32) + jnp.dot(lo, m, preferred_element_type=F32)


def _dot_split_l(m, a):
    m = m.astype(BF16)
    hi, lo = _hi_lo(a)
    return jnp.dot(m, hi, preferred_element_type=F32) + jnp.dot(m, lo, preferred_element_type=F32)


def _pack_rows(x):
    bits = pltpu.bitcast(x.astype(BF16).astype(F32), jnp.uint32)
    half = D_MODEL // 2
    return [(bits[:, p * SUBROW:(p + 1) * SUBROW] & jnp.uint32(0xFFFF0000))
            | (bits[:, half + p * SUBROW:half + (p + 1) * SUBROW] >> jnp.uint32(16)) for p in range(ROW_SPLIT)]


def _unpack_rows(planes):
    hi = [pltpu.bitcast(w & jnp.uint32(0xFFFF0000), F32) for w in planes]
    lo = [pltpu.bitcast(w << jnp.uint32(16), F32) for w in planes]
    return jnp.concatenate(hi + lo, axis=-1)


def _softplus(x):
    return jnp.maximum(x, 0.0) + jnp.log(1.0 + jnp.exp(-jnp.abs(x)))


def _sigmoid(x):
    return 1.0 / (1.0 + jnp.exp(-x))


def _silu(x):
    return x * _sigmoid(x)


def _iota2(shape, axis):
    return lax.broadcasted_iota(jnp.int32, shape, axis)


def _group_mean_matrix(n, group):
    r = _iota2((n, n), 0) // group
    c = _iota2((n, n), 1) // group
    return jnp.where(r == c, 1.0 / group, 0.0).astype(F32)


def _chunk_tri(n):
    r = _iota2((n, n), 0)
    c = _iota2((n, n), 1)
    return jnp.where((r // CHUNK == c // CHUNK) & (r >= c), 1.0, 0.0).astype(F32)


def _resident(shape):
    nd = len(shape)
    return pl.BlockSpec(shape, lambda *_: (0,) * nd, pipeline_mode=pl.Buffered(1))


def _params(*sem, vmem_limit=VMEM_LIMIT):
    return pltpu.CompilerParams(dimension_semantics=sem, vmem_limit_bytes=vmem_limit)


_DONE = object()


class _Part(NamedTuple):
    body: Any
    inputs: list
    in_specs: list
    out_shapes: list
    out_specs: list
    scratch_shapes: list


def _run_parts(parts, grid, name):
    def split(refs, counts):
        out, lo = [], 0
        for n in counts:
            out.append(refs[lo:lo + n])
            lo += n
        return out

    n_in = [len(p.inputs) for p in parts]
    n_out = [len(p.out_shapes) for p in parts]
    n_scr = [len(p.scratch_shapes) for p in parts]

    def kernel(*refs):
        ins = split(refs[:sum(n_in)], n_in)
        outs = split(refs[sum(n_in):sum(n_in) + sum(n_out)], n_out)
        scr = split(refs[sum(n_in) + sum(n_out):], n_scr)
        stages = [part.body(*i, *o, *c) for part, i, o, c in zip(parts, ins, outs, scr)]
        while stages:
            for stage in list(stages):
                if next(stage, _DONE) is _DONE:
                    stages.remove(stage)

    flat = lambda field: [x for p in parts for x in getattr(p, field)]
    res = pl.pallas_call(
        kernel,
        grid=grid,
        in_specs=flat("in_specs"),
        out_specs=flat("out_specs"),
        out_shape=flat("out_shapes"),
        scratch_shapes=flat("scratch_shapes"),
        compiler_params=_params("parallel", "arbitrary", vmem_limit=BIG_VMEM_LIMIT),
        name=name,
    )(*flat("inputs"))
    return split(list(res), n_out)


def _cast_rider(w, steps):
    cols = w.shape[-1]
    sliced = (steps, w.size // cols // steps, cols)
    assert w.size // cols % steps == 0 and sliced[1] % 16 == 0
    return w.reshape(sliced), pl.BlockSpec((None,) + sliced[1:], lambda i: (i, 0, 0)), jax.ShapeDtypeStruct(sliced, BF16)


def _cast_slices(src_refs, dst_refs):
    for src, dst in zip(src_refs, dst_refs):
        dst[...] = src[...].astype(dst.dtype)


MIXER_COLS = (RWKV_PAD, SSM_PAD, ATT_COLS, GLA_PAD)
MIXER_BLOCK = 256


def _pack_kernel(w_ref, vd_ref, wmix_ref, wg_ref):
    w = w_ref[...]
    zeros = lambda n: jnp.zeros((w.shape[0], n), F32)
    o_ss, o_at = RWKV_COLS + SSM_COLS, RWKV_COLS + SSM_COLS + ATT_COLS
    o_gl = o_at + GLA_COLS
    pieces = [w[:, :RWKV_COLS], vd_ref[...], zeros(RWKV_PAD - RWKV_COLS - 32),
              w[:, RWKV_COLS:o_ss], zeros(SSM_PAD - SSM_COLS),
              w[:, o_ss:o_at],
              w[:, o_at:o_at + 1024], w[:, o_at + 1040:o_gl], w[:, o_at + 1024:o_at + 1040], zeros(GLA_PAD - GLA_COLS)]
    wmix_ref[...] = jnp.concatenate(pieces, axis=1).astype(BF16)
    wg_ref[...] = w[:, o_gl:].astype(BF16)


def _pack_weights(w_in, layer, vd, rows=128):
    n_cols = w_in.shape[2]
    n_gate = n_cols - (RWKV_COLS + SSM_COLS + ATT_COLS + GLA_COLS)
    blk = lambda n: pl.BlockSpec((rows, n), lambda i: (i, 0))
    return pl.pallas_call(
        _pack_kernel,
        grid=(D_MODEL // rows,),
        in_specs=[pl.BlockSpec((None, rows, n_cols), lambda i: (layer, i, 0)), blk(vd.shape[1])],
        out_specs=[blk(sum(MIXER_COLS)), blk(n_gate)],
        out_shape=[jax.ShapeDtypeStruct((D_MODEL, sum(MIXER_COLS)), BF16), jax.ShapeDtypeStruct((D_MODEL, n_gate), BF16)],
        compiler_params=_params("parallel"),
        name="pack_weights",
    )(w_in, vd)


def _inproj_kernel(n_cast, x_ref, g_ref, w_ref, *refs):
    cast_in, (h_ref, *z_refs), cast_out = refs[:n_cast], refs[n_cast:len(refs) - n_cast], refs[len(refs) - n_cast:]
    _cast_slices(cast_in, cast_out)
    x = x_ref[...]
    ms = jnp.mean(x * x, axis=-1, keepdims=True)
    h = (x * lax.rsqrt(ms + NORM_EPS) * g_ref[...]).astype(BF16)
    h_ref[...] = h
    lo = 0
    for pair in (z_refs[:2], z_refs[2:]):
        widths = [z_ref.shape[-1] for z_ref in pair]
        z = jnp.dot(h, w_ref[:, lo:lo + sum(widths)], preferred_element_type=F32)
        pair[0][...] = z[:, :widths[0]]
        pair[1][...] = z[:, widths[0]:]
        lo += sum(widths)


def _inproj(x2, g, w_mix, cast=(), tm=512):
    n_tok = x2.shape[0]
    row = lambda n: pl.BlockSpec((tm, n), lambda i: (i, 0))
    riders = [_cast_rider(w, n_tok // tm) for w in cast]
    res = pl.pallas_call(
        functools.partial(_inproj_kernel, len(cast)),
        grid=(n_tok // tm,),
        in_specs=[row(D_MODEL), _resident((1, D_MODEL)), _resident(w_mix.shape)] + [r[1] for r in riders],
        out_specs=[row(D_MODEL)] + [row(n) for n in MIXER_COLS] + [r[1] for r in riders],
        out_shape=[jax.ShapeDtypeStruct((n_tok, D_MODEL), BF16)]
        + [jax.ShapeDtypeStruct((n_tok, n), F32) for n in MIXER_COLS] + [r[2] for r in riders],
        compiler_params=_params("parallel", vmem_limit=BIG_VMEM_LIMIT if cast else VMEM_LIMIT),
        name="inproj",
    )(x2, g, w_mix, *[r[0] for r in riders])
    n_plain = len(res) - len(cast)
    return list(res[:n_plain]) + [wb16.reshape(w.shape) for wb16, w in zip(res[n_plain:], cast)]


def _rwkv_block(r, k, v, kk, ka, ld, tb, state_ref, y_ref):
    n_chunks = tb // CHUNK
    cum = _dot_split_l(_chunk_tri(tb), ld)
    p_inv = jnp.exp(-cum)
    a_t = -kk * jnp.exp(cum - ld)
    b_t = kk * ka * p_inv
    k_t = k * p_inv
    r_t = r * jnp.exp(cum)
    bf = lambda z: z.astype(BF16)
    a_b, b_b, k_b, r_b, v_b = bf(a_t), bf(b_t), bf(k_t), bf(r_t), bf(v)
    yield

    lane_head = _iota2((CHUNK, LANES), 1) // CHUNK
    row = _iota2((CHUNK, LANES), 0)
    col = _iota2((CHUNK, LANES), 1) % CHUNK
    strict = row > col
    incl = row >= col
    bd_mask = (_iota2((LANES, LANES), 0) // CHUNK) == (_iota2((LANES, LANES), 1) // CHUNK)
    eye = jnp.where(_iota2((LANES, LANES), 0) == _iota2((LANES, LANES), 1), 1.0, 0.0).astype(F32)

    def stack(z):
        return jnp.concatenate([jnp.where(lane_head == 0, z, 0.0), jnp.where(lane_head == 1, z, 0.0)], axis=0)

    def sub(z, u):
        c, p = u
        return z[c * CHUNK:(c + 1) * CHUNK, p * LANES:(p + 1) * LANES]

    def prepare(units):
        gram = {u: _dot_nt(jnp.concatenate([sub(a_b, u), sub(r_b, u)], axis=0),
                           jnp.concatenate([stack(sub(b_b, u)), stack(sub(k_b, u))], axis=0))
                for u in units}
        yield
        a_rb = {u: bf(jnp.where(incl, gram[u][CHUNK:, :LANES], 0.0)) for u in units}
        a_akrk = {u: bf(jnp.concatenate([jnp.where(strict, gram[u][:CHUNK, LANES:], 0.0),
                                         jnp.where(incl, gram[u][CHUNK:, LANES:], 0.0)], axis=0)) for u in units}
        pw = {u: stack(jnp.where(strict, gram[u][:CHUNK, :LANES], 0.0)) for u in units}
        t_bd = {u: eye + pw[u] for u in units}
        pw = {u: _dot(pw[u], pw[u]) for u in units}
        yield
        for _ in range(CHUNK.bit_length() - 3):
            sp = {u: _dot(jnp.concatenate([t_bd[u], pw[u]], axis=0), pw[u]) for u in units}
            t_bd = {u: t_bd[u] + sp[u][:LANES] for u in units}
            pw = {u: sp[u][LANES:] for u in units}
            yield
        t_bd = {u: bf(t_bd[u] + _dot(t_bd[u], pw[u])) for u in units}
        v_bd = {u: stack(sub(v_b, u)) for u in units}
        avyv = {u: _dot(a_akrk[u], v_bd[u]) for u in units}
        yield
        wuv = {u: _dot(t_bd[u], jnp.concatenate([stack(sub(a_b, u)), stack(bf(avyv[u][:CHUNK]))], axis=1))
               for u in units}
        wr = {u: jnp.concatenate([bf(wuv[u][:, :LANES]), sub(r_b, u)], axis=0) for u in units}
        yield
        return a_rb, avyv, wuv, wr

    a_rb, avyv, wuv, wr = yield from prepare([(c, p) for c in range(n_chunks) for p in range(4)])

    h = [state_ref[p] for p in range(4)]
    for c in range(n_chunks):
        last = c * CHUNK + CHUNK - 1
        p_last = jnp.exp(cum[last:last + 1, :])
        hs = [_dot_nt(wr[(c, p)], h[p]) for p in range(4)]
        u2 = [hs[p][:LANES] + wuv[(c, p)][:, LANES:] for p in range(4)]
        u_p = [u2[p][:CHUNK] + u2[p][CHUNK:] for p in range(4)]
        upd = [_dot_tn(jnp.concatenate([u_p[p], sub(v, (c, p))], axis=0),
                       jnp.concatenate([sub(b_t, (c, p)), sub(k_t, (c, p))], axis=0)
                       * p_last[:, p * LANES:(p + 1) * LANES]) for p in range(4)]
        for p in range(4):
            y_ref[c * CHUNK:(c + 1) * CHUNK, p * LANES:(p + 1) * LANES] = (
                hs[p][LANES:] + _dot(a_rb[(c, p)], stack(u_p[p])) + avyv[(c, p)][CHUNK:])
        h = [h[p] * p_last[:, p * LANES:(p + 1) * LANES] + jnp.where(bd_mask, upd[p], 0.0) for p in range(4)]
        yield
    for p in range(4):
        state_ref[p] = h[p]


def _rwkv_kernel(has_vres, tb, *refs):
    if has_vres:
        (za_ref, vfirst_ref, mu_ref, dup_ref, w0_ref, aup_ref, a0_ref, gup_ref, vup_ref, v0_ref,
         kk_ref, ka_ref, rk_ref, lnw_ref, lnb_ref, o_ref, prev_ref, state_ref, y_ref) = refs
    else:
        (za_ref, mu_ref, dup_ref, w0_ref, aup_ref, a0_ref, gup_ref,
         kk_ref, ka_ref, rk_ref, lnw_ref, lnb_ref, o_ref, vraw_ref, prev_ref, state_ref, y_ref) = refs

    @pl.when(pl.program_id(1) == 0)
    def _():
        prev_ref[...] = jnp.zeros_like(prev_ref)
        state_ref[...] = jnp.zeros_like(state_ref)

    yield
    za = za_ref[...]
    shifted = pltpu.roll(za, 1, axis=0)
    shifted = jnp.where(_iota2(za.shape, 0) == 0, prev_ref[7:8, :], shifted)
    prev_ref[...] = za[tb - 8:, :]
    za = za + (shifted - za) * mu_ref[...]

    r = za[:, 0:512]
    k = za[:, 512:1024]
    v = za[:, 1024:1536]
    xwa = za[:, 1536:1664]
    xg = za[:, 1664:1920]
    ld = -RWKV_DECAY_SCALE * _sigmoid(w0_ref[...] + _dot_3x(jnp.tanh(xwa), dup_ref[...]))
    a = _sigmoid(a0_ref[...] + _dot(xwa, aup_ref[...]))
    g = _dot(_sigmoid(xg), gup_ref[...])
    if has_vres:
        v_mix = _sigmoid(v0_ref[...] + _dot(xg, vup_ref[...]))
        v = v + (vfirst_ref[...] - v) * v_mix
    else:
        vraw_ref[...] = v
    kk = k * kk_ref[...]
    k = k * (1.0 + (a - 1.0) * ka_ref[...])
    head_sum = _group_mean_matrix(512, CHUNK) * float(CHUNK)
    kk = kk / jnp.maximum(jnp.sqrt(_dot(kk * kk, head_sum)), 1e-12)
    yield

    yield from _rwkv_block(r, k, v, kk, a, ld, tb, state_ref, y_ref)
    y = y_ref[...]
    head_mean = _group_mean_matrix(512, CHUNK)
    mean = _dot_split(y, head_mean)
    yc = y - mean
    var = _dot(yc * yc, head_mean)
    y = yc * lax.rsqrt(var + RWKV_LN_EPS) * lnw_ref[...] + lnb_ref[...]
    y = y + _dot(r * k * rk_ref[...], head_sum) * v
    o_ref[...] = (y * g).astype(o_ref.dtype)


def _rwkv_part(za, v_first, prm, tb):
    bsz, seq, _ = za.shape
    has_vres = v_first is not None
    blk = lambda n: pl.BlockSpec((None, tb, n), lambda b, i: (b, i, 0))
    names = (["mu", "dup", "w0", "aup", "a0", "gup"] + (["vup", "v0"] if has_vres else [])
             + ["kk", "ka", "rk", "lnw", "lnb"])
    weights = [prm[n] for n in names]
    ins = [za] + ([v_first] if has_vres else []) + weights
    in_specs = [blk(RWKV_PAD)] + ([blk(512)] if has_vres else []) + [_resident(w.shape) for w in weights]
    out_shape = [jax.ShapeDtypeStruct((bsz, seq, 512), BF16)]
    out_specs = [blk(512)]
    if not has_vres:
        out_shape.append(jax.ShapeDtypeStruct((bsz, seq, 512), F32))
        out_specs.append(blk(512))
    return _Part(functools.partial(_rwkv_kernel, has_vres, tb), ins, in_specs, out_shape, out_specs,
                 [pltpu.VMEM((8, RWKV_PAD), F32), pltpu.VMEM((4, LANES, LANES), F32), pltpu.VMEM((tb, 512), F32)])


def _mamba_kernel(tb, zb_ref, cw_ref, cb_ref, dtb_ref, alog_ref, dexp_ref, nw_ref,
                  o_ref, xbuf_ref, state_ref, y_ref):
    @pl.when(pl.program_id(1) == 0)
    def _():
        xbuf_ref[0:8, :] = jnp.zeros((8, 1024), F32)
        state_ref[...] = jnp.zeros_like(state_ref)

    yield
    xbuf_ref[8:8 + tb, :] = zb_ref[:, 512:1536]
    conv = cb_ref[...]
    for i in range(4):
        conv = conv + cw_ref[i:i + 1, :] * xbuf_ref[5 + i:5 + i + tb, :]
    xbuf_ref[0:8, :] = xbuf_ref[tb:tb + 8, :]
    xbc = _silu(conv)
    xs = xbc[:, 0:512]
    bm = xbc[:, 512:768]
    cm = xbc[:, 768:1024]
    yield

    lane = _iota2((1, LANES), 1)
    dt = _softplus(zb_ref[:, 1536:1664] + dtb_ref[...])
    a_neg = jnp.where(lane < SSM_HEADS, -jnp.exp(alog_ref[...]), 0.0)
    acs = _dot_split_l(_chunk_tri(tb), dt * a_neg)
    expand = jnp.where(_iota2((LANES, 512), 0) == _iota2((LANES, 512), 1) // CHUNK, 1.0, 0.0)
    dt_e = _dot(dt, expand)
    acs_e = _dot_split(acs, expand)
    xdt = xs * dt_e
    yield

    n_chunks = tb // CHUNK
    causal = _iota2((CHUNK, 512), 0) >= _iota2((CHUNK, 512), 1) % CHUNK
    pair_head = _iota2((CHUNK, LANES), 1) // CHUNK
    spread = jnp.where(_iota2((CHUNK, 512), 0) == _iota2((CHUNK, 512), 1) % CHUNK, 1.0, 0.0)
    own_head = _iota2((SSM_HEADS, 512), 0) == _iota2((SSM_HEADS, 512), 1) // CHUNK
    rows_of = lambda c: slice(c * CHUNK, (c + 1) * CHUNK)
    grp = lambda g: slice(g * SSM_STATE, (g + 1) * SSM_STATE)

    def stack(z):
        return jnp.concatenate([jnp.where(pair_head == 0, z, 0.0), jnp.where(pair_head == 1, z, 0.0)], axis=0)

    y_intra, upd, e_in, e_last = [], [], [], []
    for c in range(n_chunks):
        rows = rows_of(c)
        acs_ec = acs_e[rows]
        acs_t = acs[rows].T[0:SSM_HEADS]
        acs_row = jnp.sum(jnp.where(own_head, _dot_split(acs_t, spread), 0.0), axis=0, keepdims=True)
        decay = jnp.exp(jnp.where(causal, acs_ec - acs_row, -jnp.inf))
        cb = jnp.concatenate(
            [_dot_nt(cm[rows, grp(g)], jnp.concatenate([bm[rows, grp(g)]] * 4, axis=0)) for g in range(2)], axis=1)
        m = cb * decay
        yield
        xdt_c = xdt[rows]
        y_intra.append([_dot(m[:, p * LANES:(p + 1) * LANES], stack(xdt_c[:, p * LANES:(p + 1) * LANES]))
                        for p in range(4)])
        last_e = acs_ec[CHUNK - 1:CHUNK, :]
        x_out = xdt_c * jnp.exp(last_e - acs_ec)
        upd.append([_dot_tn(bm[rows, grp(g)], x_out[:, g * 256:(g + 1) * 256]) for g in range(2)])
        e_in.append(jnp.exp(acs_ec))
        e_last.append(jnp.exp(last_e))
        yield

    state = [state_ref[g] for g in range(2)]
    for c in range(n_chunks):
        rows = rows_of(c)
        for g in range(2):
            ls = slice(g * 256, (g + 1) * 256)
            y_in = jnp.concatenate(y_intra[c][2 * g:2 * g + 2], axis=1)
            y_ref[rows, ls] = y_in + _dot(cm[rows, grp(g)], state[g]) * e_in[c][:, ls]
            state[g] = state[g] * e_last[c][:, ls] + upd[c][g]
        yield
    for g in range(2):
        state_ref[g] = state[g]

    y = (y_ref[...] + xs * dexp_ref[...]) * _silu(zb_ref[:, 0:512])
    for g in range(2):
        ls = slice(g * 256, (g + 1) * 256)
        yg = y[:, ls]
        ms = jnp.mean(yg * yg, axis=-1, keepdims=True)
        o_ref[:, ls] = (yg * lax.rsqrt(ms + NORM_EPS) * nw_ref[:, ls]).astype(o_ref.dtype)


def _mamba_part(zb, prm, tb):
    bsz, seq, _ = zb.shape
    blk = lambda n: pl.BlockSpec((None, tb, n), lambda b, i: (b, i, 0))
    weights = [prm[n] for n in ("cw", "cb", "dtb", "alog", "dexp", "nw")]
    return _Part(functools.partial(_mamba_kernel, tb), [zb] + weights,
                 [blk(SSM_PAD)] + [_resident(w.shape) for w in weights],
                 [jax.ShapeDtypeStruct((bsz, seq, 512), BF16)], [blk(512)],
                 [pltpu.VMEM((tb + 8, 1024), F32), pltpu.VMEM((2, SSM_STATE, 256), F32), pltpu.VMEM((tb, 512), F32)])


def _attn_kernel(tq, qkv_ref, qg_ref, kg_ref, bias_ref, o_ref, kn_ref, vb_ref):
    i = pl.program_id(1)
    head_mean = _group_mean_matrix(512, CHUNK)
    win = tq + ATT_LEFT

    @pl.when(i == 0)
    def _():
        kn_ref[0:ATT_LEFT, :] = jnp.zeros((ATT_LEFT, 512), BF16)
        vb_ref[0:ATT_LEFT, :] = jnp.zeros((ATT_LEFT, 512), BF16)

    yield
    start = pl.multiple_of(i * tq, tq)
    k = qkv_ref[:, 512:1024]
    ms = _dot(k * k, head_mean)
    kn_ref[pl.ds(ATT_LEFT + start, tq), :] = (k * lax.rsqrt(ms + NORM_EPS) * kg_ref[...]).astype(BF16)
    vb_ref[pl.ds(ATT_LEFT + start, tq), :] = qkv_ref[:, 1024:1536].astype(BF16)
    q = qkv_ref[:, 0:512]
    ms = _dot(q * q, head_mean)
    qn = q * lax.rsqrt(ms + NORM_EPS) * qg_ref[...] * (CHUNK ** -0.5 * LOG2_E)
    kwin = kn_ref[pl.ds(start, win), :]
    vwin = vb_ref[pl.ds(start, win), :]
    lane_head = _iota2((tq, LANES), 1) // CHUNK
    yield
    for p in range(4):
        ls = slice(p * LANES, (p + 1) * LANES)
        q2 = jnp.concatenate([jnp.where(lane_head == s, qn[:, ls], 0.0) for s in range(2)], axis=0)
        sc = _dot_nt(q2, kwin[:, ls]) + bias_ref[2 * p:2 * p + 2].reshape(2 * tq, win).astype(F32)
        yield
        e = jnp.exp2(sc - jnp.max(sc, axis=-1, keepdims=True))
        yield
        o2 = _dot(e, vwin[:, ls]) / jnp.sum(e, axis=-1, keepdims=True)
        o_ref[:, ls] = jnp.where(lane_head == 0, o2[:tq], o2[tq:]).astype(o_ref.dtype)
        yield


def _band_bias(rel_bias, tq, win):
    period = tq + win
    m = jnp.arange(period)
    d = jnp.where(m < win, m, m - period) - ATT_LEFT
    f = (rel_bias.astype(F32)[:, jnp.clip(d, -REL_CLIP, REL_CLIP) + REL_CLIP] * LOG2_E).astype(BF16)
    g = jnp.tile(f, (1, tq))[:, :tq * (period - 1)].reshape(-1, tq, period - 1)
    bias = g[:, :, :win]
    left = ATT_LEFT // CHUNK
    qc = left + jnp.arange(tq)[:, None] // CHUNK
    kc = jnp.arange(win)[None, :] // CHUNK
    tiles = []
    for blk in range(ATT_LEFT // tq + 1):
        first_kc = jnp.maximum(qc - left, left - blk * (tq // CHUNK))
        tiles.append(jnp.where((kc <= qc) & (kc >= first_kc), bias, -jnp.inf))
    return jnp.stack(tiles)


def _attention_part(zc, q_gain, k_gain, rel_bias, tq):
    bsz, seq, _ = zc.shape
    win = tq + ATT_LEFT
    bias = _band_bias(rel_bias, tq, win)
    qg = jnp.tile(q_gain, ATT_HEADS)[None, :]
    kg = jnp.tile(k_gain, ATT_HEADS)[None, :]
    last_tile = bias.shape[0] - 1
    bias_spec = pl.BlockSpec((None,) + bias.shape[1:], lambda b, i: (jnp.minimum(i, last_tile), 0, 0, 0))
    blk = lambda n: pl.BlockSpec((None, tq, n), lambda b, i: (b, i, 0))
    return _Part(functools.partial(_attn_kernel, tq), [zc, qg, kg, bias],
                 [blk(ATT_COLS), _resident(qg.shape), _resident(kg.shape), bias_spec],
                 [jax.ShapeDtypeStruct((bsz, seq, 512), BF16)], [blk(512)],
                 [pltpu.VMEM((ATT_LEFT + seq, 512), BF16), pltpu.VMEM((ATT_LEFT + seq, 512), BF16)])


def _gla_kernel(tb, zd_ref, gup_ref, gb_ref, nw_ref, o_ref, state_ref, y_ref):
    @pl.when(pl.program_id(1) == 0)
    def _():
        state_ref[...] = jnp.zeros_like(state_ref)

    yield
    q = zd_ref[:, 0:256] * (CHUNK ** -0.5)
    k = zd_ref[:, 256:512]
    v = zd_ref[:, 512:1024]
    log_a = -_softplus(-(_dot_3x(zd_ref[:, 1536:1664], gup_ref[...]) + gb_ref[...])) / GLA_GATE_NORM
    bcum = _dot_split_l(_chunk_tri(tb), log_a)

    k_head = _iota2((CHUNK, 256), 1) // CHUNK
    v_head = _iota2((CHUNK, 512), 1) // LANES
    causal = _iota2((CHUNK, 256), 0) >= _iota2((CHUNK, 256), 1) % CHUNK
    bd = (_iota2((512, 256), 0) // LANES) == (_iota2((512, 256), 1) // CHUNK)
    n_chunks = tb // CHUNK
    qg_all = (q * jnp.exp(bcum)).astype(BF16)
    kg_all = k * jnp.exp(-bcum)
    yield
    qg, att, o_intra, upd, e_last = [], [], [], [], []
    for c in range(n_chunks):
        rows = slice(c * CHUNK, (c + 1) * CHUNK)
        kg_bd = jnp.concatenate([jnp.where(k_head == h, kg_all[rows], 0.0) for h in range(4)], axis=0)
        qg.append(qg_all[rows])
        att.append(jnp.where(causal, _dot_nt(qg[c], kg_bd), 0.0))
        yield
    for c in range(n_chunks):
        rows = slice(c * CHUNK, (c + 1) * CHUNK)
        bc = bcum[rows]
        blast = bc[CHUNK - 1:CHUNK, :]
        v_c = v[rows]
        v_bd = jnp.concatenate([jnp.where(v_head == h, v_c, 0.0) for h in range(4)], axis=0)
        o_intra.append(_dot(att[c], v_bd))
        upd.append(jnp.where(bd, _dot_tn(v_c, k[rows] * jnp.exp(blast - bc)), 0.0))
        e_last.append(jnp.exp(blast))
        yield
    st = state_ref[...]
    for c in range(n_chunks):
        y_ref[c * CHUNK:(c + 1) * CHUNK, :] = o_intra[c] + _dot_nt(qg[c], st)
        st = st * e_last[c] + upd[c]
        yield
    state_ref[...] = st

    o = y_ref[...]
    for h in range(4):
        ls = slice(h * LANES, (h + 1) * LANES)
        oh = o[:, ls]
        ms = jnp.mean(oh * oh, axis=-1, keepdims=True)
        o_ref[:, ls] = (oh * lax.rsqrt(ms + NORM_EPS) * nw_ref[...]
                        * _silu(zd_ref[:, 1024 + h * LANES:1024 + (h + 1) * LANES])).astype(o_ref.dtype)


def _gla_part(zd, prm, tb):
    bsz, seq, _ = zd.shape
    blk = lambda n: pl.BlockSpec((None, tb, n), lambda b, i: (b, i, 0))
    weights = [prm[n] for n in ("gup", "gb", "nw")]
    return _Part(functools.partial(_gla_kernel, tb), [zd] + weights,
                 [blk(GLA_PAD)] + [_resident(w.shape) for w in weights],
                 [jax.ShapeDtypeStruct((bsz, seq, 512), BF16)], [blk(512)],
                 [pltpu.VMEM((512, 256), F32), pltpu.VMEM((tb, 512), F32)])


def _route(tm, hf, wr_ref, idx_ref, gate_ref, rank_ref, cnt_ref, hf_ref, carry_ref):
    @pl.when(pl.program_id(0) == 0)
    def _():
        carry_ref[...] = jnp.zeros_like(carry_ref)

    for j, words in enumerate(_pack_rows(hf)):
        hf_ref[j] = words
    nt = lambda a, b: lax.dot_general(a, b, (((1,), (1,)), ((), ())), preferred_element_type=F32)
    w_hi, w_lo = _hi_lo(wr_ref[...])
    h_hi, h_lo = _hi_lo(hf)
    logits = nt(w_hi, h_hi) + nt(w_lo, h_hi) + nt(w_hi, h_lo)
    e_iota = _iota2((N_EXPERTS, tm), 0)
    m1 = jnp.max(logits, axis=0, keepdims=True)
    i1 = jnp.min(jnp.where(logits == m1, e_iota, N_EXPERTS), axis=0, keepdims=True)
    rest = jnp.where(e_iota == i1, -jnp.inf, logits)
    m2 = jnp.max(rest, axis=0, keepdims=True)
    i2 = jnp.min(jnp.where(rest == m2, e_iota, N_EXPERTS), axis=0, keepdims=True)
    e2 = jnp.exp(m2 - m1)
    gate_ref[0:1, :] = 1.0 / (1.0 + e2)
    gate_ref[1:2, :] = e2 / (1.0 + e2)
    idx_ref[0:1, :] = i1
    idx_ref[1:2, :] = i2
    hit1 = jnp.where(e_iota == i1, 1.0, 0.0)
    hit2 = jnp.where(e_iota == i2, 1.0, 0.0)
    before = jnp.where(_iota2((tm, tm), 0) < _iota2((tm, tm), 1), 1.0, 0.0)
    prior = _dot(hit1 + hit2, before) + carry_ref[:, 0:1]
    rank_ref[0:1, :] = jnp.sum(hit1 * prior, axis=0, keepdims=True).astype(jnp.int32)
    rank_ref[1:2, :] = jnp.sum(hit2 * prior, axis=0, keepdims=True).astype(jnp.int32)
    carry_ref[...] = carry_ref[...] + jnp.sum(hit1 + hit2, axis=1, keepdims=True)
    cnt_ref[...] = carry_ref[...]


def _merge_kernel(tm, route, n_cast, *refs):
    n_in, n_out = (11, 6) if route else (10, 2)
    ins, cast_in = refs[:n_in], refs[n_in:n_in + n_cast]
    outs = refs[n_in + n_cast:n_in + n_cast + n_out]
    cast_out = refs[n_in + n_cast + n_out:n_in + 2 * n_cast + n_out]
    scratch = refs[n_in + 2 * n_cast + n_out:]
    x_ref, h_ref, oa_ref, ob_ref, oc_ref, od_ref, wg_ref, wb_ref, wo_ref, nf_ref = ins[:10]
    _cast_slices(cast_in, cast_out)
    h = h_ref[...]
    acc = jnp.zeros(x_ref.shape, F32)
    for i, o_ref in enumerate((oa_ref, ob_ref, oc_ref, od_ref)):
        gate = _sigmoid(jnp.dot(h, wg_ref[:, i * D_MODEL:(i + 1) * D_MODEL], preferred_element_type=F32))
        acc = acc + gate * jnp.dot(o_ref[...], wb_ref[i], preferred_element_type=F32)
    x1 = x_ref[...] + jnp.dot(acc.astype(BF16), wo_ref[...], preferred_element_type=F32)
    ms = jnp.mean(x1 * x1, axis=-1, keepdims=True)
    hf = x1 * lax.rsqrt(ms + NORM_EPS) * nf_ref[...]
    if route:
        _route(tm, hf, ins[10], *outs[1:], *scratch)
    else:
        outs[1][...] = hf.astype(BF16)
    outs[0][...] = x1


def _merge(x2, h, outs, wg, wb, wo, nf, wr_t=None, cast=(), tm=512):
    n_tok = x2.shape[0]
    route = wr_t is not None
    steps = n_tok // tm
    row = lambda n: pl.BlockSpec((tm, n), lambda i: (i, 0))
    ins = [x2, h, *outs, wg, wb, wo, nf]
    in_specs = ([row(D_MODEL), row(D_MODEL)] + [row(BRANCH_DIM)] * 4
                + [_resident(wg.shape), _resident(wb.shape), _resident(wo.shape), _resident(nf.shape)])
    out_specs = [row(D_MODEL)]
    out_shape = [jax.ShapeDtypeStruct((n_tok, D_MODEL), F32)]
    scratch = []
    if route:
        ins.append(wr_t)
        in_specs.append(_resident(wr_t.shape))
        col = pl.BlockSpec((2, tm), lambda i: (0, i))
        out_specs += [col, col, col, pl.BlockSpec((N_EXPERTS, LANES), lambda i: (0, 0)),
                      pl.BlockSpec((ROW_SPLIT, tm, SUBROW), lambda i: (0, i, 0))]
        out_shape += [jax.ShapeDtypeStruct((2, n_tok), jnp.int32), jax.ShapeDtypeStruct((2, n_tok), F32),
                      jax.ShapeDtypeStruct((2, n_tok), jnp.int32), jax.ShapeDtypeStruct((N_EXPERTS, LANES), F32),
                      jax.ShapeDtypeStruct((ROW_SPLIT, n_tok, SUBROW), jnp.uint32)]
        scratch = [pltpu.VMEM((N_EXPERTS, LANES), F32)]
    else:
        out_specs.append(row(D_MODEL))
        out_shape.append(jax.ShapeDtypeStruct((n_tok, D_MODEL), BF16))
    riders = [_cast_rider(w, steps) for w in cast]
    res = pl.pallas_call(
        functools.partial(_merge_kernel, tm, route, len(cast)),
        grid=(steps,),
        in_specs=in_specs + [r[1] for r in riders],
        out_specs=out_specs + [r[1] for r in riders],
        out_shape=out_shape + [r[2] for r in riders],
        scratch_shapes=scratch,
        compiler_params=_params("arbitrary" if route else "parallel"),
        name="merge",
    )(*ins, *[r[0] for r in riders])
    n_plain = len(res) - len(cast)
    return list(res[:n_plain]) + [wb16.reshape(w.shape) for wb16, w in zip(res[n_plain:], cast)]


def _ffn_kernel(tf, x1_ref, hf_ref, w1_ref, w3_ref, w2_ref, o_ref):
    hf = hf_ref[...]
    acc = x1_ref[...]
    for lo in range(0, FFN_DIM, tf):
        cols = slice(lo, min(lo + tf, FFN_DIM))
        a = jnp.dot(hf, w1_ref[:, cols], preferred_element_type=F32)
        b = jnp.dot(hf, w3_ref[:, cols], preferred_element_type=F32)
        acc = acc + jnp.dot((_silu(a) * b).astype(BF16), w2_ref[cols, :], preferred_element_type=F32)
    o_ref[...] = acc


def _ffn(x1, hf, w1, w3, w2, tm=512, tf=512):
    n_tok = x1.shape[0]
    row = pl.BlockSpec((tm, D_MODEL), lambda i: (i, 0))
    return pl.pallas_call(
        functools.partial(_ffn_kernel, tf),
        grid=(n_tok // tm,),
        in_specs=[row, row, _resident(w1.shape), _resident(w3.shape), _resident(w2.shape)],
        out_specs=row,
        out_shape=jax.ShapeDtypeStruct((n_tok, D_MODEL), F32),
        compiler_params=_params("parallel"),
        name="ffn",
    )(x1, hf, w1, w3, w2)


def _gather_rows(table, idx, window=128):
    split, n_table, width = table.shape
    n = idx.shape[0]
    flat_idx = (idx[None, :] + n_table * jnp.arange(split, dtype=jnp.int32)[:, None]).reshape(-1)
    return _gather_subrows(table.reshape(split * n_table, width), flat_idx, window).reshape(split, n, width)


def _gather_subrows(table, idx, window):
    n = idx.shape[0]
    d = table.shape[1]
    mesh = plsc.VectorSubcoreMesh(core_axis_name="core", subcore_axis_name="subcore")

    @functools.partial(pl.kernel, out_type=jax.ShapeDtypeStruct((n, d), table.dtype), mesh=mesh,
                       name="gather_rows")
    def gather(table_hbm, idx_hbm, out_hbm):
        def body(idx_vmem, out_vmem):
            pltpu.sync_copy(table_hbm.at[idx_vmem.at[0]], out_vmem)

        pltpu.emit_pipeline(
            body,
            grid=(n // window,),
            in_specs=[pl.BlockSpec((1, window), index_map=lambda i: (0, i))],
            out_specs=[pl.BlockSpec((window, d), index_map=lambda i: (i, 0))],
            core_axis_name=("core", "subcore"),
            dimension_semantics=(pltpu.PARALLEL,),
        )(idx_hbm, out_hbm)

    return gather(table, idx.reshape(1, n))


def _scatter_rows(table, dest, n_out, window=128):
    split, n_table, width = table.shape
    n = dest.shape[0]
    flat_dest = (dest[None, :] + n_out * jnp.arange(split, dtype=jnp.int32)[:, None]).reshape(1, split * n)
    src_blocks, blocks = n_table // window, n // window
    mesh = plsc.VectorSubcoreMesh(core_axis_name="core", subcore_axis_name="subcore")

    @functools.partial(pl.kernel, out_type=jax.ShapeDtypeStruct((split * n_out, width), table.dtype), mesh=mesh,
                       name="scatter_rows")
    def scatter(table_hbm, dest_hbm, out_hbm):
        def body(rows_vmem, dest_vmem):
            pltpu.sync_copy(rows_vmem, out_hbm.at[dest_vmem.at[0]])

        pltpu.emit_pipeline(
            body,
            grid=(split * blocks,),
            in_specs=[pl.BlockSpec((window, width),
                                   index_map=lambda i: ((i // blocks) * src_blocks + (i % blocks) % src_blocks, 0)),
                      pl.BlockSpec((1, window), index_map=lambda i: (0, i))],
            out_specs=[],
            core_axis_name=("core", "subcore"),
            dimension_semantics=(pltpu.PARALLEL,),
        )(table_hbm, dest_hbm)

    return scatter(table.reshape(split * n_table, width), flat_dest).reshape(split, n_out, width)


def _expert_kernel(tf, ge_ref, nv_ref, x_ref, w1_ref, w3_ref, w2_ref, y_ref):
    g = pl.program_id(0)

    @pl.when(g < nv_ref[0])
    def _():
        x = _unpack_rows([x_ref[j] for j in range(ROW_SPLIT)]).astype(BF16)
        acc = jnp.zeros((MOE_ROWS, D_MODEL), F32)
        for lo in range(0, EXPERT_DIM, tf):
            a = jnp.dot(x, w1_ref[:, lo:lo + tf], preferred_element_type=F32)
            b = jnp.dot(x, w3_ref[:, lo:lo + tf], preferred_element_type=F32)
            acc = acc + jnp.dot((_silu(a) * b).astype(BF16), w2_ref[lo:lo + tf, :], preferred_element_type=F32)
        for j, words in enumerate(_pack_rows(acc)):
            y_ref[j] = words

    @pl.when(g >= nv_ref[0])
    def _():
        y_ref[...] = jnp.zeros_like(y_ref)


def _experts(xg, group_expert, n_valid, w1, w3, w2, tf=512):
    n_rows = xg.shape[1]
    rows = pl.BlockSpec((ROW_SPLIT, MOE_ROWS, SUBROW), lambda g, ge, nv: (0, g, 0))
    return pl.pallas_call(
        functools.partial(_expert_kernel, tf),
        grid_spec=pltpu.PrefetchScalarGridSpec(
            num_scalar_prefetch=2,
            grid=(n_rows // MOE_ROWS,),
            in_specs=[rows,
                      pl.BlockSpec((None, D_MODEL, EXPERT_DIM), lambda g, ge, nv: (ge[g], 0, 0)),
                      pl.BlockSpec((None, D_MODEL, EXPERT_DIM), lambda g, ge, nv: (ge[g], 0, 0)),
                      pl.BlockSpec((None, EXPERT_DIM, D_MODEL), lambda g, ge, nv: (ge[g], 0, 0))],
            out_specs=rows,
        ),
        out_shape=jax.ShapeDtypeStruct((ROW_SPLIT, n_rows, SUBROW), jnp.uint32),
        compiler_params=_params("arbitrary", vmem_limit=BIG_VMEM_LIMIT),
        name="experts",
    )(group_expert, n_valid, xg, w1, w3, w2)


def _combine_kernel(x1_ref, y0_ref, y1_ref, gate_ref, o_ref):
    gate = gate_ref[...]
    y0 = _unpack_rows([y0_ref[j] for j in range(ROW_SPLIT)])
    y1 = _unpack_rows([y1_ref[j] for j in range(ROW_SPLIT)])
    o_ref[...] = x1_ref[...] + gate[:, 0:1] * y0 + gate[:, 1:2] * y1


def _combine(x1, yg, gate_t, tm=512):
    n_tok = x1.shape[0]
    nb = n_tok // tm
    return pl.pallas_call(
        _combine_kernel,
        grid=(nb,),
        in_specs=[pl.BlockSpec((tm, D_MODEL), lambda i: (i, 0)),
                  pl.BlockSpec((ROW_SPLIT, tm, SUBROW), lambda i: (0, i, 0)),
                  pl.BlockSpec((ROW_SPLIT, tm, SUBROW), lambda i: (0, i + nb, 0)),
                  pl.BlockSpec((tm, 2), lambda i: (i, 0))],
        out_specs=pl.BlockSpec((tm, D_MODEL), lambda i: (i, 0)),
        out_shape=jax.ShapeDtypeStruct((n_tok, D_MODEL), F32),
        compiler_params=_params("parallel"),
        name="moe_combine",
    )(x1, yg, yg, gate_t)


def _moe(x1, routing, w1, w3, w2):
    n_tok = x1.shape[0]
    idx, gate, rank, cnt, hf = routing
    counts = cnt[:, 0].astype(jnp.int32)
    padded = (counts + MOE_ROWS - 1) // MOE_ROWS * MOE_ROWS
    end_padded = jnp.cumsum(padded)
    start_padded = end_padded - padded
    start = sum(jnp.where(idx == e, start_padded[e], 0) for e in range(N_EXPERTS))
    dest = (start + rank).reshape(-1)
    n_groups = (n_tok * 2 + MOE_ROWS - 1) // MOE_ROWS + N_EXPERTS
    n_rows = n_groups * MOE_ROWS
    group_row = jnp.arange(n_groups, dtype=jnp.int32)[:, None] * MOE_ROWS
    group_expert = jnp.minimum(jnp.sum(group_row >= end_padded[None, :], axis=1), N_EXPERTS - 1).astype(jnp.int32)
    n_valid = (end_padded[-1:] // MOE_ROWS).astype(jnp.int32)
    fill = jnp.arange(MOE_ROWS, dtype=jnp.int32)[None, :]
    pad_dest = jnp.where(fill < (padded - counts)[:, None], (start_padded + counts)[:, None] + fill,
                         n_rows - MOE_ROWS + fill).reshape(-1)
    xg = _scatter_rows(hf, jnp.concatenate([dest, pad_dest]), n_rows)
    y = _experts(xg, group_expert, n_valid, w1, w3, w2)
    return _combine(x1, _gather_rows(y, dest), gate.T)


def _row(v, pad=0):
    v = v.reshape(1, -1).astype(F32)
    return jnp.pad(v, ((0, 0), (0, pad))) if pad else v


def _rows_at(w, start, total):
    return jnp.pad(w.astype(F32), ((start, total - start - w.shape[0]), (0, 0)))


def kernel(x, w_in, norm_mix, rwkv_mu, rwkv_decay_up, rwkv_w0, rwkv_a_up, rwkv_a0, rwkv_gate_up, rwkv_k_k, rwkv_k_a, rwkv_r_k, rwkv_ln_w, rwkv_ln_b, vres_down, vres_up, vres_v0, ssm_conv_w, ssm_conv_b, ssm_dt_bias, ssm_a_log, ssm_d, ssm_norm_w, att_q_gain, att_k_gain, att_rel_bias, gla_gate_up, gla_gate_bias, gla_norm_w, w_branch, w_out, norm_ffn, ffn_w1, ffn_w3, ffn_w2, moe_router, moe_w1, moe_w3, moe_w2):
    bsz, seq, _ = x.shape
    n_tok = bsz * seq
    depth = w_in.shape[0]
    x2 = x.reshape(n_tok, D_MODEL)
    v_first = None
    for l in range(depth):
        vd = vres_down[l - 1] if l > 0 else jnp.zeros((D_MODEL, 32), F32)
        w_mix, wg = _pack_weights(w_in, l, vd)
        moe = l % 2 == 1
        up_w = [moe_w3[l // 2]] if moe else [ffn_w1[l // 2], ffn_w3[l // 2]]
        h, za, zb, zc, zd, wb_b, wo_b, *up_b = _inproj(x2, _row(norm_mix[l]), w_mix,
                                                       cast=[w_branch[l], w_out[l]] + up_w)
        shp = lambda z: z.reshape(bsz, seq, z.shape[-1])

        rw = dict(mu=_row(rwkv_mu[l], RWKV_PAD - RWKV_COLS),
                  dup=_rows_at(rwkv_decay_up[l], 0, LANES), w0=_row(rwkv_w0[l]),
                  aup=_rows_at(rwkv_a_up[l], 64, LANES), a0=_row(rwkv_a0[l]),
                  gup=_rows_at(rwkv_gate_up[l], 0, 256).astype(BF16),
                  kk=_row(rwkv_k_k[l]), ka=_row(rwkv_k_a[l]), rk=_row(rwkv_r_k[l]),
                  lnw=_row(rwkv_ln_w[l]), lnb=_row(rwkv_ln_b[l]))
        if l > 0:
            rw.update(vup=_rows_at(vres_up[l - 1], 160, 256), v0=_row(vres_v0[l - 1]))
        ssm = dict(cw=ssm_conv_w[l], cb=_row(ssm_conv_b[l]), dtb=_row(ssm_dt_bias[l], LANES - SSM_HEADS),
                   alog=_row(ssm_a_log[l], LANES - SSM_HEADS), dexp=_row(jnp.repeat(ssm_d[l], CHUNK)),
                   nw=_row(ssm_norm_w[l]))
        gla = dict(gup=_rows_at(gla_gate_up[l], 0, LANES), gb=_row(gla_gate_bias[l]), nw=_row(gla_norm_w[l]))
        (o_c,), (o_d,), (o_b,), res_a = _run_parts(
            [_attention_part(shp(zc), att_q_gain[l], att_k_gain[l], att_rel_bias, MIXER_BLOCK),
             _gla_part(shp(zd), gla, MIXER_BLOCK), _mamba_part(shp(zb), ssm, MIXER_BLOCK),
             _rwkv_part(shp(za), v_first, rw, MIXER_BLOCK)],
            grid=(bsz, seq // MIXER_BLOCK), name="mixers")
        o_a = res_a[0]
        if l == 0:
            v_first = res_a[1]
        outs = [o.reshape(n_tok, BRANCH_DIM) for o in (o_a, o_b, o_c, o_d)]
        merge_w = (wg, wb_b, wo_b, _row(norm_ffn[l]))
        if not moe:
            x1, hf, *w1_b = _merge(x2, h, outs, *merge_w, cast=(moe_w1[l // 2],) if l + 1 < depth else ())
            x2 = _ffn(x1, hf, *up_b, ffn_w2[l // 2].astype(BF16))
        else:
            x1, *routing, w2_b = _merge(x2, h, outs, *merge_w, wr_t=moe_router[l // 2].T, cast=(moe_w2[l // 2],))
            x2 = _moe(x1, routing, w1_b[0], up_b[0], w2_b)
    return x2.reshape(bsz, seq, D_MODEL)
```

```python
import functools
from typing import Any, NamedTuple

import jax
import jax.numpy as jnp
from jax import lax
from jax.experimental import pallas as pl
from jax.experimental.pallas import tpu as pltpu
from jax.experimental.pallas import tpu_sc as plsc

F32 = jnp.float32
BF16 = jnp.bfloat16

D_MODEL = 1024
CHUNK = 64
BRANCH_DIM = 512
NORM_EPS = 1e-6
LANES = 128
VMEM_LIMIT = 56 * 1024 * 1024
BIG_VMEM_LIMIT = 61 * 1024 * 1024

LOG2_E = 1.4426950408889634

RWKV_LN_EPS = 64e-5
RWKV_DECAY_SCALE = 0.6065306597126334
RWKV_COLS = 1824
RWKV_PAD = 1920
SSM_COLS = 1544
SSM_PAD = 1664
SSM_HEADS = 8
SSM_STATE = 128
ATT_COLS = 1536
ATT_HEADS = 8
ATT_LEFT = 8 * CHUNK
REL_CLIP = 2 * CHUNK
GLA_COLS = 1552
GLA_PAD = 1664
GLA_GATE_NORM = 16.0
FFN_DIM = 2816
N_EXPERTS = 8
EXPERT_DIM = 3584
MOE_ROWS = 512
ROW_SPLIT = 2
SUBROW = D_MODEL // (2 * ROW_SPLIT)


def _dot(a, b):
    return jnp.dot(a.astype(BF16), b.astype(BF16), preferred_element_type=F32)


def _dot_nt(a, b):
    return lax.dot_general(a.astype(BF16), b.astype(BF16), (((1,), (1,)), ((), ())),
                           preferred_element_type=F32)


def _dot_tn(a, b):
    return lax.dot_general(a.astype(BF16), b.astype(BF16), (((0,), (0,)), ((), ())),
                           preferred_element_type=F32)


def _hi_lo(a):
    hi = a.astype(BF16)
    return hi, (a - hi.astype(F32)).astype(BF16)


def _dot_3x(a, b):
    a_hi, a_lo = _hi_lo(a)
    b_hi, b_lo = _hi_lo(b)
    return (jnp.dot(a_hi, b_hi, preferred_element_type=F32) + jnp.dot(a_lo, b_hi, preferred_element_type=F32)
            + jnp.dot(a_hi, b_lo, preferred_element_type=F32))


def _dot_split(a, m):
    m = m.astype(BF16)
    hi, lo = _hi_lo(a)
    return jnp.dot(hi, m, preferred_element_type=F32) + jnp.dot(lo, m, preferred_element_type=F32)


def _dot_split_l(m, a):
    m = m.astype(BF16)
    hi, lo = _hi_lo(a)
    return jnp.dot(m, hi, preferred_element_type=F32) + jnp.dot(m, lo, preferred_element_type=F32)


def _pack_rows(x):
    bits = pltpu.bitcast(x.astype(BF16).astype(F32), jnp.uint32)
    half = D_MODEL // 2
    return [(bits[:, p * SUBROW:(p + 1) * SUBROW] & jnp.uint32(0xFFFF0000))
            | (bits[:, half + p * SUBROW:half + (p + 1) * SUBROW] >> jnp.uint32(16)) for p in range(ROW_SPLIT)]


def _unpack_rows(planes):
    hi = [pltpu.bitcast(w & jnp.uint32(0xFFFF0000), F32) for w in planes]
    lo = [pltpu.bitcast(w << jnp.uint32(16), F32) for w in planes]
    return jnp.concatenate(hi + lo, axis=-1)


def _softplus(x):
    return jnp.maximum(x, 0.0) + jnp.log(1.0 + jnp.exp(-jnp.abs(x)))


def _sigmoid(x):
    return 1.0 / (1.0 + jnp.exp(-x))


def _silu(x):
    return x * _sigmoid(x)


def _iota2(shape, axis):
    return lax.broadcasted_iota(jnp.int32, shape, axis)


def _group_mean_matrix(n, group):
    r = _iota2((n, n), 0) // group
    c = _iota2((n, n), 1) // group
    return jnp.where(r == c, 1.0 / group, 0.0).astype(F32)


def _chunk_tri(n):
    r = _iota2((n, n), 0)
    c = _iota2((n, n), 1)
    return jnp.where((r // CHUNK == c // CHUNK) & (r >= c), 1.0, 0.0).astype(F32)


def _resident(shape):
    nd = len(shape)
    return pl.BlockSpec(shape, lambda *_: (0,) * nd, pipeline_mode=pl.Buffered(1))


def _params(*sem, vmem_limit=VMEM_LIMIT):
    return pltpu.CompilerParams(dimension_semantics=sem, vmem_limit_bytes=vmem_limit)


_DONE = object()


class _Part(NamedTuple):
    body: Any
    inputs: list
    in_specs: list
    out_shapes: list
    out_specs: list
    scratch_shapes: list


def _run_parts(parts, grid, name):
    def split(refs, counts):
        out, lo = [], 0
        for n in counts:
            out.append(refs[lo:lo + n])
            lo += n
        return out

    n_in = [len(p.inputs) for p in parts]
    n_out = [len(p.out_shapes) for p in parts]
    n_scr = [len(p.scratch_shapes) for p in parts]

    def kernel(*refs):
        ins = split(refs[:sum(n_in)], n_in)
        outs = split(refs[sum(n_in):sum(n_in) + sum(n_out)], n_out)
        scr = split(refs[sum(n_in) + sum(n_out):], n_scr)
        stages = [part.body(*i, *o, *c) for part, i, o, c in zip(parts, ins, outs, scr)]
        while stages:
            for stage in list(stages):
                if next(stage, _DONE) is _DONE:
                    stages.remove(stage)

    flat = lambda field: [x for p in parts for x in getattr(p, field)]
    res = pl.pallas_call(
        kernel,
        grid=grid,
        in_specs=flat("in_specs"),
        out_specs=flat("out_specs"),
        out_shape=flat("out_shapes"),
        scratch_shapes=flat("scratch_shapes"),
        compiler_params=_params("parallel", "arbitrary", vmem_limit=BIG_VMEM_LIMIT),
        name=name,
    )(*flat("inputs"))
    return split(list(res), n_out)


def _cast_rider(w, steps):
    cols = w.shape[-1]
    sliced = (steps, w.size // cols // steps, cols)
    assert w.size // cols % steps == 0 and sliced[1] % 16 == 0
    return w.reshape(sliced), pl.BlockSpec((None,) + sliced[1:], lambda i: (i, 0, 0)), jax.ShapeDtypeStruct(sliced, BF16)


def _cast_slices(src_refs, dst_refs):
    for src, dst in zip(src_refs, dst_refs):
        dst[...] = src[...].astype(dst.dtype)


MIXER_COLS = (RWKV_PAD, SSM_PAD, ATT_COLS, GLA_PAD)
MIXER_BLOCK = 256


def _pack_kernel(w_ref, vd_ref, wmix_ref, wg_ref):
    w = w_ref[...]
    zeros = lambda n: jnp.zeros((w.shape[0], n), F32)
    o_ss, o_at = RWKV_COLS + SSM_COLS, RWKV_COLS + SSM_COLS + ATT_COLS
    o_gl = o_at + GLA_COLS
    pieces = [w[:, :RWKV_COLS], vd_ref[...], zeros(RWKV_PAD - RWKV_COLS - 32),
              w[:, RWKV_COLS:o_ss], zeros(SSM_PAD - SSM_COLS),
              w[:, o_ss:o_at],
              w[:, o_at:o_at + 1024], w[:, o_at + 1040:o_gl], w[:, o_at + 1024:o_at + 1040], zeros(GLA_PAD - GLA_COLS)]
    wmix_ref[...] = jnp.concatenate(pieces, axis=1).astype(BF16)
    wg_ref[...] = w[:, o_gl:].astype(BF16)


def _pack_weights(w_in, layer, vd, rows=128):
    n_cols = w_in.shape[2]
    n_gate = n_cols - (RWKV_COLS + SSM_COLS + ATT_COLS + GLA_COLS)
    blk = lambda n: pl.BlockSpec((rows, n), lambda i: (i, 0))
    return pl.pallas_call(
        _pack_kernel,
        grid=(D_MODEL // rows,),
        in_specs=[pl.BlockSpec((rows, n_cols), lambda i: (layer * (D_MODEL // rows) + i, 0)), blk(vd.shape[1])],
        out_specs=[blk(sum(MIXER_COLS)), blk(n_gate)],
        out_shape=[jax.ShapeDtypeStruct((D_MODEL, sum(MIXER_COLS)), BF16), jax.ShapeDtypeStruct((D_MODEL, n_gate), BF16)],
        compiler_params=_params("parallel"),
        name="pack_weights",
    )(w_in.reshape(-1, n_cols), vd)


def _inproj_kernel(n_cast, x_ref, g_ref, w_ref, *refs):
    cast_in, (h_ref, *z_refs), cast_out = refs[:n_cast], refs[n_cast:len(refs) - n_cast], refs[len(refs) - n_cast:]
    _cast_slices(cast_in, cast_out)
    x = x_ref[...]
    ms = jnp.mean(x * x, axis=-1, keepdims=True)
    h = (x * lax.rsqrt(ms + NORM_EPS) * g_ref[...]).astype(BF16)
    h_ref[...] = h
    lo = 0
    for pair in (z_refs[:2], z_refs[2:]):
        widths = [z_ref.shape[-1] for z_ref in pair]
        z = jnp.dot(h, w_ref[:, lo:lo + sum(widths)], preferred_element_type=F32)
        pair[0][...] = z[:, :widths[0]]
        pair[1][...] = z[:, widths[0]:]
        lo += sum(widths)


def _inproj(x2, g, w_mix, cast=(), tm=512):
    n_tok = x2.shape[0]
    row = lambda n: pl.BlockSpec((tm, n), lambda i: (i, 0))
    riders = [_cast_rider(w, n_tok // tm) for w in cast]
    res = pl.pallas_call(
        functools.partial(_inproj_kernel, len(cast)),
        grid=(n_tok // tm,),
        in_specs=[row(D_MODEL), _resident((1, D_MODEL)), _resident(w_mix.shape)] + [r[1] for r in riders],
        out_specs=[row(D_MODEL)] + [row(n) for n in MIXER_COLS] + [r[1] for r in riders],
        out_shape=[jax.ShapeDtypeStruct((n_tok, D_MODEL), BF16)]
        + [jax.ShapeDtypeStruct((n_tok, n), F32) for n in MIXER_COLS] + [r[2] for r in riders],
        compiler_params=_params("parallel", vmem_limit=BIG_VMEM_LIMIT if cast else VMEM_LIMIT),
        name="inproj",
    )(x2, g, w_mix, *[r[0] for r in riders])
    n_plain = len(res) - len(cast)
    return list(res[:n_plain]) + [wb16.reshape(w.shape) for wb16, w in zip(res[n_plain:], cast)]


def _rwkv_block(r, k, v, kk, ka, ld, tb, state_ref, y_ref):
    n_chunks = tb // CHUNK
    cum = _dot_split_l(_chunk_tri(tb), ld)
    p_inv = jnp.exp(-cum)
    a_t = -kk * jnp.exp(cum - ld)
    b_t = kk * ka * p_inv
    k_t = k * p_inv
    r_t = r * jnp.exp(cum)
    bf = lambda z: z.astype(BF16)
    a_b, b_b, k_b, r_b, v_b = bf(a_t), bf(b_t), bf(k_t), bf(r_t), bf(v)
    yield

    lane_head = _iota2((CHUNK, LANES), 1) // CHUNK
    row = _iota2((CHUNK, LANES), 0)
    col = _iota2((CHUNK, LANES), 1) % CHUNK
    strict = row > col
    incl = row >= col
    bd_mask = (_iota2((LANES, LANES), 0) // CHUNK) == (_iota2((LANES, LANES), 1) // CHUNK)
    eye = jnp.where(_iota2((LANES, LANES), 0) == _iota2((LANES, LANES), 1), 1.0, 0.0).astype(F32)

    def stack(z):
        return jnp.concatenate([jnp.where(lane_head == 0, z, 0.0), jnp.where(lane_head == 1, z, 0.0)], axis=0)

    def sub(z, u):
        c, p = u
        return z[c * CHUNK:(c + 1) * CHUNK, p * LANES:(p + 1) * LANES]

    def prepare(units):
        gram = {u: _dot_nt(jnp.concatenate([sub(a_b, u), sub(r_b, u)], axis=0),
                           jnp.concatenate([stack(sub(b_b, u)), stack(sub(k_b, u))], axis=0))
                for u in units}
        yield
        a_rb = {u: bf(jnp.where(incl, gram[u][CHUNK:, :LANES], 0.0)) for u in units}
        a_akrk = {u: bf(jnp.concatenate([jnp.where(strict, gram[u][:CHUNK, LANES:], 0.0),
                                         jnp.where(incl, gram[u][CHUNK:, LANES:], 0.0)], axis=0)) for u in units}
        pw = {u: stack(jnp.where(strict, gram[u][:CHUNK, :LANES], 0.0)) for u in units}
        t_bd = {u: eye + pw[u] for u in units}
        pw = {u: _dot(pw[u], pw[u]) for u in units}
        yield
        for _ in range(CHUNK.bit_length() - 3):
            sp = {u: _dot(jnp.concatenate([t_bd[u], pw[u]], axis=0), pw[u]) for u in units}
            t_bd = {u: t_bd[u] + sp[u][:LANES] for u in units}
            pw = {u: sp[u][LANES:] for u in units}
            yield
        t_bd = {u: bf(t_bd[u] + _dot(t_bd[u], pw[u])) for u in units}
        v_bd = {u: stack(sub(v_b, u)) for u in units}
        avyv = {u: _dot(a_akrk[u], v_bd[u]) for u in units}
        yield
        wuv = {u: _dot(t_bd[u], jnp.concatenate([stack(sub(a_b, u)), stack(bf(avyv[u][:CHUNK]))], axis=1))
               for u in units}
        wr = {u: jnp.concatenate([bf(wuv[u][:, :LANES]), sub(r_b, u)], axis=0) for u in units}
        yield
        return a_rb, avyv, wuv, wr

    a_rb, avyv, wuv, wr = yield from prepare([(c, p) for c in range(n_chunks) for p in range(4)])

    h = [state_ref[p] for p in range(4)]
    for c in range(n_chunks):
        last = c * CHUNK + CHUNK - 1
        p_last = jnp.exp(cum[last:last + 1, :])
        hs = [_dot_nt(wr[(c, p)], h[p]) for p in range(4)]
        u2 = [hs[p][:LANES] + wuv[(c, p)][:, LANES:] for p in range(4)]
        u_p = [u2[p][:CHUNK] + u2[p][CHUNK:] for p in range(4)]
        upd = [_dot_tn(jnp.concatenate([u_p[p], sub(v, (c, p))], axis=0),
                       jnp.concatenate([sub(b_t, (c, p)), sub(k_t, (c, p))], axis=0)
                       * p_last[:, p * LANES:(p + 1) * LANES]) for p in range(4)]
        for p in range(4):
            y_ref[c * CHUNK:(c + 1) * CHUNK, p * LANES:(p + 1) * LANES] = (
                hs[p][LANES:] + _dot(a_rb[(c, p)], stack(u_p[p])) + avyv[(c, p)][CHUNK:])
        h = [h[p] * p_last[:, p * LANES:(p + 1) * LANES] + jnp.where(bd_mask, upd[p], 0.0) for p in range(4)]
        yield
    for p in range(4):
        state_ref[p] = h[p]


def _rwkv_kernel(has_vres, tb, *refs):
    if has_vres:
        (za_ref, vfirst_ref, mu_ref, dup_ref, w0_ref, aup_ref, a0_ref, gup_ref, vup_ref, v0_ref,
         kk_ref, ka_ref, rk_ref, lnw_ref, lnb_ref, o_ref, prev_ref, state_ref, y_ref) = refs
    else:
        (za_ref, mu_ref, dup_ref, w0_ref, aup_ref, a0_ref, gup_ref,
         kk_ref, ka_ref, rk_ref, lnw_ref, lnb_ref, o_ref, vraw_ref, prev_ref, state_ref, y_ref) = refs

    @pl.when(pl.program_id(1) == 0)
    def _():
        prev_ref[...] = jnp.zeros_like(prev_ref)
        state_ref[...] = jnp.zeros_like(state_ref)

    yield
    za = za_ref[...]
    shifted = pltpu.roll(za, 1, axis=0)
    shifted = jnp.where(_iota2(za.shape, 0) == 0, prev_ref[7:8, :], shifted)
    prev_ref[...] = za[tb - 8:, :]
    za = za + (shifted - za) * mu_ref[...]

    r = za[:, 0:512]
    k = za[:, 512:1024]
    v = za[:, 1024:1536]
    xwa = za[:, 1536:1664]
    xg = za[:, 1664:1920]
    ld = -RWKV_DECAY_SCALE * _sigmoid(w0_ref[...] + _dot_3x(jnp.tanh(xwa), dup_ref[...]))
    a = _sigmoid(a0_ref[...] + _dot(xwa, aup_ref[...]))
    g = _dot(_sigmoid(xg), gup_ref[...])
    if has_vres:
        v_mix = _sigmoid(v0_ref[...] + _dot(xg, vup_ref[...]))
        v = v + (vfirst_ref[...] - v) * v_mix
    else:
        vraw_ref[...] = v
    kk = k * kk_ref[...]
    k = k * (1.0 + (a - 1.0) * ka_ref[...])
    head_sum = _group_mean_matrix(512, CHUNK) * float(CHUNK)
    kk = kk / jnp.maximum(jnp.sqrt(_dot(kk * kk, head_sum)), 1e-12)
    yield

    yield from _rwkv_block(r, k, v, kk, a, ld, tb, state_ref, y_ref)
    y = y_ref[...]
    head_mean = _group_mean_matrix(512, CHUNK)
    mean = _dot_split(y, head_mean)
    yc = y - mean
    var = _dot(yc * yc, head_mean)
    y = yc * lax.rsqrt(var + RWKV_LN_EPS) * lnw_ref[...] + lnb_ref[...]
    y = y + _dot(r * k * rk_ref[...], head_sum) * v
    o_ref[...] = (y * g).astype(o_ref.dtype)


def _rwkv_part(za, v_first, prm, tb):
    bsz, seq, _ = za.shape
    has_vres = v_first is not None
    blk = lambda n: pl.BlockSpec((None, tb, n), lambda b, i: (b, i, 0))
    names = (["mu", "dup", "w0", "aup", "a0", "gup"] + (["vup", "v0"] if has_vres else [])
             + ["kk", "ka", "rk", "lnw", "lnb"])
    weights = [prm[n] for n in names]
    ins = [za] + ([v_first] if has_vres else []) + weights
    in_specs = [blk(RWKV_PAD)] + ([blk(512)] if has_vres else []) + [_resident(w.shape) for w in weights]
    out_shape = [jax.ShapeDtypeStruct((bsz, seq, 512), BF16)]
    out_specs = [blk(512)]
    if not has_vres:
        out_shape.append(jax.ShapeDtypeStruct((bsz, seq, 512), F32))
        out_specs.append(blk(512))
    return _Part(functools.partial(_rwkv_kernel, has_vres, tb), ins, in_specs, out_shape, out_specs,
                 [pltpu.VMEM((8, RWKV_PAD), F32), pltpu.VMEM((4, LANES, LANES), F32), pltpu.VMEM((tb, 512), F32)])


def _mamba_kernel(tb, zb_ref, cw_ref, cb_ref, dtb_ref, alog_ref, dexp_ref, nw_ref,
                  o_ref, xbuf_ref, state_ref, y_ref):
    @pl.when(pl.program_id(1) == 0)
    def _():
        xbuf_ref[0:8, :] = jnp.zeros((8, 1024), F32)
        state_ref[...] = jnp.zeros_like(state_ref)

    yield
    xbuf_ref[8:8 + tb, :] = zb_ref[:, 512:1536]
    conv = cb_ref[...]
    for i in range(4):
        conv = conv + cw_ref[i:i + 1, :] * xbuf_ref[5 + i:5 + i + tb, :]
    xbuf_ref[0:8, :] = xbuf_ref[tb:tb + 8, :]
    xbc = _silu(conv)
    xs = xbc[:, 0:512]
    bm = xbc[:, 512:768]
    cm = xbc[:, 768:1024]
    yield

    lane = _iota2((1, LANES), 1)
    dt = _softplus(zb_ref[:, 1536:1664] + dtb_ref[...])
    a_neg = jnp.where(lane < SSM_HEADS, -jnp.exp(alog_ref[...]), 0.0)
    acs = _dot_split_l(_chunk_tri(tb), dt * a_neg)
    expand = jnp.where(_iota2((LANES, 512), 0) == _iota2((LANES, 512), 1) // CHUNK, 1.0, 0.0)
    dt_e = _dot(dt, expand)
    acs_e = _dot_split(acs, expand)
    xdt = xs * dt_e
    yield

    n_chunks = tb // CHUNK
    causal = _iota2((CHUNK, 512), 0) >= _iota2((CHUNK, 512), 1) % CHUNK
    pair_head = _iota2((CHUNK, LANES), 1) // CHUNK
    spread = jnp.where(_iota2((CHUNK, 512), 0) == _iota2((CHUNK, 512), 1) % CHUNK, 1.0, 0.0)
    own_head = _iota2((SSM_HEADS, 512), 0) == _iota2((SSM_HEADS, 512), 1) // CHUNK
    rows_of = lambda c: slice(c * CHUNK, (c + 1) * CHUNK)
    grp = lambda g: slice(g * SSM_STATE, (g + 1) * SSM_STATE)

    def stack(z):
        return jnp.concatenate([jnp.where(pair_head == 0, z, 0.0), jnp.where(pair_head == 1, z, 0.0)], axis=0)

    y_intra, upd, e_in, e_last = [], [], [], []
    for c in range(n_chunks):
        rows = rows_of(c)
        acs_ec = acs_e[rows]
        acs_t = acs[rows].T[0:SSM_HEADS]
        acs_row = jnp.sum(jnp.where(own_head, _dot_split(acs_t, spread), 0.0), axis=0, keepdims=True)
        decay = jnp.exp(jnp.where(causal, acs_ec - acs_row, -jnp.inf))
        cb = jnp.concatenate(
            [_dot_nt(cm[rows, grp(g)], jnp.concatenate([bm[rows, grp(g)]] * 4, axis=0)) for g in range(2)], axis=1)
        m = cb * decay
        yield
        xdt_c = xdt[rows]
        y_intra.append([_dot(m[:, p * LANES:(p + 1) * LANES], stack(xdt_c[:, p * LANES:(p + 1) * LANES]))
                        for p in range(4)])
        last_e = acs_ec[CHUNK - 1:CHUNK, :]
        x_out = xdt_c * jnp.exp(last_e - acs_ec)
        upd.append([_dot_tn(bm[rows, grp(g)], x_out[:, g * 256:(g + 1) * 256]) for g in range(2)])
        e_in.append(jnp.exp(acs_ec))
        e_last.append(jnp.exp(last_e))
        yield

    state = [state_ref[g] for g in range(2)]
    for c in range(n_chunks):
        rows = rows_of(c)
        for g in range(2):
            ls = slice(g * 256, (g + 1) * 256)
            y_in = jnp.concatenate(y_intra[c][2 * g:2 * g + 2], axis=1)
            y_ref[rows, ls] = y_in + _dot(cm[rows, grp(g)], state[g]) * e_in[c][:, ls]
            state[g] = state[g] * e_last[c][:, ls] + upd[c][g]
        yield
    for g in range(2):
        state_ref[g] = state[g]

    y = (y_ref[...] + xs * dexp_ref[...]) * _silu(zb_ref[:, 0:512])
    for g in range(2):
        ls = slice(g * 256, (g + 1) * 256)
        yg = y[:, ls]
        ms = jnp.mean(yg * yg, axis=-1, keepdims=True)
        o_ref[:, ls] = (yg * lax.rsqrt(ms + NORM_EPS) * nw_ref[:, ls]).astype(o_ref.dtype)


def _mamba_part(zb, prm, tb):
    bsz, seq, _ = zb.shape
    blk = lambda n: pl.BlockSpec((None, tb, n), lambda b, i: (b, i, 0))
    weights = [prm[n] for n in ("cw", "cb", "dtb", "alog", "dexp", "nw")]
    return _Part(functools.partial(_mamba_kernel, tb), [zb] + weights,
                 [blk(SSM_PAD)] + [_resident(w.shape) for w in weights],
                 [jax.ShapeDtypeStruct((bsz, seq, 512), BF16)], [blk(512)],
                 [pltpu.VMEM((tb + 8, 1024), F32), pltpu.VMEM((2, SSM_STATE, 256), F32), pltpu.VMEM((tb, 512), F32)])


def _attn_kernel(tq, qkv_ref, qg_ref, kg_ref, bias_ref, o_ref, kn_ref, vb_ref):
    i = pl.program_id(1)
    head_mean = _group_mean_matrix(512, CHUNK)
    win = tq + ATT_LEFT

    @pl.when(i == 0)
    def _():
        kn_ref[0:ATT_LEFT, :] = jnp.zeros((ATT_LEFT, 512), BF16)
        vb_ref[0:ATT_LEFT, :] = jnp.zeros((ATT_LEFT, 512), BF16)

    yield
    start = pl.multiple_of(i * tq, tq)
    k = qkv_ref[:, 512:1024]
    ms = _dot(k * k, head_mean)
    kn_ref[pl.ds(ATT_LEFT + start, tq), :] = (k * lax.rsqrt(ms + NORM_EPS) * kg_ref[...]).astype(BF16)
    vb_ref[pl.ds(ATT_LEFT + start, tq), :] = qkv_ref[:, 1024:1536].astype(BF16)
    q = qkv_ref[:, 0:512]
    ms = _dot(q * q, head_mean)
    qn = q * lax.rsqrt(ms + NORM_EPS) * qg_ref[...] * (CHUNK ** -0.5 * LOG2_E)
    kwin = kn_ref[pl.ds(start, win), :]
    vwin = vb_ref[pl.ds(start, win), :]
    lane_head = _iota2((tq, LANES), 1) // CHUNK
    yield
    for p in range(4):
        ls = slice(p * LANES, (p + 1) * LANES)
        q2 = jnp.concatenate([jnp.where(lane_head == s, qn[:, ls], 0.0) for s in range(2)], axis=0)
        sc = _dot_nt(q2, kwin[:, ls]) + bias_ref[2 * p:2 * p + 2].reshape(2 * tq, win).astype(F32)
        yield
        e = jnp.exp2(sc - jnp.max(sc, axis=-1, keepdims=True))
        yield
        o2 = _dot(e, vwin[:, ls]) / jnp.sum(e, axis=-1, keepdims=True)
        o_ref[:, ls] = jnp.where(lane_head == 0, o2[:tq], o2[tq:]).astype(o_ref.dtype)
        yield


def _band_bias(rel_bias, tq, win):
    period = tq + win
    m = jnp.arange(period)
    d = jnp.where(m < win, m, m - period) - ATT_LEFT
    f = (rel_bias.astype(F32)[:, jnp.clip(d, -REL_CLIP, REL_CLIP) + REL_CLIP] * LOG2_E).astype(BF16)
    g = jnp.tile(f, (1, tq))[:, :tq * (period - 1)].reshape(-1, tq, period - 1)
    bias = g[:, :, :win]
    left = ATT_LEFT // CHUNK
    qc = left + jnp.arange(tq)[:, None] // CHUNK
    kc = jnp.arange(win)[None, :] // CHUNK
    tiles = []
    for blk in range(ATT_LEFT // tq + 1):
        first_kc = jnp.maximum(qc - left, left - blk * (tq // CHUNK))
        tiles.append(jnp.where((kc <= qc) & (kc >= first_kc), bias, -jnp.inf))
    return jnp.stack(tiles)


def _attention_part(zc, q_gain, k_gain, rel_bias, tq):
    bsz, seq, _ = zc.shape
    win = tq + ATT_LEFT
    bias = _band_bias(rel_bias, tq, win)
    qg = jnp.tile(q_gain, ATT_HEADS)[None, :]
    kg = jnp.tile(k_gain, ATT_HEADS)[None, :]
    last_tile = bias.shape[0] - 1
    bias_spec = pl.BlockSpec((None,) + bias.shape[1:], lambda b, i: (jnp.minimum(i, last_tile), 0, 0, 0))
    blk = lambda n: pl.BlockSpec((None, tq, n), lambda b, i: (b, i, 0))
    return _Part(functools.partial(_attn_kernel, tq), [zc, qg, kg, bias],
                 [blk(ATT_COLS), _resident(qg.shape), _resident(kg.shape), bias_spec],
                 [jax.ShapeDtypeStruct((bsz, seq, 512), BF16)], [blk(512)],
                 [pltpu.VMEM((ATT_LEFT + seq, 512), BF16), pltpu.VMEM((ATT_LEFT + seq, 512), BF16)])


def _gla_kernel(tb, zd_ref, gup_ref, gb_ref, nw_ref, o_ref, state_ref, y_ref):
    @pl.when(pl.program_id(1) == 0)
    def _():
        state_ref[...] = jnp.zeros_like(state_ref)

    yield
    q = zd_ref[:, 0:256] * (CHUNK ** -0.5)
    k = zd_ref[:, 256:512]
    v = zd_ref[:, 512:1024]
    log_a = -_softplus(-(_dot_3x(zd_ref[:, 1536:1664], gup_ref[...]) + gb_ref[...])) / GLA_GATE_NORM
    bcum = _dot_split_l(_chunk_tri(tb), log_a)

    k_head = _iota2((CHUNK, 256), 1) // CHUNK
    v_head = _iota2((CHUNK, 512), 1) // LANES
    causal = _iota2((CHUNK, 256), 0) >= _iota2((CHUNK, 256), 1) % CHUNK
    bd = (_iota2((512, 256), 0) // LANES) == (_iota2((512, 256), 1) // CHUNK)
    n_chunks = tb // CHUNK
    qg_all = (q * jnp.exp(bcum)).astype(BF16)
    kg_all = k * jnp.exp(-bcum)
    yield
    qg, att, o_intra, upd, e_last = [], [], [], [], []
    for c in range(n_chunks):
        rows = slice(c * CHUNK, (c + 1) * CHUNK)
        kg_bd = jnp.concatenate([jnp.where(k_head == h, kg_all[rows], 0.0) for h in range(4)], axis=0)
        qg.append(qg_all[rows])
        att.append(jnp.where(causal, _dot_nt(qg[c], kg_bd), 0.0))
        yield
    for c in range(n_chunks):
        rows = slice(c * CHUNK, (c + 1) * CHUNK)
        bc = bcum[rows]
        blast = bc[CHUNK - 1:CHUNK, :]
        v_c = v[rows]
        v_bd = jnp.concatenate([jnp.where(v_head == h, v_c, 0.0) for h in range(4)], axis=0)
        o_intra.append(_dot(att[c], v_bd))
        upd.append(jnp.where(bd, _dot_tn(v_c, k[rows] * jnp.exp(blast - bc)), 0.0))
        e_last.append(jnp.exp(blast))
        yield
    st = state_ref[...]
    for c in range(n_chunks):
        y_ref[c * CHUNK:(c + 1) * CHUNK, :] = o_intra[c] + _dot_nt(qg[c], st)
        st = st * e_last[c] + upd[c]
        yield
    state_ref[...] = st

    o = y_ref[...]
    for h in range(4):
        ls = slice(h * LANES, (h + 1) * LANES)
        oh = o[:, ls]
        ms = jnp.mean(oh * oh, axis=-1, keepdims=True)
        o_ref[:, ls] = (oh * lax.rsqrt(ms + NORM_EPS) * nw_ref[...]
                        * _silu(zd_ref[:, 1024 + h * LANES:1024 + (h + 1) * LANES])).astype(o_ref.dtype)


def _gla_part(zd, prm, tb):
    bsz, seq, _ = zd.shape
    blk = lambda n: pl.BlockSpec((None, tb, n), lambda b, i: (b, i, 0))
    weights = [prm[n] for n in ("gup", "gb", "nw")]
    return _Part(functools.partial(_gla_kernel, tb), [zd] + weights,
                 [blk(GLA_PAD)] + [_resident(w.shape) for w in weights],
                 [jax.ShapeDtypeStruct((bsz, seq, 512), BF16)], [blk(512)],
                 [pltpu.VMEM((512, 256), F32), pltpu.VMEM((tb, 512), F32)])


def _route(tm, hf, wr_ref, idx_ref, gate_ref, rank_ref, cnt_ref, hf_ref, carry_ref):
    @pl.when(pl.program_id(0) == 0)
    def _():
        carry_ref[...] = jnp.zeros_like(carry_ref)

    for j, words in enumerate(_pack_rows(hf)):
        hf_ref[j] = words
    nt = lambda a, b: lax.dot_general(a, b, (((1,), (1,)), ((), ())), preferred_element_type=F32)
    w_hi, w_lo = _hi_lo(wr_ref[...])
    h_hi, h_lo = _hi_lo(hf)
    logits = nt(w_hi, h_hi) + nt(w_lo, h_hi) + nt(w_hi, h_lo)
    e_iota = _iota2((N_EXPERTS, tm), 0)
    m1 = jnp.max(logits, axis=0, keepdims=True)
    i1 = jnp.min(jnp.where(logits == m1, e_iota, N_EXPERTS), axis=0, keepdims=True)
    rest = jnp.where(e_iota == i1, -jnp.inf, logits)
    m2 = jnp.max(rest, axis=0, keepdims=True)
    i2 = jnp.min(jnp.where(rest == m2, e_iota, N_EXPERTS), axis=0, keepdims=True)
    e2 = jnp.exp(m2 - m1)
    gate_ref[0:1, :] = 1.0 / (1.0 + e2)
    gate_ref[1:2, :] = e2 / (1.0 + e2)
    idx_ref[0:1, :] = i1
    idx_ref[1:2, :] = i2
    hit1 = jnp.where(e_iota == i1, 1.0, 0.0)
    hit2 = jnp.where(e_iota == i2, 1.0, 0.0)
    before = jnp.where(_iota2((tm, tm), 0) < _iota2((tm, tm), 1), 1.0, 0.0)
    prior = _dot(hit1 + hit2, before) + carry_ref[:, 0:1]
    rank_ref[0:1, :] = jnp.sum(hit1 * prior, axis=0, keepdims=True).astype(jnp.int32)
    rank_ref[1:2, :] = jnp.sum(hit2 * prior, axis=0, keepdims=True).astype(jnp.int32)
    carry_ref[...] = carry_ref[...] + jnp.sum(hit1 + hit2, axis=1, keepdims=True)
    cnt_ref[...] = carry_ref[...]


def _merge_kernel(tm, route, n_cast, *refs):
    n_in, n_out = (11, 6) if route else (10, 2)
    ins, cast_in = refs[:n_in], refs[n_in:n_in + n_cast]
    outs = refs[n_in + n_cast:n_in + n_cast + n_out]
    cast_out = refs[n_in + n_cast + n_out:n_in + 2 * n_cast + n_out]
    scratch = refs[n_in + 2 * n_cast + n_out:]
    x_ref, h_ref, oa_ref, ob_ref, oc_ref, od_ref, wg_ref, wb_ref, wo_ref, nf_ref = ins[:10]
    _cast_slices(cast_in, cast_out)
    h = h_ref[...]
    acc = jnp.zeros(x_ref.shape, F32)
    for i, o_ref in enumerate((oa_ref, ob_ref, oc_ref, od_ref)):
        gate = _sigmoid(jnp.dot(h, wg_ref[:, i * D_MODEL:(i + 1) * D_MODEL], preferred_element_type=F32))
        acc = acc + gate * jnp.dot(o_ref[...], wb_ref[i], preferred_element_type=F32)
    x1 = x_ref[...] + jnp.dot(acc.astype(BF16), wo_ref[...], preferred_element_type=F32)
    ms = jnp.mean(x1 * x1, axis=-1, keepdims=True)
    hf = x1 * lax.rsqrt(ms + NORM_EPS) * nf_ref[...]
    if route:
        _route(tm, hf, ins[10], *outs[1:], *scratch)
    else:
        outs[1][...] = hf.astype(BF16)
    outs[0][...] = x1


def _merge(x2, h, outs, wg, wb, wo, nf, wr_t=None, cast=(), tm=512):
    n_tok = x2.shape[0]
    route = wr_t is not None
    steps = n_tok // tm
    row = lambda n: pl.BlockSpec((tm, n), lambda i: (i, 0))
    ins = [x2, h, *outs, wg, wb, wo, nf]
    in_specs = ([row(D_MODEL), row(D_MODEL)] + [row(BRANCH_DIM)] * 4
                + [_resident(wg.shape), _resident(wb.shape), _resident(wo.shape), _resident(nf.shape)])
    out_specs = [row(D_MODEL)]
    out_shape = [jax.ShapeDtypeStruct((n_tok, D_MODEL), F32)]
    scratch = []
    if route:
        ins.append(wr_t)
        in_specs.append(_resident(wr_t.shape))
        col = pl.BlockSpec((2, tm), lambda i: (0, i))
        out_specs += [col, col, col, pl.BlockSpec((N_EXPERTS, LANES), lambda i: (0, 0)),
                      pl.BlockSpec((ROW_SPLIT, tm, SUBROW), lambda i: (0, i, 0))]
        out_shape += [jax.ShapeDtypeStruct((2, n_tok), jnp.int32), jax.ShapeDtypeStruct((2, n_tok), F32),
                      jax.ShapeDtypeStruct((2, n_tok), jnp.int32), jax.ShapeDtypeStruct((N_EXPERTS, LANES), F32),
                      jax.ShapeDtypeStruct((ROW_SPLIT, n_tok, SUBROW), jnp.uint32)]
        scratch = [pltpu.VMEM((N_EXPERTS, LANES), F32)]
    else:
        out_specs.append(row(D_MODEL))
        out_shape.append(jax.ShapeDtypeStruct((n_tok, D_MODEL), BF16))
    riders = [_cast_rider(w, steps) for w in cast]
    res = pl.pallas_call(
        functools.partial(_merge_kernel, tm, route, len(cast)),
        grid=(steps,),
        in_specs=in_specs + [r[1] for r in riders],
        out_specs=out_specs + [r[1] for r in riders],
        out_shape=out_shape + [r[2] for r in riders],
        scratch_shapes=scratch,
        compiler_params=_params("arbitrary" if route else "parallel"),
        name="merge",
    )(*ins, *[r[0] for r in riders])
    n_plain = len(res) - len(cast)
    return list(res[:n_plain]) + [wb16.reshape(w.shape) for wb16, w in zip(res[n_plain:], cast)]


def _ffn_kernel(tf, x1_ref, hf_ref, w1_ref, w3_ref, w2_ref, o_ref):
    hf = hf_ref[...]
    acc = x1_ref[...]
    for lo in range(0, FFN_DIM, tf):
        cols = slice(lo, min(lo + tf, FFN_DIM))
        a = jnp.dot(hf, w1_ref[:, cols], preferred_element_type=F32)
        b = jnp.dot(hf, w3_ref[:, cols], preferred_element_type=F32)
        acc = acc + jnp.dot((_silu(a) * b).astype(BF16), w2_ref[cols, :], preferred_element_type=F32)
    o_ref[...] = acc


def _ffn(x1, hf, w1, w3, w2, tm=512, tf=512):
    n_tok = x1.shape[0]
    row = pl.BlockSpec((tm, D_MODEL), lambda i: (i, 0))
    return pl.pallas_call(
        functools.partial(_ffn_kernel, tf),
        grid=(n_tok // tm,),
        in_specs=[row, row, _resident(w1.shape), _resident(w3.shape), _resident(w2.shape)],
        out_specs=row,
        out_shape=jax.ShapeDtypeStruct((n_tok, D_MODEL), F32),
        compiler_params=_params("parallel"),
        name="ffn",
    )(x1, hf, w1, w3, w2)


def _gather_rows(table, idx, window=128):
    split, n_table, width = table.shape
    n = idx.shape[0]
    flat_idx = (idx[None, :] + n_table * jnp.arange(split, dtype=jnp.int32)[:, None]).reshape(-1)
    return _gather_subrows(table.reshape(split * n_table, width), flat_idx, window).reshape(split, n, width)


def _gather_subrows(table, idx, window):
    n = idx.shape[0]
    d = table.shape[1]
    mesh = plsc.VectorSubcoreMesh(core_axis_name="core", subcore_axis_name="subcore")

    @functools.partial(pl.kernel, out_type=jax.ShapeDtypeStruct((n, d), table.dtype), mesh=mesh,
                       name="gather_rows")
    def gather(table_hbm, idx_hbm, out_hbm):
        def body(idx_vmem, out_vmem):
            pltpu.sync_copy(table_hbm.at[idx_vmem.at[0]], out_vmem)

        pltpu.emit_pipeline(
            body,
            grid=(n // window,),
            in_specs=[pl.BlockSpec((1, window), index_map=lambda i: (0, i))],
            out_specs=[pl.BlockSpec((window, d), index_map=lambda i: (i, 0))],
            core_axis_name=("core", "subcore"),
            dimension_semantics=(pltpu.PARALLEL,),
        )(idx_hbm, out_hbm)

    return gather(table, idx.reshape(1, n))


def _scatter_rows(table, dest, n_out, window=128):
    split, n_table, width = table.shape
    n = dest.shape[0]
    flat_dest = (dest[None, :] + n_out * jnp.arange(split, dtype=jnp.int32)[:, None]).reshape(1, split * n)
    src_blocks, blocks = n_table // window, n // window
    mesh = plsc.VectorSubcoreMesh(core_axis_name="core", subcore_axis_name="subcore")

    @functools.partial(pl.kernel, out_type=jax.ShapeDtypeStruct((split * n_out, width), table.dtype), mesh=mesh,
                       name="scatter_rows")
    def scatter(table_hbm, dest_hbm, out_hbm):
        def body(rows_vmem, dest_vmem):
            pltpu.sync_copy(rows_vmem, out_hbm.at[dest_vmem.at[0]])

        pltpu.emit_pipeline(
            body,
            grid=(split * blocks,),
            in_specs=[pl.BlockSpec((window, width),
                                   index_map=lambda i: ((i // blocks) * src_blocks + (i % blocks) % src_blocks, 0)),
                      pl.BlockSpec((1, window), index_map=lambda i: (0, i))],
            out_specs=[],
            core_axis_name=("core", "subcore"),
            dimension_semantics=(pltpu.PARALLEL,),
        )(table_hbm, dest_hbm)

    return scatter(table.reshape(split * n_table, width), flat_dest).reshape(split, n_out, width)


def _expert_kernel(tf, ge_ref, nv_ref, x_ref, w1_ref, w3_ref, w2_ref, y_ref):
    g = pl.program_id(0)

    @pl.when(g < nv_ref[0])
    def _():
        x = _unpack_rows([x_ref[j] for j in range(ROW_SPLIT)]).astype(BF16)
        acc = jnp.zeros((MOE_ROWS, D_MODEL), F32)
        for lo in range(0, EXPERT_DIM, tf):
            a = jnp.dot(x, w1_ref[:, lo:lo + tf], preferred_element_type=F32)
            b = jnp.dot(x, w3_ref[:, lo:lo + tf], preferred_element_type=F32)
            acc = acc + jnp.dot((_silu(a) * b).astype(BF16), w2_ref[lo:lo + tf, :], preferred_element_type=F32)
        for j, words in enumerate(_pack_rows(acc)):
            y_ref[j] = words

    @pl.when(g >= nv_ref[0])
    def _():
        y_ref[...] = jnp.zeros_like(y_ref)


def _experts(xg, group_expert, n_valid, w1, w3, w2, tf=512):
    n_rows = xg.shape[1]
    rows = pl.BlockSpec((ROW_SPLIT, MOE_ROWS, SUBROW), lambda g, ge, nv: (0, g, 0))
    return pl.pallas_call(
        functools.partial(_expert_kernel, tf),
        grid_spec=pltpu.PrefetchScalarGridSpec(
            num_scalar_prefetch=2,
            grid=(n_rows // MOE_ROWS,),
            in_specs=[rows,
                      pl.BlockSpec((None, D_MODEL, EXPERT_DIM), lambda g, ge, nv: (ge[g], 0, 0)),
                      pl.BlockSpec((None, D_MODEL, EXPERT_DIM), lambda g, ge, nv: (ge[g], 0, 0)),
                      pl.BlockSpec((None, EXPERT_DIM, D_MODEL), lambda g, ge, nv: (ge[g], 0, 0))],
            out_specs=rows,
        ),
        out_shape=jax.ShapeDtypeStruct((ROW_SPLIT, n_rows, SUBROW), jnp.uint32),
        compiler_params=_params("arbitrary", vmem_limit=BIG_VMEM_LIMIT),
        name="experts",
    )(group_expert, n_valid, xg, w1, w3, w2)


def _combine_kernel(x1_ref, y0_ref, y1_ref, gate_ref, o_ref):
    gate = gate_ref[...]
    y0 = _unpack_rows([y0_ref[j] for j in range(ROW_SPLIT)])
    y1 = _unpack_rows([y1_ref[j] for j in range(ROW_SPLIT)])
    o_ref[...] = x1_ref[...] + gate[:, 0:1] * y0 + gate[:, 1:2] * y1


def _combine(x1, yg, gate_t, tm=512):
    n_tok = x1.shape[0]
    nb = n_tok // tm
    return pl.pallas_call(
        _combine_kernel,
        grid=(nb,),
        in_specs=[pl.BlockSpec((tm, D_MODEL), lambda i: (i, 0)),
                  pl.BlockSpec((ROW_SPLIT, tm, SUBROW), lambda i: (0, i, 0)),
                  pl.BlockSpec((ROW_SPLIT, tm, SUBROW), lambda i: (0, i + nb, 0)),
                  pl.BlockSpec((tm, 2), lambda i: (i, 0))],
        out_specs=pl.BlockSpec((tm, D_MODEL), lambda i: (i, 0)),
        out_shape=jax.ShapeDtypeStruct((n_tok, D_MODEL), F32),
        compiler_params=_params("parallel"),
        name="moe_combine",
    )(x1, yg, yg, gate_t)


def _moe(x1, routing, w1, w3, w2):
    n_tok = x1.shape[0]
    idx, gate, rank, cnt, hf = routing
    counts = cnt[:, 0].astype(jnp.int32)
    padded = (counts + MOE_ROWS - 1) // MOE_ROWS * MOE_ROWS
    end_padded = jnp.cumsum(padded)
    start_padded = end_padded - padded
    start = sum(jnp.where(idx == e, start_padded[e], 0) for e in range(N_EXPERTS))
    dest = (start + rank).reshape(-1)
    n_groups = (n_tok * 2 + MOE_ROWS - 1) // MOE_ROWS + N_EXPERTS
    n_rows = n_groups * MOE_ROWS
    group_row = jnp.arange(n_groups, dtype=jnp.int32)[:, None] * MOE_ROWS
    group_expert = jnp.minimum(jnp.sum(group_row >= end_padded[None, :], axis=1), N_EXPERTS - 1).astype(jnp.int32)
    n_valid = (end_padded[-1:] // MOE_ROWS).astype(jnp.int32)
    fill = jnp.arange(MOE_ROWS, dtype=jnp.int32)[None, :]
    pad_dest = jnp.where(fill < (padded - counts)[:, None], (start_padded + counts)[:, None] + fill,
                         n_rows - MOE_ROWS + fill).reshape(-1)
    xg = _scatter_rows(hf, jnp.concatenate([dest, pad_dest]), n_rows)
    y = _experts(xg, group_expert, n_valid, w1, w3, w2)
    return _combine(x1, _gather_rows(y, dest), gate.T)


def _row(v, pad=0):
    v = v.reshape(1, -1).astype(F32)
    return jnp.pad(v, ((0, 0), (0, pad))) if pad else v


def _rows_at(w, start, total):
    return jnp.pad(w.astype(F32), ((start, total - start - w.shape[0]), (0, 0)))


def kernel(x, w_in, norm_mix, rwkv_mu, rwkv_decay_up, rwkv_w0, rwkv_a_up, rwkv_a0, rwkv_gate_up, rwkv_k_k, rwkv_k_a, rwkv_r_k, rwkv_ln_w, rwkv_ln_b, vres_down, vres_up, vres_v0, ssm_conv_w, ssm_conv_b, ssm_dt_bias, ssm_a_log, ssm_d, ssm_norm_w, att_q_gain, att_k_gain, att_rel_bias, gla_gate_up, gla_gate_bias, gla_norm_w, w_branch, w_out, norm_ffn, ffn_w1, ffn_w3, ffn_w2, moe_router, moe_w1, moe_w3, moe_w2):
    bsz, seq, _ = x.shape
    n_tok = bsz * seq
    depth = w_in.shape[0]
    x2 = x.reshape(n_tok, D_MODEL)
    v_first = None
    for l in range(depth):
        vd = vres_down[l - 1] if l > 0 else jnp.zeros((D_MODEL, 32), F32)
        w_mix, wg = _pack_weights(w_in, l, vd)
        moe = l % 2 == 1
        up_w = [moe_w3[l // 2]] if moe else [ffn_w1[l // 2], ffn_w3[l // 2]]
        h, za, zb, zc, zd, wb_b, wo_b, *up_b = _inproj(x2, _row(norm_mix[l]), w_mix,
                                                       cast=[w_branch[l], w_out[l]] + up_w)
        shp = lambda z: z.reshape(bsz, seq, z.shape[-1])

        rw = dict(mu=_row(rwkv_mu[l], RWKV_PAD - RWKV_COLS),
                  dup=_rows_at(rwkv_decay_up[l], 0, LANES), w0=_row(rwkv_w0[l]),
                  aup=_rows_at(rwkv_a_up[l], 64, LANES), a0=_row(rwkv_a0[l]),
                  gup=_rows_at(rwkv_gate_up[l], 0, 256).astype(BF16),
                  kk=_row(rwkv_k_k[l]), ka=_row(rwkv_k_a[l]), rk=_row(rwkv_r_k[l]),
                  lnw=_row(rwkv_ln_w[l]), lnb=_row(rwkv_ln_b[l]))
        if l > 0:
            rw.update(vup=_rows_at(vres_up[l - 1], 160, 256), v0=_row(vres_v0[l - 1]))
        ssm = dict(cw=ssm_conv_w[l], cb=_row(ssm_conv_b[l]), dtb=_row(ssm_dt_bias[l], LANES - SSM_HEADS),
                   alog=_row(ssm_a_log[l], LANES - SSM_HEADS), dexp=_row(jnp.repeat(ssm_d[l], CHUNK)),
                   nw=_row(ssm_norm_w[l]))
        gla = dict(gup=_rows_at(gla_gate_up[l], 0, LANES), gb=_row(gla_gate_bias[l]), nw=_row(gla_norm_w[l]))
        (o_c,), (o_d,), (o_b,), res_a = _run_parts(
            [_attention_part(shp(zc), att_q_gain[l], att_k_gain[l], att_rel_bias, MIXER_BLOCK),
             _gla_part(shp(zd), gla, MIXER_BLOCK), _mamba_part(shp(zb), ssm, MIXER_BLOCK),
             _rwkv_part(shp(za), v_first, rw, MIXER_BLOCK)],
            grid=(bsz, seq // MIXER_BLOCK), name="mixers")
        o_a = res_a[0]
        if l == 0:
            v_first = res_a[1]
        outs = [o.reshape(n_tok, BRANCH_DIM) for o in (o_a, o_b, o_c, o_d)]
        merge_w = (wg, wb_b, wo_b, _row(norm_ffn[l]))
        if not moe:
            x1, hf, *w1_b = _merge(x2, h, outs, *merge_w, cast=(moe_w1[l // 2],) if l + 1 < depth else ())
            x2 = _ffn(x1, hf, *up_b, ffn_w2[l // 2].astype(BF16))
        else:
            x1, *routing, w2_b = _merge(x2, h, outs, *merge_w, wr_t=moe_router[l // 2].T, cast=(moe_w2[l // 2],))
            x2 = _moe(x1, routing, w1_b[0], up_b[0], w2_b)
    return x2.reshape(bsz, seq, D_MODEL)
```

```python
import functools
from typing import Any, NamedTuple

import jax
import jax.numpy as jnp
from jax import lax
from jax.experimental import pallas as pl
from jax.experimental.pallas import tpu as pltpu
from jax.experimental.pallas import tpu_sc as plsc

F32 = jnp.float32
BF16 = jnp.bfloat16

D_MODEL = 1024
CHUNK = 64
BRANCH_DIM = 512
NORM_EPS = 1e-6
LANES = 128
VMEM_LIMIT = 56 * 1024 * 1024
BIG_VMEM_LIMIT = 61 * 1024 * 1024

LOG2_E = 1.4426950408889634

RWKV_LN_EPS = 64e-5
RWKV_DECAY_SCALE = 0.6065306597126334
RWKV_COLS = 1824
RWKV_PAD = 1920
SSM_COLS = 1544
SSM_PAD = 1664
SSM_HEADS = 8
SSM_STATE = 128
ATT_COLS = 1536
ATT_HEADS = 8
ATT_LEFT = 8 * CHUNK
REL_CLIP = 2 * CHUNK
GLA_COLS = 1552
GLA_PAD = 1664
GLA_GATE_NORM = 16.0
FFN_DIM = 2816
N_EXPERTS = 8
EXPERT_DIM = 3584
MOE_ROWS = 512
ROW_SPLIT = 2
SUBROW = D_MODEL // (2 * ROW_SPLIT)


def _dot(a, b):
    return jnp.dot(a.astype(BF16), b.astype(BF16), preferred_element_type=F32)


def _dot_nt(a, b):
    return lax.dot_general(a.astype(BF16), b.astype(BF16), (((1,), (1,)), ((), ())),
                           preferred_element_type=F32)


def _dot_tn(a, b):
    return lax.dot_general(a.astype(BF16), b.astype(BF16), (((0,), (0,)), ((), ())),
                           preferred_element_type=F32)


def _hi_lo(a):
    hi = a.astype(BF16)
    return hi, (a - hi.astype(F32)).astype(BF16)


def _dot_3x(a, b):
    a_hi, a_lo = _hi_lo(a)
    b_hi, b_lo = _hi_lo(b)
    return (jnp.dot(a_hi, b_hi, preferred_element_type=F32) + jnp.dot(a_lo, b_hi, preferred_element_type=F32)
            + jnp.dot(a_hi, b_lo, preferred_element_type=F32))


def _dot_split(a, m):
    m = m.astype(BF16)
    hi, lo = _hi_lo(a)
    return jnp.dot(hi, m, preferred_element_type=F32) + jnp.dot(lo, m, preferred_element_type=F32)


def _dot_split_l(m, a):
    m = m.astype(BF16)
    hi, lo = _hi_lo(a)
    return jnp.dot(m, hi, preferred_element_type=F32) + jnp.dot(m, lo, preferred_element_type=F32)


def _pack_rows(x):
    bits = pltpu.bitcast(x.astype(BF16).astype(F32), jnp.uint32)
    half = D_MODEL // 2
    return [(bits[:, p * SUBROW:(p + 1) * SUBROW] & jnp.uint32(0xFFFF0000))
            | (bits[:, half + p * SUBROW:half + (p + 1) * SUBROW] >> jnp.uint32(16)) for p in range(ROW_SPLIT)]


def _unpack_rows(planes):
    hi = [pltpu.bitcast(w & jnp.uint32(0xFFFF0000), F32) for w in planes]
    lo = [pltpu.bitcast(w << jnp.uint32(16), F32) for w in planes]
    return jnp.concatenate(hi + lo, axis=-1)


def _softplus(x):
    return jnp.maximum(x, 0.0) + jnp.log(1.0 + jnp.exp(-jnp.abs(x)))


def _sigmoid(x):
    return 1.0 / (1.0 + jnp.exp(-x))


def _silu(x):
    return x * _sigmoid(x)


def _iota2(shape, axis):
    return lax.broadcasted_iota(jnp.int32, shape, axis)


def _group_mean_matrix(n, group):
    r = _iota2((n, n), 0) // group
    c = _iota2((n, n), 1) // group
    return jnp.where(r == c, 1.0 / group, 0.0).astype(F32)


def _chunk_tri(n):
    r = _iota2((n, n), 0)
    c = _iota2((n, n), 1)
    return jnp.where((r // CHUNK == c // CHUNK) & (r >= c), 1.0, 0.0).astype(F32)


def _resident(shape):
    nd = len(shape)
    return pl.BlockSpec(shape, lambda *_: (0,) * nd, pipeline_mode=pl.Buffered(1))


def _params(*sem, vmem_limit=VMEM_LIMIT):
    return pltpu.CompilerParams(dimension_semantics=sem, vmem_limit_bytes=vmem_limit)


_DONE = object()


class _Part(NamedTuple):
    body: Any
    inputs: list
    in_specs: list
    out_shapes: list
    out_specs: list
    scratch_shapes: list


def _run_parts(parts, grid, name):
    def split(refs, counts):
        out, lo = [], 0
        for n in counts:
            out.append(refs[lo:lo + n])
            lo += n
        return out

    n_in = [len(p.inputs) for p in parts]
    n_out = [len(p.out_shapes) for p in parts]
    n_scr = [len(p.scratch_shapes) for p in parts]

    def kernel(*refs):
        ins = split(refs[:sum(n_in)], n_in)
        outs = split(refs[sum(n_in):sum(n_in) + sum(n_out)], n_out)
        scr = split(refs[sum(n_in) + sum(n_out):], n_scr)
        stages = [part.body(*i, *o, *c) for part, i, o, c in zip(parts, ins, outs, scr)]
        while stages:
            for stage in list(stages):
                if next(stage, _DONE) is _DONE:
                    stages.remove(stage)

    flat = lambda field: [x for p in parts for x in getattr(p, field)]
    res = pl.pallas_call(
        kernel,
        grid=grid,
        in_specs=flat("in_specs"),
        out_specs=flat("out_specs"),
        out_shape=flat("out_shapes"),
        scratch_shapes=flat("scratch_shapes"),
        compiler_params=_params("parallel", "arbitrary", vmem_limit=BIG_VMEM_LIMIT),
        name=name,
    )(*flat("inputs"))
    return split(list(res), n_out)


def _cast_rider(w, steps):
    cols = w.shape[-1]
    sliced = (steps, w.size // cols // steps, cols)
    assert w.size // cols % steps == 0 and sliced[1] % 16 == 0
    return w.reshape(sliced), pl.BlockSpec((None,) + sliced[1:], lambda i: (i, 0, 0)), jax.ShapeDtypeStruct(sliced, BF16)


def _cast_slices(src_refs, dst_refs):
    for src, dst in zip(src_refs, dst_refs):
        dst[...] = src[...].astype(dst.dtype)


MIXER_COLS = (RWKV_PAD, SSM_PAD, ATT_COLS, GLA_PAD)
MIXER_BLOCK = 256


PACK_ROWS = 128


def _pack_kernel(src_ref, keep_ref, fill_ref, w_ref, patch_ref, o_ref):
    del src_ref, fill_ref
    rows = _iota2(o_ref.shape, 0)
    o_ref[...] = jnp.where(rows < keep_ref[pl.program_id(0)], w_ref[...], patch_ref[...]).astype(o_ref.dtype)


def _pack_weight_rows(w_t, plan, patches):
    src, keep, fill = (jnp.asarray(v, jnp.int32) for v in zip(*plan))
    cols = w_t.shape[1]
    return pl.pallas_call(
        _pack_kernel,
        grid_spec=pltpu.PrefetchScalarGridSpec(
            num_scalar_prefetch=3,
            grid=(len(plan),),
            in_specs=[pl.BlockSpec((pl.Element(PACK_ROWS), pl.Element(cols)), lambda j, src, keep, fill: (pl.multiple_of(src[j], 8), 0)),
                      pl.BlockSpec((None, PACK_ROWS, cols), lambda j, src, keep, fill: (fill[j], 0, 0))],
            out_specs=pl.BlockSpec((PACK_ROWS, cols), lambda j, src, keep, fill: (j, 0)),
        ),
        out_shape=jax.ShapeDtypeStruct((len(plan) * PACK_ROWS, cols), BF16),
        compiler_params=_params("parallel"),
        name="pack_weights",
    )(src, keep, fill, w_t, patches)


def _mixer_plan(base):
    o_ss, o_at = RWKV_COLS + SSM_COLS, RWKV_COLS + SSM_COLS + ATT_COLS
    plan = []

    def rows(lo, hi, fill=0):
        for r in range(lo, hi, PACK_ROWS):
            plan.append((base + r, min(PACK_ROWS, hi - r), fill))

    rows(0, RWKV_COLS, fill=1)
    rows(RWKV_COLS, o_ss)
    rows(o_ss, o_at)
    rows(o_at, o_at + 1024)
    rows(o_at + 1040, o_at + GLA_COLS)
    rows(o_at + 1024, o_at + 1040)
    assert len(plan) * PACK_ROWS == sum(MIXER_COLS)
    return plan


def _inproj_kernel(n_cast, x_ref, g_ref, w_ref, *refs):
    cast_in, (h_ref, *z_refs), cast_out = refs[:n_cast], refs[n_cast:len(refs) - n_cast], refs[len(refs) - n_cast:]
    _cast_slices(cast_in, cast_out)
    x = x_ref[...]
    ms = jnp.mean(x * x, axis=-1, keepdims=True)
    h = (x * lax.rsqrt(ms + NORM_EPS) * g_ref[...]).astype(BF16)
    h_ref[...] = h
    lo = 0
    for pair in (z_refs[:2], z_refs[2:]):
        widths = [z_ref.shape[-1] for z_ref in pair]
        z = _dot_nt(h, w_ref[lo:lo + sum(widths), :])
        pair[0][...] = z[:, :widths[0]]
        pair[1][...] = z[:, widths[0]:]
        lo += sum(widths)


def _inproj(x2, g, w_mix, cast=(), tm=512):
    n_tok = x2.shape[0]
    row = lambda n: pl.BlockSpec((tm, n), lambda i: (i, 0))
    riders = [_cast_rider(w, n_tok // tm) for w in cast]
    res = pl.pallas_call(
        functools.partial(_inproj_kernel, len(cast)),
        grid=(n_tok // tm,),
        in_specs=[row(D_MODEL), _resident((1, D_MODEL)), _resident(w_mix.shape)] + [r[1] for r in riders],
        out_specs=[row(D_MODEL)] + [row(n) for n in MIXER_COLS] + [r[1] for r in riders],
        out_shape=[jax.ShapeDtypeStruct((n_tok, D_MODEL), BF16)]
        + [jax.ShapeDtypeStruct((n_tok, n), F32) for n in MIXER_COLS] + [r[2] for r in riders],
        compiler_params=_params("parallel", vmem_limit=BIG_VMEM_LIMIT if cast else VMEM_LIMIT),
        name="inproj",
    )(x2, g, w_mix, *[r[0] for r in riders])
    n_plain = len(res) - len(cast)
    return list(res[:n_plain]) + [wb16.reshape(w.shape) for wb16, w in zip(res[n_plain:], cast)]


def _rwkv_block(r, k, v, kk, ka, ld, tb, state_ref, y_ref):
    n_chunks = tb // CHUNK
    cum = _dot_split_l(_chunk_tri(tb), ld)
    p_inv = jnp.exp(-cum)
    a_t = -kk * jnp.exp(cum - ld)
    b_t = kk * ka * p_inv
    k_t = k * p_inv
    r_t = r * jnp.exp(cum)
    bf = lambda z: z.astype(BF16)
    a_b, b_b, k_b, r_b, v_b = bf(a_t), bf(b_t), bf(k_t), bf(r_t), bf(v)
    yield

    lane_head = _iota2((CHUNK, LANES), 1) // CHUNK
    row = _iota2((CHUNK, LANES), 0)
    col = _iota2((CHUNK, LANES), 1) % CHUNK
    strict = row > col
    incl = row >= col
    bd_mask = (_iota2((LANES, LANES), 0) // CHUNK) == (_iota2((LANES, LANES), 1) // CHUNK)
    eye = jnp.where(_iota2((LANES, LANES), 0) == _iota2((LANES, LANES), 1), 1.0, 0.0).astype(F32)

    def stack(z):
        return jnp.concatenate([jnp.where(lane_head == 0, z, 0.0), jnp.where(lane_head == 1, z, 0.0)], axis=0)

    def sub(z, u):
        c, p = u
        return z[c * CHUNK:(c + 1) * CHUNK, p * LANES:(p + 1) * LANES]

    def prepare(units):
        gram = {u: _dot_nt(jnp.concatenate([sub(a_b, u), sub(r_b, u)], axis=0),
                           jnp.concatenate([stack(sub(b_b, u)), stack(sub(k_b, u))], axis=0))
                for u in units}
        yield
        a_rb = {u: bf(jnp.where(incl, gram[u][CHUNK:, :LANES], 0.0)) for u in units}
        a_akrk = {u: bf(jnp.concatenate([jnp.where(strict, gram[u][:CHUNK, LANES:], 0.0),
                                         jnp.where(incl, gram[u][CHUNK:, LANES:], 0.0)], axis=0)) for u in units}
        pw = {u: stack(jnp.where(strict, gram[u][:CHUNK, :LANES], 0.0)) for u in units}
        t_bd = {u: eye + pw[u] for u in units}
        pw = {u: _dot(pw[u], pw[u]) for u in units}
        yield
        for _ in range(CHUNK.bit_length() - 3):
            sp = {u: _dot(jnp.concatenate([t_bd[u], pw[u]], axis=0), pw[u]) for u in units}
            t_bd = {u: t_bd[u] + sp[u][:LANES] for u in units}
            pw = {u: sp[u][LANES:] for u in units}
            yield
        t_bd = {u: bf(t_bd[u] + _dot(t_bd[u], pw[u])) for u in units}
        v_bd = {u: stack(sub(v_b, u)) for u in units}
        avyv = {u: _dot(a_akrk[u], v_bd[u]) for u in units}
        yield
        wuv = {u: _dot(t_bd[u], jnp.concatenate([stack(sub(a_b, u)), stack(bf(avyv[u][:CHUNK]))], axis=1))
               for u in units}
        wr = {u: jnp.concatenate([bf(wuv[u][:, :LANES]), sub(r_b, u)], axis=0) for u in units}
        yield
        return a_rb, avyv, wuv, wr

    a_rb, avyv, wuv, wr = yield from prepare([(c, p) for c in range(n_chunks) for p in range(4)])

    h = [state_ref[p] for p in range(4)]
    for c in range(n_chunks):
        last = c * CHUNK + CHUNK - 1
        p_last = jnp.exp(cum[last:last + 1, :])
        hs = [_dot_nt(wr[(c, p)], h[p]) for p in range(4)]
        u2 = [hs[p][:LANES] + wuv[(c, p)][:, LANES:] for p in range(4)]
        u_p = [u2[p][:CHUNK] + u2[p][CHUNK:] for p in range(4)]
        upd = [_dot_tn(jnp.concatenate([u_p[p], sub(v, (c, p))], axis=0),
                       jnp.concatenate([sub(b_t, (c, p)), sub(k_t, (c, p))], axis=0)
                       * p_last[:, p * LANES:(p + 1) * LANES]) for p in range(4)]
        for p in range(4):
            y_ref[c * CHUNK:(c + 1) * CHUNK, p * LANES:(p + 1) * LANES] = (
                hs[p][LANES:] + _dot(a_rb[(c, p)], stack(u_p[p])) + avyv[(c, p)][CHUNK:])
        h = [h[p] * p_last[:, p * LANES:(p + 1) * LANES] + jnp.where(bd_mask, upd[p], 0.0) for p in range(4)]
        yield
    for p in range(4):
        state_ref[p] = h[p]


def _rwkv_kernel(has_vres, tb, *refs):
    if has_vres:
        (za_ref, vfirst_ref, mu_ref, dup_ref, w0_ref, aup_ref, a0_ref, gup_ref, vup_ref, v0_ref,
         kk_ref, ka_ref, rk_ref, lnw_ref, lnb_ref, o_ref, prev_ref, state_ref, y_ref) = refs
    else:
        (za_ref, mu_ref, dup_ref, w0_ref, aup_ref, a0_ref, gup_ref,
         kk_ref, ka_ref, rk_ref, lnw_ref, lnb_ref, o_ref, vraw_ref, prev_ref, state_ref, y_ref) = refs

    @pl.when(pl.program_id(1) == 0)
    def _():
        prev_ref[...] = jnp.zeros_like(prev_ref)
        state_ref[...] = jnp.zeros_like(state_ref)

    yield
    za = za_ref[...]
    shifted = pltpu.roll(za, 1, axis=0)
    shifted = jnp.where(_iota2(za.shape, 0) == 0, prev_ref[7:8, :], shifted)
    prev_ref[...] = za[tb - 8:, :]
    za = za + (shifted - za) * mu_ref[...]

    r = za[:, 0:512]
    k = za[:, 512:1024]
    v = za[:, 1024:1536]
    xwa = za[:, 1536:1664]
    xg = za[:, 1664:1920]
    ld = -RWKV_DECAY_SCALE * _sigmoid(w0_ref[...] + _dot_3x(jnp.tanh(xwa), dup_ref[...]))
    a = _sigmoid(a0_ref[...] + _dot(xwa, aup_ref[...]))
    g = _dot(_sigmoid(xg), gup_ref[...])
    if has_vres:
        v_mix = _sigmoid(v0_ref[...] + _dot(xg, vup_ref[...]))
        v = v + (vfirst_ref[...] - v) * v_mix
    else:
        vraw_ref[...] = v
    kk = k * kk_ref[...]
    k = k * (1.0 + (a - 1.0) * ka_ref[...])
    head_sum = _group_mean_matrix(512, CHUNK) * float(CHUNK)
    kk = kk / jnp.maximum(jnp.sqrt(_dot(kk * kk, head_sum)), 1e-12)
    yield

    yield from _rwkv_block(r, k, v, kk, a, ld, tb, state_ref, y_ref)
    y = y_ref[...]
    head_mean = _group_mean_matrix(512, CHUNK)
    mean = _dot_split(y, head_mean)
    yc = y - mean
    var = _dot(yc * yc, head_mean)
    y = yc * lax.rsqrt(var + RWKV_LN_EPS) * lnw_ref[...] + lnb_ref[...]
    y = y + _dot(r * k * rk_ref[...], head_sum) * v
    o_ref[...] = (y * g).astype(o_ref.dtype)


def _rwkv_part(za, v_first, prm, tb):
    bsz, seq, _ = za.shape
    has_vres = v_first is not None
    blk = lambda n: pl.BlockSpec((None, tb, n), lambda b, i: (b, i, 0))
    names = (["mu", "dup", "w0", "aup", "a0", "gup"] + (["vup", "v0"] if has_vres else [])
             + ["kk", "ka", "rk", "lnw", "lnb"])
    weights = [prm[n] for n in names]
    ins = [za] + ([v_first] if has_vres else []) + weights
    in_specs = [blk(RWKV_PAD)] + ([blk(512)] if has_vres else []) + [_resident(w.shape) for w in weights]
    out_shape = [jax.ShapeDtypeStruct((bsz, seq, 512), BF16)]
    out_specs = [blk(512)]
    if not has_vres:
        out_shape.append(jax.ShapeDtypeStruct((bsz, seq, 512), F32))
        out_specs.append(blk(512))
    return _Part(functools.partial(_rwkv_kernel, has_vres, tb), ins, in_specs, out_shape, out_specs,
                 [pltpu.VMEM((8, RWKV_PAD), F32), pltpu.VMEM((4, LANES, LANES), F32), pltpu.VMEM((tb, 512), F32)])


def _mamba_kernel(tb, zb_ref, cw_ref, cb_ref, dtb_ref, alog_ref, dexp_ref, nw_ref,
                  o_ref, xbuf_ref, state_ref, y_ref):
    @pl.when(pl.program_id(1) == 0)
    def _():
        xbuf_ref[0:8, :] = jnp.zeros((8, 1024), F32)
        state_ref[...] = jnp.zeros_like(state_ref)

    yield
    xbuf_ref[8:8 + tb, :] = zb_ref[:, 512:1536]
    conv = cb_ref[...]
    for i in range(4):
        conv = conv + cw_ref[i:i + 1, :] * xbuf_ref[5 + i:5 + i + tb, :]
    xbuf_ref[0:8, :] = xbuf_ref[tb:tb + 8, :]
    xbc = _silu(conv)
    xs = xbc[:, 0:512]
    bm = xbc[:, 512:768]
    cm = xbc[:, 768:1024]
    yield

    lane = _iota2((1, LANES), 1)
    dt = _softplus(zb_ref[:, 1536:1664] + dtb_ref[...])
    a_neg = jnp.where(lane < SSM_HEADS, -jnp.exp(alog_ref[...]), 0.0)
    acs = _dot_split_l(_chunk_tri(tb), dt * a_neg)
    expand = jnp.where(_iota2((LANES, 512), 0) == _iota2((LANES, 512), 1) // CHUNK, 1.0, 0.0)
    dt_e = _dot(dt, expand)
    acs_e = _dot_split(acs, expand)
    xdt = xs * dt_e
    yield

    n_chunks = tb // CHUNK
    causal = _iota2((CHUNK, 512), 0) >= _iota2((CHUNK, 512), 1) % CHUNK
    pair_head = _iota2((CHUNK, LANES), 1) // CHUNK
    spread = jnp.where(_iota2((CHUNK, 512), 0) == _iota2((CHUNK, 512), 1) % CHUNK, 1.0, 0.0)
    own_head = _iota2((SSM_HEADS, 512), 0) == _iota2((SSM_HEADS, 512), 1) // CHUNK
    rows_of = lambda c: slice(c * CHUNK, (c + 1) * CHUNK)
    grp = lambda g: slice(g * SSM_STATE, (g + 1) * SSM_STATE)

    def stack(z):
        return jnp.concatenate([jnp.where(pair_head == 0, z, 0.0), jnp.where(pair_head == 1, z, 0.0)], axis=0)

    y_intra, upd, e_in, e_last = [], [], [], []
    for c in range(n_chunks):
        rows = rows_of(c)
        acs_ec = acs_e[rows]
        acs_t = acs[rows].T[0:SSM_HEADS]
        acs_row = jnp.sum(jnp.where(own_head, _dot_split(acs_t, spread), 0.0), axis=0, keepdims=True)
        decay = jnp.exp(jnp.where(causal, acs_ec - acs_row, -jnp.inf))
        cb = jnp.concatenate(
            [_dot_nt(cm[rows, grp(g)], jnp.concatenate([bm[rows, grp(g)]] * 4, axis=0)) for g in range(2)], axis=1)
        m = cb * decay
        yield
        xdt_c = xdt[rows]
        y_intra.append([_dot(m[:, p * LANES:(p + 1) * LANES], stack(xdt_c[:, p * LANES:(p + 1) * LANES]))
                        for p in range(4)])
        last_e = acs_ec[CHUNK - 1:CHUNK, :]
        x_out = xdt_c * jnp.exp(last_e - acs_ec)
        upd.append([_dot_tn(bm[rows, grp(g)], x_out[:, g * 256:(g + 1) * 256]) for g in range(2)])
        e_in.append(jnp.exp(acs_ec))
        e_last.append(jnp.exp(last_e))
        yield

    state = [state_ref[g] for g in range(2)]
    for c in range(n_chunks):
        rows = rows_of(c)
        for g in range(2):
            ls = slice(g * 256, (g + 1) * 256)
            y_in = jnp.concatenate(y_intra[c][2 * g:2 * g + 2], axis=1)
            y_ref[rows, ls] = y_in + _dot(cm[rows, grp(g)], state[g]) * e_in[c][:, ls]
            state[g] = state[g] * e_last[c][:, ls] + upd[c][g]
        yield
    for g in range(2):
        state_ref[g] = state[g]

    y = (y_ref[...] + xs * dexp_ref[...]) * _silu(zb_ref[:, 0:512])
    for g in range(2):
        ls = slice(g * 256, (g + 1) * 256)
        yg = y[:, ls]
        ms = jnp.mean(yg * yg, axis=-1, keepdims=True)
        o_ref[:, ls] = (yg * lax.rsqrt(ms + NORM_EPS) * nw_ref[:, ls]).astype(o_ref.dtype)


def _mamba_part(zb, prm, tb):
    bsz, seq, _ = zb.shape
    blk = lambda n: pl.BlockSpec((None, tb, n), lambda b, i: (b, i, 0))
    weights = [prm[n] for n in ("cw", "cb", "dtb", "alog", "dexp", "nw")]
    return _Part(functools.partial(_mamba_kernel, tb), [zb] + weights,
                 [blk(SSM_PAD)] + [_resident(w.shape) for w in weights],
                 [jax.ShapeDtypeStruct((bsz, seq, 512), BF16)], [blk(512)],
                 [pltpu.VMEM((tb + 8, 1024), F32), pltpu.VMEM((2, SSM_STATE, 256), F32), pltpu.VMEM((tb, 512), F32)])


def _attn_kernel(tq, qkv_ref, qg_ref, kg_ref, bias_ref, o_ref, kn_ref, vb_ref):
    i = pl.program_id(1)
    head_mean = _group_mean_matrix(512, CHUNK)
    win = tq + ATT_LEFT

    @pl.when(i == 0)
    def _():
        kn_ref[0:ATT_LEFT, :] = jnp.zeros((ATT_LEFT, 512), BF16)
        vb_ref[0:ATT_LEFT, :] = jnp.zeros((ATT_LEFT, 512), BF16)

    yield
    start = pl.multiple_of(i * tq, tq)
    k = qkv_ref[:, 512:1024]
    ms = _dot(k * k, head_mean)
    kn_ref[pl.ds(ATT_LEFT + start, tq), :] = (k * lax.rsqrt(ms + NORM_EPS) * kg_ref[...]).astype(BF16)
    vb_ref[pl.ds(ATT_LEFT + start, tq), :] = qkv_ref[:, 1024:1536].astype(BF16)
    q = qkv_ref[:, 0:512]
    ms = _dot(q * q, head_mean)
    qn = q * lax.rsqrt(ms + NORM_EPS) * qg_ref[...] * (CHUNK ** -0.5 * LOG2_E)
    kwin = kn_ref[pl.ds(start, win), :]
    vwin = vb_ref[pl.ds(start, win), :]
    lane_head = _iota2((tq, LANES), 1) // CHUNK
    yield
    for p in range(4):
        ls = slice(p * LANES, (p + 1) * LANES)
        q2 = jnp.concatenate([jnp.where(lane_head == s, qn[:, ls], 0.0) for s in range(2)], axis=0)
        sc = _dot_nt(q2, kwin[:, ls]) + bias_ref[2 * p:2 * p + 2].reshape(2 * tq, win).astype(F32)
        yield
        e = jnp.exp2(sc - jnp.max(sc, axis=-1, keepdims=True))
        yield
        o2 = _dot(e, vwin[:, ls]) / jnp.sum(e, axis=-1, keepdims=True)
        o_ref[:, ls] = jnp.where(lane_head == 0, o2[:tq], o2[tq:]).astype(o_ref.dtype)
        yield


def _band_bias(rel_bias, tq, win):
    period = tq + win
    m = jnp.arange(period)
    d = jnp.where(m < win, m, m - period) - ATT_LEFT
    f = (rel_bias.astype(F32)[:, jnp.clip(d, -REL_CLIP, REL_CLIP) + REL_CLIP] * LOG2_E).astype(BF16)
    g = jnp.tile(f, (1, tq))[:, :tq * (period - 1)].reshape(-1, tq, period - 1)
    bias = g[:, :, :win]
    left = ATT_LEFT // CHUNK
    qc = left + jnp.arange(tq)[:, None] // CHUNK
    kc = jnp.arange(win)[None, :] // CHUNK
    tiles = []
    for blk in range(ATT_LEFT // tq + 1):
        first_kc = jnp.maximum(qc - left, left - blk * (tq // CHUNK))
        tiles.append(jnp.where((kc <= qc) & (kc >= first_kc), bias, -jnp.inf))
    return jnp.stack(tiles)


def _attention_part(zc, q_gain, k_gain, rel_bias, tq):
    bsz, seq, _ = zc.shape
    win = tq + ATT_LEFT
    bias = _band_bias(rel_bias, tq, win)
    qg = jnp.tile(q_gain, ATT_HEADS)[None, :]
    kg = jnp.tile(k_gain, ATT_HEADS)[None, :]
    last_tile = bias.shape[0] - 1
    bias_spec = pl.BlockSpec((None,) + bias.shape[1:], lambda b, i: (jnp.minimum(i, last_tile), 0, 0, 0))
    blk = lambda n: pl.BlockSpec((None, tq, n), lambda b, i: (b, i, 0))
    return _Part(functools.partial(_attn_kernel, tq), [zc, qg, kg, bias],
                 [blk(ATT_COLS), _resident(qg.shape), _resident(kg.shape), bias_spec],
                 [jax.ShapeDtypeStruct((bsz, seq, 512), BF16)], [blk(512)],
                 [pltpu.VMEM((ATT_LEFT + seq, 512), BF16), pltpu.VMEM((ATT_LEFT + seq, 512), BF16)])


def _gla_kernel(tb, zd_ref, gup_ref, gb_ref, nw_ref, o_ref, state_ref, y_ref):
    @pl.when(pl.program_id(1) == 0)
    def _():
        state_ref[...] = jnp.zeros_like(state_ref)

    yield
    q = zd_ref[:, 0:256] * (CHUNK ** -0.5)
    k = zd_ref[:, 256:512]
    v = zd_ref[:, 512:1024]
    log_a = -_softplus(-(_dot_3x(zd_ref[:, 1536:1664], gup_ref[...]) + gb_ref[...])) / GLA_GATE_NORM
    bcum = _dot_split_l(_chunk_tri(tb), log_a)

    k_head = _iota2((CHUNK, 256), 1) // CHUNK
    v_head = _iota2((CHUNK, 512), 1) // LANES
    causal = _iota2((CHUNK, 256), 0) >= _iota2((CHUNK, 256), 1) % CHUNK
    bd = (_iota2((512, 256), 0) // LANES) == (_iota2((512, 256), 1) // CHUNK)
    n_chunks = tb // CHUNK
    qg_all = (q * jnp.exp(bcum)).astype(BF16)
    kg_all = k * jnp.exp(-bcum)
    yield
    qg, att, o_intra, upd, e_last = [], [], [], [], []
    for c in range(n_chunks):
        rows = slice(c * CHUNK, (c + 1) * CHUNK)
        kg_bd = jnp.concatenate([jnp.where(k_head == h, kg_all[rows], 0.0) for h in range(4)], axis=0)
        qg.append(qg_all[rows])
        att.append(jnp.where(causal, _dot_nt(qg[c], kg_bd), 0.0))
        yield
    for c in range(n_chunks):
        rows = slice(c * CHUNK, (c + 1) * CHUNK)
        bc = bcum[rows]
        blast = bc[CHUNK - 1:CHUNK, :]
        v_c = v[rows]
        v_bd = jnp.concatenate([jnp.where(v_head == h, v_c, 0.0) for h in range(4)], axis=0)
        o_intra.append(_dot(att[c], v_bd))
        upd.append(jnp.where(bd, _dot_tn(v_c, k[rows] * jnp.exp(blast - bc)), 0.0))
        e_last.append(jnp.exp(blast))
        yield
    st = state_ref[...]
    for c in range(n_chunks):
        y_ref[c * CHUNK:(c + 1) * CHUNK, :] = o_intra[c] + _dot_nt(qg[c], st)
        st = st * e_last[c] + upd[c]
        yield
    state_ref[...] = st

    o = y_ref[...]
    for h in range(4):
        ls = slice(h * LANES, (h + 1) * LANES)
        oh = o[:, ls]
        ms = jnp.mean(oh * oh, axis=-1, keepdims=True)
        o_ref[:, ls] = (oh * lax.rsqrt(ms + NORM_EPS) * nw_ref[...]
                        * _silu(zd_ref[:, 1024 + h * LANES:1024 + (h + 1) * LANES])).astype(o_ref.dtype)


def _gla_part(zd, prm, tb):
    bsz, seq, _ = zd.shape
    blk = lambda n: pl.BlockSpec((None, tb, n), lambda b, i: (b, i, 0))
    weights = [prm[n] for n in ("gup", "gb", "nw")]
    return _Part(functools.partial(_gla_kernel, tb), [zd] + weights,
                 [blk(GLA_PAD)] + [_resident(w.shape) for w in weights],
                 [jax.ShapeDtypeStruct((bsz, seq, 512), BF16)], [blk(512)],
                 [pltpu.VMEM((512, 256), F32), pltpu.VMEM((tb, 512), F32)])


def _route(tm, hf, wr_ref, idx_ref, gate_ref, rank_ref, cnt_ref, hf_ref, carry_ref):
    @pl.when(pl.program_id(0) == 0)
    def _():
        carry_ref[...] = jnp.zeros_like(carry_ref)

    for j, words in enumerate(_pack_rows(hf)):
        hf_ref[j] = words
    nt = lambda a, b: lax.dot_general(a, b, (((1,), (1,)), ((), ())), preferred_element_type=F32)
    w_hi, w_lo = _hi_lo(wr_ref[...])
    h_hi, h_lo = _hi_lo(hf)
    logits = nt(w_hi, h_hi) + nt(w_lo, h_hi) + nt(w_hi, h_lo)
    e_iota = _iota2((N_EXPERTS, tm), 0)
    m1 = jnp.max(logits, axis=0, keepdims=True)
    i1 = jnp.min(jnp.where(logits == m1, e_iota, N_EXPERTS), axis=0, keepdims=True)
    rest = jnp.where(e_iota == i1, -jnp.inf, logits)
    m2 = jnp.max(rest, axis=0, keepdims=True)
    i2 = jnp.min(jnp.where(rest == m2, e_iota, N_EXPERTS), axis=0, keepdims=True)
    e2 = jnp.exp(m2 - m1)
    gate_ref[0:1, :] = 1.0 / (1.0 + e2)
    gate_ref[1:2, :] = e2 / (1.0 + e2)
    idx_ref[0:1, :] = i1
    idx_ref[1:2, :] = i2
    hit1 = jnp.where(e_iota == i1, 1.0, 0.0)
    hit2 = jnp.where(e_iota == i2, 1.0, 0.0)
    before = jnp.where(_iota2((tm, tm), 0) < _iota2((tm, tm), 1), 1.0, 0.0)
    prior = _dot(hit1 + hit2, before) + carry_ref[:, 0:1]
    rank_ref[0:1, :] = jnp.sum(hit1 * prior, axis=0, keepdims=True).astype(jnp.int32)
    rank_ref[1:2, :] = jnp.sum(hit2 * prior, axis=0, keepdims=True).astype(jnp.int32)
    carry_ref[...] = carry_ref[...] + jnp.sum(hit1 + hit2, axis=1, keepdims=True)
    cnt_ref[...] = carry_ref[...]


def _merge_kernel(tm, route, n_cast, *refs):
    n_in, n_out = (11, 6) if route else (10, 2)
    ins, cast_in = refs[:n_in], refs[n_in:n_in + n_cast]
    outs = refs[n_in + n_cast:n_in + n_cast + n_out]
    cast_out = refs[n_in + n_cast + n_out:n_in + 2 * n_cast + n_out]
    scratch = refs[n_in + 2 * n_cast + n_out:]
    x_ref, h_ref, oa_ref, ob_ref, oc_ref, od_ref, wg_ref, wb_ref, wo_ref, nf_ref = ins[:10]
    _cast_slices(cast_in, cast_out)
    h = h_ref[...]
    acc = jnp.zeros(x_ref.shape, F32)
    for i, o_ref in enumerate((oa_ref, ob_ref, oc_ref, od_ref)):
        gate = _sigmoid(_dot_nt(h, wg_ref[i * D_MODEL:(i + 1) * D_MODEL, :]))
        acc = acc + gate * jnp.dot(o_ref[...], wb_ref[i], preferred_element_type=F32)
    x1 = x_ref[...] + jnp.dot(acc.astype(BF16), wo_ref[...], preferred_element_type=F32)
    ms = jnp.mean(x1 * x1, axis=-1, keepdims=True)
    hf = x1 * lax.rsqrt(ms + NORM_EPS) * nf_ref[...]
    if route:
        _route(tm, hf, ins[10], *outs[1:], *scratch)
    else:
        outs[1][...] = hf.astype(BF16)
    outs[0][...] = x1


def _merge(x2, h, outs, wg, wb, wo, nf, wr_t=None, cast=(), tm=512):
    n_tok = x2.shape[0]
    route = wr_t is not None
    steps = n_tok // tm
    row = lambda n: pl.BlockSpec((tm, n), lambda i: (i, 0))
    ins = [x2, h, *outs, wg, wb, wo, nf]
    in_specs = ([row(D_MODEL), row(D_MODEL)] + [row(BRANCH_DIM)] * 4
                + [_resident(wg.shape), _resident(wb.shape), _resident(wo.shape), _resident(nf.shape)])
    out_specs = [row(D_MODEL)]
    out_shape = [jax.ShapeDtypeStruct((n_tok, D_MODEL), F32)]
    scratch = []
    if route:
        ins.append(wr_t)
        in_specs.append(_resident(wr_t.shape))
        col = pl.BlockSpec((2, tm), lambda i: (0, i))
        out_specs += [col, col, col, pl.BlockSpec((N_EXPERTS, LANES), lambda i: (0, 0)),
                      pl.BlockSpec((ROW_SPLIT, tm, SUBROW), lambda i: (0, i, 0))]
        out_shape += [jax.ShapeDtypeStruct((2, n_tok), jnp.int32), jax.ShapeDtypeStruct((2, n_tok), F32),
                      jax.ShapeDtypeStruct((2, n_tok), jnp.int32), jax.ShapeDtypeStruct((N_EXPERTS, LANES), F32),
                      jax.ShapeDtypeStruct((ROW_SPLIT, n_tok, SUBROW), jnp.uint32)]
        scratch = [pltpu.VMEM((N_EXPERTS, LANES), F32)]
    else:
        out_specs.append(row(D_MODEL))
        out_shape.append(jax.ShapeDtypeStruct((n_tok, D_MODEL), BF16))
    riders = [_cast_rider(w, steps) for w in cast]
    res = pl.pallas_call(
        functools.partial(_merge_kernel, tm, route, len(cast)),
        grid=(steps,),
        in_specs=in_specs + [r[1] for r in riders],
        out_specs=out_specs + [r[1] for r in riders],
        out_shape=out_shape + [r[2] for r in riders],
        scratch_shapes=scratch,
        compiler_params=_params("arbitrary" if route else "parallel"),
        name="merge",
    )(*ins, *[r[0] for r in riders])
    n_plain = len(res) - len(cast)
    return list(res[:n_plain]) + [wb16.reshape(w.shape) for wb16, w in zip(res[n_plain:], cast)]


def _ffn_kernel(tf, x1_ref, hf_ref, w1_ref, w3_ref, w2_ref, o_ref):
    hf = hf_ref[...]
    acc = x1_ref[...]
    for lo in range(0, FFN_DIM, tf):
        cols = slice(lo, min(lo + tf, FFN_DIM))
        a = jnp.dot(hf, w1_ref[:, cols], preferred_element_type=F32)
        b = jnp.dot(hf, w3_ref[:, cols], preferred_element_type=F32)
        acc = acc + jnp.dot((_silu(a) * b).astype(BF16), w2_ref[cols, :], preferred_element_type=F32)
    o_ref[...] = acc


def _ffn(x1, hf, w1, w3, w2, tm=512, tf=512):
    n_tok = x1.shape[0]
    row = pl.BlockSpec((tm, D_MODEL), lambda i: (i, 0))
    return pl.pallas_call(
        functools.partial(_ffn_kernel, tf),
        grid=(n_tok // tm,),
        in_specs=[row, row, _resident(w1.shape), _resident(w3.shape), _resident(w2.shape)],
        out_specs=row,
        out_shape=jax.ShapeDtypeStruct((n_tok, D_MODEL), F32),
        compiler_params=_params("parallel"),
        name="ffn",
    )(x1, hf, w1, w3, w2)


def _gather_rows(table, idx, window=128):
    split, n_table, width = table.shape
    n = idx.shape[0]
    flat_idx = (idx[None, :] + n_table * jnp.arange(split, dtype=jnp.int32)[:, None]).reshape(-1)
    return _gather_subrows(table.reshape(split * n_table, width), flat_idx, window).reshape(split, n, width)


def _gather_subrows(table, idx, window):
    n = idx.shape[0]
    d = table.shape[1]
    mesh = plsc.VectorSubcoreMesh(core_axis_name="core", subcore_axis_name="subcore")

    @functools.partial(pl.kernel, out_type=jax.ShapeDtypeStruct((n, d), table.dtype), mesh=mesh,
                       name="gather_rows")
    def gather(table_hbm, idx_hbm, out_hbm):
        def body(idx_vmem, out_vmem):
            pltpu.sync_copy(table_hbm.at[idx_vmem.at[0]], out_vmem)

        pltpu.emit_pipeline(
            body,
            grid=(n // window,),
            in_specs=[pl.BlockSpec((1, window), index_map=lambda i: (0, i))],
            out_specs=[pl.BlockSpec((window, d), index_map=lambda i: (i, 0))],
            core_axis_name=("core", "subcore"),
            dimension_semantics=(pltpu.PARALLEL,),
        )(idx_hbm, out_hbm)

    return gather(table, idx.reshape(1, n))


def _scatter_rows(table, dest, n_out, window=128):
    split, n_table, width = table.shape
    n = dest.shape[0]
    flat_dest = (dest[None, :] + n_out * jnp.arange(split, dtype=jnp.int32)[:, None]).reshape(1, split * n)
    src_blocks, blocks = n_table // window, n // window
    mesh = plsc.VectorSubcoreMesh(core_axis_name="core", subcore_axis_name="subcore")

    @functools.partial(pl.kernel, out_type=jax.ShapeDtypeStruct((split * n_out, width), table.dtype), mesh=mesh,
                       name="scatter_rows")
    def scatter(table_hbm, dest_hbm, out_hbm):
        def body(rows_vmem, dest_vmem):
            pltpu.sync_copy(rows_vmem, out_hbm.at[dest_vmem.at[0]])

        pltpu.emit_pipeline(
            body,
            grid=(split * blocks,),
            in_specs=[pl.BlockSpec((window, width),
                                   index_map=lambda i: ((i // blocks) * src_blocks + (i % blocks) % src_blocks, 0)),
                      pl.BlockSpec((1, window), index_map=lambda i: (0, i))],
            out_specs=[],
            core_axis_name=("core", "subcore"),
            dimension_semantics=(pltpu.PARALLEL,),
        )(table_hbm, dest_hbm)

    return scatter(table.reshape(split * n_table, width), flat_dest).reshape(split, n_out, width)


def _expert_kernel(tf, ge_ref, nv_ref, x_ref, w1_ref, w3_ref, w2_ref, y_ref):
    g = pl.program_id(0)

    @pl.when(g < nv_ref[0])
    def _():
        x = _unpack_rows([x_ref[j] for j in range(ROW_SPLIT)]).astype(BF16)
        acc = jnp.zeros((MOE_ROWS, D_MODEL), F32)
        for lo in range(0, EXPERT_DIM, tf):
            a = jnp.dot(x, w1_ref[:, lo:lo + tf], preferred_element_type=F32)
            b = jnp.dot(x, w3_ref[:, lo:lo + tf], preferred_element_type=F32)
            acc = acc + jnp.dot((_silu(a) * b).astype(BF16), w2_ref[lo:lo + tf, :], preferred_element_type=F32)
        for j, words in enumerate(_pack_rows(acc)):
            y_ref[j] = words

    @pl.when(g >= nv_ref[0])
    def _():
        y_ref[...] = jnp.zeros_like(y_ref)


def _experts(xg, group_expert, n_valid, w1, w3, w2, tf=512):
    n_rows = xg.shape[1]
    rows = pl.BlockSpec((ROW_SPLIT, MOE_ROWS, SUBROW), lambda g, ge, nv: (0, g, 0))
    return pl.pallas_call(
        functools.partial(_expert_kernel, tf),
        grid_spec=pltpu.PrefetchScalarGridSpec(
            num_scalar_prefetch=2,
            grid=(n_rows // MOE_ROWS,),
            in_specs=[rows,
                      pl.BlockSpec((None, D_MODEL, EXPERT_DIM), lambda g, ge, nv: (ge[g], 0, 0)),
                      pl.BlockSpec((None, D_MODEL, EXPERT_DIM), lambda g, ge, nv: (ge[g], 0, 0)),
                      pl.BlockSpec((None, EXPERT_DIM, D_MODEL), lambda g, ge, nv: (ge[g], 0, 0))],
            out_specs=rows,
        ),
        out_shape=jax.ShapeDtypeStruct((ROW_SPLIT, n_rows, SUBROW), jnp.uint32),
        compiler_params=_params("arbitrary", vmem_limit=BIG_VMEM_LIMIT),
        name="experts",
    )(group_expert, n_valid, xg, w1, w3, w2)


def _combine_kernel(x1_ref, y0_ref, y1_ref, gate_ref, o_ref):
    gate = gate_ref[...]
    y0 = _unpack_rows([y0_ref[j] for j in range(ROW_SPLIT)])
    y1 = _unpack_rows([y1_ref[j] for j in range(ROW_SPLIT)])
    o_ref[...] = x1_ref[...] + gate[:, 0:1] * y0 + gate[:, 1:2] * y1


def _combine(x1, yg, gate_t, tm=512):
    n_tok = x1.shape[0]
    nb = n_tok // tm
    return pl.pallas_call(
        _combine_kernel,
        grid=(nb,),
        in_specs=[pl.BlockSpec((tm, D_MODEL), lambda i: (i, 0)),
                  pl.BlockSpec((ROW_SPLIT, tm, SUBROW), lambda i: (0, i, 0)),
                  pl.BlockSpec((ROW_SPLIT, tm, SUBROW), lambda i: (0, i + nb, 0)),
                  pl.BlockSpec((tm, 2), lambda i: (i, 0))],
        out_specs=pl.BlockSpec((tm, D_MODEL), lambda i: (i, 0)),
        out_shape=jax.ShapeDtypeStruct((n_tok, D_MODEL), F32),
        compiler_params=_params("parallel"),
        name="moe_combine",
    )(x1, yg, yg, gate_t)


def _moe(x1, routing, w1, w3, w2):
    n_tok = x1.shape[0]
    idx, gate, rank, cnt, hf = routing
    counts = cnt[:, 0].astype(jnp.int32)
    padded = (counts + MOE_ROWS - 1) // MOE_ROWS * MOE_ROWS
    end_padded = jnp.cumsum(padded)
    start_padded = end_padded - padded
    start = sum(jnp.where(idx == e, start_padded[e], 0) for e in range(N_EXPERTS))
    dest = (start + rank).reshape(-1)
    n_groups = (n_tok * 2 + MOE_ROWS - 1) // MOE_ROWS + N_EXPERTS
    n_rows = n_groups * MOE_ROWS
    group_row = jnp.arange(n_groups, dtype=jnp.int32)[:, None] * MOE_ROWS
    group_expert = jnp.minimum(jnp.sum(group_row >= end_padded[None, :], axis=1), N_EXPERTS - 1).astype(jnp.int32)
    n_valid = (end_padded[-1:] // MOE_ROWS).astype(jnp.int32)
    fill = jnp.arange(MOE_ROWS, dtype=jnp.int32)[None, :]
    pad_dest = jnp.where(fill < (padded - counts)[:, None], (start_padded + counts)[:, None] + fill,
                         n_rows - MOE_ROWS + fill).reshape(-1)
    xg = _scatter_rows(hf, jnp.concatenate([dest, pad_dest]), n_rows)
    y = _experts(xg, group_expert, n_valid, w1, w3, w2)
    return _combine(x1, _gather_rows(y, dest), gate.T)


def _row(v, pad=0):
    v = v.reshape(1, -1).astype(F32)
    return jnp.pad(v, ((0, 0), (0, pad))) if pad else v


def _rows_at(w, start, total):
    return jnp.pad(w.astype(F32), ((start, total - start - w.shape[0]), (0, 0)))


def kernel(x, w_in, norm_mix, rwkv_mu, rwkv_decay_up, rwkv_w0, rwkv_a_up, rwkv_a0, rwkv_gate_up, rwkv_k_k, rwkv_k_a, rwkv_r_k, rwkv_ln_w, rwkv_ln_b, vres_down, vres_up, vres_v0, ssm_conv_w, ssm_conv_b, ssm_dt_bias, ssm_a_log, ssm_d, ssm_norm_w, att_q_gain, att_k_gain, att_rel_bias, gla_gate_up, gla_gate_bias, gla_norm_w, w_branch, w_out, norm_ffn, ffn_w1, ffn_w3, ffn_w2, moe_router, moe_w1, moe_w3, moe_w2):
    bsz, seq, _ = x.shape
    n_tok = bsz * seq
    depth = w_in.shape[0]
    x2 = x.reshape(n_tok, D_MODEL)
    v_first = None
    for l in range(depth):
        n_cols = w_in.shape[2]
        w_t = jnp.swapaxes(w_in, 1, 2).reshape(depth * n_cols, D_MODEL)
        patches = jnp.zeros((2, PACK_ROWS, D_MODEL), F32)
        if l > 0:
            lo = RWKV_COLS % PACK_ROWS
            patches = patches.at[1, lo:lo + 32].set(vres_down[l - 1].T)
        w_mix = _pack_weight_rows(w_t, _mixer_plan(l * n_cols), patches)
        o_gate = RWKV_COLS + SSM_COLS + ATT_COLS + GLA_COLS
        wg = _pack_weight_rows(w_t, [(l * n_cols + r, PACK_ROWS, 0) for r in range(o_gate, n_cols, PACK_ROWS)], patches)
        moe = l % 2 == 1
        up_w = [moe_w3[l // 2]] if moe else [ffn_w1[l // 2], ffn_w3[l // 2]]
        h, za, zb, zc, zd, wb_b, wo_b, *up_b = _inproj(x2, _row(norm_mix[l]), w_mix,
                                                       cast=[w_branch[l], w_out[l]] + up_w)
        shp = lambda z: z.reshape(bsz, seq, z.shape[-1])

        rw = dict(mu=_row(rwkv_mu[l], RWKV_PAD - RWKV_COLS),
                  dup=_rows_at(rwkv_decay_up[l], 0, LANES), w0=_row(rwkv_w0[l]),
                  aup=_rows_at(rwkv_a_up[l], 64, LANES), a0=_row(rwkv_a0[l]),
                  gup=_rows_at(rwkv_gate_up[l], 0, 256).astype(BF16),
                  kk=_row(rwkv_k_k[l]), ka=_row(rwkv_k_a[l]), rk=_row(rwkv_r_k[l]),
                  lnw=_row(rwkv_ln_w[l]), lnb=_row(rwkv_ln_b[l]))
        if l > 0:
            rw.update(vup=_rows_at(vres_up[l - 1], 160, 256), v0=_row(vres_v0[l - 1]))
        ssm = dict(cw=ssm_conv_w[l], cb=_row(ssm_conv_b[l]), dtb=_row(ssm_dt_bias[l], LANES - SSM_HEADS),
                   alog=_row(ssm_a_log[l], LANES - SSM_HEADS), dexp=_row(jnp.repeat(ssm_d[l], CHUNK)),
                   nw=_row(ssm_norm_w[l]))
        gla = dict(gup=_rows_at(gla_gate_up[l], 0, LANES), gb=_row(gla_gate_bias[l]), nw=_row(gla_norm_w[l]))
        (o_c,), (o_d,), (o_b,), res_a = _run_parts(
            [_attention_part(shp(zc), att_q_gain[l], att_k_gain[l], att_rel_bias, MIXER_BLOCK),
             _gla_part(shp(zd), gla, MIXER_BLOCK), _mamba_part(shp(zb), ssm, MIXER_BLOCK),
             _rwkv_part(shp(za), v_first, rw, MIXER_BLOCK)],
            grid=(bsz, seq // MIXER_BLOCK), name="mixers")
        o_a = res_a[0]
        if l == 0:
            v_first = res_a[1]
        outs = [o.reshape(n_tok, BRANCH_DIM) for o in (o_a, o_b, o_c, o_d)]
        merge_w = (wg, wb_b, wo_b, _row(norm_ffn[l]))
        if not moe:
            x1, hf, *w1_b = _merge(x2, h, outs, *merge_w, cast=(moe_w1[l // 2],) if l + 1 < depth else ())
            x2 = _ffn(x1, hf, *up_b, ffn_w2[l // 2].astype(BF16))
        else:
            x1, *routing, w2_b = _merge(x2, h, outs, *merge_w, wr_t=moe_router[l // 2].T, cast=(moe_w2[l // 2],))
            x2 = _moe(x1, routing, w1_b[0], up_b[0], w2_b)
    return x2.reshape(bsz, seq, D_MODEL)
```

```python
import functools
from typing import Any, NamedTuple

import jax
import jax.numpy as jnp
from jax import lax
from jax.experimental import pallas as pl
from jax.experimental.pallas import tpu as pltpu
from jax.experimental.pallas import tpu_sc as plsc

F32 = jnp.float32
BF16 = jnp.bfloat16

D_MODEL = 1024
CHUNK = 64
BRANCH_DIM = 512
NORM_EPS = 1e-6
LANES = 128
VMEM_LIMIT = 56 * 1024 * 1024
BIG_VMEM_LIMIT = 61 * 1024 * 1024

LOG2_E = 1.4426950408889634

RWKV_LN_EPS = 64e-5
RWKV_DECAY_SCALE = 0.6065306597126334
RWKV_COLS = 1824
RWKV_PAD = 1920
SSM_COLS = 1544
SSM_PAD = 1664
SSM_HEADS = 8
SSM_STATE = 128
ATT_COLS = 1536
ATT_HEADS = 8
ATT_LEFT = 8 * CHUNK
REL_CLIP = 2 * CHUNK
GLA_COLS = 1552
GLA_PAD = 1664
GLA_GATE_NORM = 16.0
FFN_DIM = 2816
N_EXPERTS = 8
EXPERT_DIM = 3584
MOE_ROWS = 512
ROW_SPLIT = 2
SUBROW = D_MODEL // (2 * ROW_SPLIT)


def _dot(a, b):
    return jnp.dot(a.astype(BF16), b.astype(BF16), preferred_element_type=F32)


def _dot_nt(a, b):
    return lax.dot_general(a.astype(BF16), b.astype(BF16), (((1,), (1,)), ((), ())),
                           preferred_element_type=F32)


def _dot_tn(a, b):
    return lax.dot_general(a.astype(BF16), b.astype(BF16), (((0,), (0,)), ((), ())),
                           preferred_element_type=F32)


def _hi_lo(a):
    hi = a.astype(BF16)
    return hi, (a - hi.astype(F32)).astype(BF16)


def _dot_3x(a, b):
    a_hi, a_lo = _hi_lo(a)
    b_hi, b_lo = _hi_lo(b)
    return (jnp.dot(a_hi, b_hi, preferred_element_type=F32) + jnp.dot(a_lo, b_hi, preferred_element_type=F32)
            + jnp.dot(a_hi, b_lo, preferred_element_type=F32))


def _dot_split(a, m):
    m = m.astype(BF16)
    hi, lo = _hi_lo(a)
    return jnp.dot(hi, m, preferred_element_type=F32) + jnp.dot(lo, m, preferred_element_type=F32)


def _dot_split_l(m, a):
    m = m.astype(BF16)
    hi, lo = _hi_lo(a)
    return jnp.dot(m, hi, preferred_element_type=F32) + jnp.dot(m, lo, preferred_element_type=F32)


def _pack_rows(x):
    bits = pltpu.bitcast(x.astype(BF16).astype(F32), jnp.uint32)
    half = D_MODEL // 2
    return [(bits[:, p * SUBROW:(p + 1) * SUBROW] & jnp.uint32(0xFFFF0000))
            | (bits[:, half + p * SUBROW:half + (p + 1) * SUBROW] >> jnp.uint32(16)) for p in range(ROW_SPLIT)]


def _unpack_rows(planes):
    hi = [pltpu.bitcast(w & jnp.uint32(0xFFFF0000), F32) for w in planes]
    lo = [pltpu.bitcast(w << jnp.uint32(16), F32) for w in planes]
    return jnp.concatenate(hi + lo, axis=-1)


def _softplus(x):
    return jnp.maximum(x, 0.0) + jnp.log(1.0 + jnp.exp(-jnp.abs(x)))


def _sigmoid(x):
    return 1.0 / (1.0 + jnp.exp(-x))


def _silu(x):
    return x * _sigmoid(x)


def _iota2(shape, axis):
    return lax.broadcasted_iota(jnp.int32, shape, axis)


def _group_mean_matrix(n, group):
    r = _iota2((n, n), 0) // group
    c = _iota2((n, n), 1) // group
    return jnp.where(r == c, 1.0 / group, 0.0).astype(F32)


def _chunk_tri(n):
    r = _iota2((n, n), 0)
    c = _iota2((n, n), 1)
    return jnp.where((r // CHUNK == c // CHUNK) & (r >= c), 1.0, 0.0).astype(F32)


def _resident(shape):
    nd = len(shape)
    return pl.BlockSpec(shape, lambda *_: (0,) * nd, pipeline_mode=pl.Buffered(1))


def _params(*sem, vmem_limit=VMEM_LIMIT):
    return pltpu.CompilerParams(dimension_semantics=sem, vmem_limit_bytes=vmem_limit)


_DONE = object()


class _Part(NamedTuple):
    body: Any
    inputs: list
    in_specs: list
    out_shapes: list
    out_specs: list
    scratch_shapes: list


def _run_parts(parts, grid, name):
    def split(refs, counts):
        out, lo = [], 0
        for n in counts:
            out.append(refs[lo:lo + n])
            lo += n
        return out

    n_in = [len(p.inputs) for p in parts]
    n_out = [len(p.out_shapes) for p in parts]
    n_scr = [len(p.scratch_shapes) for p in parts]

    def kernel(*refs):
        ins = split(refs[:sum(n_in)], n_in)
        outs = split(refs[sum(n_in):sum(n_in) + sum(n_out)], n_out)
        scr = split(refs[sum(n_in) + sum(n_out):], n_scr)
        stages = [part.body(*i, *o, *c) for part, i, o, c in zip(parts, ins, outs, scr)]
        while stages:
            for stage in list(stages):
                if next(stage, _DONE) is _DONE:
                    stages.remove(stage)

    flat = lambda field: [x for p in parts for x in getattr(p, field)]
    res = pl.pallas_call(
        kernel,
        grid=grid,
        in_specs=flat("in_specs"),
        out_specs=flat("out_specs"),
        out_shape=flat("out_shapes"),
        scratch_shapes=flat("scratch_shapes"),
        compiler_params=_params("parallel", "arbitrary", vmem_limit=BIG_VMEM_LIMIT),
        name=name,
    )(*flat("inputs"))
    return split(list(res), n_out)


def _cast_rider(w, steps):
    cols = w.shape[-1]
    sliced = (steps, w.size // cols // steps, cols)
    assert w.size // cols % steps == 0 and sliced[1] % 16 == 0
    return w.reshape(sliced), pl.BlockSpec((None,) + sliced[1:], lambda i: (i, 0, 0)), jax.ShapeDtypeStruct(sliced, BF16)


def _cast_slices(src_refs, dst_refs):
    for src, dst in zip(src_refs, dst_refs):
        dst[...] = src[...].astype(dst.dtype)


MIXER_COLS = (RWKV_PAD, SSM_PAD, ATT_COLS, GLA_PAD)
MIXER_BLOCK = 256


PACK_ROWS = 128
GATE_PACK_ROWS = 512


def _pack_kernel(src_ref, keep_ref, fill_ref, w_ref, patch_ref, o_ref):
    del src_ref, fill_ref
    rows = _iota2(o_ref.shape, 0)
    o_ref[...] = jnp.where(rows < keep_ref[pl.program_id(0)], w_ref[...], patch_ref[...]).astype(o_ref.dtype)


def _pack_weight_rows(w_t, plan, patches):
    src, keep, fill = (jnp.asarray(v, jnp.int32) for v in zip(*plan))
    rows, cols = patches.shape[1:]
    return pl.pallas_call(
        _pack_kernel,
        grid_spec=pltpu.PrefetchScalarGridSpec(
            num_scalar_prefetch=3,
            grid=(len(plan),),
            in_specs=[pl.BlockSpec((pl.Element(rows), pl.Element(cols)),
                                   lambda j, src, keep, fill: (pl.multiple_of(src[j], 8), 0)),
                      pl.BlockSpec((None, rows, cols), lambda j, src, keep, fill: (fill[j], 0, 0))],
            out_specs=pl.BlockSpec((rows, cols), lambda j, src, keep, fill: (j, 0)),
        ),
        out_shape=jax.ShapeDtypeStruct((len(plan) * rows, cols), BF16),
        compiler_params=_params("parallel"),
        name="pack_weights",
    )(src, keep, fill, w_t, patches)


def _mixer_plan(base):
    o_ss, o_at = RWKV_COLS + SSM_COLS, RWKV_COLS + SSM_COLS + ATT_COLS
    plan = []

    def rows(lo, hi, fill=0):
        for r in range(lo, hi, PACK_ROWS):
            plan.append((base + r, min(PACK_ROWS, hi - r), fill))

    rows(0, RWKV_COLS, fill=1)
    rows(RWKV_COLS, o_ss)
    rows(o_ss, o_at)
    rows(o_at, o_at + 1024)
    rows(o_at + 1040, o_at + GLA_COLS)
    rows(o_at + 1024, o_at + 1040)
    assert len(plan) * PACK_ROWS == sum(MIXER_COLS)
    return plan


def _inproj_kernel(n_cast, x_ref, g_ref, w_ref, *refs):
    cast_in, (h_ref, *z_refs), cast_out = refs[:n_cast], refs[n_cast:len(refs) - n_cast], refs[len(refs) - n_cast:]
    _cast_slices(cast_in, cast_out)
    x = x_ref[...]
    ms = jnp.mean(x * x, axis=-1, keepdims=True)
    h = (x * lax.rsqrt(ms + NORM_EPS) * g_ref[...]).astype(BF16)
    h_ref[...] = h
    lo = 0
    for pair in (z_refs[:2], z_refs[2:]):
        widths = [z_ref.shape[-1] for z_ref in pair]
        z = _dot_nt(h, w_ref[lo:lo + sum(widths), :])
        pair[0][...] = z[:, :widths[0]]
        pair[1][...] = z[:, widths[0]:]
        lo += sum(widths)


def _inproj(x2, g, w_mix, cast=(), tm=512):
    n_tok = x2.shape[0]
    row = lambda n: pl.BlockSpec((tm, n), lambda i: (i, 0))
    riders = [_cast_rider(w, n_tok // tm) for w in cast]
    res = pl.pallas_call(
        functools.partial(_inproj_kernel, len(cast)),
        grid=(n_tok // tm,),
        in_specs=[row(D_MODEL), _resident((1, D_MODEL)), _resident(w_mix.shape)] + [r[1] for r in riders],
        out_specs=[row(D_MODEL)] + [row(n) for n in MIXER_COLS] + [r[1] for r in riders],
        out_shape=[jax.ShapeDtypeStruct((n_tok, D_MODEL), BF16)]
        + [jax.ShapeDtypeStruct((n_tok, n), F32) for n in MIXER_COLS] + [r[2] for r in riders],
        compiler_params=_params("parallel", vmem_limit=BIG_VMEM_LIMIT if cast else VMEM_LIMIT),
        name="inproj",
    )(x2, g, w_mix, *[r[0] for r in riders])
    n_plain = len(res) - len(cast)
    return list(res[:n_plain]) + [wb16.reshape(w.shape) for wb16, w in zip(res[n_plain:], cast)]


def _rwkv_block(r, k, v, kk, ka, ld, tb, state_ref, y_ref):
    n_chunks = tb // CHUNK
    cum = _dot_split_l(_chunk_tri(tb), ld)
    p_inv = jnp.exp(-cum)
    a_t = -kk * jnp.exp(cum - ld)
    b_t = kk * ka * p_inv
    k_t = k * p_inv
    r_t = r * jnp.exp(cum)
    bf = lambda z: z.astype(BF16)
    a_b, b_b, k_b, r_b, v_b = bf(a_t), bf(b_t), bf(k_t), bf(r_t), bf(v)
    yield

    lane_head = _iota2((CHUNK, LANES), 1) // CHUNK
    row = _iota2((CHUNK, LANES), 0)
    col = _iota2((CHUNK, LANES), 1) % CHUNK
    strict = row > col
    incl = row >= col
    bd_mask = (_iota2((LANES, LANES), 0) // CHUNK) == (_iota2((LANES, LANES), 1) // CHUNK)
    eye = jnp.where(_iota2((LANES, LANES), 0) == _iota2((LANES, LANES), 1), 1.0, 0.0).astype(F32)

    def stack(z):
        return jnp.concatenate([jnp.where(lane_head == 0, z, 0.0), jnp.where(lane_head == 1, z, 0.0)], axis=0)

    def sub(z, u):
        c, p = u
        return z[c * CHUNK:(c + 1) * CHUNK, p * LANES:(p + 1) * LANES]

    def prepare(units):
        gram = {u: _dot_nt(jnp.concatenate([sub(a_b, u), sub(r_b, u)], axis=0),
                           jnp.concatenate([stack(sub(b_b, u)), stack(sub(k_b, u))], axis=0))
                for u in units}
        yield
        a_rb = {u: bf(jnp.where(incl, gram[u][CHUNK:, :LANES], 0.0)) for u in units}
        a_akrk = {u: bf(jnp.concatenate([jnp.where(strict, gram[u][:CHUNK, LANES:], 0.0),
                                         jnp.where(incl, gram[u][CHUNK:, LANES:], 0.0)], axis=0)) for u in units}
        pw = {u: stack(jnp.where(strict, gram[u][:CHUNK, :LANES], 0.0)) for u in units}
        t_bd = {u: eye + pw[u] for u in units}
        pw = {u: _dot(pw[u], pw[u]) for u in units}
        yield
        for _ in range(CHUNK.bit_length() - 3):
            sp = {u: _dot(jnp.concatenate([t_bd[u], pw[u]], axis=0), pw[u]) for u in units}
            t_bd = {u: t_bd[u] + sp[u][:LANES] for u in units}
            pw = {u: sp[u][LANES:] for u in units}
            yield
        t_bd = {u: bf(t_bd[u] + _dot(t_bd[u], pw[u])) for u in units}
        v_bd = {u: stack(sub(v_b, u)) for u in units}
        avyv = {u: _dot(a_akrk[u], v_bd[u]) for u in units}
        yield
        wuv = {u: _dot(t_bd[u], jnp.concatenate([stack(sub(a_b, u)), stack(bf(avyv[u][:CHUNK]))], axis=1))
               for u in units}
        wr = {u: jnp.concatenate([bf(wuv[u][:, :LANES]), sub(r_b, u)], axis=0) for u in units}
        yield
        return a_rb, avyv, wuv, wr

    a_rb, avyv, wuv, wr = yield from prepare([(c, p) for c in range(n_chunks) for p in range(4)])

    h = [state_ref[p] for p in range(4)]
    for c in range(n_chunks):
        last = c * CHUNK + CHUNK - 1
        p_last = jnp.exp(cum[last:last + 1, :])
        hs = [_dot_nt(wr[(c, p)], h[p]) for p in range(4)]
        u2 = [hs[p][:LANES] + wuv[(c, p)][:, LANES:] for p in range(4)]
        u_p = [u2[p][:CHUNK] + u2[p][CHUNK:] for p in range(4)]
        upd = [_dot_tn(jnp.concatenate([u_p[p], sub(v, (c, p))], axis=0),
                       jnp.concatenate([sub(b_t, (c, p)), sub(k_t, (c, p))], axis=0)
                       * p_last[:, p * LANES:(p + 1) * LANES]) for p in range(4)]
        for p in range(4):
            y_ref[c * CHUNK:(c + 1) * CHUNK, p * LANES:(p + 1) * LANES] = (
                hs[p][LANES:] + _dot(a_rb[(c, p)], stack(u_p[p])) + avyv[(c, p)][CHUNK:])
        h = [h[p] * p_last[:, p * LANES:(p + 1) * LANES] + jnp.where(bd_mask, upd[p], 0.0) for p in range(4)]
        yield
    for p in range(4):
        state_ref[p] = h[p]


def _rwkv_kernel(has_vres, tb, *refs):
    if has_vres:
        (za_ref, vfirst_ref, mu_ref, dup_ref, w0_ref, aup_ref, a0_ref, gup_ref, vup_ref, v0_ref,
         kk_ref, ka_ref, rk_ref, lnw_ref, lnb_ref, o_ref, prev_ref, state_ref, y_ref) = refs
    else:
        (za_ref, mu_ref, dup_ref, w0_ref, aup_ref, a0_ref, gup_ref,
         kk_ref, ka_ref, rk_ref, lnw_ref, lnb_ref, o_ref, vraw_ref, prev_ref, state_ref, y_ref) = refs

    @pl.when(pl.program_id(1) == 0)
    def _():
        prev_ref[...] = jnp.zeros_like(prev_ref)
        state_ref[...] = jnp.zeros_like(state_ref)

    yield
    za = za_ref[...]
    shifted = pltpu.roll(za, 1, axis=0)
    shifted = jnp.where(_iota2(za.shape, 0) == 0, prev_ref[7:8, :], shifted)
    prev_ref[...] = za[tb - 8:, :]
    za = za + (shifted - za) * mu_ref[...]

    r = za[:, 0:512]
    k = za[:, 512:1024]
    v = za[:, 1024:1536]
    xwa = za[:, 1536:1664]
    xg = za[:, 1664:1920]
    ld = -RWKV_DECAY_SCALE * _sigmoid(w0_ref[...] + _dot_3x(jnp.tanh(xwa), dup_ref[...]))
    a = _sigmoid(a0_ref[...] + _dot(xwa, aup_ref[...]))
    g = _dot(_sigmoid(xg), gup_ref[...])
    if has_vres:
        v_mix = _sigmoid(v0_ref[...] + _dot(xg, vup_ref[...]))
        v = v + (vfirst_ref[...] - v) * v_mix
    else:
        vraw_ref[...] = v
    kk = k * kk_ref[...]
    k = k * (1.0 + (a - 1.0) * ka_ref[...])
    head_sum = _group_mean_matrix(512, CHUNK) * float(CHUNK)
    kk = kk / jnp.maximum(jnp.sqrt(_dot(kk * kk, head_sum)), 1e-12)
    yield

    yield from _rwkv_block(r, k, v, kk, a, ld, tb, state_ref, y_ref)
    y = y_ref[...]
    head_mean = _group_mean_matrix(512, CHUNK)
    mean = _dot_split(y, head_mean)
    yc = y - mean
    var = _dot(yc * yc, head_mean)
    y = yc * lax.rsqrt(var + RWKV_LN_EPS) * lnw_ref[...] + lnb_ref[...]
    y = y + _dot(r * k * rk_ref[...], head_sum) * v
    o_ref[...] = (y * g).astype(o_ref.dtype)


def _rwkv_part(za, v_first, prm, tb):
    bsz, seq, _ = za.shape
    has_vres = v_first is not None
    blk = lambda n: pl.BlockSpec((None, tb, n), lambda b, i: (b, i, 0))
    names = (["mu", "dup", "w0", "aup", "a0", "gup"] + (["vup", "v0"] if has_vres else [])
             + ["kk", "ka", "rk", "lnw", "lnb"])
    weights = [prm[n] for n in names]
    ins = [za] + ([v_first] if has_vres else []) + weights
    in_specs = [blk(RWKV_PAD)] + ([blk(512)] if has_vres else []) + [_resident(w.shape) for w in weights]
    out_shape = [jax.ShapeDtypeStruct((bsz, seq, 512), BF16)]
    out_specs = [blk(512)]
    if not has_vres:
        out_shape.append(jax.ShapeDtypeStruct((bsz, seq, 512), F32))
        out_specs.append(blk(512))
    return _Part(functools.partial(_rwkv_kernel, has_vres, tb), ins, in_specs, out_shape, out_specs,
                 [pltpu.VMEM((8, RWKV_PAD), F32), pltpu.VMEM((4, LANES, LANES), F32), pltpu.VMEM((tb, 512), F32)])


def _mamba_kernel(tb, zb_ref, cw_ref, cb_ref, dtb_ref, alog_ref, dexp_ref, nw_ref,
                  o_ref, xbuf_ref, state_ref, y_ref):
    @pl.when(pl.program_id(1) == 0)
    def _():
        xbuf_ref[0:8, :] = jnp.zeros((8, 1024), F32)
        state_ref[...] = jnp.zeros_like(state_ref)

    yield
    xbuf_ref[8:8 + tb, :] = zb_ref[:, 512:1536]
    conv = cb_ref[...]
    for i in range(4):
        conv = conv + cw_ref[i:i + 1, :] * xbuf_ref[5 + i:5 + i + tb, :]
    xbuf_ref[0:8, :] = xbuf_ref[tb:tb + 8, :]
    xbc = _silu(conv)
    xs = xbc[:, 0:512]
    bm = xbc[:, 512:768]
    cm = xbc[:, 768:1024]
    yield

    lane = _iota2((1, LANES), 1)
    dt = _softplus(zb_ref[:, 1536:1664] + dtb_ref[...])
    a_neg = jnp.where(lane < SSM_HEADS, -jnp.exp(alog_ref[...]), 0.0)
    acs = _dot_split_l(_chunk_tri(tb), dt * a_neg)
    expand = jnp.where(_iota2((LANES, 512), 0) == _iota2((LANES, 512), 1) // CHUNK, 1.0, 0.0)
    dt_e = _dot(dt, expand)
    acs_e = _dot_split(acs, expand)
    xdt = xs * dt_e
    yield

    n_chunks = tb // CHUNK
    causal = _iota2((CHUNK, 512), 0) >= _iota2((CHUNK, 512), 1) % CHUNK
    pair_head = _iota2((CHUNK, LANES), 1) // CHUNK
    spread = jnp.where(_iota2((CHUNK, 512), 0) == _iota2((CHUNK, 512), 1) % CHUNK, 1.0, 0.0)
    own_head = _iota2((SSM_HEADS, 512), 0) == _iota2((SSM_HEADS, 512), 1) // CHUNK
    rows_of = lambda c: slice(c * CHUNK, (c + 1) * CHUNK)
    grp = lambda g: slice(g * SSM_STATE, (g + 1) * SSM_STATE)

    def stack(z):
        return jnp.concatenate([jnp.where(pair_head == 0, z, 0.0), jnp.where(pair_head == 1, z, 0.0)], axis=0)

    y_intra, upd, e_in, e_last = [], [], [], []
    for c in range(n_chunks):
        rows = rows_of(c)
        acs_ec = acs_e[rows]
        acs_t = acs[rows].T[0:SSM_HEADS]
        acs_row = jnp.sum(jnp.where(own_head, _dot_split(acs_t, spread), 0.0), axis=0, keepdims=True)
        decay = jnp.exp(jnp.where(causal, acs_ec - acs_row, -jnp.inf))
        cb = jnp.concatenate(
            [_dot_nt(cm[rows, grp(g)], jnp.concatenate([bm[rows, grp(g)]] * 4, axis=0)) for g in range(2)], axis=1)
        m = cb * decay
        yield
        xdt_c = xdt[rows]
        y_intra.append([_dot(m[:, p * LANES:(p + 1) * LANES], stack(xdt_c[:, p * LANES:(p + 1) * LANES]))
                        for p in range(4)])
        last_e = acs_ec[CHUNK - 1:CHUNK, :]
        x_out = xdt_c * jnp.exp(last_e - acs_ec)
        upd.append([_dot_tn(bm[rows, grp(g)], x_out[:, g * 256:(g + 1) * 256]) for g in range(2)])
        e_in.append(jnp.exp(acs_ec))
        e_last.append(jnp.exp(last_e))
        yield

    state = [state_ref[g] for g in range(2)]
    for c in range(n_chunks):
        rows = rows_of(c)
        for g in range(2):
            ls = slice(g * 256, (g + 1) * 256)
            y_in = jnp.concatenate(y_intra[c][2 * g:2 * g + 2], axis=1)
            y_ref[rows, ls] = y_in + _dot(cm[rows, grp(g)], state[g]) * e_in[c][:, ls]
            state[g] = state[g] * e_last[c][:, ls] + upd[c][g]
        yield
    for g in range(2):
        state_ref[g] = state[g]

    y = (y_ref[...] + xs * dexp_ref[...]) * _silu(zb_ref[:, 0:512])
    for g in range(2):
        ls = slice(g * 256, (g + 1) * 256)
        yg = y[:, ls]
        ms = jnp.mean(yg * yg, axis=-1, keepdims=True)
        o_ref[:, ls] = (yg * lax.rsqrt(ms + NORM_EPS) * nw_ref[:, ls]).astype(o_ref.dtype)


def _mamba_part(zb, prm, tb):
    bsz, seq, _ = zb.shape
    blk = lambda n: pl.BlockSpec((None, tb, n), lambda b, i: (b, i, 0))
    weights = [prm[n] for n in ("cw", "cb", "dtb", "alog", "dexp", "nw")]
    return _Part(functools.partial(_mamba_kernel, tb), [zb] + weights,
                 [blk(SSM_PAD)] + [_resident(w.shape) for w in weights],
                 [jax.ShapeDtypeStruct((bsz, seq, 512), BF16)], [blk(512)],
                 [pltpu.VMEM((tb + 8, 1024), F32), pltpu.VMEM((2, SSM_STATE, 256), F32), pltpu.VMEM((tb, 512), F32)])


def _attn_kernel(tq, qkv_ref, qg_ref, kg_ref, bias_ref, o_ref, kn_ref, vb_ref):
    i = pl.program_id(1)
    head_mean = _group_mean_matrix(512, CHUNK)
    win = tq + ATT_LEFT

    @pl.when(i == 0)
    def _():
        kn_ref[0:ATT_LEFT, :] = jnp.zeros((ATT_LEFT, 512), BF16)
        vb_ref[0:ATT_LEFT, :] = jnp.zeros((ATT_LEFT, 512), BF16)

    yield
    start = pl.multiple_of(i * tq, tq)
    k = qkv_ref[:, 512:1024]
    ms = _dot(k * k, head_mean)
    kn_ref[pl.ds(ATT_LEFT + start, tq), :] = (k * lax.rsqrt(ms + NORM_EPS) * kg_ref[...]).astype(BF16)
    vb_ref[pl.ds(ATT_LEFT + start, tq), :] = qkv_ref[:, 1024:1536].astype(BF16)
    q = qkv_ref[:, 0:512]
    ms = _dot(q * q, head_mean)
    qn = q * lax.rsqrt(ms + NORM_EPS) * qg_ref[...] * (CHUNK ** -0.5 * LOG2_E)
    kwin = kn_ref[pl.ds(start, win), :]
    vwin = vb_ref[pl.ds(start, win), :]
    lane_head = _iota2((tq, LANES), 1) // CHUNK
    yield
    for p in range(4):
        ls = slice(p * LANES, (p + 1) * LANES)
        q2 = jnp.concatenate([jnp.where(lane_head == s, qn[:, ls], 0.0) for s in range(2)], axis=0)
        sc = _dot_nt(q2, kwin[:, ls]) + bias_ref[2 * p:2 * p + 2].reshape(2 * tq, win).astype(F32)
        yield
        e = jnp.exp2(sc - jnp.max(sc, axis=-1, keepdims=True))
        yield
        o2 = _dot(e, vwin[:, ls]) / jnp.sum(e, axis=-1, keepdims=True)
        o_ref[:, ls] = jnp.where(lane_head == 0, o2[:tq], o2[tq:]).astype(o_ref.dtype)
        yield


def _band_bias(rel_bias, tq, win):
    period = tq + win
    m = jnp.arange(period)
    d = jnp.where(m < win, m, m - period) - ATT_LEFT
    f = (rel_bias.astype(F32)[:, jnp.clip(d, -REL_CLIP, REL_CLIP) + REL_CLIP] * LOG2_E).astype(BF16)
    g = jnp.tile(f, (1, tq))[:, :tq * (period - 1)].reshape(-1, tq, period - 1)
    bias = g[:, :, :win]
    left = ATT_LEFT // CHUNK
    qc = left + jnp.arange(tq)[:, None] // CHUNK
    kc = jnp.arange(win)[None, :] // CHUNK
    tiles = []
    for blk in range(ATT_LEFT // tq + 1):
        first_kc = jnp.maximum(qc - left, left - blk * (tq // CHUNK))
        tiles.append(jnp.where((kc <= qc) & (kc >= first_kc), bias, -jnp.inf))
    return jnp.stack(tiles)


def _attention_part(zc, q_gain, k_gain, rel_bias, tq):
    bsz, seq, _ = zc.shape
    win = tq + ATT_LEFT
    bias = _band_bias(rel_bias, tq, win)
    qg = jnp.tile(q_gain, ATT_HEADS)[None, :]
    kg = jnp.tile(k_gain, ATT_HEADS)[None, :]
    last_tile = bias.shape[0] - 1
    bias_spec = pl.BlockSpec((None,) + bias.shape[1:], lambda b, i: (jnp.minimum(i, last_tile), 0, 0, 0))
    blk = lambda n: pl.BlockSpec((None, tq, n), lambda b, i: (b, i, 0))
    return _Part(functools.partial(_attn_kernel, tq), [zc, qg, kg, bias],
                 [blk(ATT_COLS), _resident(qg.shape), _resident(kg.shape), bias_spec],
                 [jax.ShapeDtypeStruct((bsz, seq, 512), BF16)], [blk(512)],
                 [pltpu.VMEM((ATT_LEFT + seq, 512), BF16), pltpu.VMEM((ATT_LEFT + seq, 512), BF16)])


def _gla_kernel(tb, zd_ref, gup_ref, gb_ref, nw_ref, o_ref, state_ref, y_ref):
    @pl.when(pl.program_id(1) == 0)
    def _():
        state_ref[...] = jnp.zeros_like(state_ref)

    yield
    q = zd_ref[:, 0:256] * (CHUNK ** -0.5)
    k = zd_ref[:, 256:512]
    v = zd_ref[:, 512:1024]
    log_a = -_softplus(-(_dot_3x(zd_ref[:, 1536:1664], gup_ref[...]) + gb_ref[...])) / GLA_GATE_NORM
    bcum = _dot_split_l(_chunk_tri(tb), log_a)

    k_head = _iota2((CHUNK, 256), 1) // CHUNK
    v_head = _iota2((CHUNK, 512), 1) // LANES
    causal = _iota2((CHUNK, 256), 0) >= _iota2((CHUNK, 256), 1) % CHUNK
    bd = (_iota2((512, 256), 0) // LANES) == (_iota2((512, 256), 1) // CHUNK)
    n_chunks = tb // CHUNK
    qg_all = (q * jnp.exp(bcum)).astype(BF16)
    kg_all = k * jnp.exp(-bcum)
    yield
    qg, att, o_intra, upd, e_last = [], [], [], [], []
    for c in range(n_chunks):
        rows = slice(c * CHUNK, (c + 1) * CHUNK)
        kg_bd = jnp.concatenate([jnp.where(k_head == h, kg_all[rows], 0.0) for h in range(4)], axis=0)
        qg.append(qg_all[rows])
        att.append(jnp.where(causal, _dot_nt(qg[c], kg_bd), 0.0))
        yield
    for c in range(n_chunks):
        rows = slice(c * CHUNK, (c + 1) * CHUNK)
        bc = bcum[rows]
        blast = bc[CHUNK - 1:CHUNK, :]
        v_c = v[rows]
        v_bd = jnp.concatenate([jnp.where(v_head == h, v_c, 0.0) for h in range(4)], axis=0)
        o_intra.append(_dot(att[c], v_bd))
        upd.append(jnp.where(bd, _dot_tn(v_c, k[rows] * jnp.exp(blast - bc)), 0.0))
        e_last.append(jnp.exp(blast))
        yield
    st = state_ref[...]
    for c in range(n_chunks):
        y_ref[c * CHUNK:(c + 1) * CHUNK, :] = o_intra[c] + _dot_nt(qg[c], st)
        st = st * e_last[c] + upd[c]
        yield
    state_ref[...] = st

    o = y_ref[...]
    for h in range(4):
        ls = slice(h * LANES, (h + 1) * LANES)
        oh = o[:, ls]
        ms = jnp.mean(oh * oh, axis=-1, keepdims=True)
        o_ref[:, ls] = (oh * lax.rsqrt(ms + NORM_EPS) * nw_ref[...]
                        * _silu(zd_ref[:, 1024 + h * LANES:1024 + (h + 1) * LANES])).astype(o_ref.dtype)


def _gla_part(zd, prm, tb):
    bsz, seq, _ = zd.shape
    blk = lambda n: pl.BlockSpec((None, tb, n), lambda b, i: (b, i, 0))
    weights = [prm[n] for n in ("gup", "gb", "nw")]
    return _Part(functools.partial(_gla_kernel, tb), [zd] + weights,
                 [blk(GLA_PAD)] + [_resident(w.shape) for w in weights],
                 [jax.ShapeDtypeStruct((bsz, seq, 512), BF16)], [blk(512)],
                 [pltpu.VMEM((512, 256), F32), pltpu.VMEM((tb, 512), F32)])


def _route(tm, hf, wr_ref, idx_ref, gate_ref, rank_ref, cnt_ref, hf_ref, carry_ref):
    @pl.when(pl.program_id(0) == 0)
    def _():
        carry_ref[...] = jnp.zeros_like(carry_ref)

    for j, words in enumerate(_pack_rows(hf)):
        hf_ref[j] = words
    nt = lambda a, b: lax.dot_general(a, b, (((1,), (1,)), ((), ())), preferred_element_type=F32)
    w_hi, w_lo = _hi_lo(wr_ref[...])
    h_hi, h_lo = _hi_lo(hf)
    logits = nt(w_hi, h_hi) + nt(w_lo, h_hi) + nt(w_hi, h_lo)
    e_iota = _iota2((N_EXPERTS, tm), 0)
    m1 = jnp.max(logits, axis=0, keepdims=True)
    i1 = jnp.min(jnp.where(logits == m1, e_iota, N_EXPERTS), axis=0, keepdims=True)
    rest = jnp.where(e_iota == i1, -jnp.inf, logits)
    m2 = jnp.max(rest, axis=0, keepdims=True)
    i2 = jnp.min(jnp.where(rest == m2, e_iota, N_EXPERTS), axis=0, keepdims=True)
    e2 = jnp.exp(m2 - m1)
    gate_ref[0:1, :] = 1.0 / (1.0 + e2)
    gate_ref[1:2, :] = e2 / (1.0 + e2)
    idx_ref[0:1, :] = i1
    idx_ref[1:2, :] = i2
    hit1 = jnp.where(e_iota == i1, 1.0, 0.0)
    hit2 = jnp.where(e_iota == i2, 1.0, 0.0)
    before = jnp.where(_iota2((tm, tm), 0) < _iota2((tm, tm), 1), 1.0, 0.0)
    prior = _dot(hit1 + hit2, before) + carry_ref[:, 0:1]
    rank_ref[0:1, :] = jnp.sum(hit1 * prior, axis=0, keepdims=True).astype(jnp.int32)
    rank_ref[1:2, :] = jnp.sum(hit2 * prior, axis=0, keepdims=True).astype(jnp.int32)
    carry_ref[...] = carry_ref[...] + jnp.sum(hit1 + hit2, axis=1, keepdims=True)
    cnt_ref[...] = carry_ref[...]


def _merge_kernel(tm, route, n_cast, *refs):
    n_in, n_out = (11, 6) if route else (10, 2)
    ins, cast_in = refs[:n_in], refs[n_in:n_in + n_cast]
    outs = refs[n_in + n_cast:n_in + n_cast + n_out]
    cast_out = refs[n_in + n_cast + n_out:n_in + 2 * n_cast + n_out]
    scratch = refs[n_in + 2 * n_cast + n_out:]
    x_ref, h_ref, oa_ref, ob_ref, oc_ref, od_ref, wg_ref, wb_ref, wo_ref, nf_ref = ins[:10]
    _cast_slices(cast_in, cast_out)
    h = h_ref[...]
    acc = jnp.zeros(x_ref.shape, F32)
    for i, o_ref in enumerate((oa_ref, ob_ref, oc_ref, od_ref)):
        gate = _sigmoid(_dot_nt(h, wg_ref[i * D_MODEL:(i + 1) * D_MODEL, :]))
        acc = acc + gate * jnp.dot(o_ref[...], wb_ref[i], preferred_element_type=F32)
    x1 = x_ref[...] + jnp.dot(acc.astype(BF16), wo_ref[...], preferred_element_type=F32)
    ms = jnp.mean(x1 * x1, axis=-1, keepdims=True)
    hf = x1 * lax.rsqrt(ms + NORM_EPS) * nf_ref[...]
    if route:
        _route(tm, hf, ins[10], *outs[1:], *scratch)
    else:
        outs[1][...] = hf.astype(BF16)
    outs[0][...] = x1


def _merge(x2, h, outs, wg, wb, wo, nf, wr_t=None, cast=(), tm=512):
    n_tok = x2.shape[0]
    route = wr_t is not None
    steps = n_tok // tm
    row = lambda n: pl.BlockSpec((tm, n), lambda i: (i, 0))
    ins = [x2, h, *outs, wg, wb, wo, nf]
    in_specs = ([row(D_MODEL), row(D_MODEL)] + [row(BRANCH_DIM)] * 4
                + [_resident(wg.shape), _resident(wb.shape), _resident(wo.shape), _resident(nf.shape)])
    out_specs = [row(D_MODEL)]
    out_shape = [jax.ShapeDtypeStruct((n_tok, D_MODEL), F32)]
    scratch = []
    if route:
        ins.append(wr_t)
        in_specs.append(_resident(wr_t.shape))
        col = pl.BlockSpec((2, tm), lambda i: (0, i))
        out_specs += [col, col, col, pl.BlockSpec((N_EXPERTS, LANES), lambda i: (0, 0)),
                      pl.BlockSpec((ROW_SPLIT, tm, SUBROW), lambda i: (0, i, 0))]
        out_shape += [jax.ShapeDtypeStruct((2, n_tok), jnp.int32), jax.ShapeDtypeStruct((2, n_tok), F32),
                      jax.ShapeDtypeStruct((2, n_tok), jnp.int32), jax.ShapeDtypeStruct((N_EXPERTS, LANES), F32),
                      jax.ShapeDtypeStruct((ROW_SPLIT, n_tok, SUBROW), jnp.uint32)]
        scratch = [pltpu.VMEM((N_EXPERTS, LANES), F32)]
    else:
        out_specs.append(row(D_MODEL))
        out_shape.append(jax.ShapeDtypeStruct((n_tok, D_MODEL), BF16))
    riders = [_cast_rider(w, steps) for w in cast]
    res = pl.pallas_call(
        functools.partial(_merge_kernel, tm, route, len(cast)),
        grid=(steps,),
        in_specs=in_specs + [r[1] for r in riders],
        out_specs=out_specs + [r[1] for r in riders],
        out_shape=out_shape + [r[2] for r in riders],
        scratch_shapes=scratch,
        compiler_params=_params("arbitrary" if route else "parallel"),
        name="merge",
    )(*ins, *[r[0] for r in riders])
    n_plain = len(res) - len(cast)
    return list(res[:n_plain]) + [wb16.reshape(w.shape) for wb16, w in zip(res[n_plain:], cast)]


def _ffn_kernel(tf, x1_ref, hf_ref, w1_ref, w3_ref, w2_ref, o_ref):
    hf = hf_ref[...]
    acc = x1_ref[...]
    for lo in range(0, FFN_DIM, tf):
        cols = slice(lo, min(lo + tf, FFN_DIM))
        a = jnp.dot(hf, w1_ref[:, cols], preferred_element_type=F32)
        b = jnp.dot(hf, w3_ref[:, cols], preferred_element_type=F32)
        acc = acc + jnp.dot((_silu(a) * b).astype(BF16), w2_ref[cols, :], preferred_element_type=F32)
    o_ref[...] = acc


def _ffn(x1, hf, w1, w3, w2, tm=512, tf=512):
    n_tok = x1.shape[0]
    row = pl.BlockSpec((tm, D_MODEL), lambda i: (i, 0))
    return pl.pallas_call(
        functools.partial(_ffn_kernel, tf),
        grid=(n_tok // tm,),
        in_specs=[row, row, _resident(w1.shape), _resident(w3.shape), _resident(w2.shape)],
        out_specs=row,
        out_shape=jax.ShapeDtypeStruct((n_tok, D_MODEL), F32),
        compiler_params=_params("parallel"),
        name="ffn",
    )(x1, hf, w1, w3, w2)


def _gather_rows(table, idx, window=128):
    split, n_table, width = table.shape
    n = idx.shape[0]
    flat_idx = (idx[None, :] + n_table * jnp.arange(split, dtype=jnp.int32)[:, None]).reshape(-1)
    return _gather_subrows(table.reshape(split * n_table, width), flat_idx, window).reshape(split, n, width)


def _gather_subrows(table, idx, window):
    n = idx.shape[0]
    d = table.shape[1]
    mesh = plsc.VectorSubcoreMesh(core_axis_name="core", subcore_axis_name="subcore")

    @functools.partial(pl.kernel, out_type=jax.ShapeDtypeStruct((n, d), table.dtype), mesh=mesh,
                       name="gather_rows")
    def gather(table_hbm, idx_hbm, out_hbm):
        def body(idx_vmem, out_vmem):
            pltpu.sync_copy(table_hbm.at[idx_vmem.at[0]], out_vmem)

        pltpu.emit_pipeline(
            body,
            grid=(n // window,),
            in_specs=[pl.BlockSpec((1, window), index_map=lambda i: (0, i))],
            out_specs=[pl.BlockSpec((window, d), index_map=lambda i: (i, 0))],
            core_axis_name=("core", "subcore"),
            dimension_semantics=(pltpu.PARALLEL,),
        )(idx_hbm, out_hbm)

    return gather(table, idx.reshape(1, n))


def _scatter_rows(table, dest, n_out, window=128):
    split, n_table, width = table.shape
    n = dest.shape[0]
    flat_dest = (dest[None, :] + n_out * jnp.arange(split, dtype=jnp.int32)[:, None]).reshape(1, split * n)
    src_blocks, blocks = n_table // window, n // window
    mesh = plsc.VectorSubcoreMesh(core_axis_name="core", subcore_axis_name="subcore")

    @functools.partial(pl.kernel, out_type=jax.ShapeDtypeStruct((split * n_out, width), table.dtype), mesh=mesh,
                       name="scatter_rows")
    def scatter(table_hbm, dest_hbm, out_hbm):
        def body(rows_vmem, dest_vmem):
            pltpu.sync_copy(rows_vmem, out_hbm.at[dest_vmem.at[0]])

        pltpu.emit_pipeline(
            body,
            grid=(split * blocks,),
            in_specs=[pl.BlockSpec((window, width),
                                   index_map=lambda i: ((i // blocks) * src_blocks + (i % blocks) % src_blocks, 0)),
                      pl.BlockSpec((1, window), index_map=lambda i: (0, i))],
            out_specs=[],
            core_axis_name=("core", "subcore"),
            dimension_semantics=(pltpu.PARALLEL,),
        )(table_hbm, dest_hbm)

    return scatter(table.reshape(split * n_table, width), flat_dest).reshape(split, n_out, width)


def _expert_kernel(tf, ge_ref, nv_ref, x_ref, w1_ref, w3_ref, w2_ref, y_ref):
    g = pl.program_id(0)

    @pl.when(g < nv_ref[0])
    def _():
        x = _unpack_rows([x_ref[j] for j in range(ROW_SPLIT)]).astype(BF16)
        acc = jnp.zeros((MOE_ROWS, D_MODEL), F32)
        for lo in range(0, EXPERT_DIM, tf):
            a = jnp.dot(x, w1_ref[:, lo:lo + tf], preferred_element_type=F32)
            b = jnp.dot(x, w3_ref[:, lo:lo + tf], preferred_element_type=F32)
            acc = acc + jnp.dot((_silu(a) * b).astype(BF16), w2_ref[lo:lo + tf, :], preferred_element_type=F32)
        for j, words in enumerate(_pack_rows(acc)):
            y_ref[j] = words

    @pl.when(g >= nv_ref[0])
    def _():
        y_ref[...] = jnp.zeros_like(y_ref)


def _experts(xg, group_expert, n_valid, w1, w3, w2, tf=512):
    n_rows = xg.shape[1]
    rows = pl.BlockSpec((ROW_SPLIT, MOE_ROWS, SUBROW), lambda g, ge, nv: (0, g, 0))
    return pl.pallas_call(
        functools.partial(_expert_kernel, tf),
        grid_spec=pltpu.PrefetchScalarGridSpec(
            num_scalar_prefetch=2,
            grid=(n_rows // MOE_ROWS,),
            in_specs=[rows,
                      pl.BlockSpec((None, D_MODEL, EXPERT_DIM), lambda g, ge, nv: (ge[g], 0, 0)),
                      pl.BlockSpec((None, D_MODEL, EXPERT_DIM), lambda g, ge, nv: (ge[g], 0, 0)),
                      pl.BlockSpec((None, EXPERT_DIM, D_MODEL), lambda g, ge, nv: (ge[g], 0, 0))],
            out_specs=rows,
        ),
        out_shape=jax.ShapeDtypeStruct((ROW_SPLIT, n_rows, SUBROW), jnp.uint32),
        compiler_params=_params("arbitrary", vmem_limit=BIG_VMEM_LIMIT),
        name="experts",
    )(group_expert, n_valid, xg, w1, w3, w2)


def _combine_kernel(x1_ref, y0_ref, y1_ref, gate_ref, o_ref):
    gate = gate_ref[...]
    y0 = _unpack_rows([y0_ref[j] for j in range(ROW_SPLIT)])
    y1 = _unpack_rows([y1_ref[j] for j in range(ROW_SPLIT)])
    o_ref[...] = x1_ref[...] + gate[:, 0:1] * y0 + gate[:, 1:2] * y1


def _combine(x1, yg, gate_t, tm=512):
    n_tok = x1.shape[0]
    nb = n_tok // tm
    return pl.pallas_call(
        _combine_kernel,
        grid=(nb,),
        in_specs=[pl.BlockSpec((tm, D_MODEL), lambda i: (i, 0)),
                  pl.BlockSpec((ROW_SPLIT, tm, SUBROW), lambda i: (0, i, 0)),
                  pl.BlockSpec((ROW_SPLIT, tm, SUBROW), lambda i: (0, i + nb, 0)),
                  pl.BlockSpec((tm, 2), lambda i: (i, 0))],
        out_specs=pl.BlockSpec((tm, D_MODEL), lambda i: (i, 0)),
        out_shape=jax.ShapeDtypeStruct((n_tok, D_MODEL), F32),
        compiler_params=_params("parallel"),
        name="moe_combine",
    )(x1, yg, yg, gate_t)


def _moe(x1, routing, w1, w3, w2):
    n_tok = x1.shape[0]
    idx, gate, rank, cnt, hf = routing
    counts = cnt[:, 0].astype(jnp.int32)
    padded = (counts + MOE_ROWS - 1) // MOE_ROWS * MOE_ROWS
    end_padded = jnp.cumsum(padded)
    start_padded = end_padded - padded
    start = sum(jnp.where(idx == e, start_padded[e], 0) for e in range(N_EXPERTS))
    dest = (start + rank).reshape(-1)
    n_groups = (n_tok * 2 + MOE_ROWS - 1) // MOE_ROWS + N_EXPERTS
    n_rows = n_groups * MOE_ROWS
    group_row = jnp.arange(n_groups, dtype=jnp.int32)[:, None] * MOE_ROWS
    group_expert = jnp.minimum(jnp.sum(group_row >= end_padded[None, :], axis=1), N_EXPERTS - 1).astype(jnp.int32)
    n_valid = (end_padded[-1:] // MOE_ROWS).astype(jnp.int32)
    fill = jnp.arange(MOE_ROWS, dtype=jnp.int32)[None, :]
    pad_dest = jnp.where(fill < (padded - counts)[:, None], (start_padded + counts)[:, None] + fill,
                         n_rows - MOE_ROWS + fill).reshape(-1)
    xg = _scatter_rows(hf, jnp.concatenate([dest, pad_dest]), n_rows)
    y = _experts(xg, group_expert, n_valid, w1, w3, w2)
    return _combine(x1, _gather_rows(y, dest), gate.T)


def _row(v, pad=0):
    v = v.reshape(1, -1).astype(F32)
    return jnp.pad(v, ((0, 0), (0, pad))) if pad else v


def _rows_at(w, start, total):
    return jnp.pad(w.astype(F32), ((start, total - start - w.shape[0]), (0, 0)))


def kernel(x, w_in, norm_mix, rwkv_mu, rwkv_decay_up, rwkv_w0, rwkv_a_up, rwkv_a0, rwkv_gate_up, rwkv_k_k, rwkv_k_a, rwkv_r_k, rwkv_ln_w, rwkv_ln_b, vres_down, vres_up, vres_v0, ssm_conv_w, ssm_conv_b, ssm_dt_bias, ssm_a_log, ssm_d, ssm_norm_w, att_q_gain, att_k_gain, att_rel_bias, gla_gate_up, gla_gate_bias, gla_norm_w, w_branch, w_out, norm_ffn, ffn_w1, ffn_w3, ffn_w2, moe_router, moe_w1, moe_w3, moe_w2):
    bsz, seq, _ = x.shape
    n_tok = bsz * seq
    depth = w_in.shape[0]
    x2 = x.reshape(n_tok, D_MODEL)
    v_first = None
    for l in range(depth):
        n_cols = w_in.shape[2]
        w_t = jnp.swapaxes(w_in, 1, 2).reshape(depth * n_cols, D_MODEL)
        patches = jnp.zeros((2, PACK_ROWS, D_MODEL), F32)
        if l > 0:
            lo = RWKV_COLS % PACK_ROWS
            patches = patches.at[1, lo:lo + 32].set(vres_down[l - 1].T)
        w_mix = _pack_weight_rows(w_t, _mixer_plan(l * n_cols), patches)
        o_gate = RWKV_COLS + SSM_COLS + ATT_COLS + GLA_COLS
        wg = _pack_weight_rows(w_t, [(l * n_cols + r, GATE_PACK_ROWS, 0) for r in range(o_gate, n_cols, GATE_PACK_ROWS)],
                               jnp.zeros((1, GATE_PACK_ROWS, D_MODEL), F32))
        moe = l % 2 == 1
        up_w = [moe_w3[l // 2]] if moe else [ffn_w1[l // 2], ffn_w3[l // 2]]
        h, za, zb, zc, zd, wb_b, wo_b, *up_b = _inproj(x2, _row(norm_mix[l]), w_mix,
                                                       cast=[w_branch[l], w_out[l]] + up_w)
        shp = lambda z: z.reshape(bsz, seq, z.shape[-1])

        rw = dict(mu=_row(rwkv_mu[l], RWKV_PAD - RWKV_COLS),
                  dup=_rows_at(rwkv_decay_up[l], 0, LANES), w0=_row(rwkv_w0[l]),
                  aup=_rows_at(rwkv_a_up[l], 64, LANES), a0=_row(rwkv_a0[l]),
                  gup=_rows_at(rwkv_gate_up[l], 0, 256).astype(BF16),
                  kk=_row(rwkv_k_k[l]), ka=_row(rwkv_k_a[l]), rk=_row(rwkv_r_k[l]),
                  lnw=_row(rwkv_ln_w[l]), lnb=_row(rwkv_ln_b[l]))
        if l > 0:
            rw.update(vup=_rows_at(vres_up[l - 1], 160, 256), v0=_row(vres_v0[l - 1]))
        ssm = dict(cw=ssm_conv_w[l], cb=_row(ssm_conv_b[l]), dtb=_row(ssm_dt_bias[l], LANES - SSM_HEADS),
                   alog=_row(ssm_a_log[l], LANES - SSM_HEADS), dexp=_row(jnp.repeat(ssm_d[l], CHUNK)),
                   nw=_row(ssm_norm_w[l]))
        gla = dict(gup=_rows_at(gla_gate_up[l], 0, LANES), gb=_row(gla_gate_bias[l]), nw=_row(gla_norm_w[l]))
        (o_c,), (o_d,), (o_b,), res_a = _run_parts(
            [_attention_part(shp(zc), att_q_gain[l], att_k_gain[l], att_rel_bias, MIXER_BLOCK),
             _gla_part(shp(zd), gla, MIXER_BLOCK), _mamba_part(shp(zb), ssm, MIXER_BLOCK),
             _rwkv_part(shp(za), v_first, rw, MIXER_BLOCK)],
            grid=(bsz, seq // MIXER_BLOCK), name="mixers")
        o_a = res_a[0]
        if l == 0:
            v_first = res_a[1]
        outs = [o.reshape(n_tok, BRANCH_DIM) for o in (o_a, o_b, o_c, o_d)]
        merge_w = (wg, wb_b, wo_b, _row(norm_ffn[l]))
        if not moe:
            x1, hf, *w1_b = _merge(x2, h, outs, *merge_w, cast=(moe_w1[l // 2],) if l + 1 < depth else ())
            x2 = _ffn(x1, hf, *up_b, ffn_w2[l // 2].astype(BF16))
        else:
            x1, *routing, w2_b = _merge(x2, h, outs, *merge_w, wr_t=moe_router[l // 2].T, cast=(moe_w2[l // 2],))
            x2 = _moe(x1, routing, w1_b[0], up_b[0], w2_b)
    return x2.reshape(bsz, seq, D_MODEL)
```

```python
import functools
from typing import Any, NamedTuple

import jax
import jax.numpy as jnp
from jax import lax
from jax.experimental import pallas as pl
from jax.experimental.pallas import tpu as pltpu
from jax.experimental.pallas import tpu_sc as plsc

F32 = jnp.float32
BF16 = jnp.bfloat16

D_MODEL = 1024
CHUNK = 64
BRANCH_DIM = 512
NORM_EPS = 1e-6
LANES = 128
VMEM_LIMIT = 56 * 1024 * 1024
BIG_VMEM_LIMIT = 61 * 1024 * 1024

LOG2_E = 1.4426950408889634

RWKV_LN_EPS = 64e-5
RWKV_DECAY_SCALE = 0.6065306597126334
RWKV_COLS = 1824
RWKV_PAD = 1920
SSM_COLS = 1544
SSM_PAD = 1664
SSM_HEADS = 8
SSM_STATE = 128
ATT_COLS = 1536
ATT_HEADS = 8
ATT_LEFT = 8 * CHUNK
REL_CLIP = 2 * CHUNK
GLA_COLS = 1552
GLA_PAD = 1664
GLA_GATE_NORM = 16.0
FFN_DIM = 2816
N_EXPERTS = 8
EXPERT_DIM = 3584
MOE_ROWS = 512
ROW_SPLIT = 2
SUBROW = D_MODEL // (2 * ROW_SPLIT)


def _dot(a, b):
    return jnp.dot(a.astype(BF16), b.astype(BF16), preferred_element_type=F32)


def _dot_nt(a, b):
    return lax.dot_general(a.astype(BF16), b.astype(BF16), (((1,), (1,)), ((), ())),
                           preferred_element_type=F32)


def _dot_tn(a, b):
    return lax.dot_general(a.astype(BF16), b.astype(BF16), (((0,), (0,)), ((), ())),
                           preferred_element_type=F32)


def _hi_lo(a):
    hi = a.astype(BF16)
    return hi, (a - hi.astype(F32)).astype(BF16)


def _dot_3x(a, b):
    a_hi, a_lo = _hi_lo(a)
    b_hi, b_lo = _hi_lo(b)
    return (jnp.dot(a_hi, b_hi, preferred_element_type=F32) + jnp.dot(a_lo, b_hi, preferred_element_type=F32)
            + jnp.dot(a_hi, b_lo, preferred_element_type=F32))


def _dot_split(a, m):
    m = m.astype(BF16)
    hi, lo = _hi_lo(a)
    return jnp.dot(hi, m, preferred_element_type=F32) + jnp.dot(lo, m, preferred_element_type=F32)


def _dot_split_l(m, a):
    m = m.astype(BF16)
    hi, lo = _hi_lo(a)
    return jnp.dot(m, hi, preferred_element_type=F32) + jnp.dot(m, lo, preferred_element_type=F32)


def _pack_rows(x):
    bits = pltpu.bitcast(x.astype(BF16).astype(F32), jnp.uint32)
    half = D_MODEL // 2
    return [(bits[:, p * SUBROW:(p + 1) * SUBROW] & jnp.uint32(0xFFFF0000))
            | (bits[:, half + p * SUBROW:half + (p + 1) * SUBROW] >> jnp.uint32(16)) for p in range(ROW_SPLIT)]


def _unpack_rows(planes):
    hi = [pltpu.bitcast(w & jnp.uint32(0xFFFF0000), F32) for w in planes]
    lo = [pltpu.bitcast(w << jnp.uint32(16), F32) for w in planes]
    return jnp.concatenate(hi + lo, axis=-1)


def _softplus(x):
    return jnp.maximum(x, 0.0) + jnp.log(1.0 + jnp.exp(-jnp.abs(x)))


def _sigmoid(x):
    return 1.0 / (1.0 + jnp.exp(-x))


def _silu(x):
    return x * _sigmoid(x)


def _iota2(shape, axis):
    return lax.broadcasted_iota(jnp.int32, shape, axis)


def _group_mean_matrix(n, group):
    r = _iota2((n, n), 0) // group
    c = _iota2((n, n), 1) // group
    return jnp.where(r == c, 1.0 / group, 0.0).astype(F32)


def _chunk_tri(n):
    r = _iota2((n, n), 0)
    c = _iota2((n, n), 1)
    return jnp.where((r // CHUNK == c // CHUNK) & (r >= c), 1.0, 0.0).astype(F32)


def _resident(shape):
    nd = len(shape)
    return pl.BlockSpec(shape, lambda *_: (0,) * nd, pipeline_mode=pl.Buffered(1))


def _params(*sem, vmem_limit=VMEM_LIMIT):
    return pltpu.CompilerParams(dimension_semantics=sem, vmem_limit_bytes=vmem_limit)


_DONE = object()


class _Part(NamedTuple):
    body: Any
    inputs: list
    in_specs: list
    out_shapes: list
    out_specs: list
    scratch_shapes: list


def _run_parts(parts, grid, name):
    def split(refs, counts):
        out, lo = [], 0
        for n in counts:
            out.append(refs[lo:lo + n])
            lo += n
        return out

    n_in = [len(p.inputs) for p in parts]
    n_out = [len(p.out_shapes) for p in parts]
    n_scr = [len(p.scratch_shapes) for p in parts]

    def kernel(*refs):
        ins = split(refs[:sum(n_in)], n_in)
        outs = split(refs[sum(n_in):sum(n_in) + sum(n_out)], n_out)
        scr = split(refs[sum(n_in) + sum(n_out):], n_scr)
        stages = [part.body(*i, *o, *c) for part, i, o, c in zip(parts, ins, outs, scr)]
        while stages:
            for stage in list(stages):
                if next(stage, _DONE) is _DONE:
                    stages.remove(stage)

    flat = lambda field: [x for p in parts for x in getattr(p, field)]
    res = pl.pallas_call(
        kernel,
        grid=grid,
        in_specs=flat("in_specs"),
        out_specs=flat("out_specs"),
        out_shape=flat("out_shapes"),
        scratch_shapes=flat("scratch_shapes"),
        compiler_params=_params("parallel", "arbitrary", vmem_limit=BIG_VMEM_LIMIT),
        name=name,
    )(*flat("inputs"))
    return split(list(res), n_out)


def _cast_rider(stack, index, steps):
    shape = stack.shape[1:]
    cols = shape[-1]
    rows = stack[0].size // cols // steps
    assert stack[0].size // cols % steps == 0 and rows % 16 == 0
    return (stack.reshape(stack.shape[0] * steps, rows, cols),
            pl.BlockSpec((None, rows, cols), lambda i: (index * steps + i, 0, 0)),
            pl.BlockSpec((None, rows, cols), lambda i: (i, 0, 0)),
            jax.ShapeDtypeStruct((steps, rows, cols), BF16), shape)


def _cast_slices(src_refs, dst_refs):
    for src, dst in zip(src_refs, dst_refs):
        dst[...] = src[...].astype(dst.dtype)


MIXER_COLS = (RWKV_PAD, SSM_PAD, ATT_COLS, GLA_PAD)
MIXER_BLOCK = 256


PACK_ROWS = 128
GATE_PACK_ROWS = 512


def _pack_kernel(src_ref, keep_ref, fill_ref, w_ref, patch_ref, o_ref):
    del src_ref, fill_ref
    rows = _iota2(o_ref.shape, 0)
    o_ref[...] = jnp.where(rows < keep_ref[pl.program_id(0)], w_ref[...], patch_ref[...]).astype(o_ref.dtype)


def _pack_weight_rows(w_t, plan, patches):
    src, keep, fill = (jnp.asarray(v, jnp.int32) for v in zip(*plan))
    rows, cols = patches.shape[1:]
    return pl.pallas_call(
        _pack_kernel,
        grid_spec=pltpu.PrefetchScalarGridSpec(
            num_scalar_prefetch=3,
            grid=(len(plan),),
            in_specs=[pl.BlockSpec((pl.Element(rows), pl.Element(cols)),
                                   lambda j, src, keep, fill: (pl.multiple_of(src[j], 8), 0)),
                      pl.BlockSpec((None, rows, cols), lambda j, src, keep, fill: (fill[j], 0, 0))],
            out_specs=pl.BlockSpec((rows, cols), lambda j, src, keep, fill: (j, 0)),
        ),
        out_shape=jax.ShapeDtypeStruct((len(plan) * rows, cols), BF16),
        compiler_params=_params("parallel"),
        name="pack_weights",
    )(src, keep, fill, w_t, patches)


def _mixer_plan(base):
    o_ss, o_at = RWKV_COLS + SSM_COLS, RWKV_COLS + SSM_COLS + ATT_COLS
    plan = []

    def rows(lo, hi, fill=0):
        for r in range(lo, hi, PACK_ROWS):
            plan.append((base + r, min(PACK_ROWS, hi - r), fill))

    rows(0, RWKV_COLS, fill=1)
    rows(RWKV_COLS, o_ss)
    rows(o_ss, o_at)
    rows(o_at, o_at + 1024)
    rows(o_at + 1040, o_at + GLA_COLS)
    rows(o_at + 1024, o_at + 1040)
    assert len(plan) * PACK_ROWS == sum(MIXER_COLS)
    return plan


def _inproj_kernel(n_cast, x_ref, g_ref, w_ref, *refs):
    cast_in, (h_ref, *z_refs), cast_out = refs[:n_cast], refs[n_cast:len(refs) - n_cast], refs[len(refs) - n_cast:]
    _cast_slices(cast_in, cast_out)
    x = x_ref[...]
    ms = jnp.mean(x * x, axis=-1, keepdims=True)
    h = (x * lax.rsqrt(ms + NORM_EPS) * g_ref[...]).astype(BF16)
    h_ref[...] = h
    lo = 0
    for pair in (z_refs[:2], z_refs[2:]):
        widths = [z_ref.shape[-1] for z_ref in pair]
        z = _dot_nt(h, w_ref[lo:lo + sum(widths), :])
        pair[0][...] = z[:, :widths[0]]
        pair[1][...] = z[:, widths[0]:]
        lo += sum(widths)


def _inproj(x2, g, w_mix, cast=(), tm=512):
    n_tok = x2.shape[0]
    row = lambda n: pl.BlockSpec((tm, n), lambda i: (i, 0))
    riders = [_cast_rider(stack, index, n_tok // tm) for stack, index in cast]
    res = pl.pallas_call(
        functools.partial(_inproj_kernel, len(cast)),
        grid=(n_tok // tm,),
        in_specs=[row(D_MODEL), _resident((1, D_MODEL)), _resident(w_mix.shape)] + [r[1] for r in riders],
        out_specs=[row(D_MODEL)] + [row(n) for n in MIXER_COLS] + [r[2] for r in riders],
        out_shape=[jax.ShapeDtypeStruct((n_tok, D_MODEL), BF16)]
        + [jax.ShapeDtypeStruct((n_tok, n), F32) for n in MIXER_COLS] + [r[3] for r in riders],
        compiler_params=_params("parallel", vmem_limit=BIG_VMEM_LIMIT if cast else VMEM_LIMIT),
        name="inproj",
    )(x2, g, w_mix, *[r[0] for r in riders])
    n_plain = len(res) - len(cast)
    return list(res[:n_plain]) + [wb16.reshape(r[4]) for wb16, r in zip(res[n_plain:], riders)]


def _rwkv_block(r, k, v, kk, ka, ld, tb, state_ref, y_ref):
    n_chunks = tb // CHUNK
    cum = _dot_split_l(_chunk_tri(tb), ld)
    p_inv = jnp.exp(-cum)
    a_t = -kk * jnp.exp(cum - ld)
    b_t = kk * ka * p_inv
    k_t = k * p_inv
    r_t = r * jnp.exp(cum)
    bf = lambda z: z.astype(BF16)
    a_b, b_b, k_b, r_b, v_b = bf(a_t), bf(b_t), bf(k_t), bf(r_t), bf(v)
    yield

    lane_head = _iota2((CHUNK, LANES), 1) // CHUNK
    row = _iota2((CHUNK, LANES), 0)
    col = _iota2((CHUNK, LANES), 1) % CHUNK
    strict = row > col
    incl = row >= col
    bd_mask = (_iota2((LANES, LANES), 0) // CHUNK) == (_iota2((LANES, LANES), 1) // CHUNK)
    eye = jnp.where(_iota2((LANES, LANES), 0) == _iota2((LANES, LANES), 1), 1.0, 0.0).astype(F32)

    def stack(z):
        return jnp.concatenate([jnp.where(lane_head == 0, z, 0.0), jnp.where(lane_head == 1, z, 0.0)], axis=0)

    def sub(z, u):
        c, p = u
        return z[c * CHUNK:(c + 1) * CHUNK, p * LANES:(p + 1) * LANES]

    def prepare(units):
        gram = {u: _dot_nt(jnp.concatenate([sub(a_b, u), sub(r_b, u)], axis=0),
                           jnp.concatenate([stack(sub(b_b, u)), stack(sub(k_b, u))], axis=0))
                for u in units}
        yield
        a_rb = {u: bf(jnp.where(incl, gram[u][CHUNK:, :LANES], 0.0)) for u in units}
        a_akrk = {u: bf(jnp.concatenate([jnp.where(strict, gram[u][:CHUNK, LANES:], 0.0),
                                         jnp.where(incl, gram[u][CHUNK:, LANES:], 0.0)], axis=0)) for u in units}
        pw = {u: stack(jnp.where(strict, gram[u][:CHUNK, :LANES], 0.0)) for u in units}
        t_bd = {u: eye + pw[u] for u in units}
        pw = {u: _dot(pw[u], pw[u]) for u in units}
        yield
        for _ in range(CHUNK.bit_length() - 3):
            sp = {u: _dot(jnp.concatenate([t_bd[u], pw[u]], axis=0), pw[u]) for u in units}
            t_bd = {u: t_bd[u] + sp[u][:LANES] for u in units}
            pw = {u: sp[u][LANES:] for u in units}
            yield
        t_bd = {u: bf(t_bd[u] + _dot(t_bd[u], pw[u])) for u in units}
        v_bd = {u: stack(sub(v_b, u)) for u in units}
        avyv = {u: _dot(a_akrk[u], v_bd[u]) for u in units}
        yield
        wuv = {u: _dot(t_bd[u], jnp.concatenate([stack(sub(a_b, u)), stack(bf(avyv[u][:CHUNK]))], axis=1))
               for u in units}
        wr = {u: jnp.concatenate([bf(wuv[u][:, :LANES]), sub(r_b, u)], axis=0) for u in units}
        yield
        return a_rb, avyv, wuv, wr

    a_rb, avyv, wuv, wr = yield from prepare([(c, p) for c in range(n_chunks) for p in range(4)])

    h = [state_ref[p] for p in range(4)]
    for c in range(n_chunks):
        last = c * CHUNK + CHUNK - 1
        p_last = jnp.exp(cum[last:last + 1, :])
        hs = [_dot_nt(wr[(c, p)], h[p]) for p in range(4)]
        u2 = [hs[p][:LANES] + wuv[(c, p)][:, LANES:] for p in range(4)]
        u_p = [u2[p][:CHUNK] + u2[p][CHUNK:] for p in range(4)]
        upd = [_dot_tn(jnp.concatenate([u_p[p], sub(v, (c, p))], axis=0),
                       jnp.concatenate([sub(b_t, (c, p)), sub(k_t, (c, p))], axis=0)
                       * p_last[:, p * LANES:(p + 1) * LANES]) for p in range(4)]
        for p in range(4):
            y_ref[c * CHUNK:(c + 1) * CHUNK, p * LANES:(p + 1) * LANES] = (
                hs[p][LANES:] + _dot(a_rb[(c, p)], stack(u_p[p])) + avyv[(c, p)][CHUNK:])
        h = [h[p] * p_last[:, p * LANES:(p + 1) * LANES] + jnp.where(bd_mask, upd[p], 0.0) for p in range(4)]
        yield
    for p in range(4):
        state_ref[p] = h[p]


def _rwkv_kernel(has_vres, tb, *refs):
    if has_vres:
        (za_ref, vfirst_ref, mu_ref, dup_ref, w0_ref, aup_ref, a0_ref, gup_ref, vup_ref, v0_ref,
         kk_ref, ka_ref, rk_ref, lnw_ref, lnb_ref, o_ref, prev_ref, state_ref, y_ref) = refs
    else:
        (za_ref, mu_ref, dup_ref, w0_ref, aup_ref, a0_ref, gup_ref,
         kk_ref, ka_ref, rk_ref, lnw_ref, lnb_ref, o_ref, vraw_ref, prev_ref, state_ref, y_ref) = refs

    @pl.when(pl.program_id(1) == 0)
    def _():
        prev_ref[...] = jnp.zeros_like(prev_ref)
        state_ref[...] = jnp.zeros_like(state_ref)

    yield
    za = za_ref[...]
    shifted = pltpu.roll(za, 1, axis=0)
    shifted = jnp.where(_iota2(za.shape, 0) == 0, prev_ref[7:8, :], shifted)
    prev_ref[...] = za[tb - 8:, :]
    za = za + (shifted - za) * mu_ref[...]

    r = za[:, 0:512]
    k = za[:, 512:1024]
    v = za[:, 1024:1536]
    xwa = za[:, 1536:1664]
    xg = za[:, 1664:1920]
    ld = -RWKV_DECAY_SCALE * _sigmoid(w0_ref[...] + _dot_3x(jnp.tanh(xwa), dup_ref[...]))
    a = _sigmoid(a0_ref[...] + _dot(xwa, aup_ref[...]))
    g = _dot(_sigmoid(xg), gup_ref[...])
    if has_vres:
        v_mix = _sigmoid(v0_ref[...] + _dot(xg, vup_ref[...]))
        v = v + (vfirst_ref[...] - v) * v_mix
    else:
        vraw_ref[...] = v
    kk = k * kk_ref[...]
    k = k * (1.0 + (a - 1.0) * ka_ref[...])
    head_sum = _group_mean_matrix(512, CHUNK) * float(CHUNK)
    kk = kk / jnp.maximum(jnp.sqrt(_dot(kk * kk, head_sum)), 1e-12)
    yield

    yield from _rwkv_block(r, k, v, kk, a, ld, tb, state_ref, y_ref)
    y = y_ref[...]
    head_mean = _group_mean_matrix(512, CHUNK)
    mean = _dot_split(y, head_mean)
    yc = y - mean
    var = _dot(yc * yc, head_mean)
    y = yc * lax.rsqrt(var + RWKV_LN_EPS) * lnw_ref[...] + lnb_ref[...]
    y = y + _dot(r * k * rk_ref[...], head_sum) * v
    o_ref[...] = (y * g).astype(o_ref.dtype)


def _rwkv_part(za, v_first, prm, tb):
    bsz, seq, _ = za.shape
    has_vres = v_first is not None
    blk = lambda n: pl.BlockSpec((None, tb, n), lambda b, i: (b, i, 0))
    names = (["mu", "dup", "w0", "aup", "a0", "gup"] + (["vup", "v0"] if has_vres else [])
             + ["kk", "ka", "rk", "lnw", "lnb"])
    weights = [prm[n] for n in names]
    ins = [za] + ([v_first] if has_vres else []) + weights
    in_specs = [blk(RWKV_PAD)] + ([blk(512)] if has_vres else []) + [_resident(w.shape) for w in weights]
    out_shape = [jax.ShapeDtypeStruct((bsz, seq, 512), BF16)]
    out_specs = [blk(512)]
    if not has_vres:
        out_shape.append(jax.ShapeDtypeStruct((bsz, seq, 512), F32))
        out_specs.append(blk(512))
    return _Part(functools.partial(_rwkv_kernel, has_vres, tb), ins, in_specs, out_shape, out_specs,
                 [pltpu.VMEM((8, RWKV_PAD), F32), pltpu.VMEM((4, LANES, LANES), F32), pltpu.VMEM((tb, 512), F32)])


def _mamba_kernel(tb, zb_ref, cw_ref, cb_ref, dtb_ref, alog_ref, dexp_ref, nw_ref,
                  o_ref, xbuf_ref, state_ref, y_ref):
    @pl.when(pl.program_id(1) == 0)
    def _():
        xbuf_ref[0:8, :] = jnp.zeros((8, 1024), F32)
        state_ref[...] = jnp.zeros_like(state_ref)

    yield
    xbuf_ref[8:8 + tb, :] = zb_ref[:, 512:1536]
    conv = cb_ref[...]
    for i in range(4):
        conv = conv + cw_ref[i:i + 1, :] * xbuf_ref[5 + i:5 + i + tb, :]
    xbuf_ref[0:8, :] = xbuf_ref[tb:tb + 8, :]
    xbc = _silu(conv)
    xs = xbc[:, 0:512]
    bm = xbc[:, 512:768]
    cm = xbc[:, 768:1024]
    yield

    lane = _iota2((1, LANES), 1)
    dt = _softplus(zb_ref[:, 1536:1664] + dtb_ref[...])
    a_neg = jnp.where(lane < SSM_HEADS, -jnp.exp(alog_ref[...]), 0.0)
    acs = _dot_split_l(_chunk_tri(tb), dt * a_neg)
    expand = jnp.where(_iota2((LANES, 512), 0) == _iota2((LANES, 512), 1) // CHUNK, 1.0, 0.0)
    dt_e = _dot(dt, expand)
    acs_e = _dot_split(acs, expand)
    xdt = xs * dt_e
    yield

    n_chunks = tb // CHUNK
    causal = _iota2((CHUNK, 512), 0) >= _iota2((CHUNK, 512), 1) % CHUNK
    pair_head = _iota2((CHUNK, LANES), 1) // CHUNK
    spread = jnp.where(_iota2((CHUNK, 512), 0) == _iota2((CHUNK, 512), 1) % CHUNK, 1.0, 0.0)
    own_head = _iota2((SSM_HEADS, 512), 0) == _iota2((SSM_HEADS, 512), 1) // CHUNK
    rows_of = lambda c: slice(c * CHUNK, (c + 1) * CHUNK)
    grp = lambda g: slice(g * SSM_STATE, (g + 1) * SSM_STATE)

    def stack(z):
        return jnp.concatenate([jnp.where(pair_head == 0, z, 0.0), jnp.where(pair_head == 1, z, 0.0)], axis=0)

    y_intra, upd, e_in, e_last = [], [], [], []
    for c in range(n_chunks):
        rows = rows_of(c)
        acs_ec = acs_e[rows]
        acs_t = acs[rows].T[0:SSM_HEADS]
        acs_row = jnp.sum(jnp.where(own_head, _dot_split(acs_t, spread), 0.0), axis=0, keepdims=True)
        decay = jnp.exp(jnp.where(causal, acs_ec - acs_row, -jnp.inf))
        cb = jnp.concatenate(
            [_dot_nt(cm[rows, grp(g)], jnp.concatenate([bm[rows, grp(g)]] * 4, axis=0)) for g in range(2)], axis=1)
        m = cb * decay
        yield
        xdt_c = xdt[rows]
        y_intra.append([_dot(m[:, p * LANES:(p + 1) * LANES], stack(xdt_c[:, p * LANES:(p + 1) * LANES]))
                        for p in range(4)])
        last_e = acs_ec[CHUNK - 1:CHUNK, :]
        x_out = xdt_c * jnp.exp(last_e - acs_ec)
        upd.append([_dot_tn(bm[rows, grp(g)], x_out[:, g * 256:(g + 1) * 256]) for g in range(2)])
        e_in.append(jnp.exp(acs_ec))
        e_last.append(jnp.exp(last_e))
        yield

    state = [state_ref[g] for g in range(2)]
    for c in range(n_chunks):
        rows = rows_of(c)
        for g in range(2):
            ls = slice(g * 256, (g + 1) * 256)
            y_in = jnp.concatenate(y_intra[c][2 * g:2 * g + 2], axis=1)
            y_ref[rows, ls] = y_in + _dot(cm[rows, grp(g)], state[g]) * e_in[c][:, ls]
            state[g] = state[g] * e_last[c][:, ls] + upd[c][g]
        yield
    for g in range(2):
        state_ref[g] = state[g]

    y = (y_ref[...] + xs * dexp_ref[...]) * _silu(zb_ref[:, 0:512])
    for g in range(2):
        ls = slice(g * 256, (g + 1) * 256)
        yg = y[:, ls]
        ms = jnp.mean(yg * yg, axis=-1, keepdims=True)
        o_ref[:, ls] = (yg * lax.rsqrt(ms + NORM_EPS) * nw_ref[:, ls]).astype(o_ref.dtype)


def _mamba_part(zb, prm, tb):
    bsz, seq, _ = zb.shape
    blk = lambda n: pl.BlockSpec((None, tb, n), lambda b, i: (b, i, 0))
    weights = [prm[n] for n in ("cw", "cb", "dtb", "alog", "dexp", "nw")]
    return _Part(functools.partial(_mamba_kernel, tb), [zb] + weights,
                 [blk(SSM_PAD)] + [_resident(w.shape) for w in weights],
                 [jax.ShapeDtypeStruct((bsz, seq, 512), BF16)], [blk(512)],
                 [pltpu.VMEM((tb + 8, 1024), F32), pltpu.VMEM((2, SSM_STATE, 256), F32), pltpu.VMEM((tb, 512), F32)])


def _attn_kernel(tq, qkv_ref, qg_ref, kg_ref, bias_ref, o_ref, kn_ref, vb_ref):
    i = pl.program_id(1)
    head_mean = _group_mean_matrix(512, CHUNK)
    win = tq + ATT_LEFT

    @pl.when(i == 0)
    def _():
        kn_ref[0:ATT_LEFT, :] = jnp.zeros((ATT_LEFT, 512), BF16)
        vb_ref[0:ATT_LEFT, :] = jnp.zeros((ATT_LEFT, 512), BF16)

    yield
    start = pl.multiple_of(i * tq, tq)
    k = qkv_ref[:, 512:1024]
    ms = _dot(k * k, head_mean)
    kn_ref[pl.ds(ATT_LEFT + start, tq), :] = (k * lax.rsqrt(ms + NORM_EPS) * kg_ref[...]).astype(BF16)
    vb_ref[pl.ds(ATT_LEFT + start, tq), :] = qkv_ref[:, 1024:1536].astype(BF16)
    q = qkv_ref[:, 0:512]
    ms = _dot(q * q, head_mean)
    qn = q * lax.rsqrt(ms + NORM_EPS) * qg_ref[...] * (CHUNK ** -0.5 * LOG2_E)
    kwin = kn_ref[pl.ds(start, win), :]
    vwin = vb_ref[pl.ds(start, win), :]
    lane_head = _iota2((tq, LANES), 1) // CHUNK
    yield
    for p in range(4):
        ls = slice(p * LANES, (p + 1) * LANES)
        q2 = jnp.concatenate([jnp.where(lane_head == s, qn[:, ls], 0.0) for s in range(2)], axis=0)
        sc = _dot_nt(q2, kwin[:, ls]) + bias_ref[2 * p:2 * p + 2].reshape(2 * tq, win).astype(F32)
        yield
        e = jnp.exp2(sc - jnp.max(sc, axis=-1, keepdims=True))
        yield
        o2 = _dot(e, vwin[:, ls]) / jnp.sum(e, axis=-1, keepdims=True)
        o_ref[:, ls] = jnp.where(lane_head == 0, o2[:tq], o2[tq:]).astype(o_ref.dtype)
        yield


def _band_bias(rel_bias, tq, win):
    period = tq + win
    m = jnp.arange(period)
    d = jnp.where(m < win, m, m - period) - ATT_LEFT
    f = (rel_bias.astype(F32)[:, jnp.clip(d, -REL_CLIP, REL_CLIP) + REL_CLIP] * LOG2_E).astype(BF16)
    g = jnp.tile(f, (1, tq))[:, :tq * (period - 1)].reshape(-1, tq, period - 1)
    bias = g[:, :, :win]
    left = ATT_LEFT // CHUNK
    qc = left + jnp.arange(tq)[:, None] // CHUNK
    kc = jnp.arange(win)[None, :] // CHUNK
    tiles = []
    for blk in range(ATT_LEFT // tq + 1):
        first_kc = jnp.maximum(qc - left, left - blk * (tq // CHUNK))
        tiles.append(jnp.where((kc <= qc) & (kc >= first_kc), bias, -jnp.inf))
    return jnp.stack(tiles)


def _attention_part(zc, q_gain, k_gain, rel_bias, tq):
    bsz, seq, _ = zc.shape
    win = tq + ATT_LEFT
    bias = _band_bias(rel_bias, tq, win)
    qg = jnp.tile(q_gain, ATT_HEADS)[None, :]
    kg = jnp.tile(k_gain, ATT_HEADS)[None, :]
    last_tile = bias.shape[0] - 1
    bias_spec = pl.BlockSpec((None,) + bias.shape[1:], lambda b, i: (jnp.minimum(i, last_tile), 0, 0, 0))
    blk = lambda n: pl.BlockSpec((None, tq, n), lambda b, i: (b, i, 0))
    return _Part(functools.partial(_attn_kernel, tq), [zc, qg, kg, bias],
                 [blk(ATT_COLS), _resident(qg.shape), _resident(kg.shape), bias_spec],
                 [jax.ShapeDtypeStruct((bsz, seq, 512), BF16)], [blk(512)],
                 [pltpu.VMEM((ATT_LEFT + seq, 512), BF16), pltpu.VMEM((ATT_LEFT + seq, 512), BF16)])


def _gla_kernel(tb, zd_ref, gup_ref, gb_ref, nw_ref, o_ref, state_ref, y_ref):
    @pl.when(pl.program_id(1) == 0)
    def _():
        state_ref[...] = jnp.zeros_like(state_ref)

    yield
    q = zd_ref[:, 0:256] * (CHUNK ** -0.5)
    k = zd_ref[:, 256:512]
    v = zd_ref[:, 512:1024]
    log_a = -_softplus(-(_dot_3x(zd_ref[:, 1536:1664], gup_ref[...]) + gb_ref[...])) / GLA_GATE_NORM
    bcum = _dot_split_l(_chunk_tri(tb), log_a)

    k_head = _iota2((CHUNK, 256), 1) // CHUNK
    v_head = _iota2((CHUNK, 512), 1) // LANES
    causal = _iota2((CHUNK, 256), 0) >= _iota2((CHUNK, 256), 1) % CHUNK
    bd = (_iota2((512, 256), 0) // LANES) == (_iota2((512, 256), 1) // CHUNK)
    n_chunks = tb // CHUNK
    qg_all = (q * jnp.exp(bcum)).astype(BF16)
    kg_all = k * jnp.exp(-bcum)
    yield
    qg, att, o_intra, upd, e_last = [], [], [], [], []
    for c in range(n_chunks):
        rows = slice(c * CHUNK, (c + 1) * CHUNK)
        kg_bd = jnp.concatenate([jnp.where(k_head == h, kg_all[rows], 0.0) for h in range(4)], axis=0)
        qg.append(qg_all[rows])
        att.append(jnp.where(causal, _dot_nt(qg[c], kg_bd), 0.0))
        yield
    for c in range(n_chunks):
        rows = slice(c * CHUNK, (c + 1) * CHUNK)
        bc = bcum[rows]
        blast = bc[CHUNK - 1:CHUNK, :]
        v_c = v[rows]
        v_bd = jnp.concatenate([jnp.where(v_head == h, v_c, 0.0) for h in range(4)], axis=0)
        o_intra.append(_dot(att[c], v_bd))
        upd.append(jnp.where(bd, _dot_tn(v_c, k[rows] * jnp.exp(blast - bc)), 0.0))
        e_last.append(jnp.exp(blast))
        yield
    st = state_ref[...]
    for c in range(n_chunks):
        y_ref[c * CHUNK:(c + 1) * CHUNK, :] = o_intra[c] + _dot_nt(qg[c], st)
        st = st * e_last[c] + upd[c]
        yield
    state_ref[...] = st

    o = y_ref[...]
    for h in range(4):
        ls = slice(h * LANES, (h + 1) * LANES)
        oh = o[:, ls]
        ms = jnp.mean(oh * oh, axis=-1, keepdims=True)
        o_ref[:, ls] = (oh * lax.rsqrt(ms + NORM_EPS) * nw_ref[...]
                        * _silu(zd_ref[:, 1024 + h * LANES:1024 + (h + 1) * LANES])).astype(o_ref.dtype)


def _gla_part(zd, prm, tb):
    bsz, seq, _ = zd.shape
    blk = lambda n: pl.BlockSpec((None, tb, n), lambda b, i: (b, i, 0))
    weights = [prm[n] for n in ("gup", "gb", "nw")]
    return _Part(functools.partial(_gla_kernel, tb), [zd] + weights,
                 [blk(GLA_PAD)] + [_resident(w.shape) for w in weights],
                 [jax.ShapeDtypeStruct((bsz, seq, 512), BF16)], [blk(512)],
                 [pltpu.VMEM((512, 256), F32), pltpu.VMEM((tb, 512), F32)])


def _route(tm, hf, wr_ref, idx_ref, gate_ref, rank_ref, cnt_ref, hf_ref, carry_ref):
    @pl.when(pl.program_id(0) == 0)
    def _():
        carry_ref[...] = jnp.zeros_like(carry_ref)

    for j, words in enumerate(_pack_rows(hf)):
        hf_ref[j] = words
    nt = lambda a, b: lax.dot_general(a, b, (((1,), (1,)), ((), ())), preferred_element_type=F32)
    w_hi, w_lo = _hi_lo(wr_ref[...])
    h_hi, h_lo = _hi_lo(hf)
    logits = nt(w_hi, h_hi) + nt(w_lo, h_hi) + nt(w_hi, h_lo)
    e_iota = _iota2((N_EXPERTS, tm), 0)
    m1 = jnp.max(logits, axis=0, keepdims=True)
    i1 = jnp.min(jnp.where(logits == m1, e_iota, N_EXPERTS), axis=0, keepdims=True)
    rest = jnp.where(e_iota == i1, -jnp.inf, logits)
    m2 = jnp.max(rest, axis=0, keepdims=True)
    i2 = jnp.min(jnp.where(rest == m2, e_iota, N_EXPERTS), axis=0, keepdims=True)
    e2 = jnp.exp(m2 - m1)
    gate_ref[0:1, :] = 1.0 / (1.0 + e2)
    gate_ref[1:2, :] = e2 / (1.0 + e2)
    idx_ref[0:1, :] = i1
    idx_ref[1:2, :] = i2
    hit1 = jnp.where(e_iota == i1, 1.0, 0.0)
    hit2 = jnp.where(e_iota == i2, 1.0, 0.0)
    before = jnp.where(_iota2((tm, tm), 0) < _iota2((tm, tm), 1), 1.0, 0.0)
    prior = _dot(hit1 + hit2, before) + carry_ref[:, 0:1]
    rank_ref[0:1, :] = jnp.sum(hit1 * prior, axis=0, keepdims=True).astype(jnp.int32)
    rank_ref[1:2, :] = jnp.sum(hit2 * prior, axis=0, keepdims=True).astype(jnp.int32)
    carry_ref[...] = carry_ref[...] + jnp.sum(hit1 + hit2, axis=1, keepdims=True)
    cnt_ref[...] = carry_ref[...]


def _merge_kernel(tm, route, n_cast, *refs):
    n_in, n_out = (11, 6) if route else (10, 2)
    ins, cast_in = refs[:n_in], refs[n_in:n_in + n_cast]
    outs = refs[n_in + n_cast:n_in + n_cast + n_out]
    cast_out = refs[n_in + n_cast + n_out:n_in + 2 * n_cast + n_out]
    scratch = refs[n_in + 2 * n_cast + n_out:]
    x_ref, h_ref, oa_ref, ob_ref, oc_ref, od_ref, wg_ref, wb_ref, wo_ref, nf_ref = ins[:10]
    _cast_slices(cast_in, cast_out)
    h = h_ref[...]
    acc = jnp.zeros(x_ref.shape, F32)
    for i, o_ref in enumerate((oa_ref, ob_ref, oc_ref, od_ref)):
        gate = _sigmoid(_dot_nt(h, wg_ref[i * D_MODEL:(i + 1) * D_MODEL, :]))
        acc = acc + gate * jnp.dot(o_ref[...], wb_ref[i], preferred_element_type=F32)
    x1 = x_ref[...] + jnp.dot(acc.astype(BF16), wo_ref[...], preferred_element_type=F32)
    ms = jnp.mean(x1 * x1, axis=-1, keepdims=True)
    hf = x1 * lax.rsqrt(ms + NORM_EPS) * nf_ref[...]
    if route:
        _route(tm, hf, ins[10], *outs[1:], *scratch)
    else:
        outs[1][...] = hf.astype(BF16)
    outs[0][...] = x1


def _merge(x2, h, outs, wg, wb, wo, nf, wr_t=None, cast=(), tm=512):
    n_tok = x2.shape[0]
    route = wr_t is not None
    steps = n_tok // tm
    row = lambda n: pl.BlockSpec((tm, n), lambda i: (i, 0))
    ins = [x2, h, *outs, wg, wb, wo, nf]
    in_specs = ([row(D_MODEL), row(D_MODEL)] + [row(BRANCH_DIM)] * 4
                + [_resident(wg.shape), _resident(wb.shape), _resident(wo.shape), _resident(nf.shape)])
    out_specs = [row(D_MODEL)]
    out_shape = [jax.ShapeDtypeStruct((n_tok, D_MODEL), F32)]
    scratch = []
    if route:
        ins.append(wr_t)
        in_specs.append(_resident(wr_t.shape))
        col = pl.BlockSpec((2, tm), lambda i: (0, i))
        out_specs += [col, col, col, pl.BlockSpec((N_EXPERTS, LANES), lambda i: (0, 0)),
                      pl.BlockSpec((ROW_SPLIT, tm, SUBROW), lambda i: (0, i, 0))]
        out_shape += [jax.ShapeDtypeStruct((2, n_tok), jnp.int32), jax.ShapeDtypeStruct((2, n_tok), F32),
                      jax.ShapeDtypeStruct((2, n_tok), jnp.int32), jax.ShapeDtypeStruct((N_EXPERTS, LANES), F32),
                      jax.ShapeDtypeStruct((ROW_SPLIT, n_tok, SUBROW), jnp.uint32)]
        scratch = [pltpu.VMEM((N_EXPERTS, LANES), F32)]
    else:
        out_specs.append(row(D_MODEL))
        out_shape.append(jax.ShapeDtypeStruct((n_tok, D_MODEL), BF16))
    riders = [_cast_rider(stack, index, steps) for stack, index in cast]
    res = pl.pallas_call(
        functools.partial(_merge_kernel, tm, route, len(cast)),
        grid=(steps,),
        in_specs=in_specs + [r[1] for r in riders],
        out_specs=out_specs + [r[2] for r in riders],
        out_shape=out_shape + [r[3] for r in riders],
        scratch_shapes=scratch,
        compiler_params=_params("arbitrary" if route else "parallel"),
        name="merge",
    )(*ins, *[r[0] for r in riders])
    n_plain = len(res) - len(cast)
    return list(res[:n_plain]) + [wb16.reshape(r[4]) for wb16, r in zip(res[n_plain:], riders)]


def _ffn_kernel(tf, x1_ref, hf_ref, w1_ref, w3_ref, w2_ref, o_ref):
    hf = hf_ref[...]
    acc = x1_ref[...]
    for lo in range(0, FFN_DIM, tf):
        cols = slice(lo, min(lo + tf, FFN_DIM))
        a = jnp.dot(hf, w1_ref[:, cols], preferred_element_type=F32)
        b = jnp.dot(hf, w3_ref[:, cols], preferred_element_type=F32)
        acc = acc + jnp.dot((_silu(a) * b).astype(BF16), w2_ref[cols, :], preferred_element_type=F32)
    o_ref[...] = acc


def _ffn(x1, hf, w1, w3, w2, tm=512, tf=512):
    n_tok = x1.shape[0]
    row = pl.BlockSpec((tm, D_MODEL), lambda i: (i, 0))
    return pl.pallas_call(
        functools.partial(_ffn_kernel, tf),
        grid=(n_tok // tm,),
        in_specs=[row, row, _resident(w1.shape), _resident(w3.shape), _resident(w2.shape)],
        out_specs=row,
        out_shape=jax.ShapeDtypeStruct((n_tok, D_MODEL), F32),
        compiler_params=_params("parallel"),
        name="ffn",
    )(x1, hf, w1, w3, w2)


def _gather_rows(table, idx, window=128):
    split, n_table, width = table.shape
    n = idx.shape[0]
    flat_idx = (idx[None, :] + n_table * jnp.arange(split, dtype=jnp.int32)[:, None]).reshape(-1)
    return _gather_subrows(table.reshape(split * n_table, width), flat_idx, window).reshape(split, n, width)


def _gather_subrows(table, idx, window):
    n = idx.shape[0]
    d = table.shape[1]
    mesh = plsc.VectorSubcoreMesh(core_axis_name="core", subcore_axis_name="subcore")

    @functools.partial(pl.kernel, out_type=jax.ShapeDtypeStruct((n, d), table.dtype), mesh=mesh,
                       name="gather_rows")
    def gather(table_hbm, idx_hbm, out_hbm):
        def body(idx_vmem, out_vmem):
            pltpu.sync_copy(table_hbm.at[idx_vmem.at[0]], out_vmem)

        pltpu.emit_pipeline(
            body,
            grid=(n // window,),
            in_specs=[pl.BlockSpec((1, window), index_map=lambda i: (0, i))],
            out_specs=[pl.BlockSpec((window, d), index_map=lambda i: (i, 0))],
            core_axis_name=("core", "subcore"),
            dimension_semantics=(pltpu.PARALLEL,),
        )(idx_hbm, out_hbm)

    return gather(table, idx.reshape(1, n))


def _scatter_rows(table, dest, n_out, window=128):
    split, n_table, width = table.shape
    n = dest.shape[0]
    flat_dest = (dest[None, :] + n_out * jnp.arange(split, dtype=jnp.int32)[:, None]).reshape(1, split * n)
    src_blocks, blocks = n_table // window, n // window
    mesh = plsc.VectorSubcoreMesh(core_axis_name="core", subcore_axis_name="subcore")

    @functools.partial(pl.kernel, out_type=jax.ShapeDtypeStruct((split * n_out, width), table.dtype), mesh=mesh,
                       name="scatter_rows")
    def scatter(table_hbm, dest_hbm, out_hbm):
        def body(rows_vmem, dest_vmem):
            pltpu.sync_copy(rows_vmem, out_hbm.at[dest_vmem.at[0]])

        pltpu.emit_pipeline(
            body,
            grid=(split * blocks,),
            in_specs=[pl.BlockSpec((window, width),
                                   index_map=lambda i: ((i // blocks) * src_blocks + (i % blocks) % src_blocks, 0)),
                      pl.BlockSpec((1, window), index_map=lambda i: (0, i))],
            out_specs=[],
            core_axis_name=("core", "subcore"),
            dimension_semantics=(pltpu.PARALLEL,),
        )(table_hbm, dest_hbm)

    return scatter(table.reshape(split * n_table, width), flat_dest).reshape(split, n_out, width)


def _expert_kernel(tf, ge_ref, nv_ref, x_ref, w1_ref, w3_ref, w2_ref, y_ref):
    g = pl.program_id(0)

    @pl.when(g < nv_ref[0])
    def _():
        x = _unpack_rows([x_ref[j] for j in range(ROW_SPLIT)]).astype(BF16)
        acc = jnp.zeros((MOE_ROWS, D_MODEL), F32)
        for lo in range(0, EXPERT_DIM, tf):
            a = jnp.dot(x, w1_ref[:, lo:lo + tf], preferred_element_type=F32)
            b = jnp.dot(x, w3_ref[:, lo:lo + tf], preferred_element_type=F32)
            acc = acc + jnp.dot((_silu(a) * b).astype(BF16), w2_ref[lo:lo + tf, :], preferred_element_type=F32)
        for j, words in enumerate(_pack_rows(acc)):
            y_ref[j] = words

    @pl.when(g >= nv_ref[0])
    def _():
        y_ref[...] = jnp.zeros_like(y_ref)


def _experts(xg, group_expert, n_valid, w1, w3, w2, tf=512):
    n_rows = xg.shape[1]
    rows = pl.BlockSpec((ROW_SPLIT, MOE_ROWS, SUBROW), lambda g, ge, nv: (0, g, 0))
    return pl.pallas_call(
        functools.partial(_expert_kernel, tf),
        grid_spec=pltpu.PrefetchScalarGridSpec(
            num_scalar_prefetch=2,
            grid=(n_rows // MOE_ROWS,),
            in_specs=[rows,
                      pl.BlockSpec((None, D_MODEL, EXPERT_DIM), lambda g, ge, nv: (ge[g], 0, 0)),
                      pl.BlockSpec((None, D_MODEL, EXPERT_DIM), lambda g, ge, nv: (ge[g], 0, 0)),
                      pl.BlockSpec((None, EXPERT_DIM, D_MODEL), lambda g, ge, nv: (ge[g], 0, 0))],
            out_specs=rows,
        ),
        out_shape=jax.ShapeDtypeStruct((ROW_SPLIT, n_rows, SUBROW), jnp.uint32),
        compiler_params=_params("arbitrary", vmem_limit=BIG_VMEM_LIMIT),
        name="experts",
    )(group_expert, n_valid, xg, w1, w3, w2)


def _combine_kernel(x1_ref, y0_ref, y1_ref, gate_ref, o_ref):
    gate = gate_ref[...]
    y0 = _unpack_rows([y0_ref[j] for j in range(ROW_SPLIT)])
    y1 = _unpack_rows([y1_ref[j] for j in range(ROW_SPLIT)])
    o_ref[...] = x1_ref[...] + gate[:, 0:1] * y0 + gate[:, 1:2] * y1


def _combine(x1, yg, gate_t, tm=512):
    n_tok = x1.shape[0]
    nb = n_tok // tm
    return pl.pallas_call(
        _combine_kernel,
        grid=(nb,),
        in_specs=[pl.BlockSpec((tm, D_MODEL), lambda i: (i, 0)),
                  pl.BlockSpec((ROW_SPLIT, tm, SUBROW), lambda i: (0, i, 0)),
                  pl.BlockSpec((ROW_SPLIT, tm, SUBROW), lambda i: (0, i + nb, 0)),
                  pl.BlockSpec((tm, 2), lambda i: (i, 0))],
        out_specs=pl.BlockSpec((tm, D_MODEL), lambda i: (i, 0)),
        out_shape=jax.ShapeDtypeStruct((n_tok, D_MODEL), F32),
        compiler_params=_params("parallel"),
        name="moe_combine",
    )(x1, yg, yg, gate_t)


def _moe(x1, routing, w1, w3, w2):
    n_tok = x1.shape[0]
    idx, gate, rank, cnt, hf = routing
    counts = cnt[:, 0].astype(jnp.int32)
    padded = (counts + MOE_ROWS - 1) // MOE_ROWS * MOE_ROWS
    end_padded = jnp.cumsum(padded)
    start_padded = end_padded - padded
    start = sum(jnp.where(idx == e, start_padded[e], 0) for e in range(N_EXPERTS))
    dest = (start + rank).reshape(-1)
    n_groups = (n_tok * 2 + MOE_ROWS - 1) // MOE_ROWS + N_EXPERTS
    n_rows = n_groups * MOE_ROWS
    group_row = jnp.arange(n_groups, dtype=jnp.int32)[:, None] * MOE_ROWS
    group_expert = jnp.minimum(jnp.sum(group_row >= end_padded[None, :], axis=1), N_EXPERTS - 1).astype(jnp.int32)
    n_valid = (end_padded[-1:] // MOE_ROWS).astype(jnp.int32)
    fill = jnp.arange(MOE_ROWS, dtype=jnp.int32)[None, :]
    pad_dest = jnp.where(fill < (padded - counts)[:, None], (start_padded + counts)[:, None] + fill,
                         n_rows - MOE_ROWS + fill).reshape(-1)
    xg = _scatter_rows(hf, jnp.concatenate([dest, pad_dest]), n_rows)
    y = _experts(xg, group_expert, n_valid, w1, w3, w2)
    return _combine(x1, _gather_rows(y, dest), gate.T)


def _row(v, pad=0):
    v = v.reshape(1, -1).astype(F32)
    return jnp.pad(v, ((0, 0), (0, pad))) if pad else v


def _rows_at(w, start, total):
    return jnp.pad(w.astype(F32), ((start, total - start - w.shape[0]), (0, 0)))


def kernel(x, w_in, norm_mix, rwkv_mu, rwkv_decay_up, rwkv_w0, rwkv_a_up, rwkv_a0, rwkv_gate_up, rwkv_k_k, rwkv_k_a, rwkv_r_k, rwkv_ln_w, rwkv_ln_b, vres_down, vres_up, vres_v0, ssm_conv_w, ssm_conv_b, ssm_dt_bias, ssm_a_log, ssm_d, ssm_norm_w, att_q_gain, att_k_gain, att_rel_bias, gla_gate_up, gla_gate_bias, gla_norm_w, w_branch, w_out, norm_ffn, ffn_w1, ffn_w3, ffn_w2, moe_router, moe_w1, moe_w3, moe_w2):
    bsz, seq, _ = x.shape
    n_tok = bsz * seq
    depth = w_in.shape[0]
    x2 = x.reshape(n_tok, D_MODEL)
    v_first = None
    for l in range(depth):
        n_cols = w_in.shape[2]
        w_t = jnp.swapaxes(w_in, 1, 2).reshape(depth * n_cols, D_MODEL)
        patches = jnp.zeros((2, PACK_ROWS, D_MODEL), F32)
        if l > 0:
            lo = RWKV_COLS % PACK_ROWS
            patches = patches.at[1, lo:lo + 32].set(vres_down[l - 1].T)
        w_mix = _pack_weight_rows(w_t, _mixer_plan(l * n_cols), patches)
        o_gate = RWKV_COLS + SSM_COLS + ATT_COLS + GLA_COLS
        wg = _pack_weight_rows(w_t, [(l * n_cols + r, GATE_PACK_ROWS, 0) for r in range(o_gate, n_cols, GATE_PACK_ROWS)],
                               jnp.zeros((1, GATE_PACK_ROWS, D_MODEL), F32))
        moe = l % 2 == 1
        up_w = [(moe_w3, l // 2)] if moe else [(ffn_w1, l // 2), (ffn_w3, l // 2)]
        h, za, zb, zc, zd, wb_b, wo_b, *up_b = _inproj(x2, _row(norm_mix[l]), w_mix,
                                                       cast=[(w_branch, l), (w_out, l)] + up_w)
        shp = lambda z: z.reshape(bsz, seq, z.shape[-1])

        rw = dict(mu=_row(rwkv_mu[l], RWKV_PAD - RWKV_COLS),
                  dup=_rows_at(rwkv_decay_up[l], 0, LANES), w0=_row(rwkv_w0[l]),
                  aup=_rows_at(rwkv_a_up[l], 64, LANES), a0=_row(rwkv_a0[l]),
                  gup=_rows_at(rwkv_gate_up[l], 0, 256).astype(BF16),
                  kk=_row(rwkv_k_k[l]), ka=_row(rwkv_k_a[l]), rk=_row(rwkv_r_k[l]),
                  lnw=_row(rwkv_ln_w[l]), lnb=_row(rwkv_ln_b[l]))
        if l > 0:
            rw.update(vup=_rows_at(vres_up[l - 1], 160, 256), v0=_row(vres_v0[l - 1]))
        ssm = dict(cw=ssm_conv_w[l], cb=_row(ssm_conv_b[l]), dtb=_row(ssm_dt_bias[l], LANES - SSM_HEADS),
                   alog=_row(ssm_a_log[l], LANES - SSM_HEADS), dexp=_row(jnp.repeat(ssm_d[l], CHUNK)),
                   nw=_row(ssm_norm_w[l]))
        gla = dict(gup=_rows_at(gla_gate_up[l], 0, LANES), gb=_row(gla_gate_bias[l]), nw=_row(gla_norm_w[l]))
        (o_c,), (o_d,), (o_b,), res_a = _run_parts(
            [_attention_part(shp(zc), att_q_gain[l], att_k_gain[l], att_rel_bias, MIXER_BLOCK),
             _gla_part(shp(zd), gla, MIXER_BLOCK), _mamba_part(shp(zb), ssm, MIXER_BLOCK),
             _rwkv_part(shp(za), v_first, rw, MIXER_BLOCK)],
            grid=(bsz, seq // MIXER_BLOCK), name="mixers")
        o_a = res_a[0]
        if l == 0:
            v_first = res_a[1]
        outs = [o.reshape(n_tok, BRANCH_DIM) for o in (o_a, o_b, o_c, o_d)]
        merge_w = (wg, wb_b, wo_b, _row(norm_ffn[l]))
        if not moe:
            x1, hf, *w1_b = _merge(x2, h, outs, *merge_w, cast=[(moe_w1, l // 2)] if l + 1 < depth else [])
            x2 = _ffn(x1, hf, *up_b, ffn_w2[l // 2].astype(BF16))
        else:
            x1, *routing, w2_b = _merge(x2, h, outs, *merge_w, wr_t=moe_router[l // 2].T, cast=[(moe_w2, l // 2)])
            x2 = _moe(x1, routing, w1_b[0], up_b[0], w2_b)
    return x2.reshape(bsz, seq, D_MODEL)
```

```python
import functools
from typing import Any, NamedTuple

import jax
import jax.numpy as jnp
from jax import lax
from jax.experimental import pallas as pl
from jax.experimental.pallas import tpu as pltpu
from jax.experimental.pallas import tpu_sc as plsc

F32 = jnp.float32
BF16 = jnp.bfloat16

D_MODEL = 1024
CHUNK = 64
BRANCH_DIM = 512
NORM_EPS = 1e-6
LANES = 128
VMEM_LIMIT = 56 * 1024 * 1024
BIG_VMEM_LIMIT = 61 * 1024 * 1024

LOG2_E = 1.4426950408889634

RWKV_LN_EPS = 64e-5
RWKV_DECAY_SCALE = 0.6065306597126334
RWKV_COLS = 1824
RWKV_PAD = 1920
SSM_COLS = 1544
SSM_PAD = 1664
SSM_HEADS = 8
SSM_STATE = 128
ATT_COLS = 1536
ATT_HEADS = 8
ATT_LEFT = 8 * CHUNK
REL_CLIP = 2 * CHUNK
GLA_COLS = 1552
GLA_PAD = 1664
GLA_GATE_NORM = 16.0
FFN_DIM = 2816
N_EXPERTS = 8
EXPERT_DIM = 3584
MOE_ROWS = 512
ROW_SPLIT = 2
SUBROW = D_MODEL // (2 * ROW_SPLIT)


def _dot(a, b):
    return jnp.dot(a.astype(BF16), b.astype(BF16), preferred_element_type=F32)


def _dot_nt(a, b):
    return lax.dot_general(a.astype(BF16), b.astype(BF16), (((1,), (1,)), ((), ())),
                           preferred_element_type=F32)


def _dot_tn(a, b):
    return lax.dot_general(a.astype(BF16), b.astype(BF16), (((0,), (0,)), ((), ())),
                           preferred_element_type=F32)


def _hi_lo(a):
    hi = a.astype(BF16)
    return hi, (a - hi.astype(F32)).astype(BF16)


def _dot_3x(a, b):
    a_hi, a_lo = _hi_lo(a)
    b_hi, b_lo = _hi_lo(b)
    return (jnp.dot(a_hi, b_hi, preferred_element_type=F32) + jnp.dot(a_lo, b_hi, preferred_element_type=F32)
            + jnp.dot(a_hi, b_lo, preferred_element_type=F32))


def _dot_split(a, m):
    m = m.astype(BF16)
    hi, lo = _hi_lo(a)
    return jnp.dot(hi, m, preferred_element_type=F32) + jnp.dot(lo, m, preferred_element_type=F32)


def _dot_split_l(m, a):
    m = m.astype(BF16)
    hi, lo = _hi_lo(a)
    return jnp.dot(m, hi, preferred_element_type=F32) + jnp.dot(m, lo, preferred_element_type=F32)


def _pack_rows(x):
    bits = pltpu.bitcast(x.astype(BF16).astype(F32), jnp.uint32)
    half = D_MODEL // 2
    return [(bits[:, p * SUBROW:(p + 1) * SUBROW] & jnp.uint32(0xFFFF0000))
            | (bits[:, half + p * SUBROW:half + (p + 1) * SUBROW] >> jnp.uint32(16)) for p in range(ROW_SPLIT)]


def _unpack_rows(planes):
    hi = [pltpu.bitcast(w & jnp.uint32(0xFFFF0000), F32) for w in planes]
    lo = [pltpu.bitcast(w << jnp.uint32(16), F32) for w in planes]
    return jnp.concatenate(hi + lo, axis=-1)


def _softplus(x):
    return jnp.maximum(x, 0.0) + jnp.log(1.0 + jnp.exp(-jnp.abs(x)))


def _sigmoid(x):
    return 1.0 / (1.0 + jnp.exp(-x))


def _silu(x):
    return x * _sigmoid(x)


def _iota2(shape, axis):
    return lax.broadcasted_iota(jnp.int32, shape, axis)


def _group_mean_matrix(n, group):
    r = _iota2((n, n), 0) // group
    c = _iota2((n, n), 1) // group
    return jnp.where(r == c, 1.0 / group, 0.0).astype(F32)


def _chunk_tri(n):
    r = _iota2((n, n), 0)
    c = _iota2((n, n), 1)
    return jnp.where((r // CHUNK == c // CHUNK) & (r >= c), 1.0, 0.0).astype(F32)


def _resident(shape):
    nd = len(shape)
    return pl.BlockSpec(shape, lambda *_: (0,) * nd, pipeline_mode=pl.Buffered(1))


def _params(*sem, vmem_limit=VMEM_LIMIT):
    return pltpu.CompilerParams(dimension_semantics=sem, vmem_limit_bytes=vmem_limit)


_DONE = object()


class _Part(NamedTuple):
    body: Any
    inputs: list
    in_specs: list
    out_shapes: list
    out_specs: list
    scratch_shapes: list


def _run_parts(parts, grid, name):
    def split(refs, counts):
        out, lo = [], 0
        for n in counts:
            out.append(refs[lo:lo + n])
            lo += n
        return out

    n_in = [len(p.inputs) for p in parts]
    n_out = [len(p.out_shapes) for p in parts]
    n_scr = [len(p.scratch_shapes) for p in parts]

    def kernel(*refs):
        ins = split(refs[:sum(n_in)], n_in)
        outs = split(refs[sum(n_in):sum(n_in) + sum(n_out)], n_out)
        scr = split(refs[sum(n_in) + sum(n_out):], n_scr)
        stages = [part.body(*i, *o, *c) for part, i, o, c in zip(parts, ins, outs, scr)]
        while stages:
            for stage in list(stages):
                if next(stage, _DONE) is _DONE:
                    stages.remove(stage)

    flat = lambda field: [x for p in parts for x in getattr(p, field)]
    res = pl.pallas_call(
        kernel,
        grid=grid,
        in_specs=flat("in_specs"),
        out_specs=flat("out_specs"),
        out_shape=flat("out_shapes"),
        scratch_shapes=flat("scratch_shapes"),
        compiler_params=_params("parallel", "arbitrary", vmem_limit=BIG_VMEM_LIMIT),
        name=name,
    )(*flat("inputs"))
    return split(list(res), n_out)


def _cast_rider(stack, index, steps):
    shape = stack.shape[1:]
    cols = shape[-1]
    rows = stack[0].size // cols // steps
    assert stack[0].size // cols % steps == 0 and rows % 16 == 0
    return (stack.reshape(stack.shape[0] * steps, rows, cols),
            pl.BlockSpec((None, rows, cols), lambda i: (index * steps + i, 0, 0)),
            pl.BlockSpec((None, rows, cols), lambda i: (i, 0, 0)),
            jax.ShapeDtypeStruct((steps, rows, cols), BF16), shape)


def _cast_slices(src_refs, dst_refs):
    for src, dst in zip(src_refs, dst_refs):
        dst[...] = src[...].astype(dst.dtype)


MIXER_COLS = (RWKV_PAD, SSM_PAD, ATT_COLS, GLA_PAD)
MIXER_BLOCK = 256


PACK_ROWS = 128
GATE_PACK_ROWS = 512


def _pack_kernel(src_ref, keep_ref, fill_ref, w_ref, patch_ref, o_ref):
    del src_ref, fill_ref
    rows = _iota2(o_ref.shape, 0)
    o_ref[...] = jnp.where(rows < keep_ref[pl.program_id(0)], w_ref[...], patch_ref[...]).astype(o_ref.dtype)


def _pack_weight_rows(w_t, plan, patches):
    src, keep, fill = (jnp.asarray(v, jnp.int32) for v in zip(*plan))
    rows, cols = patches.shape[1:]
    return pl.pallas_call(
        _pack_kernel,
        grid_spec=pltpu.PrefetchScalarGridSpec(
            num_scalar_prefetch=3,
            grid=(len(plan),),
            in_specs=[pl.BlockSpec((pl.Element(rows), pl.Element(cols)),
                                   lambda j, src, keep, fill: (pl.multiple_of(src[j], 8), 0)),
                      pl.BlockSpec((None, rows, cols), lambda j, src, keep, fill: (fill[j], 0, 0))],
            out_specs=pl.BlockSpec((rows, cols), lambda j, src, keep, fill: (j, 0)),
        ),
        out_shape=jax.ShapeDtypeStruct((len(plan) * rows, cols), BF16),
        compiler_params=_params("parallel"),
        name="pack_weights",
    )(src, keep, fill, w_t, patches)


def _mixer_plan(base):
    o_ss, o_at = RWKV_COLS + SSM_COLS, RWKV_COLS + SSM_COLS + ATT_COLS
    plan = []

    def rows(lo, hi, fill=0):
        for r in range(lo, hi, PACK_ROWS):
            plan.append((base + r, min(PACK_ROWS, hi - r), fill))

    rows(0, RWKV_COLS, fill=1)
    rows(RWKV_COLS, o_ss)
    rows(o_ss, o_at)
    rows(o_at, o_at + 1024)
    rows(o_at + 1040, o_at + GLA_COLS)
    rows(o_at + 1024, o_at + 1040)
    assert len(plan) * PACK_ROWS == sum(MIXER_COLS)
    return plan


def _inproj_kernel(n_cast, x_ref, g_ref, w_ref, *refs):
    cast_in, (h_ref, *z_refs), cast_out = refs[:n_cast], refs[n_cast:len(refs) - n_cast], refs[len(refs) - n_cast:]
    _cast_slices(cast_in, cast_out)
    x = x_ref[...]
    ms = jnp.mean(x * x, axis=-1, keepdims=True)
    h = (x * lax.rsqrt(ms + NORM_EPS) * g_ref[...]).astype(BF16)
    h_ref[...] = h
    lo = 0
    for pair in (z_refs[:2], z_refs[2:]):
        widths = [z_ref.shape[-1] for z_ref in pair]
        z = _dot_nt(h, w_ref[lo:lo + sum(widths), :])
        pair[0][...] = z[:, :widths[0]]
        pair[1][...] = z[:, widths[0]:]
        lo += sum(widths)


def _inproj(x2, g, w_mix, cast=(), tm=512):
    n_tok = x2.shape[0]
    row = lambda n: pl.BlockSpec((tm, n), lambda i: (i, 0))
    riders = [_cast_rider(stack, index, n_tok // tm) for stack, index in cast]
    res = pl.pallas_call(
        functools.partial(_inproj_kernel, len(cast)),
        grid=(n_tok // tm,),
        in_specs=[row(D_MODEL), _resident((1, D_MODEL)), _resident(w_mix.shape)] + [r[1] for r in riders],
        out_specs=[row(D_MODEL)] + [row(n) for n in MIXER_COLS] + [r[2] for r in riders],
        out_shape=[jax.ShapeDtypeStruct((n_tok, D_MODEL), BF16)]
        + [jax.ShapeDtypeStruct((n_tok, n), F32) for n in MIXER_COLS] + [r[3] for r in riders],
        compiler_params=_params("parallel", vmem_limit=BIG_VMEM_LIMIT if cast else VMEM_LIMIT),
        name="inproj",
    )(x2, g, w_mix, *[r[0] for r in riders])
    n_plain = len(res) - len(cast)
    return list(res[:n_plain]) + [wb16.reshape(r[4]) for wb16, r in zip(res[n_plain:], riders)]


def _rwkv_block(r, k, v, kk, ka, ld, tb, state_ref, y_ref):
    n_chunks = tb // CHUNK
    cum = _dot_split_l(_chunk_tri(tb), ld)
    p_inv = jnp.exp(-cum)
    a_t = -kk * jnp.exp(cum - ld)
    b_t = kk * ka * p_inv
    k_t = k * p_inv
    r_t = r * jnp.exp(cum)
    bf = lambda z: z.astype(BF16)
    a_b, b_b, k_b, r_b, v_b = bf(a_t), bf(b_t), bf(k_t), bf(r_t), bf(v)
    yield

    lane_head = _iota2((CHUNK, LANES), 1) // CHUNK
    row = _iota2((CHUNK, LANES), 0)
    col = _iota2((CHUNK, LANES), 1) % CHUNK
    strict = row > col
    incl = row >= col
    bd_mask = (_iota2((LANES, LANES), 0) // CHUNK) == (_iota2((LANES, LANES), 1) // CHUNK)
    eye = jnp.where(_iota2((LANES, LANES), 0) == _iota2((LANES, LANES), 1), 1.0, 0.0).astype(F32)

    def stack(z):
        return jnp.concatenate([jnp.where(lane_head == 0, z, 0.0), jnp.where(lane_head == 1, z, 0.0)], axis=0)

    def sub(z, u):
        c, p = u
        return z[c * CHUNK:(c + 1) * CHUNK, p * LANES:(p + 1) * LANES]

    def prepare(units):
        gram = {u: _dot_nt(jnp.concatenate([sub(a_b, u), sub(r_b, u)], axis=0),
                           jnp.concatenate([stack(sub(b_b, u)), stack(sub(k_b, u))], axis=0))
                for u in units}
        yield
        a_rb = {u: bf(jnp.where(incl, gram[u][CHUNK:, :LANES], 0.0)) for u in units}
        a_akrk = {u: bf(jnp.concatenate([jnp.where(strict, gram[u][:CHUNK, LANES:], 0.0),
                                         jnp.where(incl, gram[u][CHUNK:, LANES:], 0.0)], axis=0)) for u in units}
        pw = {u: stack(jnp.where(strict, gram[u][:CHUNK, :LANES], 0.0)) for u in units}
        t_bd = {u: eye + pw[u] for u in units}
        pw = {u: _dot(pw[u], pw[u]) for u in units}
        yield
        for _ in range(CHUNK.bit_length() - 3):
            sp = {u: _dot(jnp.concatenate([t_bd[u], pw[u]], axis=0), pw[u]) for u in units}
            t_bd = {u: t_bd[u] + sp[u][:LANES] for u in units}
            pw = {u: sp[u][LANES:] for u in units}
            yield
        t_bd = {u: bf(t_bd[u] + _dot(t_bd[u], pw[u])) for u in units}
        v_bd = {u: stack(sub(v_b, u)) for u in units}
        avyv = {u: _dot(a_akrk[u], v_bd[u]) for u in units}
        yield
        wuv = {u: _dot(t_bd[u], jnp.concatenate([stack(sub(a_b, u)), stack(bf(avyv[u][:CHUNK]))], axis=1))
               for u in units}
        wr = {u: jnp.concatenate([bf(wuv[u][:, :LANES]), sub(r_b, u)], axis=0) for u in units}
        yield
        return a_rb, avyv, wuv, wr

    a_rb, avyv, wuv, wr = yield from prepare([(c, p) for c in range(n_chunks) for p in range(4)])

    h = [state_ref[p] for p in range(4)]
    for c in range(n_chunks):
        last = c * CHUNK + CHUNK - 1
        p_last = jnp.exp(cum[last:last + 1, :])
        hs = [_dot_nt(wr[(c, p)], h[p]) for p in range(4)]
        u2 = [hs[p][:LANES] + wuv[(c, p)][:, LANES:] for p in range(4)]
        u_p = [u2[p][:CHUNK] + u2[p][CHUNK:] for p in range(4)]
        upd = [_dot_tn(jnp.concatenate([u_p[p], sub(v, (c, p))], axis=0),
                       jnp.concatenate([sub(b_t, (c, p)), sub(k_t, (c, p))], axis=0)
                       * p_last[:, p * LANES:(p + 1) * LANES]) for p in range(4)]
        for p in range(4):
            y_ref[c * CHUNK:(c + 1) * CHUNK, p * LANES:(p + 1) * LANES] = (
                hs[p][LANES:] + _dot(a_rb[(c, p)], stack(u_p[p])) + avyv[(c, p)][CHUNK:])
        h = [h[p] * p_last[:, p * LANES:(p + 1) * LANES] + jnp.where(bd_mask, upd[p], 0.0) for p in range(4)]
        yield
    for p in range(4):
        state_ref[p] = h[p]


def _rwkv_kernel(has_vres, tb, *refs):
    if has_vres:
        (za_ref, vfirst_ref, mu_ref, dup_ref, w0_ref, aup_ref, a0_ref, gup_ref, vup_ref, v0_ref,
         kk_ref, ka_ref, rk_ref, lnw_ref, lnb_ref, o_ref, prev_ref, state_ref, y_ref) = refs
    else:
        (za_ref, mu_ref, dup_ref, w0_ref, aup_ref, a0_ref, gup_ref,
         kk_ref, ka_ref, rk_ref, lnw_ref, lnb_ref, o_ref, vraw_ref, prev_ref, state_ref, y_ref) = refs

    @pl.when(pl.program_id(1) == 0)
    def _():
        prev_ref[...] = jnp.zeros_like(prev_ref)
        state_ref[...] = jnp.zeros_like(state_ref)

    yield
    za = za_ref[...]
    shifted = pltpu.roll(za, 1, axis=0)
    shifted = jnp.where(_iota2(za.shape, 0) == 0, prev_ref[7:8, :], shifted)
    prev_ref[...] = za[tb - 8:, :]
    za = za + (shifted - za) * mu_ref[...]

    r = za[:, 0:512]
    k = za[:, 512:1024]
    v = za[:, 1024:1536]
    xwa = za[:, 1536:1664]
    xg = za[:, 1664:1920]
    ld = -RWKV_DECAY_SCALE * _sigmoid(w0_ref[...] + _dot_3x(jnp.tanh(xwa), dup_ref[...]))
    yield
    a = _sigmoid(a0_ref[...] + _dot(xwa, aup_ref[...]))
    g = _dot(_sigmoid(xg), gup_ref[...])
    if has_vres:
        v_mix = _sigmoid(v0_ref[...] + _dot(xg, vup_ref[...]))
        v = v + (vfirst_ref[...] - v) * v_mix
    else:
        vraw_ref[...] = v
    yield
    kk = k * kk_ref[...]
    k = k * (1.0 + (a - 1.0) * ka_ref[...])
    head_sum = _group_mean_matrix(512, CHUNK) * float(CHUNK)
    kk = kk / jnp.maximum(jnp.sqrt(_dot(kk * kk, head_sum)), 1e-12)
    yield

    yield from _rwkv_block(r, k, v, kk, a, ld, tb, state_ref, y_ref)
    y = y_ref[...]
    head_mean = _group_mean_matrix(512, CHUNK)
    mean = _dot_split(y, head_mean)
    yc = y - mean
    var = _dot(yc * yc, head_mean)
    y = yc * lax.rsqrt(var + RWKV_LN_EPS) * lnw_ref[...] + lnb_ref[...]
    y = y + _dot(r * k * rk_ref[...], head_sum) * v
    o_ref[...] = (y * g).astype(o_ref.dtype)


def _rwkv_part(za, v_first, prm, tb):
    bsz, seq, _ = za.shape
    has_vres = v_first is not None
    blk = lambda n: pl.BlockSpec((None, tb, n), lambda b, i: (b, i, 0))
    names = (["mu", "dup", "w0", "aup", "a0", "gup"] + (["vup", "v0"] if has_vres else [])
             + ["kk", "ka", "rk", "lnw", "lnb"])
    weights = [prm[n] for n in names]
    ins = [za] + ([v_first] if has_vres else []) + weights
    in_specs = [blk(RWKV_PAD)] + ([blk(512)] if has_vres else []) + [_resident(w.shape) for w in weights]
    out_shape = [jax.ShapeDtypeStruct((bsz, seq, 512), BF16)]
    out_specs = [blk(512)]
    if not has_vres:
        out_shape.append(jax.ShapeDtypeStruct((bsz, seq, 512), F32))
        out_specs.append(blk(512))
    return _Part(functools.partial(_rwkv_kernel, has_vres, tb), ins, in_specs, out_shape, out_specs,
                 [pltpu.VMEM((8, RWKV_PAD), F32), pltpu.VMEM((4, LANES, LANES), F32), pltpu.VMEM((tb, 512), F32)])


def _mamba_kernel(tb, zb_ref, cw_ref, cb_ref, dtb_ref, alog_ref, dexp_ref, nw_ref,
                  o_ref, xbuf_ref, state_ref, y_ref):
    @pl.when(pl.program_id(1) == 0)
    def _():
        xbuf_ref[0:8, :] = jnp.zeros((8, 1024), F32)
        state_ref[...] = jnp.zeros_like(state_ref)

    yield
    xbuf_ref[8:8 + tb, :] = zb_ref[:, 512:1536]
    conv = cb_ref[...]
    for i in range(4):
        conv = conv + cw_ref[i:i + 1, :] * xbuf_ref[5 + i:5 + i + tb, :]
    xbuf_ref[0:8, :] = xbuf_ref[tb:tb + 8, :]
    xbc = _silu(conv)
    xs = xbc[:, 0:512]
    bm = xbc[:, 512:768]
    cm = xbc[:, 768:1024]
    yield

    lane = _iota2((1, LANES), 1)
    dt = _softplus(zb_ref[:, 1536:1664] + dtb_ref[...])
    a_neg = jnp.where(lane < SSM_HEADS, -jnp.exp(alog_ref[...]), 0.0)
    acs = _dot_split_l(_chunk_tri(tb), dt * a_neg)
    expand = jnp.where(_iota2((LANES, 512), 0) == _iota2((LANES, 512), 1) // CHUNK, 1.0, 0.0)
    dt_e = _dot(dt, expand)
    acs_e = _dot_split(acs, expand)
    xdt = xs * dt_e
    yield

    n_chunks = tb // CHUNK
    causal = _iota2((CHUNK, 512), 0) >= _iota2((CHUNK, 512), 1) % CHUNK
    pair_head = _iota2((CHUNK, LANES), 1) // CHUNK
    spread = jnp.where(_iota2((CHUNK, 512), 0) == _iota2((CHUNK, 512), 1) % CHUNK, 1.0, 0.0)
    own_head = _iota2((SSM_HEADS, 512), 0) == _iota2((SSM_HEADS, 512), 1) // CHUNK
    rows_of = lambda c: slice(c * CHUNK, (c + 1) * CHUNK)
    grp = lambda g: slice(g * SSM_STATE, (g + 1) * SSM_STATE)

    def stack(z):
        return jnp.concatenate([jnp.where(pair_head == 0, z, 0.0), jnp.where(pair_head == 1, z, 0.0)], axis=0)

    y_intra, upd, e_in, e_last = [], [], [], []
    for c in range(n_chunks):
        rows = rows_of(c)
        acs_ec = acs_e[rows]
        acs_t = acs[rows].T[0:SSM_HEADS]
        acs_row = jnp.sum(jnp.where(own_head, _dot_split(acs_t, spread), 0.0), axis=0, keepdims=True)
        decay = jnp.exp(jnp.where(causal, acs_ec - acs_row, -jnp.inf))
        cb = jnp.concatenate(
            [_dot_nt(cm[rows, grp(g)], jnp.concatenate([bm[rows, grp(g)]] * 4, axis=0)) for g in range(2)], axis=1)
        m = cb * decay
        yield
        xdt_c = xdt[rows]
        y_intra.append([_dot(m[:, p * LANES:(p + 1) * LANES], stack(xdt_c[:, p * LANES:(p + 1) * LANES]))
                        for p in range(4)])
        last_e = acs_ec[CHUNK - 1:CHUNK, :]
        x_out = xdt_c * jnp.exp(last_e - acs_ec)
        upd.append([_dot_tn(bm[rows, grp(g)], x_out[:, g * 256:(g + 1) * 256]) for g in range(2)])
        e_in.append(jnp.exp(acs_ec))
        e_last.append(jnp.exp(last_e))
        yield

    state = [state_ref[g] for g in range(2)]
    for c in range(n_chunks):
        rows = rows_of(c)
        for g in range(2):
            ls = slice(g * 256, (g + 1) * 256)
            y_in = jnp.concatenate(y_intra[c][2 * g:2 * g + 2], axis=1)
            y_ref[rows, ls] = y_in + _dot(cm[rows, grp(g)], state[g]) * e_in[c][:, ls]
            state[g] = state[g] * e_last[c][:, ls] + upd[c][g]
        yield
    for g in range(2):
        state_ref[g] = state[g]

    y = (y_ref[...] + xs * dexp_ref[...]) * _silu(zb_ref[:, 0:512])
    for g in range(2):
        ls = slice(g * 256, (g + 1) * 256)
        yg = y[:, ls]
        ms = jnp.mean(yg * yg, axis=-1, keepdims=True)
        o_ref[:, ls] = (yg * lax.rsqrt(ms + NORM_EPS) * nw_ref[:, ls]).astype(o_ref.dtype)


def _mamba_part(zb, prm, tb):
    bsz, seq, _ = zb.shape
    blk = lambda n: pl.BlockSpec((None, tb, n), lambda b, i: (b, i, 0))
    weights = [prm[n] for n in ("cw", "cb", "dtb", "alog", "dexp", "nw")]
    return _Part(functools.partial(_mamba_kernel, tb), [zb] + weights,
                 [blk(SSM_PAD)] + [_resident(w.shape) for w in weights],
                 [jax.ShapeDtypeStruct((bsz, seq, 512), BF16)], [blk(512)],
                 [pltpu.VMEM((tb + 8, 1024), F32), pltpu.VMEM((2, SSM_STATE, 256), F32), pltpu.VMEM((tb, 512), F32)])


def _attn_kernel(tq, qkv_ref, qg_ref, kg_ref, bias_ref, o_ref, kn_ref, vb_ref):
    i = pl.program_id(1)
    head_mean = _group_mean_matrix(512, CHUNK)
    win = tq + ATT_LEFT

    @pl.when(i == 0)
    def _():
        kn_ref[0:ATT_LEFT, :] = jnp.zeros((ATT_LEFT, 512), BF16)
        vb_ref[0:ATT_LEFT, :] = jnp.zeros((ATT_LEFT, 512), BF16)

    yield
    start = pl.multiple_of(i * tq, tq)
    k = qkv_ref[:, 512:1024]
    ms = _dot(k * k, head_mean)
    kn_ref[pl.ds(ATT_LEFT + start, tq), :] = (k * lax.rsqrt(ms + NORM_EPS) * kg_ref[...]).astype(BF16)
    vb_ref[pl.ds(ATT_LEFT + start, tq), :] = qkv_ref[:, 1024:1536].astype(BF16)
    q = qkv_ref[:, 0:512]
    ms = _dot(q * q, head_mean)
    qn = q * lax.rsqrt(ms + NORM_EPS) * qg_ref[...] * (CHUNK ** -0.5 * LOG2_E)
    kwin = kn_ref[pl.ds(start, win), :]
    vwin = vb_ref[pl.ds(start, win), :]
    lane_head = _iota2((tq, LANES), 1) // CHUNK
    yield
    for p in range(4):
        ls = slice(p * LANES, (p + 1) * LANES)
        q2 = jnp.concatenate([jnp.where(lane_head == s, qn[:, ls], 0.0) for s in range(2)], axis=0)
        sc = _dot_nt(q2, kwin[:, ls]) + bias_ref[2 * p:2 * p + 2].reshape(2 * tq, win).astype(F32)
        yield
        e = jnp.exp2(sc - jnp.max(sc, axis=-1, keepdims=True))
        yield
        o2 = _dot(e, vwin[:, ls]) / jnp.sum(e, axis=-1, keepdims=True)
        o_ref[:, ls] = jnp.where(lane_head == 0, o2[:tq], o2[tq:]).astype(o_ref.dtype)
        yield


def _band_bias(rel_bias, tq, win):
    period = tq + win
    m = jnp.arange(period)
    d = jnp.where(m < win, m, m - period) - ATT_LEFT
    f = (rel_bias.astype(F32)[:, jnp.clip(d, -REL_CLIP, REL_CLIP) + REL_CLIP] * LOG2_E).astype(BF16)
    g = jnp.tile(f, (1, tq))[:, :tq * (period - 1)].reshape(-1, tq, period - 1)
    bias = g[:, :, :win]
    left = ATT_LEFT // CHUNK
    qc = left + jnp.arange(tq)[:, None] // CHUNK
    kc = jnp.arange(win)[None, :] // CHUNK
    tiles = []
    for blk in range(ATT_LEFT // tq + 1):
        first_kc = jnp.maximum(qc - left, left - blk * (tq // CHUNK))
        tiles.append(jnp.where((kc <= qc) & (kc >= first_kc), bias, -jnp.inf))
    return jnp.stack(tiles)


def _attention_part(zc, q_gain, k_gain, rel_bias, tq):
    bsz, seq, _ = zc.shape
    win = tq + ATT_LEFT
    bias = _band_bias(rel_bias, tq, win)
    qg = jnp.tile(q_gain, ATT_HEADS)[None, :]
    kg = jnp.tile(k_gain, ATT_HEADS)[None, :]
    last_tile = bias.shape[0] - 1
    bias_spec = pl.BlockSpec((None,) + bias.shape[1:], lambda b, i: (jnp.minimum(i, last_tile), 0, 0, 0))
    blk = lambda n: pl.BlockSpec((None, tq, n), lambda b, i: (b, i, 0))
    return _Part(functools.partial(_attn_kernel, tq), [zc, qg, kg, bias],
                 [blk(ATT_COLS), _resident(qg.shape), _resident(kg.shape), bias_spec],
                 [jax.ShapeDtypeStruct((bsz, seq, 512), BF16)], [blk(512)],
                 [pltpu.VMEM((ATT_LEFT + seq, 512), BF16), pltpu.VMEM((ATT_LEFT + seq, 512), BF16)])


def _gla_kernel(tb, zd_ref, gup_ref, gb_ref, nw_ref, o_ref, state_ref, y_ref):
    @pl.when(pl.program_id(1) == 0)
    def _():
        state_ref[...] = jnp.zeros_like(state_ref)

    yield
    q = zd_ref[:, 0:256] * (CHUNK ** -0.5)
    k = zd_ref[:, 256:512]
    v = zd_ref[:, 512:1024]
    log_a = -_softplus(-(_dot_3x(zd_ref[:, 1536:1664], gup_ref[...]) + gb_ref[...])) / GLA_GATE_NORM
    bcum = _dot_split_l(_chunk_tri(tb), log_a)

    k_head = _iota2((CHUNK, 256), 1) // CHUNK
    v_head = _iota2((CHUNK, 512), 1) // LANES
    causal = _iota2((CHUNK, 256), 0) >= _iota2((CHUNK, 256), 1) % CHUNK
    bd = (_iota2((512, 256), 0) // LANES) == (_iota2((512, 256), 1) // CHUNK)
    n_chunks = tb // CHUNK
    qg_all = (q * jnp.exp(bcum)).astype(BF16)
    kg_all = k * jnp.exp(-bcum)
    yield
    qg, att, o_intra, upd, e_last = [], [], [], [], []
    for c in range(n_chunks):
        rows = slice(c * CHUNK, (c + 1) * CHUNK)
        kg_bd = jnp.concatenate([jnp.where(k_head == h, kg_all[rows], 0.0) for h in range(4)], axis=0)
        qg.append(qg_all[rows])
        att.append(jnp.where(causal, _dot_nt(qg[c], kg_bd), 0.0))
        yield
    for c in range(n_chunks):
        rows = slice(c * CHUNK, (c + 1) * CHUNK)
        bc = bcum[rows]
        blast = bc[CHUNK - 1:CHUNK, :]
        v_c = v[rows]
        v_bd = jnp.concatenate([jnp.where(v_head == h, v_c, 0.0) for h in range(4)], axis=0)
        o_intra.append(_dot(att[c], v_bd))
        upd.append(jnp.where(bd, _dot_tn(v_c, k[rows] * jnp.exp(blast - bc)), 0.0))
        e_last.append(jnp.exp(blast))
        yield
    st = state_ref[...]
    for c in range(n_chunks):
        y_ref[c * CHUNK:(c + 1) * CHUNK, :] = o_intra[c] + _dot_nt(qg[c], st)
        st = st * e_last[c] + upd[c]
        yield
    state_ref[...] = st

    o = y_ref[...]
    for h in range(4):
        ls = slice(h * LANES, (h + 1) * LANES)
        oh = o[:, ls]
        ms = jnp.mean(oh * oh, axis=-1, keepdims=True)
        o_ref[:, ls] = (oh * lax.rsqrt(ms + NORM_EPS) * nw_ref[...]
                        * _silu(zd_ref[:, 1024 + h * LANES:1024 + (h + 1) * LANES])).astype(o_ref.dtype)


def _gla_part(zd, prm, tb):
    bsz, seq, _ = zd.shape
    blk = lambda n: pl.BlockSpec((None, tb, n), lambda b, i: (b, i, 0))
    weights = [prm[n] for n in ("gup", "gb", "nw")]
    return _Part(functools.partial(_gla_kernel, tb), [zd] + weights,
                 [blk(GLA_PAD)] + [_resident(w.shape) for w in weights],
                 [jax.ShapeDtypeStruct((bsz, seq, 512), BF16)], [blk(512)],
                 [pltpu.VMEM((512, 256), F32), pltpu.VMEM((tb, 512), F32)])


def _route(tm, hf, wr_ref, idx_ref, gate_ref, rank_ref, cnt_ref, hf_ref, carry_ref):
    @pl.when(pl.program_id(0) == 0)
    def _():
        carry_ref[...] = jnp.zeros_like(carry_ref)

    for j, words in enumerate(_pack_rows(hf)):
        hf_ref[j] = words
    nt = lambda a, b: lax.dot_general(a, b, (((1,), (1,)), ((), ())), preferred_element_type=F32)
    w_hi, w_lo = _hi_lo(wr_ref[...])
    h_hi, h_lo = _hi_lo(hf)
    logits = nt(w_hi, h_hi) + nt(w_lo, h_hi) + nt(w_hi, h_lo)
    e_iota = _iota2((N_EXPERTS, tm), 0)
    m1 = jnp.max(logits, axis=0, keepdims=True)
    i1 = jnp.min(jnp.where(logits == m1, e_iota, N_EXPERTS), axis=0, keepdims=True)
    rest = jnp.where(e_iota == i1, -jnp.inf, logits)
    m2 = jnp.max(rest, axis=0, keepdims=True)
    i2 = jnp.min(jnp.where(rest == m2, e_iota, N_EXPERTS), axis=0, keepdims=True)
    e2 = jnp.exp(m2 - m1)
    gate_ref[0:1, :] = 1.0 / (1.0 + e2)
    gate_ref[1:2, :] = e2 / (1.0 + e2)
    idx_ref[0:1, :] = i1
    idx_ref[1:2, :] = i2
    hit1 = jnp.where(e_iota == i1, 1.0, 0.0)
    hit2 = jnp.where(e_iota == i2, 1.0, 0.0)
    before = jnp.where(_iota2((tm, tm), 0) < _iota2((tm, tm), 1), 1.0, 0.0)
    prior = _dot(hit1 + hit2, before) + carry_ref[:, 0:1]
    rank_ref[0:1, :] = jnp.sum(hit1 * prior, axis=0, keepdims=True).astype(jnp.int32)
    rank_ref[1:2, :] = jnp.sum(hit2 * prior, axis=0, keepdims=True).astype(jnp.int32)
    carry_ref[...] = carry_ref[...] + jnp.sum(hit1 + hit2, axis=1, keepdims=True)
    cnt_ref[...] = carry_ref[...]


def _merge_kernel(tm, route, n_cast, *refs):
    n_in, n_out = (11, 6) if route else (10, 2)
    ins, cast_in = refs[:n_in], refs[n_in:n_in + n_cast]
    outs = refs[n_in + n_cast:n_in + n_cast + n_out]
    cast_out = refs[n_in + n_cast + n_out:n_in + 2 * n_cast + n_out]
    scratch = refs[n_in + 2 * n_cast + n_out:]
    x_ref, h_ref, oa_ref, ob_ref, oc_ref, od_ref, wg_ref, wb_ref, wo_ref, nf_ref = ins[:10]
    _cast_slices(cast_in, cast_out)
    h = h_ref[...]
    acc = jnp.zeros(x_ref.shape, F32)
    for i, o_ref in enumerate((oa_ref, ob_ref, oc_ref, od_ref)):
        gate = _sigmoid(_dot_nt(h, wg_ref[i * D_MODEL:(i + 1) * D_MODEL, :]))
        acc = acc + gate * jnp.dot(o_ref[...], wb_ref[i], preferred_element_type=F32)
    x1 = x_ref[...] + jnp.dot(acc.astype(BF16), wo_ref[...], preferred_element_type=F32)
    ms = jnp.mean(x1 * x1, axis=-1, keepdims=True)
    hf = x1 * lax.rsqrt(ms + NORM_EPS) * nf_ref[...]
    if route:
        _route(tm, hf, ins[10], *outs[1:], *scratch)
    else:
        outs[1][...] = hf.astype(BF16)
    outs[0][...] = x1


def _merge(x2, h, outs, wg, wb, wo, nf, wr_t=None, cast=(), tm=512):
    n_tok = x2.shape[0]
    route = wr_t is not None
    steps = n_tok // tm
    row = lambda n: pl.BlockSpec((tm, n), lambda i: (i, 0))
    ins = [x2, h, *outs, wg, wb, wo, nf]
    in_specs = ([row(D_MODEL), row(D_MODEL)] + [row(BRANCH_DIM)] * 4
                + [_resident(wg.shape), _resident(wb.shape), _resident(wo.shape), _resident(nf.shape)])
    out_specs = [row(D_MODEL)]
    out_shape = [jax.ShapeDtypeStruct((n_tok, D_MODEL), F32)]
    scratch = []
    if route:
        ins.append(wr_t)
        in_specs.append(_resident(wr_t.shape))
        col = pl.BlockSpec((2, tm), lambda i: (0, i))
        out_specs += [col, col, col, pl.BlockSpec((N_EXPERTS, LANES), lambda i: (0, 0)),
                      pl.BlockSpec((ROW_SPLIT, tm, SUBROW), lambda i: (0, i, 0))]
        out_shape += [jax.ShapeDtypeStruct((2, n_tok), jnp.int32), jax.ShapeDtypeStruct((2, n_tok), F32),
                      jax.ShapeDtypeStruct((2, n_tok), jnp.int32), jax.ShapeDtypeStruct((N_EXPERTS, LANES), F32),
                      jax.ShapeDtypeStruct((ROW_SPLIT, n_tok, SUBROW), jnp.uint32)]
        scratch = [pltpu.VMEM((N_EXPERTS, LANES), F32)]
    else:
        out_specs.append(row(D_MODEL))
        out_shape.append(jax.ShapeDtypeStruct((n_tok, D_MODEL), BF16))
    riders = [_cast_rider(stack, index, steps) for stack, index in cast]
    res = pl.pallas_call(
        functools.partial(_merge_kernel, tm, route, len(cast)),
        grid=(steps,),
        in_specs=in_specs + [r[1] for r in riders],
        out_specs=out_specs + [r[2] for r in riders],
        out_shape=out_shape + [r[3] for r in riders],
        scratch_shapes=scratch,
        compiler_params=_params("arbitrary" if route else "parallel"),
        name="merge",
    )(*ins, *[r[0] for r in riders])
    n_plain = len(res) - len(cast)
    return list(res[:n_plain]) + [wb16.reshape(r[4]) for wb16, r in zip(res[n_plain:], riders)]


def _ffn_kernel(tf, x1_ref, hf_ref, w1_ref, w3_ref, w2_ref, o_ref):
    hf = hf_ref[...]
    acc = x1_ref[...]
    for lo in range(0, FFN_DIM, tf):
        cols = slice(lo, min(lo + tf, FFN_DIM))
        a = jnp.dot(hf, w1_ref[:, cols], preferred_element_type=F32)
        b = jnp.dot(hf, w3_ref[:, cols], preferred_element_type=F32)
        acc = acc + jnp.dot((_silu(a) * b).astype(BF16), w2_ref[cols, :], preferred_element_type=F32)
    o_ref[...] = acc


def _ffn(x1, hf, w1, w3, w2, tm=512, tf=512):
    n_tok = x1.shape[0]
    row = pl.BlockSpec((tm, D_MODEL), lambda i: (i, 0))
    return pl.pallas_call(
        functools.partial(_ffn_kernel, tf),
        grid=(n_tok // tm,),
        in_specs=[row, row, _resident(w1.shape), _resident(w3.shape), _resident(w2.shape)],
        out_specs=row,
        out_shape=jax.ShapeDtypeStruct((n_tok, D_MODEL), F32),
        compiler_params=_params("parallel"),
        name="ffn",
    )(x1, hf, w1, w3, w2)


def _gather_rows(table, idx, window=128):
    split, n_table, width = table.shape
    n = idx.shape[0]
    flat_idx = (idx[None, :] + n_table * jnp.arange(split, dtype=jnp.int32)[:, None]).reshape(-1)
    return _gather_subrows(table.reshape(split * n_table, width), flat_idx, window).reshape(split, n, width)


def _gather_subrows(table, idx, window):
    n = idx.shape[0]
    d = table.shape[1]
    mesh = plsc.VectorSubcoreMesh(core_axis_name="core", subcore_axis_name="subcore")

    @functools.partial(pl.kernel, out_type=jax.ShapeDtypeStruct((n, d), table.dtype), mesh=mesh,
                       name="gather_rows")
    def gather(table_hbm, idx_hbm, out_hbm):
        def body(idx_vmem, out_vmem):
            pltpu.sync_copy(table_hbm.at[idx_vmem.at[0]], out_vmem)

        pltpu.emit_pipeline(
            body,
            grid=(n // window,),
            in_specs=[pl.BlockSpec((1, window), index_map=lambda i: (0, i))],
            out_specs=[pl.BlockSpec((window, d), index_map=lambda i: (i, 0))],
            core_axis_name=("core", "subcore"),
            dimension_semantics=(pltpu.PARALLEL,),
        )(idx_hbm, out_hbm)

    return gather(table, idx.reshape(1, n))


def _scatter_rows(table, dest, n_out, window=128):
    split, n_table, width = table.shape
    n = dest.shape[0]
    flat_dest = (dest[None, :] + n_out * jnp.arange(split, dtype=jnp.int32)[:, None]).reshape(1, split * n)
    src_blocks, blocks = n_table // window, n // window
    mesh = plsc.VectorSubcoreMesh(core_axis_name="core", subcore_axis_name="subcore")

    @functools.partial(pl.kernel, out_type=jax.ShapeDtypeStruct((split * n_out, width), table.dtype), mesh=mesh,
                       name="scatter_rows")
    def scatter(table_hbm, dest_hbm, out_hbm):
        def body(rows_vmem, dest_vmem):
            pltpu.sync_copy(rows_vmem, out_hbm.at[dest_vmem.at[0]])

        pltpu.emit_pipeline(
            body,
            grid=(split * blocks,),
            in_specs=[pl.BlockSpec((window, width),
                                   index_map=lambda i: ((i // blocks) * src_blocks + (i % blocks) % src_blocks, 0)),
                      pl.BlockSpec((1, window), index_map=lambda i: (0, i))],
            out_specs=[],
            core_axis_name=("core", "subcore"),
            dimension_semantics=(pltpu.PARALLEL,),
        )(table_hbm, dest_hbm)

    return scatter(table.reshape(split * n_table, width), flat_dest).reshape(split, n_out, width)


def _expert_kernel(tf, ge_ref, nv_ref, x_ref, w1_ref, w3_ref, w2_ref, y_ref):
    g = pl.program_id(0)

    @pl.when(g < nv_ref[0])
    def _():
        x = _unpack_rows([x_ref[j] for j in range(ROW_SPLIT)]).astype(BF16)
        acc = jnp.zeros((MOE_ROWS, D_MODEL), F32)
        for lo in range(0, EXPERT_DIM, tf):
            a = jnp.dot(x, w1_ref[:, lo:lo + tf], preferred_element_type=F32)
            b = jnp.dot(x, w3_ref[:, lo:lo + tf], preferred_element_type=F32)
            acc = acc + jnp.dot((_silu(a) * b).astype(BF16), w2_ref[lo:lo + tf, :], preferred_element_type=F32)
        for j, words in enumerate(_pack_rows(acc)):
            y_ref[j] = words

    @pl.when(g >= nv_ref[0])
    def _():
        y_ref[...] = jnp.zeros_like(y_ref)


def _experts(xg, group_expert, n_valid, w1, w3, w2, tf=512):
    n_rows = xg.shape[1]
    rows = pl.BlockSpec((ROW_SPLIT, MOE_ROWS, SUBROW), lambda g, ge, nv: (0, g, 0))
    return pl.pallas_call(
        functools.partial(_expert_kernel, tf),
        grid_spec=pltpu.PrefetchScalarGridSpec(
            num_scalar_prefetch=2,
            grid=(n_rows // MOE_ROWS,),
            in_specs=[rows,
                      pl.BlockSpec((None, D_MODEL, EXPERT_DIM), lambda g, ge, nv: (ge[g], 0, 0)),
                      pl.BlockSpec((None, D_MODEL, EXPERT_DIM), lambda g, ge, nv: (ge[g], 0, 0)),
                      pl.BlockSpec((None, EXPERT_DIM, D_MODEL), lambda g, ge, nv: (ge[g], 0, 0))],
            out_specs=rows,
        ),
        out_shape=jax.ShapeDtypeStruct((ROW_SPLIT, n_rows, SUBROW), jnp.uint32),
        compiler_params=_params("arbitrary", vmem_limit=BIG_VMEM_LIMIT),
        name="experts",
    )(group_expert, n_valid, xg, w1, w3, w2)


def _combine_kernel(x1_ref, y0_ref, y1_ref, gate_ref, o_ref):
    gate = gate_ref[...]
    y0 = _unpack_rows([y0_ref[j] for j in range(ROW_SPLIT)])
    y1 = _unpack_rows([y1_ref[j] for j in range(ROW_SPLIT)])
    o_ref[...] = x1_ref[...] + gate[:, 0:1] * y0 + gate[:, 1:2] * y1


def _combine(x1, yg, gate_t, tm=512):
    n_tok = x1.shape[0]
    nb = n_tok // tm
    return pl.pallas_call(
        _combine_kernel,
        grid=(nb,),
        in_specs=[pl.BlockSpec((tm, D_MODEL), lambda i: (i, 0)),
                  pl.BlockSpec((ROW_SPLIT, tm, SUBROW), lambda i: (0, i, 0)),
                  pl.BlockSpec((ROW_SPLIT, tm, SUBROW), lambda i: (0, i + nb, 0)),
                  pl.BlockSpec((tm, 2), lambda i: (i, 0))],
        out_specs=pl.BlockSpec((tm, D_MODEL), lambda i: (i, 0)),
        out_shape=jax.ShapeDtypeStruct((n_tok, D_MODEL), F32),
        compiler_params=_params("parallel"),
        name="moe_combine",
    )(x1, yg, yg, gate_t)


def _moe(x1, routing, w1, w3, w2):
    n_tok = x1.shape[0]
    idx, gate, rank, cnt, hf = routing
    counts = cnt[:, 0].astype(jnp.int32)
    padded = (counts + MOE_ROWS - 1) // MOE_ROWS * MOE_ROWS
    end_padded = jnp.cumsum(padded)
    start_padded = end_padded - padded
    start = sum(jnp.where(idx == e, start_padded[e], 0) for e in range(N_EXPERTS))
    dest = (start + rank).reshape(-1)
    n_groups = (n_tok * 2 + MOE_ROWS - 1) // MOE_ROWS + N_EXPERTS
    n_rows = n_groups * MOE_ROWS
    group_row = jnp.arange(n_groups, dtype=jnp.int32)[:, None] * MOE_ROWS
    group_expert = jnp.minimum(jnp.sum(group_row >= end_padded[None, :], axis=1), N_EXPERTS - 1).astype(jnp.int32)
    n_valid = (end_padded[-1:] // MOE_ROWS).astype(jnp.int32)
    fill = jnp.arange(MOE_ROWS, dtype=jnp.int32)[None, :]
    pad_dest = jnp.where(fill < (padded - counts)[:, None], (start_padded + counts)[:, None] + fill,
                         n_rows - MOE_ROWS + fill).reshape(-1)
    xg = _scatter_rows(hf, jnp.concatenate([dest, pad_dest]), n_rows)
    y = _experts(xg, group_expert, n_valid, w1, w3, w2)
    return _combine(x1, _gather_rows(y, dest), gate.T)


def _row(v, pad=0):
    v = v.reshape(1, -1).astype(F32)
    return jnp.pad(v, ((0, 0), (0, pad))) if pad else v


def _rows_at(w, start, total):
    return jnp.pad(w.astype(F32), ((start, total - start - w.shape[0]), (0, 0)))


def kernel(x, w_in, norm_mix, rwkv_mu, rwkv_decay_up, rwkv_w0, rwkv_a_up, rwkv_a0, rwkv_gate_up, rwkv_k_k, rwkv_k_a, rwkv_r_k, rwkv_ln_w, rwkv_ln_b, vres_down, vres_up, vres_v0, ssm_conv_w, ssm_conv_b, ssm_dt_bias, ssm_a_log, ssm_d, ssm_norm_w, att_q_gain, att_k_gain, att_rel_bias, gla_gate_up, gla_gate_bias, gla_norm_w, w_branch, w_out, norm_ffn, ffn_w1, ffn_w3, ffn_w2, moe_router, moe_w1, moe_w3, moe_w2):
    bsz, seq, _ = x.shape
    n_tok = bsz * seq
    depth = w_in.shape[0]
    x2 = x.reshape(n_tok, D_MODEL)
    v_first = None
    for l in range(depth):
        n_cols = w_in.shape[2]
        w_t = jnp.swapaxes(w_in, 1, 2).reshape(depth * n_cols, D_MODEL)
        patches = jnp.zeros((2, PACK_ROWS, D_MODEL), F32)
        if l > 0:
            lo = RWKV_COLS % PACK_ROWS
            patches = patches.at[1, lo:lo + 32].set(vres_down[l - 1].T)
        w_mix = _pack_weight_rows(w_t, _mixer_plan(l * n_cols), patches)
        o_gate = RWKV_COLS + SSM_COLS + ATT_COLS + GLA_COLS
        wg = _pack_weight_rows(w_t, [(l * n_cols + r, GATE_PACK_ROWS, 0) for r in range(o_gate, n_cols, GATE_PACK_ROWS)],
                               jnp.zeros((1, GATE_PACK_ROWS, D_MODEL), F32))
        moe = l % 2 == 1
        up_w = [(moe_w3, l // 2)] if moe else [(ffn_w1, l // 2), (ffn_w3, l // 2)]
        h, za, zb, zc, zd, wb_b, wo_b, *up_b = _inproj(x2, _row(norm_mix[l]), w_mix,
                                                       cast=[(w_branch, l), (w_out, l)] + up_w)
        shp = lambda z: z.reshape(bsz, seq, z.shape[-1])

        rw = dict(mu=_row(rwkv_mu[l], RWKV_PAD - RWKV_COLS),
                  dup=_rows_at(rwkv_decay_up[l], 0, LANES), w0=_row(rwkv_w0[l]),
                  aup=_rows_at(rwkv_a_up[l], 64, LANES), a0=_row(rwkv_a0[l]),
                  gup=_rows_at(rwkv_gate_up[l], 0, 256).astype(BF16),
                  kk=_row(rwkv_k_k[l]), ka=_row(rwkv_k_a[l]), rk=_row(rwkv_r_k[l]),
                  lnw=_row(rwkv_ln_w[l]), lnb=_row(rwkv_ln_b[l]))
        if l > 0:
            rw.update(vup=_rows_at(vres_up[l - 1], 160, 256), v0=_row(vres_v0[l - 1]))
        ssm = dict(cw=ssm_conv_w[l], cb=_row(ssm_conv_b[l]), dtb=_row(ssm_dt_bias[l], LANES - SSM_HEADS),
                   alog=_row(ssm_a_log[l], LANES - SSM_HEADS), dexp=_row(jnp.repeat(ssm_d[l], CHUNK)),
                   nw=_row(ssm_norm_w[l]))
        gla = dict(gup=_rows_at(gla_gate_up[l], 0, LANES), gb=_row(gla_gate_bias[l]), nw=_row(gla_norm_w[l]))
        (o_c,), (o_d,), (o_b,), res_a = _run_parts(
            [_attention_part(shp(zc), att_q_gain[l], att_k_gain[l], att_rel_bias, MIXER_BLOCK),
             _gla_part(shp(zd), gla, MIXER_BLOCK), _mamba_part(shp(zb), ssm, MIXER_BLOCK),
             _rwkv_part(shp(za), v_first, rw, MIXER_BLOCK)],
            grid=(bsz, seq // MIXER_BLOCK), name="mixers")
        o_a = res_a[0]
        if l == 0:
            v_first = res_a[1]
        outs = [o.reshape(n_tok, BRANCH_DIM) for o in (o_a, o_b, o_c, o_d)]
        merge_w = (wg, wb_b, wo_b, _row(norm_ffn[l]))
        if not moe:
            x1, hf, *w1_b = _merge(x2, h, outs, *merge_w, cast=[(moe_w1, l // 2)] if l + 1 < depth else [])
            x2 = _ffn(x1, hf, *up_b, ffn_w2[l // 2].astype(BF16))
        else:
            x1, *routing, w2_b = _merge(x2, h, outs, *merge_w, wr_t=moe_router[l // 2].T, cast=[(moe_w2, l // 2)])
            x2 = _moe(x1, routing, w1_b[0], up_b[0], w2_b)
    return x2.reshape(bsz, seq, D_MODEL)
```

```python
import functools
from typing import Any, NamedTuple

import jax
import jax.numpy as jnp
from jax import lax
from jax.experimental import pallas as pl
from jax.experimental.pallas import tpu as pltpu
from jax.experimental.pallas import tpu_sc as plsc

F32 = jnp.float32
BF16 = jnp.bfloat16

D_MODEL = 1024
CHUNK = 64
BRANCH_DIM = 512
NORM_EPS = 1e-6
LANES = 128
VMEM_LIMIT = 56 * 1024 * 1024
BIG_VMEM_LIMIT = 61 * 1024 * 1024

LOG2_E = 1.4426950408889634

RWKV_LN_EPS = 64e-5
RWKV_DECAY_SCALE = 0.6065306597126334
RWKV_COLS = 1824
RWKV_PAD = 1920
SSM_COLS = 1544
SSM_PAD = 1664
SSM_HEADS = 8
SSM_STATE = 128
ATT_COLS = 1536
ATT_HEADS = 8
ATT_LEFT = 8 * CHUNK
REL_CLIP = 2 * CHUNK
GLA_COLS = 1552
GLA_PAD = 1664
GLA_GATE_NORM = 16.0
FFN_DIM = 2816
N_EXPERTS = 8
EXPERT_DIM = 3584
MOE_ROWS = 512
ROW_SPLIT = 2
SUBROW = D_MODEL // (2 * ROW_SPLIT)


def _dot(a, b):
    return jnp.dot(a.astype(BF16), b.astype(BF16), preferred_element_type=F32)


def _dot_nt(a, b):
    return lax.dot_general(a.astype(BF16), b.astype(BF16), (((1,), (1,)), ((), ())),
                           preferred_element_type=F32)


def _dot_tn(a, b):
    return lax.dot_general(a.astype(BF16), b.astype(BF16), (((0,), (0,)), ((), ())),
                           preferred_element_type=F32)


def _hi_lo(a):
    hi = a.astype(BF16)
    return hi, (a - hi.astype(F32)).astype(BF16)


def _dot_3x(a, b):
    a_hi, a_lo = _hi_lo(a)
    b_hi, b_lo = _hi_lo(b)
    return (jnp.dot(a_hi, b_hi, preferred_element_type=F32) + jnp.dot(a_lo, b_hi, preferred_element_type=F32)
            + jnp.dot(a_hi, b_lo, preferred_element_type=F32))


def _dot_split(a, m):
    m = m.astype(BF16)
    hi, lo = _hi_lo(a)
    return jnp.dot(hi, m, preferred_element_type=F32) + jnp.dot(lo, m, preferred_element_type=F32)


def _dot_split_l(m, a):
    m = m.astype(BF16)
    hi, lo = _hi_lo(a)
    return jnp.dot(m, hi, preferred_element_type=F32) + jnp.dot(m, lo, preferred_element_type=F32)


def _pack_rows(x):
    bits = pltpu.bitcast(x.astype(BF16).astype(F32), jnp.uint32)
    half = D_MODEL // 2
    return [(bits[:, p * SUBROW:(p + 1) * SUBROW] & jnp.uint32(0xFFFF0000))
            | (bits[:, half + p * SUBROW:half + (p + 1) * SUBROW] >> jnp.uint32(16)) for p in range(ROW_SPLIT)]


def _unpack_rows(planes):
    hi = [pltpu.bitcast(w & jnp.uint32(0xFFFF0000), F32) for w in planes]
    lo = [pltpu.bitcast(w << jnp.uint32(16), F32) for w in planes]
    return jnp.concatenate(hi + lo, axis=-1)


def _softplus(x):
    return jnp.maximum(x, 0.0) + jnp.log(1.0 + jnp.exp(-jnp.abs(x)))


def _sigmoid(x):
    return 1.0 / (1.0 + jnp.exp(-x))


def _silu(x):
    return x * _sigmoid(x)


def _iota2(shape, axis):
    return lax.broadcasted_iota(jnp.int32, shape, axis)


def _group_mean_matrix(n, group):
    r = _iota2((n, n), 0) // group
    c = _iota2((n, n), 1) // group
    return jnp.where(r == c, 1.0 / group, 0.0).astype(F32)


def _chunk_tri(n):
    r = _iota2((n, n), 0)
    c = _iota2((n, n), 1)
    return jnp.where((r // CHUNK == c // CHUNK) & (r >= c), 1.0, 0.0).astype(F32)


def _resident(shape):
    nd = len(shape)
    return pl.BlockSpec(shape, lambda *_: (0,) * nd, pipeline_mode=pl.Buffered(1))


def _params(*sem, vmem_limit=VMEM_LIMIT):
    return pltpu.CompilerParams(dimension_semantics=sem, vmem_limit_bytes=vmem_limit)


_DONE = object()


class _Part(NamedTuple):
    body: Any
    inputs: list
    in_specs: list
    out_shapes: list
    out_specs: list
    scratch_shapes: list


def _run_parts(parts, grid, name):
    def split(refs, counts):
        out, lo = [], 0
        for n in counts:
            out.append(refs[lo:lo + n])
            lo += n
        return out

    n_in = [len(p.inputs) for p in parts]
    n_out = [len(p.out_shapes) for p in parts]
    n_scr = [len(p.scratch_shapes) for p in parts]

    def kernel(*refs):
        ins = split(refs[:sum(n_in)], n_in)
        outs = split(refs[sum(n_in):sum(n_in) + sum(n_out)], n_out)
        scr = split(refs[sum(n_in) + sum(n_out):], n_scr)
        stages = [part.body(*i, *o, *c) for part, i, o, c in zip(parts, ins, outs, scr)]
        while stages:
            for stage in list(stages):
                if next(stage, _DONE) is _DONE:
                    stages.remove(stage)

    flat = lambda field: [x for p in parts for x in getattr(p, field)]
    res = pl.pallas_call(
        kernel,
        grid=grid,
        in_specs=flat("in_specs"),
        out_specs=flat("out_specs"),
        out_shape=flat("out_shapes"),
        scratch_shapes=flat("scratch_shapes"),
        compiler_params=_params("parallel", "arbitrary", vmem_limit=BIG_VMEM_LIMIT),
        name=name,
    )(*flat("inputs"))
    return split(list(res), n_out)


def _cast_rider(stack, index, steps):
    shape = stack.shape[1:]
    cols = shape[-1]
    rows = stack[0].size // cols // steps
    assert stack[0].size // cols % steps == 0 and rows % 16 == 0
    return (stack.reshape(stack.shape[0] * steps, rows, cols),
            pl.BlockSpec((None, rows, cols), lambda i: (index * steps + i, 0, 0)),
            pl.BlockSpec((None, rows, cols), lambda i: (i, 0, 0)),
            jax.ShapeDtypeStruct((steps, rows, cols), BF16), shape)


def _cast_slices(src_refs, dst_refs):
    for src, dst in zip(src_refs, dst_refs):
        dst[...] = src[...].astype(dst.dtype)


MIXER_COLS = (RWKV_PAD, SSM_PAD, ATT_COLS, GLA_PAD)
MIXER_DTYPES = (F32, F32, BF16, F32)
MIXER_BLOCK = 256


PACK_ROWS = 128
MIXER_PACK_BLOCKS = 4
GATE_PACK_ROWS = 512


def _pack_kernel(n_sub, src_ref, keep_ref, fill_ref, *refs):
    del src_ref, fill_ref
    w_refs, patch_refs, o_ref = refs[:n_sub], refs[n_sub:2 * n_sub], refs[2 * n_sub]
    rows = w_refs[0].shape[0]
    row = _iota2(w_refs[0].shape, 0)
    for k in range(n_sub):
        keep = keep_ref[pl.program_id(0) * n_sub + k]
        o_ref[k * rows:(k + 1) * rows, :] = jnp.where(row < keep, w_refs[k][...], patch_refs[k][...]).astype(o_ref.dtype)


def _pack_weight_rows(w_t, plan, patches, n_sub=1):
    plan = plan + [plan[-1]] * (-len(plan) % n_sub)
    src, keep, fill = (jnp.asarray(v, jnp.int32) for v in zip(*plan))
    rows, cols = patches.shape[1:]
    sub = lambda k: pl.BlockSpec((pl.Element(rows), pl.Element(cols)),
                                 lambda j, src, keep, fill: (pl.multiple_of(src[j * n_sub + k], 8), 0))
    patch = lambda k: pl.BlockSpec((None, rows, cols), lambda j, src, keep, fill: (fill[j * n_sub + k], 0, 0))
    return pl.pallas_call(
        functools.partial(_pack_kernel, n_sub),
        grid_spec=pltpu.PrefetchScalarGridSpec(
            num_scalar_prefetch=3,
            grid=(len(plan) // n_sub,),
            in_specs=[sub(k) for k in range(n_sub)] + [patch(k) for k in range(n_sub)],
            out_specs=pl.BlockSpec((n_sub * rows, cols), lambda j, src, keep, fill: (j, 0)),
        ),
        out_shape=jax.ShapeDtypeStruct((len(plan) * rows, cols), BF16),
        compiler_params=_params("parallel"),
        name="pack_weights",
    )(src, keep, fill, *[w_t] * n_sub, *[patches] * n_sub)


def _mixer_plan(base):
    o_ss, o_at = RWKV_COLS + SSM_COLS, RWKV_COLS + SSM_COLS + ATT_COLS
    plan = []

    def rows(lo, hi, fill=0):
        for r in range(lo, hi, PACK_ROWS):
            plan.append((base + r, min(PACK_ROWS, hi - r), fill))

    rows(0, RWKV_COLS, fill=1)
    rows(RWKV_COLS, o_ss)
    rows(o_ss, o_at)
    rows(o_at, o_at + 1024)
    rows(o_at + 1040, o_at + GLA_COLS)
    rows(o_at + 1024, o_at + 1040)
    assert len(plan) * PACK_ROWS == sum(MIXER_COLS)
    return plan


def _inproj_kernel(n_cast, x_ref, g_ref, w_ref, *refs):
    cast_in, (h_ref, *z_refs), cast_out = refs[:n_cast], refs[n_cast:len(refs) - n_cast], refs[len(refs) - n_cast:]
    _cast_slices(cast_in, cast_out)
    x = x_ref[...]
    ms = jnp.mean(x * x, axis=-1, keepdims=True)
    h = (x * lax.rsqrt(ms + NORM_EPS) * g_ref[...]).astype(BF16)
    h_ref[...] = h
    lo = 0
    for pair in (z_refs[:2], z_refs[2:]):
        widths = [z_ref.shape[-1] for z_ref in pair]
        z = _dot_nt(h, w_ref[lo:lo + sum(widths), :])
        pair[0][...] = z[:, :widths[0]].astype(pair[0].dtype)
        pair[1][...] = z[:, widths[0]:].astype(pair[1].dtype)
        lo += sum(widths)


def _inproj(x2, g, w_mix, cast=(), tm=512):
    n_tok = x2.shape[0]
    row = lambda n: pl.BlockSpec((tm, n), lambda i: (i, 0))
    riders = [_cast_rider(stack, index, n_tok // tm) for stack, index in cast]
    res = pl.pallas_call(
        functools.partial(_inproj_kernel, len(cast)),
        grid=(n_tok // tm,),
        in_specs=[row(D_MODEL), _resident((1, D_MODEL)), _resident(w_mix.shape)] + [r[1] for r in riders],
        out_specs=[row(D_MODEL)] + [row(n) for n in MIXER_COLS] + [r[2] for r in riders],
        out_shape=[jax.ShapeDtypeStruct((n_tok, D_MODEL), BF16)]
        + [jax.ShapeDtypeStruct((n_tok, n), dt) for n, dt in zip(MIXER_COLS, MIXER_DTYPES)] + [r[3] for r in riders],
        compiler_params=_params("parallel", vmem_limit=BIG_VMEM_LIMIT if cast else VMEM_LIMIT),
        name="inproj",
    )(x2, g, w_mix, *[r[0] for r in riders])
    n_plain = len(res) - len(cast)
    return list(res[:n_plain]) + [wb16.reshape(r[4]) for wb16, r in zip(res[n_plain:], riders)]


def _rwkv_block(r, k, v, kk, ka, ld, tb, state_ref, y_ref):
    n_chunks = tb // CHUNK
    cum = _dot_split_l(_chunk_tri(tb), ld)
    p_inv = jnp.exp(-cum)
    a_t = -kk * jnp.exp(cum - ld)
    b_t = kk * ka * p_inv
    k_t = k * p_inv
    r_t = r * jnp.exp(cum)
    bf = lambda z: z.astype(BF16)
    a_b, b_b, k_b, r_b, v_b = bf(a_t), bf(b_t), bf(k_t), bf(r_t), bf(v)
    yield

    lane_head = _iota2((CHUNK, LANES), 1) // CHUNK
    row = _iota2((CHUNK, LANES), 0)
    col = _iota2((CHUNK, LANES), 1) % CHUNK
    strict = row > col
    incl = row >= col
    bd_mask = (_iota2((LANES, LANES), 0) // CHUNK) == (_iota2((LANES, LANES), 1) // CHUNK)
    eye = jnp.where(_iota2((LANES, LANES), 0) == _iota2((LANES, LANES), 1), 1.0, 0.0).astype(F32)

    def stack(z):
        return jnp.concatenate([jnp.where(lane_head == 0, z, 0.0), jnp.where(lane_head == 1, z, 0.0)], axis=0)

    def sub(z, u):
        c, p = u
        return z[c * CHUNK:(c + 1) * CHUNK, p * LANES:(p + 1) * LANES]

    def prepare(units):
        gram = {u: _dot_nt(jnp.concatenate([sub(a_b, u), sub(r_b, u)], axis=0),
                           jnp.concatenate([stack(sub(b_b, u)), stack(sub(k_b, u))], axis=0))
                for u in units}
        yield
        a_rb = {u: bf(jnp.where(incl, gram[u][CHUNK:, :LANES], 0.0)) for u in units}
        a_akrk = {u: bf(jnp.concatenate([jnp.where(strict, gram[u][:CHUNK, LANES:], 0.0),
                                         jnp.where(incl, gram[u][CHUNK:, LANES:], 0.0)], axis=0)) for u in units}
        pw = {u: stack(jnp.where(strict, gram[u][:CHUNK, :LANES], 0.0)) for u in units}
        t_bd = {u: eye + pw[u] for u in units}
        pw = {u: _dot(pw[u], pw[u]) for u in units}
        yield
        for _ in range(CHUNK.bit_length() - 3):
            sp = {u: _dot(jnp.concatenate([t_bd[u], pw[u]], axis=0), pw[u]) for u in units}
            t_bd = {u: t_bd[u] + sp[u][:LANES] for u in units}
            pw = {u: sp[u][LANES:] for u in units}
            yield
        t_bd = {u: bf(t_bd[u] + _dot(t_bd[u], pw[u])) for u in units}
        v_bd = {u: stack(sub(v_b, u)) for u in units}
        avyv = {u: _dot(a_akrk[u], v_bd[u]) for u in units}
        yield
        wuv = {u: _dot(t_bd[u], jnp.concatenate([stack(sub(a_b, u)), stack(bf(avyv[u][:CHUNK]))], axis=1))
               for u in units}
        wr = {u: jnp.concatenate([bf(wuv[u][:, :LANES]), sub(r_b, u)], axis=0) for u in units}
        yield
        return a_rb, avyv, wuv, wr

    a_rb, avyv, wuv, wr = yield from prepare([(c, p) for c in range(n_chunks) for p in range(4)])

    h = [state_ref[p] for p in range(4)]
    for c in range(n_chunks):
        last = c * CHUNK + CHUNK - 1
        p_last = jnp.exp(cum[last:last + 1, :])
        hs = [_dot_nt(wr[(c, p)], h[p]) for p in range(4)]
        u2 = [hs[p][:LANES] + wuv[(c, p)][:, LANES:] for p in range(4)]
        u_p = [u2[p][:CHUNK] + u2[p][CHUNK:] for p in range(4)]
        upd = [_dot_tn(jnp.concatenate([u_p[p], sub(v, (c, p))], axis=0),
                       jnp.concatenate([sub(b_t, (c, p)), sub(k_t, (c, p))], axis=0)
                       * p_last[:, p * LANES:(p + 1) * LANES]) for p in range(4)]
        for p in range(4):
            y_ref[c * CHUNK:(c + 1) * CHUNK, p * LANES:(p + 1) * LANES] = (
                hs[p][LANES:] + _dot(a_rb[(c, p)], stack(u_p[p])) + avyv[(c, p)][CHUNK:])
        h = [h[p] * p_last[:, p * LANES:(p + 1) * LANES] + jnp.where(bd_mask, upd[p], 0.0) for p in range(4)]
        yield
    for p in range(4):
        state_ref[p] = h[p]


def _rwkv_kernel(has_vres, tb, *refs):
    if has_vres:
        (za_ref, vfirst_ref, mu_ref, dup_ref, w0_ref, aup_ref, a0_ref, gup_ref, vup_ref, v0_ref,
         kk_ref, ka_ref, rk_ref, lnw_ref, lnb_ref, o_ref, prev_ref, state_ref, y_ref) = refs
    else:
        (za_ref, mu_ref, dup_ref, w0_ref, aup_ref, a0_ref, gup_ref,
         kk_ref, ka_ref, rk_ref, lnw_ref, lnb_ref, o_ref, vraw_ref, prev_ref, state_ref, y_ref) = refs

    @pl.when(pl.program_id(1) == 0)
    def _():
        prev_ref[...] = jnp.zeros_like(prev_ref)
        state_ref[...] = jnp.zeros_like(state_ref)

    yield
    za = za_ref[...]
    shifted = pltpu.roll(za, 1, axis=0)
    shifted = jnp.where(_iota2(za.shape, 0) == 0, prev_ref[7:8, :], shifted)
    prev_ref[...] = za[tb - 8:, :]
    za = za + (shifted - za) * mu_ref[...]

    r = za[:, 0:512]
    k = za[:, 512:1024]
    v = za[:, 1024:1536]
    xwa = za[:, 1536:1664]
    xg = za[:, 1664:1920]
    ld = -RWKV_DECAY_SCALE * _sigmoid(w0_ref[...] + _dot_3x(jnp.tanh(xwa), dup_ref[...]))
    yield
    a = _sigmoid(a0_ref[...] + _dot(xwa, aup_ref[...]))
    g = _dot(_sigmoid(xg), gup_ref[...])
    if has_vres:
        v_mix = _sigmoid(v0_ref[...] + _dot(xg, vup_ref[...]))
        v = v + (vfirst_ref[...] - v) * v_mix
    else:
        vraw_ref[...] = v
    yield
    kk = k * kk_ref[...]
    k = k * (1.0 + (a - 1.0) * ka_ref[...])
    head_sum = _group_mean_matrix(512, CHUNK) * float(CHUNK)
    kk = kk / jnp.maximum(jnp.sqrt(_dot(kk * kk, head_sum)), 1e-12)
    yield

    yield from _rwkv_block(r, k, v, kk, a, ld, tb, state_ref, y_ref)
    y = y_ref[...]
    head_mean = _group_mean_matrix(512, CHUNK)
    mean = _dot_split(y, head_mean)
    yc = y - mean
    var = _dot(yc * yc, head_mean)
    y = yc * lax.rsqrt(var + RWKV_LN_EPS) * lnw_ref[...] + lnb_ref[...]
    y = y + _dot(r * k * rk_ref[...], head_sum) * v
    o_ref[...] = (y * g).astype(o_ref.dtype)


def _rwkv_part(za, v_first, prm, tb):
    bsz, seq, _ = za.shape
    has_vres = v_first is not None
    blk = lambda n: pl.BlockSpec((None, tb, n), lambda b, i: (b, i, 0))
    names = (["mu", "dup", "w0", "aup", "a0", "gup"] + (["vup", "v0"] if has_vres else [])
             + ["kk", "ka", "rk", "lnw", "lnb"])
    weights = [prm[n] for n in names]
    ins = [za] + ([v_first] if has_vres else []) + weights
    in_specs = [blk(RWKV_PAD)] + ([blk(512)] if has_vres else []) + [_resident(w.shape) for w in weights]
    out_shape = [jax.ShapeDtypeStruct((bsz, seq, 512), BF16)]
    out_specs = [blk(512)]
    if not has_vres:
        out_shape.append(jax.ShapeDtypeStruct((bsz, seq, 512), F32))
        out_specs.append(blk(512))
    return _Part(functools.partial(_rwkv_kernel, has_vres, tb), ins, in_specs, out_shape, out_specs,
                 [pltpu.VMEM((8, RWKV_PAD), F32), pltpu.VMEM((4, LANES, LANES), F32), pltpu.VMEM((tb, 512), F32)])


def _mamba_kernel(tb, zb_ref, cw_ref, cb_ref, dtb_ref, alog_ref, dexp_ref, nw_ref,
                  o_ref, xbuf_ref, state_ref, y_ref):
    @pl.when(pl.program_id(1) == 0)
    def _():
        xbuf_ref[0:8, :] = jnp.zeros((8, 1024), F32)
        state_ref[...] = jnp.zeros_like(state_ref)

    yield
    xbuf_ref[8:8 + tb, :] = zb_ref[:, 512:1536]
    conv = cb_ref[...]
    for i in range(4):
        conv = conv + cw_ref[i:i + 1, :] * xbuf_ref[5 + i:5 + i + tb, :]
    xbuf_ref[0:8, :] = xbuf_ref[tb:tb + 8, :]
    xbc = _silu(conv)
    xs = xbc[:, 0:512]
    bm = xbc[:, 512:768]
    cm = xbc[:, 768:1024]
    yield

    lane = _iota2((1, LANES), 1)
    dt = _softplus(zb_ref[:, 1536:1664] + dtb_ref[...])
    a_neg = jnp.where(lane < SSM_HEADS, -jnp.exp(alog_ref[...]), 0.0)
    acs = _dot_split_l(_chunk_tri(tb), dt * a_neg)
    expand = jnp.where(_iota2((LANES, 512), 0) == _iota2((LANES, 512), 1) // CHUNK, 1.0, 0.0)
    dt_e = _dot(dt, expand)
    acs_e = _dot_split(acs, expand)
    xdt = xs * dt_e
    yield

    n_chunks = tb // CHUNK
    causal = _iota2((CHUNK, 512), 0) >= _iota2((CHUNK, 512), 1) % CHUNK
    pair_head = _iota2((CHUNK, LANES), 1) // CHUNK
    spread = jnp.where(_iota2((CHUNK, 512), 0) == _iota2((CHUNK, 512), 1) % CHUNK, 1.0, 0.0)
    own_head = _iota2((SSM_HEADS, 512), 0) == _iota2((SSM_HEADS, 512), 1) // CHUNK
    rows_of = lambda c: slice(c * CHUNK, (c + 1) * CHUNK)
    grp = lambda g: slice(g * SSM_STATE, (g + 1) * SSM_STATE)

    def stack(z):
        return jnp.concatenate([jnp.where(pair_head == 0, z, 0.0), jnp.where(pair_head == 1, z, 0.0)], axis=0)

    y_intra, upd, e_in, e_last = [], [], [], []
    for c in range(n_chunks):
        rows = rows_of(c)
        acs_ec = acs_e[rows]
        acs_t = acs[rows].T[0:SSM_HEADS]
        acs_row = jnp.sum(jnp.where(own_head, _dot_split(acs_t, spread), 0.0), axis=0, keepdims=True)
        decay = jnp.exp(jnp.where(causal, acs_ec - acs_row, -jnp.inf))
        cb = jnp.concatenate(
            [_dot_nt(cm[rows, grp(g)], jnp.concatenate([bm[rows, grp(g)]] * 4, axis=0)) for g in range(2)], axis=1)
        m = cb * decay
        yield
        xdt_c = xdt[rows]
        y_intra.append([_dot(m[:, p * LANES:(p + 1) * LANES], stack(xdt_c[:, p * LANES:(p + 1) * LANES]))
                        for p in range(4)])
        last_e = acs_ec[CHUNK - 1:CHUNK, :]
        x_out = xdt_c * jnp.exp(last_e - acs_ec)
        upd.append([_dot_tn(bm[rows, grp(g)], x_out[:, g * 256:(g + 1) * 256]) for g in range(2)])
        e_in.append(jnp.exp(acs_ec))
        e_last.append(jnp.exp(last_e))
        yield

    state = [state_ref[g] for g in range(2)]
    for c in range(n_chunks):
        rows = rows_of(c)
        for g in range(2):
            ls = slice(g * 256, (g + 1) * 256)
            y_in = jnp.concatenate(y_intra[c][2 * g:2 * g + 2], axis=1)
            y_ref[rows, ls] = y_in + _dot(cm[rows, grp(g)], state[g]) * e_in[c][:, ls]
            state[g] = state[g] * e_last[c][:, ls] + upd[c][g]
        yield
    for g in range(2):
        state_ref[g] = state[g]

    y = (y_ref[...] + xs * dexp_ref[...]) * _silu(zb_ref[:, 0:512])
    for g in range(2):
        ls = slice(g * 256, (g + 1) * 256)
        yg = y[:, ls]
        ms = jnp.mean(yg * yg, axis=-1, keepdims=True)
        o_ref[:, ls] = (yg * lax.rsqrt(ms + NORM_EPS) * nw_ref[:, ls]).astype(o_ref.dtype)


def _mamba_part(zb, prm, tb):
    bsz, seq, _ = zb.shape
    blk = lambda n: pl.BlockSpec((None, tb, n), lambda b, i: (b, i, 0))
    weights = [prm[n] for n in ("cw", "cb", "dtb", "alog", "dexp", "nw")]
    return _Part(functools.partial(_mamba_kernel, tb), [zb] + weights,
                 [blk(SSM_PAD)] + [_resident(w.shape) for w in weights],
                 [jax.ShapeDtypeStruct((bsz, seq, 512), BF16)], [blk(512)],
                 [pltpu.VMEM((tb + 8, 1024), F32), pltpu.VMEM((2, SSM_STATE, 256), F32), pltpu.VMEM((tb, 512), F32)])


def _attn_kernel(tq, qkv_ref, qg_ref, kg_ref, bias_ref, o_ref, kn_ref, vb_ref):
    i = pl.program_id(1)
    head_mean = _group_mean_matrix(512, CHUNK)
    win = tq + ATT_LEFT

    @pl.when(i == 0)
    def _():
        kn_ref[0:ATT_LEFT, :] = jnp.zeros((ATT_LEFT, 512), BF16)
        vb_ref[0:ATT_LEFT, :] = jnp.zeros((ATT_LEFT, 512), BF16)

    yield
    start = pl.multiple_of(i * tq, tq)
    k = qkv_ref[:, 512:1024].astype(F32)
    ms = _dot(k * k, head_mean)
    kn_ref[pl.ds(ATT_LEFT + start, tq), :] = (k * lax.rsqrt(ms + NORM_EPS) * kg_ref[...]).astype(BF16)
    vb_ref[pl.ds(ATT_LEFT + start, tq), :] = qkv_ref[:, 1024:1536].astype(BF16)
    q = qkv_ref[:, 0:512].astype(F32)
    ms = _dot(q * q, head_mean)
    qn = q * lax.rsqrt(ms + NORM_EPS) * qg_ref[...] * (CHUNK ** -0.5 * LOG2_E)
    kwin = kn_ref[pl.ds(start, win), :]
    vwin = vb_ref[pl.ds(start, win), :]
    lane_head = _iota2((tq, LANES), 1) // CHUNK
    yield
    for p in range(4):
        ls = slice(p * LANES, (p + 1) * LANES)
        q2 = jnp.concatenate([jnp.where(lane_head == s, qn[:, ls], 0.0) for s in range(2)], axis=0)
        sc = _dot_nt(q2, kwin[:, ls]) + bias_ref[2 * p:2 * p + 2].reshape(2 * tq, win).astype(F32)
        yield
        e = jnp.exp2(sc - jnp.max(sc, axis=-1, keepdims=True))
        yield
        o2 = _dot(e, vwin[:, ls]) / jnp.sum(e, axis=-1, keepdims=True)
        o_ref[:, ls] = jnp.where(lane_head == 0, o2[:tq], o2[tq:]).astype(o_ref.dtype)
        yield


def _band_bias(rel_bias, tq, win):
    period = tq + win
    m = jnp.arange(period)
    d = jnp.where(m < win, m, m - period) - ATT_LEFT
    f = (rel_bias.astype(F32)[:, jnp.clip(d, -REL_CLIP, REL_CLIP) + REL_CLIP] * LOG2_E).astype(BF16)
    g = jnp.tile(f, (1, tq))[:, :tq * (period - 1)].reshape(-1, tq, period - 1)
    bias = g[:, :, :win]
    left = ATT_LEFT // CHUNK
    qc = left + jnp.arange(tq)[:, None] // CHUNK
    kc = jnp.arange(win)[None, :] // CHUNK
    tiles = []
    for blk in range(ATT_LEFT // tq + 1):
        first_kc = jnp.maximum(qc - left, left - blk * (tq // CHUNK))
        tiles.append(jnp.where((kc <= qc) & (kc >= first_kc), bias, -jnp.inf))
    return jnp.stack(tiles)


def _attention_part(zc, q_gain, k_gain, rel_bias, tq):
    bsz, seq, _ = zc.shape
    win = tq + ATT_LEFT
    bias = _band_bias(rel_bias, tq, win)
    qg = jnp.tile(q_gain, ATT_HEADS)[None, :]
    kg = jnp.tile(k_gain, ATT_HEADS)[None, :]
    last_tile = bias.shape[0] - 1
    bias_spec = pl.BlockSpec((None,) + bias.shape[1:], lambda b, i: (jnp.minimum(i, last_tile), 0, 0, 0))
    blk = lambda n: pl.BlockSpec((None, tq, n), lambda b, i: (b, i, 0))
    return _Part(functools.partial(_attn_kernel, tq), [zc, qg, kg, bias],
                 [blk(ATT_COLS), _resident(qg.shape), _resident(kg.shape), bias_spec],
                 [jax.ShapeDtypeStruct((bsz, seq, 512), BF16)], [blk(512)],
                 [pltpu.VMEM((ATT_LEFT + seq, 512), BF16), pltpu.VMEM((ATT_LEFT + seq, 512), BF16)])


def _gla_kernel(tb, zd_ref, gup_ref, gb_ref, nw_ref, o_ref, state_ref, y_ref):
    @pl.when(pl.program_id(1) == 0)
    def _():
        state_ref[...] = jnp.zeros_like(state_ref)

    yield
    q = zd_ref[:, 0:256] * (CHUNK ** -0.5)
    k = zd_ref[:, 256:512]
    v = zd_ref[:, 512:1024]
    log_a = -_softplus(-(_dot_3x(zd_ref[:, 1536:1664], gup_ref[...]) + gb_ref[...])) / GLA_GATE_NORM
    bcum = _dot_split_l(_chunk_tri(tb), log_a)

    k_head = _iota2((CHUNK, 256), 1) // CHUNK
    v_head = _iota2((CHUNK, 512), 1) // LANES
    causal = _iota2((CHUNK, 256), 0) >= _iota2((CHUNK, 256), 1) % CHUNK
    bd = (_iota2((512, 256), 0) // LANES) == (_iota2((512, 256), 1) // CHUNK)
    n_chunks = tb // CHUNK
    qg_all = (q * jnp.exp(bcum)).astype(BF16)
    kg_all = k * jnp.exp(-bcum)
    yield
    qg, att, o_intra, upd, e_last = [], [], [], [], []
    for c in range(n_chunks):
        rows = slice(c * CHUNK, (c + 1) * CHUNK)
        kg_bd = jnp.concatenate([jnp.where(k_head == h, kg_all[rows], 0.0) for h in range(4)], axis=0)
        qg.append(qg_all[rows])
        att.append(jnp.where(causal, _dot_nt(qg[c], kg_bd), 0.0))
        yield
    for c in range(n_chunks):
        rows = slice(c * CHUNK, (c + 1) * CHUNK)
        bc = bcum[rows]
        blast = bc[CHUNK - 1:CHUNK, :]
        v_c = v[rows]
        v_bd = jnp.concatenate([jnp.where(v_head == h, v_c, 0.0) for h in range(4)], axis=0)
        o_intra.append(_dot(att[c], v_bd))
        upd.append(jnp.where(bd, _dot_tn(v_c, k[rows] * jnp.exp(blast - bc)), 0.0))
        e_last.append(jnp.exp(blast))
        yield
    st = state_ref[...]
    for c in range(n_chunks):
        y_ref[c * CHUNK:(c + 1) * CHUNK, :] = o_intra[c] + _dot_nt(qg[c], st)
        st = st * e_last[c] + upd[c]
        yield
    state_ref[...] = st

    o = y_ref[...]
    for h in range(4):
        ls = slice(h * LANES, (h + 1) * LANES)
        oh = o[:, ls]
        ms = jnp.mean(oh * oh, axis=-1, keepdims=True)
        o_ref[:, ls] = (oh * lax.rsqrt(ms + NORM_EPS) * nw_ref[...]
                        * _silu(zd_ref[:, 1024 + h * LANES:1024 + (h + 1) * LANES])).astype(o_ref.dtype)


def _gla_part(zd, prm, tb):
    bsz, seq, _ = zd.shape
    blk = lambda n: pl.BlockSpec((None, tb, n), lambda b, i: (b, i, 0))
    weights = [prm[n] for n in ("gup", "gb", "nw")]
    return _Part(functools.partial(_gla_kernel, tb), [zd] + weights,
                 [blk(GLA_PAD)] + [_resident(w.shape) for w in weights],
                 [jax.ShapeDtypeStruct((bsz, seq, 512), BF16)], [blk(512)],
                 [pltpu.VMEM((512, 256), F32), pltpu.VMEM((tb, 512), F32)])


def _route(tm, hf, wr_ref, idx_ref, gate_ref, rank_ref, cnt_ref, hf_ref, carry_ref):
    @pl.when(pl.program_id(0) == 0)
    def _():
        carry_ref[...] = jnp.zeros_like(carry_ref)

    for j, words in enumerate(_pack_rows(hf)):
        hf_ref[j] = words
    nt = lambda a, b: lax.dot_general(a, b, (((1,), (1,)), ((), ())), preferred_element_type=F32)
    w_hi, w_lo = _hi_lo(wr_ref[...])
    h_hi, h_lo = _hi_lo(hf)
    logits = nt(w_hi, h_hi) + nt(w_lo, h_hi) + nt(w_hi, h_lo)
    e_iota = _iota2((N_EXPERTS, tm), 0)
    m1 = jnp.max(logits, axis=0, keepdims=True)
    i1 = jnp.min(jnp.where(logits == m1, e_iota, N_EXPERTS), axis=0, keepdims=True)
    rest = jnp.where(e_iota == i1, -jnp.inf, logits)
    m2 = jnp.max(rest, axis=0, keepdims=True)
    i2 = jnp.min(jnp.where(rest == m2, e_iota, N_EXPERTS), axis=0, keepdims=True)
    e2 = jnp.exp(m2 - m1)
    gate_ref[0:1, :] = 1.0 / (1.0 + e2)
    gate_ref[1:2, :] = e2 / (1.0 + e2)
    idx_ref[0:1, :] = i1
    idx_ref[1:2, :] = i2
    hit1 = jnp.where(e_iota == i1, 1.0, 0.0)
    hit2 = jnp.where(e_iota == i2, 1.0, 0.0)
    before = jnp.where(_iota2((tm, tm), 0) < _iota2((tm, tm), 1), 1.0, 0.0)
    prior = _dot(hit1 + hit2, before) + carry_ref[:, 0:1]
    rank_ref[0:1, :] = jnp.sum(hit1 * prior, axis=0, keepdims=True).astype(jnp.int32)
    rank_ref[1:2, :] = jnp.sum(hit2 * prior, axis=0, keepdims=True).astype(jnp.int32)
    carry_ref[...] = carry_ref[...] + jnp.sum(hit1 + hit2, axis=1, keepdims=True)
    cnt_ref[...] = carry_ref[...]


def _merge_kernel(tm, route, n_cast, *refs):
    n_in, n_out = (11, 6) if route else (10, 2)
    ins, cast_in = refs[:n_in], refs[n_in:n_in + n_cast]
    outs = refs[n_in + n_cast:n_in + n_cast + n_out]
    cast_out = refs[n_in + n_cast + n_out:n_in + 2 * n_cast + n_out]
    scratch = refs[n_in + 2 * n_cast + n_out:]
    x_ref, h_ref, oa_ref, ob_ref, oc_ref, od_ref, wg_ref, wb_ref, wo_ref, nf_ref = ins[:10]
    _cast_slices(cast_in, cast_out)
    h = h_ref[...]
    acc = jnp.zeros(x_ref.shape, F32)
    for i, o_ref in enumerate((oa_ref, ob_ref, oc_ref, od_ref)):
        gate = _sigmoid(_dot_nt(h, wg_ref[i * D_MODEL:(i + 1) * D_MODEL, :]))
        acc = acc + gate * jnp.dot(o_ref[...], wb_ref[i], preferred_element_type=F32)
    x1 = x_ref[...] + jnp.dot(acc.astype(BF16), wo_ref[...], preferred_element_type=F32)
    ms = jnp.mean(x1 * x1, axis=-1, keepdims=True)
    hf = x1 * lax.rsqrt(ms + NORM_EPS) * nf_ref[...]
    if route:
        _route(tm, hf, ins[10], *outs[1:], *scratch)
    else:
        outs[1][...] = hf.astype(BF16)
    outs[0][...] = x1


def _merge(x2, h, outs, wg, wb, wo, nf, wr_t=None, cast=(), tm=512):
    n_tok = x2.shape[0]
    route = wr_t is not None
    steps = n_tok // tm
    row = lambda n: pl.BlockSpec((tm, n), lambda i: (i, 0))
    ins = [x2, h, *outs, wg, wb, wo, nf]
    in_specs = ([row(D_MODEL), row(D_MODEL)] + [row(BRANCH_DIM)] * 4
                + [_resident(wg.shape), _resident(wb.shape), _resident(wo.shape), _resident(nf.shape)])
    out_specs = [row(D_MODEL)]
    out_shape = [jax.ShapeDtypeStruct((n_tok, D_MODEL), F32)]
    scratch = []
    if route:
        ins.append(wr_t)
        in_specs.append(_resident(wr_t.shape))
        col = pl.BlockSpec((2, tm), lambda i: (0, i))
        out_specs += [col, col, col, pl.BlockSpec((N_EXPERTS, LANES), lambda i: (0, 0)),
                      pl.BlockSpec((ROW_SPLIT, tm, SUBROW), lambda i: (0, i, 0))]
        out_shape += [jax.ShapeDtypeStruct((2, n_tok), jnp.int32), jax.ShapeDtypeStruct((2, n_tok), F32),
                      jax.ShapeDtypeStruct((2, n_tok), jnp.int32), jax.ShapeDtypeStruct((N_EXPERTS, LANES), F32),
                      jax.ShapeDtypeStruct((ROW_SPLIT, n_tok, SUBROW), jnp.uint32)]
        scratch = [pltpu.VMEM((N_EXPERTS, LANES), F32)]
    else:
        out_specs.append(row(D_MODEL))
        out_shape.append(jax.ShapeDtypeStruct((n_tok, D_MODEL), BF16))
    riders = [_cast_rider(stack, index, steps) for stack, index in cast]
    res = pl.pallas_call(
        functools.partial(_merge_kernel, tm, route, len(cast)),
        grid=(steps,),
        in_specs=in_specs + [r[1] for r in riders],
        out_specs=out_specs + [r[2] for r in riders],
        out_shape=out_shape + [r[3] for r in riders],
        scratch_shapes=scratch,
        compiler_params=_params("arbitrary" if route else "parallel"),
        name="merge",
    )(*ins, *[r[0] for r in riders])
    n_plain = len(res) - len(cast)
    return list(res[:n_plain]) + [wb16.reshape(r[4]) for wb16, r in zip(res[n_plain:], riders)]


def _ffn_kernel(tf, x1_ref, hf_ref, w1_ref, w3_ref, w2_ref, o_ref):
    hf = hf_ref[...]
    acc = x1_ref[...]
    for lo in range(0, FFN_DIM, tf):
        cols = slice(lo, min(lo + tf, FFN_DIM))
        a = jnp.dot(hf, w1_ref[:, cols], preferred_element_type=F32)
        b = jnp.dot(hf, w3_ref[:, cols], preferred_element_type=F32)
        acc = acc + jnp.dot((_silu(a) * b).astype(BF16), w2_ref[cols, :], preferred_element_type=F32)
    o_ref[...] = acc


def _ffn(x1, hf, w1, w3, w2, tm=512, tf=512):
    n_tok = x1.shape[0]
    row = pl.BlockSpec((tm, D_MODEL), lambda i: (i, 0))
    return pl.pallas_call(
        functools.partial(_ffn_kernel, tf),
        grid=(n_tok // tm,),
        in_specs=[row, row, _resident(w1.shape), _resident(w3.shape), _resident(w2.shape)],
        out_specs=row,
        out_shape=jax.ShapeDtypeStruct((n_tok, D_MODEL), F32),
        compiler_params=_params("parallel"),
        name="ffn",
    )(x1, hf, w1, w3, w2)


def _gather_rows(table, idx, window=128):
    split, n_table, width = table.shape
    n = idx.shape[0]
    flat_idx = (idx[None, :] + n_table * jnp.arange(split, dtype=jnp.int32)[:, None]).reshape(-1)
    return _gather_subrows(table.reshape(split * n_table, width), flat_idx, window).reshape(split, n, width)


def _gather_subrows(table, idx, window):
    n = idx.shape[0]
    d = table.shape[1]
    mesh = plsc.VectorSubcoreMesh(core_axis_name="core", subcore_axis_name="subcore")

    @functools.partial(pl.kernel, out_type=jax.ShapeDtypeStruct((n, d), table.dtype), mesh=mesh,
                       name="gather_rows")
    def gather(table_hbm, idx_hbm, out_hbm):
        def body(idx_vmem, out_vmem):
            pltpu.sync_copy(table_hbm.at[idx_vmem.at[0]], out_vmem)

        pltpu.emit_pipeline(
            body,
            grid=(n // window,),
            in_specs=[pl.BlockSpec((1, window), index_map=lambda i: (0, i))],
            out_specs=[pl.BlockSpec((window, d), index_map=lambda i: (i, 0))],
            core_axis_name=("core", "subcore"),
            dimension_semantics=(pltpu.PARALLEL,),
        )(idx_hbm, out_hbm)

    return gather(table, idx.reshape(1, n))


def _scatter_rows(table, dest, n_out, window=128):
    split, n_table, width = table.shape
    n = dest.shape[0]
    flat_dest = (dest[None, :] + n_out * jnp.arange(split, dtype=jnp.int32)[:, None]).reshape(1, split * n)
    src_blocks, blocks = n_table // window, n // window
    mesh = plsc.VectorSubcoreMesh(core_axis_name="core", subcore_axis_name="subcore")

    @functools.partial(pl.kernel, out_type=jax.ShapeDtypeStruct((split * n_out, width), table.dtype), mesh=mesh,
                       name="scatter_rows")
    def scatter(table_hbm, dest_hbm, out_hbm):
        def body(rows_vmem, dest_vmem):
            pltpu.sync_copy(rows_vmem, out_hbm.at[dest_vmem.at[0]])

        pltpu.emit_pipeline(
            body,
            grid=(split * blocks,),
            in_specs=[pl.BlockSpec((window, width),
                                   index_map=lambda i: ((i // blocks) * src_blocks + (i % blocks) % src_blocks, 0)),
                      pl.BlockSpec((1, window), index_map=lambda i: (0, i))],
            out_specs=[],
            core_axis_name=("core", "subcore"),
            dimension_semantics=(pltpu.PARALLEL,),
        )(table_hbm, dest_hbm)

    return scatter(table.reshape(split * n_table, width), flat_dest).reshape(split, n_out, width)


def _expert_kernel(tf, ge_ref, nv_ref, x_ref, w1_ref, w3_ref, w2_ref, y_ref):
    g = pl.program_id(0)

    @pl.when(g < nv_ref[0])
    def _():
        x = _unpack_rows([x_ref[j] for j in range(ROW_SPLIT)]).astype(BF16)
        acc = jnp.zeros((MOE_ROWS, D_MODEL), F32)
        for lo in range(0, EXPERT_DIM, tf):
            a = jnp.dot(x, w1_ref[:, lo:lo + tf], preferred_element_type=F32)
            b = jnp.dot(x, w3_ref[:, lo:lo + tf], preferred_element_type=F32)
            acc = acc + jnp.dot((_silu(a) * b).astype(BF16), w2_ref[lo:lo + tf, :], preferred_element_type=F32)
        for j, words in enumerate(_pack_rows(acc)):
            y_ref[j] = words

    @pl.when(g >= nv_ref[0])
    def _():
        y_ref[...] = jnp.zeros_like(y_ref)


def _experts(xg, group_expert, n_valid, w1, w3, w2, tf=512):
    n_rows = xg.shape[1]
    rows = pl.BlockSpec((ROW_SPLIT, MOE_ROWS, SUBROW), lambda g, ge, nv: (0, g, 0))
    return pl.pallas_call(
        functools.partial(_expert_kernel, tf),
        grid_spec=pltpu.PrefetchScalarGridSpec(
            num_scalar_prefetch=2,
            grid=(n_rows // MOE_ROWS,),
            in_specs=[rows,
                      pl.BlockSpec((None, D_MODEL, EXPERT_DIM), lambda g, ge, nv: (ge[g], 0, 0)),
                      pl.BlockSpec((None, D_MODEL, EXPERT_DIM), lambda g, ge, nv: (ge[g], 0, 0)),
                      pl.BlockSpec((None, EXPERT_DIM, D_MODEL), lambda g, ge, nv: (ge[g], 0, 0))],
            out_specs=rows,
        ),
        out_shape=jax.ShapeDtypeStruct((ROW_SPLIT, n_rows, SUBROW), jnp.uint32),
        compiler_params=_params("arbitrary", vmem_limit=BIG_VMEM_LIMIT),
        name="experts",
    )(group_expert, n_valid, xg, w1, w3, w2)


def _combine_kernel(x1_ref, y0_ref, y1_ref, gate_ref, o_ref):
    gate = gate_ref[...]
    y0 = _unpack_rows([y0_ref[j] for j in range(ROW_SPLIT)])
    y1 = _unpack_rows([y1_ref[j] for j in range(ROW_SPLIT)])
    o_ref[...] = x1_ref[...] + gate[:, 0:1] * y0 + gate[:, 1:2] * y1


def _combine(x1, yg, gate_t, tm=512):
    n_tok = x1.shape[0]
    nb = n_tok // tm
    return pl.pallas_call(
        _combine_kernel,
        grid=(nb,),
        in_specs=[pl.BlockSpec((tm, D_MODEL), lambda i: (i, 0)),
                  pl.BlockSpec((ROW_SPLIT, tm, SUBROW), lambda i: (0, i, 0)),
                  pl.BlockSpec((ROW_SPLIT, tm, SUBROW), lambda i: (0, i + nb, 0)),
                  pl.BlockSpec((tm, 2), lambda i: (i, 0))],
        out_specs=pl.BlockSpec((tm, D_MODEL), lambda i: (i, 0)),
        out_shape=jax.ShapeDtypeStruct((n_tok, D_MODEL), F32),
        compiler_params=_params("parallel"),
        name="moe_combine",
    )(x1, yg, yg, gate_t)


def _moe(x1, routing, w1, w3, w2):
    n_tok = x1.shape[0]
    idx, gate, rank, cnt, hf = routing
    counts = cnt[:, 0].astype(jnp.int32)
    padded = (counts + MOE_ROWS - 1) // MOE_ROWS * MOE_ROWS
    end_padded = jnp.cumsum(padded)
    start_padded = end_padded - padded
    start = sum(jnp.where(idx == e, start_padded[e], 0) for e in range(N_EXPERTS))
    dest = (start + rank).reshape(-1)
    n_groups = (n_tok * 2 + MOE_ROWS - 1) // MOE_ROWS + N_EXPERTS
    n_rows = n_groups * MOE_ROWS
    group_row = jnp.arange(n_groups, dtype=jnp.int32)[:, None] * MOE_ROWS
    group_expert = jnp.minimum(jnp.sum(group_row >= end_padded[None, :], axis=1), N_EXPERTS - 1).astype(jnp.int32)
    n_valid = (end_padded[-1:] // MOE_ROWS).astype(jnp.int32)
    fill = jnp.arange(MOE_ROWS, dtype=jnp.int32)[None, :]
    pad_dest = jnp.where(fill < (padded - counts)[:, None], (start_padded + counts)[:, None] + fill,
                         n_rows - MOE_ROWS + fill).reshape(-1)
    xg = _scatter_rows(hf, jnp.concatenate([dest, pad_dest]), n_rows)
    y = _experts(xg, group_expert, n_valid, w1, w3, w2)
    return _combine(x1, _gather_rows(y, dest), gate.T)


def _row(v, pad=0):
    v = v.reshape(1, -1).astype(F32)
    return jnp.pad(v, ((0, 0), (0, pad))) if pad else v


def _rows_at(w, start, total):
    return jnp.pad(w.astype(F32), ((start, total - start - w.shape[0]), (0, 0)))


def kernel(x, w_in, norm_mix, rwkv_mu, rwkv_decay_up, rwkv_w0, rwkv_a_up, rwkv_a0, rwkv_gate_up, rwkv_k_k, rwkv_k_a, rwkv_r_k, rwkv_ln_w, rwkv_ln_b, vres_down, vres_up, vres_v0, ssm_conv_w, ssm_conv_b, ssm_dt_bias, ssm_a_log, ssm_d, ssm_norm_w, att_q_gain, att_k_gain, att_rel_bias, gla_gate_up, gla_gate_bias, gla_norm_w, w_branch, w_out, norm_ffn, ffn_w1, ffn_w3, ffn_w2, moe_router, moe_w1, moe_w3, moe_w2):
    bsz, seq, _ = x.shape
    n_tok = bsz * seq
    depth = w_in.shape[0]
    x2 = x.reshape(n_tok, D_MODEL)
    v_first = None
    for l in range(depth):
        n_cols = w_in.shape[2]
        w_t = jnp.swapaxes(w_in, 1, 2).reshape(depth * n_cols, D_MODEL)
        patches = jnp.zeros((2, PACK_ROWS, D_MODEL), F32)
        if l > 0:
            lo = RWKV_COLS % PACK_ROWS
            patches = patches.at[1, lo:lo + 32].set(vres_down[l - 1].T)
        w_mix = _pack_weight_rows(w_t, _mixer_plan(l * n_cols), patches, n_sub=MIXER_PACK_BLOCKS)
        o_gate = RWKV_COLS + SSM_COLS + ATT_COLS + GLA_COLS
        wg = _pack_weight_rows(w_t, [(l * n_cols + r, GATE_PACK_ROWS, 0) for r in range(o_gate, n_cols, GATE_PACK_ROWS)],
                               jnp.zeros((1, GATE_PACK_ROWS, D_MODEL), F32))
        moe = l % 2 == 1
        up_w = [(moe_w3, l // 2)] if moe else [(ffn_w1, l // 2), (ffn_w3, l // 2)]
        h, za, zb, zc, zd, wb_b, wo_b, *up_b = _inproj(x2, _row(norm_mix[l]), w_mix,
                                                       cast=[(w_branch, l), (w_out, l)] + up_w)
        shp = lambda z: z.reshape(bsz, seq, z.shape[-1])

        rw = dict(mu=_row(rwkv_mu[l], RWKV_PAD - RWKV_COLS),
                  dup=_rows_at(rwkv_decay_up[l], 0, LANES), w0=_row(rwkv_w0[l]),
                  aup=_rows_at(rwkv_a_up[l], 64, LANES), a0=_row(rwkv_a0[l]),
                  gup=_rows_at(rwkv_gate_up[l], 0, 256).astype(BF16),
                  kk=_row(rwkv_k_k[l]), ka=_row(rwkv_k_a[l]), rk=_row(rwkv_r_k[l]),
                  lnw=_row(rwkv_ln_w[l]), lnb=_row(rwkv_ln_b[l]))
        if l > 0:
            rw.update(vup=_rows_at(vres_up[l - 1], 160, 256), v0=_row(vres_v0[l - 1]))
        ssm = dict(cw=ssm_conv_w[l], cb=_row(ssm_conv_b[l]), dtb=_row(ssm_dt_bias[l], LANES - SSM_HEADS),
                   alog=_row(ssm_a_log[l], LANES - SSM_HEADS), dexp=_row(jnp.repeat(ssm_d[l], CHUNK)),
                   nw=_row(ssm_norm_w[l]))
        gla = dict(gup=_rows_at(gla_gate_up[l], 0, LANES), gb=_row(gla_gate_bias[l]), nw=_row(gla_norm_w[l]))
        (o_c,), (o_d,), (o_b,), res_a = _run_parts(
            [_attention_part(shp(zc), att_q_gain[l], att_k_gain[l], att_rel_bias, MIXER_BLOCK),
             _gla_part(shp(zd), gla, MIXER_BLOCK), _mamba_part(shp(zb), ssm, MIXER_BLOCK),
             _rwkv_part(shp(za), v_first, rw, MIXER_BLOCK)],
            grid=(bsz, seq // MIXER_BLOCK), name="mixers")
        o_a = res_a[0]
        if l == 0:
            v_first = res_a[1]
        outs = [o.reshape(n_tok, BRANCH_DIM) for o in (o_a, o_b, o_c, o_d)]
        merge_w = (wg, wb_b, wo_b, _row(norm_ffn[l]))
        if not moe:
            x1, hf, *w1_b = _merge(x2, h, outs, *merge_w, cast=[(moe_w1, l // 2)] if l + 1 < depth else [])
            x2 = _ffn(x1, hf, *up_b, ffn_w2[l // 2].astype(BF16))
        else:
            x1, *routing, w2_b = _merge(x2, h, outs, *merge_w, wr_t=moe_router[l // 2].T, cast=[(moe_w2, l // 2)])
            x2 = _moe(x1, routing, w1_b[0], up_b[0], w2_b)
    return x2.reshape(bsz, seq, D_MODEL)
```

```python
import functools
from typing import Any, NamedTuple

import jax
import jax.numpy as jnp
from jax import lax
from jax.experimental import pallas as pl
from jax.experimental.pallas import tpu as pltpu
from jax.experimental.pallas import tpu_sc as plsc

F32 = jnp.float32
BF16 = jnp.bfloat16

D_MODEL = 1024
CHUNK = 64
BRANCH_DIM = 512
NORM_EPS = 1e-6
LANES = 128
VMEM_LIMIT = 56 * 1024 * 1024
BIG_VMEM_LIMIT = 61 * 1024 * 1024

LOG2_E = 1.4426950408889634

RWKV_LN_EPS = 64e-5
RWKV_DECAY_SCALE = 0.6065306597126334
RWKV_COLS = 1824
RWKV_PAD = 1920
SSM_COLS = 1544
SSM_PAD = 1664
SSM_HEADS = 8
SSM_STATE = 128
ATT_COLS = 1536
ATT_HEADS = 8
ATT_LEFT = 8 * CHUNK
REL_CLIP = 2 * CHUNK
GLA_COLS = 1552
GLA_PAD = 1664
GLA_GATE_NORM = 16.0
FFN_DIM = 2816
N_EXPERTS = 8
EXPERT_DIM = 3584
MOE_ROWS = 512
ROW_SPLIT = 2
SUBROW = D_MODEL // (2 * ROW_SPLIT)


def _dot(a, b):
    return jnp.dot(a.astype(BF16), b.astype(BF16), preferred_element_type=F32)


def _dot_nt(a, b):
    return lax.dot_general(a.astype(BF16), b.astype(BF16), (((1,), (1,)), ((), ())),
                           preferred_element_type=F32)


def _dot_tn(a, b):
    return lax.dot_general(a.astype(BF16), b.astype(BF16), (((0,), (0,)), ((), ())),
                           preferred_element_type=F32)


def _hi_lo(a):
    hi = a.astype(BF16)
    return hi, (a - hi.astype(F32)).astype(BF16)


def _dot_3x(a, b):
    a_hi, a_lo = _hi_lo(a)
    b_hi, b_lo = _hi_lo(b)
    return (jnp.dot(a_hi, b_hi, preferred_element_type=F32) + jnp.dot(a_lo, b_hi, preferred_element_type=F32)
            + jnp.dot(a_hi, b_lo, preferred_element_type=F32))


def _dot_split(a, m):
    m = m.astype(BF16)
    hi, lo = _hi_lo(a)
    return jnp.dot(hi, m, preferred_element_type=F32) + jnp.dot(lo, m, preferred_element_type=F32)


def _dot_split_l(m, a):
    m = m.astype(BF16)
    hi, lo = _hi_lo(a)
    return jnp.dot(m, hi, preferred_element_type=F32) + jnp.dot(m, lo, preferred_element_type=F32)


def _pack_rows(x):
    bits = pltpu.bitcast(x.astype(BF16).astype(F32), jnp.uint32)
    half = D_MODEL // 2
    return [(bits[:, p * SUBROW:(p + 1) * SUBROW] & jnp.uint32(0xFFFF0000))
            | (bits[:, half + p * SUBROW:half + (p + 1) * SUBROW] >> jnp.uint32(16)) for p in range(ROW_SPLIT)]


def _unpack_rows(planes):
    hi = [pltpu.bitcast(w & jnp.uint32(0xFFFF0000), F32) for w in planes]
    lo = [pltpu.bitcast(w << jnp.uint32(16), F32) for w in planes]
    return jnp.concatenate(hi + lo, axis=-1)


def _softplus(x):
    return jnp.maximum(x, 0.0) + jnp.log(1.0 + jnp.exp(-jnp.abs(x)))


def _sigmoid(x):
    return 1.0 / (1.0 + jnp.exp(-x))


def _silu(x):
    return x * _sigmoid(x)


def _iota2(shape, axis):
    return lax.broadcasted_iota(jnp.int32, shape, axis)


def _group_mean_matrix(n, group):
    r = _iota2((n, n), 0) // group
    c = _iota2((n, n), 1) // group
    return jnp.where(r == c, 1.0 / group, 0.0).astype(F32)


def _chunk_tri(n):
    r = _iota2((n, n), 0)
    c = _iota2((n, n), 1)
    return jnp.where((r // CHUNK == c // CHUNK) & (r >= c), 1.0, 0.0).astype(F32)


def _resident(shape):
    nd = len(shape)
    return pl.BlockSpec(shape, lambda *_: (0,) * nd, pipeline_mode=pl.Buffered(1))


def _params(*sem, vmem_limit=VMEM_LIMIT):
    return pltpu.CompilerParams(dimension_semantics=sem, vmem_limit_bytes=vmem_limit)


_DONE = object()


class _Part(NamedTuple):
    body: Any
    inputs: list
    in_specs: list
    out_shapes: list
    out_specs: list
    scratch_shapes: list


def _run_parts(parts, grid, name):
    def split(refs, counts):
        out, lo = [], 0
        for n in counts:
            out.append(refs[lo:lo + n])
            lo += n
        return out

    n_in = [len(p.inputs) for p in parts]
    n_out = [len(p.out_shapes) for p in parts]
    n_scr = [len(p.scratch_shapes) for p in parts]

    def kernel(*refs):
        ins = split(refs[:sum(n_in)], n_in)
        outs = split(refs[sum(n_in):sum(n_in) + sum(n_out)], n_out)
        scr = split(refs[sum(n_in) + sum(n_out):], n_scr)
        stages = [part.body(*i, *o, *c) for part, i, o, c in zip(parts, ins, outs, scr)]
        while stages:
            for stage in list(stages):
                if next(stage, _DONE) is _DONE:
                    stages.remove(stage)

    flat = lambda field: [x for p in parts for x in getattr(p, field)]
    res = pl.pallas_call(
        kernel,
        grid=grid,
        in_specs=flat("in_specs"),
        out_specs=flat("out_specs"),
        out_shape=flat("out_shapes"),
        scratch_shapes=flat("scratch_shapes"),
        compiler_params=_params("parallel", "arbitrary", vmem_limit=BIG_VMEM_LIMIT),
        name=name,
    )(*flat("inputs"))
    return split(list(res), n_out)


def _cast_rider(stack, index, steps):
    shape = stack.shape[1:]
    cols = shape[-1]
    rows = stack[0].size // cols // steps
    assert stack[0].size // cols % steps == 0 and rows % 16 == 0
    return (stack.reshape(stack.shape[0] * steps, rows, cols),
            pl.BlockSpec((None, rows, cols), lambda i: (index * steps + i, 0, 0)),
            pl.BlockSpec((None, rows, cols), lambda i: (i, 0, 0)),
            jax.ShapeDtypeStruct((steps, rows, cols), BF16), shape)


def _cast_slices(src_refs, dst_refs):
    for src, dst in zip(src_refs, dst_refs):
        dst[...] = src[...].astype(dst.dtype)


MIXER_COLS = (RWKV_PAD, SSM_PAD, ATT_COLS, GLA_PAD)
MIXER_DTYPES = (F32, F32, BF16, BF16)
MIXER_BLOCK = 256


PACK_ROWS = 128
MIXER_PACK_BLOCKS = 4
GATE_PACK_ROWS = 512


def _pack_kernel(n_sub, src_ref, keep_ref, fill_ref, *refs):
    del src_ref, fill_ref
    w_refs, patch_refs, o_ref = refs[:n_sub], refs[n_sub:2 * n_sub], refs[2 * n_sub]
    rows = w_refs[0].shape[0]
    row = _iota2(w_refs[0].shape, 0)
    for k in range(n_sub):
        keep = keep_ref[pl.program_id(0) * n_sub + k]
        o_ref[k * rows:(k + 1) * rows, :] = jnp.where(row < keep, w_refs[k][...], patch_refs[k][...]).astype(o_ref.dtype)


def _pack_weight_rows(w_t, plan, patches, n_sub=1):
    plan = plan + [plan[-1]] * (-len(plan) % n_sub)
    src, keep, fill = (jnp.asarray(v, jnp.int32) for v in zip(*plan))
    rows, cols = patches.shape[1:]
    sub = lambda k: pl.BlockSpec((pl.Element(rows), pl.Element(cols)),
                                 lambda j, src, keep, fill: (pl.multiple_of(src[j * n_sub + k], 8), 0))
    patch = lambda k: pl.BlockSpec((None, rows, cols), lambda j, src, keep, fill: (fill[j * n_sub + k], 0, 0))
    return pl.pallas_call(
        functools.partial(_pack_kernel, n_sub),
        grid_spec=pltpu.PrefetchScalarGridSpec(
            num_scalar_prefetch=3,
            grid=(len(plan) // n_sub,),
            in_specs=[sub(k) for k in range(n_sub)] + [patch(k) for k in range(n_sub)],
            out_specs=pl.BlockSpec((n_sub * rows, cols), lambda j, src, keep, fill: (j, 0)),
        ),
        out_shape=jax.ShapeDtypeStruct((len(plan) * rows, cols), BF16),
        compiler_params=_params("parallel"),
        name="pack_weights",
    )(src, keep, fill, *[w_t] * n_sub, *[patches] * n_sub)


def _mixer_plan(base):
    o_ss, o_at = RWKV_COLS + SSM_COLS, RWKV_COLS + SSM_COLS + ATT_COLS
    plan = []

    def rows(lo, hi, fill=0):
        for r in range(lo, hi, PACK_ROWS):
            plan.append((base + r, min(PACK_ROWS, hi - r), fill))

    rows(0, RWKV_COLS, fill=1)
    rows(RWKV_COLS, o_ss)
    rows(o_ss, o_at)
    rows(o_at, o_at + 1024)
    rows(o_at + 1040, o_at + GLA_COLS)
    rows(o_at + 1024, o_at + 1040)
    assert len(plan) * PACK_ROWS == sum(MIXER_COLS)
    return plan


def _inproj_kernel(n_cast, x_ref, g_ref, w_ref, *refs):
    cast_in, (h_ref, *z_refs), cast_out = refs[:n_cast], refs[n_cast:len(refs) - n_cast], refs[len(refs) - n_cast:]
    _cast_slices(cast_in, cast_out)
    x = x_ref[...]
    ms = jnp.mean(x * x, axis=-1, keepdims=True)
    h = (x * lax.rsqrt(ms + NORM_EPS) * g_ref[...]).astype(BF16)
    h_ref[...] = h
    lo = 0
    for pair in (z_refs[:2], z_refs[2:]):
        widths = [z_ref.shape[-1] for z_ref in pair]
        z = _dot_nt(h, w_ref[lo:lo + sum(widths), :])
        pair[0][...] = z[:, :widths[0]].astype(pair[0].dtype)
        pair[1][...] = z[:, widths[0]:].astype(pair[1].dtype)
        lo += sum(widths)


def _inproj(x2, g, w_mix, cast=(), tm=512):
    n_tok = x2.shape[0]
    row = lambda n: pl.BlockSpec((tm, n), lambda i: (i, 0))
    riders = [_cast_rider(stack, index, n_tok // tm) for stack, index in cast]
    res = pl.pallas_call(
        functools.partial(_inproj_kernel, len(cast)),
        grid=(n_tok // tm,),
        in_specs=[row(D_MODEL), _resident((1, D_MODEL)), _resident(w_mix.shape)] + [r[1] for r in riders],
        out_specs=[row(D_MODEL)] + [row(n) for n in MIXER_COLS] + [r[2] for r in riders],
        out_shape=[jax.ShapeDtypeStruct((n_tok, D_MODEL), BF16)]
        + [jax.ShapeDtypeStruct((n_tok, n), dt) for n, dt in zip(MIXER_COLS, MIXER_DTYPES)] + [r[3] for r in riders],
        compiler_params=_params("parallel", vmem_limit=BIG_VMEM_LIMIT if cast else VMEM_LIMIT),
        name="inproj",
    )(x2, g, w_mix, *[r[0] for r in riders])
    n_plain = len(res) - len(cast)
    return list(res[:n_plain]) + [wb16.reshape(r[4]) for wb16, r in zip(res[n_plain:], riders)]


def _rwkv_block(r, k, v, kk, ka, ld, tb, state_ref, y_ref):
    n_chunks = tb // CHUNK
    cum = _dot_split_l(_chunk_tri(tb), ld)
    p_inv = jnp.exp(-cum)
    a_t = -kk * jnp.exp(cum - ld)
    b_t = kk * ka * p_inv
    k_t = k * p_inv
    r_t = r * jnp.exp(cum)
    bf = lambda z: z.astype(BF16)
    a_b, b_b, k_b, r_b, v_b = bf(a_t), bf(b_t), bf(k_t), bf(r_t), bf(v)
    yield

    lane_head = _iota2((CHUNK, LANES), 1) // CHUNK
    row = _iota2((CHUNK, LANES), 0)
    col = _iota2((CHUNK, LANES), 1) % CHUNK
    strict = row > col
    incl = row >= col
    bd_mask = (_iota2((LANES, LANES), 0) // CHUNK) == (_iota2((LANES, LANES), 1) // CHUNK)
    eye = jnp.where(_iota2((LANES, LANES), 0) == _iota2((LANES, LANES), 1), 1.0, 0.0).astype(F32)

    def stack(z):
        return jnp.concatenate([jnp.where(lane_head == 0, z, 0.0), jnp.where(lane_head == 1, z, 0.0)], axis=0)

    def sub(z, u):
        c, p = u
        return z[c * CHUNK:(c + 1) * CHUNK, p * LANES:(p + 1) * LANES]

    def prepare(units):
        gram = {u: _dot_nt(jnp.concatenate([sub(a_b, u), sub(r_b, u)], axis=0),
                           jnp.concatenate([stack(sub(b_b, u)), stack(sub(k_b, u))], axis=0))
                for u in units}
        yield
        a_rb = {u: bf(jnp.where(incl, gram[u][CHUNK:, :LANES], 0.0)) for u in units}
        a_akrk = {u: bf(jnp.concatenate([jnp.where(strict, gram[u][:CHUNK, LANES:], 0.0),
                                         jnp.where(incl, gram[u][CHUNK:, LANES:], 0.0)], axis=0)) for u in units}
        pw = {u: stack(jnp.where(strict, gram[u][:CHUNK, :LANES], 0.0)) for u in units}
        t_bd = {u: eye + pw[u] for u in units}
        pw = {u: _dot(pw[u], pw[u]) for u in units}
        yield
        for _ in range(CHUNK.bit_length() - 3):
            sp = {u: _dot(jnp.concatenate([t_bd[u], pw[u]], axis=0), pw[u]) for u in units}
            t_bd = {u: t_bd[u] + sp[u][:LANES] for u in units}
            pw = {u: sp[u][LANES:] for u in units}
            yield
        t_bd = {u: bf(t_bd[u] + _dot(t_bd[u], pw[u])) for u in units}
        v_bd = {u: stack(sub(v_b, u)) for u in units}
        avyv = {u: _dot(a_akrk[u], v_bd[u]) for u in units}
        yield
        wuv = {u: _dot(t_bd[u], jnp.concatenate([stack(sub(a_b, u)), stack(bf(avyv[u][:CHUNK]))], axis=1))
               for u in units}
        wr = {u: jnp.concatenate([bf(wuv[u][:, :LANES]), sub(r_b, u)], axis=0) for u in units}
        yield
        return a_rb, avyv, wuv, wr

    a_rb, avyv, wuv, wr = yield from prepare([(c, p) for c in range(n_chunks) for p in range(4)])

    h = [state_ref[p] for p in range(4)]
    for c in range(n_chunks):
        last = c * CHUNK + CHUNK - 1
        p_last = jnp.exp(cum[last:last + 1, :])
        hs = [_dot_nt(wr[(c, p)], h[p]) for p in range(4)]
        u2 = [hs[p][:LANES] + wuv[(c, p)][:, LANES:] for p in range(4)]
        u_p = [u2[p][:CHUNK] + u2[p][CHUNK:] for p in range(4)]
        upd = [_dot_tn(jnp.concatenate([u_p[p], sub(v, (c, p))], axis=0),
                       jnp.concatenate([sub(b_t, (c, p)), sub(k_t, (c, p))], axis=0)
                       * p_last[:, p * LANES:(p + 1) * LANES]) for p in range(4)]
        for p in range(4):
            y_ref[c * CHUNK:(c + 1) * CHUNK, p * LANES:(p + 1) * LANES] = (
                hs[p][LANES:] + _dot(a_rb[(c, p)], stack(u_p[p])) + avyv[(c, p)][CHUNK:])
        h = [h[p] * p_last[:, p * LANES:(p + 1) * LANES] + jnp.where(bd_mask, upd[p], 0.0) for p in range(4)]
        yield
    for p in range(4):
        state_ref[p] = h[p]


def _rwkv_kernel(has_vres, tb, *refs):
    if has_vres:
        (za_ref, vfirst_ref, mu_ref, dup_ref, w0_ref, aup_ref, a0_ref, gup_ref, vup_ref, v0_ref,
         kk_ref, ka_ref, rk_ref, lnw_ref, lnb_ref, o_ref, prev_ref, state_ref, y_ref) = refs
    else:
        (za_ref, mu_ref, dup_ref, w0_ref, aup_ref, a0_ref, gup_ref,
         kk_ref, ka_ref, rk_ref, lnw_ref, lnb_ref, o_ref, vraw_ref, prev_ref, state_ref, y_ref) = refs

    @pl.when(pl.program_id(1) == 0)
    def _():
        prev_ref[...] = jnp.zeros_like(prev_ref)
        state_ref[...] = jnp.zeros_like(state_ref)

    yield
    za = za_ref[...]
    shifted = pltpu.roll(za, 1, axis=0)
    shifted = jnp.where(_iota2(za.shape, 0) == 0, prev_ref[7:8, :], shifted)
    prev_ref[...] = za[tb - 8:, :]
    za = za + (shifted - za) * mu_ref[...]

    r = za[:, 0:512]
    k = za[:, 512:1024]
    v = za[:, 1024:1536]
    xwa = za[:, 1536:1664]
    xg = za[:, 1664:1920]
    ld = -RWKV_DECAY_SCALE * _sigmoid(w0_ref[...] + _dot_3x(jnp.tanh(xwa), dup_ref[...]))
    yield
    a = _sigmoid(a0_ref[...] + _dot(xwa, aup_ref[...]))
    g = _dot(_sigmoid(xg), gup_ref[...])
    if has_vres:
        v_mix = _sigmoid(v0_ref[...] + _dot(xg, vup_ref[...]))
        v = v + (vfirst_ref[...] - v) * v_mix
    else:
        vraw_ref[...] = v
    yield
    kk = k * kk_ref[...]
    k = k * (1.0 + (a - 1.0) * ka_ref[...])
    head_sum = _group_mean_matrix(512, CHUNK) * float(CHUNK)
    kk = kk / jnp.maximum(jnp.sqrt(_dot(kk * kk, head_sum)), 1e-12)
    yield

    yield from _rwkv_block(r, k, v, kk, a, ld, tb, state_ref, y_ref)
    y = y_ref[...]
    head_mean = _group_mean_matrix(512, CHUNK)
    mean = _dot_split(y, head_mean)
    yc = y - mean
    var = _dot(yc * yc, head_mean)
    y = yc * lax.rsqrt(var + RWKV_LN_EPS) * lnw_ref[...] + lnb_ref[...]
    y = y + _dot(r * k * rk_ref[...], head_sum) * v
    o_ref[...] = (y * g).astype(o_ref.dtype)


def _rwkv_part(za, v_first, prm, tb):
    bsz, seq, _ = za.shape
    has_vres = v_first is not None
    blk = lambda n: pl.BlockSpec((None, tb, n), lambda b, i: (b, i, 0))
    names = (["mu", "dup", "w0", "aup", "a0", "gup"] + (["vup", "v0"] if has_vres else [])
             + ["kk", "ka", "rk", "lnw", "lnb"])
    weights = [prm[n] for n in names]
    ins = [za] + ([v_first] if has_vres else []) + weights
    in_specs = [blk(RWKV_PAD)] + ([blk(512)] if has_vres else []) + [_resident(w.shape) for w in weights]
    out_shape = [jax.ShapeDtypeStruct((bsz, seq, 512), BF16)]
    out_specs = [blk(512)]
    if not has_vres:
        out_shape.append(jax.ShapeDtypeStruct((bsz, seq, 512), F32))
        out_specs.append(blk(512))
    return _Part(functools.partial(_rwkv_kernel, has_vres, tb), ins, in_specs, out_shape, out_specs,
                 [pltpu.VMEM((8, RWKV_PAD), F32), pltpu.VMEM((4, LANES, LANES), F32), pltpu.VMEM((tb, 512), F32)])


def _mamba_kernel(tb, zb_ref, cw_ref, cb_ref, dtb_ref, alog_ref, dexp_ref, nw_ref,
                  o_ref, xbuf_ref, state_ref, y_ref):
    @pl.when(pl.program_id(1) == 0)
    def _():
        xbuf_ref[0:8, :] = jnp.zeros((8, 1024), F32)
        state_ref[...] = jnp.zeros_like(state_ref)

    yield
    xbuf_ref[8:8 + tb, :] = zb_ref[:, 512:1536]
    conv = cb_ref[...]
    for i in range(4):
        conv = conv + cw_ref[i:i + 1, :] * xbuf_ref[5 + i:5 + i + tb, :]
    xbuf_ref[0:8, :] = xbuf_ref[tb:tb + 8, :]
    xbc = _silu(conv)
    xs = xbc[:, 0:512]
    bm = xbc[:, 512:768]
    cm = xbc[:, 768:1024]
    yield

    lane = _iota2((1, LANES), 1)
    dt = _softplus(zb_ref[:, 1536:1664] + dtb_ref[...])
    a_neg = jnp.where(lane < SSM_HEADS, -jnp.exp(alog_ref[...]), 0.0)
    acs = _dot_split_l(_chunk_tri(tb), dt * a_neg)
    expand = jnp.where(_iota2((LANES, 512), 0) == _iota2((LANES, 512), 1) // CHUNK, 1.0, 0.0)
    dt_e = _dot(dt, expand)
    acs_e = _dot_split(acs, expand)
    xdt = xs * dt_e
    yield

    n_chunks = tb // CHUNK
    causal = _iota2((CHUNK, 512), 0) >= _iota2((CHUNK, 512), 1) % CHUNK
    pair_head = _iota2((CHUNK, LANES), 1) // CHUNK
    spread = jnp.where(_iota2((CHUNK, 512), 0) == _iota2((CHUNK, 512), 1) % CHUNK, 1.0, 0.0)
    own_head = _iota2((SSM_HEADS, 512), 0) == _iota2((SSM_HEADS, 512), 1) // CHUNK
    rows_of = lambda c: slice(c * CHUNK, (c + 1) * CHUNK)
    grp = lambda g: slice(g * SSM_STATE, (g + 1) * SSM_STATE)

    def stack(z):
        return jnp.concatenate([jnp.where(pair_head == 0, z, 0.0), jnp.where(pair_head == 1, z, 0.0)], axis=0)

    y_intra, upd, e_in, e_last = [], [], [], []
    for c in range(n_chunks):
        rows = rows_of(c)
        acs_ec = acs_e[rows]
        acs_t = acs[rows].T[0:SSM_HEADS]
        acs_row = jnp.sum(jnp.where(own_head, _dot_split(acs_t, spread), 0.0), axis=0, keepdims=True)
        decay = jnp.exp(jnp.where(causal, acs_ec - acs_row, -jnp.inf))
        cb = jnp.concatenate(
            [_dot_nt(cm[rows, grp(g)], jnp.concatenate([bm[rows, grp(g)]] * 4, axis=0)) for g in range(2)], axis=1)
        m = cb * decay
        yield
        xdt_c = xdt[rows]
        y_intra.append([_dot(m[:, p * LANES:(p + 1) * LANES], stack(xdt_c[:, p * LANES:(p + 1) * LANES]))
                        for p in range(4)])
        last_e = acs_ec[CHUNK - 1:CHUNK, :]
        x_out = xdt_c * jnp.exp(last_e - acs_ec)
        upd.append([_dot_tn(bm[rows, grp(g)], x_out[:, g * 256:(g + 1) * 256]) for g in range(2)])
        e_in.append(jnp.exp(acs_ec))
        e_last.append(jnp.exp(last_e))
        yield

    state = [state_ref[g] for g in range(2)]
    for c in range(n_chunks):
        rows = rows_of(c)
        for g in range(2):
            ls = slice(g * 256, (g + 1) * 256)
            y_in = jnp.concatenate(y_intra[c][2 * g:2 * g + 2], axis=1)
            y_ref[rows, ls] = y_in + _dot(cm[rows, grp(g)], state[g]) * e_in[c][:, ls]
            state[g] = state[g] * e_last[c][:, ls] + upd[c][g]
        yield
    for g in range(2):
        state_ref[g] = state[g]

    y = (y_ref[...] + xs * dexp_ref[...]) * _silu(zb_ref[:, 0:512])
    for g in range(2):
        ls = slice(g * 256, (g + 1) * 256)
        yg = y[:, ls]
        ms = jnp.mean(yg * yg, axis=-1, keepdims=True)
        o_ref[:, ls] = (yg * lax.rsqrt(ms + NORM_EPS) * nw_ref[:, ls]).astype(o_ref.dtype)


def _mamba_part(zb, prm, tb):
    bsz, seq, _ = zb.shape
    blk = lambda n: pl.BlockSpec((None, tb, n), lambda b, i: (b, i, 0))
    weights = [prm[n] for n in ("cw", "cb", "dtb", "alog", "dexp", "nw")]
    return _Part(functools.partial(_mamba_kernel, tb), [zb] + weights,
                 [blk(SSM_PAD)] + [_resident(w.shape) for w in weights],
                 [jax.ShapeDtypeStruct((bsz, seq, 512), BF16)], [blk(512)],
                 [pltpu.VMEM((tb + 8, 1024), F32), pltpu.VMEM((2, SSM_STATE, 256), F32), pltpu.VMEM((tb, 512), F32)])


def _attn_kernel(tq, qkv_ref, qg_ref, kg_ref, bias_ref, o_ref, kn_ref, vb_ref):
    i = pl.program_id(1)
    head_mean = _group_mean_matrix(512, CHUNK)
    win = tq + ATT_LEFT

    @pl.when(i == 0)
    def _():
        kn_ref[0:ATT_LEFT, :] = jnp.zeros((ATT_LEFT, 512), BF16)
        vb_ref[0:ATT_LEFT, :] = jnp.zeros((ATT_LEFT, 512), BF16)

    yield
    start = pl.multiple_of(i * tq, tq)
    k = qkv_ref[:, 512:1024].astype(F32)
    ms = _dot(k * k, head_mean)
    kn_ref[pl.ds(ATT_LEFT + start, tq), :] = (k * lax.rsqrt(ms + NORM_EPS) * kg_ref[...]).astype(BF16)
    vb_ref[pl.ds(ATT_LEFT + start, tq), :] = qkv_ref[:, 1024:1536].astype(BF16)
    q = qkv_ref[:, 0:512].astype(F32)
    ms = _dot(q * q, head_mean)
    qn = q * lax.rsqrt(ms + NORM_EPS) * qg_ref[...] * (CHUNK ** -0.5 * LOG2_E)
    kwin = kn_ref[pl.ds(start, win), :]
    vwin = vb_ref[pl.ds(start, win), :]
    lane_head = _iota2((tq, LANES), 1) // CHUNK
    yield
    for p in range(4):
        ls = slice(p * LANES, (p + 1) * LANES)
        q2 = jnp.concatenate([jnp.where(lane_head == s, qn[:, ls], 0.0) for s in range(2)], axis=0)
        sc = _dot_nt(q2, kwin[:, ls]) + bias_ref[2 * p:2 * p + 2].reshape(2 * tq, win).astype(F32)
        yield
        e = jnp.exp2(sc - jnp.max(sc, axis=-1, keepdims=True))
        yield
        o2 = _dot(e, vwin[:, ls]) / jnp.sum(e, axis=-1, keepdims=True)
        o_ref[:, ls] = jnp.where(lane_head == 0, o2[:tq], o2[tq:]).astype(o_ref.dtype)
        yield


def _band_bias(rel_bias, tq, win):
    period = tq + win
    m = jnp.arange(period)
    d = jnp.where(m < win, m, m - period) - ATT_LEFT
    f = (rel_bias.astype(F32)[:, jnp.clip(d, -REL_CLIP, REL_CLIP) + REL_CLIP] * LOG2_E).astype(BF16)
    g = jnp.tile(f, (1, tq))[:, :tq * (period - 1)].reshape(-1, tq, period - 1)
    bias = g[:, :, :win]
    left = ATT_LEFT // CHUNK
    qc = left + jnp.arange(tq)[:, None] // CHUNK
    kc = jnp.arange(win)[None, :] // CHUNK
    tiles = []
    for blk in range(ATT_LEFT // tq + 1):
        first_kc = jnp.maximum(qc - left, left - blk * (tq // CHUNK))
        tiles.append(jnp.where((kc <= qc) & (kc >= first_kc), bias, -jnp.inf))
    return jnp.stack(tiles)


def _attention_part(zc, q_gain, k_gain, rel_bias, tq):
    bsz, seq, _ = zc.shape
    win = tq + ATT_LEFT
    bias = _band_bias(rel_bias, tq, win)
    qg = jnp.tile(q_gain, ATT_HEADS)[None, :]
    kg = jnp.tile(k_gain, ATT_HEADS)[None, :]
    last_tile = bias.shape[0] - 1
    bias_spec = pl.BlockSpec((None,) + bias.shape[1:], lambda b, i: (jnp.minimum(i, last_tile), 0, 0, 0))
    blk = lambda n: pl.BlockSpec((None, tq, n), lambda b, i: (b, i, 0))
    return _Part(functools.partial(_attn_kernel, tq), [zc, qg, kg, bias],
                 [blk(ATT_COLS), _resident(qg.shape), _resident(kg.shape), bias_spec],
                 [jax.ShapeDtypeStruct((bsz, seq, 512), BF16)], [blk(512)],
                 [pltpu.VMEM((ATT_LEFT + seq, 512), BF16), pltpu.VMEM((ATT_LEFT + seq, 512), BF16)])


def _gla_kernel(tb, zd_ref, gup_ref, gb_ref, nw_ref, o_ref, state_ref, y_ref):
    @pl.when(pl.program_id(1) == 0)
    def _():
        state_ref[...] = jnp.zeros_like(state_ref)

    yield
    q = zd_ref[:, 0:256].astype(F32) * (CHUNK ** -0.5)
    k = zd_ref[:, 256:512].astype(F32)
    v = zd_ref[:, 512:1024].astype(F32)
    log_a = -_softplus(-(_dot_3x(zd_ref[:, 1536:1664].astype(F32), gup_ref[...]) + gb_ref[...])) / GLA_GATE_NORM
    bcum = _dot_split_l(_chunk_tri(tb), log_a)

    k_head = _iota2((CHUNK, 256), 1) // CHUNK
    v_head = _iota2((CHUNK, 512), 1) // LANES
    causal = _iota2((CHUNK, 256), 0) >= _iota2((CHUNK, 256), 1) % CHUNK
    bd = (_iota2((512, 256), 0) // LANES) == (_iota2((512, 256), 1) // CHUNK)
    n_chunks = tb // CHUNK
    qg_all = (q * jnp.exp(bcum)).astype(BF16)
    kg_all = k * jnp.exp(-bcum)
    yield
    qg, att, o_intra, upd, e_last = [], [], [], [], []
    for c in range(n_chunks):
        rows = slice(c * CHUNK, (c + 1) * CHUNK)
        kg_bd = jnp.concatenate([jnp.where(k_head == h, kg_all[rows], 0.0) for h in range(4)], axis=0)
        qg.append(qg_all[rows])
        att.append(jnp.where(causal, _dot_nt(qg[c], kg_bd), 0.0))
        yield
    for c in range(n_chunks):
        rows = slice(c * CHUNK, (c + 1) * CHUNK)
        bc = bcum[rows]
        blast = bc[CHUNK - 1:CHUNK, :]
        v_c = v[rows]
        v_bd = jnp.concatenate([jnp.where(v_head == h, v_c, 0.0) for h in range(4)], axis=0)
        o_intra.append(_dot(att[c], v_bd))
        upd.append(jnp.where(bd, _dot_tn(v_c, k[rows] * jnp.exp(blast - bc)), 0.0))
        e_last.append(jnp.exp(blast))
        yield
    st = state_ref[...]
    for c in range(n_chunks):
        y_ref[c * CHUNK:(c + 1) * CHUNK, :] = o_intra[c] + _dot_nt(qg[c], st)
        st = st * e_last[c] + upd[c]
        yield
    state_ref[...] = st

    o = y_ref[...]
    for h in range(4):
        ls = slice(h * LANES, (h + 1) * LANES)
        oh = o[:, ls]
        ms = jnp.mean(oh * oh, axis=-1, keepdims=True)
        o_ref[:, ls] = (oh * lax.rsqrt(ms + NORM_EPS) * nw_ref[...]
                        * _silu(zd_ref[:, 1024 + h * LANES:1024 + (h + 1) * LANES].astype(F32))).astype(o_ref.dtype)


def _gla_part(zd, prm, tb):
    bsz, seq, _ = zd.shape
    blk = lambda n: pl.BlockSpec((None, tb, n), lambda b, i: (b, i, 0))
    weights = [prm[n] for n in ("gup", "gb", "nw")]
    return _Part(functools.partial(_gla_kernel, tb), [zd] + weights,
                 [blk(GLA_PAD)] + [_resident(w.shape) for w in weights],
                 [jax.ShapeDtypeStruct((bsz, seq, 512), BF16)], [blk(512)],
                 [pltpu.VMEM((512, 256), F32), pltpu.VMEM((tb, 512), F32)])


def _route(tm, hf, wr_ref, idx_ref, gate_ref, rank_ref, cnt_ref, hf_ref, carry_ref):
    @pl.when(pl.program_id(0) == 0)
    def _():
        carry_ref[...] = jnp.zeros_like(carry_ref)

    for j, words in enumerate(_pack_rows(hf)):
        hf_ref[j] = words
    nt = lambda a, b: lax.dot_general(a, b, (((1,), (1,)), ((), ())), preferred_element_type=F32)
    w_hi, w_lo = _hi_lo(wr_ref[...])
    h_hi, h_lo = _hi_lo(hf)
    logits = nt(w_hi, h_hi) + nt(w_lo, h_hi) + nt(w_hi, h_lo)
    e_iota = _iota2((N_EXPERTS, tm), 0)
    m1 = jnp.max(logits, axis=0, keepdims=True)
    i1 = jnp.min(jnp.where(logits == m1, e_iota, N_EXPERTS), axis=0, keepdims=True)
    rest = jnp.where(e_iota == i1, -jnp.inf, logits)
    m2 = jnp.max(rest, axis=0, keepdims=True)
    i2 = jnp.min(jnp.where(rest == m2, e_iota, N_EXPERTS), axis=0, keepdims=True)
    e2 = jnp.exp(m2 - m1)
    gate_ref[0:1, :] = 1.0 / (1.0 + e2)
    gate_ref[1:2, :] = e2 / (1.0 + e2)
    idx_ref[0:1, :] = i1
    idx_ref[1:2, :] = i2
    hit1 = jnp.where(e_iota == i1, 1.0, 0.0)
    hit2 = jnp.where(e_iota == i2, 1.0, 0.0)
    before = jnp.where(_iota2((tm, tm), 0) < _iota2((tm, tm), 1), 1.0, 0.0)
    prior = _dot(hit1 + hit2, before) + carry_ref[:, 0:1]
    rank_ref[0:1, :] = jnp.sum(hit1 * prior, axis=0, keepdims=True).astype(jnp.int32)
    rank_ref[1:2, :] = jnp.sum(hit2 * prior, axis=0, keepdims=True).astype(jnp.int32)
    carry_ref[...] = carry_ref[...] + jnp.sum(hit1 + hit2, axis=1, keepdims=True)
    cnt_ref[...] = carry_ref[...]


def _merge_kernel(tm, route, n_cast, *refs):
    n_in, n_out = (11, 6) if route else (10, 2)
    ins, cast_in = refs[:n_in], refs[n_in:n_in + n_cast]
    outs = refs[n_in + n_cast:n_in + n_cast + n_out]
    cast_out = refs[n_in + n_cast + n_out:n_in + 2 * n_cast + n_out]
    scratch = refs[n_in + 2 * n_cast + n_out:]
    x_ref, h_ref, oa_ref, ob_ref, oc_ref, od_ref, wg_ref, wb_ref, wo_ref, nf_ref = ins[:10]
    _cast_slices(cast_in, cast_out)
    h = h_ref[...]
    acc = jnp.zeros(x_ref.shape, F32)
    for i, o_ref in enumerate((oa_ref, ob_ref, oc_ref, od_ref)):
        gate = _sigmoid(_dot_nt(h, wg_ref[i * D_MODEL:(i + 1) * D_MODEL, :]))
        acc = acc + gate * jnp.dot(o_ref[...], wb_ref[i], preferred_element_type=F32)
    x1 = x_ref[...] + jnp.dot(acc.astype(BF16), wo_ref[...], preferred_element_type=F32)
    ms = jnp.mean(x1 * x1, axis=-1, keepdims=True)
    hf = x1 * lax.rsqrt(ms + NORM_EPS) * nf_ref[...]
    if route:
        _route(tm, hf, ins[10], *outs[1:], *scratch)
    else:
        outs[1][...] = hf.astype(BF16)
    outs[0][...] = x1


def _merge(x2, h, outs, wg, wb, wo, nf, wr_t=None, cast=(), tm=512):
    n_tok = x2.shape[0]
    route = wr_t is not None
    steps = n_tok // tm
    row = lambda n: pl.BlockSpec((tm, n), lambda i: (i, 0))
    ins = [x2, h, *outs, wg, wb, wo, nf]
    in_specs = ([row(D_MODEL), row(D_MODEL)] + [row(BRANCH_DIM)] * 4
                + [_resident(wg.shape), _resident(wb.shape), _resident(wo.shape), _resident(nf.shape)])
    out_specs = [row(D_MODEL)]
    out_shape = [jax.ShapeDtypeStruct((n_tok, D_MODEL), F32)]
    scratch = []
    if route:
        ins.append(wr_t)
        in_specs.append(_resident(wr_t.shape))
        col = pl.BlockSpec((2, tm), lambda i: (0, i))
        out_specs += [col, col, col, pl.BlockSpec((N_EXPERTS, LANES), lambda i: (0, 0)),
                      pl.BlockSpec((ROW_SPLIT, tm, SUBROW), lambda i: (0, i, 0))]
        out_shape += [jax.ShapeDtypeStruct((2, n_tok), jnp.int32), jax.ShapeDtypeStruct((2, n_tok), F32),
                      jax.ShapeDtypeStruct((2, n_tok), jnp.int32), jax.ShapeDtypeStruct((N_EXPERTS, LANES), F32),
                      jax.ShapeDtypeStruct((ROW_SPLIT, n_tok, SUBROW), jnp.uint32)]
        scratch = [pltpu.VMEM((N_EXPERTS, LANES), F32)]
    else:
        out_specs.append(row(D_MODEL))
        out_shape.append(jax.ShapeDtypeStruct((n_tok, D_MODEL), BF16))
    riders = [_cast_rider(stack, index, steps) for stack, index in cast]
    res = pl.pallas_call(
        functools.partial(_merge_kernel, tm, route, len(cast)),
        grid=(steps,),
        in_specs=in_specs + [r[1] for r in riders],
        out_specs=out_specs + [r[2] for r in riders],
        out_shape=out_shape + [r[3] for r in riders],
        scratch_shapes=scratch,
        compiler_params=_params("arbitrary" if route else "parallel"),
        name="merge",
    )(*ins, *[r[0] for r in riders])
    n_plain = len(res) - len(cast)
    return list(res[:n_plain]) + [wb16.reshape(r[4]) for wb16, r in zip(res[n_plain:], riders)]


def _ffn_kernel(tf, x1_ref, hf_ref, w1_ref, w3_ref, w2_ref, o_ref):
    hf = hf_ref[...]
    acc = x1_ref[...]
    for lo in range(0, FFN_DIM, tf):
        cols = slice(lo, min(lo + tf, FFN_DIM))
        a = jnp.dot(hf, w1_ref[:, cols], preferred_element_type=F32)
        b = jnp.dot(hf, w3_ref[:, cols], preferred_element_type=F32)
        acc = acc + jnp.dot((_silu(a) * b).astype(BF16), w2_ref[cols, :], preferred_element_type=F32)
    o_ref[...] = acc


def _ffn(x1, hf, w1, w3, w2, tm=512, tf=512):
    n_tok = x1.shape[0]
    row = pl.BlockSpec((tm, D_MODEL), lambda i: (i, 0))
    return pl.pallas_call(
        functools.partial(_ffn_kernel, tf),
        grid=(n_tok // tm,),
        in_specs=[row, row, _resident(w1.shape), _resident(w3.shape), _resident(w2.shape)],
        out_specs=row,
        out_shape=jax.ShapeDtypeStruct((n_tok, D_MODEL), F32),
        compiler_params=_params("parallel"),
        name="ffn",
    )(x1, hf, w1, w3, w2)


def _gather_rows(table, idx, window=128):
    split, n_table, width = table.shape
    n = idx.shape[0]
    flat_idx = (idx[None, :] + n_table * jnp.arange(split, dtype=jnp.int32)[:, None]).reshape(-1)
    return _gather_subrows(table.reshape(split * n_table, width), flat_idx, window).reshape(split, n, width)


def _gather_subrows(table, idx, window):
    n = idx.shape[0]
    d = table.shape[1]
    mesh = plsc.VectorSubcoreMesh(core_axis_name="core", subcore_axis_name="subcore")

    @functools.partial(pl.kernel, out_type=jax.ShapeDtypeStruct((n, d), table.dtype), mesh=mesh,
                       name="gather_rows")
    def gather(table_hbm, idx_hbm, out_hbm):
        def body(idx_vmem, out_vmem):
            pltpu.sync_copy(table_hbm.at[idx_vmem.at[0]], out_vmem)

        pltpu.emit_pipeline(
            body,
            grid=(n // window,),
            in_specs=[pl.BlockSpec((1, window), index_map=lambda i: (0, i))],
            out_specs=[pl.BlockSpec((window, d), index_map=lambda i: (i, 0))],
            core_axis_name=("core", "subcore"),
            dimension_semantics=(pltpu.PARALLEL,),
        )(idx_hbm, out_hbm)

    return gather(table, idx.reshape(1, n))


def _scatter_rows(table, dest, n_out, window=128):
    split, n_table, width = table.shape
    n = dest.shape[0]
    flat_dest = (dest[None, :] + n_out * jnp.arange(split, dtype=jnp.int32)[:, None]).reshape(1, split * n)
    src_blocks, blocks = n_table // window, n // window
    mesh = plsc.VectorSubcoreMesh(core_axis_name="core", subcore_axis_name="subcore")

    @functools.partial(pl.kernel, out_type=jax.ShapeDtypeStruct((split * n_out, width), table.dtype), mesh=mesh,
                       name="scatter_rows")
    def scatter(table_hbm, dest_hbm, out_hbm):
        def body(rows_vmem, dest_vmem):
            pltpu.sync_copy(rows_vmem, out_hbm.at[dest_vmem.at[0]])

        pltpu.emit_pipeline(
            body,
            grid=(split * blocks,),
            in_specs=[pl.BlockSpec((window, width),
                                   index_map=lambda i: ((i // blocks) * src_blocks + (i % blocks) % src_blocks, 0)),
                      pl.BlockSpec((1, window), index_map=lambda i: (0, i))],
            out_specs=[],
            core_axis_name=("core", "subcore"),
            dimension_semantics=(pltpu.PARALLEL,),
        )(table_hbm, dest_hbm)

    return scatter(table.reshape(split * n_table, width), flat_dest).reshape(split, n_out, width)


def _expert_kernel(tf, ge_ref, nv_ref, x_ref, w1_ref, w3_ref, w2_ref, y_ref):
    g = pl.program_id(0)

    @pl.when(g < nv_ref[0])
    def _():
        x = _unpack_rows([x_ref[j] for j in range(ROW_SPLIT)]).astype(BF16)
        acc = jnp.zeros((MOE_ROWS, D_MODEL), F32)
        for lo in range(0, EXPERT_DIM, tf):
            a = jnp.dot(x, w1_ref[:, lo:lo + tf], preferred_element_type=F32)
            b = jnp.dot(x, w3_ref[:, lo:lo + tf], preferred_element_type=F32)
            acc = acc + jnp.dot((_silu(a) * b).astype(BF16), w2_ref[lo:lo + tf, :], preferred_element_type=F32)
        for j, words in enumerate(_pack_rows(acc)):
            y_ref[j] = words

    @pl.when(g >= nv_ref[0])
    def _():
        y_ref[...] = jnp.zeros_like(y_ref)


def _experts(xg, group_expert, n_valid, w1, w3, w2, tf=512):
    n_rows = xg.shape[1]
    rows = pl.BlockSpec((ROW_SPLIT, MOE_ROWS, SUBROW), lambda g, ge, nv: (0, g, 0))
    return pl.pallas_call(
        functools.partial(_expert_kernel, tf),
        grid_spec=pltpu.PrefetchScalarGridSpec(
            num_scalar_prefetch=2,
            grid=(n_rows // MOE_ROWS,),
            in_specs=[rows,
                      pl.BlockSpec((None, D_MODEL, EXPERT_DIM), lambda g, ge, nv: (ge[g], 0, 0)),
                      pl.BlockSpec((None, D_MODEL, EXPERT_DIM), lambda g, ge, nv: (ge[g], 0, 0)),
                      pl.BlockSpec((None, EXPERT_DIM, D_MODEL), lambda g, ge, nv: (ge[g], 0, 0))],
            out_specs=rows,
        ),
        out_shape=jax.ShapeDtypeStruct((ROW_SPLIT, n_rows, SUBROW), jnp.uint32),
        compiler_params=_params("arbitrary", vmem_limit=BIG_VMEM_LIMIT),
        name="experts",
    )(group_expert, n_valid, xg, w1, w3, w2)


def _combine_kernel(x1_ref, y0_ref, y1_ref, gate_ref, o_ref):
    gate = gate_ref[...]
    y0 = _unpack_rows([y0_ref[j] for j in range(ROW_SPLIT)])
    y1 = _unpack_rows([y1_ref[j] for j in range(ROW_SPLIT)])
    o_ref[...] = x1_ref[...] + gate[:, 0:1] * y0 + gate[:, 1:2] * y1


def _combine(x1, yg, gate_t, tm=512):
    n_tok = x1.shape[0]
    nb = n_tok // tm
    return pl.pallas_call(
        _combine_kernel,
        grid=(nb,),
        in_specs=[pl.BlockSpec((tm, D_MODEL), lambda i: (i, 0)),
                  pl.BlockSpec((ROW_SPLIT, tm, SUBROW), lambda i: (0, i, 0)),
                  pl.BlockSpec((ROW_SPLIT, tm, SUBROW), lambda i: (0, i + nb, 0)),
                  pl.BlockSpec((tm, 2), lambda i: (i, 0))],
        out_specs=pl.BlockSpec((tm, D_MODEL), lambda i: (i, 0)),
        out_shape=jax.ShapeDtypeStruct((n_tok, D_MODEL), F32),
        compiler_params=_params("parallel"),
        name="moe_combine",
    )(x1, yg, yg, gate_t)


def _moe(x1, routing, w1, w3, w2):
    n_tok = x1.shape[0]
    idx, gate, rank, cnt, hf = routing
    counts = cnt[:, 0].astype(jnp.int32)
    padded = (counts + MOE_ROWS - 1) // MOE_ROWS * MOE_ROWS
    end_padded = jnp.cumsum(padded)
    start_padded = end_padded - padded
    start = sum(jnp.where(idx == e, start_padded[e], 0) for e in range(N_EXPERTS))
    dest = (start + rank).reshape(-1)
    n_groups = (n_tok * 2 + MOE_ROWS - 1) // MOE_ROWS + N_EXPERTS
    n_rows = n_groups * MOE_ROWS
    group_row = jnp.arange(n_groups, dtype=jnp.int32)[:, None] * MOE_ROWS
    group_expert = jnp.minimum(jnp.sum(group_row >= end_padded[None, :], axis=1), N_EXPERTS - 1).astype(jnp.int32)
    n_valid = (end_padded[-1:] // MOE_ROWS).astype(jnp.int32)
    fill = jnp.arange(MOE_ROWS, dtype=jnp.int32)[None, :]
    pad_dest = jnp.where(fill < (padded - counts)[:, None], (start_padded + counts)[:, None] + fill,
                         n_rows - MOE_ROWS + fill).reshape(-1)
    xg = _scatter_rows(hf, jnp.concatenate([dest, pad_dest]), n_rows)
    y = _experts(xg, group_expert, n_valid, w1, w3, w2)
    return _combine(x1, _gather_rows(y, dest), gate.T)


def _row(v, pad=0):
    v = v.reshape(1, -1).astype(F32)
    return jnp.pad(v, ((0, 0), (0, pad))) if pad else v


def _rows_at(w, start, total):
    return jnp.pad(w.astype(F32), ((start, total - start - w.shape[0]), (0, 0)))


def kernel(x, w_in, norm_mix, rwkv_mu, rwkv_decay_up, rwkv_w0, rwkv_a_up, rwkv_a0, rwkv_gate_up, rwkv_k_k, rwkv_k_a, rwkv_r_k, rwkv_ln_w, rwkv_ln_b, vres_down, vres_up, vres_v0, ssm_conv_w, ssm_conv_b, ssm_dt_bias, ssm_a_log, ssm_d, ssm_norm_w, att_q_gain, att_k_gain, att_rel_bias, gla_gate_up, gla_gate_bias, gla_norm_w, w_branch, w_out, norm_ffn, ffn_w1, ffn_w3, ffn_w2, moe_router, moe_w1, moe_w3, moe_w2):
    bsz, seq, _ = x.shape
    n_tok = bsz * seq
    depth = w_in.shape[0]
    x2 = x.reshape(n_tok, D_MODEL)
    v_first = None
    for l in range(depth):
        n_cols = w_in.shape[2]
        w_t = jnp.swapaxes(w_in, 1, 2).reshape(depth * n_cols, D_MODEL)
        patches = jnp.zeros((2, PACK_ROWS, D_MODEL), F32)
        if l > 0:
            lo = RWKV_COLS % PACK_ROWS
            patches = patches.at[1, lo:lo + 32].set(vres_down[l - 1].T)
        w_mix = _pack_weight_rows(w_t, _mixer_plan(l * n_cols), patches, n_sub=MIXER_PACK_BLOCKS)
        o_gate = RWKV_COLS + SSM_COLS + ATT_COLS + GLA_COLS
        wg = _pack_weight_rows(w_t, [(l * n_cols + r, GATE_PACK_ROWS, 0) for r in range(o_gate, n_cols, GATE_PACK_ROWS)],
                               jnp.zeros((1, GATE_PACK_ROWS, D_MODEL), F32))
        moe = l % 2 == 1
        up_w = [(moe_w3, l // 2)] if moe else [(ffn_w1, l // 2), (ffn_w3, l // 2)]
        h, za, zb, zc, zd, wb_b, wo_b, *up_b = _inproj(x2, _row(norm_mix[l]), w_mix,
                                                       cast=[(w_branch, l), (w_out, l)] + up_w)
        shp = lambda z: z.reshape(bsz, seq, z.shape[-1])

        rw = dict(mu=_row(rwkv_mu[l], RWKV_PAD - RWKV_COLS),
                  dup=_rows_at(rwkv_decay_up[l], 0, LANES), w0=_row(rwkv_w0[l]),
                  aup=_rows_at(rwkv_a_up[l], 64, LANES), a0=_row(rwkv_a0[l]),
                  gup=_rows_at(rwkv_gate_up[l], 0, 256).astype(BF16),
                  kk=_row(rwkv_k_k[l]), ka=_row(rwkv_k_a[l]), rk=_row(rwkv_r_k[l]),
                  lnw=_row(rwkv_ln_w[l]), lnb=_row(rwkv_ln_b[l]))
        if l > 0:
            rw.update(vup=_rows_at(vres_up[l - 1], 160, 256), v0=_row(vres_v0[l - 1]))
        ssm = dict(cw=ssm_conv_w[l], cb=_row(ssm_conv_b[l]), dtb=_row(ssm_dt_bias[l], LANES - SSM_HEADS),
                   alog=_row(ssm_a_log[l], LANES - SSM_HEADS), dexp=_row(jnp.repeat(ssm_d[l], CHUNK)),
                   nw=_row(ssm_norm_w[l]))
        gla = dict(gup=_rows_at(gla_gate_up[l], 0, LANES), gb=_row(gla_gate_bias[l]), nw=_row(gla_norm_w[l]))
        (o_c,), (o_d,), (o_b,), res_a = _run_parts(
            [_attention_part(shp(zc), att_q_gain[l], att_k_gain[l], att_rel_bias, MIXER_BLOCK),
             _gla_part(shp(zd), gla, MIXER_BLOCK), _mamba_part(shp(zb), ssm, MIXER_BLOCK),
             _rwkv_part(shp(za), v_first, rw, MIXER_BLOCK)],
            grid=(bsz, seq // MIXER_BLOCK), name="mixers")
        o_a = res_a[0]
        if l == 0:
            v_first = res_a[1]
        outs = [o.reshape(n_tok, BRANCH_DIM) for o in (o_a, o_b, o_c, o_d)]
        merge_w = (wg, wb_b, wo_b, _row(norm_ffn[l]))
        if not moe:
            x1, hf, *w1_b = _merge(x2, h, outs, *merge_w, cast=[(moe_w1, l // 2)] if l + 1 < depth else [])
            x2 = _ffn(x1, hf, *up_b, ffn_w2[l // 2].astype(BF16))
        else:
            x1, *routing, w2_b = _merge(x2, h, outs, *merge_w, wr_t=moe_router[l // 2].T, cast=[(moe_w2, l // 2)])
            x2 = _moe(x1, routing, w1_b[0], up_b[0], w2_b)
    return x2.reshape(bsz, seq, D_MODEL)
```
